```python
import jax, jax.numpy as jnp
from jax import lax
import numpy as np

D_MODEL = 1024
BATCH = 8
SEQ = 2048
DEPTH = 4

HEAD_DIM = 64
CONV_WIDTH = 3 * D_MODEL // 8
POOL_WIDTH = D_MODEL // 4
SGU_WIDTH = D_MODEL - CONV_WIDTH - POOL_WIDTH
CONV_HEADS = CONV_WIDTH // HEAD_DIM
SGU_HEADS = SGU_WIDTH // HEAD_DIM
POOL_WINDOWS = (2, 4, 8, 16)
POOL_GROUPS = len(POOL_WINDOWS)
POOL_GROUP_DIM = POOL_WIDTH // POOL_GROUPS
CONV_K = 3
CHUNK = 128
MIX_WIDTH = CONV_WIDTH + POOL_WIDTH + SGU_WIDTH
IN_WIDTH = 3 * CONV_WIDTH + POOL_WIDTH + 2 * SGU_WIDTH
D_FF = -(-8 * D_MODEL // (3 * 256)) * 256
ALPHA = float((2 * DEPTH) ** 0.25)
BETA = float((8 * DEPTH) ** -0.25)
LN_EPS = 1e-5

kernel_name = "hybrid_conv_pool_sgu_deepnorm"


def _norm_stats(x, eps=LN_EPS):
    xf = x.astype(jnp.float32)
    mu = jnp.mean(xf, axis=-1, keepdims=True)
    var = jnp.mean(jnp.square(xf - mu), axis=-1, keepdims=True)
    return ((xf - mu) * lax.rsqrt(var + eps)).astype(x.dtype)


def layer_norm(x, g, b):
    return _norm_stats(x) * g + b


def short_gated_conv(xa, gb, gc, w_conv):
    z = gc * xa
    s = z.shape[1]
    zp = jnp.pad(z, ((0, 0), (CONV_K - 1, 0), (0, 0)))
    y = w_conv[0] * zp[:, 0:s] + w_conv[1] * zp[:, 1:s + 1] + w_conv[2] * zp[:, 2:s + 2]
    return gb * y


def multiscale_pool(p, w_pool, pool_scale):
    b, s, _ = p.shape
    pg = p.reshape(b, s, POOL_GROUPS, POOL_GROUP_DIM)
    cs = jnp.cumsum(pg.astype(jnp.float32), axis=1)
    t1 = jnp.arange(1, s + 1, dtype=jnp.float32)
    means = []
    for g, w in enumerate(POOL_WINDOWS):
        csg = cs[:, :, g]
        csp = jnp.pad(csg, ((0, 0), (w, 0), (0, 0)))
        win_sum = csp[:, w:] - csp[:, :s]
        count = jnp.minimum(t1, float(w))[None, :, None]
        means.append(win_sum / count)
    mean = jnp.stack(means, axis=2).astype(p.dtype)
    d = mean - pg
    y = jnp.einsum('bsgc,gcd->bsgd', d, w_pool)
    return y.reshape(b, s, POOL_WIDTH) * pool_scale


def chunked_sgu(uv, sgu_ln_g, w_spatial, b_spatial):
    b, s, _ = uv.shape
    uv = jax.nn.gelu(uv, approximate=False)
    u, v = uv[..., :SGU_WIDTH], uv[..., SGU_WIDTH:]
    vh = v.reshape(b, s, SGU_HEADS, HEAD_DIM)
    vh = _norm_stats(vh) * sgu_ln_g.reshape(SGU_HEADS, HEAD_DIM)
    vc = vh.reshape(b, s // CHUNK, CHUNK, SGU_HEADS, HEAD_DIM)
    mask = jnp.tril(jnp.ones((CHUNK, CHUNK), dtype=w_spatial.dtype))
    wm = w_spatial * mask
    mixed = jnp.einsum('hts,bnshd->bnthd', wm, vc) + b_spatial.T[None, None, :, :, None]
    return u * mixed.reshape(b, s, SGU_WIDTH)


def swiglu(h, w_gate_up, w_down):
    gu = h @ w_gate_up
    g, u = gu[..., :D_FF], gu[..., D_FF:]
    return (jax.nn.silu(g) * u) @ w_down


def _fwd_setup_inputs(seed: int = 0) -> dict:
    key = jax.random.key(seed)
    ks = jax.random.split(key, 16)
    f32 = jnp.float32
    nrm = lambda k, shape: jax.random.normal(k, shape, dtype=f32)
    x = nrm(ks[0], (BATCH, SEQ, D_MODEL))
    w_in = nrm(ks[1], (DEPTH, D_MODEL, IN_WIDTH)) * D_MODEL ** -0.5
    w_conv = nrm(ks[2], (DEPTH, CONV_K, CONV_WIDTH)) * CONV_K ** -0.5
    w_pool = nrm(ks[3], (DEPTH, POOL_GROUPS, POOL_GROUP_DIM, POOL_GROUP_DIM)) * POOL_GROUP_DIM ** -0.5
    pool_scale = 1.0 + 0.1 * nrm(ks[4], (DEPTH, POOL_WIDTH))
    sgu_ln_g = 1.0 + 0.1 * nrm(ks[5], (DEPTH, SGU_WIDTH))
    w_spatial = nrm(ks[6], (DEPTH, SGU_HEADS, CHUNK, CHUNK)) * CHUNK ** -0.5
    b_spatial = 1.0 + 0.1 * nrm(ks[7], (DEPTH, SGU_HEADS, CHUNK))
    w_o = nrm(ks[8], (DEPTH, MIX_WIDTH, D_MODEL)) * (MIX_WIDTH ** -0.5) * BETA
    ln1_g = 1.0 + 0.1 * nrm(ks[9], (DEPTH, D_MODEL))
    ln1_b = 0.02 * nrm(ks[10], (DEPTH, D_MODEL))
    w_gate_up = nrm(ks[11], (DEPTH, D_MODEL, 2 * D_FF)) * D_MODEL ** -0.5
    w_down = nrm(ks[12], (DEPTH, D_FF, D_MODEL)) * (D_FF ** -0.5) * BETA
    ln2_g = 1.0 + 0.1 * nrm(ks[13], (DEPTH, D_MODEL))
    ln2_b = 0.02 * nrm(ks[14], (DEPTH, D_MODEL))
    return {"x": x, "w_in": w_in, "w_conv": w_conv, "w_pool": w_pool,
            "pool_scale": pool_scale, "sgu_ln_g": sgu_ln_g, "w_spatial": w_spatial,
            "b_spatial": b_spatial, "w_o": w_o, "ln1_g": ln1_g, "ln1_b": ln1_b,
            "w_gate_up": w_gate_up, "w_down": w_down, "ln2_g": ln2_g, "ln2_b": ln2_b}


def _fwd_reference(x, w_in, w_conv, w_pool, pool_scale, sgu_ln_g, w_spatial, b_spatial,
              w_o, ln1_g, ln1_b, w_gate_up, w_down, ln2_g, ln2_b):
    c0 = CONV_WIDTH
    for l in range(DEPTH):
        proj = x @ w_in[l]
        xa = proj[..., 0:c0]
        gb = proj[..., c0:2 * c0]
        gc = proj[..., 2 * c0:3 * c0]
        p = proj[..., 3 * c0:3 * c0 + POOL_WIDTH]
        uv = proj[..., 3 * c0 + POOL_WIDTH:]
        ya = short_gated_conv(xa, gb, gc, w_conv[l])
        yb = multiscale_pool(p, w_pool[l], pool_scale[l])
        yc = chunked_sgu(uv, sgu_ln_g[l], w_spatial[l], b_spatial[l])
        mix = jnp.concatenate([ya, yb, yc], axis=-1) @ w_o[l]
        h = layer_norm(ALPHA * x + mix, ln1_g[l], ln1_b[l])
        x = layer_norm(ALPHA * h + swiglu(h, w_gate_up[l], w_down[l]), ln2_g[l], ln2_b[l])
    return x


import jax as _jax
import jax.numpy as _jnp

TWIN_FORMAT = 'train_step'
FWD_PARAMS = ['x', 'w_in', 'w_conv', 'w_pool', 'pool_scale', 'sgu_ln_g', 'w_spatial', 'b_spatial', 'w_o', 'ln1_g', 'ln1_b', 'w_gate_up', 'w_down', 'ln2_g', 'ln2_b']
TWIN_WEIGHTS = ['w_in', 'w_conv', 'w_pool', 'pool_scale', 'sgu_ln_g', 'w_spatial', 'b_spatial', 'w_o', 'ln1_g', 'ln1_b', 'w_gate_up', 'w_down', 'ln2_g', 'ln2_b']
TWIN_DIFF_INPUT = 'x'
TWIN_INPUTS = ['x', 'w_in', 'w_conv', 'w_pool', 'pool_scale', 'sgu_ln_g', 'w_spatial', 'b_spatial', 'w_o', 'ln1_g', 'ln1_b', 'w_gate_up', 'w_down', 'ln2_g', 'ln2_b', 'loss_target', 'm_w_in', 'm_w_conv', 'm_w_pool', 'm_pool_scale', 'm_sgu_ln_g', 'm_w_spatial', 'm_b_spatial', 'm_w_o', 'm_ln1_g', 'm_ln1_b', 'm_w_gate_up', 'm_w_down', 'm_ln2_g', 'm_ln2_b', 'v_w_in', 'v_w_conv', 'v_w_pool', 'v_pool_scale', 'v_sgu_ln_g', 'v_w_spatial', 'v_b_spatial', 'v_w_o', 'v_ln1_g', 'v_ln1_b', 'v_w_gate_up', 'v_w_down', 'v_ln2_g', 'v_ln2_b']
TWIN_OUTPUTS = ['loss', 'grad_x', 'grad_w_in', 'grad_w_conv', 'grad_w_pool', 'grad_pool_scale', 'grad_sgu_ln_g', 'grad_w_spatial', 'grad_b_spatial', 'grad_w_o', 'grad_ln1_g', 'grad_ln1_b', 'grad_w_gate_up', 'grad_w_down', 'grad_ln2_g', 'grad_ln2_b', 'delta_w_in', 'delta_w_conv', 'delta_w_pool', 'delta_pool_scale', 'delta_sgu_ln_g', 'delta_w_spatial', 'delta_b_spatial', 'delta_w_o', 'delta_ln1_g', 'delta_ln1_b', 'delta_w_gate_up', 'delta_w_down', 'delta_ln2_g', 'delta_ln2_b', 'new_m_w_in', 'new_m_w_conv', 'new_m_w_pool', 'new_m_pool_scale', 'new_m_sgu_ln_g', 'new_m_w_spatial', 'new_m_b_spatial', 'new_m_w_o', 'new_m_ln1_g', 'new_m_ln1_b', 'new_m_w_gate_up', 'new_m_w_down', 'new_m_ln2_g', 'new_m_ln2_b', 'new_v_w_in', 'new_v_w_conv', 'new_v_w_pool', 'new_v_pool_scale', 'new_v_sgu_ln_g', 'new_v_w_spatial', 'new_v_b_spatial', 'new_v_w_o', 'new_v_ln1_g', 'new_v_ln1_b', 'new_v_w_gate_up', 'new_v_w_down', 'new_v_ln2_g', 'new_v_ln2_b']
TWIN_LEAF_KINDS = {'loss': 'loss', 'grad_x': 'grad_x', 'grad_w_in': 'grad_w', 'grad_w_conv': 'grad_w', 'grad_w_pool': 'grad_w', 'grad_pool_scale': 'grad_w', 'grad_sgu_ln_g': 'grad_w', 'grad_w_spatial': 'grad_w', 'grad_b_spatial': 'grad_w', 'grad_w_o': 'grad_w', 'grad_ln1_g': 'grad_w', 'grad_ln1_b': 'grad_w', 'grad_w_gate_up': 'grad_w', 'grad_w_down': 'grad_w', 'grad_ln2_g': 'grad_w', 'grad_ln2_b': 'grad_w', 'delta_w_in': 'delta_w', 'delta_w_conv': 'delta_w', 'delta_w_pool': 'delta_w', 'delta_pool_scale': 'delta_w', 'delta_sgu_ln_g': 'delta_w', 'delta_w_spatial': 'delta_w', 'delta_b_spatial': 'delta_w', 'delta_w_o': 'delta_w', 'delta_ln1_g': 'delta_w', 'delta_ln1_b': 'delta_w', 'delta_w_gate_up': 'delta_w', 'delta_w_down': 'delta_w', 'delta_ln2_g': 'delta_w', 'delta_ln2_b': 'delta_w', 'new_m_w_in': 'new_m', 'new_m_w_conv': 'new_m', 'new_m_w_pool': 'new_m', 'new_m_pool_scale': 'new_m', 'new_m_sgu_ln_g': 'new_m', 'new_m_w_spatial': 'new_m', 'new_m_b_spatial': 'new_m', 'new_m_w_o': 'new_m', 'new_m_ln1_g': 'new_m', 'new_m_ln1_b': 'new_m', 'new_m_w_gate_up': 'new_m', 'new_m_w_down': 'new_m', 'new_m_ln2_g': 'new_m', 'new_m_ln2_b': 'new_m', 'new_v_w_in': 'new_v', 'new_v_w_conv': 'new_v', 'new_v_w_pool': 'new_v', 'new_v_pool_scale': 'new_v', 'new_v_sgu_ln_g': 'new_v', 'new_v_w_spatial': 'new_v', 'new_v_b_spatial': 'new_v', 'new_v_w_o': 'new_v', 'new_v_ln1_g': 'new_v', 'new_v_ln1_b': 'new_v', 'new_v_w_gate_up': 'new_v', 'new_v_w_down': 'new_v', 'new_v_ln2_g': 'new_v', 'new_v_ln2_b': 'new_v'}


def _forward(args):
    return _fwd_reference(*[args[k] for k in FWD_PARAMS])


def _output_shape():
    out = _jax.eval_shape(lambda: _forward(_fwd_setup_inputs(0)))
    return out.shape, out.dtype

N_MICROBATCH = 1
ADAM_LR = 0.001
ADAM_B1 = 0.9
ADAM_B2 = 0.999
ADAM_EPS = 1e-08
ADAM_WD = 0.01
ADAM_STEP = 10
PER_EXAMPLE_BATCH_AXIS = {'x': 0, 'loss_target': 0}
SHARED_INPUTS = []
_WEIGHT_DTYPES = {'w_in': _jnp.float32, 'w_conv': _jnp.float32, 'w_pool': _jnp.float32, 'pool_scale': _jnp.float32, 'sgu_ln_g': _jnp.float32, 'w_spatial': _jnp.float32, 'b_spatial': _jnp.float32, 'w_o': _jnp.float32, 'ln1_g': _jnp.float32, 'ln1_b': _jnp.float32, 'w_gate_up': _jnp.float32, 'w_down': _jnp.float32, 'ln2_g': _jnp.float32, 'ln2_b': _jnp.float32}
MOMENT_SCALE = {'w_in': 3.056325e-02, 'w_conv': 3.362231e-02, 'w_pool': 3.237590e-02, 'pool_scale': 3.235126e-02, 'sgu_ln_g': 1.586072e-02, 'w_spatial': 1.095721e-02, 'b_spatial': 1.552780e-02, 'w_o': 7.993774e-02, 'ln1_g': 3.038187e+00, 'ln1_b': 3.446093e-01, 'w_gate_up': 1.244130e-02, 'w_down': 4.871691e-02, 'ln2_g': 9.263713e+00, 'ln2_b': 7.566226e-01}


def _to_microbatches(a, axis):
    t = _jnp.moveaxis(a, axis, 0)
    t = t.reshape((N_MICROBATCH, t.shape[0] // N_MICROBATCH) + t.shape[1:])
    return _jnp.moveaxis(t, 1, axis + 1)


def setup_inputs(seed: int = 0) -> dict:
    inp = _fwd_setup_inputs(seed)
    key = _jax.random.fold_in(_jax.random.key(seed), 7919)
    shape, _ = _output_shape()
    out = dict(inp)
    out["loss_target"] = _jax.random.normal(_jax.random.fold_in(key, 0), shape, _jnp.float32)
    for i, name in enumerate(TWIN_WEIGHTS):
        w = inp[name].astype(_jnp.float32)
        if MOMENT_SCALE is None:
            s = _jnp.sqrt(_jnp.mean(_jnp.square(w)) + 1e-30)
        else:
            s = MOMENT_SCALE[name]
        km, kv = _jax.random.split(_jax.random.fold_in(key, i + 1))
        out[name] = w
        out["m_" + name] = s * _jax.random.normal(km, w.shape, _jnp.float32)
        out["v_" + name] = (s * s) * _jax.random.uniform(kv, w.shape, _jnp.float32, 0.5, 1.5)
    if N_MICROBATCH > 1:
        for name, axis in PER_EXAMPLE_BATCH_AXIS.items():
            out[name] = _to_microbatches(out[name], axis)
    return {'x': out['x'], 'w_in': out['w_in'], 'w_conv': out['w_conv'], 'w_pool': out['w_pool'], 'pool_scale': out['pool_scale'], 'sgu_ln_g': out['sgu_ln_g'], 'w_spatial': out['w_spatial'], 'b_spatial': out['b_spatial'], 'w_o': out['w_o'], 'ln1_g': out['ln1_g'], 'ln1_b': out['ln1_b'], 'w_gate_up': out['w_gate_up'], 'w_down': out['w_down'], 'ln2_g': out['ln2_g'], 'ln2_b': out['ln2_b'], 'loss_target': out['loss_target'], 'm_w_in': out['m_w_in'], 'm_w_conv': out['m_w_conv'], 'm_w_pool': out['m_w_pool'], 'm_pool_scale': out['m_pool_scale'], 'm_sgu_ln_g': out['m_sgu_ln_g'], 'm_w_spatial': out['m_w_spatial'], 'm_b_spatial': out['m_b_spatial'], 'm_w_o': out['m_w_o'], 'm_ln1_g': out['m_ln1_g'], 'm_ln1_b': out['m_ln1_b'], 'm_w_gate_up': out['m_w_gate_up'], 'm_w_down': out['m_w_down'], 'm_ln2_g': out['m_ln2_g'], 'm_ln2_b': out['m_ln2_b'], 'v_w_in': out['v_w_in'], 'v_w_conv': out['v_w_conv'], 'v_w_pool': out['v_w_pool'], 'v_pool_scale': out['v_pool_scale'], 'v_sgu_ln_g': out['v_sgu_ln_g'], 'v_w_spatial': out['v_w_spatial'], 'v_b_spatial': out['v_b_spatial'], 'v_w_o': out['v_w_o'], 'v_ln1_g': out['v_ln1_g'], 'v_ln1_b': out['v_ln1_b'], 'v_w_gate_up': out['v_w_gate_up'], 'v_w_down': out['v_w_down'], 'v_ln2_g': out['v_ln2_g'], 'v_ln2_b': out['v_ln2_b']}


def _loss(weights, diff, rest, loss_target):
    with _jax.named_scope("forward"):
        args = {**rest, TWIN_DIFF_INPUT: diff, **{k: w.astype(_WEIGHT_DTYPES[k]) for k, w in weights.items()}}
        y = _forward(args)
    with _jax.named_scope("loss_head"):
        err = _jnp.square(y.astype(_jnp.float32) - loss_target)
        return 0.5 * _jnp.sum(_jnp.mean(err, axis=-1)) if err.ndim else 0.5 * err


def _adamw(w, g, m, v):
    m = ADAM_B1 * m + (1.0 - ADAM_B1) * g
    v = ADAM_B2 * v + (1.0 - ADAM_B2) * _jnp.square(g)
    m_hat = m / (1.0 - ADAM_B1 ** ADAM_STEP)
    v_hat = v / (1.0 - ADAM_B2 ** ADAM_STEP)
    delta = -ADAM_LR * (m_hat / (_jnp.sqrt(v_hat) + ADAM_EPS) + ADAM_WD * w)
    return delta, m, v


def reference(x, w_in, w_conv, w_pool, pool_scale, sgu_ln_g, w_spatial, b_spatial, w_o, ln1_g, ln1_b, w_gate_up, w_down, ln2_g, ln2_b, loss_target, m_w_in, m_w_conv, m_w_pool, m_pool_scale, m_sgu_ln_g, m_w_spatial, m_b_spatial, m_w_o, m_ln1_g, m_ln1_b, m_w_gate_up, m_w_down, m_ln2_g, m_ln2_b, v_w_in, v_w_conv, v_w_pool, v_pool_scale, v_sgu_ln_g, v_w_spatial, v_b_spatial, v_w_o, v_ln1_g, v_ln1_b, v_w_gate_up, v_w_down, v_ln2_g, v_ln2_b):
    given = dict(x=x, w_in=w_in, w_conv=w_conv, w_pool=w_pool, pool_scale=pool_scale, sgu_ln_g=sgu_ln_g, w_spatial=w_spatial, b_spatial=b_spatial, w_o=w_o, ln1_g=ln1_g, ln1_b=ln1_b, w_gate_up=w_gate_up, w_down=w_down, ln2_g=ln2_g, ln2_b=ln2_b, loss_target=loss_target, m_w_in=m_w_in, m_w_conv=m_w_conv, m_w_pool=m_w_pool, m_pool_scale=m_pool_scale, m_sgu_ln_g=m_sgu_ln_g, m_w_spatial=m_w_spatial, m_b_spatial=m_b_spatial, m_w_o=m_w_o, m_ln1_g=m_ln1_g, m_ln1_b=m_ln1_b, m_w_gate_up=m_w_gate_up, m_w_down=m_w_down, m_ln2_g=m_ln2_g, m_ln2_b=m_ln2_b, v_w_in=v_w_in, v_w_conv=v_w_conv, v_w_pool=v_w_pool, v_pool_scale=v_pool_scale, v_sgu_ln_g=v_sgu_ln_g, v_w_spatial=v_w_spatial, v_b_spatial=v_b_spatial, v_w_o=v_w_o, v_ln1_g=v_ln1_g, v_ln1_b=v_ln1_b, v_w_gate_up=v_w_gate_up, v_w_down=v_w_down, v_ln2_g=v_ln2_g, v_ln2_b=v_ln2_b)
    weights = {n: given[n] for n in TWIN_WEIGHTS}
    shared = {n: given[n] for n in SHARED_INPUTS}
    per_example = {n: given[n] for n in ['x']}
    grad_fn = _jax.value_and_grad(_loss, argnums=(0, 1))

    def one_microbatch(ex, loss_target):
        ex = dict(ex)
        diff = ex.pop(TWIN_DIFF_INPUT)
        return grad_fn(weights, diff, {**shared, **ex}, loss_target)

    if N_MICROBATCH == 1:
        loss, (grad_w, grad_x) = one_microbatch(per_example, given["loss_target"])
    else:
        def body(carry, xs):
            loss_sum, grad_sum = carry
            l_k, (gw_k, gx_k) = one_microbatch(xs[0], xs[1])
            with _jax.named_scope("update"):
                return (loss_sum + l_k, _jax.tree.map(_jnp.add, grad_sum, gw_k)), gx_k

        init = (_jnp.zeros((), _jnp.float32), _jax.tree.map(_jnp.zeros_like, weights))
        (loss, grad_w), grad_x = _jax.lax.scan(body, init, (per_example, given["loss_target"]))
    with _jax.named_scope("update"):
        delta_w, new_m, new_v = {}, {}, {}
        for n in TWIN_WEIGHTS:
            delta_w[n], new_m[n], new_v[n] = _adamw(weights[n], grad_w[n], given["m_" + n], given["v_" + n])
    return (loss, grad_x, *[grad_w[n] for n in TWIN_WEIGHTS], *[delta_w[n] for n in TWIN_WEIGHTS],
            *[new_m[n] for n in TWIN_WEIGHTS], *[new_v[n] for n in TWIN_WEIGHTS])
```

```python
import functools

import jax
import jax.numpy as jnp
from jax import lax
from jax.experimental import pallas as pl
from jax.experimental.pallas import tpu as pltpu

f32 = jnp.float32
bf16 = jnp.bfloat16

D_MODEL = 1024
DEPTH = 4
CONV_W = 384
POOL_W = 256
SGU_W = 384
IN_W = 3 * CONV_W + POOL_W + 2 * SGU_W
D_FF = 2816
CHUNK = 128
HEAD = 64
POOL_WINDOWS = (2, 4, 8, 16)
ALPHA = float((2 * DEPTH) ** 0.25)
LN_EPS = 1e-5
ADAM_LR = 0.001
ADAM_B1 = 0.9
ADAM_B2 = 0.999
ADAM_EPS = 1e-08
ADAM_WD = 0.01
ADAM_STEP = 10

LANES = 128
TOKEN_TILE = 256
N_CHIPS = 4
VMEM_LIMIT = 56 * 1024 * 1024

BLK_XA, BLK_GB, BLK_GC, BLK_P, BLK_U, BLK_V = 0, 3, 6, 9, 11, 14

MESH = pl.DeviceIdType.MESH


def _params(sem=None):
    return pltpu.CompilerParams(dimension_semantics=sem, vmem_limit_bytes=VMEM_LIMIT)


def _rows(width, tile=TOKEN_TILE):
    return pl.BlockSpec((tile, width), lambda i: (i, 0))


def _resident(shape):
    zeros = (0,) * len(shape)
    return pl.BlockSpec(shape, lambda *_: zeros, pipeline_mode=pl.Buffered(1))


def _nt(a, b):
    return lax.dot_general(a, b, (((1,), (1,)), ((), ())), preferred_element_type=f32)


def _tn(a, b):
    return lax.dot_general(a, b, (((0,), (0,)), ((), ())), preferred_element_type=f32)


def _mm(a, b):
    return jnp.dot(a, b, preferred_element_type=f32)


def _norm_fwd(z):
    mu = jnp.mean(z, axis=-1, keepdims=True)
    zc = z - mu
    var = jnp.mean(zc * zc, axis=-1, keepdims=True)
    rstd = lax.rsqrt(var + LN_EPS)
    return zc * rstd, rstd


def _norm_bwd(dxhat, xhat, rstd):
    m1 = jnp.mean(dxhat, axis=-1, keepdims=True)
    m2 = jnp.mean(dxhat * xhat, axis=-1, keepdims=True)
    return rstd * (dxhat - m1 - xhat * m2)


def _proj(x, w_in_b):
    s = x.shape[0]

    def body(x_ref, w_ref, p_ref, xb_ref):
        xb = x_ref[...].astype(bf16)
        xb_ref[...] = xb
        p_ref[...] = _mm(xb, w_ref[...])

    return pl.pallas_call(
        body, grid=(s // TOKEN_TILE,),
        in_specs=[_rows(D_MODEL), _resident((D_MODEL, IN_W))],
        out_specs=[_rows(IN_W), _rows(D_MODEL)],
        out_shape=[jax.ShapeDtypeStruct((s, IN_W), f32), jax.ShapeDtypeStruct((s, D_MODEL), bf16)],
        name="proj", compiler_params=_params(("arbitrary",)))(x, w_in_b)


def _wo_ln1(mixcat, x, w_o_b, g, b):
    s = x.shape[0]

    def body(m_ref, x_ref, w_ref, g_ref, b_ref, xhat_ref, rstd_ref, hb_ref):
        z = ALPHA * x_ref[...] + _mm(m_ref[...], w_ref[...])
        xhat, rstd = _norm_fwd(z)
        xhat_ref[...] = xhat
        rstd_ref[...] = rstd
        hb_ref[...] = (xhat * g_ref[...] + b_ref[...]).astype(bf16)

    return pl.pallas_call(
        body, grid=(s // TOKEN_TILE,),
        in_specs=[_rows(D_MODEL), _rows(D_MODEL), _resident((D_MODEL, D_MODEL)), _resident((1, D_MODEL)), _resident((1, D_MODEL))],
        out_specs=[_rows(D_MODEL), _rows(1), _rows(D_MODEL)],
        out_shape=[jax.ShapeDtypeStruct((s, D_MODEL), f32), jax.ShapeDtypeStruct((s, 1), f32),
                   jax.ShapeDtypeStruct((s, D_MODEL), bf16)],
        name="wo_ln1", compiler_params=_params(("arbitrary",)))(mixcat, x, w_o_b, g, b)


def _mlp_fwd(xhat1, g1, b1, w_gu_b, w_down_b, g2, b2):
    s = xhat1.shape[0]

    def body(xh_ref, g1_ref, b1_ref, wgu_ref, wd_ref, g2_ref, b2_ref, gu_ref, xhat2_ref, rstd2_ref, y_ref):
        h = xh_ref[...] * g1_ref[...] + b1_ref[...]
        gu = _mm(h.astype(bf16), wgu_ref[...])
        gu_ref[...] = gu
        gate = gu[:, :D_FF]
        act = gate * jax.nn.sigmoid(gate) * gu[:, D_FF:]
        z = ALPHA * h + _mm(act.astype(bf16), wd_ref[...])
        xhat2, rstd2 = _norm_fwd(z)
        xhat2_ref[...] = xhat2
        rstd2_ref[...] = rstd2
        y_ref[...] = xhat2 * g2_ref[...] + b2_ref[...]

    vec = _resident((1, D_MODEL))
    return pl.pallas_call(
        body, grid=(s // TOKEN_TILE,),
        in_specs=[_rows(D_MODEL), vec, vec, _resident((D_MODEL, 2 * D_FF)), _resident((D_FF, D_MODEL)), vec, vec],
        out_specs=[_rows(2 * D_FF), _rows(D_MODEL), _rows(1), _rows(D_MODEL)],
        out_shape=[jax.ShapeDtypeStruct((s, 2 * D_FF), f32), jax.ShapeDtypeStruct((s, D_MODEL), f32),
                   jax.ShapeDtypeStruct((s, 1), f32), jax.ShapeDtypeStruct((s, D_MODEL), f32)],
        name="mlp_fwd", compiler_params=_params(("arbitrary",)))(xhat1, g1, b1, w_gu_b, w_down_b, g2, b2)


def _loss_head(y, target):
    s = y.shape[0]

    def body(y_ref, t_ref, dy_ref, sq_ref):
        @pl.when(pl.program_id(0) == 0)
        def _():
            sq_ref[...] = jnp.zeros_like(sq_ref)

        e = y_ref[...] - t_ref[...]
        dy_ref[...] = e * (1.0 / D_MODEL)
        sq_ref[...] += jnp.sum(e * e, axis=0, keepdims=True)

    return pl.pallas_call(
        body, grid=(s // TOKEN_TILE,),
        in_specs=[_rows(D_MODEL), _rows(D_MODEL)],
        out_specs=[_rows(D_MODEL), pl.BlockSpec((1, D_MODEL), lambda i: (0, 0))],
        out_shape=[jax.ShapeDtypeStruct((s, D_MODEL), f32), jax.ShapeDtypeStruct((1, D_MODEL), f32)],
        name="loss_head", compiler_params=_params(("arbitrary",)))(y, target)


def _mlp_bwd(dy, xhat2, rstd2, g2, gu, w_gu_b, w_down_b):
    s = dy.shape[0]

    def body(dy_ref, xh_ref, rs_ref, g2_ref, gu_ref, wgu_ref, wd_ref, dz_ref, act_ref, dgu_ref, dh_ref, gg_ref, gb_ref):
        @pl.when(pl.program_id(0) == 0)
        def _():
            gg_ref[...] = jnp.zeros_like(gg_ref)
            gb_ref[...] = jnp.zeros_like(gb_ref)

        dy_t = dy_ref[...]
        xhat = xh_ref[...]
        gg_ref[...] += jnp.sum(dy_t * xhat, axis=0, keepdims=True)
        gb_ref[...] += jnp.sum(dy_t, axis=0, keepdims=True)
        dz = _norm_bwd(dy_t * g2_ref[...], xhat, rs_ref[...])
        dzb = dz.astype(bf16)
        dz_ref[...] = dzb
        dact = _nt(dzb, wd_ref[...])
        gate = gu_ref[:, :D_FF]
        up = gu_ref[:, D_FF:]
        sg = jax.nn.sigmoid(gate)
        silu = gate * sg
        act_ref[...] = (silu * up).astype(bf16)
        dgu_ref[:, :D_FF] = (dact * up * (sg * (1.0 + gate * (1.0 - sg)))).astype(bf16)
        dgu_ref[:, D_FF:] = (dact * silu).astype(bf16)
        dh_ref[...] = ALPHA * dz + _nt(dgu_ref[...], wgu_ref[...])

    vec_out = pl.BlockSpec((1, D_MODEL), lambda i: (0, 0))
    return pl.pallas_call(
        body, grid=(s // TOKEN_TILE,),
        in_specs=[_rows(D_MODEL), _rows(D_MODEL), _rows(1), _resident((1, D_MODEL)), _rows(2 * D_FF),
                  _resident((D_MODEL, 2 * D_FF)), _resident((D_FF, D_MODEL))],
        out_specs=[_rows(D_MODEL), _rows(D_FF), _rows(2 * D_FF), _rows(D_MODEL), vec_out, vec_out],
        out_shape=[jax.ShapeDtypeStruct((s, D_MODEL), bf16), jax.ShapeDtypeStruct((s, D_FF), bf16),
                   jax.ShapeDtypeStruct((s, 2 * D_FF), bf16), jax.ShapeDtypeStruct((s, D_MODEL), f32),
                   jax.ShapeDtypeStruct((1, D_MODEL), f32), jax.ShapeDtypeStruct((1, D_MODEL), f32)],
        name="mlp_bwd", compiler_params=_params(("arbitrary",)))(dy, xhat2, rstd2, g2, gu, w_gu_b, w_down_b)


def _ln1_wo_bwd(dh, xhat1, rstd1, g1, w_o_b):
    s = dh.shape[0]

    def body(dh_ref, xh_ref, rs_ref, g1_ref, w_ref, dz_ref, dzb_ref, dm_ref, gg_ref, gb_ref):
        @pl.when(pl.program_id(0) == 0)
        def _():
            gg_ref[...] = jnp.zeros_like(gg_ref)
            gb_ref[...] = jnp.zeros_like(gb_ref)

        dh_t = dh_ref[...]
        xhat = xh_ref[...]
        gg_ref[...] += jnp.sum(dh_t * xhat, axis=0, keepdims=True)
        gb_ref[...] += jnp.sum(dh_t, axis=0, keepdims=True)
        dz = _norm_bwd(dh_t * g1_ref[...], xhat, rs_ref[...])
        dz_ref[...] = dz
        dzb = dz.astype(bf16)
        dzb_ref[...] = dzb
        dm_ref[...] = _nt(dzb, w_ref[...])

    vec_out = pl.BlockSpec((1, D_MODEL), lambda i: (0, 0))
    return pl.pallas_call(
        body, grid=(s // TOKEN_TILE,),
        in_specs=[_rows(D_MODEL), _rows(D_MODEL), _rows(1), _resident((1, D_MODEL)), _resident((D_MODEL, D_MODEL))],
        out_specs=[_rows(D_MODEL), _rows(D_MODEL), _rows(D_MODEL), vec_out, vec_out],
        out_shape=[jax.ShapeDtypeStruct((s, D_MODEL), f32), jax.ShapeDtypeStruct((s, D_MODEL), bf16),
                   jax.ShapeDtypeStruct((s, D_MODEL), f32), jax.ShapeDtypeStruct((1, D_MODEL), f32),
                   jax.ShapeDtypeStruct((1, D_MODEL), f32)],
        name="ln1_wo_bwd", compiler_params=_params(("arbitrary",)))(dh, xhat1, rstd1, g1, w_o_b)


def _dx(dz1, dproj, w_in_b):
    s = dz1.shape[0]

    def body(dz_ref, dp_ref, w_ref, dx_ref):
        dx_ref[...] = ALPHA * dz_ref[...] + _nt(dp_ref[...], w_ref[...])

    return pl.pallas_call(
        body, grid=(s // TOKEN_TILE,),
        in_specs=[_rows(D_MODEL), _rows(IN_W), _resident((D_MODEL, IN_W))],
        out_specs=_rows(D_MODEL),
        out_shape=jax.ShapeDtypeStruct((s, D_MODEL), f32),
        name="dx", compiler_params=_params(("arbitrary",)))(dz1, dproj, w_in_b)


def _weight_grad(a, b, bm, bn):
    s, m = a.shape
    n = b.shape[1]

    def body(a_ref, b_ref, o_ref):
        o_ref[...] = _tn(a_ref[...], b_ref[...]).astype(bf16)

    return pl.pallas_call(
        body, grid=(m // bm, n // bn),
        in_specs=[pl.BlockSpec((s, bm), lambda i, j: (0, i)), pl.BlockSpec((s, bn), lambda i, j: (0, j))],
        out_specs=pl.BlockSpec((bm, bn), lambda i, j: (i, j)),
        out_shape=jax.ShapeDtypeStruct((m, n), bf16),
        name="weight_grad", compiler_params=_params(("arbitrary", "arbitrary")))(a, b)


def _shift_down(a, k):
    row = lax.broadcasted_iota(jnp.int32, a.shape, 0)
    return jnp.where(row >= k, pltpu.roll(a, k, 0), 0.0)


def _shift_up(a, k):
    n = a.shape[0]
    row = lax.broadcasted_iota(jnp.int32, a.shape, 0)
    return jnp.where(row < n - k, pltpu.roll(a, n - k, 0), 0.0)


def _slab(s, block):
    return pl.BlockSpec((s, LANES), lambda k: (0, block + k))


def _conv_y(z, w):
    return w[0:1, :] * _shift_down(z, 2) + w[1:2, :] * _shift_down(z, 1) + w[2:3, :] * z


def _conv_fwd(proj, w_conv):
    s = proj.shape[0]

    def body(xa_ref, gb_ref, gc_ref, w_ref, o_ref):
        z = gc_ref[...] * xa_ref[...]
        o_ref[...] = (gb_ref[...] * _conv_y(z, w_ref[...])).astype(bf16)

    return pl.pallas_call(
        body, grid=(CONV_W // LANES,),
        in_specs=[_slab(s, BLK_XA), _slab(s, BLK_GB), _slab(s, BLK_GC), pl.BlockSpec((3, LANES), lambda k: (0, k))],
        out_specs=_slab(s, 0),
        out_shape=jax.ShapeDtypeStruct((s, CONV_W), bf16),
        name="conv_fwd", compiler_params=_params(("arbitrary",)))(proj, proj, proj, w_conv)


def _conv_bwd(proj, dmix, w_conv):
    s = proj.shape[0]

    def body(xa_ref, gb_ref, gc_ref, dy_ref, w_ref, dxa_ref, dgb_ref, dgc_ref, dw_ref):
        xa = xa_ref[...]
        gc = gc_ref[...]
        w = w_ref[...]
        z = gc * xa
        dya = dy_ref[...]
        dgb_ref[...] = (dya * _conv_y(z, w)).astype(bf16)
        dy = dya * gb_ref[...]
        dz = w[2:3, :] * dy + w[1:2, :] * _shift_up(dy, 1) + w[0:1, :] * _shift_up(dy, 2)
        dxa_ref[...] = (dz * gc).astype(bf16)
        dgc_ref[...] = (dz * xa).astype(bf16)
        dw_ref[0:1, :] = jnp.sum(dy * _shift_down(z, 2), axis=0, keepdims=True)
        dw_ref[1:2, :] = jnp.sum(dy * _shift_down(z, 1), axis=0, keepdims=True)
        dw_ref[2:3, :] = jnp.sum(dy * z, axis=0, keepdims=True)

    out = jax.ShapeDtypeStruct((s, CONV_W), bf16)
    return pl.pallas_call(
        body, grid=(CONV_W // LANES,),
        in_specs=[_slab(s, BLK_XA), _slab(s, BLK_GB), _slab(s, BLK_GC), _slab(s, 0), pl.BlockSpec((3, LANES), lambda k: (0, k))],
        out_specs=[_slab(s, 0), _slab(s, 0), _slab(s, 0), pl.BlockSpec((3, LANES), lambda k: (0, k))],
        out_shape=[out, out, out, jax.ShapeDtypeStruct((3, CONV_W), f32)],
        name="conv_bwd", compiler_params=_params(("arbitrary",)))(proj, proj, proj, dmix, w_conv)


def _pool_window(k):
    lane = lax.broadcasted_iota(jnp.int32, (1, LANES), 1)
    low = lane < HEAD
    first = k == 0
    wlen = jnp.where(low, jnp.where(first, POOL_WINDOWS[0], POOL_WINDOWS[2]), jnp.where(first, POOL_WINDOWS[1], POOL_WINDOWS[3]))
    return wlen, low, first


def _pool_diff(p, k):
    wlen, low, first = _pool_window(k)
    s2 = p + _shift_down(p, 1)
    s4 = s2 + _shift_down(s2, 2)
    s8 = s4 + _shift_down(s4, 4)
    s16 = s8 + _shift_down(s8, 8)
    win = jnp.where(low, jnp.where(first, s2, s8), jnp.where(first, s4, s16))
    row = lax.broadcasted_iota(jnp.int32, p.shape, 0)
    count = jnp.minimum(row + 1, wlen).astype(f32)
    return win / count - p, count


def _pool_weight(w_ref):
    zero = jnp.zeros((HEAD, HEAD), f32)
    top = jnp.concatenate([w_ref[0], zero], axis=1)
    bottom = jnp.concatenate([zero, w_ref[1]], axis=1)
    return jnp.concatenate([top, bottom], axis=0).astype(bf16)


def _pool_fwd(proj, w_pool, pool_scale):
    s = proj.shape[0]

    def body(p_ref, w_ref, sc_ref, o_ref):
        d, _ = _pool_diff(p_ref[...], pl.program_id(0))
        o_ref[...] = (_mm(d.astype(bf16), _pool_weight(w_ref)) * sc_ref[...]).astype(bf16)

    return pl.pallas_call(
        body, grid=(POOL_W // LANES,),
        in_specs=[_slab(s, BLK_P), pl.BlockSpec((2, HEAD, HEAD), lambda k: (k, 0, 0)), pl.BlockSpec((1, LANES), lambda k: (0, k))],
        out_specs=_slab(s, 0),
        out_shape=jax.ShapeDtypeStruct((s, POOL_W), bf16),
        name="pool_fwd", compiler_params=_params(("arbitrary",)))(proj, w_pool, pool_scale)


def _pool_bwd(proj, dmix, w_pool, pool_scale):
    s = proj.shape[0]

    def body(p_ref, dy_ref, w_ref, sc_ref, dp_ref, dw_ref, dsc_ref):
        k = pl.program_id(0)
        d, count = _pool_diff(p_ref[...], k)
        wbd = _pool_weight(w_ref)
        db = d.astype(bf16)
        dyb = dy_ref[...]
        dsc_ref[...] = jnp.sum(dyb * _mm(db, wbd), axis=0, keepdims=True)
        dpre = (dyb * sc_ref[...]).astype(bf16)
        dwbd = _tn(db, dpre)
        dw_ref[0] = dwbd[:HEAD, :HEAD]
        dw_ref[1] = dwbd[HEAD:, HEAD:]
        dd = _nt(dpre, wbd)
        e = dd / count
        wlen, low, first = _pool_window(k)
        a2 = e + _shift_up(e, 1)
        a4 = a2 + _shift_up(a2, 2)
        a8 = a4 + _shift_up(a4, 4)
        a16 = a8 + _shift_up(a8, 8)
        back = jnp.where(low, jnp.where(first, a2, a8), jnp.where(first, a4, a16))
        dp_ref[...] = (back - dd).astype(bf16)

    return pl.pallas_call(
        body, grid=(POOL_W // LANES,),
        in_specs=[_slab(s, BLK_P), _slab(s, CONV_W // LANES), pl.BlockSpec((2, HEAD, HEAD), lambda k: (k, 0, 0)),
                  pl.BlockSpec((1, LANES), lambda k: (0, k))],
        out_specs=[_slab(s, 0), pl.BlockSpec((2, HEAD, HEAD), lambda k: (k, 0, 0)), pl.BlockSpec((1, LANES), lambda k: (0, k))],
        out_shape=[jax.ShapeDtypeStruct((s, POOL_W), bf16), jax.ShapeDtypeStruct((4, HEAD, HEAD), f32),
                   jax.ShapeDtypeStruct((1, POOL_W), f32)],
        name="pool_bwd", compiler_params=_params(("arbitrary",)))(proj, dmix, w_pool, pool_scale)


INV_SQRT2 = 0.7071067811865476
INV_SQRT_2PI = 0.3989422804014327


def _gelu(x):
    return 0.5 * x * (1.0 + lax.erf(x * INV_SQRT2))


def _gelu_grad(x):
    return 0.5 * (1.0 + lax.erf(x * INV_SQRT2)) + x * (INV_SQRT_2PI * jnp.exp(-0.5 * x * x))


def _head_mean(a, low):
    s_low = jnp.sum(jnp.where(low, a, 0.0), axis=-1, keepdims=True)
    s_high = jnp.sum(jnp.where(low, 0.0, a), axis=-1, keepdims=True)
    return jnp.where(low, s_low, s_high) * (1.0 / HEAD)


def _tril():
    r = lax.broadcasted_iota(jnp.int32, (CHUNK, CHUNK), 0)
    c = lax.broadcasted_iota(jnp.int32, (CHUNK, CHUNK), 1)
    return r >= c


def _sgu_chunk(up, vp, g, wm0, wm1, b0, b1, low):
    ug = _gelu(up)
    vg = _gelu(vp)
    vc = vg - _head_mean(vg, low)
    rstd = lax.rsqrt(_head_mean(vc * vc, low) + LN_EPS)
    vn = vc * rstd
    vb = (vn * g).astype(bf16)
    mixed = jnp.where(low, _mm(wm0, vb) + b0, _mm(wm1, vb) + b1)
    return ug, vn, rstd, vb, mixed


def _sgu_specs(s):
    return [_slab(s, BLK_U), _slab(s, BLK_V), pl.BlockSpec((1, LANES), lambda k: (0, k)),
            pl.BlockSpec((2, CHUNK, CHUNK), lambda k: (k, 0, 0)), pl.BlockSpec((2, CHUNK, 1), lambda k: (k, 0, 0))]


def _sgu_fwd(proj, sgu_g, w_spatial, b_spatial3):
    s = proj.shape[0]

    def body(u_ref, v_ref, g_ref, w_ref, b_ref, o_ref):
        low = lax.broadcasted_iota(jnp.int32, (1, LANES), 1) < HEAD
        mask = _tril()
        wm0 = jnp.where(mask, w_ref[0], 0.0).astype(bf16)
        wm1 = jnp.where(mask, w_ref[1], 0.0).astype(bf16)
        g = g_ref[...]
        b0 = b_ref[0]
        b1 = b_ref[1]

        def chunk(n, carry):
            rows = pl.ds(pl.multiple_of(n * CHUNK, CHUNK), CHUNK)
            ug, _, _, _, mixed = _sgu_chunk(u_ref[rows, :], v_ref[rows, :], g, wm0, wm1, b0, b1, low)
            o_ref[rows, :] = (ug * mixed).astype(bf16)
            return carry

        lax.fori_loop(0, s // CHUNK, chunk, 0)

    return pl.pallas_call(
        body, grid=(SGU_W // LANES,),
        in_specs=_sgu_specs(s),
        out_specs=_slab(s, 0),
        out_shape=jax.ShapeDtypeStruct((s, SGU_W), bf16),
        name="sgu_fwd", compiler_params=_params(("arbitrary",)))(proj, proj, sgu_g, w_spatial, b_spatial3)


def _sgu_bwd(proj, dmix, sgu_g, w_spatial, b_spatial3):
    s = proj.shape[0]

    def body(u_ref, v_ref, g_ref, w_ref, b_ref, dy_ref, du_ref, dv_ref, dg_ref, dw_ref, db_ref):
        low = lax.broadcasted_iota(jnp.int32, (1, LANES), 1) < HEAD
        mask = _tril()
        w0 = jnp.where(mask, w_ref[0], 0.0)
        w1 = jnp.where(mask, w_ref[1], 0.0)
        wm0 = w0.astype(bf16)
        wm1 = w1.astype(bf16)
        wt0 = w0.T.astype(bf16)
        wt1 = w1.T.astype(bf16)
        g = g_ref[...]
        b0 = b_ref[0]
        b1 = b_ref[1]
        dg_ref[...] = jnp.zeros_like(dg_ref)
        dw_ref[...] = jnp.zeros_like(dw_ref)
        db_ref[...] = jnp.zeros_like(db_ref)

        def chunk(n, carry):
            rows = pl.ds(pl.multiple_of(n * CHUNK, CHUNK), CHUNK)
            up = u_ref[rows, :]
            vp = v_ref[rows, :]
            ug, vn, rstd, vb, mixed = _sgu_chunk(up, vp, g, wm0, wm1, b0, b1, low)
            dy = dy_ref[rows, :]
            du_ref[rows, :] = (dy * mixed * _gelu_grad(up)).astype(bf16)
            dmix_c = dy * ug
            db_ref[0] += jnp.sum(jnp.where(low, dmix_c, 0.0), axis=-1, keepdims=True)
            db_ref[1] += jnp.sum(jnp.where(low, 0.0, dmix_c), axis=-1, keepdims=True)
            dmb = dmix_c.astype(bf16)
            zero = jnp.zeros_like(dmb)
            dw_ref[0] += _nt(jnp.where(low, dmb, zero), vb)
            dw_ref[1] += _nt(jnp.where(low, zero, dmb), vb)
            dvnorm = jnp.where(low, _mm(wt0, dmb), _mm(wt1, dmb))
            dg_ref[...] += jnp.sum(dvnorm * vn, axis=0, keepdims=True)
            dvn = dvnorm * g
            dvg = rstd * (dvn - _head_mean(dvn, low) - vn * _head_mean(dvn * vn, low))
            dv_ref[rows, :] = (dvg * _gelu_grad(vp)).astype(bf16)
            return carry

        lax.fori_loop(0, s // CHUNK, chunk, 0)
        dw_ref[0] = jnp.where(mask, dw_ref[0], 0.0)
        dw_ref[1] = jnp.where(mask, dw_ref[1], 0.0)

    out = jax.ShapeDtypeStruct((s, SGU_W), bf16)
    return pl.pallas_call(
        body, grid=(SGU_W // LANES,),
        in_specs=_sgu_specs(s) + [_slab(s, (CONV_W + POOL_W) // LANES)],
        out_specs=[_slab(s, 0), _slab(s, 0), pl.BlockSpec((1, LANES), lambda k: (0, k)),
                   pl.BlockSpec((2, CHUNK, CHUNK), lambda k: (k, 0, 0)), pl.BlockSpec((2, CHUNK, 1), lambda k: (k, 0, 0))],
        out_shape=[out, out, jax.ShapeDtypeStruct((1, SGU_W), f32), jax.ShapeDtypeStruct((6, CHUNK, CHUNK), f32),
                   jax.ShapeDtypeStruct((6, CHUNK, 1), f32)],
        name="sgu_bwd", compiler_params=_params(("arbitrary",)))(proj, proj, sgu_g, w_spatial, b_spatial3, dmix)


def _layer_fwd(x, w):
    proj, xb = _proj(x, w["w_in"])
    ya = _conv_fwd(proj, w["w_conv"])
    yb = _pool_fwd(proj, w["w_pool"], w["pool_scale"])
    yc = _sgu_fwd(proj, w["sgu_ln_g"], w["w_spatial"], w["b_spatial"])
    mixcat = jnp.concatenate([ya, yb, yc], axis=1)
    xhat1, rstd1, hb = _wo_ln1(mixcat, x, w["w_o"], w["ln1_g"], w["ln1_b"])
    gu, xhat2, rstd2, y = _mlp_fwd(xhat1, w["ln1_g"], w["ln1_b"], w["w_gate_up"], w["w_down"], w["ln2_g"], w["ln2_b"])
    saved = dict(proj=proj, xb=xb, mixcat=mixcat, xhat1=xhat1, rstd1=rstd1, hb=hb, gu=gu, xhat2=xhat2, rstd2=rstd2)
    return y, saved


def _layer_bwd(dy, w, sv):
    dz2b, actb, dgub, dh, g_ln2_g, g_ln2_b = _mlp_bwd(dy, sv["xhat2"], sv["rstd2"], w["ln2_g"], sv["gu"], w["w_gate_up"], w["w_down"])
    dz1, dz1b, dmix, g_ln1_g, g_ln1_b = _ln1_wo_bwd(dh, sv["xhat1"], sv["rstd1"], w["ln1_g"], w["w_o"])
    dxa, dgb, dgc, g_conv = _conv_bwd(sv["proj"], dmix, w["w_conv"])
    dp, g_pool, g_pscale = _pool_bwd(sv["proj"], dmix, w["w_pool"], w["pool_scale"])
    du, dv, g_sgu_g, g_spatial, g_bsp = _sgu_bwd(sv["proj"], dmix, w["sgu_ln_g"], w["w_spatial"], w["b_spatial"])
    dproj = jnp.concatenate([dxa, dgb, dgc, dp, du, dv], axis=1)
    dx = _dx(dz1, dproj, w["w_in"])
    grads = dict(
        w_in=_weight_grad(sv["xb"], dproj, 512, IN_W),
        w_o=_weight_grad(sv["mixcat"], dz1b, 512, D_MODEL),
        w_gate_up=_weight_grad(sv["hb"], dgub, 512, D_FF // 2),
        w_down=_weight_grad(actb, dz2b, D_FF // 2, D_MODEL),
        w_conv=g_conv, w_pool=g_pool, pool_scale=g_pscale, sgu_ln_g=g_sgu_g, w_spatial=g_spatial,
        b_spatial=g_bsp.reshape(6, CHUNK), ln1_g=g_ln1_g, ln1_b=g_ln1_b, ln2_g=g_ln2_g, ln2_b=g_ln2_b)
    return dx, grads


def _local_step(x, target, layers):
    saved = []
    for w in layers:
        x, sv = _layer_fwd(x, w)
        saved.append(sv)
    dy, sq = _loss_head(x, target)
    grads = [None] * len(layers)
    for l in reversed(range(len(layers))):
        dy, grads[l] = _layer_bwd(dy, layers[l], saved[l])
    return sq, dy, grads


ANY = pl.BlockSpec(memory_space=pl.ANY)


def _place():
    x, y, c = lax.axis_index("x"), lax.axis_index("y"), lax.axis_index("c")
    others = [(1 - x, y), (x, 1 - y), (1 - x, 1 - y)]
    return x, y, c, others


def _chip_index(cx, cy):
    return 2 * cx + cy


def _half(ref_rows, c):
    half = ref_rows // 2
    return pl.ds(pl.multiple_of(c * half, 8), half)


def _remote(src, dst, send_sem, recv_sem, device):
    return pltpu.make_async_remote_copy(src_ref=src, dst_ref=dst, send_sem=send_sem, recv_sem=recv_sem,
                                        device_id=device, device_id_type=MESH)


def _gather_shards(shards):
    n = len(shards)

    def body(*refs):
        ins, outs = refs[:n], refs[n:2 * n]
        send, recv, local = refs[2 * n:]
        x, y, c, others = _place()
        me = _chip_index(x, y)
        copies = []
        for f in range(n):
            rows = ins[f].shape[1]
            mine = pltpu.make_async_copy(ins[f], outs[f].at[:, me], local.at[f])
            mine.start()
            copies.append(mine)
        sends = []
        for f in range(n):
            rows = ins[f].shape[1]
            for k, (cx, cy) in enumerate(others):
                cp = _remote(ins[f].at[:, _half(rows, c)], outs[f].at[:, me, _half(rows, c)],
                             send.at[f * 6 + k], recv.at[f * 6 + k], (cx, cy, c))
                cp.start()
                sends.append(cp)
        for f in range(n):
            rows = ins[f].shape[1]
            for k, (cx, cy) in enumerate(others):
                landed = outs[f].at[:, _chip_index(cx, cy), _half(rows, c)]
                _remote(landed, landed, send.at[f * 6 + k], recv.at[f * 6 + k], (cx, cy, c)).wait_recv()
                cp = _remote(landed, landed, send.at[f * 6 + 3 + k], recv.at[f * 6 + 3 + k], (x, y, 1 - c))
                cp.start()
                sends.append(cp)
        for f in range(n):
            rows = ins[f].shape[1]
            for k, (cx, cy) in enumerate(others):
                passed = outs[f].at[:, _chip_index(cx, cy), _half(rows, 1 - c)]
                _remote(passed, passed, send.at[f * 6 + 3 + k], recv.at[f * 6 + 3 + k], (x, y, 1 - c)).wait_recv()
        for cp in sends:
            cp.wait_send()
        for cp in copies:
            cp.wait()

    return pl.pallas_call(
        body, in_specs=[ANY] * n, out_specs=[ANY] * n,
        out_shape=[jax.ShapeDtypeStruct((s.shape[0], N_CHIPS) + s.shape[1:], s.dtype) for s in shards],
        scratch_shapes=[pltpu.SemaphoreType.DMA((6 * n,)), pltpu.SemaphoreType.DMA((6 * n,)), pltpu.SemaphoreType.DMA((n,))],
        name="gather_shards")(*shards)


def _sibling_split(parts):
    n = len(parts)

    def body(*refs):
        ins, own, got = refs[:n], refs[n:2 * n], refs[2 * n:3 * n]
        send, recv, local = refs[3 * n:]
        x, y, c, _ = _place()
        copies = []
        for f in range(n):
            rows = ins[f].shape[2]
            keep = pltpu.make_async_copy(ins[f].at[:, :, _half(rows, c)], own[f], local.at[f])
            keep.start()
            cp = _remote(ins[f].at[:, :, _half(rows, 1 - c)], got[f], send.at[f], recv.at[f], (x, y, 1 - c))
            cp.start()
            copies.append((keep, cp))
        for keep, cp in copies:
            cp.wait()
            keep.wait()

    half = [jax.ShapeDtypeStruct(p.shape[:2] + (p.shape[2] // 2, p.shape[3]), p.dtype) for p in parts]
    outs = pl.pallas_call(
        body, in_specs=[ANY] * n, out_specs=[ANY] * (2 * n), out_shape=half + half,
        scratch_shapes=[pltpu.SemaphoreType.DMA((n,)), pltpu.SemaphoreType.DMA((n,)), pltpu.SemaphoreType.DMA((n,))],
        name="sibling_split")(*parts)
    return outs[:n], outs[n:]


def _chip_scatter(sums):
    n = len(sums)

    def body(*refs):
        ins, outs = refs[:n], refs[n:2 * n]
        send, recv, local = refs[2 * n:]
        x, y, c, others = _place()
        copies = []
        for f in range(n):
            keep = pltpu.make_async_copy(ins[f].at[:, _chip_index(x, y)], outs[f].at[3], local.at[f])
            keep.start()
            copies.append(keep)
            for k, (cx, cy) in enumerate(others):
                cp = _remote(ins[f].at[:, _chip_index(cx, cy)], outs[f].at[k], send.at[f * 3 + k], recv.at[f * 3 + k], (cx, cy, c))
                cp.start()
                copies.append(cp)
        for cp in copies:
            cp.wait()

    return pl.pallas_call(
        body, in_specs=[ANY] * n, out_specs=[ANY] * n,
        out_shape=[jax.ShapeDtypeStruct((N_CHIPS, s.shape[0]) + s.shape[2:], s.dtype) for s in sums],
        scratch_shapes=[pltpu.SemaphoreType.DMA((3 * n,)), pltpu.SemaphoreType.DMA((3 * n,)), pltpu.SemaphoreType.DMA((n,))],
        name="chip_scatter")(*sums)


def _sibling_join(halves):
    n = len(halves)

    def body(*refs):
        ins, outs = refs[:n], refs[n:2 * n]
        send, recv, local = refs[2 * n:]
        x, y, c, _ = _place()
        copies = []
        for f in range(n):
            rows = outs[f].shape[1]
            keep = pltpu.make_async_copy(ins[f], outs[f].at[:, _half(rows, c)], local.at[f])
            keep.start()
            cp = _remote(ins[f], outs[f].at[:, _half(rows, c)], send.at[f], recv.at[f], (x, y, 1 - c))
            cp.start()
            copies.append((keep, cp))
        for f, (keep, cp) in enumerate(copies):
            rows = outs[f].shape[1]
            cp.wait_send()
            arrived = outs[f].at[:, _half(rows, 1 - c)]
            _remote(arrived, arrived, send.at[f], recv.at[f], (x, y, 1 - c)).wait_recv()
            keep.wait()

    return pl.pallas_call(
        body, in_specs=[ANY] * n, out_specs=[ANY] * n,
        out_shape=[jax.ShapeDtypeStruct((h.shape[0], 2 * h.shape[1], h.shape[2]), h.dtype) for h in halves],
        scratch_shapes=[pltpu.SemaphoreType.DMA((n,)), pltpu.SemaphoreType.DMA((n,)), pltpu.SemaphoreType.DMA((n,))],
        name="sibling_join")(*halves)


def _row_tile(rows):
    for tile in (TOKEN_TILE, 128, 64, 32, 16):
        if rows % tile == 0:
            return tile
    return rows


def _add_pair(a, b):
    shape = a.shape
    a2, b2 = a.reshape(-1, shape[-1]), b.reshape(-1, shape[-1])
    tile = _row_tile(a2.shape[0])

    def body(a_ref, b_ref, o_ref):
        o_ref[...] = (a_ref[...].astype(f32) + b_ref[...].astype(f32)).astype(o_ref.dtype)

    out = pl.pallas_call(
        body, grid=(a2.shape[0] // tile,),
        in_specs=[_rows(shape[-1], tile)] * 2, out_specs=_rows(shape[-1], tile),
        out_shape=jax.ShapeDtypeStruct(a2.shape, a.dtype),
        name="add_pair", compiler_params=_params(("arbitrary",)))(a2, b2)
    return out.reshape(shape)


def _add_slots(slots):
    shape = slots.shape
    s2 = slots.reshape(shape[0], -1, shape[-1])
    tile = _row_tile(s2.shape[1])

    def body(s_ref, o_ref):
        acc = s_ref[3].astype(f32)
        for k in range(3):
            acc = acc + s_ref[k].astype(f32)
        o_ref[...] = acc

    out = pl.pallas_call(
        body, grid=(s2.shape[1] // tile,),
        in_specs=[pl.BlockSpec((4, tile, shape[-1]), lambda i: (0, i, 0))], out_specs=_rows(shape[-1], tile),
        out_shape=jax.ShapeDtypeStruct(s2.shape[1:], f32),
        name="add_slots", compiler_params=_params(("arbitrary",)))(s2)
    return out.reshape(shape[1:])


def _adamw(w, g, m, v):
    shape = w.shape
    flat = [a.reshape(-1, shape[-1]) for a in (w, g, m, v)]
    tile = _row_tile(flat[0].shape[0])

    def body(w_ref, g_ref, m_ref, v_ref, d_ref, nm_ref, nv_ref):
        grad = g_ref[...]
        nm = ADAM_B1 * m_ref[...] + (1.0 - ADAM_B1) * grad
        nv = ADAM_B2 * v_ref[...] + (1.0 - ADAM_B2) * (grad * grad)
        m_hat = nm / (1.0 - ADAM_B1 ** ADAM_STEP)
        v_hat = nv / (1.0 - ADAM_B2 ** ADAM_STEP)
        d_ref[...] = -ADAM_LR * (m_hat / (jnp.sqrt(v_hat) + ADAM_EPS) + ADAM_WD * w_ref[...])
        nm_ref[...] = nm
        nv_ref[...] = nv

    spec = _rows(shape[-1], tile)
    out = jax.ShapeDtypeStruct(flat[0].shape, f32)
    res = pl.pallas_call(
        body, grid=(flat[0].shape[0] // tile,),
        in_specs=[spec] * 4, out_specs=[spec] * 3, out_shape=[out] * 3,
        name="adamw", compiler_params=_params(("arbitrary",)))(*flat)
    return [r.reshape(shape) for r in res]


def _reduce_to_owners(parts):
    own, got = _sibling_split(parts)
    chip_sums = [_add_pair(a, b) for a, b in zip(own, got)]
    slots = _chip_scatter(chip_sums)
    halves = [_add_slots(s) for s in slots]
    return _sibling_join(halves)


SMALL = ("w_conv", "w_pool", "pool_scale", "sgu_ln_g", "w_spatial", "b_spatial", "ln1_g", "ln1_b", "ln2_g", "ln2_b")
WEIGHTS = ("w_in", "w_conv", "w_pool", "pool_scale", "sgu_ln_g", "w_spatial", "b_spatial", "w_o", "ln1_g", "ln1_b",
           "w_gate_up", "w_down", "ln2_g", "ln2_b")
BIG = ("w_in", "w_o", "w_gate_up", "w_down")
COLUMN_SHARDED = ("w_in", "w_gate_up")
SMALL_ROWS = 4096


def _pack_small(arrays):
    flat = jnp.concatenate([a.reshape(-1) for a in arrays])
    return jnp.pad(flat, (0, SMALL_ROWS * LANES - flat.shape[0])).reshape(SMALL_ROWS, LANES)


def _unpack_small(packed, shapes):
    flat = packed.reshape(-1)
    out, at = [], 0
    for shp in shapes:
        size = 1
        for d in shp:
            size *= d
        out.append(flat[at:at + size].reshape(shp))
        at += size
    return out


def _to_blocks(full, name):
    depth, rows, cols = full.shape
    if name in COLUMN_SHARDED:
        return full.reshape(depth, rows, N_CHIPS, cols // N_CHIPS).transpose(0, 2, 1, 3)
    return full.reshape(depth, N_CHIPS, rows // N_CHIPS, cols)


def _from_blocks(blocks, name):
    depth, _, rows, cols = blocks.shape
    if name in COLUMN_SHARDED:
        return blocks.transpose(0, 2, 1, 3).reshape(depth, rows, N_CHIPS * cols)
    return blocks.reshape(depth, N_CHIPS * rows, cols)


def kernel(x, w_in, w_conv, w_pool, pool_scale, sgu_ln_g, w_spatial, b_spatial, w_o, ln1_g, ln1_b, w_gate_up, w_down, ln2_g, ln2_b, loss_target, m_w_in, m_w_conv, m_w_pool, m_pool_scale, m_sgu_ln_g, m_w_spatial, m_b_spatial, m_w_o, m_ln1_g, m_ln1_b, m_w_gate_up, m_w_down, m_ln2_g, m_ln2_b, v_w_in, v_w_conv, v_w_pool, v_pool_scale, v_sgu_ln_g, v_w_spatial, v_b_spatial, v_w_o, v_ln1_g, v_ln1_b, v_w_gate_up, v_w_down, v_ln2_g, v_ln2_b):
    weights = dict(w_in=w_in, w_conv=w_conv, w_pool=w_pool, pool_scale=pool_scale, sgu_ln_g=sgu_ln_g, w_spatial=w_spatial,
                   b_spatial=b_spatial, w_o=w_o, ln1_g=ln1_g, ln1_b=ln1_b, w_gate_up=w_gate_up, w_down=w_down, ln2_g=ln2_g, ln2_b=ln2_b)
    m_in = dict(w_in=m_w_in, w_conv=m_w_conv, w_pool=m_w_pool, pool_scale=m_pool_scale, sgu_ln_g=m_sgu_ln_g, w_spatial=m_w_spatial,
                b_spatial=m_b_spatial, w_o=m_w_o, ln1_g=m_ln1_g, ln1_b=m_ln1_b, w_gate_up=m_w_gate_up, w_down=m_w_down,
                ln2_g=m_ln2_g, ln2_b=m_ln2_b)
    v_in = dict(w_in=v_w_in, w_conv=v_w_conv, w_pool=v_w_pool, pool_scale=v_pool_scale, sgu_ln_g=v_sgu_ln_g, w_spatial=v_w_spatial,
                b_spatial=v_b_spatial, w_o=v_w_o, ln1_g=v_ln1_g, ln1_b=v_ln1_b, w_gate_up=v_w_gate_up, w_down=v_w_down,
                ln2_g=v_ln2_g, ln2_b=v_ln2_b)
    depth = w_in.shape[0]
    conv_cols = w_conv.shape[2]
    chip = _chip_index(lax.axis_index("x"), lax.axis_index("y"))

    conv_flat = jnp.pad(w_conv.reshape(-1), (0, 16 * LANES - w_conv.size)).reshape(1, 16, LANES)
    gathered = _gather_shards([weights[n].astype(bf16) for n in BIG] + [conv_flat])
    full = {n: _from_blocks(g, n) for n, g in zip(BIG, gathered[:4])}
    conv_full = gathered[4].reshape(N_CHIPS, 16 * LANES)[:, :w_conv.size].reshape(N_CHIPS, depth, 3, conv_cols)
    conv_full = conv_full.transpose(1, 2, 0, 3).reshape(depth, 3, N_CHIPS * conv_cols)

    layers = []
    for l in range(depth):
        layers.append(dict(
            w_in=full["w_in"][l], w_o=full["w_o"][l], w_gate_up=full["w_gate_up"][l], w_down=full["w_down"][l],
            w_conv=conv_full[l], w_pool=w_pool[l], pool_scale=pool_scale[l][None], sgu_ln_g=sgu_ln_g[l][None],
            w_spatial=w_spatial[l], b_spatial=b_spatial[l][:, :, None], ln1_g=ln1_g[l][None], ln1_b=ln1_b[l][None],
            ln2_g=ln2_g[l][None], ln2_b=ln2_b[l][None]))

    sq, grad_x, grads = _local_step(x[0], loss_target[0], layers)
    loss = lax.psum(0.5 / D_MODEL * jnp.sum(sq), ("x", "y", "c"))

    small_shapes = [(depth,) + grads[0][n].shape for n in SMALL]
    small_part = _pack_small([jnp.stack([g[n] for g in grads]) for n in SMALL])
    parts = [_to_blocks(jnp.stack([g[n] for g in grads]), n) for n in BIG]
    parts.append(small_part.reshape(1, N_CHIPS, SMALL_ROWS // N_CHIPS, LANES))
    summed = _reduce_to_owners(parts)
    small_sum = _gather_shards([summed[4]])[0].reshape(SMALL_ROWS, LANES)
    grad = dict(zip(BIG, summed[:4]))
    grad.update(zip(SMALL, _unpack_small(small_sum, small_shapes)))
    grad["pool_scale"] = grad["pool_scale"].reshape(pool_scale.shape)
    grad["sgu_ln_g"] = grad["sgu_ln_g"].reshape(sgu_ln_g.shape)
    for n in ("ln1_g", "ln1_b", "ln2_g", "ln2_b"):
        grad[n] = grad[n].reshape(ln1_g.shape)
    grad["w_conv"] = lax.dynamic_slice_in_dim(grad["w_conv"], chip * conv_cols, conv_cols, axis=2)

    delta, new_m, new_v = {}, {}, {}
    for n in BIG + ("w_conv",):
        delta[n], new_m[n], new_v[n] = _adamw(weights[n], grad[n], m_in[n], v_in[n])
    rest = [n for n in SMALL if n != "w_conv"]
    rest_shapes = [weights[n].shape for n in rest]
    packed = [_pack_small([src[n] for n in rest]) for src in (weights, grad, m_in, v_in)]
    for dst, res in zip((delta, new_m, new_v), _adamw(*packed)):
        dst.update(zip(rest, _unpack_small(res, rest_shapes)))

    return (loss, grad_x[None], *[grad[n] for n in WEIGHTS], *[delta[n] for n in WEIGHTS],
            *[new_m[n] for n in WEIGHTS], *[new_v[n] for n in WEIGHTS])
```

```python
import functools

import jax
import jax.numpy as jnp
from jax import lax
from jax.experimental import pallas as pl
from jax.experimental.pallas import tpu as pltpu

f32 = jnp.float32
bf16 = jnp.bfloat16

D_MODEL = 1024
DEPTH = 4
CONV_W = 384
POOL_W = 256
SGU_W = 384
IN_W = 3 * CONV_W + POOL_W + 2 * SGU_W
D_FF = 2816
CHUNK = 128
HEAD = 64
POOL_WINDOWS = (2, 4, 8, 16)
ALPHA = float((2 * DEPTH) ** 0.25)
LN_EPS = 1e-5
ADAM_LR = 0.001
ADAM_B1 = 0.9
ADAM_B2 = 0.999
ADAM_EPS = 1e-08
ADAM_WD = 0.01
ADAM_STEP = 10

LANES = 128
TOKEN_TILE = 256
N_CHIPS = 4
VMEM_LIMIT = 56 * 1024 * 1024

BLK_XA, BLK_GB, BLK_GC, BLK_P, BLK_U, BLK_V = 0, 3, 6, 9, 11, 14

MESH = pl.DeviceIdType.MESH


def _params(sem=None):
    return pltpu.CompilerParams(dimension_semantics=sem, vmem_limit_bytes=VMEM_LIMIT)


def _rows(width, tile=TOKEN_TILE):
    return pl.BlockSpec((tile, width), lambda i: (i, 0))


def _resident(shape):
    zeros = (0,) * len(shape)
    return pl.BlockSpec(shape, lambda *_: zeros, pipeline_mode=pl.Buffered(1))


def _nt(a, b):
    return lax.dot_general(a, b, (((1,), (1,)), ((), ())), preferred_element_type=f32)


def _tn(a, b):
    return lax.dot_general(a, b, (((0,), (0,)), ((), ())), preferred_element_type=f32)


def _mm(a, b):
    return jnp.dot(a, b, preferred_element_type=f32)


def _norm_fwd(z):
    mu = jnp.mean(z, axis=-1, keepdims=True)
    zc = z - mu
    var = jnp.mean(zc * zc, axis=-1, keepdims=True)
    rstd = lax.rsqrt(var + LN_EPS)
    return zc * rstd, rstd


def _norm_bwd(dxhat, xhat, rstd):
    m1 = jnp.mean(dxhat, axis=-1, keepdims=True)
    m2 = jnp.mean(dxhat * xhat, axis=-1, keepdims=True)
    return rstd * (dxhat - m1 - xhat * m2)


def _proj(x, w_in_b):
    s = x.shape[0]

    def body(x_ref, w_ref, p_ref, xb_ref):
        xb = x_ref[...].astype(bf16)
        xb_ref[...] = xb
        p_ref[...] = _mm(xb, w_ref[...])

    return pl.pallas_call(
        body, grid=(s // TOKEN_TILE,),
        in_specs=[_rows(D_MODEL), _resident((D_MODEL, IN_W))],
        out_specs=[_rows(IN_W), _rows(D_MODEL)],
        out_shape=[jax.ShapeDtypeStruct((s, IN_W), f32), jax.ShapeDtypeStruct((s, D_MODEL), bf16)],
        name="proj", compiler_params=_params(("arbitrary",)))(x, w_in_b)


def _wo_ln1(mixcat, x, w_o_b, g, b):
    s = x.shape[0]

    def body(m_ref, x_ref, w_ref, g_ref, b_ref, xhat_ref, rstd_ref, hb_ref):
        z = ALPHA * x_ref[...] + _mm(m_ref[...], w_ref[...])
        xhat, rstd = _norm_fwd(z)
        xhat_ref[...] = xhat
        rstd_ref[...] = rstd
        hb_ref[...] = (xhat * g_ref[...] + b_ref[...]).astype(bf16)

    return pl.pallas_call(
        body, grid=(s // TOKEN_TILE,),
        in_specs=[_rows(D_MODEL), _rows(D_MODEL), _resident((D_MODEL, D_MODEL)), _resident((1, D_MODEL)), _resident((1, D_MODEL))],
        out_specs=[_rows(D_MODEL), _rows(1), _rows(D_MODEL)],
        out_shape=[jax.ShapeDtypeStruct((s, D_MODEL), f32), jax.ShapeDtypeStruct((s, 1), f32),
                   jax.ShapeDtypeStruct((s, D_MODEL), bf16)],
        name="wo_ln1", compiler_params=_params(("arbitrary",)))(mixcat, x, w_o_b, g, b)


def _mlp_fwd(xhat1, g1, b1, w_gu_b, w_down_b, g2, b2):
    s = xhat1.shape[0]

    def body(xh_ref, g1_ref, b1_ref, wgu_ref, wd_ref, g2_ref, b2_ref, gu_ref, xhat2_ref, rstd2_ref, y_ref):
        h = xh_ref[...] * g1_ref[...] + b1_ref[...]
        gu = _mm(h.astype(bf16), wgu_ref[...])
        gu_ref[...] = gu
        gate = gu[:, :D_FF]
        act = gate * jax.nn.sigmoid(gate) * gu[:, D_FF:]
        z = ALPHA * h + _mm(act.astype(bf16), wd_ref[...])
        xhat2, rstd2 = _norm_fwd(z)
        xhat2_ref[...] = xhat2
        rstd2_ref[...] = rstd2
        y_ref[...] = xhat2 * g2_ref[...] + b2_ref[...]

    vec = _resident((1, D_MODEL))
    return pl.pallas_call(
        body, grid=(s // TOKEN_TILE,),
        in_specs=[_rows(D_MODEL), vec, vec, _resident((D_MODEL, 2 * D_FF)), _resident((D_FF, D_MODEL)), vec, vec],
        out_specs=[_rows(2 * D_FF), _rows(D_MODEL), _rows(1), _rows(D_MODEL)],
        out_shape=[jax.ShapeDtypeStruct((s, 2 * D_FF), f32), jax.ShapeDtypeStruct((s, D_MODEL), f32),
                   jax.ShapeDtypeStruct((s, 1), f32), jax.ShapeDtypeStruct((s, D_MODEL), f32)],
        name="mlp_fwd", compiler_params=_params(("arbitrary",)))(xhat1, g1, b1, w_gu_b, w_down_b, g2, b2)


def _loss_head(y, target):
    s = y.shape[0]

    def body(y_ref, t_ref, dy_ref, sq_ref):
        @pl.when(pl.program_id(0) == 0)
        def _():
            sq_ref[...] = jnp.zeros_like(sq_ref)

        e = y_ref[...] - t_ref[...]
        dy_ref[...] = e * (1.0 / D_MODEL)
        sq_ref[...] += jnp.sum(e * e, axis=0, keepdims=True)

    return pl.pallas_call(
        body, grid=(s // TOKEN_TILE,),
        in_specs=[_rows(D_MODEL), _rows(D_MODEL)],
        out_specs=[_rows(D_MODEL), pl.BlockSpec((1, D_MODEL), lambda i: (0, 0))],
        out_shape=[jax.ShapeDtypeStruct((s, D_MODEL), f32), jax.ShapeDtypeStruct((1, D_MODEL), f32)],
        name="loss_head", compiler_params=_params(("arbitrary",)))(y, target)


def _mlp_bwd(dy, xhat2, rstd2, g2, gu, w_gu_b, w_down_b):
    s = dy.shape[0]

    def body(dy_ref, xh_ref, rs_ref, g2_ref, gu_ref, wgu_ref, wd_ref, dz_ref, act_ref, dgu_ref, dh_ref, gg_ref, gb_ref):
        @pl.when(pl.program_id(0) == 0)
        def _():
            gg_ref[...] = jnp.zeros_like(gg_ref)
            gb_ref[...] = jnp.zeros_like(gb_ref)

        dy_t = dy_ref[...]
        xhat = xh_ref[...]
        gg_ref[...] += jnp.sum(dy_t * xhat, axis=0, keepdims=True)
        gb_ref[...] += jnp.sum(dy_t, axis=0, keepdims=True)
        dz = _norm_bwd(dy_t * g2_ref[...], xhat, rs_ref[...])
        dzb = dz.astype(bf16)
        dz_ref[...] = dzb
        dact = _nt(dzb, wd_ref[...])
        gate = gu_ref[:, :D_FF]
        up = gu_ref[:, D_FF:]
        sg = jax.nn.sigmoid(gate)
        silu = gate * sg
        act_ref[...] = (silu * up).astype(bf16)
        dgu_ref[:, :D_FF] = (dact * up * (sg * (1.0 + gate * (1.0 - sg)))).astype(bf16)
        dgu_ref[:, D_FF:] = (dact * silu).astype(bf16)
        dh_ref[...] = ALPHA * dz + _nt(dgu_ref[...], wgu_ref[...])

    vec_out = pl.BlockSpec((1, D_MODEL), lambda i: (0, 0))
    return pl.pallas_call(
        body, grid=(s // TOKEN_TILE,),
        in_specs=[_rows(D_MODEL), _rows(D_MODEL), _rows(1), _resident((1, D_MODEL)), _rows(2 * D_FF),
                  _resident((D_MODEL, 2 * D_FF)), _resident((D_FF, D_MODEL))],
        out_specs=[_rows(D_MODEL), _rows(D_FF), _rows(2 * D_FF), _rows(D_MODEL), vec_out, vec_out],
        out_shape=[jax.ShapeDtypeStruct((s, D_MODEL), bf16), jax.ShapeDtypeStruct((s, D_FF), bf16),
                   jax.ShapeDtypeStruct((s, 2 * D_FF), bf16), jax.ShapeDtypeStruct((s, D_MODEL), f32),
                   jax.ShapeDtypeStruct((1, D_MODEL), f32), jax.ShapeDtypeStruct((1, D_MODEL), f32)],
        name="mlp_bwd", compiler_params=_params(("arbitrary",)))(dy, xhat2, rstd2, g2, gu, w_gu_b, w_down_b)


def _ln1_wo_bwd(dh, xhat1, rstd1, g1, w_o_b):
    s = dh.shape[0]

    def body(dh_ref, xh_ref, rs_ref, g1_ref, w_ref, dz_ref, dzb_ref, dm_ref, gg_ref, gb_ref):
        @pl.when(pl.program_id(0) == 0)
        def _():
            gg_ref[...] = jnp.zeros_like(gg_ref)
            gb_ref[...] = jnp.zeros_like(gb_ref)

        dh_t = dh_ref[...]
        xhat = xh_ref[...]
        gg_ref[...] += jnp.sum(dh_t * xhat, axis=0, keepdims=True)
        gb_ref[...] += jnp.sum(dh_t, axis=0, keepdims=True)
        dz = _norm_bwd(dh_t * g1_ref[...], xhat, rs_ref[...])
        dz_ref[...] = dz
        dzb = dz.astype(bf16)
        dzb_ref[...] = dzb
        dm_ref[...] = _nt(dzb, w_ref[...])

    vec_out = pl.BlockSpec((1, D_MODEL), lambda i: (0, 0))
    return pl.pallas_call(
        body, grid=(s // TOKEN_TILE,),
        in_specs=[_rows(D_MODEL), _rows(D_MODEL), _rows(1), _resident((1, D_MODEL)), _resident((D_MODEL, D_MODEL))],
        out_specs=[_rows(D_MODEL), _rows(D_MODEL), _rows(D_MODEL), vec_out, vec_out],
        out_shape=[jax.ShapeDtypeStruct((s, D_MODEL), f32), jax.ShapeDtypeStruct((s, D_MODEL), bf16),
                   jax.ShapeDtypeStruct((s, D_MODEL), f32), jax.ShapeDtypeStruct((1, D_MODEL), f32),
                   jax.ShapeDtypeStruct((1, D_MODEL), f32)],
        name="ln1_wo_bwd", compiler_params=_params(("arbitrary",)))(dh, xhat1, rstd1, g1, w_o_b)


def _dx(dz1, dproj, w_in_b):
    s = dz1.shape[0]

    def body(dz_ref, dp_ref, w_ref, dx_ref):
        dx_ref[...] = ALPHA * dz_ref[...] + _nt(dp_ref[...], w_ref[...])

    return pl.pallas_call(
        body, grid=(s // TOKEN_TILE,),
        in_specs=[_rows(D_MODEL), _rows(IN_W), _resident((D_MODEL, IN_W))],
        out_specs=_rows(D_MODEL),
        out_shape=jax.ShapeDtypeStruct((s, D_MODEL), f32),
        name="dx", compiler_params=_params(("arbitrary",)))(dz1, dproj, w_in_b)


def _weight_grad(a, b, bm, bn):
    s, m = a.shape
    n = b.shape[1]

    def body(a_ref, b_ref, o_ref):
        o_ref[...] = _tn(a_ref[...], b_ref[...]).astype(bf16)

    return pl.pallas_call(
        body, grid=(m // bm, n // bn),
        in_specs=[pl.BlockSpec((s, bm), lambda i, j: (0, i)), pl.BlockSpec((s, bn), lambda i, j: (0, j))],
        out_specs=pl.BlockSpec((bm, bn), lambda i, j: (i, j)),
        out_shape=jax.ShapeDtypeStruct((m, n), bf16),
        name="weight_grad", compiler_params=_params(("arbitrary", "arbitrary")))(a, b)


def _shift_down(a, k):
    row = lax.broadcasted_iota(jnp.int32, a.shape, 0)
    return jnp.where(row >= k, pltpu.roll(a, k, 0), 0.0)


def _shift_up(a, k):
    n = a.shape[0]
    row = lax.broadcasted_iota(jnp.int32, a.shape, 0)
    return jnp.where(row < n - k, pltpu.roll(a, n - k, 0), 0.0)


def _slab(s, block):
    return pl.BlockSpec((s, LANES), lambda k: (0, block + k))


def _conv_y(z, w):
    return w[0:1, :] * _shift_down(z, 2) + w[1:2, :] * _shift_down(z, 1) + w[2:3, :] * z


def _conv_fwd(proj, w_conv):
    s = proj.shape[0]

    def body(xa_ref, gb_ref, gc_ref, w_ref, o_ref):
        z = gc_ref[...] * xa_ref[...]
        o_ref[...] = (gb_ref[...] * _conv_y(z, w_ref[...])).astype(bf16)

    return pl.pallas_call(
        body, grid=(CONV_W // LANES,),
        in_specs=[_slab(s, BLK_XA), _slab(s, BLK_GB), _slab(s, BLK_GC), pl.BlockSpec((3, LANES), lambda k: (0, k))],
        out_specs=_slab(s, 0),
        out_shape=jax.ShapeDtypeStruct((s, CONV_W), bf16),
        name="conv_fwd", compiler_params=_params(("arbitrary",)))(proj, proj, proj, w_conv)


def _conv_bwd(proj, dmix, w_conv):
    s = proj.shape[0]

    def body(xa_ref, gb_ref, gc_ref, dy_ref, w_ref, dxa_ref, dgb_ref, dgc_ref, dw_ref):
        xa = xa_ref[...]
        gc = gc_ref[...]
        w = w_ref[...]
        z = gc * xa
        dya = dy_ref[...]
        dgb_ref[...] = (dya * _conv_y(z, w)).astype(bf16)
        dy = dya * gb_ref[...]
        dz = w[2:3, :] * dy + w[1:2, :] * _shift_up(dy, 1) + w[0:1, :] * _shift_up(dy, 2)
        dxa_ref[...] = (dz * gc).astype(bf16)
        dgc_ref[...] = (dz * xa).astype(bf16)
        dw_ref[0:1, :] = jnp.sum(dy * _shift_down(z, 2), axis=0, keepdims=True)
        dw_ref[1:2, :] = jnp.sum(dy * _shift_down(z, 1), axis=0, keepdims=True)
        dw_ref[2:3, :] = jnp.sum(dy * z, axis=0, keepdims=True)

    out = jax.ShapeDtypeStruct((s, CONV_W), bf16)
    return pl.pallas_call(
        body, grid=(CONV_W // LANES,),
        in_specs=[_slab(s, BLK_XA), _slab(s, BLK_GB), _slab(s, BLK_GC), _slab(s, 0), pl.BlockSpec((3, LANES), lambda k: (0, k))],
        out_specs=[_slab(s, 0), _slab(s, 0), _slab(s, 0), pl.BlockSpec((3, LANES), lambda k: (0, k))],
        out_shape=[out, out, out, jax.ShapeDtypeStruct((3, CONV_W), f32)],
        name="conv_bwd", compiler_params=_params(("arbitrary",)))(proj, proj, proj, dmix, w_conv)


def _pool_window(k):
    lane = lax.broadcasted_iota(jnp.int32, (1, LANES), 1)
    low = lane < HEAD
    first = k == 0
    wlen = jnp.where(low, jnp.where(first, POOL_WINDOWS[0], POOL_WINDOWS[2]), jnp.where(first, POOL_WINDOWS[1], POOL_WINDOWS[3]))
    return wlen, low, first


def _pool_diff(p, k):
    wlen, low, first = _pool_window(k)
    s2 = p + _shift_down(p, 1)
    s4 = s2 + _shift_down(s2, 2)
    s8 = s4 + _shift_down(s4, 4)
    s16 = s8 + _shift_down(s8, 8)
    win = jnp.where(low, jnp.where(first, s2, s8), jnp.where(first, s4, s16))
    row = lax.broadcasted_iota(jnp.int32, p.shape, 0)
    count = jnp.minimum(row + 1, wlen).astype(f32)
    return win / count - p, count


def _pool_weight(w_ref):
    zero = jnp.zeros((HEAD, HEAD), f32)
    top = jnp.concatenate([w_ref[0], zero], axis=1)
    bottom = jnp.concatenate([zero, w_ref[1]], axis=1)
    return jnp.concatenate([top, bottom], axis=0).astype(bf16)


def _pool_fwd(proj, w_pool, pool_scale):
    s = proj.shape[0]

    def body(p_ref, w_ref, sc_ref, o_ref):
        d, _ = _pool_diff(p_ref[...], pl.program_id(0))
        o_ref[...] = (_mm(d.astype(bf16), _pool_weight(w_ref)) * sc_ref[...]).astype(bf16)

    return pl.pallas_call(
        body, grid=(POOL_W // LANES,),
        in_specs=[_slab(s, BLK_P), pl.BlockSpec((2, HEAD, HEAD), lambda k: (k, 0, 0)), pl.BlockSpec((1, LANES), lambda k: (0, k))],
        out_specs=_slab(s, 0),
        out_shape=jax.ShapeDtypeStruct((s, POOL_W), bf16),
        name="pool_fwd", compiler_params=_params(("arbitrary",)))(proj, w_pool, pool_scale)


def _pool_bwd(proj, dmix, w_pool, pool_scale):
    s = proj.shape[0]

    def body(p_ref, dy_ref, w_ref, sc_ref, dp_ref, dw_ref, dsc_ref):
        k = pl.program_id(0)
        d, count = _pool_diff(p_ref[...], k)
        wbd = _pool_weight(w_ref)
        db = d.astype(bf16)
        dyb = dy_ref[...]
        dsc_ref[...] = jnp.sum(dyb * _mm(db, wbd), axis=0, keepdims=True)
        dpre = (dyb * sc_ref[...]).astype(bf16)
        dwbd = _tn(db, dpre)
        dw_ref[0] = dwbd[:HEAD, :HEAD]
        dw_ref[1] = dwbd[HEAD:, HEAD:]
        dd = _nt(dpre, wbd)
        e = dd / count
        wlen, low, first = _pool_window(k)
        a2 = e + _shift_up(e, 1)
        a4 = a2 + _shift_up(a2, 2)
        a8 = a4 + _shift_up(a4, 4)
        a16 = a8 + _shift_up(a8, 8)
        back = jnp.where(low, jnp.where(first, a2, a8), jnp.where(first, a4, a16))
        dp_ref[...] = (back - dd).astype(bf16)

    return pl.pallas_call(
        body, grid=(POOL_W // LANES,),
        in_specs=[_slab(s, BLK_P), _slab(s, CONV_W // LANES), pl.BlockSpec((2, HEAD, HEAD), lambda k: (k, 0, 0)),
                  pl.BlockSpec((1, LANES), lambda k: (0, k))],
        out_specs=[_slab(s, 0), pl.BlockSpec((2, HEAD, HEAD), lambda k: (k, 0, 0)), pl.BlockSpec((1, LANES), lambda k: (0, k))],
        out_shape=[jax.ShapeDtypeStruct((s, POOL_W), bf16), jax.ShapeDtypeStruct((4, HEAD, HEAD), f32),
                   jax.ShapeDtypeStruct((1, POOL_W), f32)],
        name="pool_bwd", compiler_params=_params(("arbitrary",)))(proj, dmix, w_pool, pool_scale)


INV_SQRT2 = 0.7071067811865476
INV_SQRT_2PI = 0.3989422804014327


def _gelu(x):
    return 0.5 * x * (1.0 + lax.erf(x * INV_SQRT2))


def _gelu_grad(x):
    return 0.5 * (1.0 + lax.erf(x * INV_SQRT2)) + x * (INV_SQRT_2PI * jnp.exp(-0.5 * x * x))


def _head_mean(a, low):
    s_low = jnp.sum(jnp.where(low, a, 0.0), axis=-1, keepdims=True)
    s_high = jnp.sum(jnp.where(low, 0.0, a), axis=-1, keepdims=True)
    return jnp.where(low, s_low, s_high) * (1.0 / HEAD)


def _tril():
    r = lax.broadcasted_iota(jnp.int32, (CHUNK, CHUNK), 0)
    c = lax.broadcasted_iota(jnp.int32, (CHUNK, CHUNK), 1)
    return r >= c


def _sgu_chunk(up, vp, g, wm0, wm1, b0, b1, low):
    ug = _gelu(up)
    vg = _gelu(vp)
    vc = vg - _head_mean(vg, low)
    rstd = lax.rsqrt(_head_mean(vc * vc, low) + LN_EPS)
    vn = vc * rstd
    vb = (vn * g).astype(bf16)
    mixed = jnp.where(low, _mm(wm0, vb) + b0, _mm(wm1, vb) + b1)
    return ug, vn, rstd, vb, mixed


def _sgu_specs(s):
    return [_slab(s, BLK_U), _slab(s, BLK_V), pl.BlockSpec((1, LANES), lambda k: (0, k)),
            pl.BlockSpec((2, CHUNK, CHUNK), lambda k: (k, 0, 0)), pl.BlockSpec((2, CHUNK, 1), lambda k: (k, 0, 0))]


def _sgu_fwd(proj, sgu_g, w_spatial, b_spatial3):
    s = proj.shape[0]

    def body(u_ref, v_ref, g_ref, w_ref, b_ref, o_ref):
        low = lax.broadcasted_iota(jnp.int32, (1, LANES), 1) < HEAD
        mask = _tril()
        wm0 = jnp.where(mask, w_ref[0], 0.0).astype(bf16)
        wm1 = jnp.where(mask, w_ref[1], 0.0).astype(bf16)
        g = g_ref[...]
        b0 = b_ref[0]
        b1 = b_ref[1]

        def chunk(n, carry):
            rows = pl.ds(pl.multiple_of(n * CHUNK, CHUNK), CHUNK)
            ug, _, _, _, mixed = _sgu_chunk(u_ref[rows, :], v_ref[rows, :], g, wm0, wm1, b0, b1, low)
            o_ref[rows, :] = (ug * mixed).astype(bf16)
            return carry

        lax.fori_loop(0, s // CHUNK, chunk, 0)

    return pl.pallas_call(
        body, grid=(SGU_W // LANES,),
        in_specs=_sgu_specs(s),
        out_specs=_slab(s, 0),
        out_shape=jax.ShapeDtypeStruct((s, SGU_W), bf16),
        name="sgu_fwd", compiler_params=_params(("arbitrary",)))(proj, proj, sgu_g, w_spatial, b_spatial3)


def _sgu_bwd(proj, dmix, sgu_g, w_spatial, b_spatial3):
    s = proj.shape[0]

    def body(u_ref, v_ref, g_ref, w_ref, b_ref, dy_ref, du_ref, dv_ref, dg_ref, dw_ref, db_ref):
        low = lax.broadcasted_iota(jnp.int32, (1, LANES), 1) < HEAD
        mask = _tril()
        w0 = jnp.where(mask, w_ref[0], 0.0)
        w1 = jnp.where(mask, w_ref[1], 0.0)
        wm0 = w0.astype(bf16)
        wm1 = w1.astype(bf16)
        wt0 = w0.T.astype(bf16)
        wt1 = w1.T.astype(bf16)
        g = g_ref[...]
        b0 = b_ref[0]
        b1 = b_ref[1]
        dg_ref[...] = jnp.zeros_like(dg_ref)
        dw_ref[...] = jnp.zeros_like(dw_ref)
        db_ref[...] = jnp.zeros_like(db_ref)

        def chunk(n, carry):
            rows = pl.ds(pl.multiple_of(n * CHUNK, CHUNK), CHUNK)
            up = u_ref[rows, :]
            vp = v_ref[rows, :]
            ug, vn, rstd, vb, mixed = _sgu_chunk(up, vp, g, wm0, wm1, b0, b1, low)
            dy = dy_ref[rows, :]
            du_ref[rows, :] = (dy * mixed * _gelu_grad(up)).astype(bf16)
            dmix_c = dy * ug
            db_ref[0] += jnp.sum(jnp.where(low, dmix_c, 0.0), axis=-1, keepdims=True)
            db_ref[1] += jnp.sum(jnp.where(low, 0.0, dmix_c), axis=-1, keepdims=True)
            dmb = dmix_c.astype(bf16)
            zero = jnp.zeros_like(dmb)
            dw_ref[0] += _nt(jnp.where(low, dmb, zero), vb)
            dw_ref[1] += _nt(jnp.where(low, zero, dmb), vb)
            dvnorm = jnp.where(low, _mm(wt0, dmb), _mm(wt1, dmb))
            dg_ref[...] += jnp.sum(dvnorm * vn, axis=0, keepdims=True)
            dvn = dvnorm * g
            dvg = rstd * (dvn - _head_mean(dvn, low) - vn * _head_mean(dvn * vn, low))
            dv_ref[rows, :] = (dvg * _gelu_grad(vp)).astype(bf16)
            return carry

        lax.fori_loop(0, s // CHUNK, chunk, 0)
        dw_ref[0] = jnp.where(mask, dw_ref[0], 0.0)
        dw_ref[1] = jnp.where(mask, dw_ref[1], 0.0)

    out = jax.ShapeDtypeStruct((s, SGU_W), bf16)
    return pl.pallas_call(
        body, grid=(SGU_W // LANES,),
        in_specs=_sgu_specs(s) + [_slab(s, (CONV_W + POOL_W) // LANES)],
        out_specs=[_slab(s, 0), _slab(s, 0), pl.BlockSpec((1, LANES), lambda k: (0, k)),
                   pl.BlockSpec((2, CHUNK, CHUNK), lambda k: (k, 0, 0)), pl.BlockSpec((2, CHUNK, 1), lambda k: (k, 0, 0))],
        out_shape=[out, out, jax.ShapeDtypeStruct((1, SGU_W), f32), jax.ShapeDtypeStruct((6, CHUNK, CHUNK), f32),
                   jax.ShapeDtypeStruct((6, CHUNK, 1), f32)],
        name="sgu_bwd", compiler_params=_params(("arbitrary",)))(proj, proj, sgu_g, w_spatial, b_spatial3, dmix)


def _layer_fwd(x, w):
    proj, xb = _proj(x, w["w_in"])
    ya = _conv_fwd(proj, w["w_conv"])
    yb = _pool_fwd(proj, w["w_pool"], w["pool_scale"])
    yc = _sgu_fwd(proj, w["sgu_ln_g"], w["w_spatial"], w["b_spatial"])
    mixcat = jnp.concatenate([ya, yb, yc], axis=1)
    xhat1, rstd1, hb = _wo_ln1(mixcat, x, w["w_o"], w["ln1_g"], w["ln1_b"])
    gu, xhat2, rstd2, y = _mlp_fwd(xhat1, w["ln1_g"], w["ln1_b"], w["w_gate_up"], w["w_down"], w["ln2_g"], w["ln2_b"])
    saved = dict(proj=proj, xb=xb, mixcat=mixcat, xhat1=xhat1, rstd1=rstd1, hb=hb, gu=gu, xhat2=xhat2, rstd2=rstd2)
    return y, saved


def _layer_bwd(dy, w, sv):
    dz2b, actb, dgub, dh, g_ln2_g, g_ln2_b = _mlp_bwd(dy, sv["xhat2"], sv["rstd2"], w["ln2_g"], sv["gu"], w["w_gate_up"], w["w_down"])
    dz1, dz1b, dmix, g_ln1_g, g_ln1_b = _ln1_wo_bwd(dh, sv["xhat1"], sv["rstd1"], w["ln1_g"], w["w_o"])
    dxa, dgb, dgc, g_conv = _conv_bwd(sv["proj"], dmix, w["w_conv"])
    dp, g_pool, g_pscale = _pool_bwd(sv["proj"], dmix, w["w_pool"], w["pool_scale"])
    du, dv, g_sgu_g, g_spatial, g_bsp = _sgu_bwd(sv["proj"], dmix, w["sgu_ln_g"], w["w_spatial"], w["b_spatial"])
    dproj = jnp.concatenate([dxa, dgb, dgc, dp, du, dv], axis=1)
    dx = _dx(dz1, dproj, w["w_in"])
    grads = dict(
        w_in=_weight_grad(sv["xb"], dproj, 512, IN_W),
        w_o=_weight_grad(sv["mixcat"], dz1b, 512, D_MODEL),
        w_gate_up=_weight_grad(sv["hb"], dgub, 512, D_FF // 2),
        w_down=_weight_grad(actb, dz2b, D_FF // 2, D_MODEL),
        w_conv=g_conv, w_pool=g_pool, pool_scale=g_pscale, sgu_ln_g=g_sgu_g, w_spatial=g_spatial,
        b_spatial=g_bsp.reshape(6, CHUNK), ln1_g=g_ln1_g, ln1_b=g_ln1_b, ln2_g=g_ln2_g, ln2_b=g_ln2_b)
    return dx, grads


def _local_step(x, target, layers):
    saved = []
    for w in layers:
        x, sv = _layer_fwd(x, w)
        saved.append(sv)
    dy, sq = _loss_head(x, target)
    grads = [None] * len(layers)
    for l in reversed(range(len(layers))):
        dy, grads[l] = _layer_bwd(dy, layers[l], saved[l])
    return sq, dy, grads


ANY = pl.BlockSpec(memory_space=pl.ANY)


def _place():
    x, y, c = lax.axis_index("x"), lax.axis_index("y"), lax.axis_index("c")
    others = [(1 - x, y), (x, 1 - y), (1 - x, 1 - y)]
    return x, y, c, others


def _chip_index(cx, cy):
    return 2 * cx + cy


def _half(ref_rows, c):
    half = ref_rows // 2
    return pl.ds(pl.multiple_of(c * half, 8), half)


def _remote(src, dst, send_sem, recv_sem, device):
    return pltpu.make_async_remote_copy(src_ref=src, dst_ref=dst, send_sem=send_sem, recv_sem=recv_sem,
                                        device_id=device, device_id_type=MESH)


def _gather_shards(shards):
    n = len(shards)
    base, total = [], 0
    for s in shards:
        base.append(total)
        total += 6 * s.shape[0]

    def body(*refs):
        ins, outs = refs[:n], refs[n:2 * n]
        send, recv = refs[2 * n:]
        x, y, c, others = _place()
        me = _chip_index(x, y)
        sib = (x, y, 1 - c)
        sends = []
        for f in range(n):
            depth, rows = ins[f].shape[0], ins[f].shape[1]
            for l in range(depth):
                for k, (cx, cy) in enumerate(others):
                    sem = base[f] + 6 * l + k
                    cp = _remote(ins[f].at[l, _half(rows, c)], outs[f].at[l, me, _half(rows, c)],
                                 send.at[sem], recv.at[sem], (cx, cy, c))
                    cp.start()
                    sends.append(cp)
        for f in range(n):
            depth, rows = ins[f].shape[0], ins[f].shape[1]
            for l in range(depth):
                for k, (cx, cy) in enumerate(others):
                    sem = base[f] + 6 * l + k
                    landed = outs[f].at[l, _chip_index(cx, cy), _half(rows, c)]
                    _remote(landed, landed, send.at[sem], recv.at[sem], (cx, cy, c)).wait_recv()
                    cp = _remote(landed, landed, send.at[sem + 3], recv.at[sem + 3], sib)
                    cp.start()
                    sends.append(cp)
        for f in range(n):
            depth, rows = ins[f].shape[0], ins[f].shape[1]
            for l in range(depth):
                for k, (cx, cy) in enumerate(others):
                    sem = base[f] + 6 * l + k + 3
                    passed = outs[f].at[l, _chip_index(cx, cy), _half(rows, 1 - c)]
                    _remote(passed, passed, send.at[sem], recv.at[sem], sib).wait_recv()
        for cp in sends:
            cp.wait_send()

    gathered = pl.pallas_call(
        body, in_specs=[ANY] * n, out_specs=[ANY] * n,
        out_shape=[jax.ShapeDtypeStruct((s.shape[0], N_CHIPS) + s.shape[1:], s.dtype) for s in shards],
        scratch_shapes=[pltpu.SemaphoreType.DMA((total,)), pltpu.SemaphoreType.DMA((total,))],
        name="gather_shards")(*shards)
    return [_place_own(g, s) for g, s in zip(gathered, shards)]


def _scalar(value):
    return jnp.reshape(value, (1,)).astype(jnp.int32)


def _place_own(blocks, shard):
    depth, rows, cols = shard.shape

    def body(me_ref, b_ref, s_ref, o_ref):
        o_ref[...] = s_ref[...]

    return pl.pallas_call(
        body,
        grid_spec=pltpu.PrefetchScalarGridSpec(
            num_scalar_prefetch=1, grid=(depth,),
            in_specs=[ANY, pl.BlockSpec((None, rows, cols), lambda l, me: (l, 0, 0))],
            out_specs=pl.BlockSpec((None, None, rows, cols), lambda l, me: (l, me[0], 0, 0))),
        out_shape=jax.ShapeDtypeStruct(blocks.shape, blocks.dtype),
        input_output_aliases={1: 0},
        name="place_own", compiler_params=_params(("arbitrary",)))(
            _scalar(_chip_index(lax.axis_index("x"), lax.axis_index("y"))), blocks, shard)


def _sibling_split(parts):
    n = len(parts)

    def body(*refs):
        ins, got = refs[:n], refs[n:2 * n]
        send, recv = refs[2 * n:]
        x, y, c, _ = _place()
        sib = (x, y, 1 - c)
        for f in range(n):
            depth, rows = ins[f].shape[0], ins[f].shape[2]
            for l in range(depth):
                for j in range(N_CHIPS):
                    _remote(ins[f].at[l, j, _half(rows, 1 - c)], got[f].at[l, j], send.at[f], recv.at[f], sib).start()
        for f in range(n):
            _remote(got[f], got[f], send.at[f], recv.at[f], sib).wait()

    return pl.pallas_call(
        body, in_specs=[ANY] * n, out_specs=[ANY] * n,
        out_shape=[jax.ShapeDtypeStruct(p.shape[:2] + (p.shape[2] // 2, p.shape[3]), p.dtype) for p in parts],
        scratch_shapes=[pltpu.SemaphoreType.DMA((n,)), pltpu.SemaphoreType.DMA((n,))],
        name="sibling_split")(*parts)


def _chip_scatter(sums):
    n = len(sums)

    def body(*refs):
        ins, outs = refs[:n], refs[n:2 * n]
        send, recv = refs[2 * n:]
        x, y, c, others = _place()
        for f in range(n):
            for l in range(ins[f].shape[0]):
                for k, (cx, cy) in enumerate(others):
                    _remote(ins[f].at[l, _chip_index(cx, cy)], outs[f].at[k, l], send.at[f * 3 + k], recv.at[f * 3 + k],
                            (cx, cy, c)).start()
        for f in range(n):
            for k, (cx, cy) in enumerate(others):
                _remote(outs[f].at[k], outs[f].at[k], send.at[f * 3 + k], recv.at[f * 3 + k], (cx, cy, c)).wait()

    return pl.pallas_call(
        body, in_specs=[ANY] * n, out_specs=[ANY] * n,
        out_shape=[jax.ShapeDtypeStruct((3, s.shape[0]) + s.shape[2:], s.dtype) for s in sums],
        scratch_shapes=[pltpu.SemaphoreType.DMA((3 * n,)), pltpu.SemaphoreType.DMA((3 * n,))],
        name="chip_scatter")(*sums)


def _sibling_join(sums):
    n = len(sums)

    def body(*refs):
        ins, outs = refs[:n], refs[n:2 * n]
        send, recv = refs[2 * n:]
        x, y, c, _ = _place()
        sib = (x, y, 1 - c)
        for f in range(n):
            depth, half_rows = outs[f].shape[0], outs[f].shape[1] // 2
            pieces = 4 if half_rows % 64 == 0 else 1
            step = half_rows // pieces
            for l in range(depth):
                for p in range(pieces):
                    mine = outs[f].at[l, pl.ds(pl.multiple_of(c * half_rows + p * step, 8), step)]
                    _remote(mine, mine, send.at[f], recv.at[f], sib).start()
        for f in range(n):
            half = outs[f].at[:, pl.ds(0, outs[f].shape[1] // 2)]
            _remote(half, half, send.at[f], recv.at[f], sib).wait()

    return pl.pallas_call(
        body, in_specs=[ANY] * n, out_specs=[ANY] * n,
        out_shape=[jax.ShapeDtypeStruct(s.shape, s.dtype) for s in sums],
        input_output_aliases={f: f for f in range(n)},
        scratch_shapes=[pltpu.SemaphoreType.DMA((n,)), pltpu.SemaphoreType.DMA((n,))],
        name="sibling_join")(*sums)


def _row_tile(rows):
    for tile in (TOKEN_TILE, 128, 64, 32, 16):
        if rows % tile == 0:
            return tile
    return rows


def _add_pair(part, got):
    depth, chips, rows, cols = part.shape
    half = rows // 2

    def body(c_ref, a_ref, b_ref, o_ref):
        o_ref[...] = (a_ref[...].astype(f32) + b_ref[...].astype(f32)).astype(o_ref.dtype)

    return pl.pallas_call(
        body,
        grid_spec=pltpu.PrefetchScalarGridSpec(
            num_scalar_prefetch=1, grid=(depth, chips),
            in_specs=[pl.BlockSpec((None, None, None, half, cols), lambda l, j, c: (l, j, c[0], 0, 0)),
                      pl.BlockSpec((None, None, half, cols), lambda l, j, c: (l, j, 0, 0))],
            out_specs=pl.BlockSpec((None, None, half, cols), lambda l, j, c: (l, j, 0, 0))),
        out_shape=jax.ShapeDtypeStruct(got.shape, part.dtype),
        name="add_pair", compiler_params=_params(("arbitrary", "arbitrary")))(
            _scalar(lax.axis_index("c")), part.reshape(depth, chips, 2, half, cols), got)


def _add_slots(chip_sums, slots):
    depth, _, half, cols = chip_sums.shape

    def body(at_ref, own_ref, s_ref, o_ref):
        acc = own_ref[...].astype(f32)
        for k in range(3):
            acc = acc + s_ref[k].astype(f32)
        o_ref[...] = acc

    at = jnp.concatenate([_scalar(_chip_index(lax.axis_index("x"), lax.axis_index("y"))), _scalar(lax.axis_index("c"))])
    out = pl.pallas_call(
        body,
        grid_spec=pltpu.PrefetchScalarGridSpec(
            num_scalar_prefetch=1, grid=(depth,),
            in_specs=[pl.BlockSpec((None, None, half, cols), lambda l, at: (l, at[0], 0, 0)),
                      pl.BlockSpec((3, None, half, cols), lambda l, at: (0, l, 0, 0))],
            out_specs=pl.BlockSpec((None, None, half, cols), lambda l, at: (l, at[1], 0, 0))),
        out_shape=jax.ShapeDtypeStruct((depth, 2, half, cols), f32),
        name="add_slots", compiler_params=_params(("arbitrary",)))(at, chip_sums, slots)
    return out.reshape(depth, 2 * half, cols)


def _adamw(w, g, m, v):
    shape = w.shape
    flat = [a.reshape(-1, shape[-1]) for a in (w, g, m, v)]
    tile = _row_tile(flat[0].shape[0])

    def body(w_ref, g_ref, m_ref, v_ref, d_ref, nm_ref, nv_ref):
        grad = g_ref[...]
        nm = ADAM_B1 * m_ref[...] + (1.0 - ADAM_B1) * grad
        nv = ADAM_B2 * v_ref[...] + (1.0 - ADAM_B2) * (grad * grad)
        m_hat = nm / (1.0 - ADAM_B1 ** ADAM_STEP)
        v_hat = nv / (1.0 - ADAM_B2 ** ADAM_STEP)
        d_ref[...] = -ADAM_LR * (m_hat / (jnp.sqrt(v_hat) + ADAM_EPS) + ADAM_WD * w_ref[...])
        nm_ref[...] = nm
        nv_ref[...] = nv

    spec = _rows(shape[-1], tile)
    out = jax.ShapeDtypeStruct(flat[0].shape, f32)
    res = pl.pallas_call(
        body, grid=(flat[0].shape[0] // tile,),
        in_specs=[spec] * 4, out_specs=[spec] * 3, out_shape=[out] * 3,
        name="adamw", compiler_params=_params(("arbitrary",)))(*flat)
    return [r.reshape(shape) for r in res]


def _reduce_to_owners(parts):
    got = _sibling_split(parts)
    chip_sums = [_add_pair(p, g) for p, g in zip(parts, got)]
    slots = _chip_scatter(chip_sums)
    return _sibling_join([_add_slots(cs, s) for cs, s in zip(chip_sums, slots)])


SMALL = ("w_conv", "w_pool", "pool_scale", "sgu_ln_g", "w_spatial", "b_spatial", "ln1_g", "ln1_b", "ln2_g", "ln2_b")
WEIGHTS = ("w_in", "w_conv", "w_pool", "pool_scale", "sgu_ln_g", "w_spatial", "b_spatial", "w_o", "ln1_g", "ln1_b",
           "w_gate_up", "w_down", "ln2_g", "ln2_b")
BIG = ("w_in", "w_o", "w_gate_up", "w_down")
COLUMN_SHARDED = ("w_in", "w_gate_up")
SMALL_ROWS = 4096


def _pack_small(arrays):
    flat = jnp.concatenate([a.reshape(-1) for a in arrays])
    return jnp.pad(flat, (0, SMALL_ROWS * LANES - flat.shape[0])).reshape(SMALL_ROWS, LANES)


def _unpack_small(packed, shapes):
    flat = packed.reshape(-1)
    out, at = [], 0
    for shp in shapes:
        size = 1
        for d in shp:
            size *= d
        out.append(flat[at:at + size].reshape(shp))
        at += size
    return out


def _to_blocks(full, name):
    depth, rows, cols = full.shape
    if name in COLUMN_SHARDED:
        return full.reshape(depth, rows, N_CHIPS, cols // N_CHIPS).transpose(0, 2, 1, 3)
    return full.reshape(depth, N_CHIPS, rows // N_CHIPS, cols)


def _from_blocks(blocks, name):
    depth, _, rows, cols = blocks.shape
    if name in COLUMN_SHARDED:
        return blocks.transpose(0, 2, 1, 3).reshape(depth, rows, N_CHIPS * cols)
    return blocks.reshape(depth, N_CHIPS * rows, cols)


def kernel(x, w_in, w_conv, w_pool, pool_scale, sgu_ln_g, w_spatial, b_spatial, w_o, ln1_g, ln1_b, w_gate_up, w_down, ln2_g, ln2_b, loss_target, m_w_in, m_w_conv, m_w_pool, m_pool_scale, m_sgu_ln_g, m_w_spatial, m_b_spatial, m_w_o, m_ln1_g, m_ln1_b, m_w_gate_up, m_w_down, m_ln2_g, m_ln2_b, v_w_in, v_w_conv, v_w_pool, v_pool_scale, v_sgu_ln_g, v_w_spatial, v_b_spatial, v_w_o, v_ln1_g, v_ln1_b, v_w_gate_up, v_w_down, v_ln2_g, v_ln2_b):
    weights = dict(w_in=w_in, w_conv=w_conv, w_pool=w_pool, pool_scale=pool_scale, sgu_ln_g=sgu_ln_g, w_spatial=w_spatial,
                   b_spatial=b_spatial, w_o=w_o, ln1_g=ln1_g, ln1_b=ln1_b, w_gate_up=w_gate_up, w_down=w_down, ln2_g=ln2_g, ln2_b=ln2_b)
    m_in = dict(w_in=m_w_in, w_conv=m_w_conv, w_pool=m_w_pool, pool_scale=m_pool_scale, sgu_ln_g=m_sgu_ln_g, w_spatial=m_w_spatial,
                b_spatial=m_b_spatial, w_o=m_w_o, ln1_g=m_ln1_g, ln1_b=m_ln1_b, w_gate_up=m_w_gate_up, w_down=m_w_down,
                ln2_g=m_ln2_g, ln2_b=m_ln2_b)
    v_in = dict(w_in=v_w_in, w_conv=v_w_conv, w_pool=v_w_pool, pool_scale=v_pool_scale, sgu_ln_g=v_sgu_ln_g, w_spatial=v_w_spatial,
                b_spatial=v_b_spatial, w_o=v_w_o, ln1_g=v_ln1_g, ln1_b=v_ln1_b, w_gate_up=v_w_gate_up, w_down=v_w_down,
                ln2_g=v_ln2_g, ln2_b=v_ln2_b)
    depth = w_in.shape[0]
    conv_cols = w_conv.shape[2]
    chip = _chip_index(lax.axis_index("x"), lax.axis_index("y"))

    conv_flat = jnp.pad(w_conv.reshape(-1), (0, 16 * LANES - w_conv.size)).reshape(1, 16, LANES)
    gathered = _gather_shards([weights[n].astype(bf16) for n in BIG] + [conv_flat])
    full = {n: _from_blocks(g, n) for n, g in zip(BIG, gathered[:4])}
    conv_full = gathered[4].reshape(N_CHIPS, 16 * LANES)[:, :w_conv.size].reshape(N_CHIPS, depth, 3, conv_cols)
    conv_full = conv_full.transpose(1, 2, 0, 3).reshape(depth, 3, N_CHIPS * conv_cols)

    layers = []
    for l in range(depth):
        layers.append(dict(
            w_in=full["w_in"][l], w_o=full["w_o"][l], w_gate_up=full["w_gate_up"][l], w_down=full["w_down"][l],
            w_conv=conv_full[l], w_pool=w_pool[l], pool_scale=pool_scale[l][None], sgu_ln_g=sgu_ln_g[l][None],
            w_spatial=w_spatial[l], b_spatial=b_spatial[l][:, :, None], ln1_g=ln1_g[l][None], ln1_b=ln1_b[l][None],
            ln2_g=ln2_g[l][None], ln2_b=ln2_b[l][None]))

    sq, grad_x, grads = _local_step(x[0], loss_target[0], layers)
    loss = lax.psum(0.5 / D_MODEL * jnp.sum(sq), ("x", "y", "c"))

    small_shapes = [(depth,) + grads[0][n].shape for n in SMALL]
    small_part = _pack_small([jnp.stack([g[n] for g in grads]) for n in SMALL])
    parts = [_to_blocks(jnp.stack([g[n] for g in grads]), n) for n in BIG]
    parts.append(small_part.reshape(1, N_CHIPS, SMALL_ROWS // N_CHIPS, LANES))
    summed = _reduce_to_owners(parts)
    small_sum = _gather_shards([summed[4]])[0].reshape(SMALL_ROWS, LANES)
    grad = dict(zip(BIG, summed[:4]))
    grad.update(zip(SMALL, _unpack_small(small_sum, small_shapes)))
    grad["pool_scale"] = grad["pool_scale"].reshape(pool_scale.shape)
    grad["sgu_ln_g"] = grad["sgu_ln_g"].reshape(sgu_ln_g.shape)
    for n in ("ln1_g", "ln1_b", "ln2_g", "ln2_b"):
        grad[n] = grad[n].reshape(ln1_g.shape)
    grad["w_conv"] = lax.dynamic_slice_in_dim(grad["w_conv"], chip * conv_cols, conv_cols, axis=2)

    delta, new_m, new_v = {}, {}, {}
    for n in BIG + ("w_conv",):
        delta[n], new_m[n], new_v[n] = _adamw(weights[n], grad[n], m_in[n], v_in[n])
    rest = [n for n in SMALL if n != "w_conv"]
    rest_shapes = [weights[n].shape for n in rest]
    packed = [_pack_small([src[n] for n in rest]) for src in (weights, grad, m_in, v_in)]
    for dst, res in zip((delta, new_m, new_v), _adamw(*packed)):
        dst.update(zip(rest, _unpack_small(res, rest_shapes)))

    return (loss, grad_x[None], *[grad[n] for n in WEIGHTS], *[delta[n] for n in WEIGHTS],
            *[new_m[n] for n in WEIGHTS], *[new_v[n] for n in WEIGHTS])
```

```python
import functools

import jax
import jax.numpy as jnp
from jax import lax
from jax.experimental import pallas as pl
from jax.experimental.pallas import tpu as pltpu

f32 = jnp.float32
bf16 = jnp.bfloat16

D_MODEL = 1024
DEPTH = 4
CONV_W = 384
POOL_W = 256
SGU_W = 384
IN_W = 3 * CONV_W + POOL_W + 2 * SGU_W
D_FF = 2816
CHUNK = 128
HEAD = 64
POOL_WINDOWS = (2, 4, 8, 16)
ALPHA = float((2 * DEPTH) ** 0.25)
LN_EPS = 1e-5
ADAM_LR = 0.001
ADAM_B1 = 0.9
ADAM_B2 = 0.999
ADAM_EPS = 1e-08
ADAM_WD = 0.01
ADAM_STEP = 10

LANES = 128
TOKEN_TILE = 256
N_CHIPS = 4
VMEM_LIMIT = 56 * 1024 * 1024

BLK_XA, BLK_GB, BLK_GC, BLK_P, BLK_U, BLK_V = 0, 3, 6, 9, 11, 14

MESH = pl.DeviceIdType.MESH


def _params(sem=None):
    return pltpu.CompilerParams(dimension_semantics=sem, vmem_limit_bytes=VMEM_LIMIT)


def _rows(width, tile=TOKEN_TILE):
    return pl.BlockSpec((tile, width), lambda i: (i, 0))


def _resident(shape):
    zeros = (0,) * len(shape)
    return pl.BlockSpec(shape, lambda *_: zeros, pipeline_mode=pl.Buffered(1))


def _nt(a, b):
    return lax.dot_general(a, b, (((1,), (1,)), ((), ())), preferred_element_type=f32)


def _tn(a, b):
    return lax.dot_general(a, b, (((0,), (0,)), ((), ())), preferred_element_type=f32)


def _mm(a, b):
    return jnp.dot(a, b, preferred_element_type=f32)


def _norm_fwd(z):
    mu = jnp.mean(z, axis=-1, keepdims=True)
    zc = z - mu
    var = jnp.mean(zc * zc, axis=-1, keepdims=True)
    rstd = lax.rsqrt(var + LN_EPS)
    return zc * rstd, rstd


def _norm_bwd(dxhat, xhat, rstd):
    m1 = jnp.mean(dxhat, axis=-1, keepdims=True)
    m2 = jnp.mean(dxhat * xhat, axis=-1, keepdims=True)
    return rstd * (dxhat - m1 - xhat * m2)


def _proj(x, w_in_b, after):
    s = x.shape[0]

    def body(x_ref, w_ref, after_ref, p_ref, xb_ref):
        xb = x_ref[...].astype(bf16)
        xb_ref[...] = xb
        p_ref[...] = _mm(xb, w_ref[...])

    return pl.pallas_call(
        body, grid=(s // TOKEN_TILE,),
        in_specs=[_rows(D_MODEL), _resident((D_MODEL, IN_W)), pl.BlockSpec(memory_space=pl.ANY)],
        out_specs=[_rows(IN_W), _rows(D_MODEL)],
        out_shape=[jax.ShapeDtypeStruct((s, IN_W), f32), jax.ShapeDtypeStruct((s, D_MODEL), bf16)],
        name="proj", compiler_params=_params(("arbitrary",)))(x, w_in_b, after)


def _wo_ln1(mixcat, x, w_o_b, g, b):
    s = x.shape[0]

    def body(m_ref, x_ref, w_ref, g_ref, b_ref, xhat_ref, rstd_ref, hb_ref):
        z = ALPHA * x_ref[...] + _mm(m_ref[...], w_ref[...])
        xhat, rstd = _norm_fwd(z)
        xhat_ref[...] = xhat
        rstd_ref[...] = rstd
        hb_ref[...] = (xhat * g_ref[...] + b_ref[...]).astype(bf16)

    return pl.pallas_call(
        body, grid=(s // TOKEN_TILE,),
        in_specs=[_rows(D_MODEL), _rows(D_MODEL), _resident((D_MODEL, D_MODEL)), _resident((1, D_MODEL)), _resident((1, D_MODEL))],
        out_specs=[_rows(D_MODEL), _rows(1), _rows(D_MODEL)],
        out_shape=[jax.ShapeDtypeStruct((s, D_MODEL), f32), jax.ShapeDtypeStruct((s, 1), f32),
                   jax.ShapeDtypeStruct((s, D_MODEL), bf16)],
        name="wo_ln1", compiler_params=_params(("arbitrary",)))(mixcat, x, w_o_b, g, b)


def _mlp_fwd(xhat1, g1, b1, w_gu_b, w_down_b, g2, b2):
    s = xhat1.shape[0]

    def body(xh_ref, g1_ref, b1_ref, wgu_ref, wd_ref, g2_ref, b2_ref, gu_ref, xhat2_ref, rstd2_ref, y_ref):
        h = xh_ref[...] * g1_ref[...] + b1_ref[...]
        gu = _mm(h.astype(bf16), wgu_ref[...])
        gu_ref[...] = gu
        gate = gu[:, :D_FF]
        act = gate * jax.nn.sigmoid(gate) * gu[:, D_FF:]
        z = ALPHA * h + _mm(act.astype(bf16), wd_ref[...])
        xhat2, rstd2 = _norm_fwd(z)
        xhat2_ref[...] = xhat2
        rstd2_ref[...] = rstd2
        y_ref[...] = xhat2 * g2_ref[...] + b2_ref[...]

    vec = _resident((1, D_MODEL))
    return pl.pallas_call(
        body, grid=(s // TOKEN_TILE,),
        in_specs=[_rows(D_MODEL), vec, vec, _resident((D_MODEL, 2 * D_FF)), _resident((D_FF, D_MODEL)), vec, vec],
        out_specs=[_rows(2 * D_FF), _rows(D_MODEL), _rows(1), _rows(D_MODEL)],
        out_shape=[jax.ShapeDtypeStruct((s, 2 * D_FF), f32), jax.ShapeDtypeStruct((s, D_MODEL), f32),
                   jax.ShapeDtypeStruct((s, 1), f32), jax.ShapeDtypeStruct((s, D_MODEL), f32)],
        name="mlp_fwd", compiler_params=_params(("arbitrary",)))(xhat1, g1, b1, w_gu_b, w_down_b, g2, b2)


def _loss_head(y, target):
    s = y.shape[0]

    def body(y_ref, t_ref, dy_ref, sq_ref):
        @pl.when(pl.program_id(0) == 0)
        def _():
            sq_ref[...] = jnp.zeros_like(sq_ref)

        e = y_ref[...] - t_ref[...]
        dy_ref[...] = e * (1.0 / D_MODEL)
        sq_ref[...] += jnp.sum(e * e, axis=0, keepdims=True)

    return pl.pallas_call(
        body, grid=(s // TOKEN_TILE,),
        in_specs=[_rows(D_MODEL), _rows(D_MODEL)],
        out_specs=[_rows(D_MODEL), pl.BlockSpec((1, D_MODEL), lambda i: (0, 0))],
        out_shape=[jax.ShapeDtypeStruct((s, D_MODEL), f32), jax.ShapeDtypeStruct((1, D_MODEL), f32)],
        name="loss_head", compiler_params=_params(("arbitrary",)))(y, target)


def _mlp_bwd(dy, xhat2, rstd2, g2, gu, w_gu_b, w_down_b, after):
    s = dy.shape[0]

    def body(dy_ref, xh_ref, rs_ref, g2_ref, gu_ref, wgu_ref, wd_ref, after_ref, dz_ref, act_ref, dgu_ref, dh_ref, gg_ref, gb_ref):
        @pl.when(pl.program_id(0) == 0)
        def _():
            gg_ref[...] = jnp.zeros_like(gg_ref)
            gb_ref[...] = jnp.zeros_like(gb_ref)

        dy_t = dy_ref[...]
        xhat = xh_ref[...]
        gg_ref[...] += jnp.sum(dy_t * xhat, axis=0, keepdims=True)
        gb_ref[...] += jnp.sum(dy_t, axis=0, keepdims=True)
        dz = _norm_bwd(dy_t * g2_ref[...], xhat, rs_ref[...])
        dzb = dz.astype(bf16)
        dz_ref[...] = dzb
        dact = _nt(dzb, wd_ref[...])
        gate = gu_ref[:, :D_FF]
        up = gu_ref[:, D_FF:]
        sg = jax.nn.sigmoid(gate)
        silu = gate * sg
        act_ref[...] = (silu * up).astype(bf16)
        dgu_ref[:, :D_FF] = (dact * up * (sg * (1.0 + gate * (1.0 - sg)))).astype(bf16)
        dgu_ref[:, D_FF:] = (dact * silu).astype(bf16)
        dh_ref[...] = ALPHA * dz + _nt(dgu_ref[...], wgu_ref[...])

    vec_out = pl.BlockSpec((1, D_MODEL), lambda i: (0, 0))
    return pl.pallas_call(
        body, grid=(s // TOKEN_TILE,),
        in_specs=[_rows(D_MODEL), _rows(D_MODEL), _rows(1), _resident((1, D_MODEL)), _rows(2 * D_FF),
                  _resident((D_MODEL, 2 * D_FF)), _resident((D_FF, D_MODEL)), pl.BlockSpec(memory_space=pl.ANY)],
        out_specs=[_rows(D_MODEL), _rows(D_FF), _rows(2 * D_FF), _rows(D_MODEL), vec_out, vec_out],
        out_shape=[jax.ShapeDtypeStruct((s, D_MODEL), bf16), jax.ShapeDtypeStruct((s, D_FF), bf16),
                   jax.ShapeDtypeStruct((s, 2 * D_FF), bf16), jax.ShapeDtypeStruct((s, D_MODEL), f32),
                   jax.ShapeDtypeStruct((1, D_MODEL), f32), jax.ShapeDtypeStruct((1, D_MODEL), f32)],
        name="mlp_bwd", compiler_params=_params(("arbitrary",)))(dy, xhat2, rstd2, g2, gu, w_gu_b, w_down_b, after)


def _ln1_wo_bwd(dh, xhat1, rstd1, g1, w_o_b):
    s = dh.shape[0]

    def body(dh_ref, xh_ref, rs_ref, g1_ref, w_ref, dz_ref, dzb_ref, dm_ref, gg_ref, gb_ref):
        @pl.when(pl.program_id(0) == 0)
        def _():
            gg_ref[...] = jnp.zeros_like(gg_ref)
            gb_ref[...] = jnp.zeros_like(gb_ref)

        dh_t = dh_ref[...]
        xhat = xh_ref[...]
        gg_ref[...] += jnp.sum(dh_t * xhat, axis=0, keepdims=True)
        gb_ref[...] += jnp.sum(dh_t, axis=0, keepdims=True)
        dz = _norm_bwd(dh_t * g1_ref[...], xhat, rs_ref[...])
        dz_ref[...] = dz
        dzb = dz.astype(bf16)
        dzb_ref[...] = dzb
        dm_ref[...] = _nt(dzb, w_ref[...])

    vec_out = pl.BlockSpec((1, D_MODEL), lambda i: (0, 0))
    return pl.pallas_call(
        body, grid=(s // TOKEN_TILE,),
        in_specs=[_rows(D_MODEL), _rows(D_MODEL), _rows(1), _resident((1, D_MODEL)), _resident((D_MODEL, D_MODEL))],
        out_specs=[_rows(D_MODEL), _rows(D_MODEL), _rows(D_MODEL), vec_out, vec_out],
        out_shape=[jax.ShapeDtypeStruct((s, D_MODEL), f32), jax.ShapeDtypeStruct((s, D_MODEL), bf16),
                   jax.ShapeDtypeStruct((s, D_MODEL), f32), jax.ShapeDtypeStruct((1, D_MODEL), f32),
                   jax.ShapeDtypeStruct((1, D_MODEL), f32)],
        name="ln1_wo_bwd", compiler_params=_params(("arbitrary",)))(dh, xhat1, rstd1, g1, w_o_b)


def _dx(dz1, dproj, w_in_b):
    s = dz1.shape[0]

    def body(dz_ref, dp_ref, w_ref, dx_ref):
        dx_ref[...] = ALPHA * dz_ref[...] + _nt(dp_ref[...], w_ref[...])

    return pl.pallas_call(
        body, grid=(s // TOKEN_TILE,),
        in_specs=[_rows(D_MODEL), _rows(IN_W), _resident((D_MODEL, IN_W))],
        out_specs=_rows(D_MODEL),
        out_shape=jax.ShapeDtypeStruct((s, D_MODEL), f32),
        name="dx", compiler_params=_params(("arbitrary",)))(dz1, dproj, w_in_b)


def _weight_grad(a, b, bm, bn):
    s, m = a.shape
    n = b.shape[1]

    def body(a_ref, b_ref, o_ref):
        o_ref[...] = _tn(a_ref[...], b_ref[...]).astype(bf16)

    return pl.pallas_call(
        body, grid=(m // bm, n // bn),
        in_specs=[pl.BlockSpec((s, bm), lambda i, j: (0, i)), pl.BlockSpec((s, bn), lambda i, j: (0, j))],
        out_specs=pl.BlockSpec((bm, bn), lambda i, j: (i, j)),
        out_shape=jax.ShapeDtypeStruct((m, n), bf16),
        name="weight_grad", compiler_params=_params(("arbitrary", "arbitrary")))(a, b)


def _shift_down(a, k):
    row = lax.broadcasted_iota(jnp.int32, a.shape, 0)
    return jnp.where(row >= k, pltpu.roll(a, k, 0), 0.0)


def _shift_up(a, k):
    n = a.shape[0]
    row = lax.broadcasted_iota(jnp.int32, a.shape, 0)
    return jnp.where(row < n - k, pltpu.roll(a, n - k, 0), 0.0)


def _slab(s, block):
    return pl.BlockSpec((s, LANES), lambda k: (0, block + k))


def _conv_y(z, w):
    return w[0:1, :] * _shift_down(z, 2) + w[1:2, :] * _shift_down(z, 1) + w[2:3, :] * z


def _conv_fwd(proj, w_conv):
    s = proj.shape[0]

    def body(xa_ref, gb_ref, gc_ref, w_ref, o_ref):
        z = gc_ref[...] * xa_ref[...]
        o_ref[...] = (gb_ref[...] * _conv_y(z, w_ref[...])).astype(bf16)

    return pl.pallas_call(
        body, grid=(CONV_W // LANES,),
        in_specs=[_slab(s, BLK_XA), _slab(s, BLK_GB), _slab(s, BLK_GC), pl.BlockSpec((3, LANES), lambda k: (0, k))],
        out_specs=_slab(s, 0),
        out_shape=jax.ShapeDtypeStruct((s, CONV_W), bf16),
        name="conv_fwd", compiler_params=_params(("arbitrary",)))(proj, proj, proj, w_conv)


def _conv_bwd(proj, dmix, w_conv):
    s = proj.shape[0]

    def body(xa_ref, gb_ref, gc_ref, dy_ref, w_ref, dxa_ref, dgb_ref, dgc_ref, dw_ref):
        xa = xa_ref[...]
        gc = gc_ref[...]
        w = w_ref[...]
        z = gc * xa
        dya = dy_ref[...]
        dgb_ref[...] = (dya * _conv_y(z, w)).astype(bf16)
        dy = dya * gb_ref[...]
        dz = w[2:3, :] * dy + w[1:2, :] * _shift_up(dy, 1) + w[0:1, :] * _shift_up(dy, 2)
        dxa_ref[...] = (dz * gc).astype(bf16)
        dgc_ref[...] = (dz * xa).astype(bf16)
        dw_ref[0:1, :] = jnp.sum(dy * _shift_down(z, 2), axis=0, keepdims=True)
        dw_ref[1:2, :] = jnp.sum(dy * _shift_down(z, 1), axis=0, keepdims=True)
        dw_ref[2:3, :] = jnp.sum(dy * z, axis=0, keepdims=True)

    out = jax.ShapeDtypeStruct((s, CONV_W), bf16)
    return pl.pallas_call(
        body, grid=(CONV_W // LANES,),
        in_specs=[_slab(s, BLK_XA), _slab(s, BLK_GB), _slab(s, BLK_GC), _slab(s, 0), pl.BlockSpec((3, LANES), lambda k: (0, k))],
        out_specs=[_slab(s, 0), _slab(s, 0), _slab(s, 0), pl.BlockSpec((3, LANES), lambda k: (0, k))],
        out_shape=[out, out, out, jax.ShapeDtypeStruct((3, CONV_W), f32)],
        name="conv_bwd", compiler_params=_params(("arbitrary",)))(proj, proj, proj, dmix, w_conv)


def _pool_window(k):
    lane = lax.broadcasted_iota(jnp.int32, (1, LANES), 1)
    low = lane < HEAD
    first = k == 0
    wlen = jnp.where(low, jnp.where(first, POOL_WINDOWS[0], POOL_WINDOWS[2]), jnp.where(first, POOL_WINDOWS[1], POOL_WINDOWS[3]))
    return wlen, low, first


def _pool_diff(p, k):
    wlen, low, first = _pool_window(k)
    s2 = p + _shift_down(p, 1)
    s4 = s2 + _shift_down(s2, 2)
    s8 = s4 + _shift_down(s4, 4)
    s16 = s8 + _shift_down(s8, 8)
    win = jnp.where(low, jnp.where(first, s2, s8), jnp.where(first, s4, s16))
    row = lax.broadcasted_iota(jnp.int32, p.shape, 0)
    count = jnp.minimum(row + 1, wlen).astype(f32)
    return win / count - p, count


def _pool_weight(w_ref):
    zero = jnp.zeros((HEAD, HEAD), f32)
    top = jnp.concatenate([w_ref[0], zero], axis=1)
    bottom = jnp.concatenate([zero, w_ref[1]], axis=1)
    return jnp.concatenate([top, bottom], axis=0).astype(bf16)


def _pool_fwd(proj, w_pool, pool_scale):
    s = proj.shape[0]

    def body(p_ref, w_ref, sc_ref, o_ref):
        d, _ = _pool_diff(p_ref[...], pl.program_id(0))
        o_ref[...] = (_mm(d.astype(bf16), _pool_weight(w_ref)) * sc_ref[...]).astype(bf16)

    return pl.pallas_call(
        body, grid=(POOL_W // LANES,),
        in_specs=[_slab(s, BLK_P), pl.BlockSpec((2, HEAD, HEAD), lambda k: (k, 0, 0)), pl.BlockSpec((1, LANES), lambda k: (0, k))],
        out_specs=_slab(s, 0),
        out_shape=jax.ShapeDtypeStruct((s, POOL_W), bf16),
        name="pool_fwd", compiler_params=_params(("arbitrary",)))(proj, w_pool, pool_scale)


def _pool_bwd(proj, dmix, w_pool, pool_scale):
    s = proj.shape[0]

    def body(p_ref, dy_ref, w_ref, sc_ref, dp_ref, dw_ref, dsc_ref):
        k = pl.program_id(0)
        d, count = _pool_diff(p_ref[...], k)
        wbd = _pool_weight(w_ref)
        db = d.astype(bf16)
        dyb = dy_ref[...]
        dsc_ref[...] = jnp.sum(dyb * _mm(db, wbd), axis=0, keepdims=True)
        dpre = (dyb * sc_ref[...]).astype(bf16)
        dwbd = _tn(db, dpre)
        dw_ref[0] = dwbd[:HEAD, :HEAD]
        dw_ref[1] = dwbd[HEAD:, HEAD:]
        dd = _nt(dpre, wbd)
        e = dd / count
        wlen, low, first = _pool_window(k)
        a2 = e + _shift_up(e, 1)
        a4 = a2 + _shift_up(a2, 2)
        a8 = a4 + _shift_up(a4, 4)
        a16 = a8 + _shift_up(a8, 8)
        back = jnp.where(low, jnp.where(first, a2, a8), jnp.where(first, a4, a16))
        dp_ref[...] = (back - dd).astype(bf16)

    return pl.pallas_call(
        body, grid=(POOL_W // LANES,),
        in_specs=[_slab(s, BLK_P), _slab(s, CONV_W // LANES), pl.BlockSpec((2, HEAD, HEAD), lambda k: (k, 0, 0)),
                  pl.BlockSpec((1, LANES), lambda k: (0, k))],
        out_specs=[_slab(s, 0), pl.BlockSpec((2, HEAD, HEAD), lambda k: (k, 0, 0)), pl.BlockSpec((1, LANES), lambda k: (0, k))],
        out_shape=[jax.ShapeDtypeStruct((s, POOL_W), bf16), jax.ShapeDtypeStruct((4, HEAD, HEAD), f32),
                   jax.ShapeDtypeStruct((1, POOL_W), f32)],
        name="pool_bwd", compiler_params=_params(("arbitrary",)))(proj, dmix, w_pool, pool_scale)


INV_SQRT2 = 0.7071067811865476
INV_SQRT_2PI = 0.3989422804014327


def _gelu(x):
    return 0.5 * x * (1.0 + lax.erf(x * INV_SQRT2))


def _gelu_grad(x):
    return 0.5 * (1.0 + lax.erf(x * INV_SQRT2)) + x * (INV_SQRT_2PI * jnp.exp(-0.5 * x * x))


def _head_mean(a, low):
    s_low = jnp.sum(jnp.where(low, a, 0.0), axis=-1, keepdims=True)
    s_high = jnp.sum(jnp.where(low, 0.0, a), axis=-1, keepdims=True)
    return jnp.where(low, s_low, s_high) * (1.0 / HEAD)


def _tril():
    r = lax.broadcasted_iota(jnp.int32, (CHUNK, CHUNK), 0)
    c = lax.broadcasted_iota(jnp.int32, (CHUNK, CHUNK), 1)
    return r >= c


def _sgu_chunk(up, vp, g, wm0, wm1, b0, b1, low):
    ug = _gelu(up)
    vg = _gelu(vp)
    vc = vg - _head_mean(vg, low)
    rstd = lax.rsqrt(_head_mean(vc * vc, low) + LN_EPS)
    vn = vc * rstd
    vb = (vn * g).astype(bf16)
    mixed = jnp.where(low, _mm(wm0, vb) + b0, _mm(wm1, vb) + b1)
    return ug, vn, rstd, vb, mixed


def _sgu_specs(s):
    return [_slab(s, BLK_U), _slab(s, BLK_V), pl.BlockSpec((1, LANES), lambda k: (0, k)),
            pl.BlockSpec((2, CHUNK, CHUNK), lambda k: (k, 0, 0)), pl.BlockSpec((2, CHUNK, 1), lambda k: (k, 0, 0))]


def _sgu_fwd(proj, sgu_g, w_spatial, b_spatial3):
    s = proj.shape[0]

    def body(u_ref, v_ref, g_ref, w_ref, b_ref, o_ref):
        low = lax.broadcasted_iota(jnp.int32, (1, LANES), 1) < HEAD
        mask = _tril()
        wm0 = jnp.where(mask, w_ref[0], 0.0).astype(bf16)
        wm1 = jnp.where(mask, w_ref[1], 0.0).astype(bf16)
        g = g_ref[...]
        b0 = b_ref[0]
        b1 = b_ref[1]

        def chunk(n, carry):
            rows = pl.ds(pl.multiple_of(n * CHUNK, CHUNK), CHUNK)
            ug, _, _, _, mixed = _sgu_chunk(u_ref[rows, :], v_ref[rows, :], g, wm0, wm1, b0, b1, low)
            o_ref[rows, :] = (ug * mixed).astype(bf16)
            return carry

        lax.fori_loop(0, s // CHUNK, chunk, 0)

    return pl.pallas_call(
        body, grid=(SGU_W // LANES,),
        in_specs=_sgu_specs(s),
        out_specs=_slab(s, 0),
        out_shape=jax.ShapeDtypeStruct((s, SGU_W), bf16),
        name="sgu_fwd", compiler_params=_params(("arbitrary",)))(proj, proj, sgu_g, w_spatial, b_spatial3)


def _sgu_bwd(proj, dmix, sgu_g, w_spatial, b_spatial3):
    s = proj.shape[0]

    def body(u_ref, v_ref, g_ref, w_ref, b_ref, dy_ref, du_ref, dv_ref, dg_ref, dw_ref, db_ref):
        low = lax.broadcasted_iota(jnp.int32, (1, LANES), 1) < HEAD
        mask = _tril()
        w0 = jnp.where(mask, w_ref[0], 0.0)
        w1 = jnp.where(mask, w_ref[1], 0.0)
        wm0 = w0.astype(bf16)
        wm1 = w1.astype(bf16)
        wt0 = w0.T.astype(bf16)
        wt1 = w1.T.astype(bf16)
        g = g_ref[...]
        b0 = b_ref[0]
        b1 = b_ref[1]
        dg_ref[...] = jnp.zeros_like(dg_ref)
        dw_ref[...] = jnp.zeros_like(dw_ref)
        db_ref[...] = jnp.zeros_like(db_ref)

        def chunk(n, carry):
            rows = pl.ds(pl.multiple_of(n * CHUNK, CHUNK), CHUNK)
            up = u_ref[rows, :]
            vp = v_ref[rows, :]
            ug, vn, rstd, vb, mixed = _sgu_chunk(up, vp, g, wm0, wm1, b0, b1, low)
            dy = dy_ref[rows, :]
            du_ref[rows, :] = (dy * mixed * _gelu_grad(up)).astype(bf16)
            dmix_c = dy * ug
            db_ref[0] += jnp.sum(jnp.where(low, dmix_c, 0.0), axis=-1, keepdims=True)
            db_ref[1] += jnp.sum(jnp.where(low, 0.0, dmix_c), axis=-1, keepdims=True)
            dmb = dmix_c.astype(bf16)
            zero = jnp.zeros_like(dmb)
            dw_ref[0] += _nt(jnp.where(low, dmb, zero), vb)
            dw_ref[1] += _nt(jnp.where(low, zero, dmb), vb)
            dvnorm = jnp.where(low, _mm(wt0, dmb), _mm(wt1, dmb))
            dg_ref[...] += jnp.sum(dvnorm * vn, axis=0, keepdims=True)
            dvn = dvnorm * g
            dvg = rstd * (dvn - _head_mean(dvn, low) - vn * _head_mean(dvn * vn, low))
            dv_ref[rows, :] = (dvg * _gelu_grad(vp)).astype(bf16)
            return carry

        lax.fori_loop(0, s // CHUNK, chunk, 0)
        dw_ref[0] = jnp.where(mask, dw_ref[0], 0.0)
        dw_ref[1] = jnp.where(mask, dw_ref[1], 0.0)

    out = jax.ShapeDtypeStruct((s, SGU_W), bf16)
    return pl.pallas_call(
        body, grid=(SGU_W // LANES,),
        in_specs=_sgu_specs(s) + [_slab(s, (CONV_W + POOL_W) // LANES)],
        out_specs=[_slab(s, 0), _slab(s, 0), pl.BlockSpec((1, LANES), lambda k: (0, k)),
                   pl.BlockSpec((2, CHUNK, CHUNK), lambda k: (k, 0, 0)), pl.BlockSpec((2, CHUNK, 1), lambda k: (k, 0, 0))],
        out_shape=[out, out, jax.ShapeDtypeStruct((1, SGU_W), f32), jax.ShapeDtypeStruct((6, CHUNK, CHUNK), f32),
                   jax.ShapeDtypeStruct((6, CHUNK, 1), f32)],
        name="sgu_bwd", compiler_params=_params(("arbitrary",)))(proj, proj, sgu_g, w_spatial, b_spatial3, dmix)


def _layer_fwd(x, w, after):
    proj, xb = _proj(x, w["w_in"], after)
    ya = _conv_fwd(proj, w["w_conv"])
    yb = _pool_fwd(proj, w["w_pool"], w["pool_scale"])
    yc = _sgu_fwd(proj, w["sgu_ln_g"], w["w_spatial"], w["b_spatial"])
    mixcat = jnp.concatenate([ya, yb, yc], axis=1)
    xhat1, rstd1, hb = _wo_ln1(mixcat, x, w["w_o"], w["ln1_g"], w["ln1_b"])
    gu, xhat2, rstd2, y = _mlp_fwd(xhat1, w["ln1_g"], w["ln1_b"], w["w_gate_up"], w["w_down"], w["ln2_g"], w["ln2_b"])
    saved = dict(proj=proj, xb=xb, mixcat=mixcat, xhat1=xhat1, rstd1=rstd1, hb=hb, gu=gu, xhat2=xhat2, rstd2=rstd2)
    return y, saved


def _layer_bwd(dy, w, sv, after):
    dz2b, actb, dgub, dh, g_ln2_g, g_ln2_b = _mlp_bwd(dy, sv["xhat2"], sv["rstd2"], w["ln2_g"], sv["gu"], w["w_gate_up"],
                                                      w["w_down"], after)
    dz1, dz1b, dmix, g_ln1_g, g_ln1_b = _ln1_wo_bwd(dh, sv["xhat1"], sv["rstd1"], w["ln1_g"], w["w_o"])
    dxa, dgb, dgc, g_conv = _conv_bwd(sv["proj"], dmix, w["w_conv"])
    dp, g_pool, g_pscale = _pool_bwd(sv["proj"], dmix, w["w_pool"], w["pool_scale"])
    du, dv, g_sgu_g, g_spatial, g_bsp = _sgu_bwd(sv["proj"], dmix, w["sgu_ln_g"], w["w_spatial"], w["b_spatial"])
    dproj = jnp.concatenate([dxa, dgb, dgc, dp, du, dv], axis=1)
    dx = _dx(dz1, dproj, w["w_in"])
    grads = dict(
        w_in=_weight_grad(sv["xb"], dproj, 512, IN_W),
        w_o=_weight_grad(sv["mixcat"], dz1b, 512, D_MODEL),
        w_gate_up=_weight_grad(sv["hb"], dgub, 512, D_FF // 2),
        w_down=_weight_grad(actb, dz2b, D_FF // 2, D_MODEL),
        w_conv=g_conv, w_pool=g_pool, pool_scale=g_pscale, sgu_ln_g=g_sgu_g, w_spatial=g_spatial,
        b_spatial=g_bsp.reshape(6, CHUNK), ln1_g=g_ln1_g, ln1_b=g_ln1_b, ln2_g=g_ln2_g, ln2_b=g_ln2_b)
    return dx, grads


def _local_step(x, target, layers):
    saved = []
    for w in layers:
        x, sv = _layer_fwd(x, w, jnp.zeros((8, LANES), f32))
        saved.append(sv)
    dy, sq = _loss_head(x, target)
    grads = [None] * len(layers)
    for l in reversed(range(len(layers))):
        dy, grads[l] = _layer_bwd(dy, layers[l], saved[l], sq)
    return sq, dy, grads


ANY = pl.BlockSpec(memory_space=pl.ANY)


def _place():
    x, y, c = lax.axis_index("x"), lax.axis_index("y"), lax.axis_index("c")
    others = [(1 - x, y), (x, 1 - y), (1 - x, 1 - y)]
    return x, y, c, others


def _chip_index(cx, cy):
    return 2 * cx + cy


def _half(ref_rows, c):
    half = ref_rows // 2
    return pl.ds(pl.multiple_of(c * half, 8), half)


def _remote(src, dst, send_sem, recv_sem, device):
    return pltpu.make_async_remote_copy(src_ref=src, dst_ref=dst, send_sem=send_sem, recv_sem=recv_sem,
                                        device_id=device, device_id_type=MESH)


def _gather_shards(shards):
    n = len(shards)
    base, total = [], 0
    for s in shards:
        base.append(total)
        total += 6 * s.shape[0]

    def body(*refs):
        ins, outs = refs[:n], refs[n:2 * n]
        send, recv = refs[2 * n:]
        x, y, c, others = _place()
        me = _chip_index(x, y)
        sib = (x, y, 1 - c)
        sends = []
        for f in range(n):
            depth, rows = ins[f].shape[0], ins[f].shape[1]
            for l in range(depth):
                for k, (cx, cy) in enumerate(others):
                    sem = base[f] + 6 * l + k
                    cp = _remote(ins[f].at[l, _half(rows, c)], outs[f].at[l, me, _half(rows, c)],
                                 send.at[sem], recv.at[sem], (cx, cy, c))
                    cp.start()
                    sends.append(cp)
        for f in range(n):
            depth, rows = ins[f].shape[0], ins[f].shape[1]
            for l in range(depth):
                for k, (cx, cy) in enumerate(others):
                    sem = base[f] + 6 * l + k
                    landed = outs[f].at[l, _chip_index(cx, cy), _half(rows, c)]
                    _remote(landed, landed, send.at[sem], recv.at[sem], (cx, cy, c)).wait_recv()
                    cp = _remote(landed, landed, send.at[sem + 3], recv.at[sem + 3], sib)
                    cp.start()
                    sends.append(cp)
        for f in range(n):
            depth, rows = ins[f].shape[0], ins[f].shape[1]
            for l in range(depth):
                for k, (cx, cy) in enumerate(others):
                    sem = base[f] + 6 * l + k + 3
                    passed = outs[f].at[l, _chip_index(cx, cy), _half(rows, 1 - c)]
                    _remote(passed, passed, send.at[sem], recv.at[sem], sib).wait_recv()
        for cp in sends:
            cp.wait_send()

    gathered = pl.pallas_call(
        body, in_specs=[ANY] * n, out_specs=[ANY] * n,
        out_shape=[jax.ShapeDtypeStruct((s.shape[0], N_CHIPS) + s.shape[1:], s.dtype) for s in shards],
        scratch_shapes=[pltpu.SemaphoreType.DMA((total,)), pltpu.SemaphoreType.DMA((total,))],
        name="gather_shards")(*shards)
    return [_place_own(g, s) for g, s in zip(gathered, shards)]


def _scalar(value):
    return jnp.reshape(value, (1,)).astype(jnp.int32)


def _place_own(blocks, shard):
    depth, rows, cols = shard.shape

    def body(me_ref, b_ref, s_ref, o_ref):
        o_ref[...] = s_ref[...]

    return pl.pallas_call(
        body,
        grid_spec=pltpu.PrefetchScalarGridSpec(
            num_scalar_prefetch=1, grid=(depth,),
            in_specs=[ANY, pl.BlockSpec((None, rows, cols), lambda l, me: (l, 0, 0))],
            out_specs=pl.BlockSpec((None, None, rows, cols), lambda l, me: (l, me[0], 0, 0))),
        out_shape=jax.ShapeDtypeStruct(blocks.shape, blocks.dtype),
        input_output_aliases={1: 0},
        name="place_own", compiler_params=_params(("arbitrary",)))(
            _scalar(_chip_index(lax.axis_index("x"), lax.axis_index("y"))), blocks, shard)


HBM = pl.BlockSpec(memory_space=pltpu.HBM)
SEM = pl.BlockSpec(memory_space=pltpu.SEMAPHORE)
TOKEN = jax.ShapeDtypeStruct((8, LANES), f32)
SPLIT_COPY = pltpu.CompilerParams(has_side_effects=pltpu.SideEffectType.DATAFLOW_SIDE_EFFECTING)


def _in_hbm(a):
    return pltpu.with_memory_space_constraint(a, pltpu.HBM)


def _gather_start(shards, after):
    n = len(shards)
    lands = [lax.empty((N_CHIPS,) + s.shape, s.dtype) for s in shards]

    def body(*refs):
        ins, lnd = refs[:n], refs[n:2 * n]
        send, recv = refs[2 * n + 1], refs[2 * n + 2]
        token = refs[-1]
        x, y, c, others = _place()
        me = _chip_index(x, y)
        for f in range(n):
            rows = ins[f].shape[0]
            for k, (cx, cy) in enumerate(others):
                _remote(ins[f].at[_half(rows, c)], lnd[f].at[me, _half(rows, c)], send.at[3 * f + k], recv.at[3 * f + k],
                        (cx, cy, c)).start()
        token[...] = jnp.zeros_like(token)

    thru = [pltpu.HBM(a.shape, a.dtype) for a in list(shards) + lands]
    outs = pl.pallas_call(
        body, name="gather_start",
        in_specs=[HBM] * (2 * n) + [ANY],
        out_specs=(SEM, SEM, *[HBM] * (2 * n), pl.BlockSpec(memory_space=pltpu.VMEM)),
        out_shape=(pltpu.SemaphoreType.DMA((3 * n,)), pltpu.SemaphoreType.DMA((3 * n,)), *thru, TOKEN),
        input_output_aliases={i: 2 + i for i in range(2 * n)},
        compiler_params=SPLIT_COPY)(*[_in_hbm(a) for a in list(shards) + lands], after)
    return (outs[0], outs[1], outs[2:2 + n], outs[2 + n:2 + 2 * n]), outs[-1]


def _gather_wait(state, after):
    send_sems, recv_sems, shards, lands = state
    n = len(shards)

    def body(*refs):
        ins, lnd = refs[:n], refs[n:2 * n]
        send, recv = refs[2 * n], refs[2 * n + 1]
        token = refs[-1]
        x, y, c, others = _place()
        for f in range(n):
            rows = ins[f].shape[0]
            for k, (cx, cy) in enumerate(others):
                cp = _remote(ins[f].at[_half(rows, c)], lnd[f].at[_chip_index(cx, cy), _half(rows, c)],
                             send.at[3 * f + k], recv.at[3 * f + k], (cx, cy, c))
                cp.wait_send()
                cp.wait_recv()
        token[...] = jnp.zeros_like(token)

    thru = [pltpu.HBM(a.shape, a.dtype) for a in list(shards) + list(lands)]
    outs = pl.pallas_call(
        body, name="gather_wait",
        in_specs=[HBM] * (2 * n) + [SEM, SEM, ANY],
        out_specs=(*[HBM] * (2 * n), pl.BlockSpec(memory_space=pltpu.VMEM)),
        out_shape=(*thru, TOKEN),
        input_output_aliases={i: i for i in range(2 * n)},
        compiler_params=SPLIT_COPY)(*shards, *lands, send_sems, recv_sems, after)
    return outs[:n], outs[n:2 * n], outs[-1]


def _gather_finish(lands, shards):
    n = len(lands)

    def body(*refs):
        outs = refs[n:2 * n]
        send, recv = refs[2 * n:]
        x, y, c, others = _place()
        sib = (x, y, 1 - c)
        sends = []
        for f in range(n):
            rows = outs[f].shape[1]
            for k, (cx, cy) in enumerate(others):
                landed = outs[f].at[_chip_index(cx, cy), _half(rows, c)]
                cp = _remote(landed, landed, send.at[3 * f + k], recv.at[3 * f + k], sib)
                cp.start()
                sends.append(cp)
        for f in range(n):
            rows = outs[f].shape[1]
            for k, (cx, cy) in enumerate(others):
                passed = outs[f].at[_chip_index(cx, cy), _half(rows, 1 - c)]
                _remote(passed, passed, send.at[3 * f + k], recv.at[3 * f + k], sib).wait_recv()
        for cp in sends:
            cp.wait_send()

    full = pl.pallas_call(
        body, in_specs=[ANY] * n, out_specs=[ANY] * n,
        out_shape=[jax.ShapeDtypeStruct(a.shape, a.dtype) for a in lands],
        input_output_aliases={f: f for f in range(n)},
        scratch_shapes=[pltpu.SemaphoreType.DMA((3 * n,)), pltpu.SemaphoreType.DMA((3 * n,))],
        name="gather_finish")(*lands)

    def place(me_ref, *refs):
        ins, outs = refs[n:2 * n], refs[2 * n:]
        for f in range(n):
            outs[f][...] = ins[f][...]

    return pl.pallas_call(
        place,
        grid_spec=pltpu.PrefetchScalarGridSpec(
            num_scalar_prefetch=1, grid=(1,),
            in_specs=[ANY] * n + [pl.BlockSpec(s.shape, lambda i, me: (0, 0)) for s in shards],
            out_specs=[pl.BlockSpec((None,) + s.shape, lambda i, me: (me[0], 0, 0)) for s in shards]),
        out_shape=[jax.ShapeDtypeStruct(a.shape, a.dtype) for a in full],
        input_output_aliases={1 + f: f for f in range(n)},
        name="place_own_layer", compiler_params=_params(("arbitrary",)))(
            _scalar(_chip_index(lax.axis_index("x"), lax.axis_index("y"))), *full, *shards)


def _scatter_start(sums, after):
    n = len(sums)
    lands = [lax.empty((3,) + s.shape[1:], s.dtype) for s in sums]

    def body(*refs):
        ins, lnd = refs[:n], refs[n:2 * n]
        send, recv = refs[2 * n + 1], refs[2 * n + 2]
        token = refs[-1]
        x, y, c, others = _place()
        for f in range(n):
            for k, (cx, cy) in enumerate(others):
                _remote(ins[f].at[_chip_index(cx, cy)], lnd[f].at[k], send.at[3 * f + k], recv.at[3 * f + k], (cx, cy, c)).start()
        token[...] = jnp.zeros_like(token)

    thru = [pltpu.HBM(a.shape, a.dtype) for a in list(sums) + lands]
    outs = pl.pallas_call(
        body, name="scatter_start",
        in_specs=[HBM] * (2 * n) + [ANY],
        out_specs=(SEM, SEM, *[HBM] * (2 * n), pl.BlockSpec(memory_space=pltpu.VMEM)),
        out_shape=(pltpu.SemaphoreType.DMA((3 * n,)), pltpu.SemaphoreType.DMA((3 * n,)), *thru, TOKEN),
        input_output_aliases={i: 2 + i for i in range(2 * n)},
        compiler_params=SPLIT_COPY)(*[_in_hbm(a) for a in list(sums) + lands], after)
    return (outs[0], outs[1], outs[2:2 + n], outs[2 + n:2 + 2 * n]), outs[-1]


def _scatter_wait(state, after):
    send_sems, recv_sems, sums, lands = state
    n = len(sums)

    def body(*refs):
        ins, lnd = refs[:n], refs[n:2 * n]
        send, recv = refs[2 * n], refs[2 * n + 1]
        token = refs[-1]
        x, y, c, others = _place()
        for f in range(n):
            for k, (cx, cy) in enumerate(others):
                cp = _remote(ins[f].at[_chip_index(cx, cy)], lnd[f].at[k], send.at[3 * f + k], recv.at[3 * f + k], (cx, cy, c))
                cp.wait_send()
                cp.wait_recv()
        token[...] = jnp.zeros_like(token)

    thru = [pltpu.HBM(a.shape, a.dtype) for a in list(sums) + list(lands)]
    outs = pl.pallas_call(
        body, name="scatter_wait",
        in_specs=[HBM] * (2 * n) + [SEM, SEM, ANY],
        out_specs=(*[HBM] * (2 * n), pl.BlockSpec(memory_space=pltpu.VMEM)),
        out_shape=(*thru, TOKEN),
        input_output_aliases={i: i for i in range(2 * n)},
        compiler_params=SPLIT_COPY)(*sums, *lands, send_sems, recv_sems, after)
    return outs[:n], outs[n:2 * n], outs[-1]


def _sibling_split(parts):
    n = len(parts)

    def body(*refs):
        ins, got = refs[:n], refs[n:2 * n]
        send, recv = refs[2 * n:]
        x, y, c, _ = _place()
        sib = (x, y, 1 - c)
        for f in range(n):
            depth, rows = ins[f].shape[0], ins[f].shape[2]
            for l in range(depth):
                for j in range(N_CHIPS):
                    _remote(ins[f].at[l, j, _half(rows, 1 - c)], got[f].at[l, j], send.at[f], recv.at[f], sib).start()
        for f in range(n):
            _remote(got[f], got[f], send.at[f], recv.at[f], sib).wait()

    return pl.pallas_call(
        body, in_specs=[ANY] * n, out_specs=[ANY] * n,
        out_shape=[jax.ShapeDtypeStruct(p.shape[:2] + (p.shape[2] // 2, p.shape[3]), p.dtype) for p in parts],
        scratch_shapes=[pltpu.SemaphoreType.DMA((n,)), pltpu.SemaphoreType.DMA((n,))],
        name="sibling_split")(*parts)


def _chip_scatter(sums):
    n = len(sums)

    def body(*refs):
        ins, outs = refs[:n], refs[n:2 * n]
        send, recv = refs[2 * n:]
        x, y, c, others = _place()
        for f in range(n):
            for l in range(ins[f].shape[0]):
                for k, (cx, cy) in enumerate(others):
                    _remote(ins[f].at[l, _chip_index(cx, cy)], outs[f].at[k, l], send.at[f * 3 + k], recv.at[f * 3 + k],
                            (cx, cy, c)).start()
        for f in range(n):
            for k, (cx, cy) in enumerate(others):
                _remote(outs[f].at[k], outs[f].at[k], send.at[f * 3 + k], recv.at[f * 3 + k], (cx, cy, c)).wait()

    return pl.pallas_call(
        body, in_specs=[ANY] * n, out_specs=[ANY] * n,
        out_shape=[jax.ShapeDtypeStruct((3, s.shape[0]) + s.shape[2:], s.dtype) for s in sums],
        scratch_shapes=[pltpu.SemaphoreType.DMA((3 * n,)), pltpu.SemaphoreType.DMA((3 * n,))],
        name="chip_scatter")(*sums)


def _sibling_join(sums):
    n = len(sums)

    def body(*refs):
        ins, outs = refs[:n], refs[n:2 * n]
        send, recv = refs[2 * n:]
        x, y, c, _ = _place()
        sib = (x, y, 1 - c)
        for f in range(n):
            depth, half_rows = outs[f].shape[0], outs[f].shape[1] // 2
            pieces = 4 if half_rows % 64 == 0 else 1
            step = half_rows // pieces
            for l in range(depth):
                for p in range(pieces):
                    mine = outs[f].at[l, pl.ds(pl.multiple_of(c * half_rows + p * step, 8), step)]
                    _remote(mine, mine, send.at[f], recv.at[f], sib).start()
        for f in range(n):
            half = outs[f].at[:, pl.ds(0, outs[f].shape[1] // 2)]
            _remote(half, half, send.at[f], recv.at[f], sib).wait()

    return pl.pallas_call(
        body, in_specs=[ANY] * n, out_specs=[ANY] * n,
        out_shape=[jax.ShapeDtypeStruct(s.shape, s.dtype) for s in sums],
        input_output_aliases={f: f for f in range(n)},
        scratch_shapes=[pltpu.SemaphoreType.DMA((n,)), pltpu.SemaphoreType.DMA((n,))],
        name="sibling_join")(*sums)


ELEMENTWISE_BLOCK_BYTES = 1 << 20


def _row_tile(rows, cols):
    best = None
    for tile in range(8, rows + 1, 8):
        if rows % tile == 0 and tile * cols * 4 <= ELEMENTWISE_BLOCK_BYTES:
            best = tile
    return best or rows


def _add_pair(part, got):
    depth, chips, rows, cols = part.shape
    half = rows // 2

    def body(c_ref, a_ref, b_ref, o_ref):
        o_ref[...] = (a_ref[...].astype(f32) + b_ref[...].astype(f32)).astype(o_ref.dtype)

    return pl.pallas_call(
        body,
        grid_spec=pltpu.PrefetchScalarGridSpec(
            num_scalar_prefetch=1, grid=(depth, chips),
            in_specs=[pl.BlockSpec((None, None, None, half, cols), lambda l, j, c: (l, j, c[0], 0, 0)),
                      pl.BlockSpec((None, None, half, cols), lambda l, j, c: (l, j, 0, 0))],
            out_specs=pl.BlockSpec((None, None, half, cols), lambda l, j, c: (l, j, 0, 0))),
        out_shape=jax.ShapeDtypeStruct(got.shape, part.dtype),
        name="add_pair", compiler_params=_params(("arbitrary", "arbitrary")))(
            _scalar(lax.axis_index("c")), part.reshape(depth, chips, 2, half, cols), got)


def _add_slots(chip_sums, slots):
    depth, _, half, cols = chip_sums.shape

    def body(at_ref, own_ref, s_ref, o_ref):
        acc = own_ref[...].astype(f32)
        for k in range(3):
            acc = acc + s_ref[k].astype(f32)
        o_ref[...] = acc

    at = jnp.concatenate([_scalar(_chip_index(lax.axis_index("x"), lax.axis_index("y"))), _scalar(lax.axis_index("c"))])
    out = pl.pallas_call(
        body,
        grid_spec=pltpu.PrefetchScalarGridSpec(
            num_scalar_prefetch=1, grid=(depth,),
            in_specs=[pl.BlockSpec((None, None, half, cols), lambda l, at: (l, at[0], 0, 0)),
                      pl.BlockSpec((3, None, half, cols), lambda l, at: (0, l, 0, 0))],
            out_specs=pl.BlockSpec((None, None, half, cols), lambda l, at: (l, at[1], 0, 0))),
        out_shape=jax.ShapeDtypeStruct((depth, 2, half, cols), f32),
        name="add_slots", compiler_params=_params(("arbitrary",)))(at, chip_sums, slots)
    return out.reshape(depth, 2 * half, cols)


def _adamw_math(w, grad, m, v):
    nm = ADAM_B1 * m + (1.0 - ADAM_B1) * grad
    nv = ADAM_B2 * v + (1.0 - ADAM_B2) * (grad * grad)
    m_hat = nm / (1.0 - ADAM_B1 ** ADAM_STEP)
    v_hat = nv / (1.0 - ADAM_B2 ** ADAM_STEP)
    return nm, nv, -ADAM_LR * (m_hat / (jnp.sqrt(v_hat) + ADAM_EPS) + ADAM_WD * w)


def _adamw(w, g, m, v):
    shape = w.shape
    flat = [a.reshape(-1, shape[-1]) for a in (w, g, m, v)]
    tile = _row_tile(flat[0].shape[0], shape[-1])

    def body(w_ref, g_ref, m_ref, v_ref, d_ref, nm_ref, nv_ref):
        nm, nv, step = _adamw_math(w_ref[...], g_ref[...], m_ref[...], v_ref[...])
        d_ref[...] = step
        nm_ref[...] = nm
        nv_ref[...] = nv

    spec = _rows(shape[-1], tile)
    out = jax.ShapeDtypeStruct(flat[0].shape, f32)
    res = pl.pallas_call(
        body, grid=(flat[0].shape[0] // tile,),
        in_specs=[spec] * 4, out_specs=[spec] * 3, out_shape=[out] * 3,
        name="adamw", compiler_params=_params(("arbitrary",)))(*flat)
    return [r.reshape(shape) for r in res]


def _adamw_layer(l, w, m, v, g, outs):
    depth, rows, cols = w.shape
    tile = _row_tile(rows, cols)

    def body(w_ref, m_ref, v_ref, g_ref, *refs):
        go_ref, d_ref, nm_ref, nv_ref = refs[4:]
        grad = g_ref[...]
        nm, nv, step = _adamw_math(w_ref[...], grad, m_ref[...], v_ref[...])
        go_ref[...] = grad
        d_ref[...] = step
        nm_ref[...] = nm
        nv_ref[...] = nv

    layer = pl.BlockSpec((None, tile, cols), lambda i: (l, i, 0))
    return pl.pallas_call(
        body, grid=(rows // tile,),
        in_specs=[layer] * 3 + [_rows(cols, tile)] + [ANY] * 4, out_specs=[layer] * 4,
        out_shape=[jax.ShapeDtypeStruct(w.shape, f32)] * 4,
        input_output_aliases={4 + k: k for k in range(4)},
        name="adamw_layer", compiler_params=_params(("arbitrary",)))(w, m, v, g, *outs)


def _reduce_to_owners(parts):
    got = _sibling_split(parts)
    chip_sums = [_add_pair(p, g) for p, g in zip(parts, got)]
    slots = _chip_scatter(chip_sums)
    return _sibling_join([_add_slots(cs, s) for cs, s in zip(chip_sums, slots)])


SMALL = ("w_conv", "w_pool", "pool_scale", "sgu_ln_g", "w_spatial", "b_spatial", "ln1_g", "ln1_b", "ln2_g", "ln2_b")
WEIGHTS = ("w_in", "w_conv", "w_pool", "pool_scale", "sgu_ln_g", "w_spatial", "b_spatial", "w_o", "ln1_g", "ln1_b",
           "w_gate_up", "w_down", "ln2_g", "ln2_b")
BIG = ("w_in", "w_o", "w_gate_up", "w_down")
COLUMN_SHARDED = ("w_in", "w_gate_up")
SMALL_ROWS = 4096


def _pack_small(arrays):
    flat = jnp.concatenate([a.reshape(-1) for a in arrays])
    return jnp.pad(flat, (0, SMALL_ROWS * LANES - flat.shape[0])).reshape(SMALL_ROWS, LANES)


def _unpack_small(packed, shapes):
    flat = packed.reshape(-1)
    out, at = [], 0
    for shp in shapes:
        size = 1
        for d in shp:
            size *= d
        out.append(flat[at:at + size].reshape(shp))
        at += size
    return out


def _to_blocks(full, name):
    depth, rows, cols = full.shape
    if name in COLUMN_SHARDED:
        return full.reshape(depth, rows, N_CHIPS, cols // N_CHIPS).transpose(0, 2, 1, 3)
    return full.reshape(depth, N_CHIPS, rows // N_CHIPS, cols)


def _from_blocks(blocks, name):
    depth, _, rows, cols = blocks.shape
    if name in COLUMN_SHARDED:
        return blocks.transpose(0, 2, 1, 3).reshape(depth, rows, N_CHIPS * cols)
    return blocks.reshape(depth, N_CHIPS * rows, cols)


def kernel(x, w_in, w_conv, w_pool, pool_scale, sgu_ln_g, w_spatial, b_spatial, w_o, ln1_g, ln1_b, w_gate_up, w_down, ln2_g, ln2_b, loss_target, m_w_in, m_w_conv, m_w_pool, m_pool_scale, m_sgu_ln_g, m_w_spatial, m_b_spatial, m_w_o, m_ln1_g, m_ln1_b, m_w_gate_up, m_w_down, m_ln2_g, m_ln2_b, v_w_in, v_w_conv, v_w_pool, v_pool_scale, v_sgu_ln_g, v_w_spatial, v_b_spatial, v_w_o, v_ln1_g, v_ln1_b, v_w_gate_up, v_w_down, v_ln2_g, v_ln2_b):
    weights = dict(w_in=w_in, w_conv=w_conv, w_pool=w_pool, pool_scale=pool_scale, sgu_ln_g=sgu_ln_g, w_spatial=w_spatial,
                   b_spatial=b_spatial, w_o=w_o, ln1_g=ln1_g, ln1_b=ln1_b, w_gate_up=w_gate_up, w_down=w_down, ln2_g=ln2_g, ln2_b=ln2_b)
    m_in = dict(w_in=m_w_in, w_conv=m_w_conv, w_pool=m_w_pool, pool_scale=m_pool_scale, sgu_ln_g=m_sgu_ln_g, w_spatial=m_w_spatial,
                b_spatial=m_b_spatial, w_o=m_w_o, ln1_g=m_ln1_g, ln1_b=m_ln1_b, w_gate_up=m_w_gate_up, w_down=m_w_down,
                ln2_g=m_ln2_g, ln2_b=m_ln2_b)
    v_in = dict(w_in=v_w_in, w_conv=v_w_conv, w_pool=v_w_pool, pool_scale=v_pool_scale, sgu_ln_g=v_sgu_ln_g, w_spatial=v_w_spatial,
                b_spatial=v_b_spatial, w_o=v_w_o, ln1_g=v_ln1_g, ln1_b=v_ln1_b, w_gate_up=v_w_gate_up, w_down=v_w_down,
                ln2_g=v_ln2_g, ln2_b=v_ln2_b)
    depth = w_in.shape[0]
    conv_cols = w_conv.shape[2]
    chip = _chip_index(lax.axis_index("x"), lax.axis_index("y"))

    conv_flat = jnp.pad(w_conv.reshape(-1), (0, 16 * LANES - w_conv.size)).reshape(1, 16, LANES)
    conv_full = _gather_shards([conv_flat])[0].reshape(N_CHIPS, 16 * LANES)[:, :w_conv.size].reshape(N_CHIPS, depth, 3, conv_cols)
    conv_full = conv_full.transpose(1, 2, 0, 3).reshape(depth, 3, N_CHIPS * conv_cols)

    def send_layer(l, after):
        return _gather_start([weights[n][l].astype(bf16) for n in BIG], after)

    def layer_weights(l, lands, shards):
        blocks = _gather_finish(lands, shards)
        mats = {n: _from_blocks(b[None], n)[0] for n, b in zip(BIG, blocks)}
        return dict(
            mats, w_conv=conv_full[l], w_pool=w_pool[l], pool_scale=pool_scale[l][None], sgu_ln_g=sgu_ln_g[l][None],
            w_spatial=w_spatial[l], b_spatial=b_spatial[l][:, :, None], ln1_g=ln1_g[l][None], ln1_b=ln1_b[l][None],
            ln2_g=ln2_g[l][None], ln2_b=ln2_b[l][None])

    act = x[0]
    layers, saved = [], []
    flight, token = send_layer(0, conv_full)
    for l in range(depth):
        shards, lands, token = _gather_wait(flight, act)
        if l + 1 < depth:
            flight, token = send_layer(l + 1, token)
        layers.append(layer_weights(l, lands, shards))
        act, sv = _layer_fwd(act, layers[l], token)
        saved.append(sv)

    def reduce_start(l, after):
        parts = [_to_blocks(grads[l][n][None], n) for n in BIG]
        got = _sibling_split(parts)
        return _scatter_start([_add_pair(p, g)[0] for p, g in zip(parts, got)], after)

    def reduce_finish(l, flight, after, outs):
        sums, slots, _ = _scatter_wait(flight, after)
        summed = _sibling_join([_add_slots(cs[None], s[:, None]) for cs, s in zip(sums, slots)])
        return [_adamw_layer(l, weights[n], m_in[n], v_in[n], g[0], o) for n, g, o in zip(BIG, summed, outs)]

    grad_x, sq = _loss_head(act, loss_target[0])
    grads = [None] * depth
    big_outs = [[lax.empty(weights[n].shape, f32) for _ in range(4)] for n in BIG]
    flight = None
    for l in reversed(range(depth)):
        grad_x, grads[l] = _layer_bwd(grad_x, layers[l], saved[l], sq if flight is None else token)
        earlier = flight
        flight, token = reduce_start(l, grad_x)
        if earlier is not None:
            big_outs = reduce_finish(l + 1, earlier, token, big_outs)
    big_outs = reduce_finish(0, flight, big_outs[0][1], big_outs)
    loss = lax.psum(0.5 / D_MODEL * jnp.sum(sq), ("x", "y", "c"))

    small_shapes = [(depth,) + grads[0][n].shape for n in SMALL]
    small_part = _pack_small([jnp.stack([g[n] for g in grads]) for n in SMALL])
    summed = _reduce_to_owners([small_part.reshape(1, N_CHIPS, SMALL_ROWS // N_CHIPS, LANES)])
    small_sum = _gather_shards([summed[0]])[0].reshape(SMALL_ROWS, LANES)
    grad = {n: o[0] for n, o in zip(BIG, big_outs)}
    grad.update(zip(SMALL, _unpack_small(small_sum, small_shapes)))
    grad["pool_scale"] = grad["pool_scale"].reshape(pool_scale.shape)
    grad["sgu_ln_g"] = grad["sgu_ln_g"].reshape(sgu_ln_g.shape)
    for n in ("ln1_g", "ln1_b", "ln2_g", "ln2_b"):
        grad[n] = grad[n].reshape(ln1_g.shape)
    grad["w_conv"] = lax.dynamic_slice_in_dim(grad["w_conv"], chip * conv_cols, conv_cols, axis=2)

    delta = {n: o[1] for n, o in zip(BIG, big_outs)}
    new_m = {n: o[2] for n, o in zip(BIG, big_outs)}
    new_v = {n: o[3] for n, o in zip(BIG, big_outs)}
    delta["w_conv"], new_m["w_conv"], new_v["w_conv"] = _adamw(w_conv, grad["w_conv"], m_w_conv, v_w_conv)
    rest = [n for n in SMALL if n != "w_conv"]
    rest_shapes = [weights[n].shape for n in rest]
    packed = [_pack_small([src[n] for n in rest]) for src in (weights, grad, m_in, v_in)]
    for dst, res in zip((delta, new_m, new_v), _adamw(*packed)):
        dst.update(zip(rest, _unpack_small(res, rest_shapes)))

    return (loss, grad_x[None], *[grad[n] for n in WEIGHTS], *[delta[n] for n in WEIGHTS],
            *[new_m[n] for n in WEIGHTS], *[new_v[n] for n in WEIGHTS])
```

```python
import functools

import jax
import jax.numpy as jnp
from jax import lax
from jax.experimental import pallas as pl
from jax.experimental.pallas import tpu as pltpu

f32 = jnp.float32
bf16 = jnp.bfloat16

D_MODEL = 1024
DEPTH = 4
CONV_W = 384
POOL_W = 256
SGU_W = 384
IN_W = 3 * CONV_W + POOL_W + 2 * SGU_W
D_FF = 2816
CHUNK = 128
HEAD = 64
POOL_WINDOWS = (2, 4, 8, 16)
ALPHA = float((2 * DEPTH) ** 0.25)
LN_EPS = 1e-5
ADAM_LR = 0.001
ADAM_B1 = 0.9
ADAM_B2 = 0.999
ADAM_EPS = 1e-08
ADAM_WD = 0.01
ADAM_STEP = 10

LANES = 128
TOKEN_TILE = 256
N_CHIPS = 4
VMEM_LIMIT = 56 * 1024 * 1024

BLK_XA, BLK_GB, BLK_GC, BLK_P, BLK_U, BLK_V = 0, 3, 6, 9, 11, 14

MESH = pl.DeviceIdType.MESH


def _params(sem=None):
    return pltpu.CompilerParams(dimension_semantics=sem, vmem_limit_bytes=VMEM_LIMIT)


def _rows(width, tile=TOKEN_TILE):
    return pl.BlockSpec((tile, width), lambda i: (i, 0))


def _resident(shape):
    zeros = (0,) * len(shape)
    return pl.BlockSpec(shape, lambda *_: zeros, pipeline_mode=pl.Buffered(1))


def _nt(a, b):
    return lax.dot_general(a, b, (((1,), (1,)), ((), ())), preferred_element_type=f32)


def _tn(a, b):
    return lax.dot_general(a, b, (((0,), (0,)), ((), ())), preferred_element_type=f32)


def _mm(a, b):
    return jnp.dot(a, b, preferred_element_type=f32)


def _norm_fwd(z):
    mu = jnp.mean(z, axis=-1, keepdims=True)
    zc = z - mu
    var = jnp.mean(zc * zc, axis=-1, keepdims=True)
    rstd = lax.rsqrt(var + LN_EPS)
    return zc * rstd, rstd


def _norm_bwd(dxhat, xhat, rstd):
    m1 = jnp.mean(dxhat, axis=-1, keepdims=True)
    m2 = jnp.mean(dxhat * xhat, axis=-1, keepdims=True)
    return rstd * (dxhat - m1 - xhat * m2)


def _proj(x, w_in_b, after):
    s = x.shape[0]

    def body(x_ref, w_ref, after_ref, p_ref, xb_ref):
        xb = x_ref[...].astype(bf16)
        xb_ref[...] = xb
        p_ref[...] = _nt(xb, w_ref[...])

    return pl.pallas_call(
        body, grid=(s // TOKEN_TILE,),
        in_specs=[_rows(D_MODEL), _resident((IN_W, D_MODEL)), pl.BlockSpec(memory_space=pl.ANY)],
        out_specs=[_rows(IN_W), _rows(D_MODEL)],
        out_shape=[jax.ShapeDtypeStruct((s, IN_W), f32), jax.ShapeDtypeStruct((s, D_MODEL), bf16)],
        name="proj", compiler_params=_params(("arbitrary",)))(x, w_in_b, after)


def _row_ranges(parts):
    out, at = [], 0
    for p in parts:
        out.append((at, at + p.shape[1]))
        at += p.shape[1]
    return out


def _wo_ln1(mix, x, w_o_b, g, b):
    s = x.shape[0]
    n = len(mix)
    ranges = _row_ranges(mix)

    def body(*refs):
        m_refs = refs[:n]
        x_ref, w_ref, g_ref, b_ref, xhat_ref, rstd_ref, hb_ref = refs[n:]
        z = ALPHA * x_ref[...]
        for m_ref, (lo, hi) in zip(m_refs, ranges):
            z = z + _mm(m_ref[...], w_ref[lo:hi, :])
        xhat, rstd = _norm_fwd(z)
        xhat_ref[...] = xhat
        rstd_ref[...] = rstd
        hb_ref[...] = (xhat * g_ref[...] + b_ref[...]).astype(bf16)

    return pl.pallas_call(
        body, grid=(s // TOKEN_TILE,),
        in_specs=[_rows(m.shape[1]) for m in mix] + [_rows(D_MODEL), _resident((D_MODEL, D_MODEL)), _resident((1, D_MODEL)),
                                                     _resident((1, D_MODEL))],
        out_specs=[_rows(D_MODEL), _rows(1), _rows(D_MODEL)],
        out_shape=[jax.ShapeDtypeStruct((s, D_MODEL), f32), jax.ShapeDtypeStruct((s, 1), f32),
                   jax.ShapeDtypeStruct((s, D_MODEL), bf16)],
        name="wo_ln1", compiler_params=_params(("arbitrary",)))(*mix, x, w_o_b, g, b)


def _mlp_fwd(xhat1, g1, b1, w_gu_b, w_down_b, g2, b2):
    s = xhat1.shape[0]

    def body(xh_ref, g1_ref, b1_ref, wgu_ref, wd_ref, g2_ref, b2_ref, gu_ref, xhat2_ref, rstd2_ref, y_ref):
        h = xh_ref[...] * g1_ref[...] + b1_ref[...]
        gu = _mm(h.astype(bf16), wgu_ref[...])
        gu_ref[...] = gu
        gate = gu[:, :D_FF]
        act = gate * jax.nn.sigmoid(gate) * gu[:, D_FF:]
        z = ALPHA * h + _mm(act.astype(bf16), wd_ref[...])
        xhat2, rstd2 = _norm_fwd(z)
        xhat2_ref[...] = xhat2
        rstd2_ref[...] = rstd2
        y_ref[...] = xhat2 * g2_ref[...] + b2_ref[...]

    vec = _resident((1, D_MODEL))
    return pl.pallas_call(
        body, grid=(s // TOKEN_TILE,),
        in_specs=[_rows(D_MODEL), vec, vec, _resident((D_MODEL, 2 * D_FF)), _resident((D_FF, D_MODEL)), vec, vec],
        out_specs=[_rows(2 * D_FF), _rows(D_MODEL), _rows(1), _rows(D_MODEL)],
        out_shape=[jax.ShapeDtypeStruct((s, 2 * D_FF), f32), jax.ShapeDtypeStruct((s, D_MODEL), f32),
                   jax.ShapeDtypeStruct((s, 1), f32), jax.ShapeDtypeStruct((s, D_MODEL), f32)],
        name="mlp_fwd", compiler_params=_params(("arbitrary",)))(xhat1, g1, b1, w_gu_b, w_down_b, g2, b2)


def _loss_head(y, target):
    s = y.shape[0]

    def body(y_ref, t_ref, dy_ref, sq_ref):
        @pl.when(pl.program_id(0) == 0)
        def _():
            sq_ref[...] = jnp.zeros_like(sq_ref)

        e = y_ref[...] - t_ref[...]
        dy_ref[...] = e * (1.0 / D_MODEL)
        sq_ref[...] += jnp.sum(e * e, axis=0, keepdims=True)

    return pl.pallas_call(
        body, grid=(s // TOKEN_TILE,),
        in_specs=[_rows(D_MODEL), _rows(D_MODEL)],
        out_specs=[_rows(D_MODEL), pl.BlockSpec((1, D_MODEL), lambda i: (0, 0))],
        out_shape=[jax.ShapeDtypeStruct((s, D_MODEL), f32), jax.ShapeDtypeStruct((1, D_MODEL), f32)],
        name="loss_head", compiler_params=_params(("arbitrary",)))(y, target)


def _mlp_bwd(dy, xhat2, rstd2, g2, gu, w_gu_b, w_down_b, after):
    s = dy.shape[0]

    def body(dy_ref, xh_ref, rs_ref, g2_ref, gu_ref, wgu_ref, wd_ref, after_ref, dz_ref, act_ref, dgu_ref, dh_ref, gg_ref, gb_ref):
        @pl.when(pl.program_id(0) == 0)
        def _():
            gg_ref[...] = jnp.zeros_like(gg_ref)
            gb_ref[...] = jnp.zeros_like(gb_ref)

        dy_t = dy_ref[...]
        xhat = xh_ref[...]
        gg_ref[...] += jnp.sum(dy_t * xhat, axis=0, keepdims=True)
        gb_ref[...] += jnp.sum(dy_t, axis=0, keepdims=True)
        dz = _norm_bwd(dy_t * g2_ref[...], xhat, rs_ref[...])
        dzb = dz.astype(bf16)
        dz_ref[...] = dzb
        dact = _nt(dzb, wd_ref[...])
        gate = gu_ref[:, :D_FF]
        up = gu_ref[:, D_FF:]
        sg = jax.nn.sigmoid(gate)
        silu = gate * sg
        act_ref[...] = (silu * up).astype(bf16)
        dgu_ref[:, :D_FF] = (dact * up * (sg * (1.0 + gate * (1.0 - sg)))).astype(bf16)
        dgu_ref[:, D_FF:] = (dact * silu).astype(bf16)
        dh_ref[...] = ALPHA * dz + _nt(dgu_ref[...], wgu_ref[...])

    vec_out = pl.BlockSpec((1, D_MODEL), lambda i: (0, 0))
    return pl.pallas_call(
        body, grid=(s // TOKEN_TILE,),
        in_specs=[_rows(D_MODEL), _rows(D_MODEL), _rows(1), _resident((1, D_MODEL)), _rows(2 * D_FF),
                  _resident((D_MODEL, 2 * D_FF)), _resident((D_FF, D_MODEL)), pl.BlockSpec(memory_space=pl.ANY)],
        out_specs=[_rows(D_MODEL), _rows(D_FF), _rows(2 * D_FF), _rows(D_MODEL), vec_out, vec_out],
        out_shape=[jax.ShapeDtypeStruct((s, D_MODEL), bf16), jax.ShapeDtypeStruct((s, D_FF), bf16),
                   jax.ShapeDtypeStruct((s, 2 * D_FF), bf16), jax.ShapeDtypeStruct((s, D_MODEL), f32),
                   jax.ShapeDtypeStruct((1, D_MODEL), f32), jax.ShapeDtypeStruct((1, D_MODEL), f32)],
        name="mlp_bwd", compiler_params=_params(("arbitrary",)))(dy, xhat2, rstd2, g2, gu, w_gu_b, w_down_b, after)


def _ln1_wo_bwd(dh, xhat1, rstd1, g1, w_o_b):
    s = dh.shape[0]

    def body(dh_ref, xh_ref, rs_ref, g1_ref, w_ref, dz_ref, dzb_ref, dm_ref, gg_ref, gb_ref):
        @pl.when(pl.program_id(0) == 0)
        def _():
            gg_ref[...] = jnp.zeros_like(gg_ref)
            gb_ref[...] = jnp.zeros_like(gb_ref)

        dh_t = dh_ref[...]
        xhat = xh_ref[...]
        gg_ref[...] += jnp.sum(dh_t * xhat, axis=0, keepdims=True)
        gb_ref[...] += jnp.sum(dh_t, axis=0, keepdims=True)
        dz = _norm_bwd(dh_t * g1_ref[...], xhat, rs_ref[...])
        dz_ref[...] = dz
        dzb = dz.astype(bf16)
        dzb_ref[...] = dzb
        dm_ref[...] = _nt(dzb, w_ref[...])

    vec_out = pl.BlockSpec((1, D_MODEL), lambda i: (0, 0))
    return pl.pallas_call(
        body, grid=(s // TOKEN_TILE,),
        in_specs=[_rows(D_MODEL), _rows(D_MODEL), _rows(1), _resident((1, D_MODEL)), _resident((D_MODEL, D_MODEL))],
        out_specs=[_rows(D_MODEL), _rows(D_MODEL), _rows(D_MODEL), vec_out, vec_out],
        out_shape=[jax.ShapeDtypeStruct((s, D_MODEL), f32), jax.ShapeDtypeStruct((s, D_MODEL), bf16),
                   jax.ShapeDtypeStruct((s, D_MODEL), f32), jax.ShapeDtypeStruct((1, D_MODEL), f32),
                   jax.ShapeDtypeStruct((1, D_MODEL), f32)],
        name="ln1_wo_bwd", compiler_params=_params(("arbitrary",)))(dh, xhat1, rstd1, g1, w_o_b)


def _dx(dz1, dparts, w_in_t):
    s = dz1.shape[0]
    n = len(dparts)
    ranges = _row_ranges(dparts)

    def body(*refs):
        d_refs = refs[:n]
        dz_ref, w_ref, dx_ref = refs[n:]
        acc = ALPHA * dz_ref[...]
        for d_ref, (lo, hi) in zip(d_refs, ranges):
            acc = acc + _mm(d_ref[...], w_ref[lo:hi, :])
        dx_ref[...] = acc

    return pl.pallas_call(
        body, grid=(s // TOKEN_TILE,),
        in_specs=[_rows(d.shape[1]) for d in dparts] + [_rows(D_MODEL), _resident((IN_W, D_MODEL))],
        out_specs=_rows(D_MODEL),
        out_shape=jax.ShapeDtypeStruct((s, D_MODEL), f32),
        name="dx", compiler_params=_params(("arbitrary",)))(*dparts, dz1, w_in_t)


def _weight_grad_rows(parts, b, bn):
    s, n_cols = b.shape
    n = len(parts)
    ranges = _row_ranges(parts)
    m = ranges[-1][1]

    def body(*refs):
        p_refs = refs[:n]
        b_ref, o_ref = refs[n:]
        for p_ref, (lo, hi) in zip(p_refs, ranges):
            o_ref[lo:hi, :] = _tn(p_ref[...], b_ref[...]).astype(bf16)

    return pl.pallas_call(
        body, grid=(n_cols // bn,),
        in_specs=[_resident(p.shape) for p in parts] + [pl.BlockSpec((s, bn), lambda j: (0, j))],
        out_specs=pl.BlockSpec((m, bn), lambda j: (0, j)),
        out_shape=jax.ShapeDtypeStruct((m, n_cols), bf16),
        name="weight_grad_rows", compiler_params=_params(("arbitrary",)))(*parts, b)


def _weight_grad(a, b, bm, bn):
    s, m = a.shape
    n = b.shape[1]

    def body(a_ref, b_ref, o_ref):
        o_ref[...] = _tn(a_ref[...], b_ref[...]).astype(bf16)

    return pl.pallas_call(
        body, grid=(m // bm, n // bn),
        in_specs=[pl.BlockSpec((s, bm), lambda i, j: (0, i)), pl.BlockSpec((s, bn), lambda i, j: (0, j))],
        out_specs=pl.BlockSpec((bm, bn), lambda i, j: (i, j)),
        out_shape=jax.ShapeDtypeStruct((m, n), bf16),
        name="weight_grad", compiler_params=_params(("arbitrary", "arbitrary")))(a, b)


def _shift_down(a, k):
    row = lax.broadcasted_iota(jnp.int32, a.shape, 0)
    return jnp.where(row >= k, pltpu.roll(a, k, 0), 0.0)


def _shift_up(a, k):
    n = a.shape[0]
    row = lax.broadcasted_iota(jnp.int32, a.shape, 0)
    return jnp.where(row < n - k, pltpu.roll(a, n - k, 0), 0.0)


def _slab(s, block):
    return pl.BlockSpec((s, LANES), lambda k: (0, block + k))


def _conv_y(z, w):
    return w[0:1, :] * _shift_down(z, 2) + w[1:2, :] * _shift_down(z, 1) + w[2:3, :] * z


def _conv_fwd(proj, w_conv):
    s = proj.shape[0]

    def body(xa_ref, gb_ref, gc_ref, w_ref, o_ref):
        z = gc_ref[...] * xa_ref[...]
        o_ref[...] = (gb_ref[...] * _conv_y(z, w_ref[...])).astype(bf16)

    return pl.pallas_call(
        body, grid=(CONV_W // LANES,),
        in_specs=[_slab(s, BLK_XA), _slab(s, BLK_GB), _slab(s, BLK_GC), pl.BlockSpec((3, LANES), lambda k: (0, k))],
        out_specs=_slab(s, 0),
        out_shape=jax.ShapeDtypeStruct((s, CONV_W), bf16),
        name="conv_fwd", compiler_params=_params(("arbitrary",)))(proj, proj, proj, w_conv)


def _conv_bwd(proj, dmix, w_conv):
    s = proj.shape[0]

    def body(xa_ref, gb_ref, gc_ref, dy_ref, w_ref, dxa_ref, dgb_ref, dgc_ref, dw_ref):
        xa = xa_ref[...]
        gc = gc_ref[...]
        w = w_ref[...]
        z = gc * xa
        dya = dy_ref[...]
        dgb_ref[...] = (dya * _conv_y(z, w)).astype(bf16)
        dy = dya * gb_ref[...]
        dz = w[2:3, :] * dy + w[1:2, :] * _shift_up(dy, 1) + w[0:1, :] * _shift_up(dy, 2)
        dxa_ref[...] = (dz * gc).astype(bf16)
        dgc_ref[...] = (dz * xa).astype(bf16)
        dw_ref[0:1, :] = jnp.sum(dy * _shift_down(z, 2), axis=0, keepdims=True)
        dw_ref[1:2, :] = jnp.sum(dy * _shift_down(z, 1), axis=0, keepdims=True)
        dw_ref[2:3, :] = jnp.sum(dy * z, axis=0, keepdims=True)

    out = jax.ShapeDtypeStruct((s, CONV_W), bf16)
    return pl.pallas_call(
        body, grid=(CONV_W // LANES,),
        in_specs=[_slab(s, BLK_XA), _slab(s, BLK_GB), _slab(s, BLK_GC), _slab(s, 0), pl.BlockSpec((3, LANES), lambda k: (0, k))],
        out_specs=[_slab(s, 0), _slab(s, 0), _slab(s, 0), pl.BlockSpec((3, LANES), lambda k: (0, k))],
        out_shape=[out, out, out, jax.ShapeDtypeStruct((3, CONV_W), f32)],
        name="conv_bwd", compiler_params=_params(("arbitrary",)))(proj, proj, proj, dmix, w_conv)


def _pool_window(k):
    lane = lax.broadcasted_iota(jnp.int32, (1, LANES), 1)
    low = lane < HEAD
    first = k == 0
    wlen = jnp.where(low, jnp.where(first, POOL_WINDOWS[0], POOL_WINDOWS[2]), jnp.where(first, POOL_WINDOWS[1], POOL_WINDOWS[3]))
    return wlen, low, first


def _pool_diff(p, k):
    wlen, low, first = _pool_window(k)
    s2 = p + _shift_down(p, 1)
    s4 = s2 + _shift_down(s2, 2)
    s8 = s4 + _shift_down(s4, 4)
    s16 = s8 + _shift_down(s8, 8)
    win = jnp.where(low, jnp.where(first, s2, s8), jnp.where(first, s4, s16))
    row = lax.broadcasted_iota(jnp.int32, p.shape, 0)
    count = jnp.minimum(row + 1, wlen).astype(f32)
    return win / count - p, count


def _pool_weight(w_ref):
    zero = jnp.zeros((HEAD, HEAD), f32)
    top = jnp.concatenate([w_ref[0], zero], axis=1)
    bottom = jnp.concatenate([zero, w_ref[1]], axis=1)
    return jnp.concatenate([top, bottom], axis=0).astype(bf16)


def _pool_fwd(proj, w_pool, pool_scale):
    s = proj.shape[0]

    def body(p_ref, w_ref, sc_ref, o_ref):
        d, _ = _pool_diff(p_ref[...], pl.program_id(0))
        o_ref[...] = (_mm(d.astype(bf16), _pool_weight(w_ref)) * sc_ref[...]).astype(bf16)

    return pl.pallas_call(
        body, grid=(POOL_W // LANES,),
        in_specs=[_slab(s, BLK_P), pl.BlockSpec((2, HEAD, HEAD), lambda k: (k, 0, 0)), pl.BlockSpec((1, LANES), lambda k: (0, k))],
        out_specs=_slab(s, 0),
        out_shape=jax.ShapeDtypeStruct((s, POOL_W), bf16),
        name="pool_fwd", compiler_params=_params(("arbitrary",)))(proj, w_pool, pool_scale)


def _pool_bwd(proj, dmix, w_pool, pool_scale):
    s = proj.shape[0]

    def body(p_ref, dy_ref, w_ref, sc_ref, dp_ref, dw_ref, dsc_ref):
        k = pl.program_id(0)
        d, count = _pool_diff(p_ref[...], k)
        wbd = _pool_weight(w_ref)
        db = d.astype(bf16)
        dyb = dy_ref[...]
        dsc_ref[...] = jnp.sum(dyb * _mm(db, wbd), axis=0, keepdims=True)
        dpre = (dyb * sc_ref[...]).astype(bf16)
        dwbd = _tn(db, dpre)
        dw_ref[0] = dwbd[:HEAD, :HEAD]
        dw_ref[1] = dwbd[HEAD:, HEAD:]
        dd = _nt(dpre, wbd)
        e = dd / count
        wlen, low, first = _pool_window(k)
        a2 = e + _shift_up(e, 1)
        a4 = a2 + _shift_up(a2, 2)
        a8 = a4 + _shift_up(a4, 4)
        a16 = a8 + _shift_up(a8, 8)
        back = jnp.where(low, jnp.where(first, a2, a8), jnp.where(first, a4, a16))
        dp_ref[...] = (back - dd).astype(bf16)

    return pl.pallas_call(
        body, grid=(POOL_W // LANES,),
        in_specs=[_slab(s, BLK_P), _slab(s, CONV_W // LANES), pl.BlockSpec((2, HEAD, HEAD), lambda k: (k, 0, 0)),
                  pl.BlockSpec((1, LANES), lambda k: (0, k))],
        out_specs=[_slab(s, 0), pl.BlockSpec((2, HEAD, HEAD), lambda k: (k, 0, 0)), pl.BlockSpec((1, LANES), lambda k: (0, k))],
        out_shape=[jax.ShapeDtypeStruct((s, POOL_W), bf16), jax.ShapeDtypeStruct((4, HEAD, HEAD), f32),
                   jax.ShapeDtypeStruct((1, POOL_W), f32)],
        name="pool_bwd", compiler_params=_params(("arbitrary",)))(proj, dmix, w_pool, pool_scale)


INV_SQRT2 = 0.7071067811865476
INV_SQRT_2PI = 0.3989422804014327


def _gelu(x):
    return 0.5 * x * (1.0 + lax.erf(x * INV_SQRT2))


def _gelu_grad(x):
    return 0.5 * (1.0 + lax.erf(x * INV_SQRT2)) + x * (INV_SQRT_2PI * jnp.exp(-0.5 * x * x))


def _head_mean(a, low):
    s_low = jnp.sum(jnp.where(low, a, 0.0), axis=-1, keepdims=True)
    s_high = jnp.sum(jnp.where(low, 0.0, a), axis=-1, keepdims=True)
    return jnp.where(low, s_low, s_high) * (1.0 / HEAD)


def _tril():
    r = lax.broadcasted_iota(jnp.int32, (CHUNK, CHUNK), 0)
    c = lax.broadcasted_iota(jnp.int32, (CHUNK, CHUNK), 1)
    return r >= c


def _sgu_chunk(up, vp, g, wm0, wm1, b0, b1, low):
    ug = _gelu(up)
    vg = _gelu(vp)
    vc = vg - _head_mean(vg, low)
    rstd = lax.rsqrt(_head_mean(vc * vc, low) + LN_EPS)
    vn = vc * rstd
    vb = (vn * g).astype(bf16)
    mixed = jnp.where(low, _mm(wm0, vb) + b0, _mm(wm1, vb) + b1)
    return ug, vn, rstd, vb, mixed


def _sgu_specs(s):
    return [_slab(s, BLK_U), _slab(s, BLK_V), pl.BlockSpec((1, LANES), lambda k: (0, k)),
            pl.BlockSpec((2, CHUNK, CHUNK), lambda k: (k, 0, 0)), pl.BlockSpec((2, CHUNK, 1), lambda k: (k, 0, 0))]


def _sgu_fwd(proj, sgu_g, w_spatial, b_spatial3):
    s = proj.shape[0]

    def body(u_ref, v_ref, g_ref, w_ref, b_ref, o_ref):
        low = lax.broadcasted_iota(jnp.int32, (1, LANES), 1) < HEAD
        mask = _tril()
        wm0 = jnp.where(mask, w_ref[0], 0.0).astype(bf16)
        wm1 = jnp.where(mask, w_ref[1], 0.0).astype(bf16)
        g = g_ref[...]
        b0 = b_ref[0]
        b1 = b_ref[1]

        def chunk(n, carry):
            rows = pl.ds(pl.multiple_of(n * CHUNK, CHUNK), CHUNK)
            ug, _, _, _, mixed = _sgu_chunk(u_ref[rows, :], v_ref[rows, :], g, wm0, wm1, b0, b1, low)
            o_ref[rows, :] = (ug * mixed).astype(bf16)
            return carry

        lax.fori_loop(0, s // CHUNK, chunk, 0)

    return pl.pallas_call(
        body, grid=(SGU_W // LANES,),
        in_specs=_sgu_specs(s),
        out_specs=_slab(s, 0),
        out_shape=jax.ShapeDtypeStruct((s, SGU_W), bf16),
        name="sgu_fwd", compiler_params=_params(("arbitrary",)))(proj, proj, sgu_g, w_spatial, b_spatial3)


def _sgu_bwd(proj, dmix, sgu_g, w_spatial, b_spatial3):
    s = proj.shape[0]

    def body(u_ref, v_ref, g_ref, w_ref, b_ref, dy_ref, du_ref, dv_ref, dg_ref, dw_ref, db_ref):
        low = lax.broadcasted_iota(jnp.int32, (1, LANES), 1) < HEAD
        mask = _tril()
        w0 = jnp.where(mask, w_ref[0], 0.0)
        w1 = jnp.where(mask, w_ref[1], 0.0)
        wm0 = w0.astype(bf16)
        wm1 = w1.astype(bf16)
        wt0 = w0.T.astype(bf16)
        wt1 = w1.T.astype(bf16)
        g = g_ref[...]
        b0 = b_ref[0]
        b1 = b_ref[1]
        dg_ref[...] = jnp.zeros_like(dg_ref)
        dw_ref[...] = jnp.zeros_like(dw_ref)
        db_ref[...] = jnp.zeros_like(db_ref)

        def chunk(n, carry):
            rows = pl.ds(pl.multiple_of(n * CHUNK, CHUNK), CHUNK)
            up = u_ref[rows, :]
            vp = v_ref[rows, :]
            ug, vn, rstd, vb, mixed = _sgu_chunk(up, vp, g, wm0, wm1, b0, b1, low)
            dy = dy_ref[rows, :]
            du_ref[rows, :] = (dy * mixed * _gelu_grad(up)).astype(bf16)
            dmix_c = dy * ug
            db_ref[0] += jnp.sum(jnp.where(low, dmix_c, 0.0), axis=-1, keepdims=True)
            db_ref[1] += jnp.sum(jnp.where(low, 0.0, dmix_c), axis=-1, keepdims=True)
            dmb = dmix_c.astype(bf16)
            zero = jnp.zeros_like(dmb)
            dw_ref[0] += _nt(jnp.where(low, dmb, zero), vb)
            dw_ref[1] += _nt(jnp.where(low, zero, dmb), vb)
            dvnorm = jnp.where(low, _mm(wt0, dmb), _mm(wt1, dmb))
            dg_ref[...] += jnp.sum(dvnorm * vn, axis=0, keepdims=True)
            dvn = dvnorm * g
            dvg = rstd * (dvn - _head_mean(dvn, low) - vn * _head_mean(dvn * vn, low))
            dv_ref[rows, :] = (dvg * _gelu_grad(vp)).astype(bf16)
            return carry

        lax.fori_loop(0, s // CHUNK, chunk, 0)
        dw_ref[0] = jnp.where(mask, dw_ref[0], 0.0)
        dw_ref[1] = jnp.where(mask, dw_ref[1], 0.0)

    out = jax.ShapeDtypeStruct((s, SGU_W), bf16)
    return pl.pallas_call(
        body, grid=(SGU_W // LANES,),
        in_specs=_sgu_specs(s) + [_slab(s, (CONV_W + POOL_W) // LANES)],
        out_specs=[_slab(s, 0), _slab(s, 0), pl.BlockSpec((1, LANES), lambda k: (0, k)),
                   pl.BlockSpec((2, CHUNK, CHUNK), lambda k: (k, 0, 0)), pl.BlockSpec((2, CHUNK, 1), lambda k: (k, 0, 0))],
        out_shape=[out, out, jax.ShapeDtypeStruct((1, SGU_W), f32), jax.ShapeDtypeStruct((6, CHUNK, CHUNK), f32),
                   jax.ShapeDtypeStruct((6, CHUNK, 1), f32)],
        name="sgu_bwd", compiler_params=_params(("arbitrary",)))(proj, proj, sgu_g, w_spatial, b_spatial3, dmix)


def _layer_fwd(x, w, after):
    proj, xb = _proj(x, w["w_in"], after)
    ya = _conv_fwd(proj, w["w_conv"])
    yb = _pool_fwd(proj, w["w_pool"], w["pool_scale"])
    yc = _sgu_fwd(proj, w["sgu_ln_g"], w["w_spatial"], w["b_spatial"])
    mix = [ya, yb, yc]
    xhat1, rstd1, hb = _wo_ln1(mix, x, w["w_o"], w["ln1_g"], w["ln1_b"])
    gu, xhat2, rstd2, y = _mlp_fwd(xhat1, w["ln1_g"], w["ln1_b"], w["w_gate_up"], w["w_down"], w["ln2_g"], w["ln2_b"])
    saved = dict(proj=proj, xb=xb, mix=mix, xhat1=xhat1, rstd1=rstd1, hb=hb, gu=gu, xhat2=xhat2, rstd2=rstd2)
    return y, saved


def _layer_bwd(dy, w, sv, after):
    dz2b, actb, dgub, dh, g_ln2_g, g_ln2_b = _mlp_bwd(dy, sv["xhat2"], sv["rstd2"], w["ln2_g"], sv["gu"], w["w_gate_up"],
                                                      w["w_down"], after)
    dz1, dz1b, dmix, g_ln1_g, g_ln1_b = _ln1_wo_bwd(dh, sv["xhat1"], sv["rstd1"], w["ln1_g"], w["w_o"])
    dxa, dgb, dgc, g_conv = _conv_bwd(sv["proj"], dmix, w["w_conv"])
    dp, g_pool, g_pscale = _pool_bwd(sv["proj"], dmix, w["w_pool"], w["pool_scale"])
    du, dv, g_sgu_g, g_spatial, g_bsp = _sgu_bwd(sv["proj"], dmix, w["sgu_ln_g"], w["w_spatial"], w["b_spatial"])
    dparts = [dxa, dgb, dgc, dp, du, dv]
    dx = _dx(dz1, dparts, w["w_in"])
    grads = dict(
        w_in=_weight_grad_rows(dparts, sv["xb"], 512),
        w_o=_weight_grad_rows(sv["mix"], dz1b, D_MODEL),
        w_gate_up=_weight_grad(sv["hb"], dgub, 512, D_FF // 2),
        w_down=_weight_grad(actb, dz2b, D_FF // 2, D_MODEL),
        w_conv=g_conv, w_pool=g_pool, pool_scale=g_pscale, sgu_ln_g=g_sgu_g, w_spatial=g_spatial,
        b_spatial=g_bsp.reshape(6, CHUNK), ln1_g=g_ln1_g, ln1_b=g_ln1_b, ln2_g=g_ln2_g, ln2_b=g_ln2_b)
    return dx, grads


def _local_step(x, target, layers):
    saved = []
    for w in layers:
        x, sv = _layer_fwd(x, w, jnp.zeros((8, LANES), f32))
        saved.append(sv)
    dy, sq = _loss_head(x, target)
    grads = [None] * len(layers)
    for l in reversed(range(len(layers))):
        dy, grads[l] = _layer_bwd(dy, layers[l], saved[l], sq)
    return sq, dy, grads


ANY = pl.BlockSpec(memory_space=pl.ANY)


def _place():
    x, y, c = lax.axis_index("x"), lax.axis_index("y"), lax.axis_index("c")
    others = [(1 - x, y), (x, 1 - y), (1 - x, 1 - y)]
    return x, y, c, others


def _chip_index(cx, cy):
    return 2 * cx + cy


def _half(ref_rows, c):
    half = ref_rows // 2
    return pl.ds(pl.multiple_of(c * half, 8), half)


def _remote(src, dst, send_sem, recv_sem, device):
    return pltpu.make_async_remote_copy(src_ref=src, dst_ref=dst, send_sem=send_sem, recv_sem=recv_sem,
                                        device_id=device, device_id_type=MESH)


def _gather_shards(shards):
    n = len(shards)
    base, total = [], 0
    for s in shards:
        base.append(total)
        total += 6 * s.shape[0]

    def body(*refs):
        ins, outs = refs[:n], refs[n:2 * n]
        send, recv = refs[2 * n:]
        x, y, c, others = _place()
        me = _chip_index(x, y)
        sib = (x, y, 1 - c)
        sends = []
        for f in range(n):
            depth, rows = ins[f].shape[0], ins[f].shape[1]
            for l in range(depth):
                for k, (cx, cy) in enumerate(others):
                    sem = base[f] + 6 * l + k
                    cp = _remote(ins[f].at[l, _half(rows, c)], outs[f].at[l, me, _half(rows, c)],
                                 send.at[sem], recv.at[sem], (cx, cy, c))
                    cp.start()
                    sends.append(cp)
        for f in range(n):
            depth, rows = ins[f].shape[0], ins[f].shape[1]
            for l in range(depth):
                for k, (cx, cy) in enumerate(others):
                    sem = base[f] + 6 * l + k
                    landed = outs[f].at[l, _chip_index(cx, cy), _half(rows, c)]
                    _remote(landed, landed, send.at[sem], recv.at[sem], (cx, cy, c)).wait_recv()
                    cp = _remote(landed, landed, send.at[sem + 3], recv.at[sem + 3], sib)
                    cp.start()
                    sends.append(cp)
        for f in range(n):
            depth, rows = ins[f].shape[0], ins[f].shape[1]
            for l in range(depth):
                for k, (cx, cy) in enumerate(others):
                    sem = base[f] + 6 * l + k + 3
                    passed = outs[f].at[l, _chip_index(cx, cy), _half(rows, 1 - c)]
                    _remote(passed, passed, send.at[sem], recv.at[sem], sib).wait_recv()
        for cp in sends:
            cp.wait_send()

    gathered = pl.pallas_call(
        body, in_specs=[ANY] * n, out_specs=[ANY] * n,
        out_shape=[jax.ShapeDtypeStruct((s.shape[0], N_CHIPS) + s.shape[1:], s.dtype) for s in shards],
        scratch_shapes=[pltpu.SemaphoreType.DMA((total,)), pltpu.SemaphoreType.DMA((total,))],
        name="gather_shards")(*shards)
    return [_place_own(g, s) for g, s in zip(gathered, shards)]


def _scalar(value):
    return jnp.reshape(value, (1,)).astype(jnp.int32)


def _place_own(blocks, shard):
    depth, rows, cols = shard.shape

    def body(me_ref, b_ref, s_ref, o_ref):
        o_ref[...] = s_ref[...]

    return pl.pallas_call(
        body,
        grid_spec=pltpu.PrefetchScalarGridSpec(
            num_scalar_prefetch=1, grid=(depth,),
            in_specs=[ANY, pl.BlockSpec((None, rows, cols), lambda l, me: (l, 0, 0))],
            out_specs=pl.BlockSpec((None, None, rows, cols), lambda l, me: (l, me[0], 0, 0))),
        out_shape=jax.ShapeDtypeStruct(blocks.shape, blocks.dtype),
        input_output_aliases={1: 0},
        name="place_own", compiler_params=_params(("arbitrary",)))(
            _scalar(_chip_index(lax.axis_index("x"), lax.axis_index("y"))), blocks, shard)


HBM = pl.BlockSpec(memory_space=pltpu.HBM)
SEM = pl.BlockSpec(memory_space=pltpu.SEMAPHORE)
TOKEN = jax.ShapeDtypeStruct((8, LANES), f32)
SPLIT_COPY = pltpu.CompilerParams(has_side_effects=pltpu.SideEffectType.DATAFLOW_SIDE_EFFECTING)


def _in_hbm(a):
    return pltpu.with_memory_space_constraint(a, pltpu.HBM)


def _full_shape(shard, axis):
    rows, cols = shard.shape
    return (N_CHIPS * rows, cols) if axis == 0 else (rows, N_CHIPS * cols)


def _block_half(ref, axis, j, h):
    if axis == 0:
        rows = ref.shape[0] // N_CHIPS
        return ref.at[pl.ds(pl.multiple_of(j * rows + h * (rows // 2), 16), rows // 2), :]
    half, cols = ref.shape[0] // 2, ref.shape[1] // N_CHIPS
    return ref.at[pl.ds(pl.multiple_of(h * half, 16), half), pl.ds(pl.multiple_of(j * cols, LANES), cols)]


def _gather_start(shards, axes, after):
    n = len(shards)
    lands = [lax.empty(_full_shape(s, ax), s.dtype) for s, ax in zip(shards, axes)]

    def body(*refs):
        ins, lnd = refs[:n], refs[n:2 * n]
        send, recv = refs[2 * n + 1], refs[2 * n + 2]
        token = refs[-1]
        x, y, c, others = _place()
        me = _chip_index(x, y)
        for f in range(n):
            rows = ins[f].shape[0]
            for k, (cx, cy) in enumerate(others):
                _remote(ins[f].at[_half(rows, c)], _block_half(lnd[f], axes[f], me, c), send.at[3 * f + k], recv.at[3 * f + k],
                        (cx, cy, c)).start()
        token[...] = jnp.zeros_like(token)

    thru = [pltpu.HBM(a.shape, a.dtype) for a in list(shards) + lands]
    outs = pl.pallas_call(
        body, name="gather_start",
        in_specs=[HBM] * (2 * n) + [ANY],
        out_specs=(SEM, SEM, *[HBM] * (2 * n), pl.BlockSpec(memory_space=pltpu.VMEM)),
        out_shape=(pltpu.SemaphoreType.DMA((3 * n,)), pltpu.SemaphoreType.DMA((3 * n,)), *thru, TOKEN),
        input_output_aliases={i: 2 + i for i in range(2 * n)},
        compiler_params=SPLIT_COPY)(*[_in_hbm(a) for a in list(shards) + lands], after)
    return (outs[0], outs[1], outs[2:2 + n], outs[2 + n:2 + 2 * n]), outs[-1]


def _gather_wait(state, axes, after):
    send_sems, recv_sems, shards, lands = state
    n = len(shards)

    def body(*refs):
        ins, lnd = refs[:n], refs[n:2 * n]
        send, recv = refs[2 * n], refs[2 * n + 1]
        token = refs[-1]
        x, y, c, others = _place()
        for f in range(n):
            rows = ins[f].shape[0]
            for k, (cx, cy) in enumerate(others):
                cp = _remote(ins[f].at[_half(rows, c)], _block_half(lnd[f], axes[f], _chip_index(cx, cy), c),
                             send.at[3 * f + k], recv.at[3 * f + k], (cx, cy, c))
                cp.wait_send()
                cp.wait_recv()
        token[...] = jnp.zeros_like(token)

    thru = [pltpu.HBM(a.shape, a.dtype) for a in list(shards) + list(lands)]
    outs = pl.pallas_call(
        body, name="gather_wait",
        in_specs=[HBM] * (2 * n) + [SEM, SEM, ANY],
        out_specs=(*[HBM] * (2 * n), pl.BlockSpec(memory_space=pltpu.VMEM)),
        out_shape=(*thru, TOKEN),
        input_output_aliases={i: i for i in range(2 * n)},
        compiler_params=SPLIT_COPY)(*shards, *lands, send_sems, recv_sems, after)
    return outs[:n], outs[n:2 * n], outs[-1]


def _gather_finish(lands, shards, axes):
    n = len(lands)

    def body(*refs):
        outs = refs[n:2 * n]
        send, recv = refs[2 * n:]
        x, y, c, others = _place()
        sib = (x, y, 1 - c)
        sends = []
        for f in range(n):
            for k, (cx, cy) in enumerate(others):
                landed = _block_half(outs[f], axes[f], _chip_index(cx, cy), c)
                cp = _remote(landed, landed, send.at[3 * f + k], recv.at[3 * f + k], sib)
                cp.start()
                sends.append(cp)
        for f in range(n):
            for k, (cx, cy) in enumerate(others):
                passed = _block_half(outs[f], axes[f], _chip_index(cx, cy), 1 - c)
                _remote(passed, passed, send.at[3 * f + k], recv.at[3 * f + k], sib).wait_recv()
        for cp in sends:
            cp.wait_send()

    full = pl.pallas_call(
        body, in_specs=[ANY] * n, out_specs=[ANY] * n,
        out_shape=[jax.ShapeDtypeStruct(a.shape, a.dtype) for a in lands],
        input_output_aliases={f: f for f in range(n)},
        scratch_shapes=[pltpu.SemaphoreType.DMA((3 * n,)), pltpu.SemaphoreType.DMA((3 * n,))],
        name="gather_finish")(*lands)

    def place(me_ref, *refs):
        ins, outs = refs[n:2 * n], refs[2 * n:]
        for f in range(n):
            outs[f][...] = ins[f][...]

    return pl.pallas_call(
        place,
        grid_spec=pltpu.PrefetchScalarGridSpec(
            num_scalar_prefetch=1, grid=(1,),
            in_specs=[ANY] * n + [pl.BlockSpec(s.shape, lambda i, me: (0, 0)) for s in shards],
            out_specs=[pl.BlockSpec(s.shape, (lambda i, me: (me[0], 0)) if ax == 0 else (lambda i, me: (0, me[0])))
                       for s, ax in zip(shards, axes)]),
        out_shape=[jax.ShapeDtypeStruct(a.shape, a.dtype) for a in full],
        input_output_aliases={1 + f: f for f in range(n)},
        name="place_own_layer", compiler_params=_params(("arbitrary",)))(
            _scalar(_chip_index(lax.axis_index("x"), lax.axis_index("y"))), *full, *shards)


def _sibling_split_layer(parts, axes):
    n = len(parts)

    def body(*refs):
        ins, got = refs[:n], refs[n:2 * n]
        send, recv = refs[2 * n:]
        x, y, c, _ = _place()
        sib = (x, y, 1 - c)
        for f in range(n):
            for j in range(N_CHIPS):
                _remote(_block_half(ins[f], axes[f], j, 1 - c), got[f].at[j], send.at[f], recv.at[f], sib).start()
        for f in range(n):
            _remote(got[f], got[f], send.at[f], recv.at[f], sib).wait()

    def half_blocks(p, axis):
        rows, cols = (p.shape[0] // N_CHIPS, p.shape[1]) if axis == 0 else (p.shape[0], p.shape[1] // N_CHIPS)
        return jax.ShapeDtypeStruct((N_CHIPS, rows // 2, cols), p.dtype)

    return pl.pallas_call(
        body, in_specs=[ANY] * n, out_specs=[ANY] * n,
        out_shape=[half_blocks(p, ax) for p, ax in zip(parts, axes)],
        scratch_shapes=[pltpu.SemaphoreType.DMA((n,)), pltpu.SemaphoreType.DMA((n,))],
        name="sibling_split_layer")(*parts)


def _add_pair_layer(part, got, axis):
    _, half, cols = got.shape

    def body(c_ref, a_ref, b_ref, o_ref):
        o_ref[...] = (a_ref[...].astype(f32) + b_ref[...].astype(f32)).astype(o_ref.dtype)

    if axis == 0:
        part = part.reshape(N_CHIPS, 2, half, cols)
        mine = pl.BlockSpec((None, None, half, cols), lambda j, c: (j, c[0], 0, 0))
    else:
        mine = pl.BlockSpec((half, cols), lambda j, c: (c[0], j))
    return pl.pallas_call(
        body,
        grid_spec=pltpu.PrefetchScalarGridSpec(
            num_scalar_prefetch=1, grid=(N_CHIPS,),
            in_specs=[mine, pl.BlockSpec((None, half, cols), lambda j, c: (j, 0, 0))],
            out_specs=pl.BlockSpec((None, half, cols), lambda j, c: (j, 0, 0))),
        out_shape=jax.ShapeDtypeStruct(got.shape, part.dtype),
        name="add_pair_layer", compiler_params=_params(("arbitrary",)))(_scalar(lax.axis_index("c")), part, got)


def _scatter_start(sums, after):
    n = len(sums)
    lands = [lax.empty((3,) + s.shape[1:], s.dtype) for s in sums]

    def body(*refs):
        ins, lnd = refs[:n], refs[n:2 * n]
        send, recv = refs[2 * n + 1], refs[2 * n + 2]
        token = refs[-1]
        x, y, c, others = _place()
        for f in range(n):
            for k, (cx, cy) in enumerate(others):
                _remote(ins[f].at[_chip_index(cx, cy)], lnd[f].at[k], send.at[3 * f + k], recv.at[3 * f + k], (cx, cy, c)).start()
        token[...] = jnp.zeros_like(token)

    thru = [pltpu.HBM(a.shape, a.dtype) for a in list(sums) + lands]
    outs = pl.pallas_call(
        body, name="scatter_start",
        in_specs=[HBM] * (2 * n) + [ANY],
        out_specs=(SEM, SEM, *[HBM] * (2 * n), pl.BlockSpec(memory_space=pltpu.VMEM)),
        out_shape=(pltpu.SemaphoreType.DMA((3 * n,)), pltpu.SemaphoreType.DMA((3 * n,)), *thru, TOKEN),
        input_output_aliases={i: 2 + i for i in range(2 * n)},
        compiler_params=SPLIT_COPY)(*[_in_hbm(a) for a in list(sums) + lands], after)
    return (outs[0], outs[1], outs[2:2 + n], outs[2 + n:2 + 2 * n]), outs[-1]


def _scatter_wait(state, after):
    send_sems, recv_sems, sums, lands = state
    n = len(sums)

    def body(*refs):
        ins, lnd = refs[:n], refs[n:2 * n]
        send, recv = refs[2 * n], refs[2 * n + 1]
        token = refs[-1]
        x, y, c, others = _place()
        for f in range(n):
            for k, (cx, cy) in enumerate(others):
                cp = _remote(ins[f].at[_chip_index(cx, cy)], lnd[f].at[k], send.at[3 * f + k], recv.at[3 * f + k], (cx, cy, c))
                cp.wait_send()
                cp.wait_recv()
        token[...] = jnp.zeros_like(token)

    thru = [pltpu.HBM(a.shape, a.dtype) for a in list(sums) + list(lands)]
    outs = pl.pallas_call(
        body, name="scatter_wait",
        in_specs=[HBM] * (2 * n) + [SEM, SEM, ANY],
        out_specs=(*[HBM] * (2 * n), pl.BlockSpec(memory_space=pltpu.VMEM)),
        out_shape=(*thru, TOKEN),
        input_output_aliases={i: i for i in range(2 * n)},
        compiler_params=SPLIT_COPY)(*sums, *lands, send_sems, recv_sems, after)
    return outs[:n], outs[n:2 * n], outs[-1]


def _sibling_split(parts):
    n = len(parts)

    def body(*refs):
        ins, got = refs[:n], refs[n:2 * n]
        send, recv = refs[2 * n:]
        x, y, c, _ = _place()
        sib = (x, y, 1 - c)
        for f in range(n):
            depth, rows = ins[f].shape[0], ins[f].shape[2]
            for l in range(depth):
                for j in range(N_CHIPS):
                    _remote(ins[f].at[l, j, _half(rows, 1 - c)], got[f].at[l, j], send.at[f], recv.at[f], sib).start()
        for f in range(n):
            _remote(got[f], got[f], send.at[f], recv.at[f], sib).wait()

    return pl.pallas_call(
        body, in_specs=[ANY] * n, out_specs=[ANY] * n,
        out_shape=[jax.ShapeDtypeStruct(p.shape[:2] + (p.shape[2] // 2, p.shape[3]), p.dtype) for p in parts],
        scratch_shapes=[pltpu.SemaphoreType.DMA((n,)), pltpu.SemaphoreType.DMA((n,))],
        name="sibling_split")(*parts)


def _chip_scatter(sums):
    n = len(sums)

    def body(*refs):
        ins, outs = refs[:n], refs[n:2 * n]
        send, recv = refs[2 * n:]
        x, y, c, others = _place()
        for f in range(n):
            for l in range(ins[f].shape[0]):
                for k, (cx, cy) in enumerate(others):
                    _remote(ins[f].at[l, _chip_index(cx, cy)], outs[f].at[k, l], send.at[f * 3 + k], recv.at[f * 3 + k],
                            (cx, cy, c)).start()
        for f in range(n):
            for k, (cx, cy) in enumerate(others):
                _remote(outs[f].at[k], outs[f].at[k], send.at[f * 3 + k], recv.at[f * 3 + k], (cx, cy, c)).wait()

    return pl.pallas_call(
        body, in_specs=[ANY] * n, out_specs=[ANY] * n,
        out_shape=[jax.ShapeDtypeStruct((3, s.shape[0]) + s.shape[2:], s.dtype) for s in sums],
        scratch_shapes=[pltpu.SemaphoreType.DMA((3 * n,)), pltpu.SemaphoreType.DMA((3 * n,))],
        name="chip_scatter")(*sums)


def _sibling_join(sums):
    n = len(sums)

    def body(*refs):
        ins, outs = refs[:n], refs[n:2 * n]
        send, recv = refs[2 * n:]
        x, y, c, _ = _place()
        sib = (x, y, 1 - c)
        for f in range(n):
            depth, half_rows = outs[f].shape[0], outs[f].shape[1] // 2
            pieces = 4 if half_rows % 64 == 0 else 1
            step = half_rows // pieces
            for l in range(depth):
                for p in range(pieces):
                    mine = outs[f].at[l, pl.ds(pl.multiple_of(c * half_rows + p * step, 8), step)]
                    _remote(mine, mine, send.at[f], recv.at[f], sib).start()
        for f in range(n):
            half = outs[f].at[:, pl.ds(0, outs[f].shape[1] // 2)]
            _remote(half, half, send.at[f], recv.at[f], sib).wait()

    return pl.pallas_call(
        body, in_specs=[ANY] * n, out_specs=[ANY] * n,
        out_shape=[jax.ShapeDtypeStruct(s.shape, s.dtype) for s in sums],
        input_output_aliases={f: f for f in range(n)},
        scratch_shapes=[pltpu.SemaphoreType.DMA((n,)), pltpu.SemaphoreType.DMA((n,))],
        name="sibling_join")(*sums)


ELEMENTWISE_BLOCK_BYTES = 1 << 20


def _row_tile(rows, cols):
    best = None
    for tile in range(8, rows + 1, 8):
        if rows % tile == 0 and tile * cols * 4 <= ELEMENTWISE_BLOCK_BYTES:
            best = tile
    return best or rows


def _add_pair(part, got):
    depth, chips, rows, cols = part.shape
    half = rows // 2

    def body(c_ref, a_ref, b_ref, o_ref):
        o_ref[...] = (a_ref[...].astype(f32) + b_ref[...].astype(f32)).astype(o_ref.dtype)

    return pl.pallas_call(
        body,
        grid_spec=pltpu.PrefetchScalarGridSpec(
            num_scalar_prefetch=1, grid=(depth, chips),
            in_specs=[pl.BlockSpec((None, None, None, half, cols), lambda l, j, c: (l, j, c[0], 0, 0)),
                      pl.BlockSpec((None, None, half, cols), lambda l, j, c: (l, j, 0, 0))],
            out_specs=pl.BlockSpec((None, None, half, cols), lambda l, j, c: (l, j, 0, 0))),
        out_shape=jax.ShapeDtypeStruct(got.shape, part.dtype),
        name="add_pair", compiler_params=_params(("arbitrary", "arbitrary")))(
            _scalar(lax.axis_index("c")), part.reshape(depth, chips, 2, half, cols), got)


def _add_slots(chip_sums, slots):
    depth, _, half, cols = chip_sums.shape

    def body(at_ref, own_ref, s_ref, o_ref):
        acc = own_ref[...].astype(f32)
        for k in range(3):
            acc = acc + s_ref[k].astype(f32)
        o_ref[...] = acc

    at = jnp.concatenate([_scalar(_chip_index(lax.axis_index("x"), lax.axis_index("y"))), _scalar(lax.axis_index("c"))])
    out = pl.pallas_call(
        body,
        grid_spec=pltpu.PrefetchScalarGridSpec(
            num_scalar_prefetch=1, grid=(depth,),
            in_specs=[pl.BlockSpec((None, None, half, cols), lambda l, at: (l, at[0], 0, 0)),
                      pl.BlockSpec((3, None, half, cols), lambda l, at: (0, l, 0, 0))],
            out_specs=pl.BlockSpec((None, None, half, cols), lambda l, at: (l, at[1], 0, 0))),
        out_shape=jax.ShapeDtypeStruct((depth, 2, half, cols), f32),
        name="add_slots", compiler_params=_params(("arbitrary",)))(at, chip_sums, slots)
    return out.reshape(depth, 2 * half, cols)


def _adamw_math(w, grad, m, v):
    nm = ADAM_B1 * m + (1.0 - ADAM_B1) * grad
    nv = ADAM_B2 * v + (1.0 - ADAM_B2) * (grad * grad)
    m_hat = nm / (1.0 - ADAM_B1 ** ADAM_STEP)
    v_hat = nv / (1.0 - ADAM_B2 ** ADAM_STEP)
    return nm, nv, -ADAM_LR * (m_hat / (jnp.sqrt(v_hat) + ADAM_EPS) + ADAM_WD * w)


def _adamw(w, g, m, v):
    shape = w.shape
    flat = [a.reshape(-1, shape[-1]) for a in (w, g, m, v)]
    tile = _row_tile(flat[0].shape[0], shape[-1])

    def body(w_ref, g_ref, m_ref, v_ref, d_ref, nm_ref, nv_ref):
        nm, nv, step = _adamw_math(w_ref[...], g_ref[...], m_ref[...], v_ref[...])
        d_ref[...] = step
        nm_ref[...] = nm
        nv_ref[...] = nv

    spec = _rows(shape[-1], tile)
    out = jax.ShapeDtypeStruct(flat[0].shape, f32)
    res = pl.pallas_call(
        body, grid=(flat[0].shape[0] // tile,),
        in_specs=[spec] * 4, out_specs=[spec] * 3, out_shape=[out] * 3,
        name="adamw", compiler_params=_params(("arbitrary",)))(*flat)
    return [r.reshape(shape) for r in res]


def _adamw_layer(l, w, m, v, g, outs):
    depth, rows, cols = w.shape
    tile = _row_tile(rows, cols)

    def body(w_ref, m_ref, v_ref, g_ref, *refs):
        go_ref, d_ref, nm_ref, nv_ref = refs[4:]
        grad = g_ref[...]
        nm, nv, step = _adamw_math(w_ref[...], grad, m_ref[...], v_ref[...])
        go_ref[...] = grad
        d_ref[...] = step
        nm_ref[...] = nm
        nv_ref[...] = nv

    layer = pl.BlockSpec((None, tile, cols), lambda i: (l, i, 0))
    return pl.pallas_call(
        body, grid=(rows // tile,),
        in_specs=[layer] * 3 + [_rows(cols, tile)] + [ANY] * 4, out_specs=[layer] * 4,
        out_shape=[jax.ShapeDtypeStruct(w.shape, f32)] * 4,
        input_output_aliases={4 + k: k for k in range(4)},
        name="adamw_layer", compiler_params=_params(("arbitrary",)))(w, m, v, g, *outs)


def _reduce_to_owners(parts):
    got = _sibling_split(parts)
    chip_sums = [_add_pair(p, g) for p, g in zip(parts, got)]
    slots = _chip_scatter(chip_sums)
    return _sibling_join([_add_slots(cs, s) for cs, s in zip(chip_sums, slots)])


SMALL = ("w_conv", "w_pool", "pool_scale", "sgu_ln_g", "w_spatial", "b_spatial", "ln1_g", "ln1_b", "ln2_g", "ln2_b")
WEIGHTS = ("w_in", "w_conv", "w_pool", "pool_scale", "sgu_ln_g", "w_spatial", "b_spatial", "w_o", "ln1_g", "ln1_b",
           "w_gate_up", "w_down", "ln2_g", "ln2_b")
BIG = ("w_in", "w_o", "w_gate_up", "w_down")
BIG_AXES = (0, 0, 1, 0)
SMALL_ROWS = 4096


def _pack_small(arrays):
    flat = jnp.concatenate([a.reshape(-1) for a in arrays])
    return jnp.pad(flat, (0, SMALL_ROWS * LANES - flat.shape[0])).reshape(SMALL_ROWS, LANES)


def _unpack_small(packed, shapes):
    flat = packed.reshape(-1)
    out, at = [], 0
    for shp in shapes:
        size = 1
        for d in shp:
            size *= d
        out.append(flat[at:at + size].reshape(shp))
        at += size
    return out


def kernel(x, w_in, w_conv, w_pool, pool_scale, sgu_ln_g, w_spatial, b_spatial, w_o, ln1_g, ln1_b, w_gate_up, w_down, ln2_g, ln2_b, loss_target, m_w_in, m_w_conv, m_w_pool, m_pool_scale, m_sgu_ln_g, m_w_spatial, m_b_spatial, m_w_o, m_ln1_g, m_ln1_b, m_w_gate_up, m_w_down, m_ln2_g, m_ln2_b, v_w_in, v_w_conv, v_w_pool, v_pool_scale, v_sgu_ln_g, v_w_spatial, v_b_spatial, v_w_o, v_ln1_g, v_ln1_b, v_w_gate_up, v_w_down, v_ln2_g, v_ln2_b):
    weights = dict(w_in=w_in, w_conv=w_conv, w_pool=w_pool, pool_scale=pool_scale, sgu_ln_g=sgu_ln_g, w_spatial=w_spatial,
                   b_spatial=b_spatial, w_o=w_o, ln1_g=ln1_g, ln1_b=ln1_b, w_gate_up=w_gate_up, w_down=w_down, ln2_g=ln2_g, ln2_b=ln2_b)
    m_in = dict(w_in=m_w_in, w_conv=m_w_conv, w_pool=m_w_pool, pool_scale=m_pool_scale, sgu_ln_g=m_sgu_ln_g, w_spatial=m_w_spatial,
                b_spatial=m_b_spatial, w_o=m_w_o, ln1_g=m_ln1_g, ln1_b=m_ln1_b, w_gate_up=m_w_gate_up, w_down=m_w_down,
                ln2_g=m_ln2_g, ln2_b=m_ln2_b)
    v_in = dict(w_in=v_w_in, w_conv=v_w_conv, w_pool=v_w_pool, pool_scale=v_pool_scale, sgu_ln_g=v_sgu_ln_g, w_spatial=v_w_spatial,
                b_spatial=v_b_spatial, w_o=v_w_o, ln1_g=v_ln1_g, ln1_b=v_ln1_b, w_gate_up=v_w_gate_up, w_down=v_w_down,
                ln2_g=v_ln2_g, ln2_b=v_ln2_b)
    depth = w_in.shape[0]
    conv_cols = w_conv.shape[2]
    chip = _chip_index(lax.axis_index("x"), lax.axis_index("y"))

    conv_flat = jnp.pad(w_conv.reshape(-1), (0, 16 * LANES - w_conv.size)).reshape(1, 16, LANES)
    conv_full = _gather_shards([conv_flat])[0].reshape(N_CHIPS, 16 * LANES)[:, :w_conv.size].reshape(N_CHIPS, depth, 3, conv_cols)
    conv_full = conv_full.transpose(1, 2, 0, 3).reshape(depth, 3, N_CHIPS * conv_cols)

    big_w = dict(w_in=jnp.swapaxes(w_in, 1, 2), w_o=w_o, w_gate_up=w_gate_up, w_down=w_down)
    big_m = dict(w_in=jnp.swapaxes(m_w_in, 1, 2), w_o=m_w_o, w_gate_up=m_w_gate_up, w_down=m_w_down)
    big_v = dict(w_in=jnp.swapaxes(v_w_in, 1, 2), w_o=v_w_o, w_gate_up=v_w_gate_up, w_down=v_w_down)

    def send_layer(l, after):
        return _gather_start([big_w[n][l].astype(bf16) for n in BIG], BIG_AXES, after)

    def layer_weights(l, lands, shards):
        mats = dict(zip(BIG, _gather_finish(lands, shards, BIG_AXES)))
        return dict(
            mats, w_conv=conv_full[l], w_pool=w_pool[l], pool_scale=pool_scale[l][None], sgu_ln_g=sgu_ln_g[l][None],
            w_spatial=w_spatial[l], b_spatial=b_spatial[l][:, :, None], ln1_g=ln1_g[l][None], ln1_b=ln1_b[l][None],
            ln2_g=ln2_g[l][None], ln2_b=ln2_b[l][None])

    act = x[0]
    layers, saved = [], []
    flight, token = send_layer(0, conv_full)
    for l in range(depth):
        shards, lands, token = _gather_wait(flight, BIG_AXES, act)
        if l + 1 < depth:
            flight, token = send_layer(l + 1, token)
        layers.append(layer_weights(l, lands, shards))
        act, sv = _layer_fwd(act, layers[l], token)
        saved.append(sv)

    def reduce_start(l, after):
        parts = [grads[l][n] for n in BIG]
        got = _sibling_split_layer(parts, BIG_AXES)
        return _scatter_start([_add_pair_layer(p, g, ax) for p, g, ax in zip(parts, got, BIG_AXES)], after)

    def reduce_finish(l, flight, after, outs):
        sums, slots, _ = _scatter_wait(flight, after)
        summed = _sibling_join([_add_slots(cs[None], s[:, None]) for cs, s in zip(sums, slots)])
        return [_adamw_layer(l, big_w[n], big_m[n], big_v[n], g[0], o) for n, g, o in zip(BIG, summed, outs)]

    grad_x, sq = _loss_head(act, loss_target[0])
    grads = [None] * depth
    big_outs = [[lax.empty(big_w[n].shape, f32) for _ in range(4)] for n in BIG]
    flight = None
    for l in reversed(range(depth)):
        grad_x, grads[l] = _layer_bwd(grad_x, layers[l], saved[l], sq if flight is None else token)
        earlier = flight
        flight, token = reduce_start(l, grad_x)
        if earlier is not None:
            big_outs = reduce_finish(l + 1, earlier, token, big_outs)
    big_outs = reduce_finish(0, flight, big_outs[0][1], big_outs)
    loss = lax.psum(0.5 / D_MODEL * jnp.sum(sq), ("x", "y", "c"))

    small_shapes = [(depth,) + grads[0][n].shape for n in SMALL]
    small_part = _pack_small([jnp.stack([g[n] for g in grads]) for n in SMALL])
    summed = _reduce_to_owners([small_part.reshape(1, N_CHIPS, SMALL_ROWS // N_CHIPS, LANES)])
    small_sum = _gather_shards([summed[0]])[0].reshape(SMALL_ROWS, LANES)
    big_outs[0] = [jnp.swapaxes(o, 1, 2) for o in big_outs[0]]
    grad = {n: o[0] for n, o in zip(BIG, big_outs)}
    grad.update(zip(SMALL, _unpack_small(small_sum, small_shapes)))
    grad["pool_scale"] = grad["pool_scale"].reshape(pool_scale.shape)
    grad["sgu_ln_g"] = grad["sgu_ln_g"].reshape(sgu_ln_g.shape)
    for n in ("ln1_g", "ln1_b", "ln2_g", "ln2_b"):
        grad[n] = grad[n].reshape(ln1_g.shape)
    grad["w_conv"] = lax.dynamic_slice_in_dim(grad["w_conv"], chip * conv_cols, conv_cols, axis=2)

    delta = {n: o[1] for n, o in zip(BIG, big_outs)}
    new_m = {n: o[2] for n, o in zip(BIG, big_outs)}
    new_v = {n: o[3] for n, o in zip(BIG, big_outs)}
    delta["w_conv"], new_m["w_conv"], new_v["w_conv"] = _adamw(w_conv, grad["w_conv"], m_w_conv, v_w_conv)
    rest = [n for n in SMALL if n != "w_conv"]
    rest_shapes = [weights[n].shape for n in rest]
    packed = [_pack_small([src[n] for n in rest]) for src in (weights, grad, m_in, v_in)]
    for dst, res in zip((delta, new_m, new_v), _adamw(*packed)):
        dst.update(zip(rest, _unpack_small(res, rest_shapes)))

    return (loss, grad_x[None], *[grad[n] for n in WEIGHTS], *[delta[n] for n in WEIGHTS],
            *[new_m[n] for n in WEIGHTS], *[new_v[n] for n in WEIGHTS])
```

```python
import functools

import jax
import jax.numpy as jnp
from jax import lax
from jax.experimental import pallas as pl
from jax.experimental.pallas import tpu as pltpu

f32 = jnp.float32
bf16 = jnp.bfloat16

D_MODEL = 1024
DEPTH = 4
CONV_W = 384
POOL_W = 256
SGU_W = 384
IN_W = 3 * CONV_W + POOL_W + 2 * SGU_W
D_FF = 2816
CHUNK = 128
HEAD = 64
POOL_WINDOWS = (2, 4, 8, 16)
ALPHA = float((2 * DEPTH) ** 0.25)
LN_EPS = 1e-5
ADAM_LR = 0.001
ADAM_B1 = 0.9
ADAM_B2 = 0.999
ADAM_EPS = 1e-08
ADAM_WD = 0.01
ADAM_STEP = 10

LANES = 128
TOKEN_TILE = 256
N_CHIPS = 4
VMEM_LIMIT = 56 * 1024 * 1024

BLK_XA, BLK_GB, BLK_GC, BLK_P, BLK_U, BLK_V = 0, 3, 6, 9, 11, 14

MESH = pl.DeviceIdType.MESH


def _params(sem=None):
    return pltpu.CompilerParams(dimension_semantics=sem, vmem_limit_bytes=VMEM_LIMIT)


def _rows(width, tile=TOKEN_TILE):
    return pl.BlockSpec((tile, width), lambda i: (i, 0))


def _resident(shape):
    zeros = (0,) * len(shape)
    return pl.BlockSpec(shape, lambda *_: zeros, pipeline_mode=pl.Buffered(1))


def _nt(a, b):
    return lax.dot_general(a, b, (((1,), (1,)), ((), ())), preferred_element_type=f32)


def _tn(a, b):
    return lax.dot_general(a, b, (((0,), (0,)), ((), ())), preferred_element_type=f32)


def _mm(a, b):
    return jnp.dot(a, b, preferred_element_type=f32)


def _norm_fwd(z):
    mu = jnp.mean(z, axis=-1, keepdims=True)
    zc = z - mu
    var = jnp.mean(zc * zc, axis=-1, keepdims=True)
    rstd = lax.rsqrt(var + LN_EPS)
    return zc * rstd, rstd


def _norm_bwd(dxhat, xhat, rstd):
    m1 = jnp.mean(dxhat, axis=-1, keepdims=True)
    m2 = jnp.mean(dxhat * xhat, axis=-1, keepdims=True)
    return rstd * (dxhat - m1 - xhat * m2)


def _proj(x, w_in_b, after):
    s = x.shape[0]

    def body(x_ref, w_ref, after_ref, p_ref, xb_ref):
        xb = x_ref[...].astype(bf16)
        xb_ref[...] = xb
        p_ref[...] = _nt(xb, w_ref[...])

    return pl.pallas_call(
        body, grid=(s // TOKEN_TILE,),
        in_specs=[_rows(D_MODEL), _resident((IN_W, D_MODEL)), pl.BlockSpec(memory_space=pl.ANY)],
        out_specs=[_rows(IN_W), _rows(D_MODEL)],
        out_shape=[jax.ShapeDtypeStruct((s, IN_W), f32), jax.ShapeDtypeStruct((s, D_MODEL), bf16)],
        name="proj", compiler_params=_params(("arbitrary",)))(x, w_in_b, after)


def _row_ranges(parts):
    out, at = [], 0
    for p in parts:
        out.append((at, at + p.shape[1]))
        at += p.shape[1]
    return out


def _wo_ln1(mix, x, w_o_b, g, b):
    s = x.shape[0]
    n = len(mix)
    ranges = _row_ranges(mix)

    def body(*refs):
        m_refs = refs[:n]
        x_ref, w_ref, g_ref, b_ref, xhat_ref, rstd_ref, hb_ref = refs[n:]
        z = ALPHA * x_ref[...]
        for m_ref, (lo, hi) in zip(m_refs, ranges):
            z = z + _mm(m_ref[...], w_ref[lo:hi, :])
        xhat, rstd = _norm_fwd(z)
        xhat_ref[...] = xhat
        rstd_ref[...] = rstd
        hb_ref[...] = (xhat * g_ref[...] + b_ref[...]).astype(bf16)

    return pl.pallas_call(
        body, grid=(s // TOKEN_TILE,),
        in_specs=[_rows(m.shape[1]) for m in mix] + [_rows(D_MODEL), _resident((D_MODEL, D_MODEL)), _resident((1, D_MODEL)),
                                                     _resident((1, D_MODEL))],
        out_specs=[_rows(D_MODEL), _rows(1), _rows(D_MODEL)],
        out_shape=[jax.ShapeDtypeStruct((s, D_MODEL), f32), jax.ShapeDtypeStruct((s, 1), f32),
                   jax.ShapeDtypeStruct((s, D_MODEL), bf16)],
        name="wo_ln1", compiler_params=_params(("arbitrary",)))(*mix, x, w_o_b, g, b)


def _mlp_fwd(xhat1, g1, b1, w_gu_b, w_down_b, g2, b2, after):
    s = xhat1.shape[0]

    def body(xh_ref, g1_ref, b1_ref, wgu_ref, wd_ref, g2_ref, b2_ref, after_ref, gu_ref, xhat2_ref, rstd2_ref, y_ref):
        h = xh_ref[...] * g1_ref[...] + b1_ref[...]
        gu = _mm(h.astype(bf16), wgu_ref[...])
        gu_ref[...] = gu
        gate = gu[:, :D_FF]
        act = gate * jax.nn.sigmoid(gate) * gu[:, D_FF:]
        z = ALPHA * h + _mm(act.astype(bf16), wd_ref[...])
        xhat2, rstd2 = _norm_fwd(z)
        xhat2_ref[...] = xhat2
        rstd2_ref[...] = rstd2
        y_ref[...] = xhat2 * g2_ref[...] + b2_ref[...]

    vec = _resident((1, D_MODEL))
    return pl.pallas_call(
        body, grid=(s // TOKEN_TILE,),
        in_specs=[_rows(D_MODEL), vec, vec, _resident((D_MODEL, 2 * D_FF)), _resident((D_FF, D_MODEL)), vec, vec,
                  pl.BlockSpec(memory_space=pl.ANY)],
        out_specs=[_rows(2 * D_FF), _rows(D_MODEL), _rows(1), _rows(D_MODEL)],
        out_shape=[jax.ShapeDtypeStruct((s, 2 * D_FF), f32), jax.ShapeDtypeStruct((s, D_MODEL), f32),
                   jax.ShapeDtypeStruct((s, 1), f32), jax.ShapeDtypeStruct((s, D_MODEL), f32)],
        name="mlp_fwd", compiler_params=_params(("arbitrary",)))(xhat1, g1, b1, w_gu_b, w_down_b, g2, b2, after)


def _loss_head(y, target):
    s = y.shape[0]

    def body(y_ref, t_ref, dy_ref, sq_ref):
        @pl.when(pl.program_id(0) == 0)
        def _():
            sq_ref[...] = jnp.zeros_like(sq_ref)

        e = y_ref[...] - t_ref[...]
        dy_ref[...] = e * (1.0 / D_MODEL)
        sq_ref[...] += jnp.sum(e * e, axis=0, keepdims=True)

    return pl.pallas_call(
        body, grid=(s // TOKEN_TILE,),
        in_specs=[_rows(D_MODEL), _rows(D_MODEL)],
        out_specs=[_rows(D_MODEL), pl.BlockSpec((1, D_MODEL), lambda i: (0, 0))],
        out_shape=[jax.ShapeDtypeStruct((s, D_MODEL), f32), jax.ShapeDtypeStruct((1, D_MODEL), f32)],
        name="loss_head", compiler_params=_params(("arbitrary",)))(y, target)


def _mlp_bwd(dy, xhat2, rstd2, g2, gu, w_gu_b, w_down_b, after):
    s = dy.shape[0]

    def body(dy_ref, xh_ref, rs_ref, g2_ref, gu_ref, wgu_ref, wd_ref, after_ref, dz_ref, act_ref, dgu_ref, dh_ref, gg_ref, gb_ref):
        @pl.when(pl.program_id(0) == 0)
        def _():
            gg_ref[...] = jnp.zeros_like(gg_ref)
            gb_ref[...] = jnp.zeros_like(gb_ref)

        dy_t = dy_ref[...]
        xhat = xh_ref[...]
        gg_ref[...] += jnp.sum(dy_t * xhat, axis=0, keepdims=True)
        gb_ref[...] += jnp.sum(dy_t, axis=0, keepdims=True)
        dz = _norm_bwd(dy_t * g2_ref[...], xhat, rs_ref[...])
        dzb = dz.astype(bf16)
        dz_ref[...] = dzb
        dact = _nt(dzb, wd_ref[...])
        gate = gu_ref[:, :D_FF]
        up = gu_ref[:, D_FF:]
        sg = jax.nn.sigmoid(gate)
        silu = gate * sg
        act_ref[...] = (silu * up).astype(bf16)
        dgu_ref[:, :D_FF] = (dact * up * (sg * (1.0 + gate * (1.0 - sg)))).astype(bf16)
        dgu_ref[:, D_FF:] = (dact * silu).astype(bf16)
        dh_ref[...] = ALPHA * dz + _nt(dgu_ref[...], wgu_ref[...])

    vec_out = pl.BlockSpec((1, D_MODEL), lambda i: (0, 0))
    return pl.pallas_call(
        body, grid=(s // TOKEN_TILE,),
        in_specs=[_rows(D_MODEL), _rows(D_MODEL), _rows(1), _resident((1, D_MODEL)), _rows(2 * D_FF),
                  _resident((D_MODEL, 2 * D_FF)), _resident((D_FF, D_MODEL)), pl.BlockSpec(memory_space=pl.ANY)],
        out_specs=[_rows(D_MODEL), _rows(D_FF), _rows(2 * D_FF), _rows(D_MODEL), vec_out, vec_out],
        out_shape=[jax.ShapeDtypeStruct((s, D_MODEL), bf16), jax.ShapeDtypeStruct((s, D_FF), bf16),
                   jax.ShapeDtypeStruct((s, 2 * D_FF), bf16), jax.ShapeDtypeStruct((s, D_MODEL), f32),
                   jax.ShapeDtypeStruct((1, D_MODEL), f32), jax.ShapeDtypeStruct((1, D_MODEL), f32)],
        name="mlp_bwd", compiler_params=_params(("arbitrary",)))(dy, xhat2, rstd2, g2, gu, w_gu_b, w_down_b, after)


def _ln1_wo_bwd(dh, xhat1, rstd1, g1, w_o_b, after):
    s = dh.shape[0]

    def body(dh_ref, xh_ref, rs_ref, g1_ref, w_ref, after_ref, dz_ref, dzb_ref, dm_ref, gg_ref, gb_ref):
        @pl.when(pl.program_id(0) == 0)
        def _():
            gg_ref[...] = jnp.zeros_like(gg_ref)
            gb_ref[...] = jnp.zeros_like(gb_ref)

        dh_t = dh_ref[...]
        xhat = xh_ref[...]
        gg_ref[...] += jnp.sum(dh_t * xhat, axis=0, keepdims=True)
        gb_ref[...] += jnp.sum(dh_t, axis=0, keepdims=True)
        dz = _norm_bwd(dh_t * g1_ref[...], xhat, rs_ref[...])
        dz_ref[...] = dz
        dzb = dz.astype(bf16)
        dzb_ref[...] = dzb
        dm_ref[...] = _nt(dzb, w_ref[...])

    vec_out = pl.BlockSpec((1, D_MODEL), lambda i: (0, 0))
    return pl.pallas_call(
        body, grid=(s // TOKEN_TILE,),
        in_specs=[_rows(D_MODEL), _rows(D_MODEL), _rows(1), _resident((1, D_MODEL)), _resident((D_MODEL, D_MODEL)),
                  pl.BlockSpec(memory_space=pl.ANY)],
        out_specs=[_rows(D_MODEL), _rows(D_MODEL), _rows(D_MODEL), vec_out, vec_out],
        out_shape=[jax.ShapeDtypeStruct((s, D_MODEL), f32), jax.ShapeDtypeStruct((s, D_MODEL), bf16),
                   jax.ShapeDtypeStruct((s, D_MODEL), f32), jax.ShapeDtypeStruct((1, D_MODEL), f32),
                   jax.ShapeDtypeStruct((1, D_MODEL), f32)],
        name="ln1_wo_bwd", compiler_params=_params(("arbitrary",)))(dh, xhat1, rstd1, g1, w_o_b, after)


def _dx(dz1, dparts, w_in_t):
    s = dz1.shape[0]
    n = len(dparts)
    ranges = _row_ranges(dparts)

    def body(*refs):
        d_refs = refs[:n]
        dz_ref, w_ref, dx_ref = refs[n:]
        acc = ALPHA * dz_ref[...]
        for d_ref, (lo, hi) in zip(d_refs, ranges):
            acc = acc + _mm(d_ref[...], w_ref[lo:hi, :])
        dx_ref[...] = acc

    return pl.pallas_call(
        body, grid=(s // TOKEN_TILE,),
        in_specs=[_rows(d.shape[1]) for d in dparts] + [_rows(D_MODEL), _resident((IN_W, D_MODEL))],
        out_specs=_rows(D_MODEL),
        out_shape=jax.ShapeDtypeStruct((s, D_MODEL), f32),
        name="dx", compiler_params=_params(("arbitrary",)))(*dparts, dz1, w_in_t)


def _weight_grad_rows(parts, b, bn):
    s, n_cols = b.shape
    n = len(parts)
    ranges = _row_ranges(parts)
    m = ranges[-1][1]

    def body(*refs):
        p_refs = refs[:n]
        b_ref, o_ref = refs[n:]
        for p_ref, (lo, hi) in zip(p_refs, ranges):
            o_ref[lo:hi, :] = _tn(p_ref[...], b_ref[...]).astype(bf16)

    return pl.pallas_call(
        body, grid=(n_cols // bn,),
        in_specs=[_resident(p.shape) for p in parts] + [pl.BlockSpec((s, bn), lambda j: (0, j))],
        out_specs=pl.BlockSpec((m, bn), lambda j: (0, j)),
        out_shape=jax.ShapeDtypeStruct((m, n_cols), bf16),
        name="weight_grad_rows", compiler_params=_params(("arbitrary",)))(*parts, b)


def _weight_grad(a, b, bm, bn, after):
    s, m = a.shape
    n = b.shape[1]

    def body(a_ref, b_ref, after_ref, o_ref):
        o_ref[...] = _tn(a_ref[...], b_ref[...]).astype(bf16)

    return pl.pallas_call(
        body, grid=(m // bm, n // bn),
        in_specs=[pl.BlockSpec((s, bm), lambda i, j: (0, i)), pl.BlockSpec((s, bn), lambda i, j: (0, j)),
                  pl.BlockSpec(memory_space=pl.ANY)],
        out_specs=pl.BlockSpec((bm, bn), lambda i, j: (i, j)),
        out_shape=jax.ShapeDtypeStruct((m, n), bf16),
        name="weight_grad", compiler_params=_params(("arbitrary", "arbitrary")))(a, b, after)


def _shift_down(a, k):
    row = lax.broadcasted_iota(jnp.int32, a.shape, 0)
    return jnp.where(row >= k, pltpu.roll(a, k, 0), 0.0)


def _shift_up(a, k):
    n = a.shape[0]
    row = lax.broadcasted_iota(jnp.int32, a.shape, 0)
    return jnp.where(row < n - k, pltpu.roll(a, n - k, 0), 0.0)


def _slab(s, block):
    return pl.BlockSpec((s, LANES), lambda k: (0, block + k))


def _conv_y(z, w):
    return w[0:1, :] * _shift_down(z, 2) + w[1:2, :] * _shift_down(z, 1) + w[2:3, :] * z


def _conv_fwd(proj, w_conv):
    s = proj.shape[0]

    def body(xa_ref, gb_ref, gc_ref, w_ref, o_ref):
        z = gc_ref[...] * xa_ref[...]
        o_ref[...] = (gb_ref[...] * _conv_y(z, w_ref[...])).astype(bf16)

    return pl.pallas_call(
        body, grid=(CONV_W // LANES,),
        in_specs=[_slab(s, BLK_XA), _slab(s, BLK_GB), _slab(s, BLK_GC), pl.BlockSpec((3, LANES), lambda k: (0, k))],
        out_specs=_slab(s, 0),
        out_shape=jax.ShapeDtypeStruct((s, CONV_W), bf16),
        name="conv_fwd", compiler_params=_params(("arbitrary",)))(proj, proj, proj, w_conv)


def _conv_bwd(proj, dmix, w_conv, after):
    s = proj.shape[0]

    def body(xa_ref, gb_ref, gc_ref, dy_ref, w_ref, after_ref, dxa_ref, dgb_ref, dgc_ref, dw_ref):
        xa = xa_ref[...]
        gc = gc_ref[...]
        w = w_ref[...]
        z = gc * xa
        dya = dy_ref[...]
        dgb_ref[...] = (dya * _conv_y(z, w)).astype(bf16)
        dy = dya * gb_ref[...]
        dz = w[2:3, :] * dy + w[1:2, :] * _shift_up(dy, 1) + w[0:1, :] * _shift_up(dy, 2)
        dxa_ref[...] = (dz * gc).astype(bf16)
        dgc_ref[...] = (dz * xa).astype(bf16)
        dw_ref[0:1, :] = jnp.sum(dy * _shift_down(z, 2), axis=0, keepdims=True)
        dw_ref[1:2, :] = jnp.sum(dy * _shift_down(z, 1), axis=0, keepdims=True)
        dw_ref[2:3, :] = jnp.sum(dy * z, axis=0, keepdims=True)

    out = jax.ShapeDtypeStruct((s, CONV_W), bf16)
    return pl.pallas_call(
        body, grid=(CONV_W // LANES,),
        in_specs=[_slab(s, BLK_XA), _slab(s, BLK_GB), _slab(s, BLK_GC), _slab(s, 0), pl.BlockSpec((3, LANES), lambda k: (0, k)),
                  pl.BlockSpec(memory_space=pl.ANY)],
        out_specs=[_slab(s, 0), _slab(s, 0), _slab(s, 0), pl.BlockSpec((3, LANES), lambda k: (0, k))],
        out_shape=[out, out, out, jax.ShapeDtypeStruct((3, CONV_W), f32)],
        name="conv_bwd", compiler_params=_params(("arbitrary",)))(proj, proj, proj, dmix, w_conv, after)


def _pool_window(k):
    lane = lax.broadcasted_iota(jnp.int32, (1, LANES), 1)
    low = lane < HEAD
    first = k == 0
    wlen = jnp.where(low, jnp.where(first, POOL_WINDOWS[0], POOL_WINDOWS[2]), jnp.where(first, POOL_WINDOWS[1], POOL_WINDOWS[3]))
    return wlen, low, first


def _pool_diff(p, k):
    wlen, low, first = _pool_window(k)
    s2 = p + _shift_down(p, 1)
    s4 = s2 + _shift_down(s2, 2)
    s8 = s4 + _shift_down(s4, 4)
    s16 = s8 + _shift_down(s8, 8)
    win = jnp.where(low, jnp.where(first, s2, s8), jnp.where(first, s4, s16))
    row = lax.broadcasted_iota(jnp.int32, p.shape, 0)
    count = jnp.minimum(row + 1, wlen).astype(f32)
    return win / count - p, count


def _pool_weight(w_ref):
    zero = jnp.zeros((HEAD, HEAD), f32)
    top = jnp.concatenate([w_ref[0], zero], axis=1)
    bottom = jnp.concatenate([zero, w_ref[1]], axis=1)
    return jnp.concatenate([top, bottom], axis=0).astype(bf16)


def _pool_fwd(proj, w_pool, pool_scale):
    s = proj.shape[0]

    def body(p_ref, w_ref, sc_ref, o_ref):
        d, _ = _pool_diff(p_ref[...], pl.program_id(0))
        o_ref[...] = (_mm(d.astype(bf16), _pool_weight(w_ref)) * sc_ref[...]).astype(bf16)

    return pl.pallas_call(
        body, grid=(POOL_W // LANES,),
        in_specs=[_slab(s, BLK_P), pl.BlockSpec((2, HEAD, HEAD), lambda k: (k, 0, 0)), pl.BlockSpec((1, LANES), lambda k: (0, k))],
        out_specs=_slab(s, 0),
        out_shape=jax.ShapeDtypeStruct((s, POOL_W), bf16),
        name="pool_fwd", compiler_params=_params(("arbitrary",)))(proj, w_pool, pool_scale)


def _pool_bwd(proj, dmix, w_pool, pool_scale):
    s = proj.shape[0]

    def body(p_ref, dy_ref, w_ref, sc_ref, dp_ref, dw_ref, dsc_ref):
        k = pl.program_id(0)
        d, count = _pool_diff(p_ref[...], k)
        wbd = _pool_weight(w_ref)
        db = d.astype(bf16)
        dyb = dy_ref[...]
        dsc_ref[...] = jnp.sum(dyb * _mm(db, wbd), axis=0, keepdims=True)
        dpre = (dyb * sc_ref[...]).astype(bf16)
        dwbd = _tn(db, dpre)
        dw_ref[0] = dwbd[:HEAD, :HEAD]
        dw_ref[1] = dwbd[HEAD:, HEAD:]
        dd = _nt(dpre, wbd)
        e = dd / count
        wlen, low, first = _pool_window(k)
        a2 = e + _shift_up(e, 1)
        a4 = a2 + _shift_up(a2, 2)
        a8 = a4 + _shift_up(a4, 4)
        a16 = a8 + _shift_up(a8, 8)
        back = jnp.where(low, jnp.where(first, a2, a8), jnp.where(first, a4, a16))
        dp_ref[...] = (back - dd).astype(bf16)

    return pl.pallas_call(
        body, grid=(POOL_W // LANES,),
        in_specs=[_slab(s, BLK_P), _slab(s, CONV_W // LANES), pl.BlockSpec((2, HEAD, HEAD), lambda k: (k, 0, 0)),
                  pl.BlockSpec((1, LANES), lambda k: (0, k))],
        out_specs=[_slab(s, 0), pl.BlockSpec((2, HEAD, HEAD), lambda k: (k, 0, 0)), pl.BlockSpec((1, LANES), lambda k: (0, k))],
        out_shape=[jax.ShapeDtypeStruct((s, POOL_W), bf16), jax.ShapeDtypeStruct((4, HEAD, HEAD), f32),
                   jax.ShapeDtypeStruct((1, POOL_W), f32)],
        name="pool_bwd", compiler_params=_params(("arbitrary",)))(proj, dmix, w_pool, pool_scale)


INV_SQRT2 = 0.7071067811865476
INV_SQRT_2PI = 0.3989422804014327


def _gelu(x):
    return 0.5 * x * (1.0 + lax.erf(x * INV_SQRT2))


def _gelu_grad(x):
    return 0.5 * (1.0 + lax.erf(x * INV_SQRT2)) + x * (INV_SQRT_2PI * jnp.exp(-0.5 * x * x))


def _head_mean(a, low):
    s_low = jnp.sum(jnp.where(low, a, 0.0), axis=-1, keepdims=True)
    s_high = jnp.sum(jnp.where(low, 0.0, a), axis=-1, keepdims=True)
    return jnp.where(low, s_low, s_high) * (1.0 / HEAD)


def _tril():
    r = lax.broadcasted_iota(jnp.int32, (CHUNK, CHUNK), 0)
    c = lax.broadcasted_iota(jnp.int32, (CHUNK, CHUNK), 1)
    return r >= c


def _sgu_chunk(up, vp, g, wm0, wm1, b0, b1, low):
    ug = _gelu(up)
    vg = _gelu(vp)
    vc = vg - _head_mean(vg, low)
    rstd = lax.rsqrt(_head_mean(vc * vc, low) + LN_EPS)
    vn = vc * rstd
    vb = (vn * g).astype(bf16)
    mixed = jnp.where(low, _mm(wm0, vb) + b0, _mm(wm1, vb) + b1)
    return ug, vn, rstd, vb, mixed


def _sgu_specs(s):
    return [_slab(s, BLK_U), _slab(s, BLK_V), pl.BlockSpec((1, LANES), lambda k: (0, k)),
            pl.BlockSpec((2, CHUNK, CHUNK), lambda k: (k, 0, 0)), pl.BlockSpec((2, CHUNK, 1), lambda k: (k, 0, 0))]


def _sgu_fwd(proj, sgu_g, w_spatial, b_spatial3):
    s = proj.shape[0]

    def body(u_ref, v_ref, g_ref, w_ref, b_ref, o_ref):
        low = lax.broadcasted_iota(jnp.int32, (1, LANES), 1) < HEAD
        mask = _tril()
        wm0 = jnp.where(mask, w_ref[0], 0.0).astype(bf16)
        wm1 = jnp.where(mask, w_ref[1], 0.0).astype(bf16)
        g = g_ref[...]
        b0 = b_ref[0]
        b1 = b_ref[1]

        def chunk(n, carry):
            rows = pl.ds(pl.multiple_of(n * CHUNK, CHUNK), CHUNK)
            ug, _, _, _, mixed = _sgu_chunk(u_ref[rows, :], v_ref[rows, :], g, wm0, wm1, b0, b1, low)
            o_ref[rows, :] = (ug * mixed).astype(bf16)
            return carry

        lax.fori_loop(0, s // CHUNK, chunk, 0)

    return pl.pallas_call(
        body, grid=(SGU_W // LANES,),
        in_specs=_sgu_specs(s),
        out_specs=_slab(s, 0),
        out_shape=jax.ShapeDtypeStruct((s, SGU_W), bf16),
        name="sgu_fwd", compiler_params=_params(("arbitrary",)))(proj, proj, sgu_g, w_spatial, b_spatial3)


def _sgu_bwd(proj, dmix, sgu_g, w_spatial, b_spatial3):
    s = proj.shape[0]

    def body(u_ref, v_ref, g_ref, w_ref, b_ref, dy_ref, du_ref, dv_ref, dg_ref, dw_ref, db_ref):
        low = lax.broadcasted_iota(jnp.int32, (1, LANES), 1) < HEAD
        mask = _tril()
        w0 = jnp.where(mask, w_ref[0], 0.0)
        w1 = jnp.where(mask, w_ref[1], 0.0)
        wm0 = w0.astype(bf16)
        wm1 = w1.astype(bf16)
        wt0 = w0.T.astype(bf16)
        wt1 = w1.T.astype(bf16)
        g = g_ref[...]
        b0 = b_ref[0]
        b1 = b_ref[1]
        dg_ref[...] = jnp.zeros_like(dg_ref)
        dw_ref[...] = jnp.zeros_like(dw_ref)
        db_ref[...] = jnp.zeros_like(db_ref)

        def chunk(n, carry):
            rows = pl.ds(pl.multiple_of(n * CHUNK, CHUNK), CHUNK)
            up = u_ref[rows, :]
            vp = v_ref[rows, :]
            ug, vn, rstd, vb, mixed = _sgu_chunk(up, vp, g, wm0, wm1, b0, b1, low)
            dy = dy_ref[rows, :]
            du_ref[rows, :] = (dy * mixed * _gelu_grad(up)).astype(bf16)
            dmix_c = dy * ug
            db_ref[0] += jnp.sum(jnp.where(low, dmix_c, 0.0), axis=-1, keepdims=True)
            db_ref[1] += jnp.sum(jnp.where(low, 0.0, dmix_c), axis=-1, keepdims=True)
            dmb = dmix_c.astype(bf16)
            zero = jnp.zeros_like(dmb)
            dw_ref[0] += _nt(jnp.where(low, dmb, zero), vb)
            dw_ref[1] += _nt(jnp.where(low, zero, dmb), vb)
            dvnorm = jnp.where(low, _mm(wt0, dmb), _mm(wt1, dmb))
            dg_ref[...] += jnp.sum(dvnorm * vn, axis=0, keepdims=True)
            dvn = dvnorm * g
            dvg = rstd * (dvn - _head_mean(dvn, low) - vn * _head_mean(dvn * vn, low))
            dv_ref[rows, :] = (dvg * _gelu_grad(vp)).astype(bf16)
            return carry

        lax.fori_loop(0, s // CHUNK, chunk, 0)
        dw_ref[0] = jnp.where(mask, dw_ref[0], 0.0)
        dw_ref[1] = jnp.where(mask, dw_ref[1], 0.0)

    out = jax.ShapeDtypeStruct((s, SGU_W), bf16)
    return pl.pallas_call(
        body, grid=(SGU_W // LANES,),
        in_specs=_sgu_specs(s) + [_slab(s, (CONV_W + POOL_W) // LANES)],
        out_specs=[_slab(s, 0), _slab(s, 0), pl.BlockSpec((1, LANES), lambda k: (0, k)),
                   pl.BlockSpec((2, CHUNK, CHUNK), lambda k: (k, 0, 0)), pl.BlockSpec((2, CHUNK, 1), lambda k: (k, 0, 0))],
        out_shape=[out, out, jax.ShapeDtypeStruct((1, SGU_W), f32), jax.ShapeDtypeStruct((6, CHUNK, CHUNK), f32),
                   jax.ShapeDtypeStruct((6, CHUNK, 1), f32)],
        name="sgu_bwd", compiler_params=_params(("arbitrary",)))(proj, proj, sgu_g, w_spatial, b_spatial3, dmix)


def _fwd_mix(x, w, after):
    proj, xb = _proj(x, w["w_in"], after)
    mix = [_conv_fwd(proj, w["w_conv"]), _pool_fwd(proj, w["w_pool"], w["pool_scale"]),
           _sgu_fwd(proj, w["sgu_ln_g"], w["w_spatial"], w["b_spatial"])]
    xhat1, rstd1, hb = _wo_ln1(mix, x, w["w_o"], w["ln1_g"], w["ln1_b"])
    return dict(proj=proj, xb=xb, mix=mix, xhat1=xhat1, rstd1=rstd1, hb=hb)


def _fwd_mlp(sv, w, after):
    gu, xhat2, rstd2, y = _mlp_fwd(sv["xhat1"], w["ln1_g"], w["ln1_b"], w["w_gate_up"], w["w_down"], w["ln2_g"], w["ln2_b"], after)
    sv.update(gu=gu, xhat2=xhat2, rstd2=rstd2)
    return y


def _bwd_mlp(dy, w, sv, after, hook):
    dz2b, actb, dgub, dh, g_ln2_g, g_ln2_b = _mlp_bwd(dy, sv["xhat2"], sv["rstd2"], w["ln2_g"], sv["gu"], w["w_gate_up"],
                                                      w["w_down"], after)
    after = hook(dh)
    grads = dict(w_gate_up=_weight_grad(sv["hb"], dgub, 512, D_FF // 2, after),
                 w_down=_weight_grad(actb, dz2b, D_FF // 2, D_MODEL, after), ln2_g=g_ln2_g, ln2_b=g_ln2_b)
    return dh, grads


def _bwd_mix(dh, w, sv, after, hook):
    dz1, dz1b, dmix, g_ln1_g, g_ln1_b = _ln1_wo_bwd(dh, sv["xhat1"], sv["rstd1"], w["ln1_g"], w["w_o"], after)
    after = hook(dz1)
    dxa, dgb, dgc, g_conv = _conv_bwd(sv["proj"], dmix, w["w_conv"], after)
    dp, g_pool, g_pscale = _pool_bwd(sv["proj"], dmix, w["w_pool"], w["pool_scale"])
    du, dv, g_sgu_g, g_spatial, g_bsp = _sgu_bwd(sv["proj"], dmix, w["sgu_ln_g"], w["w_spatial"], w["b_spatial"])
    dparts = [dxa, dgb, dgc, dp, du, dv]
    dx = _dx(dz1, dparts, w["w_in"])
    grads = dict(
        w_in=_weight_grad_rows(dparts, sv["xb"], 512), w_o=_weight_grad_rows(sv["mix"], dz1b, D_MODEL),
        w_conv=g_conv, w_pool=g_pool, pool_scale=g_pscale, sgu_ln_g=g_sgu_g, w_spatial=g_spatial,
        b_spatial=g_bsp.reshape(6, CHUNK), ln1_g=g_ln1_g, ln1_b=g_ln1_b)
    return dx, grads


def _local_step(x, target, layers):
    saved = []
    for w in layers:
        sv = _fwd_mix(x, w, x)
        x = _fwd_mlp(sv, w, x)
        saved.append(sv)
    dy, sq = _loss_head(x, target)
    grads = [None] * len(layers)
    for l in reversed(range(len(layers))):
        dh, g_mlp = _bwd_mlp(dy, layers[l], saved[l], sq, lambda a: a)
        dy, g_mix = _bwd_mix(dh, layers[l], saved[l], dh, lambda a: a)
        grads[l] = dict(g_mlp, **g_mix)
    return sq, dy, grads


ANY = pl.BlockSpec(memory_space=pl.ANY)


def _place():
    x, y, c = lax.axis_index("x"), lax.axis_index("y"), lax.axis_index("c")
    others = [(1 - x, y), (x, 1 - y), (1 - x, 1 - y)]
    return x, y, c, others


def _chip_index(cx, cy):
    return 2 * cx + cy


def _half(ref_rows, c):
    half = ref_rows // 2
    return pl.ds(pl.multiple_of(c * half, 8), half)


def _remote(src, dst, send_sem, recv_sem, device):
    return pltpu.make_async_remote_copy(src_ref=src, dst_ref=dst, send_sem=send_sem, recv_sem=recv_sem,
                                        device_id=device, device_id_type=MESH)


def _gather_shards(shards):
    n = len(shards)
    base, total = [], 0
    for s in shards:
        base.append(total)
        total += 6 * s.shape[0]

    def body(*refs):
        ins, outs = refs[:n], refs[n:2 * n]
        send, recv = refs[2 * n:]
        x, y, c, others = _place()
        me = _chip_index(x, y)
        sib = (x, y, 1 - c)
        sends = []
        for f in range(n):
            depth, rows = ins[f].shape[0], ins[f].shape[1]
            for l in range(depth):
                for k, (cx, cy) in enumerate(others):
                    sem = base[f] + 6 * l + k
                    cp = _remote(ins[f].at[l, _half(rows, c)], outs[f].at[l, me, _half(rows, c)],
                                 send.at[sem], recv.at[sem], (cx, cy, c))
                    cp.start()
                    sends.append(cp)
        for f in range(n):
            depth, rows = ins[f].shape[0], ins[f].shape[1]
            for l in range(depth):
                for k, (cx, cy) in enumerate(others):
                    sem = base[f] + 6 * l + k
                    landed = outs[f].at[l, _chip_index(cx, cy), _half(rows, c)]
                    _remote(landed, landed, send.at[sem], recv.at[sem], (cx, cy, c)).wait_recv()
                    cp = _remote(landed, landed, send.at[sem + 3], recv.at[sem + 3], sib)
                    cp.start()
                    sends.append(cp)
        for f in range(n):
            depth, rows = ins[f].shape[0], ins[f].shape[1]
            for l in range(depth):
                for k, (cx, cy) in enumerate(others):
                    sem = base[f] + 6 * l + k + 3
                    passed = outs[f].at[l, _chip_index(cx, cy), _half(rows, 1 - c)]
                    _remote(passed, passed, send.at[sem], recv.at[sem], sib).wait_recv()
        for cp in sends:
            cp.wait_send()

    gathered = pl.pallas_call(
        body, in_specs=[ANY] * n, out_specs=[ANY] * n,
        out_shape=[jax.ShapeDtypeStruct((s.shape[0], N_CHIPS) + s.shape[1:], s.dtype) for s in shards],
        scratch_shapes=[pltpu.SemaphoreType.DMA((total,)), pltpu.SemaphoreType.DMA((total,))],
        name="gather_shards")(*shards)
    return [_place_own(g, s) for g, s in zip(gathered, shards)]


def _scalar(value):
    return jnp.reshape(value, (1,)).astype(jnp.int32)


def _place_own(blocks, shard):
    depth, rows, cols = shard.shape

    def body(me_ref, b_ref, s_ref, o_ref):
        o_ref[...] = s_ref[...]

    return pl.pallas_call(
        body,
        grid_spec=pltpu.PrefetchScalarGridSpec(
            num_scalar_prefetch=1, grid=(depth,),
            in_specs=[ANY, pl.BlockSpec((None, rows, cols), lambda l, me: (l, 0, 0))],
            out_specs=pl.BlockSpec((None, None, rows, cols), lambda l, me: (l, me[0], 0, 0))),
        out_shape=jax.ShapeDtypeStruct(blocks.shape, blocks.dtype),
        input_output_aliases={1: 0},
        name="place_own", compiler_params=_params(("arbitrary",)))(
            _scalar(_chip_index(lax.axis_index("x"), lax.axis_index("y"))), blocks, shard)


HBM = pl.BlockSpec(memory_space=pltpu.HBM)
SEM = pl.BlockSpec(memory_space=pltpu.SEMAPHORE)
TOKEN = jax.ShapeDtypeStruct((8, LANES), f32)
SPLIT_COPY = pltpu.CompilerParams(has_side_effects=pltpu.SideEffectType.DATAFLOW_SIDE_EFFECTING)


def _in_hbm(a):
    return pltpu.with_memory_space_constraint(a, pltpu.HBM)


def _full_shape(shard, axis):
    rows, cols = shard.shape
    return (N_CHIPS * rows, cols) if axis == 0 else (rows, N_CHIPS * cols)


def _block_half(ref, axis, j, h):
    if axis == 0:
        rows = ref.shape[0] // N_CHIPS
        return ref.at[pl.ds(pl.multiple_of(j * rows + h * (rows // 2), 16), rows // 2), :]
    half, cols = ref.shape[0] // 2, ref.shape[1] // N_CHIPS
    return ref.at[pl.ds(pl.multiple_of(h * half, 16), half), pl.ds(pl.multiple_of(j * cols, LANES), cols)]


def _gather_start(shards, axes, after):
    lands = [lax.empty(_full_shape(s, ax), s.dtype) for s, ax in zip(shards, axes)]
    return _split_copy_start("gather", _gather_plan(axes), 3 * len(shards), shards, lands, after)


def _gather_wait(state, axes, after):
    return _split_copy_wait("gather", _gather_plan(axes), state, after)


def _split_copy_start(name, plan, count, ins, lands, after):
    arrays = list(ins) + list(lands)
    n_in, n = len(ins), len(arrays)

    def body(*refs):
        send, recv, token = refs[n + 1], refs[n + 2], refs[-1]
        for i, (src, dst, _, peer) in enumerate(plan(refs[:n_in], refs[n_in:n])):
            _remote(src, dst, send.at[i], recv.at[i], peer).start()
        token[...] = jnp.zeros_like(token)

    outs = pl.pallas_call(
        body, name=name + "_start",
        in_specs=[HBM] * n + [ANY],
        out_specs=(SEM, SEM, *[HBM] * n, pl.BlockSpec(memory_space=pltpu.VMEM)),
        out_shape=(pltpu.SemaphoreType.DMA((count,)), pltpu.SemaphoreType.DMA((count,)),
                   *[pltpu.HBM(a.shape, a.dtype) for a in arrays], TOKEN),
        input_output_aliases={i: 2 + i for i in range(n)},
        compiler_params=SPLIT_COPY)(*[_in_hbm(a) for a in arrays], after)
    return (outs[0], outs[1], outs[2:2 + n_in], outs[2 + n_in:2 + n]), outs[-1]


def _split_copy_wait(name, plan, state, after):
    send_sems, recv_sems, ins, lands = state
    arrays = list(ins) + list(lands)
    n_in, n = len(ins), len(arrays)

    def body(*refs):
        send, recv, token = refs[n], refs[n + 1], refs[-1]
        for i, (src, _, landing, peer) in enumerate(plan(refs[:n_in], refs[n_in:n])):
            cp = _remote(src, landing, send.at[i], recv.at[i], peer)
            cp.wait_send()
            cp.wait_recv()
        token[...] = jnp.zeros_like(token)

    outs = pl.pallas_call(
        body, name=name + "_wait",
        in_specs=[HBM] * n + [SEM, SEM, ANY],
        out_specs=(*[HBM] * n, pl.BlockSpec(memory_space=pltpu.VMEM)),
        out_shape=(*[pltpu.HBM(a.shape, a.dtype) for a in arrays], TOKEN),
        input_output_aliases={i: i for i in range(n)},
        compiler_params=SPLIT_COPY)(*arrays, send_sems, recv_sems, after)
    return outs[:n_in], outs[n_in:n], outs[-1]


def _gather_plan(axes):
    def plan(ins, lnd):
        x, y, c, others = _place()
        me = _chip_index(x, y)
        return [(ins[f].at[_half(ins[f].shape[0], c)], _block_half(lnd[f], ax, me, c),
                 _block_half(lnd[f], ax, _chip_index(cx, cy), c), (cx, cy, c))
                for f, ax in enumerate(axes) for cx, cy in others]
    return plan


def _pair_plan(axes):
    def plan(ins, lnd):
        x, y, c, _ = _place()
        return [(_block_half(ins[f], ax, j, 1 - c), lnd[f].at[j], lnd[f].at[j], (x, y, 1 - c))
                for f, ax in enumerate(axes) for j in range(N_CHIPS)]
    return plan


def _scatter_plan(ins, lnd):
    x, y, c, others = _place()
    return [(ins[f].at[_chip_index(cx, cy)], lnd[f].at[k], lnd[f].at[k], (cx, cy, c))
            for f in range(len(ins)) for k, (cx, cy) in enumerate(others)]


def _join_plan(ins, lnd):
    x, y, c, _ = _place()
    return [(lnd[f].at[_half(lnd[f].shape[0], c)], lnd[f].at[_half(lnd[f].shape[0], c)],
             lnd[f].at[_half(lnd[f].shape[0], 1 - c)], (x, y, 1 - c)) for f in range(len(lnd))]


def _gather_finish(lands, shards, axes):
    n = len(lands)

    def body(*refs):
        outs = refs[n:2 * n]
        send, recv = refs[2 * n:]
        x, y, c, others = _place()
        sib = (x, y, 1 - c)
        sends = []
        for f in range(n):
            for k, (cx, cy) in enumerate(others):
                landed = _block_half(outs[f], axes[f], _chip_index(cx, cy), c)
                cp = _remote(landed, landed, send.at[3 * f + k], recv.at[3 * f + k], sib)
                cp.start()
                sends.append(cp)
        for f in range(n):
            for k, (cx, cy) in enumerate(others):
                passed = _block_half(outs[f], axes[f], _chip_index(cx, cy), 1 - c)
                _remote(passed, passed, send.at[3 * f + k], recv.at[3 * f + k], sib).wait_recv()
        for cp in sends:
            cp.wait_send()

    full = pl.pallas_call(
        body, in_specs=[ANY] * n, out_specs=[ANY] * n,
        out_shape=[jax.ShapeDtypeStruct(a.shape, a.dtype) for a in lands],
        input_output_aliases={f: f for f in range(n)},
        scratch_shapes=[pltpu.SemaphoreType.DMA((3 * n,)), pltpu.SemaphoreType.DMA((3 * n,))],
        name="gather_finish")(*lands)

    def place(me_ref, *refs):
        ins, outs = refs[n:2 * n], refs[2 * n:]
        for f in range(n):
            outs[f][...] = ins[f][...]

    return pl.pallas_call(
        place,
        grid_spec=pltpu.PrefetchScalarGridSpec(
            num_scalar_prefetch=1, grid=(1,),
            in_specs=[ANY] * n + [pl.BlockSpec(s.shape, lambda i, me: (0, 0)) for s in shards],
            out_specs=[pl.BlockSpec(s.shape, (lambda i, me: (me[0], 0)) if ax == 0 else (lambda i, me: (0, me[0])))
                       for s, ax in zip(shards, axes)]),
        out_shape=[jax.ShapeDtypeStruct(a.shape, a.dtype) for a in full],
        input_output_aliases={1 + f: f for f in range(n)},
        name="place_own_layer", compiler_params=_params(("arbitrary",)))(
            _scalar(_chip_index(lax.axis_index("x"), lax.axis_index("y"))), *full, *shards)


def _half_blocks(part, axis):
    rows, cols = (part.shape[0] // N_CHIPS, part.shape[1]) if axis == 0 else (part.shape[0], part.shape[1] // N_CHIPS)
    return lax.empty((N_CHIPS, rows // 2, cols), part.dtype)


def _add_pair_layer(part, got, axis):
    _, half, cols = got.shape

    def body(c_ref, a_ref, b_ref, o_ref):
        o_ref[...] = (a_ref[...].astype(f32) + b_ref[...].astype(f32)).astype(o_ref.dtype)

    if axis == 0:
        part = part.reshape(N_CHIPS, 2, half, cols)
        mine = pl.BlockSpec((None, None, half, cols), lambda j, c: (j, c[0], 0, 0))
    else:
        mine = pl.BlockSpec((half, cols), lambda j, c: (c[0], j))
    return pl.pallas_call(
        body,
        grid_spec=pltpu.PrefetchScalarGridSpec(
            num_scalar_prefetch=1, grid=(N_CHIPS,),
            in_specs=[mine, pl.BlockSpec((None, half, cols), lambda j, c: (j, 0, 0))],
            out_specs=pl.BlockSpec((None, half, cols), lambda j, c: (j, 0, 0))),
        out_shape=jax.ShapeDtypeStruct(got.shape, part.dtype),
        name="add_pair_layer", compiler_params=_params(("arbitrary",)))(_scalar(lax.axis_index("c")), part, got)


def _scatter_start(sums, after):
    lands = [lax.empty((3,) + s.shape[1:], s.dtype) for s in sums]
    return _split_copy_start("scatter", _scatter_plan, 3 * len(sums), sums, lands, after)


def _scatter_wait(state, after):
    return _split_copy_wait("scatter", _scatter_plan, state, after)


def _sibling_split(parts):
    n = len(parts)

    def body(*refs):
        ins, got = refs[:n], refs[n:2 * n]
        send, recv = refs[2 * n:]
        x, y, c, _ = _place()
        sib = (x, y, 1 - c)
        for f in range(n):
            depth, rows = ins[f].shape[0], ins[f].shape[2]
            for l in range(depth):
                for j in range(N_CHIPS):
                    _remote(ins[f].at[l, j, _half(rows, 1 - c)], got[f].at[l, j], send.at[f], recv.at[f], sib).start()
        for f in range(n):
            _remote(got[f], got[f], send.at[f], recv.at[f], sib).wait()

    return pl.pallas_call(
        body, in_specs=[ANY] * n, out_specs=[ANY] * n,
        out_shape=[jax.ShapeDtypeStruct(p.shape[:2] + (p.shape[2] // 2, p.shape[3]), p.dtype) for p in parts],
        scratch_shapes=[pltpu.SemaphoreType.DMA((n,)), pltpu.SemaphoreType.DMA((n,))],
        name="sibling_split")(*parts)


def _chip_scatter(sums):
    n = len(sums)

    def body(*refs):
        ins, outs = refs[:n], refs[n:2 * n]
        send, recv = refs[2 * n:]
        x, y, c, others = _place()
        for f in range(n):
            for l in range(ins[f].shape[0]):
                for k, (cx, cy) in enumerate(others):
                    _remote(ins[f].at[l, _chip_index(cx, cy)], outs[f].at[k, l], send.at[f * 3 + k], recv.at[f * 3 + k],
                            (cx, cy, c)).start()
        for f in range(n):
            for k, (cx, cy) in enumerate(others):
                _remote(outs[f].at[k], outs[f].at[k], send.at[f * 3 + k], recv.at[f * 3 + k], (cx, cy, c)).wait()

    return pl.pallas_call(
        body, in_specs=[ANY] * n, out_specs=[ANY] * n,
        out_shape=[jax.ShapeDtypeStruct((3, s.shape[0]) + s.shape[2:], s.dtype) for s in sums],
        scratch_shapes=[pltpu.SemaphoreType.DMA((3 * n,)), pltpu.SemaphoreType.DMA((3 * n,))],
        name="chip_scatter")(*sums)


def _sibling_join(sums):
    n = len(sums)

    def body(*refs):
        ins, outs = refs[:n], refs[n:2 * n]
        send, recv = refs[2 * n:]
        x, y, c, _ = _place()
        sib = (x, y, 1 - c)
        for f in range(n):
            depth, half_rows = outs[f].shape[0], outs[f].shape[1] // 2
            pieces = 4 if half_rows % 64 == 0 else 1
            step = half_rows // pieces
            for l in range(depth):
                for p in range(pieces):
                    mine = outs[f].at[l, pl.ds(pl.multiple_of(c * half_rows + p * step, 8), step)]
                    _remote(mine, mine, send.at[f], recv.at[f], sib).start()
        for f in range(n):
            half = outs[f].at[:, pl.ds(0, outs[f].shape[1] // 2)]
            _remote(half, half, send.at[f], recv.at[f], sib).wait()

    return pl.pallas_call(
        body, in_specs=[ANY] * n, out_specs=[ANY] * n,
        out_shape=[jax.ShapeDtypeStruct(s.shape, s.dtype) for s in sums],
        input_output_aliases={f: f for f in range(n)},
        scratch_shapes=[pltpu.SemaphoreType.DMA((n,)), pltpu.SemaphoreType.DMA((n,))],
        name="sibling_join")(*sums)


ELEMENTWISE_BLOCK_BYTES = 1 << 20


def _row_tile(rows, cols):
    best = None
    for tile in range(8, rows + 1, 8):
        if rows % tile == 0 and tile * cols * 4 <= ELEMENTWISE_BLOCK_BYTES:
            best = tile
    return best or rows


def _add_pair(part, got):
    depth, chips, rows, cols = part.shape
    half = rows // 2

    def body(c_ref, a_ref, b_ref, o_ref):
        o_ref[...] = (a_ref[...].astype(f32) + b_ref[...].astype(f32)).astype(o_ref.dtype)

    return pl.pallas_call(
        body,
        grid_spec=pltpu.PrefetchScalarGridSpec(
            num_scalar_prefetch=1, grid=(depth, chips),
            in_specs=[pl.BlockSpec((None, None, None, half, cols), lambda l, j, c: (l, j, c[0], 0, 0)),
                      pl.BlockSpec((None, None, half, cols), lambda l, j, c: (l, j, 0, 0))],
            out_specs=pl.BlockSpec((None, None, half, cols), lambda l, j, c: (l, j, 0, 0))),
        out_shape=jax.ShapeDtypeStruct(got.shape, part.dtype),
        name="add_pair", compiler_params=_params(("arbitrary", "arbitrary")))(
            _scalar(lax.axis_index("c")), part.reshape(depth, chips, 2, half, cols), got)


def _add_slots(chip_sums, slots):
    depth, _, half, cols = chip_sums.shape

    def body(at_ref, own_ref, s_ref, o_ref):
        acc = own_ref[...].astype(f32)
        for k in range(3):
            acc = acc + s_ref[k].astype(f32)
        o_ref[...] = acc

    at = jnp.concatenate([_scalar(_chip_index(lax.axis_index("x"), lax.axis_index("y"))), _scalar(lax.axis_index("c"))])
    out = pl.pallas_call(
        body,
        grid_spec=pltpu.PrefetchScalarGridSpec(
            num_scalar_prefetch=1, grid=(depth,),
            in_specs=[pl.BlockSpec((None, None, half, cols), lambda l, at: (l, at[0], 0, 0)),
                      pl.BlockSpec((3, None, half, cols), lambda l, at: (0, l, 0, 0))],
            out_specs=pl.BlockSpec((None, None, half, cols), lambda l, at: (l, at[1], 0, 0))),
        out_shape=jax.ShapeDtypeStruct((depth, 2, half, cols), f32),
        name="add_slots", compiler_params=_params(("arbitrary",)))(at, chip_sums, slots)
    return out.reshape(depth, 2 * half, cols)


def _adamw_math(w, grad, m, v):
    nm = ADAM_B1 * m + (1.0 - ADAM_B1) * grad
    nv = ADAM_B2 * v + (1.0 - ADAM_B2) * (grad * grad)
    m_hat = nm / (1.0 - ADAM_B1 ** ADAM_STEP)
    v_hat = nv / (1.0 - ADAM_B2 ** ADAM_STEP)
    return nm, nv, -ADAM_LR * (m_hat / (jnp.sqrt(v_hat) + ADAM_EPS) + ADAM_WD * w)


def _adamw(w, g, m, v):
    shape = w.shape
    flat = [a.reshape(-1, shape[-1]) for a in (w, g, m, v)]
    tile = _row_tile(flat[0].shape[0], shape[-1])

    def body(w_ref, g_ref, m_ref, v_ref, d_ref, nm_ref, nv_ref):
        nm, nv, step = _adamw_math(w_ref[...], g_ref[...], m_ref[...], v_ref[...])
        d_ref[...] = step
        nm_ref[...] = nm
        nv_ref[...] = nv

    spec = _rows(shape[-1], tile)
    out = jax.ShapeDtypeStruct(flat[0].shape, f32)
    res = pl.pallas_call(
        body, grid=(flat[0].shape[0] // tile,),
        in_specs=[spec] * 4, out_specs=[spec] * 3, out_shape=[out] * 3,
        name="adamw", compiler_params=_params(("arbitrary",)))(*flat)
    return [r.reshape(shape) for r in res]


def _adamw_layer(l, w, m, v, g, outs):
    depth, rows, cols = w.shape
    tile = _row_tile(rows, cols)

    def body(w_ref, m_ref, v_ref, g_ref, *refs):
        go_ref, d_ref, nm_ref, nv_ref = refs[4:]
        grad = g_ref[...]
        nm, nv, step = _adamw_math(w_ref[...], grad, m_ref[...], v_ref[...])
        go_ref[...] = grad
        d_ref[...] = step
        nm_ref[...] = nm
        nv_ref[...] = nv

    layer = pl.BlockSpec((None, tile, cols), lambda i: (l, i, 0))
    return pl.pallas_call(
        body, grid=(rows // tile,),
        in_specs=[layer] * 3 + [_rows(cols, tile)] + [ANY] * 4, out_specs=[layer] * 4,
        out_shape=[jax.ShapeDtypeStruct(w.shape, f32)] * 4,
        input_output_aliases={4 + k: k for k in range(4)},
        name="adamw_layer", compiler_params=_params(("arbitrary",)))(w, m, v, g, *outs)


SMALL = ("w_conv", "w_pool", "pool_scale", "sgu_ln_g", "w_spatial", "b_spatial", "ln1_g", "ln1_b", "ln2_g", "ln2_b")
WEIGHTS = ("w_in", "w_conv", "w_pool", "pool_scale", "sgu_ln_g", "w_spatial", "b_spatial", "w_o", "ln1_g", "ln1_b",
           "w_gate_up", "w_down", "ln2_g", "ln2_b")
BIG = ("w_in", "w_o", "w_gate_up", "w_down")
GROUPS = (("w_in", "w_o"), ("w_gate_up", "w_down"))
GROUP_AXES = ((0, 0), (1, 0))
SCATTER_HOOKS = (1, 2)
SMALL_LAYER_ROWS = 1024


def _pack_layer(arrays):
    flat = jnp.concatenate([a.reshape(-1) for a in arrays])
    return jnp.pad(flat, (0, SMALL_LAYER_ROWS * LANES - flat.shape[0])).reshape(SMALL_LAYER_ROWS, LANES)


def _unpack_layers(flat, shapes):
    out, at = {}, 0
    for name, shape in shapes.items():
        size = 1
        for d in shape:
            size *= d
        out[name] = flat[:, at:at + size].reshape((flat.shape[0],) + tuple(shape))
        at += size
    return out


def kernel(x, w_in, w_conv, w_pool, pool_scale, sgu_ln_g, w_spatial, b_spatial, w_o, ln1_g, ln1_b, w_gate_up, w_down, ln2_g, ln2_b, loss_target, m_w_in, m_w_conv, m_w_pool, m_pool_scale, m_sgu_ln_g, m_w_spatial, m_b_spatial, m_w_o, m_ln1_g, m_ln1_b, m_w_gate_up, m_w_down, m_ln2_g, m_ln2_b, v_w_in, v_w_conv, v_w_pool, v_pool_scale, v_sgu_ln_g, v_w_spatial, v_b_spatial, v_w_o, v_ln1_g, v_ln1_b, v_w_gate_up, v_w_down, v_ln2_g, v_ln2_b):
    weights = dict(w_in=w_in, w_conv=w_conv, w_pool=w_pool, pool_scale=pool_scale, sgu_ln_g=sgu_ln_g, w_spatial=w_spatial,
                   b_spatial=b_spatial, w_o=w_o, ln1_g=ln1_g, ln1_b=ln1_b, w_gate_up=w_gate_up, w_down=w_down, ln2_g=ln2_g, ln2_b=ln2_b)
    m_in = dict(w_in=m_w_in, w_conv=m_w_conv, w_pool=m_w_pool, pool_scale=m_pool_scale, sgu_ln_g=m_sgu_ln_g, w_spatial=m_w_spatial,
                b_spatial=m_b_spatial, w_o=m_w_o, ln1_g=m_ln1_g, ln1_b=m_ln1_b, w_gate_up=m_w_gate_up, w_down=m_w_down,
                ln2_g=m_ln2_g, ln2_b=m_ln2_b)
    v_in = dict(w_in=v_w_in, w_conv=v_w_conv, w_pool=v_w_pool, pool_scale=v_pool_scale, sgu_ln_g=v_sgu_ln_g, w_spatial=v_w_spatial,
                b_spatial=v_b_spatial, w_o=v_w_o, ln1_g=v_ln1_g, ln1_b=v_ln1_b, w_gate_up=v_w_gate_up, w_down=v_w_down,
                ln2_g=v_ln2_g, ln2_b=v_ln2_b)
    depth = w_in.shape[0]
    conv_cols = w_conv.shape[2]
    chip = _chip_index(lax.axis_index("x"), lax.axis_index("y"))

    conv_flat = jnp.pad(w_conv.reshape(-1), (0, 16 * LANES - w_conv.size)).reshape(1, 16, LANES)
    conv_full = _gather_shards([conv_flat])[0].reshape(N_CHIPS, 16 * LANES)[:, :w_conv.size].reshape(N_CHIPS, depth, 3, conv_cols)
    conv_full = conv_full.transpose(1, 2, 0, 3).reshape(depth, 3, N_CHIPS * conv_cols)

    big_w = dict(w_in=jnp.swapaxes(w_in, 1, 2), w_o=w_o, w_gate_up=w_gate_up, w_down=w_down)
    big_m = dict(w_in=jnp.swapaxes(m_w_in, 1, 2), w_o=m_w_o, w_gate_up=m_w_gate_up, w_down=m_w_down)
    big_v = dict(w_in=jnp.swapaxes(v_w_in, 1, 2), w_o=v_w_o, w_gate_up=v_w_gate_up, w_down=v_w_down)

    def send(l, g, after):
        return _gather_start([big_w[n][l].astype(bf16) for n in GROUPS[g]], GROUP_AXES[g], after)

    def receive(g, flight, after):
        shards, lands, token = _gather_wait(flight, GROUP_AXES[g], after)
        return shards, lands, token

    act = x[0]
    layers, saved = [], []
    flight, token = send(0, 0, conv_full)
    for l in range(depth):
        w = dict(w_conv=conv_full[l], w_pool=w_pool[l], pool_scale=pool_scale[l][None], sgu_ln_g=sgu_ln_g[l][None],
                 w_spatial=w_spatial[l], b_spatial=b_spatial[l][:, :, None], ln1_g=ln1_g[l][None], ln1_b=ln1_b[l][None],
                 ln2_g=ln2_g[l][None], ln2_b=ln2_b[l][None])
        shards, lands, token = receive(0, flight, act)
        flight, token = send(l, 1, token)
        w.update(zip(GROUPS[0], _gather_finish(lands, shards, GROUP_AXES[0])))
        sv = _fwd_mix(act, w, token)
        shards, lands, token = receive(1, flight, sv["xhat1"])
        if l + 1 < depth:
            flight, token = send(l + 1, 0, token)
        w.update(zip(GROUPS[1], _gather_finish(lands, shards, GROUP_AXES[1])))
        act = _fwd_mlp(sv, w, token)
        layers.append(w)
        saved.append(sv)

    big_outs = {n: [lax.empty(big_w[n].shape, f32) for _ in range(4)] for n in BIG}
    small_sums = [None] * depth
    pending = []
    latest = dict(token=None)

    def begin(l, g, parts):
        axes = GROUP_AXES[g] + (0,) * (len(parts) - len(GROUPS[g]))
        lands = [_half_blocks(p, ax) for p, ax in zip(parts, axes)]
        flight, latest["token"] = _split_copy_start("pair", _pair_plan(axes), N_CHIPS * len(parts), parts, lands, latest["token"])
        pending.append(dict(l=l, g=g, axes=axes, step="pair", age=0, flight=flight))

    def advance(st, recent):
        if st["step"] == "pair":
            parts, got, _ = _split_copy_wait("pair", _pair_plan(st["axes"]), st["flight"], recent)
            sums = [_add_pair_layer(p, q, ax) for p, q, ax in zip(parts, got, st["axes"])]
            st["flight"], latest["token"] = _scatter_start(sums, latest["token"])
            st["step"] = "scatter"
        elif st["step"] == "scatter":
            sums, slots, _ = _scatter_wait(st["flight"], recent)
            filled = [_add_slots(cs[None], s[:, None])[0] for cs, s in zip(sums, slots)]
            st["flight"], latest["token"] = _split_copy_start("join", _join_plan, len(filled), [], filled, latest["token"])
            st["step"] = "join"
        else:
            _, summed, _ = _split_copy_wait("join", _join_plan, st["flight"], recent)
            for n, total in zip(GROUPS[st["g"]], summed):
                big_outs[n] = _adamw_layer(st["l"], big_w[n], big_m[n], big_v[n], total, big_outs[n])
            if st["g"] == 0:
                small_sums[st["l"]] = summed[-1]
            st["step"] = "done"
        st["age"] = 0

    def hook(recent):
        for st in list(pending):
            st["age"] += 1
            if st["age"] >= SCATTER_HOOKS[st["g"]] or st["step"] != "scatter":
                advance(st, recent)
                if st["step"] == "done":
                    pending.remove(st)
        return latest["token"]

    grad_x, sq = _loss_head(act, loss_target[0])
    latest["token"] = sq
    grads = [None] * depth
    for l in reversed(range(depth)):
        dh, g_mlp = _bwd_mlp(grad_x, layers[l], saved[l], latest["token"], hook)
        hook(g_mlp["w_down"])
        begin(l, 1, [g_mlp[n] for n in GROUPS[1]])
        grad_x, g_mix = _bwd_mix(dh, layers[l], saved[l], latest["token"], hook)
        grads[l] = dict(g_mlp, **g_mix)
        hook(g_mix["w_o"])
        begin(l, 0, [g_mix[n] for n in GROUPS[0]] + [_pack_layer([grads[l][n] for n in SMALL])])
    while pending:
        hook(grad_x)
    loss = lax.psum(0.5 / D_MODEL * jnp.sum(sq), ("x", "y", "c"))

    small_sum = _gather_shards([jnp.stack(small_sums)])[0].reshape(depth, SMALL_LAYER_ROWS * LANES)
    grad = {n: [jnp.swapaxes(o, 1, 2) for o in big_outs[n]] if n == "w_in" else big_outs[n] for n in BIG}
    delta = {n: o[1] for n, o in grad.items()}
    new_m = {n: o[2] for n, o in grad.items()}
    new_v = {n: o[3] for n, o in grad.items()}
    grad = {n: o[0] for n, o in grad.items()}
    grad.update(_unpack_layers(small_sum, {n: (3, N_CHIPS * conv_cols) if n == "w_conv" else weights[n].shape[1:] for n in SMALL}))
    grad["w_conv"] = lax.dynamic_slice_in_dim(grad["w_conv"], chip * conv_cols, conv_cols, axis=2)

    delta["w_conv"], new_m["w_conv"], new_v["w_conv"] = _adamw(w_conv, grad["w_conv"], m_w_conv, v_w_conv)
    rest = [n for n in SMALL if n != "w_conv"]
    rest_shapes = {n: weights[n].shape[1:] for n in rest}
    packed = [jnp.concatenate([_pack_layer([src[n][l] for n in rest]) for l in range(depth)]) for src in (weights, grad, m_in, v_in)]
    for dst, res in zip((delta, new_m, new_v), _adamw(*packed)):
        dst.update(_unpack_layers(res.reshape(depth, SMALL_LAYER_ROWS * LANES), rest_shapes))

    return (loss, grad_x[None], *[grad[n] for n in WEIGHTS], *[delta[n] for n in WEIGHTS],
            *[new_m[n] for n in WEIGHTS], *[new_v[n] for n in WEIGHTS])
```

```python
import functools

import jax
import jax.numpy as jnp
from jax import lax
from jax.experimental import pallas as pl
from jax.experimental.pallas import tpu as pltpu

f32 = jnp.float32
bf16 = jnp.bfloat16

D_MODEL = 1024
DEPTH = 4
CONV_W = 384
POOL_W = 256
SGU_W = 384
IN_W = 3 * CONV_W + POOL_W + 2 * SGU_W
D_FF = 2816
CHUNK = 128
HEAD = 64
POOL_WINDOWS = (2, 4, 8, 16)
ALPHA = float((2 * DEPTH) ** 0.25)
LN_EPS = 1e-5
ADAM_LR = 0.001
ADAM_B1 = 0.9
ADAM_B2 = 0.999
ADAM_EPS = 1e-08
ADAM_WD = 0.01
ADAM_STEP = 10

LANES = 128
TOKEN_TILE = 256
N_CHIPS = 4
VMEM_LIMIT = 56 * 1024 * 1024

BLK_XA, BLK_GB, BLK_GC, BLK_P, BLK_U, BLK_V = 0, 3, 6, 9, 11, 14

MESH = pl.DeviceIdType.MESH


def _params(sem=None):
    return pltpu.CompilerParams(dimension_semantics=sem, vmem_limit_bytes=VMEM_LIMIT)


def _rows(width, tile=TOKEN_TILE):
    return pl.BlockSpec((tile, width), lambda i: (i, 0))


def _resident(shape):
    zeros = (0,) * len(shape)
    return pl.BlockSpec(shape, lambda *_: zeros, pipeline_mode=pl.Buffered(1))


def _nt(a, b):
    return lax.dot_general(a, b, (((1,), (1,)), ((), ())), preferred_element_type=f32)


def _tn(a, b):
    return lax.dot_general(a, b, (((0,), (0,)), ((), ())), preferred_element_type=f32)


def _mm(a, b):
    return jnp.dot(a, b, preferred_element_type=f32)


def _norm_fwd(z):
    mu = jnp.mean(z, axis=-1, keepdims=True)
    zc = z - mu
    var = jnp.mean(zc * zc, axis=-1, keepdims=True)
    rstd = lax.rsqrt(var + LN_EPS)
    return zc * rstd, rstd


def _norm_bwd(dxhat, xhat, rstd):
    m1 = jnp.mean(dxhat, axis=-1, keepdims=True)
    m2 = jnp.mean(dxhat * xhat, axis=-1, keepdims=True)
    return rstd * (dxhat - m1 - xhat * m2)


def _proj(x, w_in_b, after):
    s = x.shape[0]

    def body(x_ref, w_ref, after_ref, p_ref, xb_ref):
        xb = x_ref[...].astype(bf16)
        xb_ref[...] = xb
        p_ref[...] = _nt(xb, w_ref[...])

    return pl.pallas_call(
        body, grid=(s // TOKEN_TILE,),
        in_specs=[_rows(D_MODEL), _resident((IN_W, D_MODEL)), pl.BlockSpec(memory_space=pl.ANY)],
        out_specs=[_rows(IN_W), _rows(D_MODEL)],
        out_shape=[jax.ShapeDtypeStruct((s, IN_W), f32), jax.ShapeDtypeStruct((s, D_MODEL), bf16)],
        name="proj", compiler_params=_params(("arbitrary",)))(x, w_in_b, after)


def _row_ranges(parts):
    out, at = [], 0
    for p in parts:
        out.append((at, at + p.shape[1]))
        at += p.shape[1]
    return out


def _wo_ln1(mix, x, w_o_b, g, b):
    s = x.shape[0]
    n = len(mix)
    ranges = _row_ranges(mix)

    def body(*refs):
        m_refs = refs[:n]
        x_ref, w_ref, g_ref, b_ref, xhat_ref, rstd_ref, hb_ref = refs[n:]
        z = ALPHA * x_ref[...]
        for m_ref, (lo, hi) in zip(m_refs, ranges):
            z = z + _mm(m_ref[...], w_ref[lo:hi, :])
        xhat, rstd = _norm_fwd(z)
        xhat_ref[...] = xhat
        rstd_ref[...] = rstd
        hb_ref[...] = (xhat * g_ref[...] + b_ref[...]).astype(bf16)

    return pl.pallas_call(
        body, grid=(s // TOKEN_TILE,),
        in_specs=[_rows(m.shape[1]) for m in mix] + [_rows(D_MODEL), _resident((D_MODEL, D_MODEL)), _resident((1, D_MODEL)),
                                                     _resident((1, D_MODEL))],
        out_specs=[_rows(D_MODEL), _rows(1), _rows(D_MODEL)],
        out_shape=[jax.ShapeDtypeStruct((s, D_MODEL), f32), jax.ShapeDtypeStruct((s, 1), f32),
                   jax.ShapeDtypeStruct((s, D_MODEL), bf16)],
        name="wo_ln1", compiler_params=_params(("arbitrary",)))(*mix, x, w_o_b, g, b)


def _mlp_fwd(xhat1, g1, b1, w_gu_b, w_down_b, g2, b2, after):
    s = xhat1.shape[0]

    def body(xh_ref, g1_ref, b1_ref, wgu_ref, wd_ref, g2_ref, b2_ref, after_ref, gu_ref, xhat2_ref, rstd2_ref, y_ref):
        h = xh_ref[...] * g1_ref[...] + b1_ref[...]
        gu = _mm(h.astype(bf16), wgu_ref[...])
        gu_ref[...] = gu
        gate = gu[:, :D_FF]
        act = gate * jax.nn.sigmoid(gate) * gu[:, D_FF:]
        z = ALPHA * h + _mm(act.astype(bf16), wd_ref[...])
        xhat2, rstd2 = _norm_fwd(z)
        xhat2_ref[...] = xhat2
        rstd2_ref[...] = rstd2
        y_ref[...] = xhat2 * g2_ref[...] + b2_ref[...]

    vec = _resident((1, D_MODEL))
    return pl.pallas_call(
        body, grid=(s // TOKEN_TILE,),
        in_specs=[_rows(D_MODEL), vec, vec, _resident((D_MODEL, 2 * D_FF)), _resident((D_FF, D_MODEL)), vec, vec,
                  pl.BlockSpec(memory_space=pl.ANY)],
        out_specs=[_rows(2 * D_FF), _rows(D_MODEL), _rows(1), _rows(D_MODEL)],
        out_shape=[jax.ShapeDtypeStruct((s, 2 * D_FF), f32), jax.ShapeDtypeStruct((s, D_MODEL), f32),
                   jax.ShapeDtypeStruct((s, 1), f32), jax.ShapeDtypeStruct((s, D_MODEL), f32)],
        name="mlp_fwd", compiler_params=_params(("arbitrary",)))(xhat1, g1, b1, w_gu_b, w_down_b, g2, b2, after)


def _loss_head(y, target):
    s = y.shape[0]

    def body(y_ref, t_ref, dy_ref, sq_ref):
        @pl.when(pl.program_id(0) == 0)
        def _():
            sq_ref[...] = jnp.zeros_like(sq_ref)

        e = y_ref[...] - t_ref[...]
        dy_ref[...] = e * (1.0 / D_MODEL)
        sq_ref[...] += jnp.sum(e * e, axis=0, keepdims=True)

    return pl.pallas_call(
        body, grid=(s // TOKEN_TILE,),
        in_specs=[_rows(D_MODEL), _rows(D_MODEL)],
        out_specs=[_rows(D_MODEL), pl.BlockSpec((1, D_MODEL), lambda i: (0, 0))],
        out_shape=[jax.ShapeDtypeStruct((s, D_MODEL), f32), jax.ShapeDtypeStruct((1, D_MODEL), f32)],
        name="loss_head", compiler_params=_params(("arbitrary",)))(y, target)


def _mlp_bwd(dy, xhat2, rstd2, g2, gu, w_gu_b, w_down_b, after):
    s = dy.shape[0]

    def body(dy_ref, xh_ref, rs_ref, g2_ref, gu_ref, wgu_ref, wd_ref, after_ref, dz_ref, act_ref, dgu_ref, dh_ref, gg_ref, gb_ref):
        @pl.when(pl.program_id(0) == 0)
        def _():
            gg_ref[...] = jnp.zeros_like(gg_ref)
            gb_ref[...] = jnp.zeros_like(gb_ref)

        dy_t = dy_ref[...]
        xhat = xh_ref[...]
        gg_ref[...] += jnp.sum(dy_t * xhat, axis=0, keepdims=True)
        gb_ref[...] += jnp.sum(dy_t, axis=0, keepdims=True)
        dz = _norm_bwd(dy_t * g2_ref[...], xhat, rs_ref[...])
        dzb = dz.astype(bf16)
        dz_ref[...] = dzb
        dact = _nt(dzb, wd_ref[...])
        gate = gu_ref[:, :D_FF]
        up = gu_ref[:, D_FF:]
        sg = jax.nn.sigmoid(gate)
        silu = gate * sg
        act_ref[...] = (silu * up).astype(bf16)
        dgu_ref[:, :D_FF] = (dact * up * (sg * (1.0 + gate * (1.0 - sg)))).astype(bf16)
        dgu_ref[:, D_FF:] = (dact * silu).astype(bf16)
        dh_ref[...] = ALPHA * dz + _nt(dgu_ref[...], wgu_ref[...])

    vec_out = pl.BlockSpec((1, D_MODEL), lambda i: (0, 0))
    return pl.pallas_call(
        body, grid=(s // TOKEN_TILE,),
        in_specs=[_rows(D_MODEL), _rows(D_MODEL), _rows(1), _resident((1, D_MODEL)), _rows(2 * D_FF),
                  _resident((D_MODEL, 2 * D_FF)), _resident((D_FF, D_MODEL)), pl.BlockSpec(memory_space=pl.ANY)],
        out_specs=[_rows(D_MODEL), _rows(D_FF), _rows(2 * D_FF), _rows(D_MODEL), vec_out, vec_out],
        out_shape=[jax.ShapeDtypeStruct((s, D_MODEL), bf16), jax.ShapeDtypeStruct((s, D_FF), bf16),
                   jax.ShapeDtypeStruct((s, 2 * D_FF), bf16), jax.ShapeDtypeStruct((s, D_MODEL), f32),
                   jax.ShapeDtypeStruct((1, D_MODEL), f32), jax.ShapeDtypeStruct((1, D_MODEL), f32)],
        name="mlp_bwd", compiler_params=_params(("arbitrary",)))(dy, xhat2, rstd2, g2, gu, w_gu_b, w_down_b, after)


def _ln1_wo_bwd(dh, xhat1, rstd1, g1, w_o_b, after):
    s = dh.shape[0]

    def body(dh_ref, xh_ref, rs_ref, g1_ref, w_ref, after_ref, dz_ref, dzb_ref, dm_ref, gg_ref, gb_ref):
        @pl.when(pl.program_id(0) == 0)
        def _():
            gg_ref[...] = jnp.zeros_like(gg_ref)
            gb_ref[...] = jnp.zeros_like(gb_ref)

        dh_t = dh_ref[...]
        xhat = xh_ref[...]
        gg_ref[...] += jnp.sum(dh_t * xhat, axis=0, keepdims=True)
        gb_ref[...] += jnp.sum(dh_t, axis=0, keepdims=True)
        dz = _norm_bwd(dh_t * g1_ref[...], xhat, rs_ref[...])
        dz_ref[...] = dz
        dzb = dz.astype(bf16)
        dzb_ref[...] = dzb
        dm_ref[...] = _nt(dzb, w_ref[...])

    vec_out = pl.BlockSpec((1, D_MODEL), lambda i: (0, 0))
    return pl.pallas_call(
        body, grid=(s // TOKEN_TILE,),
        in_specs=[_rows(D_MODEL), _rows(D_MODEL), _rows(1), _resident((1, D_MODEL)), _resident((D_MODEL, D_MODEL)),
                  pl.BlockSpec(memory_space=pl.ANY)],
        out_specs=[_rows(D_MODEL), _rows(D_MODEL), _rows(D_MODEL), vec_out, vec_out],
        out_shape=[jax.ShapeDtypeStruct((s, D_MODEL), f32), jax.ShapeDtypeStruct((s, D_MODEL), bf16),
                   jax.ShapeDtypeStruct((s, D_MODEL), f32), jax.ShapeDtypeStruct((1, D_MODEL), f32),
                   jax.ShapeDtypeStruct((1, D_MODEL), f32)],
        name="ln1_wo_bwd", compiler_params=_params(("arbitrary",)))(dh, xhat1, rstd1, g1, w_o_b, after)


def _dx(dz1, dparts, w_in_t):
    s = dz1.shape[0]
    n = len(dparts)
    ranges = _row_ranges(dparts)

    def body(*refs):
        d_refs = refs[:n]
        dz_ref, w_ref, dx_ref = refs[n:]
        acc = ALPHA * dz_ref[...]
        for d_ref, (lo, hi) in zip(d_refs, ranges):
            acc = acc + _mm(d_ref[...], w_ref[lo:hi, :])
        dx_ref[...] = acc

    return pl.pallas_call(
        body, grid=(s // TOKEN_TILE,),
        in_specs=[_rows(d.shape[1]) for d in dparts] + [_rows(D_MODEL), _resident((IN_W, D_MODEL))],
        out_specs=_rows(D_MODEL),
        out_shape=jax.ShapeDtypeStruct((s, D_MODEL), f32),
        name="dx", compiler_params=_params(("arbitrary",)))(*dparts, dz1, w_in_t)


def _weight_grad_rows(parts, b, bn):
    s, n_cols = b.shape
    n = len(parts)
    ranges = _row_ranges(parts)
    m = ranges[-1][1]

    def body(*refs):
        p_refs = refs[:n]
        b_ref, o_ref = refs[n:]
        for p_ref, (lo, hi) in zip(p_refs, ranges):
            o_ref[lo:hi, :] = _tn(p_ref[...], b_ref[...]).astype(bf16)

    return pl.pallas_call(
        body, grid=(n_cols // bn,),
        in_specs=[_resident(p.shape) for p in parts] + [pl.BlockSpec((s, bn), lambda j: (0, j))],
        out_specs=pl.BlockSpec((m, bn), lambda j: (0, j)),
        out_shape=jax.ShapeDtypeStruct((m, n_cols), bf16),
        name="weight_grad_rows", compiler_params=_params(("arbitrary",)))(*parts, b)


def _weight_grad(a, b, bm, bn, after):
    s, m = a.shape
    n = b.shape[1]

    def body(a_ref, b_ref, after_ref, o_ref):
        o_ref[...] = _tn(a_ref[...], b_ref[...]).astype(bf16)

    return pl.pallas_call(
        body, grid=(m // bm, n // bn),
        in_specs=[pl.BlockSpec((s, bm), lambda i, j: (0, i)), pl.BlockSpec((s, bn), lambda i, j: (0, j)),
                  pl.BlockSpec(memory_space=pl.ANY)],
        out_specs=pl.BlockSpec((bm, bn), lambda i, j: (i, j)),
        out_shape=jax.ShapeDtypeStruct((m, n), bf16),
        name="weight_grad", compiler_params=_params(("arbitrary", "arbitrary")))(a, b, after)


def _shift_down(a, k):
    row = lax.broadcasted_iota(jnp.int32, a.shape, 0)
    return jnp.where(row >= k, pltpu.roll(a, k, 0), 0.0)


def _shift_up(a, k):
    n = a.shape[0]
    row = lax.broadcasted_iota(jnp.int32, a.shape, 0)
    return jnp.where(row < n - k, pltpu.roll(a, n - k, 0), 0.0)


def _slab(s, block):
    return pl.BlockSpec((s, LANES), lambda k: (0, block + k))


def _conv_y(z, w):
    return w[0:1, :] * _shift_down(z, 2) + w[1:2, :] * _shift_down(z, 1) + w[2:3, :] * z


def _conv_fwd(proj, w_conv):
    s = proj.shape[0]

    def body(xa_ref, gb_ref, gc_ref, w_ref, o_ref):
        z = gc_ref[...] * xa_ref[...]
        o_ref[...] = (gb_ref[...] * _conv_y(z, w_ref[...])).astype(bf16)

    return pl.pallas_call(
        body, grid=(CONV_W // LANES,),
        in_specs=[_slab(s, BLK_XA), _slab(s, BLK_GB), _slab(s, BLK_GC), pl.BlockSpec((3, LANES), lambda k: (0, k))],
        out_specs=_slab(s, 0),
        out_shape=jax.ShapeDtypeStruct((s, CONV_W), bf16),
        name="conv_fwd", compiler_params=_params(("arbitrary",)))(proj, proj, proj, w_conv)


def _conv_bwd(proj, dmix, w_conv, after):
    s = proj.shape[0]

    def body(xa_ref, gb_ref, gc_ref, dy_ref, w_ref, after_ref, dxa_ref, dgb_ref, dgc_ref, dw_ref):
        xa = xa_ref[...]
        gc = gc_ref[...]
        w = w_ref[...]
        z = gc * xa
        dya = dy_ref[...]
        dgb_ref[...] = (dya * _conv_y(z, w)).astype(bf16)
        dy = dya * gb_ref[...]
        dz = w[2:3, :] * dy + w[1:2, :] * _shift_up(dy, 1) + w[0:1, :] * _shift_up(dy, 2)
        dxa_ref[...] = (dz * gc).astype(bf16)
        dgc_ref[...] = (dz * xa).astype(bf16)
        dw_ref[0:1, :] = jnp.sum(dy * _shift_down(z, 2), axis=0, keepdims=True)
        dw_ref[1:2, :] = jnp.sum(dy * _shift_down(z, 1), axis=0, keepdims=True)
        dw_ref[2:3, :] = jnp.sum(dy * z, axis=0, keepdims=True)

    out = jax.ShapeDtypeStruct((s, CONV_W), bf16)
    return pl.pallas_call(
        body, grid=(CONV_W // LANES,),
        in_specs=[_slab(s, BLK_XA), _slab(s, BLK_GB), _slab(s, BLK_GC), _slab(s, 0), pl.BlockSpec((3, LANES), lambda k: (0, k)),
                  pl.BlockSpec(memory_space=pl.ANY)],
        out_specs=[_slab(s, 0), _slab(s, 0), _slab(s, 0), pl.BlockSpec((3, LANES), lambda k: (0, k))],
        out_shape=[out, out, out, jax.ShapeDtypeStruct((3, CONV_W), f32)],
        name="conv_bwd", compiler_params=_params(("arbitrary",)))(proj, proj, proj, dmix, w_conv, after)


def _pool_window(k):
    lane = lax.broadcasted_iota(jnp.int32, (1, LANES), 1)
    low = lane < HEAD
    first = k == 0
    wlen = jnp.where(low, jnp.where(first, POOL_WINDOWS[0], POOL_WINDOWS[2]), jnp.where(first, POOL_WINDOWS[1], POOL_WINDOWS[3]))
    return wlen, low, first


def _pool_diff(p, k):
    wlen, low, first = _pool_window(k)
    s2 = p + _shift_down(p, 1)
    s4 = s2 + _shift_down(s2, 2)
    s8 = s4 + _shift_down(s4, 4)
    s16 = s8 + _shift_down(s8, 8)
    win = jnp.where(low, jnp.where(first, s2, s8), jnp.where(first, s4, s16))
    row = lax.broadcasted_iota(jnp.int32, p.shape, 0)
    count = jnp.minimum(row + 1, wlen).astype(f32)
    return win / count - p, count


def _pool_weight(w_ref):
    zero = jnp.zeros((HEAD, HEAD), f32)
    top = jnp.concatenate([w_ref[0], zero], axis=1)
    bottom = jnp.concatenate([zero, w_ref[1]], axis=1)
    return jnp.concatenate([top, bottom], axis=0).astype(bf16)


def _pool_fwd(proj, w_pool, pool_scale):
    s = proj.shape[0]

    def body(p_ref, w_ref, sc_ref, o_ref):
        d, _ = _pool_diff(p_ref[...], pl.program_id(0))
        o_ref[...] = (_mm(d.astype(bf16), _pool_weight(w_ref)) * sc_ref[...]).astype(bf16)

    return pl.pallas_call(
        body, grid=(POOL_W // LANES,),
        in_specs=[_slab(s, BLK_P), pl.BlockSpec((2, HEAD, HEAD), lambda k: (k, 0, 0)), pl.BlockSpec((1, LANES), lambda k: (0, k))],
        out_specs=_slab(s, 0),
        out_shape=jax.ShapeDtypeStruct((s, POOL_W), bf16),
        name="pool_fwd", compiler_params=_params(("arbitrary",)))(proj, w_pool, pool_scale)


def _pool_bwd(proj, dmix, w_pool, pool_scale):
    s = proj.shape[0]

    def body(p_ref, dy_ref, w_ref, sc_ref, dp_ref, dw_ref, dsc_ref):
        k = pl.program_id(0)
        d, count = _pool_diff(p_ref[...], k)
        wbd = _pool_weight(w_ref)
        db = d.astype(bf16)
        dyb = dy_ref[...]
        dsc_ref[...] = jnp.sum(dyb * _mm(db, wbd), axis=0, keepdims=True)
        dpre = (dyb * sc_ref[...]).astype(bf16)
        dwbd = _tn(db, dpre)
        dw_ref[0] = dwbd[:HEAD, :HEAD]
        dw_ref[1] = dwbd[HEAD:, HEAD:]
        dd = _nt(dpre, wbd)
        e = dd / count
        wlen, low, first = _pool_window(k)
        a2 = e + _shift_up(e, 1)
        a4 = a2 + _shift_up(a2, 2)
        a8 = a4 + _shift_up(a4, 4)
        a16 = a8 + _shift_up(a8, 8)
        back = jnp.where(low, jnp.where(first, a2, a8), jnp.where(first, a4, a16))
        dp_ref[...] = (back - dd).astype(bf16)

    return pl.pallas_call(
        body, grid=(POOL_W // LANES,),
        in_specs=[_slab(s, BLK_P), _slab(s, CONV_W // LANES), pl.BlockSpec((2, HEAD, HEAD), lambda k: (k, 0, 0)),
                  pl.BlockSpec((1, LANES), lambda k: (0, k))],
        out_specs=[_slab(s, 0), pl.BlockSpec((2, HEAD, HEAD), lambda k: (k, 0, 0)), pl.BlockSpec((1, LANES), lambda k: (0, k))],
        out_shape=[jax.ShapeDtypeStruct((s, POOL_W), bf16), jax.ShapeDtypeStruct((4, HEAD, HEAD), f32),
                   jax.ShapeDtypeStruct((1, POOL_W), f32)],
        name="pool_bwd", compiler_params=_params(("arbitrary",)))(proj, dmix, w_pool, pool_scale)


INV_SQRT2 = 0.7071067811865476
INV_SQRT_2PI = 0.3989422804014327


def _gelu(x):
    return 0.5 * x * (1.0 + lax.erf(x * INV_SQRT2))


def _gelu_grad(x):
    return 0.5 * (1.0 + lax.erf(x * INV_SQRT2)) + x * (INV_SQRT_2PI * jnp.exp(-0.5 * x * x))


def _head_mean(a, low):
    s_low = jnp.sum(jnp.where(low, a, 0.0), axis=-1, keepdims=True)
    s_high = jnp.sum(jnp.where(low, 0.0, a), axis=-1, keepdims=True)
    return jnp.where(low, s_low, s_high) * (1.0 / HEAD)


def _tril():
    r = lax.broadcasted_iota(jnp.int32, (CHUNK, CHUNK), 0)
    c = lax.broadcasted_iota(jnp.int32, (CHUNK, CHUNK), 1)
    return r >= c


def _sgu_chunk(up, vp, g, wm0, wm1, b0, b1, low):
    ug = _gelu(up)
    vg = _gelu(vp)
    vc = vg - _head_mean(vg, low)
    rstd = lax.rsqrt(_head_mean(vc * vc, low) + LN_EPS)
    vn = vc * rstd
    vb = (vn * g).astype(bf16)
    mixed = jnp.where(low, _mm(wm0, vb) + b0, _mm(wm1, vb) + b1)
    return ug, vn, rstd, vb, mixed


def _sgu_specs(s):
    return [_slab(s, BLK_U), _slab(s, BLK_V), pl.BlockSpec((1, LANES), lambda k: (0, k)),
            pl.BlockSpec((2, CHUNK, CHUNK), lambda k: (k, 0, 0)), pl.BlockSpec((2, CHUNK, 1), lambda k: (k, 0, 0))]


def _sgu_fwd(proj, sgu_g, w_spatial, b_spatial3):
    s = proj.shape[0]

    def body(u_ref, v_ref, g_ref, w_ref, b_ref, o_ref):
        low = lax.broadcasted_iota(jnp.int32, (1, LANES), 1) < HEAD
        mask = _tril()
        wm0 = jnp.where(mask, w_ref[0], 0.0).astype(bf16)
        wm1 = jnp.where(mask, w_ref[1], 0.0).astype(bf16)
        g = g_ref[...]
        b0 = b_ref[0]
        b1 = b_ref[1]

        def chunk(n, carry):
            rows = pl.ds(pl.multiple_of(n * CHUNK, CHUNK), CHUNK)
            ug, _, _, _, mixed = _sgu_chunk(u_ref[rows, :], v_ref[rows, :], g, wm0, wm1, b0, b1, low)
            o_ref[rows, :] = (ug * mixed).astype(bf16)
            return carry

        lax.fori_loop(0, s // CHUNK, chunk, 0)

    return pl.pallas_call(
        body, grid=(SGU_W // LANES,),
        in_specs=_sgu_specs(s),
        out_specs=_slab(s, 0),
        out_shape=jax.ShapeDtypeStruct((s, SGU_W), bf16),
        name="sgu_fwd", compiler_params=_params(("arbitrary",)))(proj, proj, sgu_g, w_spatial, b_spatial3)


def _sgu_bwd(proj, dmix, sgu_g, w_spatial, b_spatial3):
    s = proj.shape[0]

    def body(u_ref, v_ref, g_ref, w_ref, b_ref, dy_ref, du_ref, dv_ref, dg_ref, dw_ref, db_ref):
        low = lax.broadcasted_iota(jnp.int32, (1, LANES), 1) < HEAD
        mask = _tril()
        w0 = jnp.where(mask, w_ref[0], 0.0)
        w1 = jnp.where(mask, w_ref[1], 0.0)
        wm0 = w0.astype(bf16)
        wm1 = w1.astype(bf16)
        wt0 = w0.T.astype(bf16)
        wt1 = w1.T.astype(bf16)
        g = g_ref[...]
        b0 = b_ref[0]
        b1 = b_ref[1]
        dg_ref[...] = jnp.zeros_like(dg_ref)
        dw_ref[...] = jnp.zeros_like(dw_ref)
        db_ref[...] = jnp.zeros_like(db_ref)

        def chunk(n, carry):
            rows = pl.ds(pl.multiple_of(n * CHUNK, CHUNK), CHUNK)
            up = u_ref[rows, :]
            vp = v_ref[rows, :]
            ug, vn, rstd, vb, mixed = _sgu_chunk(up, vp, g, wm0, wm1, b0, b1, low)
            dy = dy_ref[rows, :]
            du_ref[rows, :] = (dy * mixed * _gelu_grad(up)).astype(bf16)
            dmix_c = dy * ug
            db_ref[0] += jnp.sum(jnp.where(low, dmix_c, 0.0), axis=-1, keepdims=True)
            db_ref[1] += jnp.sum(jnp.where(low, 0.0, dmix_c), axis=-1, keepdims=True)
            dmb = dmix_c.astype(bf16)
            zero = jnp.zeros_like(dmb)
            dw_ref[0] += _nt(jnp.where(low, dmb, zero), vb)
            dw_ref[1] += _nt(jnp.where(low, zero, dmb), vb)
            dvnorm = jnp.where(low, _mm(wt0, dmb), _mm(wt1, dmb))
            dg_ref[...] += jnp.sum(dvnorm * vn, axis=0, keepdims=True)
            dvn = dvnorm * g
            dvg = rstd * (dvn - _head_mean(dvn, low) - vn * _head_mean(dvn * vn, low))
            dv_ref[rows, :] = (dvg * _gelu_grad(vp)).astype(bf16)
            return carry

        lax.fori_loop(0, s // CHUNK, chunk, 0)
        dw_ref[0] = jnp.where(mask, dw_ref[0], 0.0)
        dw_ref[1] = jnp.where(mask, dw_ref[1], 0.0)

    out = jax.ShapeDtypeStruct((s, SGU_W), bf16)
    return pl.pallas_call(
        body, grid=(SGU_W // LANES,),
        in_specs=_sgu_specs(s) + [_slab(s, (CONV_W + POOL_W) // LANES)],
        out_specs=[_slab(s, 0), _slab(s, 0), pl.BlockSpec((1, LANES), lambda k: (0, k)),
                   pl.BlockSpec((2, CHUNK, CHUNK), lambda k: (k, 0, 0)), pl.BlockSpec((2, CHUNK, 1), lambda k: (k, 0, 0))],
        out_shape=[out, out, jax.ShapeDtypeStruct((1, SGU_W), f32), jax.ShapeDtypeStruct((6, CHUNK, CHUNK), f32),
                   jax.ShapeDtypeStruct((6, CHUNK, 1), f32)],
        name="sgu_bwd", compiler_params=_params(("arbitrary",)))(proj, proj, sgu_g, w_spatial, b_spatial3, dmix)


def _fwd_mix(x, w, after):
    proj, xb = _proj(x, w["w_in"], after)
    mix = [_conv_fwd(proj, w["w_conv"]), _pool_fwd(proj, w["w_pool"], w["pool_scale"]),
           _sgu_fwd(proj, w["sgu_ln_g"], w["w_spatial"], w["b_spatial"])]
    xhat1, rstd1, hb = _wo_ln1(mix, x, w["w_o"], w["ln1_g"], w["ln1_b"])
    return dict(proj=proj, xb=xb, mix=mix, xhat1=xhat1, rstd1=rstd1, hb=hb)


def _fwd_mlp(sv, w, after):
    gu, xhat2, rstd2, y = _mlp_fwd(sv["xhat1"], w["ln1_g"], w["ln1_b"], w["w_gate_up"], w["w_down"], w["ln2_g"], w["ln2_b"], after)
    sv.update(gu=gu, xhat2=xhat2, rstd2=rstd2)
    return y


def _bwd_mlp(dy, w, sv, after, hook):
    dz2b, actb, dgub, dh, g_ln2_g, g_ln2_b = _mlp_bwd(dy, sv["xhat2"], sv["rstd2"], w["ln2_g"], sv["gu"], w["w_gate_up"],
                                                      w["w_down"], after)
    after = hook(dh)
    grads = dict(w_gate_up=_weight_grad(sv["hb"], dgub, 512, D_FF // 2, after),
                 w_down=_weight_grad(actb, dz2b, D_FF // 2, D_MODEL, after), ln2_g=g_ln2_g, ln2_b=g_ln2_b)
    return dh, grads


def _bwd_mix(dh, w, sv, after, hook):
    dz1, dz1b, dmix, g_ln1_g, g_ln1_b = _ln1_wo_bwd(dh, sv["xhat1"], sv["rstd1"], w["ln1_g"], w["w_o"], after)
    after = hook(dz1)
    dxa, dgb, dgc, g_conv = _conv_bwd(sv["proj"], dmix, w["w_conv"], after)
    dp, g_pool, g_pscale = _pool_bwd(sv["proj"], dmix, w["w_pool"], w["pool_scale"])
    du, dv, g_sgu_g, g_spatial, g_bsp = _sgu_bwd(sv["proj"], dmix, w["sgu_ln_g"], w["w_spatial"], w["b_spatial"])
    dparts = [dxa, dgb, dgc, dp, du, dv]
    dx = _dx(dz1, dparts, w["w_in"])
    grads = dict(
        w_in=_weight_grad_rows(dparts, sv["xb"], 512), w_o=_weight_grad_rows(sv["mix"], dz1b, D_MODEL),
        w_conv=g_conv, w_pool=g_pool, pool_scale=g_pscale, sgu_ln_g=g_sgu_g, w_spatial=g_spatial,
        b_spatial=g_bsp.reshape(6, CHUNK), ln1_g=g_ln1_g, ln1_b=g_ln1_b)
    return dx, grads


def _local_step(x, target, layers):
    saved = []
    for w in layers:
        sv = _fwd_mix(x, w, x)
        x = _fwd_mlp(sv, w, x)
        saved.append(sv)
    dy, sq = _loss_head(x, target)
    grads = [None] * len(layers)
    for l in reversed(range(len(layers))):
        dh, g_mlp = _bwd_mlp(dy, layers[l], saved[l], sq, lambda a: a)
        dy, g_mix = _bwd_mix(dh, layers[l], saved[l], dh, lambda a: a)
        grads[l] = dict(g_mlp, **g_mix)
    return sq, dy, grads


ANY = pl.BlockSpec(memory_space=pl.ANY)


def _place():
    x, y, c = lax.axis_index("x"), lax.axis_index("y"), lax.axis_index("c")
    others = [(1 - x, y), (x, 1 - y), (1 - x, 1 - y)]
    return x, y, c, others


def _chip_index(cx, cy):
    return 2 * cx + cy


def _half(ref_rows, c):
    half = ref_rows // 2
    return pl.ds(pl.multiple_of(c * half, 8), half)


def _remote(src, dst, send_sem, recv_sem, device):
    return pltpu.make_async_remote_copy(src_ref=src, dst_ref=dst, send_sem=send_sem, recv_sem=recv_sem,
                                        device_id=device, device_id_type=MESH)


def _gather_shards(shards):
    n = len(shards)
    base, total = [], 0
    for s in shards:
        base.append(total)
        total += 6 * s.shape[0]

    def body(*refs):
        ins, outs = refs[:n], refs[n:2 * n]
        send, recv = refs[2 * n:]
        x, y, c, others = _place()
        me = _chip_index(x, y)
        sib = (x, y, 1 - c)
        sends = []
        for f in range(n):
            depth, rows = ins[f].shape[0], ins[f].shape[1]
            for l in range(depth):
                for k, (cx, cy) in enumerate(others):
                    sem = base[f] + 6 * l + k
                    cp = _remote(ins[f].at[l, _half(rows, c)], outs[f].at[l, me, _half(rows, c)],
                                 send.at[sem], recv.at[sem], (cx, cy, c))
                    cp.start()
                    sends.append(cp)
        for f in range(n):
            depth, rows = ins[f].shape[0], ins[f].shape[1]
            for l in range(depth):
                for k, (cx, cy) in enumerate(others):
                    sem = base[f] + 6 * l + k
                    landed = outs[f].at[l, _chip_index(cx, cy), _half(rows, c)]
                    _remote(landed, landed, send.at[sem], recv.at[sem], (cx, cy, c)).wait_recv()
                    cp = _remote(landed, landed, send.at[sem + 3], recv.at[sem + 3], sib)
                    cp.start()
                    sends.append(cp)
        for f in range(n):
            depth, rows = ins[f].shape[0], ins[f].shape[1]
            for l in range(depth):
                for k, (cx, cy) in enumerate(others):
                    sem = base[f] + 6 * l + k + 3
                    passed = outs[f].at[l, _chip_index(cx, cy), _half(rows, 1 - c)]
                    _remote(passed, passed, send.at[sem], recv.at[sem], sib).wait_recv()
        for cp in sends:
            cp.wait_send()

    gathered = pl.pallas_call(
        body, in_specs=[ANY] * n, out_specs=[ANY] * n,
        out_shape=[jax.ShapeDtypeStruct((s.shape[0], N_CHIPS) + s.shape[1:], s.dtype) for s in shards],
        scratch_shapes=[pltpu.SemaphoreType.DMA((total,)), pltpu.SemaphoreType.DMA((total,))],
        name="gather_shards")(*shards)
    return [_place_own(g, s) for g, s in zip(gathered, shards)]


def _scalar(value):
    return jnp.reshape(value, (1,)).astype(jnp.int32)


def _place_own(blocks, shard):
    depth, rows, cols = shard.shape

    def body(me_ref, b_ref, s_ref, o_ref):
        o_ref[...] = s_ref[...]

    return pl.pallas_call(
        body,
        grid_spec=pltpu.PrefetchScalarGridSpec(
            num_scalar_prefetch=1, grid=(depth,),
            in_specs=[ANY, pl.BlockSpec((None, rows, cols), lambda l, me: (l, 0, 0))],
            out_specs=pl.BlockSpec((None, None, rows, cols), lambda l, me: (l, me[0], 0, 0))),
        out_shape=jax.ShapeDtypeStruct(blocks.shape, blocks.dtype),
        input_output_aliases={1: 0},
        name="place_own", compiler_params=_params(("arbitrary",)))(
            _scalar(_chip_index(lax.axis_index("x"), lax.axis_index("y"))), blocks, shard)


HBM = pl.BlockSpec(memory_space=pltpu.HBM)
SEM = pl.BlockSpec(memory_space=pltpu.SEMAPHORE)
TOKEN = jax.ShapeDtypeStruct((8, LANES), f32)
SPLIT_COPY = pltpu.CompilerParams(has_side_effects=pltpu.SideEffectType.DATAFLOW_SIDE_EFFECTING)


def _in_hbm(a):
    return pltpu.with_memory_space_constraint(a, pltpu.HBM)


def _full_shape(shard, axis):
    rows, cols = shard.shape
    return (N_CHIPS * rows, cols) if axis == 0 else (rows, N_CHIPS * cols)


def _block_half(ref, axis, j, h):
    if axis == 0:
        rows = ref.shape[0] // N_CHIPS
        return ref.at[pl.ds(pl.multiple_of(j * rows + h * (rows // 2), 16), rows // 2), :]
    half, cols = ref.shape[0] // 2, ref.shape[1] // N_CHIPS
    return ref.at[pl.ds(pl.multiple_of(h * half, 16), half), pl.ds(pl.multiple_of(j * cols, LANES), cols)]


def _gather_start(shards, axes, after):
    lands = [lax.empty(_full_shape(s, ax), s.dtype) for s, ax in zip(shards, axes)]
    return _split_copy_start("gather", _gather_plan(axes), 3 * len(shards), shards, lands, after)


def _gather_wait(state, axes, after):
    return _split_copy_wait("gather", _gather_plan(axes), state, after)


def _split_copy_start(name, plan, count, ins, lands, after):
    arrays = list(ins) + list(lands)
    n_in, n = len(ins), len(arrays)

    def body(*refs):
        send, recv, token = refs[n + 1], refs[n + 2], refs[-1]
        for i, (src, dst, _, peer) in enumerate(plan(refs[:n_in], refs[n_in:n])):
            _remote(src, dst, send.at[i], recv.at[i], peer).start()
        token[...] = jnp.zeros_like(token)

    outs = pl.pallas_call(
        body, name=name + "_start",
        in_specs=[HBM] * n + [ANY],
        out_specs=(SEM, SEM, *[HBM] * n, pl.BlockSpec(memory_space=pltpu.VMEM)),
        out_shape=(pltpu.SemaphoreType.DMA((count,)), pltpu.SemaphoreType.DMA((count,)),
                   *[pltpu.HBM(a.shape, a.dtype) for a in arrays], TOKEN),
        input_output_aliases={i: 2 + i for i in range(n)},
        compiler_params=SPLIT_COPY)(*[_in_hbm(a) for a in arrays], after)
    return (outs[0], outs[1], outs[2:2 + n_in], outs[2 + n_in:2 + n]), outs[-1]


def _split_copy_wait(name, plan, state, after):
    send_sems, recv_sems, ins, lands = state
    arrays = list(ins) + list(lands)
    n_in, n = len(ins), len(arrays)

    def body(*refs):
        send, recv, token = refs[n], refs[n + 1], refs[-1]
        for i, (src, _, landing, peer) in enumerate(plan(refs[:n_in], refs[n_in:n])):
            cp = _remote(src, landing, send.at[i], recv.at[i], peer)
            cp.wait_send()
            cp.wait_recv()
        token[...] = jnp.zeros_like(token)

    outs = pl.pallas_call(
        body, name=name + "_wait",
        in_specs=[HBM] * n + [SEM, SEM, ANY],
        out_specs=(*[HBM] * n, pl.BlockSpec(memory_space=pltpu.VMEM)),
        out_shape=(*[pltpu.HBM(a.shape, a.dtype) for a in arrays], TOKEN),
        input_output_aliases={i: i for i in range(n)},
        compiler_params=SPLIT_COPY)(*arrays, send_sems, recv_sems, after)
    return outs[:n_in], outs[n_in:n], outs[-1]


def _gather_plan(axes):
    def plan(ins, lnd):
        x, y, c, others = _place()
        me = _chip_index(x, y)
        return [(ins[f].at[_half(ins[f].shape[0], c)], _block_half(lnd[f], ax, me, c),
                 _block_half(lnd[f], ax, _chip_index(cx, cy), c), (cx, cy, c))
                for f, ax in enumerate(axes) for cx, cy in others]
    return plan


def _pair_plan(axes):
    def plan(ins, lnd):
        x, y, c, _ = _place()
        return [(_block_half(ins[f], ax, j, 1 - c), lnd[f].at[j], lnd[f].at[j], (x, y, 1 - c))
                for f, ax in enumerate(axes) for j in range(N_CHIPS)]
    return plan


def _scatter_plan(ins, lnd):
    x, y, c, others = _place()
    return [(ins[f].at[_chip_index(cx, cy)], lnd[f].at[k], lnd[f].at[k], (cx, cy, c))
            for f in range(len(ins)) for k, (cx, cy) in enumerate(others)]


def _join_plan(ins, lnd):
    x, y, c, _ = _place()
    return [(lnd[f].at[_half(lnd[f].shape[0], c)], lnd[f].at[_half(lnd[f].shape[0], c)],
             lnd[f].at[_half(lnd[f].shape[0], 1 - c)], (x, y, 1 - c)) for f in range(len(lnd))]


def _gather_finish(lands, shards, axes):
    n = len(lands)

    def body(*refs):
        outs = refs[n:2 * n]
        send, recv = refs[2 * n:]
        x, y, c, others = _place()
        sib = (x, y, 1 - c)
        sends = []
        for f in range(n):
            for k, (cx, cy) in enumerate(others):
                landed = _block_half(outs[f], axes[f], _chip_index(cx, cy), c)
                cp = _remote(landed, landed, send.at[3 * f + k], recv.at[3 * f + k], sib)
                cp.start()
                sends.append(cp)
        for f in range(n):
            for k, (cx, cy) in enumerate(others):
                passed = _block_half(outs[f], axes[f], _chip_index(cx, cy), 1 - c)
                _remote(passed, passed, send.at[3 * f + k], recv.at[3 * f + k], sib).wait_recv()
        for cp in sends:
            cp.wait_send()

    full = pl.pallas_call(
        body, in_specs=[ANY] * n, out_specs=[ANY] * n,
        out_shape=[jax.ShapeDtypeStruct(a.shape, a.dtype) for a in lands],
        input_output_aliases={f: f for f in range(n)},
        scratch_shapes=[pltpu.SemaphoreType.DMA((3 * n,)), pltpu.SemaphoreType.DMA((3 * n,))],
        name="gather_finish")(*lands)

    def place(me_ref, *refs):
        ins, outs = refs[n:2 * n], refs[2 * n:]
        for f in range(n):
            outs[f][...] = ins[f][...]

    return pl.pallas_call(
        place,
        grid_spec=pltpu.PrefetchScalarGridSpec(
            num_scalar_prefetch=1, grid=(1,),
            in_specs=[ANY] * n + [pl.BlockSpec(s.shape, lambda i, me: (0, 0)) for s in shards],
            out_specs=[pl.BlockSpec(s.shape, (lambda i, me: (me[0], 0)) if ax == 0 else (lambda i, me: (0, me[0])))
                       for s, ax in zip(shards, axes)]),
        out_shape=[jax.ShapeDtypeStruct(a.shape, a.dtype) for a in full],
        input_output_aliases={1 + f: f for f in range(n)},
        name="place_own_layer", compiler_params=_params(("arbitrary",)))(
            _scalar(_chip_index(lax.axis_index("x"), lax.axis_index("y"))), *full, *shards)


def _half_blocks(part, axis):
    rows, cols = (part.shape[0] // N_CHIPS, part.shape[1]) if axis == 0 else (part.shape[0], part.shape[1] // N_CHIPS)
    return lax.empty((N_CHIPS, rows // 2, cols), part.dtype)


def _add_pair_layer(part, got, axis):
    _, half, cols = got.shape

    def body(c_ref, a_ref, b_ref, o_ref):
        o_ref[...] = (a_ref[...].astype(f32) + b_ref[...].astype(f32)).astype(o_ref.dtype)

    if axis == 0:
        part = part.reshape(N_CHIPS, 2, half, cols)
        mine = pl.BlockSpec((None, None, half, cols), lambda j, c: (j, c[0], 0, 0))
    else:
        mine = pl.BlockSpec((half, cols), lambda j, c: (c[0], j))
    return pl.pallas_call(
        body,
        grid_spec=pltpu.PrefetchScalarGridSpec(
            num_scalar_prefetch=1, grid=(N_CHIPS,),
            in_specs=[mine, pl.BlockSpec((None, half, cols), lambda j, c: (j, 0, 0))],
            out_specs=pl.BlockSpec((None, half, cols), lambda j, c: (j, 0, 0))),
        out_shape=jax.ShapeDtypeStruct(got.shape, part.dtype),
        name="add_pair_layer", compiler_params=_params(("arbitrary",)))(_scalar(lax.axis_index("c")), part, got)


def _scatter_start(sums, after):
    lands = [lax.empty((3,) + s.shape[1:], s.dtype) for s in sums]
    return _split_copy_start("scatter", _scatter_plan, 3 * len(sums), sums, lands, after)


def _scatter_wait(state, after):
    return _split_copy_wait("scatter", _scatter_plan, state, after)


ELEMENTWISE_BLOCK_BYTES = 1 << 20


def _row_tile(rows, cols):
    best = None
    for tile in range(8, rows + 1, 8):
        if rows % tile == 0 and tile * cols * 4 <= ELEMENTWISE_BLOCK_BYTES:
            best = tile
    return best or rows


def _add_slots(chip_sums, slots):
    depth, _, half, cols = chip_sums.shape

    def body(at_ref, own_ref, s_ref, o_ref):
        acc = own_ref[...].astype(f32)
        for k in range(3):
            acc = acc + s_ref[k].astype(f32)
        o_ref[...] = acc

    at = jnp.concatenate([_scalar(_chip_index(lax.axis_index("x"), lax.axis_index("y"))), _scalar(lax.axis_index("c"))])
    out = pl.pallas_call(
        body,
        grid_spec=pltpu.PrefetchScalarGridSpec(
            num_scalar_prefetch=1, grid=(depth,),
            in_specs=[pl.BlockSpec((None, None, half, cols), lambda l, at: (l, at[0], 0, 0)),
                      pl.BlockSpec((3, None, half, cols), lambda l, at: (0, l, 0, 0))],
            out_specs=pl.BlockSpec((None, None, half, cols), lambda l, at: (l, at[1], 0, 0))),
        out_shape=jax.ShapeDtypeStruct((depth, 2, half, cols), f32),
        name="add_slots", compiler_params=_params(("arbitrary",)))(at, chip_sums, slots)
    return out.reshape(depth, 2 * half, cols)


def _adamw_math(w, grad, m, v):
    nm = ADAM_B1 * m + (1.0 - ADAM_B1) * grad
    nv = ADAM_B2 * v + (1.0 - ADAM_B2) * (grad * grad)
    m_hat = nm / (1.0 - ADAM_B1 ** ADAM_STEP)
    v_hat = nv / (1.0 - ADAM_B2 ** ADAM_STEP)
    return nm, nv, -ADAM_LR * (m_hat / (jnp.sqrt(v_hat) + ADAM_EPS) + ADAM_WD * w)


def _adamw(w, g, m, v):
    shape = w.shape
    flat = [a.reshape(-1, shape[-1]) for a in (w, g, m, v)]
    tile = _row_tile(flat[0].shape[0], shape[-1])

    def body(w_ref, g_ref, m_ref, v_ref, d_ref, nm_ref, nv_ref):
        nm, nv, step = _adamw_math(w_ref[...], g_ref[...], m_ref[...], v_ref[...])
        d_ref[...] = step
        nm_ref[...] = nm
        nv_ref[...] = nv

    spec = _rows(shape[-1], tile)
    out = jax.ShapeDtypeStruct(flat[0].shape, f32)
    res = pl.pallas_call(
        body, grid=(flat[0].shape[0] // tile,),
        in_specs=[spec] * 4, out_specs=[spec] * 3, out_shape=[out] * 3,
        name="adamw", compiler_params=_params(("arbitrary",)))(*flat)
    return [r.reshape(shape) for r in res]


def _adamw_layer(l, w, m, v, g, outs, after):
    depth, rows, cols = w.shape
    tile = _row_tile(rows, cols)

    def body(w_ref, m_ref, v_ref, g_ref, *refs):
        go_ref, d_ref, nm_ref, nv_ref = refs[5:]
        grad = g_ref[...]
        nm, nv, step = _adamw_math(w_ref[...], grad, m_ref[...], v_ref[...])
        go_ref[...] = grad
        d_ref[...] = step
        nm_ref[...] = nm
        nv_ref[...] = nv

    layer = pl.BlockSpec((None, tile, cols), lambda i: (l, i, 0))
    return pl.pallas_call(
        body, grid=(rows // tile,),
        in_specs=[layer] * 3 + [_rows(cols, tile)] + [ANY] * 5, out_specs=[layer] * 4,
        out_shape=[jax.ShapeDtypeStruct(w.shape, f32)] * 4,
        input_output_aliases={4 + k: k for k in range(4)},
        name="adamw_layer", compiler_params=_params(("arbitrary",)))(w, m, v, g, *outs, after)


SMALL = ("w_conv", "w_pool", "pool_scale", "sgu_ln_g", "w_spatial", "b_spatial", "ln1_g", "ln1_b", "ln2_g", "ln2_b")
WEIGHTS = ("w_in", "w_conv", "w_pool", "pool_scale", "sgu_ln_g", "w_spatial", "b_spatial", "w_o", "ln1_g", "ln1_b",
           "w_gate_up", "w_down", "ln2_g", "ln2_b")
BIG = ("w_in", "w_o", "w_gate_up", "w_down")
GROUPS = (("w_in", "w_o"), ("w_gate_up", "w_down"))
GROUP_AXES = ((0, 0), (1, 0))
SCATTER_HOOKS = 2
SMALL_LAYER_ROWS = 1024


def _pack_layer(arrays):
    flat = jnp.concatenate([a.reshape(-1) for a in arrays])
    return jnp.pad(flat, (0, SMALL_LAYER_ROWS * LANES - flat.shape[0])).reshape(SMALL_LAYER_ROWS, LANES)


def _unpack_layers(flat, shapes):
    out, at = {}, 0
    for name, shape in shapes.items():
        size = 1
        for d in shape:
            size *= d
        out[name] = flat[:, at:at + size].reshape((flat.shape[0],) + tuple(shape))
        at += size
    return out


def kernel(x, w_in, w_conv, w_pool, pool_scale, sgu_ln_g, w_spatial, b_spatial, w_o, ln1_g, ln1_b, w_gate_up, w_down, ln2_g, ln2_b, loss_target, m_w_in, m_w_conv, m_w_pool, m_pool_scale, m_sgu_ln_g, m_w_spatial, m_b_spatial, m_w_o, m_ln1_g, m_ln1_b, m_w_gate_up, m_w_down, m_ln2_g, m_ln2_b, v_w_in, v_w_conv, v_w_pool, v_pool_scale, v_sgu_ln_g, v_w_spatial, v_b_spatial, v_w_o, v_ln1_g, v_ln1_b, v_w_gate_up, v_w_down, v_ln2_g, v_ln2_b):
    weights = dict(w_in=w_in, w_conv=w_conv, w_pool=w_pool, pool_scale=pool_scale, sgu_ln_g=sgu_ln_g, w_spatial=w_spatial,
                   b_spatial=b_spatial, w_o=w_o, ln1_g=ln1_g, ln1_b=ln1_b, w_gate_up=w_gate_up, w_down=w_down, ln2_g=ln2_g, ln2_b=ln2_b)
    m_in = dict(w_in=m_w_in, w_conv=m_w_conv, w_pool=m_w_pool, pool_scale=m_pool_scale, sgu_ln_g=m_sgu_ln_g, w_spatial=m_w_spatial,
                b_spatial=m_b_spatial, w_o=m_w_o, ln1_g=m_ln1_g, ln1_b=m_ln1_b, w_gate_up=m_w_gate_up, w_down=m_w_down,
                ln2_g=m_ln2_g, ln2_b=m_ln2_b)
    v_in = dict(w_in=v_w_in, w_conv=v_w_conv, w_pool=v_w_pool, pool_scale=v_pool_scale, sgu_ln_g=v_sgu_ln_g, w_spatial=v_w_spatial,
                b_spatial=v_b_spatial, w_o=v_w_o, ln1_g=v_ln1_g, ln1_b=v_ln1_b, w_gate_up=v_w_gate_up, w_down=v_w_down,
                ln2_g=v_ln2_g, ln2_b=v_ln2_b)
    depth = w_in.shape[0]
    conv_cols = w_conv.shape[2]
    chip = _chip_index(lax.axis_index("x"), lax.axis_index("y"))

    conv_flat = jnp.pad(w_conv.reshape(-1), (0, 16 * LANES - w_conv.size)).reshape(1, 16, LANES)
    conv_full = _gather_shards([conv_flat])[0].reshape(N_CHIPS, 16 * LANES)[:, :w_conv.size].reshape(N_CHIPS, depth, 3, conv_cols)
    conv_full = conv_full.transpose(1, 2, 0, 3).reshape(depth, 3, N_CHIPS * conv_cols)

    big_w = dict(w_in=jnp.swapaxes(w_in, 1, 2), w_o=w_o, w_gate_up=w_gate_up, w_down=w_down)
    big_m = dict(w_in=jnp.swapaxes(m_w_in, 1, 2), w_o=m_w_o, w_gate_up=m_w_gate_up, w_down=m_w_down)
    big_v = dict(w_in=jnp.swapaxes(v_w_in, 1, 2), w_o=v_w_o, w_gate_up=v_w_gate_up, w_down=v_w_down)

    def send(l, g, after):
        return _gather_start([big_w[n][l].astype(bf16) for n in GROUPS[g]], GROUP_AXES[g], after)

    def receive(g, flight, after):
        shards, lands, token = _gather_wait(flight, GROUP_AXES[g], after)
        return shards, lands, token

    act = x[0]
    layers, saved = [], []
    flight, token = send(0, 0, conv_full)
    for l in range(depth):
        w = dict(w_conv=conv_full[l], w_pool=w_pool[l], pool_scale=pool_scale[l][None], sgu_ln_g=sgu_ln_g[l][None],
                 w_spatial=w_spatial[l], b_spatial=b_spatial[l][:, :, None], ln1_g=ln1_g[l][None], ln1_b=ln1_b[l][None],
                 ln2_g=ln2_g[l][None], ln2_b=ln2_b[l][None])
        shards, lands, token = receive(0, flight, act)
        flight, token = send(l, 1, token)
        w.update(zip(GROUPS[0], _gather_finish(lands, shards, GROUP_AXES[0])))
        sv = _fwd_mix(act, w, token)
        shards, lands, token = receive(1, flight, sv["xhat1"])
        if l + 1 < depth:
            flight, token = send(l + 1, 0, token)
        w.update(zip(GROUPS[1], _gather_finish(lands, shards, GROUP_AXES[1])))
        act = _fwd_mlp(sv, w, token)
        layers.append(w)
        saved.append(sv)

    big_outs = {n: [lax.empty(big_w[n].shape, f32) for _ in range(4)] for n in BIG}
    small_sums = [None] * depth
    pending, updates = [], []
    latest = dict(token=None)

    def begin(l, g, parts):
        axes = GROUP_AXES[g] + (0,) * (len(parts) - len(GROUPS[g]))
        lands = [_half_blocks(p, ax) for p, ax in zip(parts, axes)]
        flight, latest["token"] = _split_copy_start("pair", _pair_plan(axes), N_CHIPS * len(parts), parts, lands, latest["token"])
        pending.append(dict(l=l, g=g, axes=axes, step="pair", age=0, flight=flight))

    def advance(st, recent):
        if st["step"] == "pair":
            parts, got, _ = _split_copy_wait("pair", _pair_plan(st["axes"]), st["flight"], recent)
            sums = [_add_pair_layer(p, q, ax) for p, q, ax in zip(parts, got, st["axes"])]
            st["flight"], latest["token"] = _scatter_start(sums, latest["token"])
            st["step"] = "scatter"
        elif st["step"] == "scatter":
            sums, slots, _ = _scatter_wait(st["flight"], recent)
            filled = [_add_slots(cs[None], s[:, None])[0] for cs, s in zip(sums, slots)]
            st["flight"], latest["token"] = _split_copy_start("join", _join_plan, len(filled), [], filled, latest["token"])
            st["step"] = "join"
        else:
            _, summed, _ = _split_copy_wait("join", _join_plan, st["flight"], recent)
            updates.extend((st["l"], n, total) for n, total in zip(GROUPS[st["g"]], summed))
            if st["g"] == 0:
                small_sums[st["l"]] = summed[-1]
            st["step"] = "done"
        st["age"] = 0

    def hook(recent):
        for st in reversed(list(pending)):
            st["age"] += 1
            if st["age"] >= SCATTER_HOOKS or st["step"] != "scatter":
                advance(st, recent)
                if st["step"] == "done":
                    pending.remove(st)
        return latest["token"]

    def update(count, recent):
        for l, n, total in updates[:count]:
            big_outs[n] = _adamw_layer(l, big_w[n], big_m[n], big_v[n], total, big_outs[n], latest["token"])
            recent = big_outs[n][1]
        del updates[:count]
        return recent

    grad_x, sq = _loss_head(act, loss_target[0])
    latest["token"] = sq
    grads = [None] * depth
    for l in reversed(range(depth)):
        dh, g_mlp = _bwd_mlp(grad_x, layers[l], saved[l], latest["token"], hook)
        hook(g_mlp["w_down"])
        begin(l, 1, [g_mlp[n] for n in GROUPS[1]])
        grad_x, g_mix = _bwd_mix(dh, layers[l], saved[l], latest["token"], hook)
        grads[l] = dict(g_mlp, **g_mix)
        hook(g_mix["w_o"])
        begin(l, 0, [g_mix[n] for n in GROUPS[0]] + [_pack_layer([grads[l][n] for n in SMALL])])
    recent = g_mix["w_o"]
    while pending:
        recent = update(-(-len(updates) // 2), recent)
        hook(recent)
    update(len(updates), recent)
    loss = lax.psum(0.5 / D_MODEL * jnp.sum(sq), ("x", "y", "c"))

    small_sum = _gather_shards([jnp.stack(small_sums)])[0].reshape(depth, SMALL_LAYER_ROWS * LANES)
    grad = {n: [jnp.swapaxes(o, 1, 2) for o in big_outs[n]] if n == "w_in" else big_outs[n] for n in BIG}
    delta = {n: o[1] for n, o in grad.items()}
    new_m = {n: o[2] for n, o in grad.items()}
    new_v = {n: o[3] for n, o in grad.items()}
    grad = {n: o[0] for n, o in grad.items()}
    grad.update(_unpack_layers(small_sum, {n: (3, N_CHIPS * conv_cols) if n == "w_conv" else weights[n].shape[1:] for n in SMALL}))
    grad["w_conv"] = lax.dynamic_slice_in_dim(grad["w_conv"], chip * conv_cols, conv_cols, axis=2)

    delta["w_conv"], new_m["w_conv"], new_v["w_conv"] = _adamw(w_conv, grad["w_conv"], m_w_conv, v_w_conv)
    rest = [n for n in SMALL if n != "w_conv"]
    rest_shapes = {n: weights[n].shape[1:] for n in rest}
    packed = [jnp.concatenate([_pack_layer([src[n][l] for n in rest]) for l in range(depth)]) for src in (weights, grad, m_in, v_in)]
    for dst, res in zip((delta, new_m, new_v), _adamw(*packed)):
        dst.update(_unpack_layers(res.reshape(depth, SMALL_LAYER_ROWS * LANES), rest_shapes))

    return (loss, grad_x[None], *[grad[n] for n in WEIGHTS], *[delta[n] for n in WEIGHTS],
            *[new_m[n] for n in WEIGHTS], *[new_v[n] for n in WEIGHTS])
```

```python
import functools

import jax
import jax.numpy as jnp
from jax import lax
from jax.experimental import pallas as pl
from jax.experimental.pallas import tpu as pltpu

f32 = jnp.float32
bf16 = jnp.bfloat16

D_MODEL = 1024
DEPTH = 4
CONV_W = 384
POOL_W = 256
SGU_W = 384
IN_W = 3 * CONV_W + POOL_W + 2 * SGU_W
D_FF = 2816
CHUNK = 128
HEAD = 64
POOL_WINDOWS = (2, 4, 8, 16)
ALPHA = float((2 * DEPTH) ** 0.25)
LN_EPS = 1e-5
ADAM_LR = 0.001
ADAM_B1 = 0.9
ADAM_B2 = 0.999
ADAM_EPS = 1e-08
ADAM_WD = 0.01
ADAM_STEP = 10

LANES = 128
TOKEN_TILE = 256
N_CHIPS = 4
VMEM_LIMIT = 56 * 1024 * 1024

BLK_XA, BLK_GB, BLK_GC, BLK_P, BLK_U, BLK_V = 0, 3, 6, 9, 11, 14

MESH = pl.DeviceIdType.MESH


def _params(sem=None):
    return pltpu.CompilerParams(dimension_semantics=sem, vmem_limit_bytes=VMEM_LIMIT)


def _rows(width, tile=TOKEN_TILE):
    return pl.BlockSpec((tile, width), lambda i: (i, 0))


def _resident(shape):
    zeros = (0,) * len(shape)
    return pl.BlockSpec(shape, lambda *_: zeros, pipeline_mode=pl.Buffered(1))


def _nt(a, b):
    return lax.dot_general(a, b, (((1,), (1,)), ((), ())), preferred_element_type=f32)


def _tn(a, b):
    return lax.dot_general(a, b, (((0,), (0,)), ((), ())), preferred_element_type=f32)


def _mm(a, b):
    return jnp.dot(a, b, preferred_element_type=f32)


def _norm_fwd(z):
    mu = jnp.mean(z, axis=-1, keepdims=True)
    zc = z - mu
    var = jnp.mean(zc * zc, axis=-1, keepdims=True)
    rstd = lax.rsqrt(var + LN_EPS)
    return zc * rstd, rstd


def _norm_bwd(dxhat, xhat, rstd):
    m1 = jnp.mean(dxhat, axis=-1, keepdims=True)
    m2 = jnp.mean(dxhat * xhat, axis=-1, keepdims=True)
    return rstd * (dxhat - m1 - xhat * m2)


def _proj(x, w_in_b, after):
    s = x.shape[0]

    def body(x_ref, w_ref, after_ref, p_ref, xb_ref):
        xb = x_ref[...].astype(bf16)
        xb_ref[...] = xb
        p_ref[...] = _nt(xb, w_ref[...])

    return pl.pallas_call(
        body, grid=(s // TOKEN_TILE,),
        in_specs=[_rows(D_MODEL), _resident((IN_W, D_MODEL)), pl.BlockSpec(memory_space=pl.ANY)],
        out_specs=[_rows(IN_W), _rows(D_MODEL)],
        out_shape=[jax.ShapeDtypeStruct((s, IN_W), f32), jax.ShapeDtypeStruct((s, D_MODEL), bf16)],
        name="proj", compiler_params=_params(("arbitrary",)))(x, w_in_b, after)


def _row_ranges(parts):
    out, at = [], 0
    for p in parts:
        out.append((at, at + p.shape[1]))
        at += p.shape[1]
    return out


def _wo_ln1(mix, x, w_o_b, g, b):
    s = x.shape[0]
    n = len(mix)
    ranges = _row_ranges(mix)

    def body(*refs):
        m_refs = refs[:n]
        x_ref, w_ref, g_ref, b_ref, xhat_ref, rstd_ref, hb_ref = refs[n:]
        z = ALPHA * x_ref[...]
        for m_ref, (lo, hi) in zip(m_refs, ranges):
            z = z + _mm(m_ref[...], w_ref[lo:hi, :])
        xhat, rstd = _norm_fwd(z)
        xhat_ref[...] = xhat
        rstd_ref[...] = rstd
        hb_ref[...] = (xhat * g_ref[...] + b_ref[...]).astype(bf16)

    return pl.pallas_call(
        body, grid=(s // TOKEN_TILE,),
        in_specs=[_rows(m.shape[1]) for m in mix] + [_rows(D_MODEL), _resident((D_MODEL, D_MODEL)), _resident((1, D_MODEL)),
                                                     _resident((1, D_MODEL))],
        out_specs=[_rows(D_MODEL), _rows(1), _rows(D_MODEL)],
        out_shape=[jax.ShapeDtypeStruct((s, D_MODEL), f32), jax.ShapeDtypeStruct((s, 1), f32),
                   jax.ShapeDtypeStruct((s, D_MODEL), bf16)],
        name="wo_ln1", compiler_params=_params(("arbitrary",)))(*mix, x, w_o_b, g, b)


def _mlp_fwd(xhat1, g1, b1, w_gu_b, w_down_b, g2, b2, after):
    s = xhat1.shape[0]

    def body(xh_ref, g1_ref, b1_ref, wgu_ref, wd_ref, g2_ref, b2_ref, after_ref, gu_ref, xhat2_ref, rstd2_ref, y_ref):
        h = xh_ref[...] * g1_ref[...] + b1_ref[...]
        gu = _mm(h.astype(bf16), wgu_ref[...])
        gu_ref[...] = gu
        gate = gu[:, :D_FF]
        act = gate * jax.nn.sigmoid(gate) * gu[:, D_FF:]
        z = ALPHA * h + _mm(act.astype(bf16), wd_ref[...])
        xhat2, rstd2 = _norm_fwd(z)
        xhat2_ref[...] = xhat2
        rstd2_ref[...] = rstd2
        y_ref[...] = xhat2 * g2_ref[...] + b2_ref[...]

    vec = _resident((1, D_MODEL))
    return pl.pallas_call(
        body, grid=(s // TOKEN_TILE,),
        in_specs=[_rows(D_MODEL), vec, vec, _resident((D_MODEL, 2 * D_FF)), _resident((D_FF, D_MODEL)), vec, vec,
                  pl.BlockSpec(memory_space=pl.ANY)],
        out_specs=[_rows(2 * D_FF), _rows(D_MODEL), _rows(1), _rows(D_MODEL)],
        out_shape=[jax.ShapeDtypeStruct((s, 2 * D_FF), f32), jax.ShapeDtypeStruct((s, D_MODEL), f32),
                   jax.ShapeDtypeStruct((s, 1), f32), jax.ShapeDtypeStruct((s, D_MODEL), f32)],
        name="mlp_fwd", compiler_params=_params(("arbitrary",)))(xhat1, g1, b1, w_gu_b, w_down_b, g2, b2, after)


def _loss_head(y, target):
    s = y.shape[0]

    def body(y_ref, t_ref, dy_ref, sq_ref):
        @pl.when(pl.program_id(0) == 0)
        def _():
            sq_ref[...] = jnp.zeros_like(sq_ref)

        e = y_ref[...] - t_ref[...]
        dy_ref[...] = e * (1.0 / D_MODEL)
        sq_ref[...] += jnp.sum(e * e, axis=0, keepdims=True)

    return pl.pallas_call(
        body, grid=(s // TOKEN_TILE,),
        in_specs=[_rows(D_MODEL), _rows(D_MODEL)],
        out_specs=[_rows(D_MODEL), pl.BlockSpec((1, D_MODEL), lambda i: (0, 0))],
        out_shape=[jax.ShapeDtypeStruct((s, D_MODEL), f32), jax.ShapeDtypeStruct((1, D_MODEL), f32)],
        name="loss_head", compiler_params=_params(("arbitrary",)))(y, target)


def _mlp_bwd(dy, xhat2, rstd2, g2, gu, w_gu_b, w_down_b, after):
    s = dy.shape[0]

    def body(dy_ref, xh_ref, rs_ref, g2_ref, gu_ref, wgu_ref, wd_ref, after_ref, dz_ref, act_ref, dgu_ref, dh_ref, gg_ref, gb_ref):
        @pl.when(pl.program_id(0) == 0)
        def _():
            gg_ref[...] = jnp.zeros_like(gg_ref)
            gb_ref[...] = jnp.zeros_like(gb_ref)

        dy_t = dy_ref[...]
        xhat = xh_ref[...]
        gg_ref[...] += jnp.sum(dy_t * xhat, axis=0, keepdims=True)
        gb_ref[...] += jnp.sum(dy_t, axis=0, keepdims=True)
        dz = _norm_bwd(dy_t * g2_ref[...], xhat, rs_ref[...])
        dzb = dz.astype(bf16)
        dz_ref[...] = dzb
        dact = _nt(dzb, wd_ref[...])
        gate = gu_ref[:, :D_FF]
        up = gu_ref[:, D_FF:]
        sg = jax.nn.sigmoid(gate)
        silu = gate * sg
        act_ref[...] = (silu * up).astype(bf16)
        dgu_ref[:, :D_FF] = (dact * up * (sg * (1.0 + gate * (1.0 - sg)))).astype(bf16)
        dgu_ref[:, D_FF:] = (dact * silu).astype(bf16)
        dh_ref[...] = ALPHA * dz + _nt(dgu_ref[...], wgu_ref[...])

    vec_out = pl.BlockSpec((1, D_MODEL), lambda i: (0, 0))
    return pl.pallas_call(
        body, grid=(s // TOKEN_TILE,),
        in_specs=[_rows(D_MODEL), _rows(D_MODEL), _rows(1), _resident((1, D_MODEL)), _rows(2 * D_FF),
                  _resident((D_MODEL, 2 * D_FF)), _resident((D_FF, D_MODEL)), pl.BlockSpec(memory_space=pl.ANY)],
        out_specs=[_rows(D_MODEL), _rows(D_FF), _rows(2 * D_FF), _rows(D_MODEL), vec_out, vec_out],
        out_shape=[jax.ShapeDtypeStruct((s, D_MODEL), bf16), jax.ShapeDtypeStruct((s, D_FF), bf16),
                   jax.ShapeDtypeStruct((s, 2 * D_FF), bf16), jax.ShapeDtypeStruct((s, D_MODEL), f32),
                   jax.ShapeDtypeStruct((1, D_MODEL), f32), jax.ShapeDtypeStruct((1, D_MODEL), f32)],
        name="mlp_bwd", compiler_params=_params(("arbitrary",)))(dy, xhat2, rstd2, g2, gu, w_gu_b, w_down_b, after)


def _ln1_wo_bwd(dh, xhat1, rstd1, g1, w_o_b, after):
    s = dh.shape[0]

    def body(dh_ref, xh_ref, rs_ref, g1_ref, w_ref, after_ref, dz_ref, dzb_ref, dm_ref, gg_ref, gb_ref):
        @pl.when(pl.program_id(0) == 0)
        def _():
            gg_ref[...] = jnp.zeros_like(gg_ref)
            gb_ref[...] = jnp.zeros_like(gb_ref)

        dh_t = dh_ref[...]
        xhat = xh_ref[...]
        gg_ref[...] += jnp.sum(dh_t * xhat, axis=0, keepdims=True)
        gb_ref[...] += jnp.sum(dh_t, axis=0, keepdims=True)
        dz = _norm_bwd(dh_t * g1_ref[...], xhat, rs_ref[...])
        dz_ref[...] = dz
        dzb = dz.astype(bf16)
        dzb_ref[...] = dzb
        dm_ref[...] = _nt(dzb, w_ref[...])

    vec_out = pl.BlockSpec((1, D_MODEL), lambda i: (0, 0))
    return pl.pallas_call(
        body, grid=(s // TOKEN_TILE,),
        in_specs=[_rows(D_MODEL), _rows(D_MODEL), _rows(1), _resident((1, D_MODEL)), _resident((D_MODEL, D_MODEL)),
                  pl.BlockSpec(memory_space=pl.ANY)],
        out_specs=[_rows(D_MODEL), _rows(D_MODEL), _rows(D_MODEL), vec_out, vec_out],
        out_shape=[jax.ShapeDtypeStruct((s, D_MODEL), f32), jax.ShapeDtypeStruct((s, D_MODEL), bf16),
                   jax.ShapeDtypeStruct((s, D_MODEL), f32), jax.ShapeDtypeStruct((1, D_MODEL), f32),
                   jax.ShapeDtypeStruct((1, D_MODEL), f32)],
        name="ln1_wo_bwd", compiler_params=_params(("arbitrary",)))(dh, xhat1, rstd1, g1, w_o_b, after)


def _dx(dz1, dparts, w_in_t):
    s = dz1.shape[0]
    n = len(dparts)
    ranges = _row_ranges(dparts)

    def body(*refs):
        d_refs = refs[:n]
        dz_ref, w_ref, dx_ref = refs[n:]
        acc = ALPHA * dz_ref[...]
        for d_ref, (lo, hi) in zip(d_refs, ranges):
            acc = acc + _mm(d_ref[...], w_ref[lo:hi, :])
        dx_ref[...] = acc

    return pl.pallas_call(
        body, grid=(s // TOKEN_TILE,),
        in_specs=[_rows(d.shape[1]) for d in dparts] + [_rows(D_MODEL), _resident((IN_W, D_MODEL))],
        out_specs=_rows(D_MODEL),
        out_shape=jax.ShapeDtypeStruct((s, D_MODEL), f32),
        name="dx", compiler_params=_params(("arbitrary",)))(*dparts, dz1, w_in_t)


def _weight_grad_rows(parts, b, bn):
    s, n_cols = b.shape
    n = len(parts)
    ranges = _row_ranges(parts)
    m = ranges[-1][1]

    def body(*refs):
        p_refs = refs[:n]
        b_ref, o_ref = refs[n:]
        for p_ref, (lo, hi) in zip(p_refs, ranges):
            o_ref[lo:hi, :] = _tn(p_ref[...], b_ref[...]).astype(bf16)

    return pl.pallas_call(
        body, grid=(n_cols // bn,),
        in_specs=[_resident(p.shape) for p in parts] + [pl.BlockSpec((s, bn), lambda j: (0, j))],
        out_specs=pl.BlockSpec((m, bn), lambda j: (0, j)),
        out_shape=jax.ShapeDtypeStruct((m, n_cols), bf16),
        name="weight_grad_rows", compiler_params=_params(("arbitrary",)))(*parts, b)


def _weight_grad(a, b, bm, bn, after):
    s, m = a.shape
    n = b.shape[1]

    def body(a_ref, b_ref, after_ref, o_ref):
        o_ref[...] = _tn(a_ref[...], b_ref[...]).astype(bf16)

    return pl.pallas_call(
        body, grid=(m // bm, n // bn),
        in_specs=[pl.BlockSpec((s, bm), lambda i, j: (0, i)), pl.BlockSpec((s, bn), lambda i, j: (0, j)),
                  pl.BlockSpec(memory_space=pl.ANY)],
        out_specs=pl.BlockSpec((bm, bn), lambda i, j: (i, j)),
        out_shape=jax.ShapeDtypeStruct((m, n), bf16),
        name="weight_grad", compiler_params=_params(("arbitrary", "arbitrary")))(a, b, after)


def _shift_down(a, k):
    row = lax.broadcasted_iota(jnp.int32, a.shape, 0)
    return jnp.where(row >= k, pltpu.roll(a, k, 0), 0.0)


def _shift_up(a, k):
    n = a.shape[0]
    row = lax.broadcasted_iota(jnp.int32, a.shape, 0)
    return jnp.where(row < n - k, pltpu.roll(a, n - k, 0), 0.0)


def _slab(s, block):
    return pl.BlockSpec((s, LANES), lambda k: (0, block + k))


def _conv_y(z, w):
    return w[0:1, :] * _shift_down(z, 2) + w[1:2, :] * _shift_down(z, 1) + w[2:3, :] * z


def _conv_fwd(proj, w_conv):
    s = proj.shape[0]

    def body(xa_ref, gb_ref, gc_ref, w_ref, o_ref):
        z = gc_ref[...] * xa_ref[...]
        o_ref[...] = (gb_ref[...] * _conv_y(z, w_ref[...])).astype(bf16)

    return pl.pallas_call(
        body, grid=(CONV_W // LANES,),
        in_specs=[_slab(s, BLK_XA), _slab(s, BLK_GB), _slab(s, BLK_GC), pl.BlockSpec((3, LANES), lambda k: (0, k))],
        out_specs=_slab(s, 0),
        out_shape=jax.ShapeDtypeStruct((s, CONV_W), bf16),
        name="conv_fwd", compiler_params=_params(("arbitrary",)))(proj, proj, proj, w_conv)


def _conv_bwd(proj, dmix, w_conv, after):
    s = proj.shape[0]

    def body(xa_ref, gb_ref, gc_ref, dy_ref, w_ref, after_ref, dxa_ref, dgb_ref, dgc_ref, dw_ref):
        xa = xa_ref[...]
        gc = gc_ref[...]
        w = w_ref[...]
        z = gc * xa
        dya = dy_ref[...]
        dgb_ref[...] = (dya * _conv_y(z, w)).astype(bf16)
        dy = dya * gb_ref[...]
        dz = w[2:3, :] * dy + w[1:2, :] * _shift_up(dy, 1) + w[0:1, :] * _shift_up(dy, 2)
        dxa_ref[...] = (dz * gc).astype(bf16)
        dgc_ref[...] = (dz * xa).astype(bf16)
        dw_ref[0:1, :] = jnp.sum(dy * _shift_down(z, 2), axis=0, keepdims=True)
        dw_ref[1:2, :] = jnp.sum(dy * _shift_down(z, 1), axis=0, keepdims=True)
        dw_ref[2:3, :] = jnp.sum(dy * z, axis=0, keepdims=True)

    out = jax.ShapeDtypeStruct((s, CONV_W), bf16)
    return pl.pallas_call(
        body, grid=(CONV_W // LANES,),
        in_specs=[_slab(s, BLK_XA), _slab(s, BLK_GB), _slab(s, BLK_GC), _slab(s, 0), pl.BlockSpec((3, LANES), lambda k: (0, k)),
                  pl.BlockSpec(memory_space=pl.ANY)],
        out_specs=[_slab(s, 0), _slab(s, 0), _slab(s, 0), pl.BlockSpec((3, LANES), lambda k: (0, k))],
        out_shape=[out, out, out, jax.ShapeDtypeStruct((3, CONV_W), f32)],
        name="conv_bwd", compiler_params=_params(("arbitrary",)))(proj, proj, proj, dmix, w_conv, after)


def _pool_window(k):
    lane = lax.broadcasted_iota(jnp.int32, (1, LANES), 1)
    low = lane < HEAD
    first = k == 0
    wlen = jnp.where(low, jnp.where(first, POOL_WINDOWS[0], POOL_WINDOWS[2]), jnp.where(first, POOL_WINDOWS[1], POOL_WINDOWS[3]))
    return wlen, low, first


def _pool_diff(p, k):
    wlen, low, first = _pool_window(k)
    s2 = p + _shift_down(p, 1)
    s4 = s2 + _shift_down(s2, 2)
    s8 = s4 + _shift_down(s4, 4)
    s16 = s8 + _shift_down(s8, 8)
    win = jnp.where(low, jnp.where(first, s2, s8), jnp.where(first, s4, s16))
    row = lax.broadcasted_iota(jnp.int32, p.shape, 0)
    count = jnp.minimum(row + 1, wlen).astype(f32)
    return win / count - p, count


def _pool_weight(w_ref):
    zero = jnp.zeros((HEAD, HEAD), f32)
    top = jnp.concatenate([w_ref[0], zero], axis=1)
    bottom = jnp.concatenate([zero, w_ref[1]], axis=1)
    return jnp.concatenate([top, bottom], axis=0).astype(bf16)


def _pool_fwd(proj, w_pool, pool_scale):
    s = proj.shape[0]

    def body(p_ref, w_ref, sc_ref, o_ref):
        d, _ = _pool_diff(p_ref[...], pl.program_id(0))
        o_ref[...] = (_mm(d.astype(bf16), _pool_weight(w_ref)) * sc_ref[...]).astype(bf16)

    return pl.pallas_call(
        body, grid=(POOL_W // LANES,),
        in_specs=[_slab(s, BLK_P), pl.BlockSpec((2, HEAD, HEAD), lambda k: (k, 0, 0)), pl.BlockSpec((1, LANES), lambda k: (0, k))],
        out_specs=_slab(s, 0),
        out_shape=jax.ShapeDtypeStruct((s, POOL_W), bf16),
        name="pool_fwd", compiler_params=_params(("arbitrary",)))(proj, w_pool, pool_scale)


def _pool_bwd(proj, dmix, w_pool, pool_scale):
    s = proj.shape[0]

    def body(p_ref, dy_ref, w_ref, sc_ref, dp_ref, dw_ref, dsc_ref):
        k = pl.program_id(0)
        d, count = _pool_diff(p_ref[...], k)
        wbd = _pool_weight(w_ref)
        db = d.astype(bf16)
        dyb = dy_ref[...]
        dsc_ref[...] = jnp.sum(dyb * _mm(db, wbd), axis=0, keepdims=True)
        dpre = (dyb * sc_ref[...]).astype(bf16)
        dwbd = _tn(db, dpre)
        dw_ref[0] = dwbd[:HEAD, :HEAD]
        dw_ref[1] = dwbd[HEAD:, HEAD:]
        dd = _nt(dpre, wbd)
        e = dd / count
        wlen, low, first = _pool_window(k)
        a2 = e + _shift_up(e, 1)
        a4 = a2 + _shift_up(a2, 2)
        a8 = a4 + _shift_up(a4, 4)
        a16 = a8 + _shift_up(a8, 8)
        back = jnp.where(low, jnp.where(first, a2, a8), jnp.where(first, a4, a16))
        dp_ref[...] = (back - dd).astype(bf16)

    return pl.pallas_call(
        body, grid=(POOL_W // LANES,),
        in_specs=[_slab(s, BLK_P), _slab(s, CONV_W // LANES), pl.BlockSpec((2, HEAD, HEAD), lambda k: (k, 0, 0)),
                  pl.BlockSpec((1, LANES), lambda k: (0, k))],
        out_specs=[_slab(s, 0), pl.BlockSpec((2, HEAD, HEAD), lambda k: (k, 0, 0)), pl.BlockSpec((1, LANES), lambda k: (0, k))],
        out_shape=[jax.ShapeDtypeStruct((s, POOL_W), bf16), jax.ShapeDtypeStruct((4, HEAD, HEAD), f32),
                   jax.ShapeDtypeStruct((1, POOL_W), f32)],
        name="pool_bwd", compiler_params=_params(("arbitrary",)))(proj, dmix, w_pool, pool_scale)


INV_SQRT2 = 0.7071067811865476
INV_SQRT_2PI = 0.3989422804014327


def _gelu(x):
    return 0.5 * x * (1.0 + lax.erf(x * INV_SQRT2))


def _gelu_grad(x):
    return 0.5 * (1.0 + lax.erf(x * INV_SQRT2)) + x * (INV_SQRT_2PI * jnp.exp(-0.5 * x * x))


def _head_mean(a, low):
    s_low = jnp.sum(jnp.where(low, a, 0.0), axis=-1, keepdims=True)
    s_high = jnp.sum(jnp.where(low, 0.0, a), axis=-1, keepdims=True)
    return jnp.where(low, s_low, s_high) * (1.0 / HEAD)


def _tril():
    r = lax.broadcasted_iota(jnp.int32, (CHUNK, CHUNK), 0)
    c = lax.broadcasted_iota(jnp.int32, (CHUNK, CHUNK), 1)
    return r >= c


def _sgu_chunk(up, vp, g, wm0, wm1, b0, b1, low):
    ug = _gelu(up)
    vg = _gelu(vp)
    vc = vg - _head_mean(vg, low)
    rstd = lax.rsqrt(_head_mean(vc * vc, low) + LN_EPS)
    vn = vc * rstd
    vb = (vn * g).astype(bf16)
    mixed = jnp.where(low, _mm(wm0, vb) + b0, _mm(wm1, vb) + b1)
    return ug, vn, rstd, vb, mixed


def _sgu_specs(s):
    return [_slab(s, BLK_U), _slab(s, BLK_V), pl.BlockSpec((1, LANES), lambda k: (0, k)),
            pl.BlockSpec((2, CHUNK, CHUNK), lambda k: (k, 0, 0)), pl.BlockSpec((2, CHUNK, 1), lambda k: (k, 0, 0))]


def _sgu_fwd(proj, sgu_g, w_spatial, b_spatial3):
    s = proj.shape[0]

    def body(u_ref, v_ref, g_ref, w_ref, b_ref, o_ref):
        low = lax.broadcasted_iota(jnp.int32, (1, LANES), 1) < HEAD
        mask = _tril()
        wm0 = jnp.where(mask, w_ref[0], 0.0).astype(bf16)
        wm1 = jnp.where(mask, w_ref[1], 0.0).astype(bf16)
        g = g_ref[...]
        b0 = b_ref[0]
        b1 = b_ref[1]

        def chunk(n, carry):
            rows = pl.ds(pl.multiple_of(n * CHUNK, CHUNK), CHUNK)
            ug, _, _, _, mixed = _sgu_chunk(u_ref[rows, :], v_ref[rows, :], g, wm0, wm1, b0, b1, low)
            o_ref[rows, :] = (ug * mixed).astype(bf16)
            return carry

        lax.fori_loop(0, s // CHUNK, chunk, 0)

    return pl.pallas_call(
        body, grid=(SGU_W // LANES,),
        in_specs=_sgu_specs(s),
        out_specs=_slab(s, 0),
        out_shape=jax.ShapeDtypeStruct((s, SGU_W), bf16),
        name="sgu_fwd", compiler_params=_params(("arbitrary",)))(proj, proj, sgu_g, w_spatial, b_spatial3)


def _sgu_bwd(proj, dmix, sgu_g, w_spatial, b_spatial3):
    s = proj.shape[0]

    def body(u_ref, v_ref, g_ref, w_ref, b_ref, dy_ref, du_ref, dv_ref, dg_ref, dw_ref, db_ref):
        low = lax.broadcasted_iota(jnp.int32, (1, LANES), 1) < HEAD
        mask = _tril()
        w0 = jnp.where(mask, w_ref[0], 0.0)
        w1 = jnp.where(mask, w_ref[1], 0.0)
        wm0 = w0.astype(bf16)
        wm1 = w1.astype(bf16)
        wt0 = w0.T.astype(bf16)
        wt1 = w1.T.astype(bf16)
        g = g_ref[...]
        b0 = b_ref[0]
        b1 = b_ref[1]
        dg_ref[...] = jnp.zeros_like(dg_ref)
        dw_ref[...] = jnp.zeros_like(dw_ref)
        db_ref[...] = jnp.zeros_like(db_ref)

        def chunk(n, carry):
            rows = pl.ds(pl.multiple_of(n * CHUNK, CHUNK), CHUNK)
            up = u_ref[rows, :]
            vp = v_ref[rows, :]
            ug, vn, rstd, vb, mixed = _sgu_chunk(up, vp, g, wm0, wm1, b0, b1, low)
            dy = dy_ref[rows, :]
            du_ref[rows, :] = (dy * mixed * _gelu_grad(up)).astype(bf16)
            dmix_c = dy * ug
            db_ref[0] += jnp.sum(jnp.where(low, dmix_c, 0.0), axis=-1, keepdims=True)
            db_ref[1] += jnp.sum(jnp.where(low, 0.0, dmix_c), axis=-1, keepdims=True)
            dmb = dmix_c.astype(bf16)
            zero = jnp.zeros_like(dmb)
            dw_ref[0] += _nt(jnp.where(low, dmb, zero), vb)
            dw_ref[1] += _nt(jnp.where(low, zero, dmb), vb)
            dvnorm = jnp.where(low, _mm(wt0, dmb), _mm(wt1, dmb))
            dg_ref[...] += jnp.sum(dvnorm * vn, axis=0, keepdims=True)
            dvn = dvnorm * g
            dvg = rstd * (dvn - _head_mean(dvn, low) - vn * _head_mean(dvn * vn, low))
            dv_ref[rows, :] = (dvg * _gelu_grad(vp)).astype(bf16)
            return carry

        lax.fori_loop(0, s // CHUNK, chunk, 0)
        dw_ref[0] = jnp.where(mask, dw_ref[0], 0.0)
        dw_ref[1] = jnp.where(mask, dw_ref[1], 0.0)

    out = jax.ShapeDtypeStruct((s, SGU_W), bf16)
    return pl.pallas_call(
        body, grid=(SGU_W // LANES,),
        in_specs=_sgu_specs(s) + [_slab(s, (CONV_W + POOL_W) // LANES)],
        out_specs=[_slab(s, 0), _slab(s, 0), pl.BlockSpec((1, LANES), lambda k: (0, k)),
                   pl.BlockSpec((2, CHUNK, CHUNK), lambda k: (k, 0, 0)), pl.BlockSpec((2, CHUNK, 1), lambda k: (k, 0, 0))],
        out_shape=[out, out, jax.ShapeDtypeStruct((1, SGU_W), f32), jax.ShapeDtypeStruct((6, CHUNK, CHUNK), f32),
                   jax.ShapeDtypeStruct((6, CHUNK, 1), f32)],
        name="sgu_bwd", compiler_params=_params(("arbitrary",)))(proj, proj, sgu_g, w_spatial, b_spatial3, dmix)


def _fwd_mix(x, w, after):
    proj, xb = _proj(x, w["w_in"], after)
    mix = [_conv_fwd(proj, w["w_conv"]), _pool_fwd(proj, w["w_pool"], w["pool_scale"]),
           _sgu_fwd(proj, w["sgu_ln_g"], w["w_spatial"], w["b_spatial"])]
    xhat1, rstd1, hb = _wo_ln1(mix, x, w["w_o"], w["ln1_g"], w["ln1_b"])
    return dict(proj=proj, xb=xb, mix=mix, xhat1=xhat1, rstd1=rstd1, hb=hb)


def _fwd_mlp(sv, w, after):
    gu, xhat2, rstd2, y = _mlp_fwd(sv["xhat1"], w["ln1_g"], w["ln1_b"], w["w_gate_up"], w["w_down"], w["ln2_g"], w["ln2_b"], after)
    sv.update(gu=gu, xhat2=xhat2, rstd2=rstd2)
    return y


def _bwd_mlp(dy, w, sv, after, hook):
    dz2b, actb, dgub, dh, g_ln2_g, g_ln2_b = _mlp_bwd(dy, sv["xhat2"], sv["rstd2"], w["ln2_g"], sv["gu"], w["w_gate_up"],
                                                      w["w_down"], after)
    after = hook(dh)
    grads = dict(w_gate_up=_weight_grad(sv["hb"], dgub, 512, D_FF // 2, after),
                 w_down=_weight_grad(actb, dz2b, D_FF // 2, D_MODEL, after), ln2_g=g_ln2_g, ln2_b=g_ln2_b)
    return dh, grads


def _bwd_mix(dh, w, sv, after, hook):
    dz1, dz1b, dmix, g_ln1_g, g_ln1_b = _ln1_wo_bwd(dh, sv["xhat1"], sv["rstd1"], w["ln1_g"], w["w_o"], after)
    after = hook(dz1)
    dxa, dgb, dgc, g_conv = _conv_bwd(sv["proj"], dmix, w["w_conv"], after)
    dp, g_pool, g_pscale = _pool_bwd(sv["proj"], dmix, w["w_pool"], w["pool_scale"])
    du, dv, g_sgu_g, g_spatial, g_bsp = _sgu_bwd(sv["proj"], dmix, w["sgu_ln_g"], w["w_spatial"], w["b_spatial"])
    dparts = [dxa, dgb, dgc, dp, du, dv]
    dx = _dx(dz1, dparts, w["w_in"])
    grads = dict(
        w_in=_weight_grad_rows(dparts, sv["xb"], 512), w_o=_weight_grad_rows(sv["mix"], dz1b, D_MODEL),
        w_conv=g_conv, w_pool=g_pool, pool_scale=g_pscale, sgu_ln_g=g_sgu_g, w_spatial=g_spatial,
        b_spatial=g_bsp.reshape(6, CHUNK), ln1_g=g_ln1_g, ln1_b=g_ln1_b)
    return dx, grads


def _local_step(x, target, layers):
    saved = []
    for w in layers:
        sv = _fwd_mix(x, w, x)
        x = _fwd_mlp(sv, w, x)
        saved.append(sv)
    dy, sq = _loss_head(x, target)
    grads = [None] * len(layers)
    for l in reversed(range(len(layers))):
        dh, g_mlp = _bwd_mlp(dy, layers[l], saved[l], sq, lambda a: a)
        dy, g_mix = _bwd_mix(dh, layers[l], saved[l], dh, lambda a: a)
        grads[l] = dict(g_mlp, **g_mix)
    return sq, dy, grads


ANY = pl.BlockSpec(memory_space=pl.ANY)


def _place():
    x, y, c = lax.axis_index("x"), lax.axis_index("y"), lax.axis_index("c")
    others = [(1 - x, y), (x, 1 - y), (1 - x, 1 - y)]
    return x, y, c, others


def _chip_index(cx, cy):
    return 2 * cx + cy


def _half(ref_rows, c):
    half = ref_rows // 2
    return pl.ds(pl.multiple_of(c * half, 8), half)


def _remote(src, dst, send_sem, recv_sem, device):
    return pltpu.make_async_remote_copy(src_ref=src, dst_ref=dst, send_sem=send_sem, recv_sem=recv_sem,
                                        device_id=device, device_id_type=MESH)


def _gather_shards(shards):
    n = len(shards)
    base, total = [], 0
    for s in shards:
        base.append(total)
        total += 6 * s.shape[0]

    def body(*refs):
        ins, outs = refs[:n], refs[n:2 * n]
        send, recv = refs[2 * n:]
        x, y, c, others = _place()
        me = _chip_index(x, y)
        sib = (x, y, 1 - c)
        sends = []
        for f in range(n):
            depth, rows = ins[f].shape[0], ins[f].shape[1]
            for l in range(depth):
                for k, (cx, cy) in enumerate(others):
                    sem = base[f] + 6 * l + k
                    cp = _remote(ins[f].at[l, _half(rows, c)], outs[f].at[l, me, _half(rows, c)],
                                 send.at[sem], recv.at[sem], (cx, cy, c))
                    cp.start()
                    sends.append(cp)
        for f in range(n):
            depth, rows = ins[f].shape[0], ins[f].shape[1]
            for l in range(depth):
                for k, (cx, cy) in enumerate(others):
                    sem = base[f] + 6 * l + k
                    landed = outs[f].at[l, _chip_index(cx, cy), _half(rows, c)]
                    _remote(landed, landed, send.at[sem], recv.at[sem], (cx, cy, c)).wait_recv()
                    cp = _remote(landed, landed, send.at[sem + 3], recv.at[sem + 3], sib)
                    cp.start()
                    sends.append(cp)
        for f in range(n):
            depth, rows = ins[f].shape[0], ins[f].shape[1]
            for l in range(depth):
                for k, (cx, cy) in enumerate(others):
                    sem = base[f] + 6 * l + k + 3
                    passed = outs[f].at[l, _chip_index(cx, cy), _half(rows, 1 - c)]
                    _remote(passed, passed, send.at[sem], recv.at[sem], sib).wait_recv()
        for cp in sends:
            cp.wait_send()

    gathered = pl.pallas_call(
        body, in_specs=[ANY] * n, out_specs=[ANY] * n,
        out_shape=[jax.ShapeDtypeStruct((s.shape[0], N_CHIPS) + s.shape[1:], s.dtype) for s in shards],
        scratch_shapes=[pltpu.SemaphoreType.DMA((total,)), pltpu.SemaphoreType.DMA((total,))],
        name="gather_shards")(*shards)
    return [_place_own(g, s) for g, s in zip(gathered, shards)]


def _scalar(value):
    return jnp.reshape(value, (1,)).astype(jnp.int32)


def _place_own(blocks, shard):
    depth, rows, cols = shard.shape

    def body(me_ref, b_ref, s_ref, o_ref):
        o_ref[...] = s_ref[...]

    return pl.pallas_call(
        body,
        grid_spec=pltpu.PrefetchScalarGridSpec(
            num_scalar_prefetch=1, grid=(depth,),
            in_specs=[ANY, pl.BlockSpec((None, rows, cols), lambda l, me: (l, 0, 0))],
            out_specs=pl.BlockSpec((None, None, rows, cols), lambda l, me: (l, me[0], 0, 0))),
        out_shape=jax.ShapeDtypeStruct(blocks.shape, blocks.dtype),
        input_output_aliases={1: 0},
        name="place_own", compiler_params=_params(("arbitrary",)))(
            _scalar(_chip_index(lax.axis_index("x"), lax.axis_index("y"))), blocks, shard)


HBM = pl.BlockSpec(memory_space=pltpu.HBM)
SEM = pl.BlockSpec(memory_space=pltpu.SEMAPHORE)
TOKEN = jax.ShapeDtypeStruct((8, LANES), f32)
SPLIT_COPY = pltpu.CompilerParams(has_side_effects=pltpu.SideEffectType.DATAFLOW_SIDE_EFFECTING)


def _in_hbm(a):
    return pltpu.with_memory_space_constraint(a, pltpu.HBM)


def _full_shape(shard, axis):
    rows, cols = shard.shape
    return (N_CHIPS * rows, cols) if axis == 0 else (rows, N_CHIPS * cols)


def _block(ref, axis, j):
    if axis == 0:
        rows = ref.shape[0] // N_CHIPS
        return ref.at[pl.ds(pl.multiple_of(j * rows, 16), rows), :]
    cols = ref.shape[1] // N_CHIPS
    return ref.at[:, pl.ds(pl.multiple_of(j * cols, LANES), cols)]


def _block_half(ref, axis, j, h):
    if axis == 0:
        rows = ref.shape[0] // N_CHIPS
        return ref.at[pl.ds(pl.multiple_of(j * rows + h * (rows // 2), 16), rows // 2), :]
    half, cols = ref.shape[0] // 2, ref.shape[1] // N_CHIPS
    return ref.at[pl.ds(pl.multiple_of(h * half, 16), half), pl.ds(pl.multiple_of(j * cols, LANES), cols)]


def _gather_start(shards, axes, after):
    lands = [lax.empty(_full_shape(s, ax), s.dtype) for s, ax in zip(shards, axes)]
    return _split_copy_start("gather", _gather_plan(axes), 3 * len(shards), shards, lands, after)


def _gather_wait(state, axes, after):
    return _split_copy_wait("gather", _gather_plan(axes), state, after)


def _split_copy_start(name, plan, count, ins, lands, after):
    arrays = list(ins) + list(lands)
    n_in, n = len(ins), len(arrays)

    def body(*refs):
        send, recv, token = refs[n + 1], refs[n + 2], refs[-1]
        for i, (src, dst, _, peer) in enumerate(plan(refs[:n_in], refs[n_in:n])):
            _remote(src, dst, send.at[i], recv.at[i], peer).start()
        token[...] = jnp.zeros_like(token)

    outs = pl.pallas_call(
        body, name=name + "_start",
        in_specs=[HBM] * n + [ANY],
        out_specs=(SEM, SEM, *[HBM] * n, pl.BlockSpec(memory_space=pltpu.VMEM)),
        out_shape=(pltpu.SemaphoreType.DMA((count,)), pltpu.SemaphoreType.DMA((count,)),
                   *[pltpu.HBM(a.shape, a.dtype) for a in arrays], TOKEN),
        input_output_aliases={i: 2 + i for i in range(n)},
        compiler_params=SPLIT_COPY)(*[_in_hbm(a) for a in arrays], after)
    return (outs[0], outs[1], outs[2:2 + n_in], outs[2 + n_in:2 + n]), outs[-1]


def _split_copy_wait(name, plan, state, after):
    send_sems, recv_sems, ins, lands = state
    arrays = list(ins) + list(lands)
    n_in, n = len(ins), len(arrays)

    def body(*refs):
        send, recv, token = refs[n], refs[n + 1], refs[-1]
        for i, (src, _, landing, peer) in enumerate(plan(refs[:n_in], refs[n_in:n])):
            cp = _remote(src, landing, send.at[i], recv.at[i], peer)
            cp.wait_send()
            cp.wait_recv()
        token[...] = jnp.zeros_like(token)

    outs = pl.pallas_call(
        body, name=name + "_wait",
        in_specs=[HBM] * n + [SEM, SEM, ANY],
        out_specs=(*[HBM] * n, pl.BlockSpec(memory_space=pltpu.VMEM)),
        out_shape=(*[pltpu.HBM(a.shape, a.dtype) for a in arrays], TOKEN),
        input_output_aliases={i: i for i in range(n)},
        compiler_params=SPLIT_COPY)(*arrays, send_sems, recv_sems, after)
    return outs[:n_in], outs[n_in:n], outs[-1]


def _gather_plan(axes):
    def plan(ins, lnd):
        x, y, c, others = _place()
        me = _chip_index(x, y)
        return [(ins[f].at[_half(ins[f].shape[0], c)], _block_half(lnd[f], ax, me, c),
                 _block_half(lnd[f], ax, _chip_index(cx, cy), c), (cx, cy, c))
                for f, ax in enumerate(axes) for cx, cy in others]
    return plan


def _pair_plan(axes):
    def plan(ins, lnd):
        x, y, c, _ = _place()
        return [(_block_half(ins[f], ax, j, 1 - c), lnd[f].at[j], lnd[f].at[j], (x, y, 1 - c))
                for f, ax in enumerate(axes) for j in range(N_CHIPS)]
    return plan


def _scatter_plan(ins, lnd):
    x, y, c, others = _place()
    return [(ins[f].at[_chip_index(cx, cy)], lnd[f].at[k], lnd[f].at[k], (cx, cy, c))
            for f in range(len(ins)) for k, (cx, cy) in enumerate(others)]


def _join_plan(ins, lnd):
    x, y, c, _ = _place()
    return [(lnd[f].at[_half(lnd[f].shape[0], c)], lnd[f].at[_half(lnd[f].shape[0], c)],
             lnd[f].at[_half(lnd[f].shape[0], 1 - c)], (x, y, 1 - c)) for f in range(len(lnd))]


def _gather_finish(lands, shards, axes):
    n = len(lands)

    def body(*refs):
        own, outs = refs[n:2 * n], refs[2 * n:3 * n]
        send, recv = refs[3 * n:]
        x, y, c, others = _place()
        me = _chip_index(x, y)
        sib = (x, y, 1 - c)
        sends = []
        for f in range(n):
            cp = _remote(own[f], _block(outs[f], axes[f], me), send.at[4 * f + 3], recv.at[4 * f + 3], sib)
            cp.start()
            sends.append(cp)
            for k, (cx, cy) in enumerate(others):
                landed = _block_half(outs[f], axes[f], _chip_index(cx, cy), c)
                cp = _remote(landed, landed, send.at[4 * f + k], recv.at[4 * f + k], sib)
                cp.start()
                sends.append(cp)
        for f in range(n):
            mine = _block(outs[f], axes[f], me)
            _remote(mine, mine, send.at[4 * f + 3], recv.at[4 * f + 3], sib).wait_recv()
            for k, (cx, cy) in enumerate(others):
                passed = _block_half(outs[f], axes[f], _chip_index(cx, cy), 1 - c)
                _remote(passed, passed, send.at[4 * f + k], recv.at[4 * f + k], sib).wait_recv()
        for cp in sends:
            cp.wait_send()

    return pl.pallas_call(
        body, in_specs=[ANY] * (2 * n), out_specs=[ANY] * n,
        out_shape=[jax.ShapeDtypeStruct(a.shape, a.dtype) for a in lands],
        input_output_aliases={f: f for f in range(n)},
        scratch_shapes=[pltpu.SemaphoreType.DMA((4 * n,)), pltpu.SemaphoreType.DMA((4 * n,))],
        name="gather_finish")(*lands, *shards)


PIECES = 4


def _pieces(half, cols):
    if cols % (PIECES * LANES) == 0:
        return True, half, cols // PIECES
    assert half % (PIECES * 16) == 0, (half, cols)
    return False, half // PIECES, cols


def _half_blocks(part, axis):
    rows, cols = (part.shape[0] // N_CHIPS, part.shape[1]) if axis == 0 else (part.shape[0], part.shape[1] // N_CHIPS)
    return lax.empty((N_CHIPS, rows // 2, cols), part.dtype)


def _add_pair_layer(part, got, axis):
    _, half, cols = got.shape
    by_cols, br, bc = _pieces(half, cols)

    def body(c_ref, a_ref, b_ref, o_ref):
        o_ref[...] = (a_ref[...].astype(f32) + b_ref[...].astype(f32)).astype(o_ref.dtype)

    def at(j, p):
        return (j, 0, p) if by_cols else (j, p, 0)

    if axis == 0:
        part = part.reshape(N_CHIPS, 2, half, cols)
        mine = pl.BlockSpec((None, None, br, bc), lambda j, p, c: (j, c[0]) + at(j, p)[1:])
    else:
        mine = pl.BlockSpec((br, bc), lambda j, p, c: (c[0] * PIECES + p, j))
        assert not by_cols
    return pl.pallas_call(
        body,
        grid_spec=pltpu.PrefetchScalarGridSpec(
            num_scalar_prefetch=1, grid=(N_CHIPS, PIECES),
            in_specs=[mine, pl.BlockSpec((None, br, bc), lambda j, p, c: at(j, p))],
            out_specs=pl.BlockSpec((None, br, bc), lambda j, p, c: at(j, p))),
        out_shape=jax.ShapeDtypeStruct(got.shape, part.dtype),
        name="add_pair_layer", compiler_params=_params(("arbitrary", "arbitrary")))(_scalar(lax.axis_index("c")), part, got)


def _scatter_start(sums, after):
    lands = [lax.empty((3,) + s.shape[1:], s.dtype) for s in sums]
    return _split_copy_start("scatter", _scatter_plan, 3 * len(sums), sums, lands, after)


def _scatter_wait(state, after):
    return _split_copy_wait("scatter", _scatter_plan, state, after)


ELEMENTWISE_BLOCK_BYTES = 1 << 20


def _row_tile(rows, cols):
    best = None
    for tile in range(8, rows + 1, 8):
        if rows % tile == 0 and tile * cols * 4 <= ELEMENTWISE_BLOCK_BYTES:
            best = tile
    return best or rows


def _add_slots(chip_sums, slots):
    depth, _, half, cols = chip_sums.shape
    by_cols, br, bc = _pieces(half, cols)

    def body(at_ref, own_ref, s_ref, o_ref):
        acc = own_ref[...].astype(f32)
        for k in range(3):
            acc = acc + s_ref[k].astype(f32)
        o_ref[...] = acc

    def piece(p):
        return (0, p) if by_cols else (p, 0)

    at = jnp.concatenate([_scalar(_chip_index(lax.axis_index("x"), lax.axis_index("y"))), _scalar(lax.axis_index("c"))])
    out = pl.pallas_call(
        body,
        grid_spec=pltpu.PrefetchScalarGridSpec(
            num_scalar_prefetch=1, grid=(depth, PIECES),
            in_specs=[pl.BlockSpec((None, None, br, bc), lambda l, p, at: (l, at[0]) + piece(p)),
                      pl.BlockSpec((3, None, br, bc), lambda l, p, at: (0, l) + piece(p))],
            out_specs=pl.BlockSpec((None, None, br, bc), lambda l, p, at: (l, at[1]) + piece(p))),
        out_shape=jax.ShapeDtypeStruct((depth, 2, half, cols), f32),
        name="add_slots", compiler_params=_params(("arbitrary", "arbitrary")))(at, chip_sums, slots)
    return out.reshape(depth, 2 * half, cols)


def _adamw_math(w, grad, m, v):
    nm = ADAM_B1 * m + (1.0 - ADAM_B1) * grad
    nv = ADAM_B2 * v + (1.0 - ADAM_B2) * (grad * grad)
    m_hat = nm / (1.0 - ADAM_B1 ** ADAM_STEP)
    v_hat = nv / (1.0 - ADAM_B2 ** ADAM_STEP)
    return nm, nv, -ADAM_LR * (m_hat / (jnp.sqrt(v_hat) + ADAM_EPS) + ADAM_WD * w)


def _adamw(w, g, m, v):
    shape = w.shape
    flat = [a.reshape(-1, shape[-1]) for a in (w, g, m, v)]
    tile = _row_tile(flat[0].shape[0], shape[-1])

    def body(w_ref, g_ref, m_ref, v_ref, d_ref, nm_ref, nv_ref):
        nm, nv, step = _adamw_math(w_ref[...], g_ref[...], m_ref[...], v_ref[...])
        d_ref[...] = step
        nm_ref[...] = nm
        nv_ref[...] = nv

    spec = _rows(shape[-1], tile)
    out = jax.ShapeDtypeStruct(flat[0].shape, f32)
    res = pl.pallas_call(
        body, grid=(flat[0].shape[0] // tile,),
        in_specs=[spec] * 4, out_specs=[spec] * 3, out_shape=[out] * 3,
        name="adamw", compiler_params=_params(("arbitrary",)))(*flat)
    return [r.reshape(shape) for r in res]


def _adamw_layer(l, w, m, v, g, outs, after):
    depth, rows, cols = w.shape
    tile = _row_tile(rows, cols)

    def body(w_ref, m_ref, v_ref, g_ref, *refs):
        go_ref, d_ref, nm_ref, nv_ref = refs[5:]
        grad = g_ref[...]
        nm, nv, step = _adamw_math(w_ref[...], grad, m_ref[...], v_ref[...])
        go_ref[...] = grad
        d_ref[...] = step
        nm_ref[...] = nm
        nv_ref[...] = nv

    layer = pl.BlockSpec((None, tile, cols), lambda i: (l, i, 0))
    return pl.pallas_call(
        body, grid=(rows // tile,),
        in_specs=[layer] * 3 + [_rows(cols, tile)] + [ANY] * 5, out_specs=[layer] * 4,
        out_shape=[jax.ShapeDtypeStruct(w.shape, f32)] * 4,
        input_output_aliases={4 + k: k for k in range(4)},
        name="adamw_layer", compiler_params=_params(("arbitrary",)))(w, m, v, g, *outs, after)


SMALL = ("w_conv", "w_pool", "pool_scale", "sgu_ln_g", "w_spatial", "b_spatial", "ln1_g", "ln1_b", "ln2_g", "ln2_b")
WEIGHTS = ("w_in", "w_conv", "w_pool", "pool_scale", "sgu_ln_g", "w_spatial", "b_spatial", "w_o", "ln1_g", "ln1_b",
           "w_gate_up", "w_down", "ln2_g", "ln2_b")
BIG = ("w_in", "w_o", "w_gate_up", "w_down")
GROUPS = (("w_in", "w_o"), ("w_gate_up", "w_down"))
GROUP_AXES = ((0, 0), (1, 0))
SCATTER_HOOKS = 2
SMALL_LAYER_ROWS = 1024


def _pack_layer(arrays):
    flat = jnp.concatenate([a.reshape(-1) for a in arrays])
    return jnp.pad(flat, (0, SMALL_LAYER_ROWS * LANES - flat.shape[0])).reshape(SMALL_LAYER_ROWS, LANES)


def _unpack_layers(flat, shapes):
    out, at = {}, 0
    for name, shape in shapes.items():
        size = 1
        for d in shape:
            size *= d
        out[name] = flat[:, at:at + size].reshape((flat.shape[0],) + tuple(shape))
        at += size
    return out


def kernel(x, w_in, w_conv, w_pool, pool_scale, sgu_ln_g, w_spatial, b_spatial, w_o, ln1_g, ln1_b, w_gate_up, w_down, ln2_g, ln2_b, loss_target, m_w_in, m_w_conv, m_w_pool, m_pool_scale, m_sgu_ln_g, m_w_spatial, m_b_spatial, m_w_o, m_ln1_g, m_ln1_b, m_w_gate_up, m_w_down, m_ln2_g, m_ln2_b, v_w_in, v_w_conv, v_w_pool, v_pool_scale, v_sgu_ln_g, v_w_spatial, v_b_spatial, v_w_o, v_ln1_g, v_ln1_b, v_w_gate_up, v_w_down, v_ln2_g, v_ln2_b):
    weights = dict(w_in=w_in, w_conv=w_conv, w_pool=w_pool, pool_scale=pool_scale, sgu_ln_g=sgu_ln_g, w_spatial=w_spatial,
                   b_spatial=b_spatial, w_o=w_o, ln1_g=ln1_g, ln1_b=ln1_b, w_gate_up=w_gate_up, w_down=w_down, ln2_g=ln2_g, ln2_b=ln2_b)
    m_in = dict(w_in=m_w_in, w_conv=m_w_conv, w_pool=m_w_pool, pool_scale=m_pool_scale, sgu_ln_g=m_sgu_ln_g, w_spatial=m_w_spatial,
                b_spatial=m_b_spatial, w_o=m_w_o, ln1_g=m_ln1_g, ln1_b=m_ln1_b, w_gate_up=m_w_gate_up, w_down=m_w_down,
                ln2_g=m_ln2_g, ln2_b=m_ln2_b)
    v_in = dict(w_in=v_w_in, w_conv=v_w_conv, w_pool=v_w_pool, pool_scale=v_pool_scale, sgu_ln_g=v_sgu_ln_g, w_spatial=v_w_spatial,
                b_spatial=v_b_spatial, w_o=v_w_o, ln1_g=v_ln1_g, ln1_b=v_ln1_b, w_gate_up=v_w_gate_up, w_down=v_w_down,
                ln2_g=v_ln2_g, ln2_b=v_ln2_b)
    depth = w_in.shape[0]
    conv_cols = w_conv.shape[2]
    chip = _chip_index(lax.axis_index("x"), lax.axis_index("y"))

    conv_flat = jnp.pad(w_conv.reshape(-1), (0, 16 * LANES - w_conv.size)).reshape(1, 16, LANES)
    conv_full = _gather_shards([conv_flat])[0].reshape(N_CHIPS, 16 * LANES)[:, :w_conv.size].reshape(N_CHIPS, depth, 3, conv_cols)
    conv_full = conv_full.transpose(1, 2, 0, 3).reshape(depth, 3, N_CHIPS * conv_cols)

    big_w = dict(w_in=jnp.swapaxes(w_in, 1, 2), w_o=w_o, w_gate_up=w_gate_up, w_down=w_down)
    big_m = dict(w_in=jnp.swapaxes(m_w_in, 1, 2), w_o=m_w_o, w_gate_up=m_w_gate_up, w_down=m_w_down)
    big_v = dict(w_in=jnp.swapaxes(v_w_in, 1, 2), w_o=v_w_o, w_gate_up=v_w_gate_up, w_down=v_w_down)

    def send(l, g, after):
        return _gather_start([big_w[n][l].astype(bf16) for n in GROUPS[g]], GROUP_AXES[g], after)

    def receive(g, flight, after):
        shards, lands, token = _gather_wait(flight, GROUP_AXES[g], after)
        return shards, lands, token

    act = x[0]
    layers, saved = [], []
    flight, token = send(0, 0, conv_full)
    for l in range(depth):
        w = dict(w_conv=conv_full[l], w_pool=w_pool[l], pool_scale=pool_scale[l][None], sgu_ln_g=sgu_ln_g[l][None],
                 w_spatial=w_spatial[l], b_spatial=b_spatial[l][:, :, None], ln1_g=ln1_g[l][None], ln1_b=ln1_b[l][None],
                 ln2_g=ln2_g[l][None], ln2_b=ln2_b[l][None])
        shards, lands, token = receive(0, flight, act)
        flight, token = send(l, 1, token)
        w.update(zip(GROUPS[0], _gather_finish(lands, shards, GROUP_AXES[0])))
        sv = _fwd_mix(act, w, token)
        shards, lands, token = receive(1, flight, sv["xhat1"])
        if l + 1 < depth:
            flight, token = send(l + 1, 0, token)
        w.update(zip(GROUPS[1], _gather_finish(lands, shards, GROUP_AXES[1])))
        act = _fwd_mlp(sv, w, token)
        layers.append(w)
        saved.append(sv)

    big_outs = {n: [lax.empty(big_w[n].shape, f32) for _ in range(4)] for n in BIG}
    small_sums = [None] * depth
    pending, updates = [], []
    latest = dict(token=None)

    def begin(l, g, parts):
        axes = GROUP_AXES[g] + (0,) * (len(parts) - len(GROUPS[g]))
        lands = [_half_blocks(p, ax) for p, ax in zip(parts, axes)]
        flight, latest["token"] = _split_copy_start("pair", _pair_plan(axes), N_CHIPS * len(parts), parts, lands, latest["token"])
        pending.append(dict(l=l, g=g, axes=axes, step="pair", age=0, flight=flight))

    def advance(st, recent):
        if st["step"] == "pair":
            parts, got, _ = _split_copy_wait("pair", _pair_plan(st["axes"]), st["flight"], recent)
            sums = [_add_pair_layer(p, q, ax) for p, q, ax in zip(parts, got, st["axes"])]
            st["flight"], latest["token"] = _scatter_start(sums, latest["token"])
            st["step"] = "scatter"
        elif st["step"] == "scatter":
            sums, slots, _ = _scatter_wait(st["flight"], recent)
            filled = [_add_slots(cs[None], s[:, None])[0] for cs, s in zip(sums, slots)]
            st["flight"], latest["token"] = _split_copy_start("join", _join_plan, len(filled), [], filled, latest["token"])
            st["step"] = "join"
        else:
            _, summed, _ = _split_copy_wait("join", _join_plan, st["flight"], recent)
            updates.extend((st["l"], n, total) for n, total in zip(GROUPS[st["g"]], summed))
            if st["g"] == 0:
                small_sums[st["l"]] = summed[-1]
            st["step"] = "done"
        st["age"] = 0

    def hook(recent):
        for st in reversed(list(pending)):
            st["age"] += 1
            if st["age"] >= SCATTER_HOOKS or st["step"] != "scatter":
                advance(st, recent)
                if st["step"] == "done":
                    pending.remove(st)
        return latest["token"]

    def update(count, recent):
        for l, n, total in updates[:count]:
            big_outs[n] = _adamw_layer(l, big_w[n], big_m[n], big_v[n], total, big_outs[n], latest["token"])
            recent = big_outs[n][1]
        del updates[:count]
        return recent

    grad_x, sq = _loss_head(act, loss_target[0])
    latest["token"] = sq
    grads = [None] * depth
    for l in reversed(range(depth)):
        dh, g_mlp = _bwd_mlp(grad_x, layers[l], saved[l], latest["token"], hook)
        hook(g_mlp["w_down"])
        begin(l, 1, [g_mlp[n] for n in GROUPS[1]])
        grad_x, g_mix = _bwd_mix(dh, layers[l], saved[l], latest["token"], hook)
        grads[l] = dict(g_mlp, **g_mix)
        hook(g_mix["w_o"])
        begin(l, 0, [g_mix[n] for n in GROUPS[0]] + [_pack_layer([grads[l][n] for n in SMALL])])
    recent = g_mix["w_o"]
    while pending:
        recent = update(-(-len(updates) // 2), recent)
        hook(recent)
    update(len(updates), recent)
    loss = lax.psum(0.5 / D_MODEL * jnp.sum(sq), ("x", "y", "c"))

    small_sum = _gather_shards([jnp.stack(small_sums)])[0].reshape(depth, SMALL_LAYER_ROWS * LANES)
    grad = {n: [jnp.swapaxes(o, 1, 2) for o in big_outs[n]] if n == "w_in" else big_outs[n] for n in BIG}
    delta = {n: o[1] for n, o in grad.items()}
    new_m = {n: o[2] for n, o in grad.items()}
    new_v = {n: o[3] for n, o in grad.items()}
    grad = {n: o[0] for n, o in grad.items()}
    grad.update(_unpack_layers(small_sum, {n: (3, N_CHIPS * conv_cols) if n == "w_conv" else weights[n].shape[1:] for n in SMALL}))
    grad["w_conv"] = lax.dynamic_slice_in_dim(grad["w_conv"], chip * conv_cols, conv_cols, axis=2)

    delta["w_conv"], new_m["w_conv"], new_v["w_conv"] = _adamw(w_conv, grad["w_conv"], m_w_conv, v_w_conv)
    rest = [n for n in SMALL if n != "w_conv"]
    rest_shapes = {n: weights[n].shape[1:] for n in rest}
    packed = [jnp.concatenate([_pack_layer([src[n][l] for n in rest]) for l in range(depth)]) for src in (weights, grad, m_in, v_in)]
    for dst, res in zip((delta, new_m, new_v), _adamw(*packed)):
        dst.update(_unpack_layers(res.reshape(depth, SMALL_LAYER_ROWS * LANES), rest_shapes))

    return (loss, grad_x[None], *[grad[n] for n in WEIGHTS], *[delta[n] for n in WEIGHTS],
            *[new_m[n] for n in WEIGHTS], *[new_v[n] for n in WEIGHTS])
```

```python
import functools

import jax
import jax.numpy as jnp
from jax import lax
from jax.experimental import pallas as pl
from jax.experimental.pallas import tpu as pltpu

f32 = jnp.float32
bf16 = jnp.bfloat16

D_MODEL = 1024
DEPTH = 4
CONV_W = 384
POOL_W = 256
SGU_W = 384
IN_W = 3 * CONV_W + POOL_W + 2 * SGU_W
D_FF = 2816
CHUNK = 128
HEAD = 64
POOL_WINDOWS = (2, 4, 8, 16)
ALPHA = float((2 * DEPTH) ** 0.25)
LN_EPS = 1e-5
ADAM_LR = 0.001
ADAM_B1 = 0.9
ADAM_B2 = 0.999
ADAM_EPS = 1e-08
ADAM_WD = 0.01
ADAM_STEP = 10

LANES = 128
TOKEN_TILE = 256
N_CHIPS = 4
VMEM_LIMIT = 56 * 1024 * 1024

BLK_XA, BLK_GB, BLK_GC, BLK_P, BLK_U, BLK_V = 0, 3, 6, 9, 11, 14

MESH = pl.DeviceIdType.MESH


def _params(sem=None):
    return pltpu.CompilerParams(dimension_semantics=sem, vmem_limit_bytes=VMEM_LIMIT)


def _rows(width, tile=TOKEN_TILE):
    return pl.BlockSpec((tile, width), lambda i: (i, 0))


def _resident(shape):
    zeros = (0,) * len(shape)
    return pl.BlockSpec(shape, lambda *_: zeros, pipeline_mode=pl.Buffered(1))


def _nt(a, b):
    return lax.dot_general(a, b, (((1,), (1,)), ((), ())), preferred_element_type=f32)


def _tn(a, b):
    return lax.dot_general(a, b, (((0,), (0,)), ((), ())), preferred_element_type=f32)


def _mm(a, b):
    return jnp.dot(a, b, preferred_element_type=f32)


def _norm_fwd(z):
    mu = jnp.mean(z, axis=-1, keepdims=True)
    zc = z - mu
    var = jnp.mean(zc * zc, axis=-1, keepdims=True)
    rstd = lax.rsqrt(var + LN_EPS)
    return zc * rstd, rstd


def _norm_bwd(dxhat, xhat, rstd):
    m1 = jnp.mean(dxhat, axis=-1, keepdims=True)
    m2 = jnp.mean(dxhat * xhat, axis=-1, keepdims=True)
    return rstd * (dxhat - m1 - xhat * m2)


def _proj(x, w_in_b, after):
    s = x.shape[0]

    def body(x_ref, w_ref, after_ref, p_ref, xb_ref):
        xb = x_ref[...].astype(bf16)
        xb_ref[...] = xb
        p_ref[...] = _nt(xb, w_ref[...])

    return pl.pallas_call(
        body, grid=(s // TOKEN_TILE,),
        in_specs=[_rows(D_MODEL), _resident((IN_W, D_MODEL)), pl.BlockSpec(memory_space=pl.ANY)],
        out_specs=[_rows(IN_W), _rows(D_MODEL)],
        out_shape=[jax.ShapeDtypeStruct((s, IN_W), f32), jax.ShapeDtypeStruct((s, D_MODEL), bf16)],
        name="proj", compiler_params=_params(("arbitrary",)))(x, w_in_b, after)


def _row_ranges(parts):
    out, at = [], 0
    for p in parts:
        out.append((at, at + p.shape[1]))
        at += p.shape[1]
    return out


def _wo_ln1(mix, x, w_o_b, g, b):
    s = x.shape[0]
    n = len(mix)
    ranges = _row_ranges(mix)

    def body(*refs):
        m_refs = refs[:n]
        x_ref, w_ref, g_ref, b_ref, xhat_ref, rstd_ref, hb_ref = refs[n:]
        z = ALPHA * x_ref[...]
        for m_ref, (lo, hi) in zip(m_refs, ranges):
            z = z + _mm(m_ref[...], w_ref[lo:hi, :])
        xhat, rstd = _norm_fwd(z)
        xhat_ref[...] = xhat
        rstd_ref[...] = rstd
        hb_ref[...] = (xhat * g_ref[...] + b_ref[...]).astype(bf16)

    return pl.pallas_call(
        body, grid=(s // TOKEN_TILE,),
        in_specs=[_rows(m.shape[1]) for m in mix] + [_rows(D_MODEL), _resident((D_MODEL, D_MODEL)), _resident((1, D_MODEL)),
                                                     _resident((1, D_MODEL))],
        out_specs=[_rows(D_MODEL), _rows(1), _rows(D_MODEL)],
        out_shape=[jax.ShapeDtypeStruct((s, D_MODEL), f32), jax.ShapeDtypeStruct((s, 1), f32),
                   jax.ShapeDtypeStruct((s, D_MODEL), bf16)],
        name="wo_ln1", compiler_params=_params(("arbitrary",)))(*mix, x, w_o_b, g, b)


def _mlp_fwd(xhat1, g1, b1, w_gu_b, w_down_b, g2, b2, after):
    s = xhat1.shape[0]

    def body(xh_ref, g1_ref, b1_ref, wgu_ref, wd_ref, g2_ref, b2_ref, after_ref, gu_ref, xhat2_ref, rstd2_ref, y_ref):
        h = xh_ref[...] * g1_ref[...] + b1_ref[...]
        gu = _mm(h.astype(bf16), wgu_ref[...])
        gu_ref[...] = gu
        gate = gu[:, :D_FF]
        act = gate * jax.nn.sigmoid(gate) * gu[:, D_FF:]
        z = ALPHA * h + _mm(act.astype(bf16), wd_ref[...])
        xhat2, rstd2 = _norm_fwd(z)
        xhat2_ref[...] = xhat2
        rstd2_ref[...] = rstd2
        y_ref[...] = xhat2 * g2_ref[...] + b2_ref[...]

    vec = _resident((1, D_MODEL))
    return pl.pallas_call(
        body, grid=(s // TOKEN_TILE,),
        in_specs=[_rows(D_MODEL), vec, vec, _resident((D_MODEL, 2 * D_FF)), _resident((D_FF, D_MODEL)), vec, vec,
                  pl.BlockSpec(memory_space=pl.ANY)],
        out_specs=[_rows(2 * D_FF), _rows(D_MODEL), _rows(1), _rows(D_MODEL)],
        out_shape=[jax.ShapeDtypeStruct((s, 2 * D_FF), f32), jax.ShapeDtypeStruct((s, D_MODEL), f32),
                   jax.ShapeDtypeStruct((s, 1), f32), jax.ShapeDtypeStruct((s, D_MODEL), f32)],
        name="mlp_fwd", compiler_params=_params(("arbitrary",)))(xhat1, g1, b1, w_gu_b, w_down_b, g2, b2, after)


def _loss_head(y, target):
    s = y.shape[0]

    def body(y_ref, t_ref, dy_ref, sq_ref):
        @pl.when(pl.program_id(0) == 0)
        def _():
            sq_ref[...] = jnp.zeros_like(sq_ref)

        e = y_ref[...] - t_ref[...]
        dy_ref[...] = e * (1.0 / D_MODEL)
        sq_ref[...] += jnp.sum(e * e, axis=0, keepdims=True)

    return pl.pallas_call(
        body, grid=(s // TOKEN_TILE,),
        in_specs=[_rows(D_MODEL), _rows(D_MODEL)],
        out_specs=[_rows(D_MODEL), pl.BlockSpec((1, D_MODEL), lambda i: (0, 0))],
        out_shape=[jax.ShapeDtypeStruct((s, D_MODEL), f32), jax.ShapeDtypeStruct((1, D_MODEL), f32)],
        name="loss_head", compiler_params=_params(("arbitrary",)))(y, target)


def _mlp_bwd(dy, xhat2, rstd2, g2, gu, w_gu_b, w_down_b, after):
    s = dy.shape[0]

    def body(dy_ref, xh_ref, rs_ref, g2_ref, gu_ref, wgu_ref, wd_ref, after_ref, dz_ref, act_ref, dgu_ref, dh_ref, gg_ref, gb_ref):
        @pl.when(pl.program_id(0) == 0)
        def _():
            gg_ref[...] = jnp.zeros_like(gg_ref)
            gb_ref[...] = jnp.zeros_like(gb_ref)

        dy_t = dy_ref[...]
        xhat = xh_ref[...]
        gg_ref[...] += jnp.sum(dy_t * xhat, axis=0, keepdims=True)
        gb_ref[...] += jnp.sum(dy_t, axis=0, keepdims=True)
        dz = _norm_bwd(dy_t * g2_ref[...], xhat, rs_ref[...])
        dzb = dz.astype(bf16)
        dz_ref[...] = dzb
        dact = _nt(dzb, wd_ref[...])
        gate = gu_ref[:, :D_FF]
        up = gu_ref[:, D_FF:]
        sg = jax.nn.sigmoid(gate)
        silu = gate * sg
        act_ref[...] = (silu * up).astype(bf16)
        dgu_ref[:, :D_FF] = (dact * up * (sg * (1.0 + gate * (1.0 - sg)))).astype(bf16)
        dgu_ref[:, D_FF:] = (dact * silu).astype(bf16)
        dh_ref[...] = ALPHA * dz + _nt(dgu_ref[...], wgu_ref[...])

    vec_out = pl.BlockSpec((1, D_MODEL), lambda i: (0, 0))
    return pl.pallas_call(
        body, grid=(s // TOKEN_TILE,),
        in_specs=[_rows(D_MODEL), _rows(D_MODEL), _rows(1), _resident((1, D_MODEL)), _rows(2 * D_FF),
                  _resident((D_MODEL, 2 * D_FF)), _resident((D_FF, D_MODEL)), pl.BlockSpec(memory_space=pl.ANY)],
        out_specs=[_rows(D_MODEL), _rows(D_FF), _rows(2 * D_FF), _rows(D_MODEL), vec_out, vec_out],
        out_shape=[jax.ShapeDtypeStruct((s, D_MODEL), bf16), jax.ShapeDtypeStruct((s, D_FF), bf16),
                   jax.ShapeDtypeStruct((s, 2 * D_FF), bf16), jax.ShapeDtypeStruct((s, D_MODEL), f32),
                   jax.ShapeDtypeStruct((1, D_MODEL), f32), jax.ShapeDtypeStruct((1, D_MODEL), f32)],
        name="mlp_bwd", compiler_params=_params(("arbitrary",)))(dy, xhat2, rstd2, g2, gu, w_gu_b, w_down_b, after)


def _ln1_wo_bwd(dh, xhat1, rstd1, g1, w_o_b, after):
    s = dh.shape[0]

    def body(dh_ref, xh_ref, rs_ref, g1_ref, w_ref, after_ref, dz_ref, dzb_ref, dm_ref, gg_ref, gb_ref):
        @pl.when(pl.program_id(0) == 0)
        def _():
            gg_ref[...] = jnp.zeros_like(gg_ref)
            gb_ref[...] = jnp.zeros_like(gb_ref)

        dh_t = dh_ref[...]
        xhat = xh_ref[...]
        gg_ref[...] += jnp.sum(dh_t * xhat, axis=0, keepdims=True)
        gb_ref[...] += jnp.sum(dh_t, axis=0, keepdims=True)
        dz = _norm_bwd(dh_t * g1_ref[...], xhat, rs_ref[...])
        dz_ref[...] = dz
        dzb = dz.astype(bf16)
        dzb_ref[...] = dzb
        dm_ref[...] = _nt(dzb, w_ref[...])

    vec_out = pl.BlockSpec((1, D_MODEL), lambda i: (0, 0))
    return pl.pallas_call(
        body, grid=(s // TOKEN_TILE,),
        in_specs=[_rows(D_MODEL), _rows(D_MODEL), _rows(1), _resident((1, D_MODEL)), _resident((D_MODEL, D_MODEL)),
                  pl.BlockSpec(memory_space=pl.ANY)],
        out_specs=[_rows(D_MODEL), _rows(D_MODEL), _rows(D_MODEL), vec_out, vec_out],
        out_shape=[jax.ShapeDtypeStruct((s, D_MODEL), f32), jax.ShapeDtypeStruct((s, D_MODEL), bf16),
                   jax.ShapeDtypeStruct((s, D_MODEL), f32), jax.ShapeDtypeStruct((1, D_MODEL), f32),
                   jax.ShapeDtypeStruct((1, D_MODEL), f32)],
        name="ln1_wo_bwd", compiler_params=_params(("arbitrary",)))(dh, xhat1, rstd1, g1, w_o_b, after)


def _dx(dz1, dparts, w_in_t):
    s = dz1.shape[0]
    n = len(dparts)
    ranges = _row_ranges(dparts)

    def body(*refs):
        d_refs = refs[:n]
        dz_ref, w_ref, dx_ref = refs[n:]
        acc = ALPHA * dz_ref[...]
        for d_ref, (lo, hi) in zip(d_refs, ranges):
            acc = acc + _mm(d_ref[...], w_ref[lo:hi, :])
        dx_ref[...] = acc

    return pl.pallas_call(
        body, grid=(s // TOKEN_TILE,),
        in_specs=[_rows(d.shape[1]) for d in dparts] + [_rows(D_MODEL), _resident((IN_W, D_MODEL))],
        out_specs=_rows(D_MODEL),
        out_shape=jax.ShapeDtypeStruct((s, D_MODEL), f32),
        name="dx", compiler_params=_params(("arbitrary",)))(*dparts, dz1, w_in_t)


def _weight_grad_rows(parts, b, bn):
    s, n_cols = b.shape
    n = len(parts)
    ranges = _row_ranges(parts)
    m = ranges[-1][1]

    def body(*refs):
        p_refs = refs[:n]
        b_ref, o_ref = refs[n:]
        for p_ref, (lo, hi) in zip(p_refs, ranges):
            o_ref[lo:hi, :] = _tn(p_ref[...], b_ref[...]).astype(bf16)

    return pl.pallas_call(
        body, grid=(n_cols // bn,),
        in_specs=[_resident(p.shape) for p in parts] + [pl.BlockSpec((s, bn), lambda j: (0, j))],
        out_specs=pl.BlockSpec((m, bn), lambda j: (0, j)),
        out_shape=jax.ShapeDtypeStruct((m, n_cols), bf16),
        name="weight_grad_rows", compiler_params=_params(("arbitrary",)))(*parts, b)


def _weight_grad(a, b, bm, bn, after):
    s, m = a.shape
    n = b.shape[1]

    def body(a_ref, b_ref, after_ref, o_ref):
        o_ref[...] = _tn(a_ref[...], b_ref[...]).astype(bf16)

    return pl.pallas_call(
        body, grid=(m // bm, n // bn),
        in_specs=[pl.BlockSpec((s, bm), lambda i, j: (0, i)), pl.BlockSpec((s, bn), lambda i, j: (0, j)),
                  pl.BlockSpec(memory_space=pl.ANY)],
        out_specs=pl.BlockSpec((bm, bn), lambda i, j: (i, j)),
        out_shape=jax.ShapeDtypeStruct((m, n), bf16),
        name="weight_grad", compiler_params=_params(("arbitrary", "arbitrary")))(a, b, after)


def _shift_down(a, k):
    row = lax.broadcasted_iota(jnp.int32, a.shape, 0)
    return jnp.where(row >= k, pltpu.roll(a, k, 0), 0.0)


def _shift_up(a, k):
    n = a.shape[0]
    row = lax.broadcasted_iota(jnp.int32, a.shape, 0)
    return jnp.where(row < n - k, pltpu.roll(a, n - k, 0), 0.0)


def _slab(s, block):
    return pl.BlockSpec((s, LANES), lambda k: (0, block + k))


def _conv_y(z, w):
    return w[0:1, :] * _shift_down(z, 2) + w[1:2, :] * _shift_down(z, 1) + w[2:3, :] * z


def _conv_fwd(proj, w_conv):
    s = proj.shape[0]

    def body(xa_ref, gb_ref, gc_ref, w_ref, o_ref):
        z = gc_ref[...] * xa_ref[...]
        o_ref[...] = (gb_ref[...] * _conv_y(z, w_ref[...])).astype(bf16)

    return pl.pallas_call(
        body, grid=(CONV_W // LANES,),
        in_specs=[_slab(s, BLK_XA), _slab(s, BLK_GB), _slab(s, BLK_GC), pl.BlockSpec((3, LANES), lambda k: (0, k))],
        out_specs=_slab(s, 0),
        out_shape=jax.ShapeDtypeStruct((s, CONV_W), bf16),
        name="conv_fwd", compiler_params=_params(("arbitrary",)))(proj, proj, proj, w_conv)


def _conv_bwd(proj, dmix, w_conv, after):
    s = proj.shape[0]

    def body(xa_ref, gb_ref, gc_ref, dy_ref, w_ref, after_ref, dxa_ref, dgb_ref, dgc_ref, dw_ref):
        xa = xa_ref[...]
        gc = gc_ref[...]
        w = w_ref[...]
        z = gc * xa
        dya = dy_ref[...]
        dgb_ref[...] = (dya * _conv_y(z, w)).astype(bf16)
        dy = dya * gb_ref[...]
        dz = w[2:3, :] * dy + w[1:2, :] * _shift_up(dy, 1) + w[0:1, :] * _shift_up(dy, 2)
        dxa_ref[...] = (dz * gc).astype(bf16)
        dgc_ref[...] = (dz * xa).astype(bf16)
        dw_ref[0:1, :] = jnp.sum(dy * _shift_down(z, 2), axis=0, keepdims=True)
        dw_ref[1:2, :] = jnp.sum(dy * _shift_down(z, 1), axis=0, keepdims=True)
        dw_ref[2:3, :] = jnp.sum(dy * z, axis=0, keepdims=True)

    out = jax.ShapeDtypeStruct((s, CONV_W), bf16)
    return pl.pallas_call(
        body, grid=(CONV_W // LANES,),
        in_specs=[_slab(s, BLK_XA), _slab(s, BLK_GB), _slab(s, BLK_GC), _slab(s, 0), pl.BlockSpec((3, LANES), lambda k: (0, k)),
                  pl.BlockSpec(memory_space=pl.ANY)],
        out_specs=[_slab(s, 0), _slab(s, 0), _slab(s, 0), pl.BlockSpec((3, LANES), lambda k: (0, k))],
        out_shape=[out, out, out, jax.ShapeDtypeStruct((3, CONV_W), f32)],
        name="conv_bwd", compiler_params=_params(("arbitrary",)))(proj, proj, proj, dmix, w_conv, after)


def _pool_window(k):
    lane = lax.broadcasted_iota(jnp.int32, (1, LANES), 1)
    low = lane < HEAD
    first = k == 0
    wlen = jnp.where(low, jnp.where(first, POOL_WINDOWS[0], POOL_WINDOWS[2]), jnp.where(first, POOL_WINDOWS[1], POOL_WINDOWS[3]))
    return wlen, low, first


def _pool_diff(p, k):
    wlen, low, first = _pool_window(k)
    s2 = p + _shift_down(p, 1)
    s4 = s2 + _shift_down(s2, 2)
    s8 = s4 + _shift_down(s4, 4)
    s16 = s8 + _shift_down(s8, 8)
    win = jnp.where(low, jnp.where(first, s2, s8), jnp.where(first, s4, s16))
    row = lax.broadcasted_iota(jnp.int32, p.shape, 0)
    count = jnp.minimum(row + 1, wlen).astype(f32)
    return win / count - p, count


def _pool_weight(w_ref):
    zero = jnp.zeros((HEAD, HEAD), f32)
    top = jnp.concatenate([w_ref[0], zero], axis=1)
    bottom = jnp.concatenate([zero, w_ref[1]], axis=1)
    return jnp.concatenate([top, bottom], axis=0).astype(bf16)


def _pool_fwd(proj, w_pool, pool_scale):
    s = proj.shape[0]

    def body(p_ref, w_ref, sc_ref, o_ref):
        d, _ = _pool_diff(p_ref[...], pl.program_id(0))
        o_ref[...] = (_mm(d.astype(bf16), _pool_weight(w_ref)) * sc_ref[...]).astype(bf16)

    return pl.pallas_call(
        body, grid=(POOL_W // LANES,),
        in_specs=[_slab(s, BLK_P), pl.BlockSpec((2, HEAD, HEAD), lambda k: (k, 0, 0)), pl.BlockSpec((1, LANES), lambda k: (0, k))],
        out_specs=_slab(s, 0),
        out_shape=jax.ShapeDtypeStruct((s, POOL_W), bf16),
        name="pool_fwd", compiler_params=_params(("arbitrary",)))(proj, w_pool, pool_scale)


def _pool_bwd(proj, dmix, w_pool, pool_scale):
    s = proj.shape[0]

    def body(p_ref, dy_ref, w_ref, sc_ref, dp_ref, dw_ref, dsc_ref):
        k = pl.program_id(0)
        d, count = _pool_diff(p_ref[...], k)
        wbd = _pool_weight(w_ref)
        db = d.astype(bf16)
        dyb = dy_ref[...]
        dsc_ref[...] = jnp.sum(dyb * _mm(db, wbd), axis=0, keepdims=True)
        dpre = (dyb * sc_ref[...]).astype(bf16)
        dwbd = _tn(db, dpre)
        dw_ref[0] = dwbd[:HEAD, :HEAD]
        dw_ref[1] = dwbd[HEAD:, HEAD:]
        dd = _nt(dpre, wbd)
        e = dd / count
        wlen, low, first = _pool_window(k)
        a2 = e + _shift_up(e, 1)
        a4 = a2 + _shift_up(a2, 2)
        a8 = a4 + _shift_up(a4, 4)
        a16 = a8 + _shift_up(a8, 8)
        back = jnp.where(low, jnp.where(first, a2, a8), jnp.where(first, a4, a16))
        dp_ref[...] = (back - dd).astype(bf16)

    return pl.pallas_call(
        body, grid=(POOL_W // LANES,),
        in_specs=[_slab(s, BLK_P), _slab(s, CONV_W // LANES), pl.BlockSpec((2, HEAD, HEAD), lambda k: (k, 0, 0)),
                  pl.BlockSpec((1, LANES), lambda k: (0, k))],
        out_specs=[_slab(s, 0), pl.BlockSpec((2, HEAD, HEAD), lambda k: (k, 0, 0)), pl.BlockSpec((1, LANES), lambda k: (0, k))],
        out_shape=[jax.ShapeDtypeStruct((s, POOL_W), bf16), jax.ShapeDtypeStruct((4, HEAD, HEAD), f32),
                   jax.ShapeDtypeStruct((1, POOL_W), f32)],
        name="pool_bwd", compiler_params=_params(("arbitrary",)))(proj, dmix, w_pool, pool_scale)


INV_SQRT2 = 0.7071067811865476
INV_SQRT_2PI = 0.3989422804014327


def _gelu(x):
    return 0.5 * x * (1.0 + lax.erf(x * INV_SQRT2))


def _gelu_grad(x):
    return 0.5 * (1.0 + lax.erf(x * INV_SQRT2)) + x * (INV_SQRT_2PI * jnp.exp(-0.5 * x * x))


def _head_mean(a, low):
    s_low = jnp.sum(jnp.where(low, a, 0.0), axis=-1, keepdims=True)
    s_high = jnp.sum(jnp.where(low, 0.0, a), axis=-1, keepdims=True)
    return jnp.where(low, s_low, s_high) * (1.0 / HEAD)


def _tril():
    r = lax.broadcasted_iota(jnp.int32, (CHUNK, CHUNK), 0)
    c = lax.broadcasted_iota(jnp.int32, (CHUNK, CHUNK), 1)
    return r >= c


def _sgu_chunk(up, vp, g, wm0, wm1, b0, b1, low):
    ug = _gelu(up)
    vg = _gelu(vp)
    vc = vg - _head_mean(vg, low)
    rstd = lax.rsqrt(_head_mean(vc * vc, low) + LN_EPS)
    vn = vc * rstd
    vb = (vn * g).astype(bf16)
    mixed = jnp.where(low, _mm(wm0, vb) + b0, _mm(wm1, vb) + b1)
    return ug, vn, rstd, vb, mixed


def _sgu_specs(s):
    return [_slab(s, BLK_U), _slab(s, BLK_V), pl.BlockSpec((1, LANES), lambda k: (0, k)),
            pl.BlockSpec((2, CHUNK, CHUNK), lambda k: (k, 0, 0)), pl.BlockSpec((2, CHUNK, 1), lambda k: (k, 0, 0))]


def _sgu_fwd(proj, sgu_g, w_spatial, b_spatial3):
    s = proj.shape[0]

    def body(u_ref, v_ref, g_ref, w_ref, b_ref, o_ref):
        low = lax.broadcasted_iota(jnp.int32, (1, LANES), 1) < HEAD
        mask = _tril()
        wm0 = jnp.where(mask, w_ref[0], 0.0).astype(bf16)
        wm1 = jnp.where(mask, w_ref[1], 0.0).astype(bf16)
        g = g_ref[...]
        b0 = b_ref[0]
        b1 = b_ref[1]

        def chunk(n, carry):
            rows = pl.ds(pl.multiple_of(n * CHUNK, CHUNK), CHUNK)
            ug, _, _, _, mixed = _sgu_chunk(u_ref[rows, :], v_ref[rows, :], g, wm0, wm1, b0, b1, low)
            o_ref[rows, :] = (ug * mixed).astype(bf16)
            return carry

        lax.fori_loop(0, s // CHUNK, chunk, 0)

    return pl.pallas_call(
        body, grid=(SGU_W // LANES,),
        in_specs=_sgu_specs(s),
        out_specs=_slab(s, 0),
        out_shape=jax.ShapeDtypeStruct((s, SGU_W), bf16),
        name="sgu_fwd", compiler_params=_params(("arbitrary",)))(proj, proj, sgu_g, w_spatial, b_spatial3)


def _sgu_bwd(proj, dmix, sgu_g, w_spatial, b_spatial3):
    s = proj.shape[0]

    def body(u_ref, v_ref, g_ref, w_ref, b_ref, dy_ref, du_ref, dv_ref, dg_ref, dw_ref, db_ref):
        low = lax.broadcasted_iota(jnp.int32, (1, LANES), 1) < HEAD
        mask = _tril()
        w0 = jnp.where(mask, w_ref[0], 0.0)
        w1 = jnp.where(mask, w_ref[1], 0.0)
        wm0 = w0.astype(bf16)
        wm1 = w1.astype(bf16)
        wt0 = w0.T.astype(bf16)
        wt1 = w1.T.astype(bf16)
        g = g_ref[...]
        b0 = b_ref[0]
        b1 = b_ref[1]
        dg_ref[...] = jnp.zeros_like(dg_ref)
        dw_ref[...] = jnp.zeros_like(dw_ref)
        db_ref[...] = jnp.zeros_like(db_ref)

        def chunk(n, carry):
            rows = pl.ds(pl.multiple_of(n * CHUNK, CHUNK), CHUNK)
            up = u_ref[rows, :]
            vp = v_ref[rows, :]
            ug, vn, rstd, vb, mixed = _sgu_chunk(up, vp, g, wm0, wm1, b0, b1, low)
            dy = dy_ref[rows, :]
            du_ref[rows, :] = (dy * mixed * _gelu_grad(up)).astype(bf16)
            dmix_c = dy * ug
            db_ref[0] += jnp.sum(jnp.where(low, dmix_c, 0.0), axis=-1, keepdims=True)
            db_ref[1] += jnp.sum(jnp.where(low, 0.0, dmix_c), axis=-1, keepdims=True)
            dmb = dmix_c.astype(bf16)
            zero = jnp.zeros_like(dmb)
            dw_ref[0] += _nt(jnp.where(low, dmb, zero), vb)
            dw_ref[1] += _nt(jnp.where(low, zero, dmb), vb)
            dvnorm = jnp.where(low, _mm(wt0, dmb), _mm(wt1, dmb))
            dg_ref[...] += jnp.sum(dvnorm * vn, axis=0, keepdims=True)
            dvn = dvnorm * g
            dvg = rstd * (dvn - _head_mean(dvn, low) - vn * _head_mean(dvn * vn, low))
            dv_ref[rows, :] = (dvg * _gelu_grad(vp)).astype(bf16)
            return carry

        lax.fori_loop(0, s // CHUNK, chunk, 0)
        dw_ref[0] = jnp.where(mask, dw_ref[0], 0.0)
        dw_ref[1] = jnp.where(mask, dw_ref[1], 0.0)

    out = jax.ShapeDtypeStruct((s, SGU_W), bf16)
    return pl.pallas_call(
        body, grid=(SGU_W // LANES,),
        in_specs=_sgu_specs(s) + [_slab(s, (CONV_W + POOL_W) // LANES)],
        out_specs=[_slab(s, 0), _slab(s, 0), pl.BlockSpec((1, LANES), lambda k: (0, k)),
                   pl.BlockSpec((2, CHUNK, CHUNK), lambda k: (k, 0, 0)), pl.BlockSpec((2, CHUNK, 1), lambda k: (k, 0, 0))],
        out_shape=[out, out, jax.ShapeDtypeStruct((1, SGU_W), f32), jax.ShapeDtypeStruct((6, CHUNK, CHUNK), f32),
                   jax.ShapeDtypeStruct((6, CHUNK, 1), f32)],
        name="sgu_bwd", compiler_params=_params(("arbitrary",)))(proj, proj, sgu_g, w_spatial, b_spatial3, dmix)


def _fwd_mix(x, w, after):
    proj, xb = _proj(x, w["w_in"], after)
    mix = [_conv_fwd(proj, w["w_conv"]), _pool_fwd(proj, w["w_pool"], w["pool_scale"]),
           _sgu_fwd(proj, w["sgu_ln_g"], w["w_spatial"], w["b_spatial"])]
    xhat1, rstd1, hb = _wo_ln1(mix, x, w["w_o"], w["ln1_g"], w["ln1_b"])
    return dict(proj=proj, xb=xb, mix=mix, xhat1=xhat1, rstd1=rstd1, hb=hb)


def _fwd_mlp(sv, w, after):
    gu, xhat2, rstd2, y = _mlp_fwd(sv["xhat1"], w["ln1_g"], w["ln1_b"], w["w_gate_up"], w["w_down"], w["ln2_g"], w["ln2_b"], after)
    sv.update(gu=gu, xhat2=xhat2, rstd2=rstd2)
    return y


def _bwd_mlp(dy, w, sv, after, hook):
    dz2b, actb, dgub, dh, g_ln2_g, g_ln2_b = _mlp_bwd(dy, sv["xhat2"], sv["rstd2"], w["ln2_g"], sv["gu"], w["w_gate_up"],
                                                      w["w_down"], after)
    after = hook(dh)
    grads = dict(w_gate_up=_weight_grad(sv["hb"], dgub, 512, D_FF // 2, after),
                 w_down=_weight_grad(actb, dz2b, D_FF // 2, D_MODEL, after), ln2_g=g_ln2_g, ln2_b=g_ln2_b)
    return dh, grads


def _bwd_mix(dh, w, sv, after, hook):
    dz1, dz1b, dmix, g_ln1_g, g_ln1_b = _ln1_wo_bwd(dh, sv["xhat1"], sv["rstd1"], w["ln1_g"], w["w_o"], after)
    after = hook(dz1)
    dxa, dgb, dgc, g_conv = _conv_bwd(sv["proj"], dmix, w["w_conv"], after)
    dp, g_pool, g_pscale = _pool_bwd(sv["proj"], dmix, w["w_pool"], w["pool_scale"])
    du, dv, g_sgu_g, g_spatial, g_bsp = _sgu_bwd(sv["proj"], dmix, w["sgu_ln_g"], w["w_spatial"], w["b_spatial"])
    dparts = [dxa, dgb, dgc, dp, du, dv]
    dx = _dx(dz1, dparts, w["w_in"])
    grads = dict(
        w_in=_weight_grad_rows(dparts, sv["xb"], 512), w_o=_weight_grad_rows(sv["mix"], dz1b, D_MODEL),
        w_conv=g_conv, w_pool=g_pool, pool_scale=g_pscale, sgu_ln_g=g_sgu_g, w_spatial=g_spatial,
        b_spatial=g_bsp.reshape(6, CHUNK), ln1_g=g_ln1_g, ln1_b=g_ln1_b)
    return dx, grads


def _local_step(x, target, layers):
    saved = []
    for w in layers:
        sv = _fwd_mix(x, w, x)
        x = _fwd_mlp(sv, w, x)
        saved.append(sv)
    dy, sq = _loss_head(x, target)
    grads = [None] * len(layers)
    for l in reversed(range(len(layers))):
        dh, g_mlp = _bwd_mlp(dy, layers[l], saved[l], sq, lambda a: a)
        dy, g_mix = _bwd_mix(dh, layers[l], saved[l], dh, lambda a: a)
        grads[l] = dict(g_mlp, **g_mix)
    return sq, dy, grads


ANY = pl.BlockSpec(memory_space=pl.ANY)


def _place():
    x, y, c = lax.axis_index("x"), lax.axis_index("y"), lax.axis_index("c")
    others = [(1 - x, y), (x, 1 - y), (1 - x, 1 - y)]
    return x, y, c, others


def _chip_index(cx, cy):
    return 2 * cx + cy


def _half(ref_rows, c):
    half = ref_rows // 2
    return pl.ds(pl.multiple_of(c * half, 8), half)


def _remote(src, dst, send_sem, recv_sem, device):
    return pltpu.make_async_remote_copy(src_ref=src, dst_ref=dst, send_sem=send_sem, recv_sem=recv_sem,
                                        device_id=device, device_id_type=MESH)


def _gather_shards(shards):
    n = len(shards)
    base, total = [], 0
    for s in shards:
        base.append(total)
        total += 6 * s.shape[0]

    def body(*refs):
        ins, outs = refs[:n], refs[n:2 * n]
        send, recv = refs[2 * n:]
        x, y, c, others = _place()
        me = _chip_index(x, y)
        sib = (x, y, 1 - c)
        sends = []
        for f in range(n):
            depth, rows = ins[f].shape[0], ins[f].shape[1]
            for l in range(depth):
                for k, (cx, cy) in enumerate(others):
                    sem = base[f] + 6 * l + k
                    cp = _remote(ins[f].at[l, _half(rows, c)], outs[f].at[l, me, _half(rows, c)],
                                 send.at[sem], recv.at[sem], (cx, cy, c))
                    cp.start()
                    sends.append(cp)
        for f in range(n):
            depth, rows = ins[f].shape[0], ins[f].shape[1]
            for l in range(depth):
                for k, (cx, cy) in enumerate(others):
                    sem = base[f] + 6 * l + k
                    landed = outs[f].at[l, _chip_index(cx, cy), _half(rows, c)]
                    _remote(landed, landed, send.at[sem], recv.at[sem], (cx, cy, c)).wait_recv()
                    cp = _remote(landed, landed, send.at[sem + 3], recv.at[sem + 3], sib)
                    cp.start()
                    sends.append(cp)
        for f in range(n):
            depth, rows = ins[f].shape[0], ins[f].shape[1]
            for l in range(depth):
                for k, (cx, cy) in enumerate(others):
                    sem = base[f] + 6 * l + k + 3
                    passed = outs[f].at[l, _chip_index(cx, cy), _half(rows, 1 - c)]
                    _remote(passed, passed, send.at[sem], recv.at[sem], sib).wait_recv()
        for cp in sends:
            cp.wait_send()

    gathered = pl.pallas_call(
        body, in_specs=[ANY] * n, out_specs=[ANY] * n,
        out_shape=[jax.ShapeDtypeStruct((s.shape[0], N_CHIPS) + s.shape[1:], s.dtype) for s in shards],
        scratch_shapes=[pltpu.SemaphoreType.DMA((total,)), pltpu.SemaphoreType.DMA((total,))],
        name="gather_shards")(*shards)
    return [_place_own(g, s) for g, s in zip(gathered, shards)]


def _scalar(value):
    return jnp.reshape(value, (1,)).astype(jnp.int32)


def _place_own(blocks, shard):
    depth, rows, cols = shard.shape

    def body(me_ref, b_ref, s_ref, o_ref):
        o_ref[...] = s_ref[...]

    return pl.pallas_call(
        body,
        grid_spec=pltpu.PrefetchScalarGridSpec(
            num_scalar_prefetch=1, grid=(depth,),
            in_specs=[ANY, pl.BlockSpec((None, rows, cols), lambda l, me: (l, 0, 0))],
            out_specs=pl.BlockSpec((None, None, rows, cols), lambda l, me: (l, me[0], 0, 0))),
        out_shape=jax.ShapeDtypeStruct(blocks.shape, blocks.dtype),
        input_output_aliases={1: 0},
        name="place_own", compiler_params=_params(("arbitrary",)))(
            _scalar(_chip_index(lax.axis_index("x"), lax.axis_index("y"))), blocks, shard)


HBM = pl.BlockSpec(memory_space=pltpu.HBM)
SEM = pl.BlockSpec(memory_space=pltpu.SEMAPHORE)
TOKEN = jax.ShapeDtypeStruct((8, LANES), f32)
SPLIT_COPY = pltpu.CompilerParams(has_side_effects=pltpu.SideEffectType.DATAFLOW_SIDE_EFFECTING)


def _in_hbm(a):
    return pltpu.with_memory_space_constraint(a, pltpu.HBM)


def _full_shape(shard, axis):
    rows, cols = shard.shape
    return (N_CHIPS * rows, cols) if axis == 0 else (rows, N_CHIPS * cols)


def _block_half(ref, axis, j, h):
    if axis == 0:
        rows = ref.shape[0] // N_CHIPS
        return ref.at[pl.ds(pl.multiple_of(j * rows + h * (rows // 2), 16), rows // 2), :]
    half, cols = ref.shape[0] // 2, ref.shape[1] // N_CHIPS
    return ref.at[pl.ds(pl.multiple_of(h * half, 16), half), pl.ds(pl.multiple_of(j * cols, LANES), cols)]


def _gather_start(shards, axes, after):
    lands = [lax.empty(_full_shape(s, ax), s.dtype) for s, ax in zip(shards, axes)]
    return _split_copy_start("gather", _gather_plan(axes), 3 * len(shards), shards, lands, after)


def _gather_wait(state, axes, after):
    return _split_copy_wait("gather", _gather_plan(axes), state, after)


def _split_copy_start(name, plan, count, ins, lands, after):
    arrays = list(ins) + list(lands)
    n_in, n = len(ins), len(arrays)

    def body(*refs):
        send, recv, token = refs[n + 1], refs[n + 2], refs[-1]
        for i, (src, dst, _, peer) in enumerate(plan(refs[:n_in], refs[n_in:n])):
            _remote(src, dst, send.at[i], recv.at[i], peer).start()
        token[...] = jnp.zeros_like(token)

    outs = pl.pallas_call(
        body, name=name + "_start",
        in_specs=[HBM] * n + [ANY],
        out_specs=(SEM, SEM, *[HBM] * n, pl.BlockSpec(memory_space=pltpu.VMEM)),
        out_shape=(pltpu.SemaphoreType.DMA((count,)), pltpu.SemaphoreType.DMA((count,)),
                   *[pltpu.HBM(a.shape, a.dtype) for a in arrays], TOKEN),
        input_output_aliases={i: 2 + i for i in range(n)},
        compiler_params=SPLIT_COPY)(*[_in_hbm(a) for a in arrays], after)
    return (outs[0], outs[1], outs[2:2 + n_in], outs[2 + n_in:2 + n]), outs[-1]


def _split_copy_wait(name, plan, state, after):
    send_sems, recv_sems, ins, lands = state
    arrays = list(ins) + list(lands)
    n_in, n = len(ins), len(arrays)

    def body(*refs):
        send, recv, token = refs[n], refs[n + 1], refs[-1]
        for i, (src, _, landing, peer) in enumerate(plan(refs[:n_in], refs[n_in:n])):
            cp = _remote(src, landing, send.at[i], recv.at[i], peer)
            cp.wait_send()
            cp.wait_recv()
        token[...] = jnp.zeros_like(token)

    outs = pl.pallas_call(
        body, name=name + "_wait",
        in_specs=[HBM] * n + [SEM, SEM, ANY],
        out_specs=(*[HBM] * n, pl.BlockSpec(memory_space=pltpu.VMEM)),
        out_shape=(*[pltpu.HBM(a.shape, a.dtype) for a in arrays], TOKEN),
        input_output_aliases={i: i for i in range(n)},
        compiler_params=SPLIT_COPY)(*arrays, send_sems, recv_sems, after)
    return outs[:n_in], outs[n_in:n], outs[-1]


def _gather_plan(axes):
    def plan(ins, lnd):
        x, y, c, others = _place()
        me = _chip_index(x, y)
        return [(ins[f].at[_half(ins[f].shape[0], c)], _block_half(lnd[f], ax, me, c),
                 _block_half(lnd[f], ax, _chip_index(cx, cy), c), (cx, cy, c))
                for f, ax in enumerate(axes) for cx, cy in others]
    return plan


def _pair_plan(axes):
    def plan(ins, lnd):
        x, y, c, _ = _place()
        return [(_block_half(ins[f], ax, j, 1 - c), lnd[f].at[j], lnd[f].at[j], (x, y, 1 - c))
                for f, ax in enumerate(axes) for j in range(N_CHIPS)]
    return plan


def _scatter_plan(ins, lnd):
    x, y, c, others = _place()
    return [(ins[f].at[_chip_index(cx, cy)], lnd[f].at[k], lnd[f].at[k], (cx, cy, c))
            for f in range(len(ins)) for k, (cx, cy) in enumerate(others)]


def _join_plan(ins, lnd):
    x, y, c, _ = _place()
    return [(lnd[f].at[_half(lnd[f].shape[0], c)], lnd[f].at[_half(lnd[f].shape[0], c)],
             lnd[f].at[_half(lnd[f].shape[0], 1 - c)], (x, y, 1 - c)) for f in range(len(lnd))]


def _gather_finish(lands, shards, axes):
    n = len(lands)

    def body(*refs):
        outs = refs[n:2 * n]
        send, recv = refs[2 * n:]
        x, y, c, others = _place()
        sib = (x, y, 1 - c)
        sends = []
        for f in range(n):
            for k, (cx, cy) in enumerate(others):
                landed = _block_half(outs[f], axes[f], _chip_index(cx, cy), c)
                cp = _remote(landed, landed, send.at[3 * f + k], recv.at[3 * f + k], sib)
                cp.start()
                sends.append(cp)
        for f in range(n):
            for k, (cx, cy) in enumerate(others):
                passed = _block_half(outs[f], axes[f], _chip_index(cx, cy), 1 - c)
                _remote(passed, passed, send.at[3 * f + k], recv.at[3 * f + k], sib).wait_recv()
        for cp in sends:
            cp.wait_send()

    full = pl.pallas_call(
        body, in_specs=[ANY] * n, out_specs=[ANY] * n,
        out_shape=[jax.ShapeDtypeStruct(a.shape, a.dtype) for a in lands],
        input_output_aliases={f: f for f in range(n)},
        scratch_shapes=[pltpu.SemaphoreType.DMA((3 * n,)), pltpu.SemaphoreType.DMA((3 * n,))],
        name="gather_finish")(*lands)

    def place(me_ref, *refs):
        ins, outs = refs[n:2 * n], refs[2 * n:]
        for f in range(n):
            outs[f][...] = ins[f][...]

    return pl.pallas_call(
        place,
        grid_spec=pltpu.PrefetchScalarGridSpec(
            num_scalar_prefetch=1, grid=(1,),
            in_specs=[ANY] * n + [pl.BlockSpec(s.shape, lambda i, me: (0, 0)) for s in shards],
            out_specs=[pl.BlockSpec(s.shape, (lambda i, me: (me[0], 0)) if ax == 0 else (lambda i, me: (0, me[0])))
                       for s, ax in zip(shards, axes)]),
        out_shape=[jax.ShapeDtypeStruct(a.shape, a.dtype) for a in full],
        input_output_aliases={1 + f: f for f in range(n)},
        name="place_own_layer", compiler_params=_params(("arbitrary",)))(
            _scalar(_chip_index(lax.axis_index("x"), lax.axis_index("y"))), *full, *shards)


def _half_blocks(part, axis):
    rows, cols = (part.shape[0] // N_CHIPS, part.shape[1]) if axis == 0 else (part.shape[0], part.shape[1] // N_CHIPS)
    return lax.empty((N_CHIPS, rows // 2, cols), part.dtype)


def _add_pair_layer(parts, gots, axes):
    k = len(parts)

    def body(c_ref, *refs):
        for f in range(k):
            a_ref, b_ref, o_ref = refs[2 * f], refs[2 * f + 1], refs[2 * k + f]
            o_ref[...] = (a_ref[...].astype(f32) + b_ref[...].astype(f32)).astype(o_ref.dtype)

    in_specs, out_specs, operands = [], [], []
    for part, got, axis in zip(parts, gots, axes):
        _, half, cols = got.shape
        if axis == 0:
            part = part.reshape(N_CHIPS, 2, half, cols)
            mine = pl.BlockSpec((None, None, half, cols), lambda j, c: (j, c[0], 0, 0))
        else:
            mine = pl.BlockSpec((half, cols), lambda j, c: (c[0], j))
        block = pl.BlockSpec((None, half, cols), lambda j, c: (j, 0, 0))
        in_specs += [mine, block]
        out_specs.append(block)
        operands += [part, got]
    return pl.pallas_call(
        body,
        grid_spec=pltpu.PrefetchScalarGridSpec(num_scalar_prefetch=1, grid=(N_CHIPS,), in_specs=in_specs, out_specs=out_specs),
        out_shape=[jax.ShapeDtypeStruct(g.shape, p.dtype) for p, g in zip(parts, gots)],
        name="add_pair_layer", compiler_params=_params(("arbitrary",)))(_scalar(lax.axis_index("c")), *operands)


def _scatter_start(sums, after):
    lands = [lax.empty((3,) + s.shape[1:], s.dtype) for s in sums]
    return _split_copy_start("scatter", _scatter_plan, 3 * len(sums), sums, lands, after)


def _scatter_wait(state, after):
    return _split_copy_wait("scatter", _scatter_plan, state, after)


ELEMENTWISE_BLOCK_BYTES = 1 << 20


def _row_tile(rows, cols):
    best = None
    for tile in range(8, rows + 1, 8):
        if rows % tile == 0 and tile * cols * 4 <= ELEMENTWISE_BLOCK_BYTES:
            best = tile
    return best or rows


def _add_slots(chip_sums, slots):
    k = len(chip_sums)

    def body(at_ref, *refs):
        for f in range(k):
            own_ref, s_ref, o_ref = refs[2 * f], refs[2 * f + 1], refs[2 * k + f]
            acc = own_ref[...].astype(f32)
            for j in range(3):
                acc = acc + s_ref[j].astype(f32)
            o_ref[...] = acc

    in_specs, out_specs, operands = [], [], []
    for cs, s in zip(chip_sums, slots):
        _, half, cols = cs.shape
        in_specs += [pl.BlockSpec((None, half, cols), lambda i, at: (at[0], 0, 0)), pl.BlockSpec((3, half, cols), lambda i, at: (0, 0, 0))]
        out_specs.append(pl.BlockSpec((None, half, cols), lambda i, at: (at[1], 0, 0)))
        operands += [cs, s]
    at = jnp.concatenate([_scalar(_chip_index(lax.axis_index("x"), lax.axis_index("y"))), _scalar(lax.axis_index("c"))])
    outs = pl.pallas_call(
        body,
        grid_spec=pltpu.PrefetchScalarGridSpec(num_scalar_prefetch=1, grid=(1,), in_specs=in_specs, out_specs=out_specs),
        out_shape=[jax.ShapeDtypeStruct((2,) + cs.shape[1:], f32) for cs in chip_sums],
        name="add_slots", compiler_params=_params(("arbitrary",)))(at, *operands)
    return [o.reshape(2 * o.shape[1], o.shape[2]) for o in outs]


def _adamw_math(w, grad, m, v):
    nm = ADAM_B1 * m + (1.0 - ADAM_B1) * grad
    nv = ADAM_B2 * v + (1.0 - ADAM_B2) * (grad * grad)
    m_hat = nm / (1.0 - ADAM_B1 ** ADAM_STEP)
    v_hat = nv / (1.0 - ADAM_B2 ** ADAM_STEP)
    return nm, nv, -ADAM_LR * (m_hat / (jnp.sqrt(v_hat) + ADAM_EPS) + ADAM_WD * w)


def _adamw(w, g, m, v):
    shape = w.shape
    flat = [a.reshape(-1, shape[-1]) for a in (w, g, m, v)]
    tile = _row_tile(flat[0].shape[0], shape[-1])

    def body(w_ref, g_ref, m_ref, v_ref, d_ref, nm_ref, nv_ref):
        nm, nv, step = _adamw_math(w_ref[...], g_ref[...], m_ref[...], v_ref[...])
        d_ref[...] = step
        nm_ref[...] = nm
        nv_ref[...] = nv

    spec = _rows(shape[-1], tile)
    out = jax.ShapeDtypeStruct(flat[0].shape, f32)
    res = pl.pallas_call(
        body, grid=(flat[0].shape[0] // tile,),
        in_specs=[spec] * 4, out_specs=[spec] * 3, out_shape=[out] * 3,
        name="adamw", compiler_params=_params(("arbitrary",)))(*flat)
    return [r.reshape(shape) for r in res]


def _adamw_layer(l, ws, ms, vs, gs, outs, steps, after):
    k = len(ws)

    def body(*refs):
        ins, new = refs[:4 * k], refs[8 * k + 1:]
        for f in range(k):
            w_ref, m_ref, v_ref, g_ref = ins[4 * f:4 * f + 4]
            go_ref, d_ref, nm_ref, nv_ref = new[4 * f:4 * f + 4]
            grad = g_ref[...]
            nm, nv, step = _adamw_math(w_ref[...], grad, m_ref[...], v_ref[...])
            go_ref[...] = grad
            d_ref[...] = step
            nm_ref[...] = nm
            nv_ref[...] = nv

    in_specs, out_specs, operands = [], [], []
    for w, m, v, g in zip(ws, ms, vs, gs):
        _, rows, cols = w.shape
        tile = rows // steps
        layer = pl.BlockSpec((None, tile, cols), lambda i: (l, i, 0))
        in_specs += [layer] * 3 + [_rows(cols, tile)]
        out_specs += [layer] * 4
        operands += [w, m, v, g]
    flat_outs = [o for four in outs for o in four]
    res = pl.pallas_call(
        body, grid=(steps,),
        in_specs=in_specs + [ANY] * (4 * k + 1), out_specs=out_specs,
        out_shape=[jax.ShapeDtypeStruct(o.shape, f32) for o in flat_outs],
        input_output_aliases={4 * k + j: j for j in range(4 * k)},
        name="adamw_layer", compiler_params=_params(("arbitrary",)))(*operands, *flat_outs, after)
    return [res[4 * f:4 * f + 4] for f in range(k)]


SMALL = ("w_conv", "w_pool", "pool_scale", "sgu_ln_g", "w_spatial", "b_spatial", "ln1_g", "ln1_b", "ln2_g", "ln2_b")
WEIGHTS = ("w_in", "w_conv", "w_pool", "pool_scale", "sgu_ln_g", "w_spatial", "b_spatial", "w_o", "ln1_g", "ln1_b",
           "w_gate_up", "w_down", "ln2_g", "ln2_b")
BIG = ("w_in", "w_o", "w_gate_up", "w_down")
GROUPS = (("w_in", "w_o"), ("w_gate_up", "w_down"))
GROUP_AXES = ((0, 0), (1, 0))
SCATTER_HOOKS = 2
ADAMW_STEPS = (4, 8)
SMALL_LAYER_ROWS = 1024


def _pack_layer(arrays):
    flat = jnp.concatenate([a.reshape(-1) for a in arrays])
    return jnp.pad(flat, (0, SMALL_LAYER_ROWS * LANES - flat.shape[0])).reshape(SMALL_LAYER_ROWS, LANES)


def _unpack_layers(flat, shapes):
    out, at = {}, 0
    for name, shape in shapes.items():
        size = 1
        for d in shape:
            size *= d
        out[name] = flat[:, at:at + size].reshape((flat.shape[0],) + tuple(shape))
        at += size
    return out


def kernel(x, w_in, w_conv, w_pool, pool_scale, sgu_ln_g, w_spatial, b_spatial, w_o, ln1_g, ln1_b, w_gate_up, w_down, ln2_g, ln2_b, loss_target, m_w_in, m_w_conv, m_w_pool, m_pool_scale, m_sgu_ln_g, m_w_spatial, m_b_spatial, m_w_o, m_ln1_g, m_ln1_b, m_w_gate_up, m_w_down, m_ln2_g, m_ln2_b, v_w_in, v_w_conv, v_w_pool, v_pool_scale, v_sgu_ln_g, v_w_spatial, v_b_spatial, v_w_o, v_ln1_g, v_ln1_b, v_w_gate_up, v_w_down, v_ln2_g, v_ln2_b):
    weights = dict(w_in=w_in, w_conv=w_conv, w_pool=w_pool, pool_scale=pool_scale, sgu_ln_g=sgu_ln_g, w_spatial=w_spatial,
                   b_spatial=b_spatial, w_o=w_o, ln1_g=ln1_g, ln1_b=ln1_b, w_gate_up=w_gate_up, w_down=w_down, ln2_g=ln2_g, ln2_b=ln2_b)
    m_in = dict(w_in=m_w_in, w_conv=m_w_conv, w_pool=m_w_pool, pool_scale=m_pool_scale, sgu_ln_g=m_sgu_ln_g, w_spatial=m_w_spatial,
                b_spatial=m_b_spatial, w_o=m_w_o, ln1_g=m_ln1_g, ln1_b=m_ln1_b, w_gate_up=m_w_gate_up, w_down=m_w_down,
                ln2_g=m_ln2_g, ln2_b=m_ln2_b)
    v_in = dict(w_in=v_w_in, w_conv=v_w_conv, w_pool=v_w_pool, pool_scale=v_pool_scale, sgu_ln_g=v_sgu_ln_g, w_spatial=v_w_spatial,
                b_spatial=v_b_spatial, w_o=v_w_o, ln1_g=v_ln1_g, ln1_b=v_ln1_b, w_gate_up=v_w_gate_up, w_down=v_w_down,
                ln2_g=v_ln2_g, ln2_b=v_ln2_b)
    depth = w_in.shape[0]
    conv_cols = w_conv.shape[2]
    chip = _chip_index(lax.axis_index("x"), lax.axis_index("y"))

    conv_flat = jnp.pad(w_conv.reshape(-1), (0, 16 * LANES - w_conv.size)).reshape(1, 16, LANES)
    conv_full = _gather_shards([conv_flat])[0].reshape(N_CHIPS, 16 * LANES)[:, :w_conv.size].reshape(N_CHIPS, depth, 3, conv_cols)
    conv_full = conv_full.transpose(1, 2, 0, 3).reshape(depth, 3, N_CHIPS * conv_cols)

    big_w = dict(w_in=jnp.swapaxes(w_in, 1, 2), w_o=w_o, w_gate_up=w_gate_up, w_down=w_down)
    big_m = dict(w_in=jnp.swapaxes(m_w_in, 1, 2), w_o=m_w_o, w_gate_up=m_w_gate_up, w_down=m_w_down)
    big_v = dict(w_in=jnp.swapaxes(v_w_in, 1, 2), w_o=v_w_o, w_gate_up=v_w_gate_up, w_down=v_w_down)

    def send(l, g, after):
        return _gather_start([big_w[n][l].astype(bf16) for n in GROUPS[g]], GROUP_AXES[g], after)

    def receive(g, flight, after):
        shards, lands, token = _gather_wait(flight, GROUP_AXES[g], after)
        return shards, lands, token

    act = x[0]
    layers, saved = [], []
    flight, token = send(0, 0, conv_full)
    for l in range(depth):
        w = dict(w_conv=conv_full[l], w_pool=w_pool[l], pool_scale=pool_scale[l][None], sgu_ln_g=sgu_ln_g[l][None],
                 w_spatial=w_spatial[l], b_spatial=b_spatial[l][:, :, None], ln1_g=ln1_g[l][None], ln1_b=ln1_b[l][None],
                 ln2_g=ln2_g[l][None], ln2_b=ln2_b[l][None])
        shards, lands, token = receive(0, flight, act)
        flight, token = send(l, 1, token)
        w.update(zip(GROUPS[0], _gather_finish(lands, shards, GROUP_AXES[0])))
        sv = _fwd_mix(act, w, token)
        shards, lands, token = receive(1, flight, sv["xhat1"])
        if l + 1 < depth:
            flight, token = send(l + 1, 0, token)
        w.update(zip(GROUPS[1], _gather_finish(lands, shards, GROUP_AXES[1])))
        act = _fwd_mlp(sv, w, token)
        layers.append(w)
        saved.append(sv)

    big_outs = {n: [lax.empty(big_w[n].shape, f32) for _ in range(4)] for n in BIG}
    small_sums = [None] * depth
    pending, updates = [], []
    latest = dict(token=None)

    def begin(l, g, parts):
        axes = GROUP_AXES[g] + (0,) * (len(parts) - len(GROUPS[g]))
        lands = [_half_blocks(p, ax) for p, ax in zip(parts, axes)]
        flight, latest["token"] = _split_copy_start("pair", _pair_plan(axes), N_CHIPS * len(parts), parts, lands, latest["token"])
        pending.append(dict(l=l, g=g, axes=axes, step="pair", age=0, flight=flight))

    def advance(st, recent):
        if st["step"] == "pair":
            parts, got, _ = _split_copy_wait("pair", _pair_plan(st["axes"]), st["flight"], recent)
            sums = _add_pair_layer(parts, got, st["axes"])
            st["flight"], latest["token"] = _scatter_start(sums, latest["token"])
            st["step"] = "scatter"
        elif st["step"] == "scatter":
            sums, slots, _ = _scatter_wait(st["flight"], recent)
            filled = _add_slots(sums, slots)
            st["flight"], latest["token"] = _split_copy_start("join", _join_plan, len(filled), [], filled, latest["token"])
            st["step"] = "join"
        else:
            _, summed, _ = _split_copy_wait("join", _join_plan, st["flight"], recent)
            updates.append((st["l"], st["g"], summed[:len(GROUPS[st["g"]])]))
            if st["g"] == 0:
                small_sums[st["l"]] = summed[-1]
            st["step"] = "done"
        st["age"] = 0

    def hook(recent):
        for st in reversed(list(pending)):
            st["age"] += 1
            if st["age"] >= SCATTER_HOOKS or st["step"] != "scatter":
                advance(st, recent)
                if st["step"] == "done":
                    pending.remove(st)
        return latest["token"]

    def update(count, recent):
        for l, g, totals in updates[:count]:
            names = GROUPS[g]
            new = _adamw_layer(l, [big_w[n] for n in names], [big_m[n] for n in names], [big_v[n] for n in names], totals,
                               [big_outs[n] for n in names], ADAMW_STEPS[g], latest["token"])
            big_outs.update(zip(names, new))
            recent = new[-1][1]
        del updates[:count]
        return recent

    grad_x, sq = _loss_head(act, loss_target[0])
    latest["token"] = sq
    grads = [None] * depth
    for l in reversed(range(depth)):
        dh, g_mlp = _bwd_mlp(grad_x, layers[l], saved[l], latest["token"], hook)
        hook(g_mlp["w_down"])
        begin(l, 1, [g_mlp[n] for n in GROUPS[1]])
        grad_x, g_mix = _bwd_mix(dh, layers[l], saved[l], latest["token"], hook)
        grads[l] = dict(g_mlp, **g_mix)
        hook(g_mix["w_o"])
        begin(l, 0, [g_mix[n] for n in GROUPS[0]] + [_pack_layer([grads[l][n] for n in SMALL])])
    recent = g_mix["w_o"]
    while pending:
        recent = update(-(-len(updates) // 2), recent)
        hook(recent)
    update(len(updates), recent)
    loss = lax.psum(0.5 / D_MODEL * jnp.sum(sq), ("x", "y", "c"))

    small_sum = _gather_shards([jnp.stack(small_sums)])[0].reshape(depth, SMALL_LAYER_ROWS * LANES)
    grad = {n: [jnp.swapaxes(o, 1, 2) for o in big_outs[n]] if n == "w_in" else big_outs[n] for n in BIG}
    delta = {n: o[1] for n, o in grad.items()}
    new_m = {n: o[2] for n, o in grad.items()}
    new_v = {n: o[3] for n, o in grad.items()}
    grad = {n: o[0] for n, o in grad.items()}
    grad.update(_unpack_layers(small_sum, {n: (3, N_CHIPS * conv_cols) if n == "w_conv" else weights[n].shape[1:] for n in SMALL}))
    grad["w_conv"] = lax.dynamic_slice_in_dim(grad["w_conv"], chip * conv_cols, conv_cols, axis=2)

    delta["w_conv"], new_m["w_conv"], new_v["w_conv"] = _adamw(w_conv, grad["w_conv"], m_w_conv, v_w_conv)
    rest = [n for n in SMALL if n != "w_conv"]
    rest_shapes = {n: weights[n].shape[1:] for n in rest}
    packed = [jnp.concatenate([_pack_layer([src[n][l] for n in rest]) for l in range(depth)]) for src in (weights, grad, m_in, v_in)]
    for dst, res in zip((delta, new_m, new_v), _adamw(*packed)):
        dst.update(_unpack_layers(res.reshape(depth, SMALL_LAYER_ROWS * LANES), rest_shapes))

    return (loss, grad_x[None], *[grad[n] for n in WEIGHTS], *[delta[n] for n in WEIGHTS],
            *[new_m[n] for n in WEIGHTS], *[new_v[n] for n in WEIGHTS])
```

```python
import functools

import jax
import jax.numpy as jnp
from jax import lax
from jax.experimental import pallas as pl
from jax.experimental.pallas import tpu as pltpu

f32 = jnp.float32
bf16 = jnp.bfloat16

D_MODEL = 1024
DEPTH = 4
CONV_W = 384
POOL_W = 256
SGU_W = 384
IN_W = 3 * CONV_W + POOL_W + 2 * SGU_W
D_FF = 2816
CHUNK = 128
HEAD = 64
POOL_WINDOWS = (2, 4, 8, 16)
ALPHA = float((2 * DEPTH) ** 0.25)
LN_EPS = 1e-5
ADAM_LR = 0.001
ADAM_B1 = 0.9
ADAM_B2 = 0.999
ADAM_EPS = 1e-08
ADAM_WD = 0.01
ADAM_STEP = 10

LANES = 128
TOKEN_TILE = 256
N_CHIPS = 4
VMEM_LIMIT = 56 * 1024 * 1024

BLK_XA, BLK_GB, BLK_GC, BLK_P, BLK_U, BLK_V = 0, 3, 6, 9, 11, 14

MESH = pl.DeviceIdType.MESH


def _params(sem=None):
    return pltpu.CompilerParams(dimension_semantics=sem, vmem_limit_bytes=VMEM_LIMIT)


def _rows(width, tile=TOKEN_TILE):
    return pl.BlockSpec((tile, width), lambda i: (i, 0))


def _resident(shape):
    zeros = (0,) * len(shape)
    return pl.BlockSpec(shape, lambda *_: zeros, pipeline_mode=pl.Buffered(1))


def _nt(a, b):
    return lax.dot_general(a, b, (((1,), (1,)), ((), ())), preferred_element_type=f32)


def _tn(a, b):
    return lax.dot_general(a, b, (((0,), (0,)), ((), ())), preferred_element_type=f32)


def _mm(a, b):
    return jnp.dot(a, b, preferred_element_type=f32)


def _norm_fwd(z):
    mu = jnp.mean(z, axis=-1, keepdims=True)
    zc = z - mu
    var = jnp.mean(zc * zc, axis=-1, keepdims=True)
    rstd = lax.rsqrt(var + LN_EPS)
    return zc * rstd, rstd


def _norm_bwd(dxhat, xhat, rstd):
    m1 = jnp.mean(dxhat, axis=-1, keepdims=True)
    m2 = jnp.mean(dxhat * xhat, axis=-1, keepdims=True)
    return rstd * (dxhat - m1 - xhat * m2)


def _proj(x, w_in_b, after):
    s = x.shape[0]

    def body(x_ref, w_ref, after_ref, p_ref, xb_ref):
        xb = x_ref[...].astype(bf16)
        xb_ref[...] = xb
        p_ref[...] = _nt(xb, w_ref[...])

    return pl.pallas_call(
        body, grid=(s // TOKEN_TILE,),
        in_specs=[_rows(D_MODEL), _resident((IN_W, D_MODEL)), pl.BlockSpec(memory_space=pl.ANY)],
        out_specs=[_rows(IN_W), _rows(D_MODEL)],
        out_shape=[jax.ShapeDtypeStruct((s, IN_W), f32), jax.ShapeDtypeStruct((s, D_MODEL), bf16)],
        name="proj", compiler_params=_params(("arbitrary",)))(x, w_in_b, after)


def _row_ranges(parts):
    out, at = [], 0
    for p in parts:
        out.append((at, at + p.shape[1]))
        at += p.shape[1]
    return out


def _wo_ln1(mix, x, w_o_b, g, b):
    s = x.shape[0]
    n = len(mix)
    ranges = _row_ranges(mix)

    def body(*refs):
        m_refs = refs[:n]
        x_ref, w_ref, g_ref, b_ref, xhat_ref, rstd_ref, hb_ref = refs[n:]
        z = ALPHA * x_ref[...]
        for m_ref, (lo, hi) in zip(m_refs, ranges):
            z = z + _mm(m_ref[...], w_ref[lo:hi, :])
        xhat, rstd = _norm_fwd(z)
        xhat_ref[...] = xhat
        rstd_ref[...] = rstd
        hb_ref[...] = (xhat * g_ref[...] + b_ref[...]).astype(bf16)

    return pl.pallas_call(
        body, grid=(s // TOKEN_TILE,),
        in_specs=[_rows(m.shape[1]) for m in mix] + [_rows(D_MODEL), _resident((D_MODEL, D_MODEL)), _resident((1, D_MODEL)),
                                                     _resident((1, D_MODEL))],
        out_specs=[_rows(D_MODEL), _rows(1), _rows(D_MODEL)],
        out_shape=[jax.ShapeDtypeStruct((s, D_MODEL), f32), jax.ShapeDtypeStruct((s, 1), f32),
                   jax.ShapeDtypeStruct((s, D_MODEL), bf16)],
        name="wo_ln1", compiler_params=_params(("arbitrary",)))(*mix, x, w_o_b, g, b)


def _mlp_fwd(xhat1, g1, b1, w_gu_b, w_down_b, g2, b2, after):
    s = xhat1.shape[0]

    def body(xh_ref, g1_ref, b1_ref, wgu_ref, wd_ref, g2_ref, b2_ref, after_ref, gu_ref, xhat2_ref, rstd2_ref, y_ref):
        h = xh_ref[...] * g1_ref[...] + b1_ref[...]
        gu = _mm(h.astype(bf16), wgu_ref[...])
        gu_ref[...] = gu
        gate = gu[:, :D_FF]
        act = gate * jax.nn.sigmoid(gate) * gu[:, D_FF:]
        z = ALPHA * h + _mm(act.astype(bf16), wd_ref[...])
        xhat2, rstd2 = _norm_fwd(z)
        xhat2_ref[...] = xhat2
        rstd2_ref[...] = rstd2
        y_ref[...] = xhat2 * g2_ref[...] + b2_ref[...]

    vec = _resident((1, D_MODEL))
    return pl.pallas_call(
        body, grid=(s // TOKEN_TILE,),
        in_specs=[_rows(D_MODEL), vec, vec, _resident((D_MODEL, 2 * D_FF)), _resident((D_FF, D_MODEL)), vec, vec,
                  pl.BlockSpec(memory_space=pl.ANY)],
        out_specs=[_rows(2 * D_FF), _rows(D_MODEL), _rows(1), _rows(D_MODEL)],
        out_shape=[jax.ShapeDtypeStruct((s, 2 * D_FF), f32), jax.ShapeDtypeStruct((s, D_MODEL), f32),
                   jax.ShapeDtypeStruct((s, 1), f32), jax.ShapeDtypeStruct((s, D_MODEL), f32)],
        name="mlp_fwd", compiler_params=_params(("arbitrary",)))(xhat1, g1, b1, w_gu_b, w_down_b, g2, b2, after)


def _loss_head(y, target):
    s = y.shape[0]

    def body(y_ref, t_ref, dy_ref, sq_ref):
        @pl.when(pl.program_id(0) == 0)
        def _():
            sq_ref[...] = jnp.zeros_like(sq_ref)

        e = y_ref[...] - t_ref[...]
        dy_ref[...] = e * (1.0 / D_MODEL)
        sq_ref[...] += jnp.sum(e * e, axis=0, keepdims=True)

    return pl.pallas_call(
        body, grid=(s // TOKEN_TILE,),
        in_specs=[_rows(D_MODEL), _rows(D_MODEL)],
        out_specs=[_rows(D_MODEL), pl.BlockSpec((1, D_MODEL), lambda i: (0, 0))],
        out_shape=[jax.ShapeDtypeStruct((s, D_MODEL), f32), jax.ShapeDtypeStruct((1, D_MODEL), f32)],
        name="loss_head", compiler_params=_params(("arbitrary",)))(y, target)


def _mlp_bwd(dy, xhat2, rstd2, g2, gu, w_gu_b, w_down_b, after):
    s = dy.shape[0]

    def body(dy_ref, xh_ref, rs_ref, g2_ref, gu_ref, wgu_ref, wd_ref, after_ref, dz_ref, act_ref, dgu_ref, dh_ref, gg_ref, gb_ref):
        @pl.when(pl.program_id(0) == 0)
        def _():
            gg_ref[...] = jnp.zeros_like(gg_ref)
            gb_ref[...] = jnp.zeros_like(gb_ref)

        dy_t = dy_ref[...]
        xhat = xh_ref[...]
        gg_ref[...] += jnp.sum(dy_t * xhat, axis=0, keepdims=True)
        gb_ref[...] += jnp.sum(dy_t, axis=0, keepdims=True)
        dz = _norm_bwd(dy_t * g2_ref[...], xhat, rs_ref[...])
        dzb = dz.astype(bf16)
        dz_ref[...] = dzb
        dact = _nt(dzb, wd_ref[...])
        gate = gu_ref[:, :D_FF]
        up = gu_ref[:, D_FF:]
        sg = jax.nn.sigmoid(gate)
        silu = gate * sg
        act_ref[...] = (silu * up).astype(bf16)
        dgu_ref[:, :D_FF] = (dact * up * (sg * (1.0 + gate * (1.0 - sg)))).astype(bf16)
        dgu_ref[:, D_FF:] = (dact * silu).astype(bf16)
        dh_ref[...] = ALPHA * dz + _nt(dgu_ref[...], wgu_ref[...])

    vec_out = pl.BlockSpec((1, D_MODEL), lambda i: (0, 0))
    return pl.pallas_call(
        body, grid=(s // TOKEN_TILE,),
        in_specs=[_rows(D_MODEL), _rows(D_MODEL), _rows(1), _resident((1, D_MODEL)), _rows(2 * D_FF),
                  _resident((D_MODEL, 2 * D_FF)), _resident((D_FF, D_MODEL)), pl.BlockSpec(memory_space=pl.ANY)],
        out_specs=[_rows(D_MODEL), _rows(D_FF), _rows(2 * D_FF), _rows(D_MODEL), vec_out, vec_out],
        out_shape=[jax.ShapeDtypeStruct((s, D_MODEL), bf16), jax.ShapeDtypeStruct((s, D_FF), bf16),
                   jax.ShapeDtypeStruct((s, 2 * D_FF), bf16), jax.ShapeDtypeStruct((s, D_MODEL), f32),
                   jax.ShapeDtypeStruct((1, D_MODEL), f32), jax.ShapeDtypeStruct((1, D_MODEL), f32)],
        name="mlp_bwd", compiler_params=_params(("arbitrary",)))(dy, xhat2, rstd2, g2, gu, w_gu_b, w_down_b, after)


def _ln1_wo_bwd(dh, xhat1, rstd1, g1, w_o_b, after):
    s = dh.shape[0]

    def body(dh_ref, xh_ref, rs_ref, g1_ref, w_ref, after_ref, dz_ref, dzb_ref, dm_ref, gg_ref, gb_ref):
        @pl.when(pl.program_id(0) == 0)
        def _():
            gg_ref[...] = jnp.zeros_like(gg_ref)
            gb_ref[...] = jnp.zeros_like(gb_ref)

        dh_t = dh_ref[...]
        xhat = xh_ref[...]
        gg_ref[...] += jnp.sum(dh_t * xhat, axis=0, keepdims=True)
        gb_ref[...] += jnp.sum(dh_t, axis=0, keepdims=True)
        dz = _norm_bwd(dh_t * g1_ref[...], xhat, rs_ref[...])
        dz_ref[...] = dz
        dzb = dz.astype(bf16)
        dzb_ref[...] = dzb
        dm_ref[...] = _nt(dzb, w_ref[...])

    vec_out = pl.BlockSpec((1, D_MODEL), lambda i: (0, 0))
    return pl.pallas_call(
        body, grid=(s // TOKEN_TILE,),
        in_specs=[_rows(D_MODEL), _rows(D_MODEL), _rows(1), _resident((1, D_MODEL)), _resident((D_MODEL, D_MODEL)),
                  pl.BlockSpec(memory_space=pl.ANY)],
        out_specs=[_rows(D_MODEL), _rows(D_MODEL), _rows(D_MODEL), vec_out, vec_out],
        out_shape=[jax.ShapeDtypeStruct((s, D_MODEL), f32), jax.ShapeDtypeStruct((s, D_MODEL), bf16),
                   jax.ShapeDtypeStruct((s, D_MODEL), f32), jax.ShapeDtypeStruct((1, D_MODEL), f32),
                   jax.ShapeDtypeStruct((1, D_MODEL), f32)],
        name="ln1_wo_bwd", compiler_params=_params(("arbitrary",)))(dh, xhat1, rstd1, g1, w_o_b, after)


def _dx(dz1, dparts, w_in_t):
    s = dz1.shape[0]
    n = len(dparts)
    ranges = _row_ranges(dparts)

    def body(*refs):
        d_refs = refs[:n]
        dz_ref, w_ref, dx_ref = refs[n:]
        acc = ALPHA * dz_ref[...]
        for d_ref, (lo, hi) in zip(d_refs, ranges):
            acc = acc + _mm(d_ref[...], w_ref[lo:hi, :])
        dx_ref[...] = acc

    return pl.pallas_call(
        body, grid=(s // TOKEN_TILE,),
        in_specs=[_rows(d.shape[1]) for d in dparts] + [_rows(D_MODEL), _resident((IN_W, D_MODEL))],
        out_specs=_rows(D_MODEL),
        out_shape=jax.ShapeDtypeStruct((s, D_MODEL), f32),
        name="dx", compiler_params=_params(("arbitrary",)))(*dparts, dz1, w_in_t)


def _weight_grad_rows(parts, b, bn):
    s, n_cols = b.shape
    n = len(parts)
    ranges = _row_ranges(parts)
    m = ranges[-1][1]

    def body(*refs):
        p_refs = refs[:n]
        b_ref, o_ref = refs[n:]
        for p_ref, (lo, hi) in zip(p_refs, ranges):
            o_ref[lo:hi, :] = _tn(p_ref[...], b_ref[...]).astype(bf16)

    return pl.pallas_call(
        body, grid=(n_cols // bn,),
        in_specs=[_resident(p.shape) for p in parts] + [pl.BlockSpec((s, bn), lambda j: (0, j))],
        out_specs=pl.BlockSpec((m, bn), lambda j: (0, j)),
        out_shape=jax.ShapeDtypeStruct((m, n_cols), bf16),
        name="weight_grad_rows", compiler_params=_params(("arbitrary",)))(*parts, b)


def _weight_grad(a, b, bm, bn, after):
    s, m = a.shape
    n = b.shape[1]

    def body(a_ref, b_ref, after_ref, o_ref):
        o_ref[...] = _tn(a_ref[...], b_ref[...]).astype(bf16)

    return pl.pallas_call(
        body, grid=(m // bm, n // bn),
        in_specs=[pl.BlockSpec((s, bm), lambda i, j: (0, i)), pl.BlockSpec((s, bn), lambda i, j: (0, j)),
                  pl.BlockSpec(memory_space=pl.ANY)],
        out_specs=pl.BlockSpec((bm, bn), lambda i, j: (i, j)),
        out_shape=jax.ShapeDtypeStruct((m, n), bf16),
        name="weight_grad", compiler_params=_params(("arbitrary", "arbitrary")))(a, b, after)


def _shift_down(a, k):
    row = lax.broadcasted_iota(jnp.int32, a.shape, 0)
    return jnp.where(row >= k, pltpu.roll(a, k, 0), 0.0)


def _shift_up(a, k):
    n = a.shape[0]
    row = lax.broadcasted_iota(jnp.int32, a.shape, 0)
    return jnp.where(row < n - k, pltpu.roll(a, n - k, 0), 0.0)


def _slab(s, block):
    return pl.BlockSpec((s, LANES), lambda k: (0, block + k))


def _conv_y(z, w):
    return w[0:1, :] * _shift_down(z, 2) + w[1:2, :] * _shift_down(z, 1) + w[2:3, :] * z


def _conv_fwd(proj, w_conv):
    s = proj.shape[0]

    def body(xa_ref, gb_ref, gc_ref, w_ref, o_ref):
        z = gc_ref[...] * xa_ref[...]
        o_ref[...] = (gb_ref[...] * _conv_y(z, w_ref[...])).astype(bf16)

    return pl.pallas_call(
        body, grid=(CONV_W // LANES,),
        in_specs=[_slab(s, BLK_XA), _slab(s, BLK_GB), _slab(s, BLK_GC), pl.BlockSpec((3, LANES), lambda k: (0, k))],
        out_specs=_slab(s, 0),
        out_shape=jax.ShapeDtypeStruct((s, CONV_W), bf16),
        name="conv_fwd", compiler_params=_params(("arbitrary",)))(proj, proj, proj, w_conv)


def _conv_bwd(proj, dmix, w_conv, after):
    s = proj.shape[0]

    def body(xa_ref, gb_ref, gc_ref, dy_ref, w_ref, after_ref, dxa_ref, dgb_ref, dgc_ref, dw_ref):
        xa = xa_ref[...]
        gc = gc_ref[...]
        w = w_ref[...]
        z = gc * xa
        dya = dy_ref[...]
        dgb_ref[...] = (dya * _conv_y(z, w)).astype(bf16)
        dy = dya * gb_ref[...]
        dz = w[2:3, :] * dy + w[1:2, :] * _shift_up(dy, 1) + w[0:1, :] * _shift_up(dy, 2)
        dxa_ref[...] = (dz * gc).astype(bf16)
        dgc_ref[...] = (dz * xa).astype(bf16)
        dw_ref[0:1, :] = jnp.sum(dy * _shift_down(z, 2), axis=0, keepdims=True)
        dw_ref[1:2, :] = jnp.sum(dy * _shift_down(z, 1), axis=0, keepdims=True)
        dw_ref[2:3, :] = jnp.sum(dy * z, axis=0, keepdims=True)

    out = jax.ShapeDtypeStruct((s, CONV_W), bf16)
    return pl.pallas_call(
        body, grid=(CONV_W // LANES,),
        in_specs=[_slab(s, BLK_XA), _slab(s, BLK_GB), _slab(s, BLK_GC), _slab(s, 0), pl.BlockSpec((3, LANES), lambda k: (0, k)),
                  pl.BlockSpec(memory_space=pl.ANY)],
        out_specs=[_slab(s, 0), _slab(s, 0), _slab(s, 0), pl.BlockSpec((3, LANES), lambda k: (0, k))],
        out_shape=[out, out, out, jax.ShapeDtypeStruct((3, CONV_W), f32)],
        name="conv_bwd", compiler_params=_params(("arbitrary",)))(proj, proj, proj, dmix, w_conv, after)


def _pool_window(k):
    lane = lax.broadcasted_iota(jnp.int32, (1, LANES), 1)
    low = lane < HEAD
    first = k == 0
    wlen = jnp.where(low, jnp.where(first, POOL_WINDOWS[0], POOL_WINDOWS[2]), jnp.where(first, POOL_WINDOWS[1], POOL_WINDOWS[3]))
    return wlen, low, first


def _pool_diff(p, k):
    wlen, low, first = _pool_window(k)
    s2 = p + _shift_down(p, 1)
    s4 = s2 + _shift_down(s2, 2)
    s8 = s4 + _shift_down(s4, 4)
    s16 = s8 + _shift_down(s8, 8)
    win = jnp.where(low, jnp.where(first, s2, s8), jnp.where(first, s4, s16))
    row = lax.broadcasted_iota(jnp.int32, p.shape, 0)
    count = jnp.minimum(row + 1, wlen).astype(f32)
    return win / count - p, count


def _pool_weight(w_ref):
    zero = jnp.zeros((HEAD, HEAD), f32)
    top = jnp.concatenate([w_ref[0], zero], axis=1)
    bottom = jnp.concatenate([zero, w_ref[1]], axis=1)
    return jnp.concatenate([top, bottom], axis=0).astype(bf16)


def _pool_fwd(proj, w_pool, pool_scale):
    s = proj.shape[0]

    def body(p_ref, w_ref, sc_ref, o_ref):
        d, _ = _pool_diff(p_ref[...], pl.program_id(0))
        o_ref[...] = (_mm(d.astype(bf16), _pool_weight(w_ref)) * sc_ref[...]).astype(bf16)

    return pl.pallas_call(
        body, grid=(POOL_W // LANES,),
        in_specs=[_slab(s, BLK_P), pl.BlockSpec((2, HEAD, HEAD), lambda k: (k, 0, 0)), pl.BlockSpec((1, LANES), lambda k: (0, k))],
        out_specs=_slab(s, 0),
        out_shape=jax.ShapeDtypeStruct((s, POOL_W), bf16),
        name="pool_fwd", compiler_params=_params(("arbitrary",)))(proj, w_pool, pool_scale)


def _pool_bwd(proj, dmix, w_pool, pool_scale):
    s = proj.shape[0]

    def body(p_ref, dy_ref, w_ref, sc_ref, dp_ref, dw_ref, dsc_ref):
        k = pl.program_id(0)
        d, count = _pool_diff(p_ref[...], k)
        wbd = _pool_weight(w_ref)
        db = d.astype(bf16)
        dyb = dy_ref[...]
        dsc_ref[...] = jnp.sum(dyb * _mm(db, wbd), axis=0, keepdims=True)
        dpre = (dyb * sc_ref[...]).astype(bf16)
        dwbd = _tn(db, dpre)
        dw_ref[0] = dwbd[:HEAD, :HEAD]
        dw_ref[1] = dwbd[HEAD:, HEAD:]
        dd = _nt(dpre, wbd)
        e = dd / count
        wlen, low, first = _pool_window(k)
        a2 = e + _shift_up(e, 1)
        a4 = a2 + _shift_up(a2, 2)
        a8 = a4 + _shift_up(a4, 4)
        a16 = a8 + _shift_up(a8, 8)
        back = jnp.where(low, jnp.where(first, a2, a8), jnp.where(first, a4, a16))
        dp_ref[...] = (back - dd).astype(bf16)

    return pl.pallas_call(
        body, grid=(POOL_W // LANES,),
        in_specs=[_slab(s, BLK_P), _slab(s, CONV_W // LANES), pl.BlockSpec((2, HEAD, HEAD), lambda k: (k, 0, 0)),
                  pl.BlockSpec((1, LANES), lambda k: (0, k))],
        out_specs=[_slab(s, 0), pl.BlockSpec((2, HEAD, HEAD), lambda k: (k, 0, 0)), pl.BlockSpec((1, LANES), lambda k: (0, k))],
        out_shape=[jax.ShapeDtypeStruct((s, POOL_W), bf16), jax.ShapeDtypeStruct((4, HEAD, HEAD), f32),
                   jax.ShapeDtypeStruct((1, POOL_W), f32)],
        name="pool_bwd", compiler_params=_params(("arbitrary",)))(proj, dmix, w_pool, pool_scale)


SGU_UNROLL = 4
INV_SQRT2 = 0.7071067811865476
INV_SQRT_2PI = 0.3989422804014327


def _gelu(x):
    return 0.5 * x * (1.0 + lax.erf(x * INV_SQRT2))


def _gelu_grad(x):
    return 0.5 * (1.0 + lax.erf(x * INV_SQRT2)) + x * (INV_SQRT_2PI * jnp.exp(-0.5 * x * x))


def _head_mean(a, low):
    s_low = jnp.sum(jnp.where(low, a, 0.0), axis=-1, keepdims=True)
    s_high = jnp.sum(jnp.where(low, 0.0, a), axis=-1, keepdims=True)
    return jnp.where(low, s_low, s_high) * (1.0 / HEAD)


def _tril():
    r = lax.broadcasted_iota(jnp.int32, (CHUNK, CHUNK), 0)
    c = lax.broadcasted_iota(jnp.int32, (CHUNK, CHUNK), 1)
    return r >= c


def _sgu_chunk(up, vp, g, wm0, wm1, b0, b1, low):
    ug = _gelu(up)
    vg = _gelu(vp)
    vc = vg - _head_mean(vg, low)
    rstd = lax.rsqrt(_head_mean(vc * vc, low) + LN_EPS)
    vn = vc * rstd
    vb = (vn * g).astype(bf16)
    mixed = jnp.where(low, _mm(wm0, vb) + b0, _mm(wm1, vb) + b1)
    return ug, vn, rstd, vb, mixed


def _sgu_specs(s):
    return [_slab(s, BLK_U), _slab(s, BLK_V), pl.BlockSpec((1, LANES), lambda k: (0, k)),
            pl.BlockSpec((2, CHUNK, CHUNK), lambda k: (k, 0, 0)), pl.BlockSpec((2, CHUNK, 1), lambda k: (k, 0, 0))]


def _sgu_fwd(proj, sgu_g, w_spatial, b_spatial3):
    s = proj.shape[0]

    def body(u_ref, v_ref, g_ref, w_ref, b_ref, o_ref):
        low = lax.broadcasted_iota(jnp.int32, (1, LANES), 1) < HEAD
        mask = _tril()
        wm0 = jnp.where(mask, w_ref[0], 0.0).astype(bf16)
        wm1 = jnp.where(mask, w_ref[1], 0.0).astype(bf16)
        g = g_ref[...]
        b0 = b_ref[0]
        b1 = b_ref[1]

        def chunk(n, carry):
            rows = pl.ds(pl.multiple_of(n * CHUNK, CHUNK), CHUNK)
            ug, _, _, _, mixed = _sgu_chunk(u_ref[rows, :], v_ref[rows, :], g, wm0, wm1, b0, b1, low)
            o_ref[rows, :] = (ug * mixed).astype(bf16)
            return carry

        lax.fori_loop(0, s // CHUNK, chunk, 0, unroll=SGU_UNROLL)

    return pl.pallas_call(
        body, grid=(SGU_W // LANES,),
        in_specs=_sgu_specs(s),
        out_specs=_slab(s, 0),
        out_shape=jax.ShapeDtypeStruct((s, SGU_W), bf16),
        name="sgu_fwd", compiler_params=_params(("arbitrary",)))(proj, proj, sgu_g, w_spatial, b_spatial3)


def _sgu_bwd(proj, dmix, sgu_g, w_spatial, b_spatial3):
    s = proj.shape[0]

    def body(u_ref, v_ref, g_ref, w_ref, b_ref, dy_ref, du_ref, dv_ref, dg_ref, dw_ref, db_ref):
        low = lax.broadcasted_iota(jnp.int32, (1, LANES), 1) < HEAD
        mask = _tril()
        w0 = jnp.where(mask, w_ref[0], 0.0)
        w1 = jnp.where(mask, w_ref[1], 0.0)
        wm0 = w0.astype(bf16)
        wm1 = w1.astype(bf16)
        wt0 = w0.T.astype(bf16)
        wt1 = w1.T.astype(bf16)
        g = g_ref[...]
        b0 = b_ref[0]
        b1 = b_ref[1]
        dg_ref[...] = jnp.zeros_like(dg_ref)
        dw_ref[...] = jnp.zeros_like(dw_ref)
        db_ref[...] = jnp.zeros_like(db_ref)

        def chunk(n, carry):
            rows = pl.ds(pl.multiple_of(n * CHUNK, CHUNK), CHUNK)
            up = u_ref[rows, :]
            vp = v_ref[rows, :]
            ug, vn, rstd, vb, mixed = _sgu_chunk(up, vp, g, wm0, wm1, b0, b1, low)
            dy = dy_ref[rows, :]
            du_ref[rows, :] = (dy * mixed * _gelu_grad(up)).astype(bf16)
            dmix_c = dy * ug
            db_ref[0] += jnp.sum(jnp.where(low, dmix_c, 0.0), axis=-1, keepdims=True)
            db_ref[1] += jnp.sum(jnp.where(low, 0.0, dmix_c), axis=-1, keepdims=True)
            dmb = dmix_c.astype(bf16)
            zero = jnp.zeros_like(dmb)
            dw_ref[0] += _nt(jnp.where(low, dmb, zero), vb)
            dw_ref[1] += _nt(jnp.where(low, zero, dmb), vb)
            dvnorm = jnp.where(low, _mm(wt0, dmb), _mm(wt1, dmb))
            dg_ref[...] += jnp.sum(dvnorm * vn, axis=0, keepdims=True)
            dvn = dvnorm * g
            dvg = rstd * (dvn - _head_mean(dvn, low) - vn * _head_mean(dvn * vn, low))
            dv_ref[rows, :] = (dvg * _gelu_grad(vp)).astype(bf16)
            return carry

        lax.fori_loop(0, s // CHUNK, chunk, 0, unroll=SGU_UNROLL)
        dw_ref[0] = jnp.where(mask, dw_ref[0], 0.0)
        dw_ref[1] = jnp.where(mask, dw_ref[1], 0.0)

    out = jax.ShapeDtypeStruct((s, SGU_W), bf16)
    return pl.pallas_call(
        body, grid=(SGU_W // LANES,),
        in_specs=_sgu_specs(s) + [_slab(s, (CONV_W + POOL_W) // LANES)],
        out_specs=[_slab(s, 0), _slab(s, 0), pl.BlockSpec((1, LANES), lambda k: (0, k)),
                   pl.BlockSpec((2, CHUNK, CHUNK), lambda k: (k, 0, 0)), pl.BlockSpec((2, CHUNK, 1), lambda k: (k, 0, 0))],
        out_shape=[out, out, jax.ShapeDtypeStruct((1, SGU_W), f32), jax.ShapeDtypeStruct((6, CHUNK, CHUNK), f32),
                   jax.ShapeDtypeStruct((6, CHUNK, 1), f32)],
        name="sgu_bwd", compiler_params=_params(("arbitrary",)))(proj, proj, sgu_g, w_spatial, b_spatial3, dmix)


def _fwd_mix(x, w, after):
    proj, xb = _proj(x, w["w_in"], after)
    mix = [_conv_fwd(proj, w["w_conv"]), _pool_fwd(proj, w["w_pool"], w["pool_scale"]),
           _sgu_fwd(proj, w["sgu_ln_g"], w["w_spatial"], w["b_spatial"])]
    xhat1, rstd1, hb = _wo_ln1(mix, x, w["w_o"], w["ln1_g"], w["ln1_b"])
    return dict(proj=proj, xb=xb, mix=mix, xhat1=xhat1, rstd1=rstd1, hb=hb)


def _fwd_mlp(sv, w, after):
    gu, xhat2, rstd2, y = _mlp_fwd(sv["xhat1"], w["ln1_g"], w["ln1_b"], w["w_gate_up"], w["w_down"], w["ln2_g"], w["ln2_b"], after)
    sv.update(gu=gu, xhat2=xhat2, rstd2=rstd2)
    return y


def _bwd_mlp(dy, w, sv, after, hook):
    dz2b, actb, dgub, dh, g_ln2_g, g_ln2_b = _mlp_bwd(dy, sv["xhat2"], sv["rstd2"], w["ln2_g"], sv["gu"], w["w_gate_up"],
                                                      w["w_down"], after)
    after = hook(dh)
    grads = dict(w_gate_up=_weight_grad(sv["hb"], dgub, 512, D_FF // 2, after),
                 w_down=_weight_grad(actb, dz2b, D_FF // 2, D_MODEL, after), ln2_g=g_ln2_g, ln2_b=g_ln2_b)
    return dh, grads


def _bwd_mix(dh, w, sv, after, hook):
    dz1, dz1b, dmix, g_ln1_g, g_ln1_b = _ln1_wo_bwd(dh, sv["xhat1"], sv["rstd1"], w["ln1_g"], w["w_o"], after)
    after = hook(dz1)
    dxa, dgb, dgc, g_conv = _conv_bwd(sv["proj"], dmix, w["w_conv"], after)
    dp, g_pool, g_pscale = _pool_bwd(sv["proj"], dmix, w["w_pool"], w["pool_scale"])
    du, dv, g_sgu_g, g_spatial, g_bsp = _sgu_bwd(sv["proj"], dmix, w["sgu_ln_g"], w["w_spatial"], w["b_spatial"])
    dparts = [dxa, dgb, dgc, dp, du, dv]
    dx = _dx(dz1, dparts, w["w_in"])
    grads = dict(
        w_in=_weight_grad_rows(dparts, sv["xb"], 512), w_o=_weight_grad_rows(sv["mix"], dz1b, D_MODEL),
        w_conv=g_conv, w_pool=g_pool, pool_scale=g_pscale, sgu_ln_g=g_sgu_g, w_spatial=g_spatial,
        b_spatial=g_bsp.reshape(6, CHUNK), ln1_g=g_ln1_g, ln1_b=g_ln1_b)
    return dx, grads


def _local_step(x, target, layers):
    saved = []
    for w in layers:
        sv = _fwd_mix(x, w, x)
        x = _fwd_mlp(sv, w, x)
        saved.append(sv)
    dy, sq = _loss_head(x, target)
    grads = [None] * len(layers)
    for l in reversed(range(len(layers))):
        dh, g_mlp = _bwd_mlp(dy, layers[l], saved[l], sq, lambda a: a)
        dy, g_mix = _bwd_mix(dh, layers[l], saved[l], dh, lambda a: a)
        grads[l] = dict(g_mlp, **g_mix)
    return sq, dy, grads


ANY = pl.BlockSpec(memory_space=pl.ANY)


def _place():
    x, y, c = lax.axis_index("x"), lax.axis_index("y"), lax.axis_index("c")
    others = [(1 - x, y), (x, 1 - y), (1 - x, 1 - y)]
    return x, y, c, others


def _chip_index(cx, cy):
    return 2 * cx + cy


def _half(ref_rows, c):
    half = ref_rows // 2
    return pl.ds(pl.multiple_of(c * half, 8), half)


def _remote(src, dst, send_sem, recv_sem, device):
    return pltpu.make_async_remote_copy(src_ref=src, dst_ref=dst, send_sem=send_sem, recv_sem=recv_sem,
                                        device_id=device, device_id_type=MESH)


def _gather_shards(shards):
    n = len(shards)
    base, total = [], 0
    for s in shards:
        base.append(total)
        total += 6 * s.shape[0]

    def body(*refs):
        ins, outs = refs[:n], refs[n:2 * n]
        send, recv = refs[2 * n:]
        x, y, c, others = _place()
        me = _chip_index(x, y)
        sib = (x, y, 1 - c)
        sends = []
        for f in range(n):
            depth, rows = ins[f].shape[0], ins[f].shape[1]
            for l in range(depth):
                for k, (cx, cy) in enumerate(others):
                    sem = base[f] + 6 * l + k
                    cp = _remote(ins[f].at[l, _half(rows, c)], outs[f].at[l, me, _half(rows, c)],
                                 send.at[sem], recv.at[sem], (cx, cy, c))
                    cp.start()
                    sends.append(cp)
        for f in range(n):
            depth, rows = ins[f].shape[0], ins[f].shape[1]
            for l in range(depth):
                for k, (cx, cy) in enumerate(others):
                    sem = base[f] + 6 * l + k
                    landed = outs[f].at[l, _chip_index(cx, cy), _half(rows, c)]
                    _remote(landed, landed, send.at[sem], recv.at[sem], (cx, cy, c)).wait_recv()
                    cp = _remote(landed, landed, send.at[sem + 3], recv.at[sem + 3], sib)
                    cp.start()
                    sends.append(cp)
        for f in range(n):
            depth, rows = ins[f].shape[0], ins[f].shape[1]
            for l in range(depth):
                for k, (cx, cy) in enumerate(others):
                    sem = base[f] + 6 * l + k + 3
                    passed = outs[f].at[l, _chip_index(cx, cy), _half(rows, 1 - c)]
                    _remote(passed, passed, send.at[sem], recv.at[sem], sib).wait_recv()
        for cp in sends:
            cp.wait_send()

    gathered = pl.pallas_call(
        body, in_specs=[ANY] * n, out_specs=[ANY] * n,
        out_shape=[jax.ShapeDtypeStruct((s.shape[0], N_CHIPS) + s.shape[1:], s.dtype) for s in shards],
        scratch_shapes=[pltpu.SemaphoreType.DMA((total,)), pltpu.SemaphoreType.DMA((total,))],
        name="gather_shards")(*shards)
    return [_place_own(g, s) for g, s in zip(gathered, shards)]


def _scalar(value):
    return jnp.reshape(value, (1,)).astype(jnp.int32)


def _place_own(blocks, shard):
    depth, rows, cols = shard.shape

    def body(me_ref, b_ref, s_ref, o_ref):
        o_ref[...] = s_ref[...]

    return pl.pallas_call(
        body,
        grid_spec=pltpu.PrefetchScalarGridSpec(
            num_scalar_prefetch=1, grid=(depth,),
            in_specs=[ANY, pl.BlockSpec((None, rows, cols), lambda l, me: (l, 0, 0))],
            out_specs=pl.BlockSpec((None, None, rows, cols), lambda l, me: (l, me[0], 0, 0))),
        out_shape=jax.ShapeDtypeStruct(blocks.shape, blocks.dtype),
        input_output_aliases={1: 0},
        name="place_own", compiler_params=_params(("arbitrary",)))(
            _scalar(_chip_index(lax.axis_index("x"), lax.axis_index("y"))), blocks, shard)


HBM = pl.BlockSpec(memory_space=pltpu.HBM)
SEM = pl.BlockSpec(memory_space=pltpu.SEMAPHORE)
TOKEN = jax.ShapeDtypeStruct((8, LANES), f32)
SPLIT_COPY = pltpu.CompilerParams(has_side_effects=pltpu.SideEffectType.DATAFLOW_SIDE_EFFECTING)


def _in_hbm(a):
    return pltpu.with_memory_space_constraint(a, pltpu.HBM)


def _full_shape(shard, axis):
    rows, cols = shard.shape
    return (N_CHIPS * rows, cols) if axis == 0 else (rows, N_CHIPS * cols)


def _block_half(ref, axis, j, h):
    if axis == 0:
        rows = ref.shape[0] // N_CHIPS
        return ref.at[pl.ds(pl.multiple_of(j * rows + h * (rows // 2), 16), rows // 2), :]
    half, cols = ref.shape[0] // 2, ref.shape[1] // N_CHIPS
    return ref.at[pl.ds(pl.multiple_of(h * half, 16), half), pl.ds(pl.multiple_of(j * cols, LANES), cols)]


def _place_layer(shards, axes, after):
    n = len(shards)

    def body(me_ref, *refs):
        ins, outs = refs[n:2 * n], refs[2 * n + 1:]
        for f in range(n):
            outs[f][...] = ins[f][...]

    lands = [lax.empty(_full_shape(s, ax), s.dtype) for s, ax in zip(shards, axes)]
    return pl.pallas_call(
        body,
        grid_spec=pltpu.PrefetchScalarGridSpec(
            num_scalar_prefetch=1, grid=(1,),
            in_specs=[ANY] * n + [pl.BlockSpec(s.shape, lambda i, me: (0, 0)) for s in shards] + [ANY],
            out_specs=[pl.BlockSpec(s.shape, (lambda i, me: (me[0], 0)) if ax == 0 else (lambda i, me: (0, me[0])))
                       for s, ax in zip(shards, axes)]),
        out_shape=[jax.ShapeDtypeStruct(a.shape, a.dtype) for a in lands],
        input_output_aliases={1 + f: f for f in range(n)},
        name="place_layer", compiler_params=_params(("arbitrary",)))(
            _scalar(_chip_index(lax.axis_index("x"), lax.axis_index("y"))), *lands, *shards, after)


def _gather_start(shards, lands, axes, after):
    return _split_copy_start("gather", _gather_plan(axes), 3 * len(shards), shards, lands, after)


def _gather_wait(state, axes, after):
    return _split_copy_wait("gather", _gather_plan(axes), state, after)


def _split_copy_start(name, plan, count, ins, lands, after):
    arrays = list(ins) + list(lands)
    n_in, n = len(ins), len(arrays)

    def body(*refs):
        send, recv, token = refs[n + 1], refs[n + 2], refs[-1]
        for i, (src, dst, _, peer) in enumerate(plan(refs[:n_in], refs[n_in:n])):
            _remote(src, dst, send.at[i], recv.at[i], peer).start()
        token[...] = jnp.zeros_like(token)

    outs = pl.pallas_call(
        body, name=name + "_start",
        in_specs=[HBM] * n + [ANY],
        out_specs=(SEM, SEM, *[HBM] * n, pl.BlockSpec(memory_space=pltpu.VMEM)),
        out_shape=(pltpu.SemaphoreType.DMA((count,)), pltpu.SemaphoreType.DMA((count,)),
                   *[pltpu.HBM(a.shape, a.dtype) for a in arrays], TOKEN),
        input_output_aliases={i: 2 + i for i in range(n)},
        compiler_params=SPLIT_COPY)(*[_in_hbm(a) for a in arrays], after)
    return (outs[0], outs[1], outs[2:2 + n_in], outs[2 + n_in:2 + n]), outs[-1]


def _split_copy_wait(name, plan, state, after):
    send_sems, recv_sems, ins, lands = state
    arrays = list(ins) + list(lands)
    n_in, n = len(ins), len(arrays)

    def body(*refs):
        send, recv, token = refs[n], refs[n + 1], refs[-1]
        for i, (src, _, landing, peer) in enumerate(plan(refs[:n_in], refs[n_in:n])):
            cp = _remote(src, landing, send.at[i], recv.at[i], peer)
            cp.wait_send()
            cp.wait_recv()
        token[...] = jnp.zeros_like(token)

    outs = pl.pallas_call(
        body, name=name + "_wait",
        in_specs=[HBM] * n + [SEM, SEM, ANY],
        out_specs=(*[HBM] * n, pl.BlockSpec(memory_space=pltpu.VMEM)),
        out_shape=(*[pltpu.HBM(a.shape, a.dtype) for a in arrays], TOKEN),
        input_output_aliases={i: i for i in range(n)},
        compiler_params=SPLIT_COPY)(*arrays, send_sems, recv_sems, after)
    return outs[:n_in], outs[n_in:n], outs[-1]


def _gather_plan(axes):
    def plan(ins, lnd):
        x, y, c, others = _place()
        me = _chip_index(x, y)
        return [(ins[f].at[_half(ins[f].shape[0], c)], _block_half(lnd[f], ax, me, c),
                 _block_half(lnd[f], ax, _chip_index(cx, cy), c), (cx, cy, c))
                for f, ax in enumerate(axes) for cx, cy in others]
    return plan


def _pair_plan(axes):
    def plan(ins, lnd):
        x, y, c, _ = _place()
        return [(_block_half(ins[f], ax, j, 1 - c), lnd[f].at[j], lnd[f].at[j], (x, y, 1 - c))
                for f, ax in enumerate(axes) for j in range(N_CHIPS)]
    return plan


def _scatter_plan(ins, lnd):
    x, y, c, others = _place()
    return [(ins[f].at[_chip_index(cx, cy)], lnd[f].at[k], lnd[f].at[k], (cx, cy, c))
            for f in range(len(ins)) for k, (cx, cy) in enumerate(others)]


def _join_plan(ins, lnd):
    x, y, c, _ = _place()
    return [(lnd[f].at[_half(lnd[f].shape[0], c)], lnd[f].at[_half(lnd[f].shape[0], c)],
             lnd[f].at[_half(lnd[f].shape[0], 1 - c)], (x, y, 1 - c)) for f in range(len(lnd))]


def _gather_finish(lands, axes):
    n = len(lands)

    def body(*refs):
        outs = refs[n:2 * n]
        send, recv = refs[2 * n:]
        x, y, c, others = _place()
        sib = (x, y, 1 - c)
        sends = []
        for f in range(n):
            for k, (cx, cy) in enumerate(others):
                landed = _block_half(outs[f], axes[f], _chip_index(cx, cy), c)
                cp = _remote(landed, landed, send.at[3 * f + k], recv.at[3 * f + k], sib)
                cp.start()
                sends.append(cp)
        for f in range(n):
            for k, (cx, cy) in enumerate(others):
                passed = _block_half(outs[f], axes[f], _chip_index(cx, cy), 1 - c)
                _remote(passed, passed, send.at[3 * f + k], recv.at[3 * f + k], sib).wait_recv()
        for cp in sends:
            cp.wait_send()

    return pl.pallas_call(
        body, in_specs=[ANY] * n, out_specs=[ANY] * n,
        out_shape=[jax.ShapeDtypeStruct(a.shape, a.dtype) for a in lands],
        input_output_aliases={f: f for f in range(n)},
        scratch_shapes=[pltpu.SemaphoreType.DMA((3 * n,)), pltpu.SemaphoreType.DMA((3 * n,))],
        name="gather_finish")(*lands)


def _half_blocks(part, axis):
    rows, cols = (part.shape[0] // N_CHIPS, part.shape[1]) if axis == 0 else (part.shape[0], part.shape[1] // N_CHIPS)
    return lax.empty((N_CHIPS, rows // 2, cols), part.dtype)


def _add_pair_layer(parts, gots, axes):
    k = len(parts)

    def body(c_ref, *refs):
        for f in range(k):
            a_ref, b_ref, o_ref = refs[2 * f], refs[2 * f + 1], refs[2 * k + f]
            o_ref[...] = (a_ref[...].astype(f32) + b_ref[...].astype(f32)).astype(o_ref.dtype)

    in_specs, out_specs, operands = [], [], []
    for part, got, axis in zip(parts, gots, axes):
        _, half, cols = got.shape
        if axis == 0:
            part = part.reshape(N_CHIPS, 2, half, cols)
            mine = pl.BlockSpec((None, None, half, cols), lambda j, c: (j, c[0], 0, 0))
        else:
            mine = pl.BlockSpec((half, cols), lambda j, c: (c[0], j))
        block = pl.BlockSpec((None, half, cols), lambda j, c: (j, 0, 0))
        in_specs += [mine, block]
        out_specs.append(block)
        operands += [part, got]
    return pl.pallas_call(
        body,
        grid_spec=pltpu.PrefetchScalarGridSpec(num_scalar_prefetch=1, grid=(N_CHIPS,), in_specs=in_specs, out_specs=out_specs),
        out_shape=[jax.ShapeDtypeStruct(g.shape, p.dtype) for p, g in zip(parts, gots)],
        name="add_pair_layer", compiler_params=_params(("arbitrary",)))(_scalar(lax.axis_index("c")), *operands)


def _scatter_start(sums, after):
    lands = [lax.empty((3,) + s.shape[1:], s.dtype) for s in sums]
    return _split_copy_start("scatter", _scatter_plan, 3 * len(sums), sums, lands, after)


def _scatter_wait(state, after):
    return _split_copy_wait("scatter", _scatter_plan, state, after)


ELEMENTWISE_BLOCK_BYTES = 1 << 20


def _row_tile(rows, cols):
    best = None
    for tile in range(8, rows + 1, 8):
        if rows % tile == 0 and tile * cols * 4 <= ELEMENTWISE_BLOCK_BYTES:
            best = tile
    return best or rows


def _add_slots(chip_sums, slots):
    k = len(chip_sums)

    def body(at_ref, *refs):
        for f in range(k):
            own_ref, s_ref, o_ref = refs[2 * f], refs[2 * f + 1], refs[2 * k + f]
            acc = own_ref[...].astype(f32)
            for j in range(3):
                acc = acc + s_ref[j].astype(f32)
            o_ref[...] = acc

    in_specs, out_specs, operands = [], [], []
    for cs, s in zip(chip_sums, slots):
        _, half, cols = cs.shape
        in_specs += [pl.BlockSpec((None, half, cols), lambda i, at: (at[0], 0, 0)), pl.BlockSpec((3, half, cols), lambda i, at: (0, 0, 0))]
        out_specs.append(pl.BlockSpec((None, half, cols), lambda i, at: (at[1], 0, 0)))
        operands += [cs, s]
    at = jnp.concatenate([_scalar(_chip_index(lax.axis_index("x"), lax.axis_index("y"))), _scalar(lax.axis_index("c"))])
    outs = pl.pallas_call(
        body,
        grid_spec=pltpu.PrefetchScalarGridSpec(num_scalar_prefetch=1, grid=(1,), in_specs=in_specs, out_specs=out_specs),
        out_shape=[jax.ShapeDtypeStruct((2,) + cs.shape[1:], f32) for cs in chip_sums],
        name="add_slots", compiler_params=_params(("arbitrary",)))(at, *operands)
    return [o.reshape(2 * o.shape[1], o.shape[2]) for o in outs]


def _adamw_math(w, grad, m, v):
    nm = ADAM_B1 * m + (1.0 - ADAM_B1) * grad
    nv = ADAM_B2 * v + (1.0 - ADAM_B2) * (grad * grad)
    m_hat = nm / (1.0 - ADAM_B1 ** ADAM_STEP)
    v_hat = nv / (1.0 - ADAM_B2 ** ADAM_STEP)
    return nm, nv, -ADAM_LR * (m_hat / (jnp.sqrt(v_hat) + ADAM_EPS) + ADAM_WD * w)


def _adamw(w, g, m, v):
    shape = w.shape
    flat = [a.reshape(-1, shape[-1]) for a in (w, g, m, v)]
    tile = _row_tile(flat[0].shape[0], shape[-1])

    def body(w_ref, g_ref, m_ref, v_ref, d_ref, nm_ref, nv_ref):
        nm, nv, step = _adamw_math(w_ref[...], g_ref[...], m_ref[...], v_ref[...])
        d_ref[...] = step
        nm_ref[...] = nm
        nv_ref[...] = nv

    spec = _rows(shape[-1], tile)
    out = jax.ShapeDtypeStruct(flat[0].shape, f32)
    res = pl.pallas_call(
        body, grid=(flat[0].shape[0] // tile,),
        in_specs=[spec] * 4, out_specs=[spec] * 3, out_shape=[out] * 3,
        name="adamw", compiler_params=_params(("arbitrary",)))(*flat)
    return [r.reshape(shape) for r in res]


def _adamw_layer(l, ws, ms, vs, gs, outs, steps, after):
    k = len(ws)

    def body(*refs):
        ins, new = refs[:4 * k], refs[8 * k + 1:]
        for f in range(k):
            w_ref, m_ref, v_ref, g_ref = ins[4 * f:4 * f + 4]
            go_ref, d_ref, nm_ref, nv_ref = new[4 * f:4 * f + 4]
            grad = g_ref[...]
            nm, nv, step = _adamw_math(w_ref[...], grad, m_ref[...], v_ref[...])
            go_ref[...] = grad
            d_ref[...] = step
            nm_ref[...] = nm
            nv_ref[...] = nv

    in_specs, out_specs, operands = [], [], []
    for w, m, v, g in zip(ws, ms, vs, gs):
        _, rows, cols = w.shape
        tile = rows // steps
        layer = pl.BlockSpec((None, tile, cols), lambda i: (l, i, 0))
        in_specs += [layer] * 3 + [_rows(cols, tile)]
        out_specs += [layer] * 4
        operands += [w, m, v, g]
    flat_outs = [o for four in outs for o in four]
    res = pl.pallas_call(
        body, grid=(steps,),
        in_specs=in_specs + [ANY] * (4 * k + 1), out_specs=out_specs,
        out_shape=[jax.ShapeDtypeStruct(o.shape, f32) for o in flat_outs],
        input_output_aliases={4 * k + j: j for j in range(4 * k)},
        name="adamw_layer", compiler_params=_params(("arbitrary",)))(*operands, *flat_outs, after)
    return [res[4 * f:4 * f + 4] for f in range(k)]


SMALL = ("w_conv", "w_pool", "pool_scale", "sgu_ln_g", "w_spatial", "b_spatial", "ln1_g", "ln1_b", "ln2_g", "ln2_b")
WEIGHTS = ("w_in", "w_conv", "w_pool", "pool_scale", "sgu_ln_g", "w_spatial", "b_spatial", "w_o", "ln1_g", "ln1_b",
           "w_gate_up", "w_down", "ln2_g", "ln2_b")
BIG = ("w_in", "w_o", "w_gate_up", "w_down")
GROUPS = (("w_in", "w_o"), ("w_gate_up", "w_down"))
GROUP_AXES = ((0, 0), (1, 0))
SCATTER_HOOKS = 2
ADAMW_STEPS = (4, 8)
SMALL_LAYER_ROWS = 1024


def _pack_layer(arrays):
    flat = jnp.concatenate([a.reshape(-1) for a in arrays])
    return jnp.pad(flat, (0, SMALL_LAYER_ROWS * LANES - flat.shape[0])).reshape(SMALL_LAYER_ROWS, LANES)


def _unpack_layers(flat, shapes):
    out, at = {}, 0
    for name, shape in shapes.items():
        size = 1
        for d in shape:
            size *= d
        out[name] = flat[:, at:at + size].reshape((flat.shape[0],) + tuple(shape))
        at += size
    return out


def kernel(x, w_in, w_conv, w_pool, pool_scale, sgu_ln_g, w_spatial, b_spatial, w_o, ln1_g, ln1_b, w_gate_up, w_down, ln2_g, ln2_b, loss_target, m_w_in, m_w_conv, m_w_pool, m_pool_scale, m_sgu_ln_g, m_w_spatial, m_b_spatial, m_w_o, m_ln1_g, m_ln1_b, m_w_gate_up, m_w_down, m_ln2_g, m_ln2_b, v_w_in, v_w_conv, v_w_pool, v_pool_scale, v_sgu_ln_g, v_w_spatial, v_b_spatial, v_w_o, v_ln1_g, v_ln1_b, v_w_gate_up, v_w_down, v_ln2_g, v_ln2_b):
    weights = dict(w_in=w_in, w_conv=w_conv, w_pool=w_pool, pool_scale=pool_scale, sgu_ln_g=sgu_ln_g, w_spatial=w_spatial,
                   b_spatial=b_spatial, w_o=w_o, ln1_g=ln1_g, ln1_b=ln1_b, w_gate_up=w_gate_up, w_down=w_down, ln2_g=ln2_g, ln2_b=ln2_b)
    m_in = dict(w_in=m_w_in, w_conv=m_w_conv, w_pool=m_w_pool, pool_scale=m_pool_scale, sgu_ln_g=m_sgu_ln_g, w_spatial=m_w_spatial,
                b_spatial=m_b_spatial, w_o=m_w_o, ln1_g=m_ln1_g, ln1_b=m_ln1_b, w_gate_up=m_w_gate_up, w_down=m_w_down,
                ln2_g=m_ln2_g, ln2_b=m_ln2_b)
    v_in = dict(w_in=v_w_in, w_conv=v_w_conv, w_pool=v_w_pool, pool_scale=v_pool_scale, sgu_ln_g=v_sgu_ln_g, w_spatial=v_w_spatial,
                b_spatial=v_b_spatial, w_o=v_w_o, ln1_g=v_ln1_g, ln1_b=v_ln1_b, w_gate_up=v_w_gate_up, w_down=v_w_down,
                ln2_g=v_ln2_g, ln2_b=v_ln2_b)
    depth = w_in.shape[0]
    conv_cols = w_conv.shape[2]
    chip = _chip_index(lax.axis_index("x"), lax.axis_index("y"))

    conv_flat = jnp.pad(w_conv.reshape(-1), (0, 16 * LANES - w_conv.size)).reshape(1, 16, LANES)
    conv_full = _gather_shards([conv_flat])[0].reshape(N_CHIPS, 16 * LANES)[:, :w_conv.size].reshape(N_CHIPS, depth, 3, conv_cols)
    conv_full = conv_full.transpose(1, 2, 0, 3).reshape(depth, 3, N_CHIPS * conv_cols)

    big_w = dict(w_in=jnp.swapaxes(w_in, 1, 2), w_o=w_o, w_gate_up=w_gate_up, w_down=w_down)
    big_m = dict(w_in=jnp.swapaxes(m_w_in, 1, 2), w_o=m_w_o, w_gate_up=m_w_gate_up, w_down=m_w_down)
    big_v = dict(w_in=jnp.swapaxes(v_w_in, 1, 2), w_o=v_w_o, w_gate_up=v_w_gate_up, w_down=v_w_down)

    def shards_of(l, g):
        return [big_w[n][l].astype(bf16) for n in GROUPS[g]]

    def send(l, g, after):
        return _gather_start(shards_of(l, g), placed[l, g], GROUP_AXES[g], after)

    def receive(g, flight, after):
        _, lands, token = _gather_wait(flight, GROUP_AXES[g], after)
        return _gather_finish(lands, GROUP_AXES[g]), token

    placed = {(0, 0): _place_layer(shards_of(0, 0), GROUP_AXES[0], conv_full)}
    flight, token = send(0, 0, conv_full)
    behind = conv_full
    for l, g in [(l, g) for l in range(depth) for g in (0, 1)][1:]:
        placed[l, g] = _place_layer(shards_of(l, g), GROUP_AXES[g], token)
        behind = placed[l, g][0]
    act = x[0]
    layers, saved = [], []
    for l in range(depth):
        w = dict(w_conv=conv_full[l], w_pool=w_pool[l], pool_scale=pool_scale[l][None], sgu_ln_g=sgu_ln_g[l][None],
                 w_spatial=w_spatial[l], b_spatial=b_spatial[l][:, :, None], ln1_g=ln1_g[l][None], ln1_b=ln1_b[l][None],
                 ln2_g=ln2_g[l][None], ln2_b=ln2_b[l][None])
        mats, token = receive(0, flight, behind if l == 0 else act)
        flight, token = send(l, 1, token)
        w.update(zip(GROUPS[0], mats))
        sv = _fwd_mix(act, w, token)
        mats, token = receive(1, flight, sv["xhat1"])
        if l + 1 < depth:
            flight, token = send(l + 1, 0, token)
        w.update(zip(GROUPS[1], mats))
        act = _fwd_mlp(sv, w, token)
        layers.append(w)
        saved.append(sv)

    big_outs = {n: [lax.empty(big_w[n].shape, f32) for _ in range(4)] for n in BIG}
    small_sums = [None] * depth
    pending, updates = [], []
    latest = dict(token=None)

    def begin(l, g, parts):
        axes = GROUP_AXES[g] + (0,) * (len(parts) - len(GROUPS[g]))
        lands = [_half_blocks(p, ax) for p, ax in zip(parts, axes)]
        flight, latest["token"] = _split_copy_start("pair", _pair_plan(axes), N_CHIPS * len(parts), parts, lands, latest["token"])
        pending.append(dict(l=l, g=g, axes=axes, step="pair", age=0, flight=flight))

    def advance(st, recent):
        if st["step"] == "pair":
            parts, got, _ = _split_copy_wait("pair", _pair_plan(st["axes"]), st["flight"], recent)
            sums = _add_pair_layer(parts, got, st["axes"])
            st["flight"], latest["token"] = _scatter_start(sums, latest["token"])
            st["step"] = "scatter"
        elif st["step"] == "scatter":
            sums, slots, _ = _scatter_wait(st["flight"], recent)
            filled = _add_slots(sums, slots)
            st["flight"], latest["token"] = _split_copy_start("join", _join_plan, len(filled), [], filled, latest["token"])
            st["step"] = "join"
        else:
            _, summed, _ = _split_copy_wait("join", _join_plan, st["flight"], recent)
            updates.append((st["l"], st["g"], summed[:len(GROUPS[st["g"]])]))
            if st["g"] == 0:
                small_sums[st["l"]] = summed[-1]
            st["step"] = "done"
        st["age"] = 0

    def hook(recent):
        for st in reversed(list(pending)):
            st["age"] += 1
            if st["age"] >= SCATTER_HOOKS or st["step"] != "scatter":
                advance(st, recent)
                if st["step"] == "done":
                    pending.remove(st)
        return latest["token"]

    def update(count, recent):
        for l, g, totals in updates[:count]:
            names = GROUPS[g]
            new = _adamw_layer(l, [big_w[n] for n in names], [big_m[n] for n in names], [big_v[n] for n in names], totals,
                               [big_outs[n] for n in names], ADAMW_STEPS[g], latest["token"])
            big_outs.update(zip(names, new))
            recent = new[-1][1]
        del updates[:count]
        return recent

    grad_x, sq = _loss_head(act, loss_target[0])
    latest["token"] = sq
    grads = [None] * depth
    for l in reversed(range(depth)):
        dh, g_mlp = _bwd_mlp(grad_x, layers[l], saved[l], latest["token"], hook)
        hook(g_mlp["w_down"])
        begin(l, 1, [g_mlp[n] for n in GROUPS[1]])
        grad_x, g_mix = _bwd_mix(dh, layers[l], saved[l], latest["token"], hook)
        grads[l] = dict(g_mlp, **g_mix)
        hook(g_mix["w_o"])
        begin(l, 0, [g_mix[n] for n in GROUPS[0]] + [_pack_layer([grads[l][n] for n in SMALL])])
    recent = g_mix["w_o"]
    while pending:
        recent = update(-(-len(updates) // 2), recent)
        hook(recent)
    update(len(updates), recent)
    loss = lax.psum(0.5 / D_MODEL * jnp.sum(sq), ("x", "y", "c"))

    small_sum = _gather_shards([jnp.stack(small_sums)])[0].reshape(depth, SMALL_LAYER_ROWS * LANES)
    grad = {n: [jnp.swapaxes(o, 1, 2) for o in big_outs[n]] if n == "w_in" else big_outs[n] for n in BIG}
    delta = {n: o[1] for n, o in grad.items()}
    new_m = {n: o[2] for n, o in grad.items()}
    new_v = {n: o[3] for n, o in grad.items()}
    grad = {n: o[0] for n, o in grad.items()}
    grad.update(_unpack_layers(small_sum, {n: (3, N_CHIPS * conv_cols) if n == "w_conv" else weights[n].shape[1:] for n in SMALL}))
    grad["w_conv"] = lax.dynamic_slice_in_dim(grad["w_conv"], chip * conv_cols, conv_cols, axis=2)

    delta["w_conv"], new_m["w_conv"], new_v["w_conv"] = _adamw(w_conv, grad["w_conv"], m_w_conv, v_w_conv)
    rest = [n for n in SMALL if n != "w_conv"]
    rest_shapes = {n: weights[n].shape[1:] for n in rest}
    packed = [jnp.concatenate([_pack_layer([src[n][l] for n in rest]) for l in range(depth)]) for src in (weights, grad, m_in, v_in)]
    for dst, res in zip((delta, new_m, new_v), _adamw(*packed)):
        dst.update(_unpack_layers(res.reshape(depth, SMALL_LAYER_ROWS * LANES), rest_shapes))

    return (loss, grad_x[None], *[grad[n] for n in WEIGHTS], *[delta[n] for n in WEIGHTS],
            *[new_m[n] for n in WEIGHTS], *[new_v[n] for n in WEIGHTS])
```

```python
import functools

import jax
import jax.numpy as jnp
from jax import lax
from jax.experimental import pallas as pl
from jax.experimental.pallas import tpu as pltpu

f32 = jnp.float32
bf16 = jnp.bfloat16

D_MODEL = 1024
DEPTH = 4
CONV_W = 384
POOL_W = 256
SGU_W = 384
IN_W = 3 * CONV_W + POOL_W + 2 * SGU_W
D_FF = 2816
CHUNK = 128
HEAD = 64
POOL_WINDOWS = (2, 4, 8, 16)
ALPHA = float((2 * DEPTH) ** 0.25)
LN_EPS = 1e-5
ADAM_LR = 0.001
ADAM_B1 = 0.9
ADAM_B2 = 0.999
ADAM_EPS = 1e-08
ADAM_WD = 0.01
ADAM_STEP = 10

LANES = 128
TOKEN_TILE = 256
N_CHIPS = 4
VMEM_LIMIT = 56 * 1024 * 1024

BLK_XA, BLK_GB, BLK_GC, BLK_P, BLK_U, BLK_V = 0, 3, 6, 9, 11, 14

MESH = pl.DeviceIdType.MESH


def _params(sem=None):
    return pltpu.CompilerParams(dimension_semantics=sem, vmem_limit_bytes=VMEM_LIMIT)


def _rows(width, tile=TOKEN_TILE):
    return pl.BlockSpec((tile, width), lambda i: (i, 0))


def _resident(shape):
    zeros = (0,) * len(shape)
    return pl.BlockSpec(shape, lambda *_: zeros, pipeline_mode=pl.Buffered(1))


def _nt(a, b):
    return lax.dot_general(a, b, (((1,), (1,)), ((), ())), preferred_element_type=f32)


def _tn(a, b):
    return lax.dot_general(a, b, (((0,), (0,)), ((), ())), preferred_element_type=f32)


def _mm(a, b):
    return jnp.dot(a, b, preferred_element_type=f32)


def _norm_fwd(z):
    mu = jnp.mean(z, axis=-1, keepdims=True)
    zc = z - mu
    var = jnp.mean(zc * zc, axis=-1, keepdims=True)
    rstd = lax.rsqrt(var + LN_EPS)
    return zc * rstd, rstd


def _norm_bwd(dxhat, xhat, rstd):
    m1 = jnp.mean(dxhat, axis=-1, keepdims=True)
    m2 = jnp.mean(dxhat * xhat, axis=-1, keepdims=True)
    return rstd * (dxhat - m1 - xhat * m2)


def _proj(x, w_in_b, after):
    s = x.shape[0]

    def body(x_ref, w_ref, after_ref, p_ref, xb_ref):
        xb = x_ref[...].astype(bf16)
        xb_ref[...] = xb
        p_ref[...] = _nt(xb, w_ref[...])

    return pl.pallas_call(
        body, grid=(s // TOKEN_TILE,),
        in_specs=[_rows(D_MODEL), _resident((IN_W, D_MODEL)), pl.BlockSpec(memory_space=pl.ANY)],
        out_specs=[_rows(IN_W), _rows(D_MODEL)],
        out_shape=[jax.ShapeDtypeStruct((s, IN_W), f32), jax.ShapeDtypeStruct((s, D_MODEL), bf16)],
        name="proj", compiler_params=_params(("arbitrary",)))(x, w_in_b, after)


def _row_ranges(parts):
    out, at = [], 0
    for p in parts:
        out.append((at, at + p.shape[1]))
        at += p.shape[1]
    return out


def _wo_ln1(mix, x, w_o_b, g, b):
    s = x.shape[0]
    n = len(mix)
    ranges = _row_ranges(mix)

    def body(*refs):
        m_refs = refs[:n]
        x_ref, w_ref, g_ref, b_ref, xhat_ref, rstd_ref, hb_ref = refs[n:]
        z = ALPHA * x_ref[...]
        for m_ref, (lo, hi) in zip(m_refs, ranges):
            z = z + _mm(m_ref[...], w_ref[lo:hi, :])
        xhat, rstd = _norm_fwd(z)
        xhat_ref[...] = xhat
        rstd_ref[...] = rstd
        hb_ref[...] = (xhat * g_ref[...] + b_ref[...]).astype(bf16)

    return pl.pallas_call(
        body, grid=(s // TOKEN_TILE,),
        in_specs=[_rows(m.shape[1]) for m in mix] + [_rows(D_MODEL), _resident((D_MODEL, D_MODEL)), _resident((1, D_MODEL)),
                                                     _resident((1, D_MODEL))],
        out_specs=[_rows(D_MODEL), _rows(1), _rows(D_MODEL)],
        out_shape=[jax.ShapeDtypeStruct((s, D_MODEL), f32), jax.ShapeDtypeStruct((s, 1), f32),
                   jax.ShapeDtypeStruct((s, D_MODEL), bf16)],
        name="wo_ln1", compiler_params=_params(("arbitrary",)))(*mix, x, w_o_b, g, b)


def _mlp_fwd(xhat1, g1, b1, w_gu_b, w_down_b, g2, b2, after):
    s = xhat1.shape[0]

    def body(xh_ref, g1_ref, b1_ref, wgu_ref, wd_ref, g2_ref, b2_ref, after_ref, gu_ref, xhat2_ref, rstd2_ref, y_ref):
        h = xh_ref[...] * g1_ref[...] + b1_ref[...]
        gu = _mm(h.astype(bf16), wgu_ref[...])
        gu_ref[...] = gu
        gate = gu[:, :D_FF]
        act = gate * jax.nn.sigmoid(gate) * gu[:, D_FF:]
        z = ALPHA * h + _mm(act.astype(bf16), wd_ref[...])
        xhat2, rstd2 = _norm_fwd(z)
        xhat2_ref[...] = xhat2
        rstd2_ref[...] = rstd2
        y_ref[...] = xhat2 * g2_ref[...] + b2_ref[...]

    vec = _resident((1, D_MODEL))
    return pl.pallas_call(
        body, grid=(s // TOKEN_TILE,),
        in_specs=[_rows(D_MODEL), vec, vec, _resident((D_MODEL, 2 * D_FF)), _resident((D_FF, D_MODEL)), vec, vec,
                  pl.BlockSpec(memory_space=pl.ANY)],
        out_specs=[_rows(2 * D_FF), _rows(D_MODEL), _rows(1), _rows(D_MODEL)],
        out_shape=[jax.ShapeDtypeStruct((s, 2 * D_FF), f32), jax.ShapeDtypeStruct((s, D_MODEL), f32),
                   jax.ShapeDtypeStruct((s, 1), f32), jax.ShapeDtypeStruct((s, D_MODEL), f32)],
        name="mlp_fwd", compiler_params=_params(("arbitrary",)))(xhat1, g1, b1, w_gu_b, w_down_b, g2, b2, after)


def _loss_head(y, target):
    s = y.shape[0]

    def body(y_ref, t_ref, dy_ref, sq_ref):
        @pl.when(pl.program_id(0) == 0)
        def _():
            sq_ref[...] = jnp.zeros_like(sq_ref)

        e = y_ref[...] - t_ref[...]
        dy_ref[...] = e * (1.0 / D_MODEL)
        sq_ref[...] += jnp.sum(e * e, axis=0, keepdims=True)

    return pl.pallas_call(
        body, grid=(s // TOKEN_TILE,),
        in_specs=[_rows(D_MODEL), _rows(D_MODEL)],
        out_specs=[_rows(D_MODEL), pl.BlockSpec((1, D_MODEL), lambda i: (0, 0))],
        out_shape=[jax.ShapeDtypeStruct((s, D_MODEL), f32), jax.ShapeDtypeStruct((1, D_MODEL), f32)],
        name="loss_head", compiler_params=_params(("arbitrary",)))(y, target)


def _mlp_bwd(dy, xhat2, rstd2, g2, gu, w_gu_b, w_down_b, xhat1, rstd1, g1, w_o_b, after):
    s = dy.shape[0]

    def body(dy_ref, xh_ref, rs_ref, g2_ref, gu_ref, wgu_ref, wd_ref, xh1_ref, rs1_ref, g1_ref, wo_ref, after_ref,
             dz_ref, act_ref, dgu_ref, dz1_ref, dz1b_ref, dm_ref, gg_ref, gb_ref, gg1_ref, gb1_ref):
        @pl.when(pl.program_id(0) == 0)
        def _():
            for ref in (gg_ref, gb_ref, gg1_ref, gb1_ref):
                ref[...] = jnp.zeros_like(ref)

        dy_t = dy_ref[...]
        xhat = xh_ref[...]
        gg_ref[...] += jnp.sum(dy_t * xhat, axis=0, keepdims=True)
        gb_ref[...] += jnp.sum(dy_t, axis=0, keepdims=True)
        dz = _norm_bwd(dy_t * g2_ref[...], xhat, rs_ref[...])
        dzb = dz.astype(bf16)
        dz_ref[...] = dzb
        dact = _nt(dzb, wd_ref[...])
        gate = gu_ref[:, :D_FF]
        up = gu_ref[:, D_FF:]
        sg = jax.nn.sigmoid(gate)
        silu = gate * sg
        act_ref[...] = (silu * up).astype(bf16)
        dgu_ref[:, :D_FF] = (dact * up * (sg * (1.0 + gate * (1.0 - sg)))).astype(bf16)
        dgu_ref[:, D_FF:] = (dact * silu).astype(bf16)
        dh = ALPHA * dz + _nt(dgu_ref[...], wgu_ref[...])
        xhat1 = xh1_ref[...]
        gg1_ref[...] += jnp.sum(dh * xhat1, axis=0, keepdims=True)
        gb1_ref[...] += jnp.sum(dh, axis=0, keepdims=True)
        dz1 = _norm_bwd(dh * g1_ref[...], xhat1, rs1_ref[...])
        dz1_ref[...] = dz1
        dz1b = dz1.astype(bf16)
        dz1b_ref[...] = dz1b
        dm_ref[...] = _nt(dz1b, wo_ref[...])

    vec, vec_out = _resident((1, D_MODEL)), pl.BlockSpec((1, D_MODEL), lambda i: (0, 0))
    tokens_f32, tokens_bf16 = jax.ShapeDtypeStruct((s, D_MODEL), f32), jax.ShapeDtypeStruct((s, D_MODEL), bf16)
    sums = jax.ShapeDtypeStruct((1, D_MODEL), f32)
    return pl.pallas_call(
        body, grid=(s // TOKEN_TILE,),
        in_specs=[_rows(D_MODEL), _rows(D_MODEL), _rows(1), vec, _rows(2 * D_FF),
                  _resident((D_MODEL, 2 * D_FF)), _resident((D_FF, D_MODEL)), _rows(D_MODEL), _rows(1), vec,
                  _resident((D_MODEL, D_MODEL)), pl.BlockSpec(memory_space=pl.ANY)],
        out_specs=[_rows(D_MODEL), _rows(D_FF), _rows(2 * D_FF), _rows(D_MODEL), _rows(D_MODEL), _rows(D_MODEL),
                   vec_out, vec_out, vec_out, vec_out],
        out_shape=[tokens_bf16, jax.ShapeDtypeStruct((s, D_FF), bf16), jax.ShapeDtypeStruct((s, 2 * D_FF), bf16),
                   tokens_f32, tokens_bf16, tokens_f32, sums, sums, sums, sums],
        name="mlp_bwd", compiler_params=_params(("arbitrary",)))(
            dy, xhat2, rstd2, g2, gu, w_gu_b, w_down_b, xhat1, rstd1, g1, w_o_b, after)


def _dx(dz1, dparts, w_in_t, after):
    s = dz1.shape[0]
    n = len(dparts)
    ranges = _row_ranges(dparts)

    def body(*refs):
        d_refs = refs[:n]
        dz_ref, w_ref, _, dx_ref = refs[n:]
        acc = ALPHA * dz_ref[...]
        for d_ref, (lo, hi) in zip(d_refs, ranges):
            acc = acc + _mm(d_ref[...], w_ref[lo:hi, :])
        dx_ref[...] = acc

    return pl.pallas_call(
        body, grid=(s // TOKEN_TILE,),
        in_specs=[_rows(d.shape[1]) for d in dparts] + [_rows(D_MODEL), _resident((IN_W, D_MODEL)),
                                                        pl.BlockSpec(memory_space=pl.ANY)],
        out_specs=_rows(D_MODEL),
        out_shape=jax.ShapeDtypeStruct((s, D_MODEL), f32),
        name="dx", compiler_params=_params(("arbitrary",)))(*dparts, dz1, w_in_t, after)


def _weight_grad_rows(parts, b, bn):
    s, n_cols = b.shape
    n = len(parts)
    ranges = _row_ranges(parts)
    m = ranges[-1][1]

    def body(*refs):
        p_refs = refs[:n]
        b_ref, o_ref = refs[n:]
        for p_ref, (lo, hi) in zip(p_refs, ranges):
            o_ref[lo:hi, :] = _tn(p_ref[...], b_ref[...]).astype(bf16)

    return pl.pallas_call(
        body, grid=(n_cols // bn,),
        in_specs=[_resident(p.shape) for p in parts] + [pl.BlockSpec((s, bn), lambda j: (0, j))],
        out_specs=pl.BlockSpec((m, bn), lambda j: (0, j)),
        out_shape=jax.ShapeDtypeStruct((m, n_cols), bf16),
        name="weight_grad_rows", compiler_params=_params(("arbitrary",)))(*parts, b)


def _weight_grad(a, b, bm, bn, after):
    s, m = a.shape
    n = b.shape[1]

    def body(a_ref, b_ref, after_ref, o_ref):
        o_ref[...] = _tn(a_ref[...], b_ref[...]).astype(bf16)

    return pl.pallas_call(
        body, grid=(m // bm, n // bn),
        in_specs=[pl.BlockSpec((s, bm), lambda i, j: (0, i)), pl.BlockSpec((s, bn), lambda i, j: (0, j)),
                  pl.BlockSpec(memory_space=pl.ANY)],
        out_specs=pl.BlockSpec((bm, bn), lambda i, j: (i, j)),
        out_shape=jax.ShapeDtypeStruct((m, n), bf16),
        name="weight_grad", compiler_params=_params(("arbitrary", "arbitrary")))(a, b, after)


def _shift_down(a, k):
    row = lax.broadcasted_iota(jnp.int32, a.shape, 0)
    return jnp.where(row >= k, pltpu.roll(a, k, 0), 0.0)


def _shift_up(a, k):
    n = a.shape[0]
    row = lax.broadcasted_iota(jnp.int32, a.shape, 0)
    return jnp.where(row < n - k, pltpu.roll(a, n - k, 0), 0.0)


def _slab(s, block):
    return pl.BlockSpec((s, LANES), lambda k: (0, block + k))


def _conv_y(z, w):
    return w[0:1, :] * _shift_down(z, 2) + w[1:2, :] * _shift_down(z, 1) + w[2:3, :] * z


def _conv_fwd(proj, w_conv):
    s = proj.shape[0]

    def body(xa_ref, gb_ref, gc_ref, w_ref, o_ref):
        z = gc_ref[...] * xa_ref[...]
        o_ref[...] = (gb_ref[...] * _conv_y(z, w_ref[...])).astype(bf16)

    return pl.pallas_call(
        body, grid=(CONV_W // LANES,),
        in_specs=[_slab(s, BLK_XA), _slab(s, BLK_GB), _slab(s, BLK_GC), pl.BlockSpec((3, LANES), lambda k: (0, k))],
        out_specs=_slab(s, 0),
        out_shape=jax.ShapeDtypeStruct((s, CONV_W), bf16),
        name="conv_fwd", compiler_params=_params(("arbitrary",)))(proj, proj, proj, w_conv)


def _conv_bwd(proj, dmix, w_conv, after):
    s = proj.shape[0]

    def body(xa_ref, gb_ref, gc_ref, dy_ref, w_ref, after_ref, dxa_ref, dgb_ref, dgc_ref, dw_ref):
        xa = xa_ref[...]
        gc = gc_ref[...]
        w = w_ref[...]
        z = gc * xa
        dya = dy_ref[...]
        dgb_ref[...] = (dya * _conv_y(z, w)).astype(bf16)
        dy = dya * gb_ref[...]
        dz = w[2:3, :] * dy + w[1:2, :] * _shift_up(dy, 1) + w[0:1, :] * _shift_up(dy, 2)
        dxa_ref[...] = (dz * gc).astype(bf16)
        dgc_ref[...] = (dz * xa).astype(bf16)
        dw_ref[0:1, :] = jnp.sum(dy * _shift_down(z, 2), axis=0, keepdims=True)
        dw_ref[1:2, :] = jnp.sum(dy * _shift_down(z, 1), axis=0, keepdims=True)
        dw_ref[2:3, :] = jnp.sum(dy * z, axis=0, keepdims=True)

    out = jax.ShapeDtypeStruct((s, CONV_W), bf16)
    return pl.pallas_call(
        body, grid=(CONV_W // LANES,),
        in_specs=[_slab(s, BLK_XA), _slab(s, BLK_GB), _slab(s, BLK_GC), _slab(s, 0), pl.BlockSpec((3, LANES), lambda k: (0, k)),
                  pl.BlockSpec(memory_space=pl.ANY)],
        out_specs=[_slab(s, 0), _slab(s, 0), _slab(s, 0), pl.BlockSpec((3, LANES), lambda k: (0, k))],
        out_shape=[out, out, out, jax.ShapeDtypeStruct((3, CONV_W), f32)],
        name="conv_bwd", compiler_params=_params(("arbitrary",)))(proj, proj, proj, dmix, w_conv, after)


def _pool_window(k):
    lane = lax.broadcasted_iota(jnp.int32, (1, LANES), 1)
    low = lane < HEAD
    first = k == 0
    wlen = jnp.where(low, jnp.where(first, POOL_WINDOWS[0], POOL_WINDOWS[2]), jnp.where(first, POOL_WINDOWS[1], POOL_WINDOWS[3]))
    return wlen, low, first


def _pool_diff(p, k):
    wlen, low, first = _pool_window(k)
    s2 = p + _shift_down(p, 1)
    s4 = s2 + _shift_down(s2, 2)
    s8 = s4 + _shift_down(s4, 4)
    s16 = s8 + _shift_down(s8, 8)
    win = jnp.where(low, jnp.where(first, s2, s8), jnp.where(first, s4, s16))
    row = lax.broadcasted_iota(jnp.int32, p.shape, 0)
    count = jnp.minimum(row + 1, wlen).astype(f32)
    return win / count - p, count


def _pool_weight(w_ref):
    zero = jnp.zeros((HEAD, HEAD), f32)
    top = jnp.concatenate([w_ref[0], zero], axis=1)
    bottom = jnp.concatenate([zero, w_ref[1]], axis=1)
    return jnp.concatenate([top, bottom], axis=0).astype(bf16)


def _pool_fwd(proj, w_pool, pool_scale):
    s = proj.shape[0]

    def body(p_ref, w_ref, sc_ref, o_ref):
        d, _ = _pool_diff(p_ref[...], pl.program_id(0))
        o_ref[...] = (_mm(d.astype(bf16), _pool_weight(w_ref)) * sc_ref[...]).astype(bf16)

    return pl.pallas_call(
        body, grid=(POOL_W // LANES,),
        in_specs=[_slab(s, BLK_P), pl.BlockSpec((2, HEAD, HEAD), lambda k: (k, 0, 0)), pl.BlockSpec((1, LANES), lambda k: (0, k))],
        out_specs=_slab(s, 0),
        out_shape=jax.ShapeDtypeStruct((s, POOL_W), bf16),
        name="pool_fwd", compiler_params=_params(("arbitrary",)))(proj, w_pool, pool_scale)


def _pool_bwd(proj, dmix, w_pool, pool_scale):
    s = proj.shape[0]

    def body(p_ref, dy_ref, w_ref, sc_ref, dp_ref, dw_ref, dsc_ref):
        k = pl.program_id(0)
        d, count = _pool_diff(p_ref[...], k)
        wbd = _pool_weight(w_ref)
        db = d.astype(bf16)
        dyb = dy_ref[...]
        dsc_ref[...] = jnp.sum(dyb * _mm(db, wbd), axis=0, keepdims=True)
        dpre = (dyb * sc_ref[...]).astype(bf16)
        dwbd = _tn(db, dpre)
        dw_ref[0] = dwbd[:HEAD, :HEAD]
        dw_ref[1] = dwbd[HEAD:, HEAD:]
        dd = _nt(dpre, wbd)
        e = dd / count
        wlen, low, first = _pool_window(k)
        a2 = e + _shift_up(e, 1)
        a4 = a2 + _shift_up(a2, 2)
        a8 = a4 + _shift_up(a4, 4)
        a16 = a8 + _shift_up(a8, 8)
        back = jnp.where(low, jnp.where(first, a2, a8), jnp.where(first, a4, a16))
        dp_ref[...] = (back - dd).astype(bf16)

    return pl.pallas_call(
        body, grid=(POOL_W // LANES,),
        in_specs=[_slab(s, BLK_P), _slab(s, CONV_W // LANES), pl.BlockSpec((2, HEAD, HEAD), lambda k: (k, 0, 0)),
                  pl.BlockSpec((1, LANES), lambda k: (0, k))],
        out_specs=[_slab(s, 0), pl.BlockSpec((2, HEAD, HEAD), lambda k: (k, 0, 0)), pl.BlockSpec((1, LANES), lambda k: (0, k))],
        out_shape=[jax.ShapeDtypeStruct((s, POOL_W), bf16), jax.ShapeDtypeStruct((4, HEAD, HEAD), f32),
                   jax.ShapeDtypeStruct((1, POOL_W), f32)],
        name="pool_bwd", compiler_params=_params(("arbitrary",)))(proj, dmix, w_pool, pool_scale)


SGU_UNROLL = 4
INV_SQRT2 = 0.7071067811865476
INV_SQRT_2PI = 0.3989422804014327


def _gelu(x):
    return 0.5 * x * (1.0 + lax.erf(x * INV_SQRT2))


def _gelu_grad(x):
    return 0.5 * (1.0 + lax.erf(x * INV_SQRT2)) + x * (INV_SQRT_2PI * jnp.exp(-0.5 * x * x))


def _head_mean(a, low):
    s_low = jnp.sum(jnp.where(low, a, 0.0), axis=-1, keepdims=True)
    s_high = jnp.sum(jnp.where(low, 0.0, a), axis=-1, keepdims=True)
    return jnp.where(low, s_low, s_high) * (1.0 / HEAD)


def _tril():
    r = lax.broadcasted_iota(jnp.int32, (CHUNK, CHUNK), 0)
    c = lax.broadcasted_iota(jnp.int32, (CHUNK, CHUNK), 1)
    return r >= c


def _sgu_chunk(up, vp, g, wm0, wm1, b0, b1, low):
    ug = _gelu(up)
    vg = _gelu(vp)
    vc = vg - _head_mean(vg, low)
    rstd = lax.rsqrt(_head_mean(vc * vc, low) + LN_EPS)
    vn = vc * rstd
    vb = (vn * g).astype(bf16)
    mixed = jnp.where(low, _mm(wm0, vb) + b0, _mm(wm1, vb) + b1)
    return ug, vn, rstd, vb, mixed


def _sgu_specs(s):
    return [_slab(s, BLK_U), _slab(s, BLK_V), pl.BlockSpec((1, LANES), lambda k: (0, k)),
            pl.BlockSpec((2, CHUNK, CHUNK), lambda k: (k, 0, 0)), pl.BlockSpec((2, CHUNK, 1), lambda k: (k, 0, 0))]


def _sgu_fwd(proj, sgu_g, w_spatial, b_spatial3):
    s = proj.shape[0]

    def body(u_ref, v_ref, g_ref, w_ref, b_ref, o_ref):
        low = lax.broadcasted_iota(jnp.int32, (1, LANES), 1) < HEAD
        mask = _tril()
        wm0 = jnp.where(mask, w_ref[0], 0.0).astype(bf16)
        wm1 = jnp.where(mask, w_ref[1], 0.0).astype(bf16)
        g = g_ref[...]
        b0 = b_ref[0]
        b1 = b_ref[1]

        def chunk(n, carry):
            rows = pl.ds(pl.multiple_of(n * CHUNK, CHUNK), CHUNK)
            ug, _, _, _, mixed = _sgu_chunk(u_ref[rows, :], v_ref[rows, :], g, wm0, wm1, b0, b1, low)
            o_ref[rows, :] = (ug * mixed).astype(bf16)
            return carry

        lax.fori_loop(0, s // CHUNK, chunk, 0, unroll=SGU_UNROLL)

    return pl.pallas_call(
        body, grid=(SGU_W // LANES,),
        in_specs=_sgu_specs(s),
        out_specs=_slab(s, 0),
        out_shape=jax.ShapeDtypeStruct((s, SGU_W), bf16),
        name="sgu_fwd", compiler_params=_params(("arbitrary",)))(proj, proj, sgu_g, w_spatial, b_spatial3)


def _sgu_bwd(proj, dmix, sgu_g, w_spatial, b_spatial3):
    s = proj.shape[0]

    def body(u_ref, v_ref, g_ref, w_ref, b_ref, dy_ref, du_ref, dv_ref, dg_ref, dw_ref, db_ref):
        low = lax.broadcasted_iota(jnp.int32, (1, LANES), 1) < HEAD
        mask = _tril()
        w0 = jnp.where(mask, w_ref[0], 0.0)
        w1 = jnp.where(mask, w_ref[1], 0.0)
        wm0 = w0.astype(bf16)
        wm1 = w1.astype(bf16)
        wt0 = w0.T.astype(bf16)
        wt1 = w1.T.astype(bf16)
        g = g_ref[...]
        b0 = b_ref[0]
        b1 = b_ref[1]
        dg_ref[...] = jnp.zeros_like(dg_ref)
        dw_ref[...] = jnp.zeros_like(dw_ref)
        db_ref[...] = jnp.zeros_like(db_ref)

        def chunk(n, carry):
            rows = pl.ds(pl.multiple_of(n * CHUNK, CHUNK), CHUNK)
            up = u_ref[rows, :]
            vp = v_ref[rows, :]
            ug, vn, rstd, vb, mixed = _sgu_chunk(up, vp, g, wm0, wm1, b0, b1, low)
            dy = dy_ref[rows, :]
            du_ref[rows, :] = (dy * mixed * _gelu_grad(up)).astype(bf16)
            dmix_c = dy * ug
            db_ref[0] += jnp.sum(jnp.where(low, dmix_c, 0.0), axis=-1, keepdims=True)
            db_ref[1] += jnp.sum(jnp.where(low, 0.0, dmix_c), axis=-1, keepdims=True)
            dmb = dmix_c.astype(bf16)
            zero = jnp.zeros_like(dmb)
            dw_ref[0] += _nt(jnp.where(low, dmb, zero), vb)
            dw_ref[1] += _nt(jnp.where(low, zero, dmb), vb)
            dvnorm = jnp.where(low, _mm(wt0, dmb), _mm(wt1, dmb))
            dg_ref[...] += jnp.sum(dvnorm * vn, axis=0, keepdims=True)
            dvn = dvnorm * g
            dvg = rstd * (dvn - _head_mean(dvn, low) - vn * _head_mean(dvn * vn, low))
            dv_ref[rows, :] = (dvg * _gelu_grad(vp)).astype(bf16)
            return carry

        lax.fori_loop(0, s // CHUNK, chunk, 0, unroll=SGU_UNROLL)
        dw_ref[0] = jnp.where(mask, dw_ref[0], 0.0)
        dw_ref[1] = jnp.where(mask, dw_ref[1], 0.0)

    out = jax.ShapeDtypeStruct((s, SGU_W), bf16)
    return pl.pallas_call(
        body, grid=(SGU_W // LANES,),
        in_specs=_sgu_specs(s) + [_slab(s, (CONV_W + POOL_W) // LANES)],
        out_specs=[_slab(s, 0), _slab(s, 0), pl.BlockSpec((1, LANES), lambda k: (0, k)),
                   pl.BlockSpec((2, CHUNK, CHUNK), lambda k: (k, 0, 0)), pl.BlockSpec((2, CHUNK, 1), lambda k: (k, 0, 0))],
        out_shape=[out, out, jax.ShapeDtypeStruct((1, SGU_W), f32), jax.ShapeDtypeStruct((6, CHUNK, CHUNK), f32),
                   jax.ShapeDtypeStruct((6, CHUNK, 1), f32)],
        name="sgu_bwd", compiler_params=_params(("arbitrary",)))(proj, proj, sgu_g, w_spatial, b_spatial3, dmix)


def _fwd_mix(x, w, after):
    proj, xb = _proj(x, w["w_in"], after)
    mix = [_conv_fwd(proj, w["w_conv"]), _pool_fwd(proj, w["w_pool"], w["pool_scale"]),
           _sgu_fwd(proj, w["sgu_ln_g"], w["w_spatial"], w["b_spatial"])]
    xhat1, rstd1, hb = _wo_ln1(mix, x, w["w_o"], w["ln1_g"], w["ln1_b"])
    return dict(proj=proj, xb=xb, mix=mix, xhat1=xhat1, rstd1=rstd1, hb=hb)


def _fwd_mlp(sv, w, after):
    gu, xhat2, rstd2, y = _mlp_fwd(sv["xhat1"], w["ln1_g"], w["ln1_b"], w["w_gate_up"], w["w_down"], w["ln2_g"], w["ln2_b"], after)
    sv.update(gu=gu, xhat2=xhat2, rstd2=rstd2)
    return y


def _bwd_mlp(dy, w, sv, after, hook):
    dz2b, actb, dgub, dz1, dz1b, dmix, g_ln2_g, g_ln2_b, g_ln1_g, g_ln1_b = _mlp_bwd(
        dy, sv["xhat2"], sv["rstd2"], w["ln2_g"], sv["gu"], w["w_gate_up"], w["w_down"], sv["xhat1"], sv["rstd1"], w["ln1_g"],
        w["w_o"], after)
    after = hook(dz1)
    grads = dict(w_gate_up=_weight_grad(sv["hb"], dgub, 512, D_FF // 2, after),
                 w_down=_weight_grad(actb, dz2b, D_FF // 2, D_MODEL, after),
                 ln2_g=g_ln2_g, ln2_b=g_ln2_b, ln1_g=g_ln1_g, ln1_b=g_ln1_b)
    return (dz1, dz1b, dmix), grads


def _bwd_mix(dz, w, sv, after, hook):
    dz1, dz1b, dmix = dz
    dxa, dgb, dgc, g_conv = _conv_bwd(sv["proj"], dmix, w["w_conv"], after)
    dp, g_pool, g_pscale = _pool_bwd(sv["proj"], dmix, w["w_pool"], w["pool_scale"])
    du, dv, g_sgu_g, g_spatial, g_bsp = _sgu_bwd(sv["proj"], dmix, w["sgu_ln_g"], w["w_spatial"], w["b_spatial"])
    dparts = [dxa, dgb, dgc, dp, du, dv]
    dx = _dx(dz1, dparts, w["w_in"], hook(du))
    grads = dict(
        w_in=_weight_grad_rows(dparts, sv["xb"], 512), w_o=_weight_grad_rows(sv["mix"], dz1b, D_MODEL),
        w_conv=g_conv, w_pool=g_pool, pool_scale=g_pscale, sgu_ln_g=g_sgu_g, w_spatial=g_spatial,
        b_spatial=g_bsp.reshape(6, CHUNK))
    return dx, grads


def _local_step(x, target, layers):
    saved = []
    for w in layers:
        sv = _fwd_mix(x, w, x)
        x = _fwd_mlp(sv, w, x)
        saved.append(sv)
    dy, sq = _loss_head(x, target)
    grads = [None] * len(layers)
    for l in reversed(range(len(layers))):
        dz, g_mlp = _bwd_mlp(dy, layers[l], saved[l], sq, lambda a: a)
        dy, g_mix = _bwd_mix(dz, layers[l], saved[l], dz[0], lambda a: a)
        grads[l] = dict(g_mlp, **g_mix)
    return sq, dy, grads


ANY = pl.BlockSpec(memory_space=pl.ANY)


def _place():
    x, y, c = lax.axis_index("x"), lax.axis_index("y"), lax.axis_index("c")
    others = [(1 - x, y), (x, 1 - y), (1 - x, 1 - y)]
    return x, y, c, others


def _chip_index(cx, cy):
    return 2 * cx + cy


def _half(ref_rows, c):
    half = ref_rows // 2
    return pl.ds(pl.multiple_of(c * half, 8), half)


def _remote(src, dst, send_sem, recv_sem, device):
    return pltpu.make_async_remote_copy(src_ref=src, dst_ref=dst, send_sem=send_sem, recv_sem=recv_sem,
                                        device_id=device, device_id_type=MESH)


def _gather_shards(shards):
    n = len(shards)
    base, total = [], 0
    for s in shards:
        base.append(total)
        total += 6 * s.shape[0]

    def body(*refs):
        ins, outs = refs[:n], refs[n:2 * n]
        send, recv = refs[2 * n:]
        x, y, c, others = _place()
        me = _chip_index(x, y)
        sib = (x, y, 1 - c)
        sends = []
        for f in range(n):
            depth, rows = ins[f].shape[0], ins[f].shape[1]
            for l in range(depth):
                for k, (cx, cy) in enumerate(others):
                    sem = base[f] + 6 * l + k
                    cp = _remote(ins[f].at[l, _half(rows, c)], outs[f].at[l, me, _half(rows, c)],
                                 send.at[sem], recv.at[sem], (cx, cy, c))
                    cp.start()
                    sends.append(cp)
        for f in range(n):
            depth, rows = ins[f].shape[0], ins[f].shape[1]
            for l in range(depth):
                for k, (cx, cy) in enumerate(others):
                    sem = base[f] + 6 * l + k
                    landed = outs[f].at[l, _chip_index(cx, cy), _half(rows, c)]
                    _remote(landed, landed, send.at[sem], recv.at[sem], (cx, cy, c)).wait_recv()
                    cp = _remote(landed, landed, send.at[sem + 3], recv.at[sem + 3], sib)
                    cp.start()
                    sends.append(cp)
        for f in range(n):
            depth, rows = ins[f].shape[0], ins[f].shape[1]
            for l in range(depth):
                for k, (cx, cy) in enumerate(others):
                    sem = base[f] + 6 * l + k + 3
                    passed = outs[f].at[l, _chip_index(cx, cy), _half(rows, 1 - c)]
                    _remote(passed, passed, send.at[sem], recv.at[sem], sib).wait_recv()
        for cp in sends:
            cp.wait_send()

    gathered = pl.pallas_call(
        body, in_specs=[ANY] * n, out_specs=[ANY] * n,
        out_shape=[jax.ShapeDtypeStruct((s.shape[0], N_CHIPS) + s.shape[1:], s.dtype) for s in shards],
        scratch_shapes=[pltpu.SemaphoreType.DMA((total,)), pltpu.SemaphoreType.DMA((total,))],
        name="gather_shards")(*shards)
    return [_place_own(g, s) for g, s in zip(gathered, shards)]


def _scalar(value):
    return jnp.reshape(value, (1,)).astype(jnp.int32)


def _place_own(blocks, shard):
    depth, rows, cols = shard.shape

    def body(me_ref, b_ref, s_ref, o_ref):
        o_ref[...] = s_ref[...]

    return pl.pallas_call(
        body,
        grid_spec=pltpu.PrefetchScalarGridSpec(
            num_scalar_prefetch=1, grid=(depth,),
            in_specs=[ANY, pl.BlockSpec((None, rows, cols), lambda l, me: (l, 0, 0))],
            out_specs=pl.BlockSpec((None, None, rows, cols), lambda l, me: (l, me[0], 0, 0))),
        out_shape=jax.ShapeDtypeStruct(blocks.shape, blocks.dtype),
        input_output_aliases={1: 0},
        name="place_own", compiler_params=_params(("arbitrary",)))(
            _scalar(_chip_index(lax.axis_index("x"), lax.axis_index("y"))), blocks, shard)


HBM = pl.BlockSpec(memory_space=pltpu.HBM)
SEM = pl.BlockSpec(memory_space=pltpu.SEMAPHORE)
TOKEN = jax.ShapeDtypeStruct((8, LANES), f32)
SPLIT_COPY = pltpu.CompilerParams(has_side_effects=pltpu.SideEffectType.DATAFLOW_SIDE_EFFECTING)


def _in_hbm(a):
    return pltpu.with_memory_space_constraint(a, pltpu.HBM)


def _full_shape(shard, axis):
    rows, cols = shard.shape
    return (N_CHIPS * rows, cols) if axis == 0 else (rows, N_CHIPS * cols)


def _block_half(ref, axis, j, h):
    if axis == 0:
        rows = ref.shape[0] // N_CHIPS
        return ref.at[pl.ds(pl.multiple_of(j * rows + h * (rows // 2), 16), rows // 2), :]
    half, cols = ref.shape[0] // 2, ref.shape[1] // N_CHIPS
    return ref.at[pl.ds(pl.multiple_of(h * half, 16), half), pl.ds(pl.multiple_of(j * cols, LANES), cols)]


def _place_layer(shards, axes, after):
    n = len(shards)

    def body(me_ref, *refs):
        ins, outs = refs[n:2 * n], refs[2 * n + 1:]
        for f in range(n):
            outs[f][...] = ins[f][...]

    lands = [lax.empty(_full_shape(s, ax), s.dtype) for s, ax in zip(shards, axes)]
    return pl.pallas_call(
        body,
        grid_spec=pltpu.PrefetchScalarGridSpec(
            num_scalar_prefetch=1, grid=(1,),
            in_specs=[ANY] * n + [pl.BlockSpec(s.shape, lambda i, me: (0, 0)) for s in shards] + [ANY],
            out_specs=[pl.BlockSpec(s.shape, (lambda i, me: (me[0], 0)) if ax == 0 else (lambda i, me: (0, me[0])))
                       for s, ax in zip(shards, axes)]),
        out_shape=[jax.ShapeDtypeStruct(a.shape, a.dtype) for a in lands],
        input_output_aliases={1 + f: f for f in range(n)},
        name="place_layer", compiler_params=_params(("arbitrary",)))(
            _scalar(_chip_index(lax.axis_index("x"), lax.axis_index("y"))), *lands, *shards, after)


def _gather_start(shards, lands, axes, after):
    return _split_copy_start("gather", _gather_plan(axes), 3 * len(shards), shards, lands, after)


def _gather_wait(state, axes, after):
    return _split_copy_wait("gather", _gather_plan(axes), state, after)


def _split_copy_start(name, plan, count, ins, lands, after):
    arrays = list(ins) + list(lands)
    n_in, n = len(ins), len(arrays)

    def body(*refs):
        send, recv, token = refs[n + 1], refs[n + 2], refs[-1]
        for i, (src, dst, _, peer) in enumerate(plan(refs[:n_in], refs[n_in:n])):
            _remote(src, dst, send.at[i], recv.at[i], peer).start()
        token[...] = jnp.zeros_like(token)

    outs = pl.pallas_call(
        body, name=name + "_start",
        in_specs=[HBM] * n + [ANY],
        out_specs=(SEM, SEM, *[HBM] * n, pl.BlockSpec(memory_space=pltpu.VMEM)),
        out_shape=(pltpu.SemaphoreType.DMA((count,)), pltpu.SemaphoreType.DMA((count,)),
                   *[pltpu.HBM(a.shape, a.dtype) for a in arrays], TOKEN),
        input_output_aliases={i: 2 + i for i in range(n)},
        compiler_params=SPLIT_COPY)(*[_in_hbm(a) for a in arrays], after)
    return (outs[0], outs[1], outs[2:2 + n_in], outs[2 + n_in:2 + n]), outs[-1]


def _split_copy_wait(name, plan, state, after):
    send_sems, recv_sems, ins, lands = state
    arrays = list(ins) + list(lands)
    n_in, n = len(ins), len(arrays)

    def body(*refs):
        send, recv, token = refs[n], refs[n + 1], refs[-1]
        for i, (src, _, landing, peer) in enumerate(plan(refs[:n_in], refs[n_in:n])):
            cp = _remote(src, landing, send.at[i], recv.at[i], peer)
            cp.wait_send()
            cp.wait_recv()
        token[...] = jnp.zeros_like(token)

    outs = pl.pallas_call(
        body, name=name + "_wait",
        in_specs=[HBM] * n + [SEM, SEM, ANY],
        out_specs=(*[HBM] * n, pl.BlockSpec(memory_space=pltpu.VMEM)),
        out_shape=(*[pltpu.HBM(a.shape, a.dtype) for a in arrays], TOKEN),
        input_output_aliases={i: i for i in range(n)},
        compiler_params=SPLIT_COPY)(*arrays, send_sems, recv_sems, after)
    return outs[:n_in], outs[n_in:n], outs[-1]


def _gather_plan(axes):
    def plan(ins, lnd):
        x, y, c, others = _place()
        me = _chip_index(x, y)
        return [(ins[f].at[_half(ins[f].shape[0], c)], _block_half(lnd[f], ax, me, c),
                 _block_half(lnd[f], ax, _chip_index(cx, cy), c), (cx, cy, c))
                for f, ax in enumerate(axes) for cx, cy in others]
    return plan


def _pair_plan(axes):
    def plan(ins, lnd):
        x, y, c, _ = _place()
        return [(_block_half(ins[f], ax, j, 1 - c), lnd[f].at[j], lnd[f].at[j], (x, y, 1 - c))
                for f, ax in enumerate(axes) for j in range(N_CHIPS)]
    return plan


def _scatter_plan(ins, lnd):
    x, y, c, others = _place()
    return [(ins[f].at[_chip_index(cx, cy)], lnd[f].at[k], lnd[f].at[k], (cx, cy, c))
            for f in range(len(ins)) for k, (cx, cy) in enumerate(others)]


def _join_plan(ins, lnd):
    x, y, c, _ = _place()
    return [(lnd[f].at[_half(lnd[f].shape[0], c)], lnd[f].at[_half(lnd[f].shape[0], c)],
             lnd[f].at[_half(lnd[f].shape[0], 1 - c)], (x, y, 1 - c)) for f in range(len(lnd))]


def _gather_finish(lands, axes):
    n = len(lands)

    def body(*refs):
        outs = refs[n:2 * n]
        send, recv = refs[2 * n:]
        x, y, c, others = _place()
        sib = (x, y, 1 - c)
        sends = []
        for f in range(n):
            for k, (cx, cy) in enumerate(others):
                landed = _block_half(outs[f], axes[f], _chip_index(cx, cy), c)
                cp = _remote(landed, landed, send.at[3 * f + k], recv.at[3 * f + k], sib)
                cp.start()
                sends.append(cp)
        for f in range(n):
            for k, (cx, cy) in enumerate(others):
                passed = _block_half(outs[f], axes[f], _chip_index(cx, cy), 1 - c)
                _remote(passed, passed, send.at[3 * f + k], recv.at[3 * f + k], sib).wait_recv()
        for cp in sends:
            cp.wait_send()

    return pl.pallas_call(
        body, in_specs=[ANY] * n, out_specs=[ANY] * n,
        out_shape=[jax.ShapeDtypeStruct(a.shape, a.dtype) for a in lands],
        input_output_aliases={f: f for f in range(n)},
        scratch_shapes=[pltpu.SemaphoreType.DMA((3 * n,)), pltpu.SemaphoreType.DMA((3 * n,))],
        name="gather_finish")(*lands)


def _half_blocks(part, axis):
    rows, cols = (part.shape[0] // N_CHIPS, part.shape[1]) if axis == 0 else (part.shape[0], part.shape[1] // N_CHIPS)
    return lax.empty((N_CHIPS, rows // 2, cols), part.dtype)


def _add_pair_layer(parts, gots, axes):
    k = len(parts)

    def body(c_ref, *refs):
        for f in range(k):
            a_ref, b_ref, o_ref = refs[2 * f], refs[2 * f + 1], refs[2 * k + f]
            o_ref[...] = (a_ref[...].astype(f32) + b_ref[...].astype(f32)).astype(o_ref.dtype)

    in_specs, out_specs, operands = [], [], []
    for part, got, axis in zip(parts, gots, axes):
        _, half, cols = got.shape
        if axis == 0:
            part = part.reshape(N_CHIPS, 2, half, cols)
            mine = pl.BlockSpec((None, None, half, cols), lambda j, c: (j, c[0], 0, 0))
        else:
            mine = pl.BlockSpec((half, cols), lambda j, c: (c[0], j))
        block = pl.BlockSpec((None, half, cols), lambda j, c: (j, 0, 0))
        in_specs += [mine, block]
        out_specs.append(block)
        operands += [part, got]
    return pl.pallas_call(
        body,
        grid_spec=pltpu.PrefetchScalarGridSpec(num_scalar_prefetch=1, grid=(N_CHIPS,), in_specs=in_specs, out_specs=out_specs),
        out_shape=[jax.ShapeDtypeStruct(g.shape, p.dtype) for p, g in zip(parts, gots)],
        name="add_pair_layer", compiler_params=_params(("arbitrary",)))(_scalar(lax.axis_index("c")), *operands)


def _scatter_start(sums, after):
    lands = [lax.empty((3,) + s.shape[1:], s.dtype) for s in sums]
    return _split_copy_start("scatter", _scatter_plan, 3 * len(sums), sums, lands, after)


def _scatter_wait(state, after):
    return _split_copy_wait("scatter", _scatter_plan, state, after)


ELEMENTWISE_BLOCK_BYTES = 1 << 20


def _row_tile(rows, cols):
    best = None
    for tile in range(8, rows + 1, 8):
        if rows % tile == 0 and tile * cols * 4 <= ELEMENTWISE_BLOCK_BYTES:
            best = tile
    return best or rows


def _add_slots(chip_sums, slots):
    k = len(chip_sums)

    def body(at_ref, *refs):
        for f in range(k):
            own_ref, s_ref, o_ref = refs[2 * f], refs[2 * f + 1], refs[2 * k + f]
            acc = own_ref[...].astype(f32)
            for j in range(3):
                acc = acc + s_ref[j].astype(f32)
            o_ref[...] = acc

    in_specs, out_specs, operands = [], [], []
    for cs, s in zip(chip_sums, slots):
        _, half, cols = cs.shape
        in_specs += [pl.BlockSpec((None, half, cols), lambda i, at: (at[0], 0, 0)), pl.BlockSpec((3, half, cols), lambda i, at: (0, 0, 0))]
        out_specs.append(pl.BlockSpec((None, half, cols), lambda i, at: (at[1], 0, 0)))
        operands += [cs, s]
    at = jnp.concatenate([_scalar(_chip_index(lax.axis_index("x"), lax.axis_index("y"))), _scalar(lax.axis_index("c"))])
    outs = pl.pallas_call(
        body,
        grid_spec=pltpu.PrefetchScalarGridSpec(num_scalar_prefetch=1, grid=(1,), in_specs=in_specs, out_specs=out_specs),
        out_shape=[jax.ShapeDtypeStruct((2,) + cs.shape[1:], f32) for cs in chip_sums],
        name="add_slots", compiler_params=_params(("arbitrary",)))(at, *operands)
    return [o.reshape(2 * o.shape[1], o.shape[2]) for o in outs]


def _adamw_math(w, grad, m, v):
    nm = ADAM_B1 * m + (1.0 - ADAM_B1) * grad
    nv = ADAM_B2 * v + (1.0 - ADAM_B2) * (grad * grad)
    m_hat = nm / (1.0 - ADAM_B1 ** ADAM_STEP)
    v_hat = nv / (1.0 - ADAM_B2 ** ADAM_STEP)
    return nm, nv, -ADAM_LR * (m_hat / (jnp.sqrt(v_hat) + ADAM_EPS) + ADAM_WD * w)


def _adamw(w, g, m, v):
    shape = w.shape
    flat = [a.reshape(-1, shape[-1]) for a in (w, g, m, v)]
    tile = _row_tile(flat[0].shape[0], shape[-1])

    def body(w_ref, g_ref, m_ref, v_ref, d_ref, nm_ref, nv_ref):
        nm, nv, step = _adamw_math(w_ref[...], g_ref[...], m_ref[...], v_ref[...])
        d_ref[...] = step
        nm_ref[...] = nm
        nv_ref[...] = nv

    spec = _rows(shape[-1], tile)
    out = jax.ShapeDtypeStruct(flat[0].shape, f32)
    res = pl.pallas_call(
        body, grid=(flat[0].shape[0] // tile,),
        in_specs=[spec] * 4, out_specs=[spec] * 3, out_shape=[out] * 3,
        name="adamw", compiler_params=_params(("arbitrary",)))(*flat)
    return [r.reshape(shape) for r in res]


def _adamw_layer(l, ws, ms, vs, gs, outs, steps, after):
    k = len(ws)

    def body(*refs):
        ins, new = refs[:4 * k], refs[8 * k + 1:]
        for f in range(k):
            w_ref, m_ref, v_ref, g_ref = ins[4 * f:4 * f + 4]
            go_ref, d_ref, nm_ref, nv_ref = new[4 * f:4 * f + 4]
            grad = g_ref[...]
            nm, nv, step = _adamw_math(w_ref[...], grad, m_ref[...], v_ref[...])
            go_ref[...] = grad
            d_ref[...] = step
            nm_ref[...] = nm
            nv_ref[...] = nv

    in_specs, out_specs, operands = [], [], []
    for w, m, v, g in zip(ws, ms, vs, gs):
        _, rows, cols = w.shape
        tile = rows // steps
        layer = pl.BlockSpec((None, tile, cols), lambda i: (l, i, 0))
        in_specs += [layer] * 3 + [_rows(cols, tile)]
        out_specs += [layer] * 4
        operands += [w, m, v, g]
    flat_outs = [o for four in outs for o in four]
    res = pl.pallas_call(
        body, grid=(steps,),
        in_specs=in_specs + [ANY] * (4 * k + 1), out_specs=out_specs,
        out_shape=[jax.ShapeDtypeStruct(o.shape, f32) for o in flat_outs],
        input_output_aliases={4 * k + j: j for j in range(4 * k)},
        name="adamw_layer", compiler_params=_params(("arbitrary",)))(*operands, *flat_outs, after)
    return [res[4 * f:4 * f + 4] for f in range(k)]


SMALL = ("w_conv", "w_pool", "pool_scale", "sgu_ln_g", "w_spatial", "b_spatial", "ln1_g", "ln1_b", "ln2_g", "ln2_b")
WEIGHTS = ("w_in", "w_conv", "w_pool", "pool_scale", "sgu_ln_g", "w_spatial", "b_spatial", "w_o", "ln1_g", "ln1_b",
           "w_gate_up", "w_down", "ln2_g", "ln2_b")
BIG = ("w_in", "w_o", "w_gate_up", "w_down")
GROUPS = (("w_in", "w_o"), ("w_gate_up", "w_down"))
GROUP_AXES = ((0, 0), (1, 0))
SCATTER_HOOKS = 2
ADAMW_STEPS = (2, 4)
SMALL_LAYER_ROWS = 1024


def _pack_layer(arrays):
    flat = jnp.concatenate([a.reshape(-1) for a in arrays])
    return jnp.pad(flat, (0, SMALL_LAYER_ROWS * LANES - flat.shape[0])).reshape(SMALL_LAYER_ROWS, LANES)


def _unpack_layers(flat, shapes):
    out, at = {}, 0
    for name, shape in shapes.items():
        size = 1
        for d in shape:
            size *= d
        out[name] = flat[:, at:at + size].reshape((flat.shape[0],) + tuple(shape))
        at += size
    return out


def kernel(x, w_in, w_conv, w_pool, pool_scale, sgu_ln_g, w_spatial, b_spatial, w_o, ln1_g, ln1_b, w_gate_up, w_down, ln2_g, ln2_b, loss_target, m_w_in, m_w_conv, m_w_pool, m_pool_scale, m_sgu_ln_g, m_w_spatial, m_b_spatial, m_w_o, m_ln1_g, m_ln1_b, m_w_gate_up, m_w_down, m_ln2_g, m_ln2_b, v_w_in, v_w_conv, v_w_pool, v_pool_scale, v_sgu_ln_g, v_w_spatial, v_b_spatial, v_w_o, v_ln1_g, v_ln1_b, v_w_gate_up, v_w_down, v_ln2_g, v_ln2_b):
    weights = dict(w_in=w_in, w_conv=w_conv, w_pool=w_pool, pool_scale=pool_scale, sgu_ln_g=sgu_ln_g, w_spatial=w_spatial,
                   b_spatial=b_spatial, w_o=w_o, ln1_g=ln1_g, ln1_b=ln1_b, w_gate_up=w_gate_up, w_down=w_down, ln2_g=ln2_g, ln2_b=ln2_b)
    m_in = dict(w_in=m_w_in, w_conv=m_w_conv, w_pool=m_w_pool, pool_scale=m_pool_scale, sgu_ln_g=m_sgu_ln_g, w_spatial=m_w_spatial,
                b_spatial=m_b_spatial, w_o=m_w_o, ln1_g=m_ln1_g, ln1_b=m_ln1_b, w_gate_up=m_w_gate_up, w_down=m_w_down,
                ln2_g=m_ln2_g, ln2_b=m_ln2_b)
    v_in = dict(w_in=v_w_in, w_conv=v_w_conv, w_pool=v_w_pool, pool_scale=v_pool_scale, sgu_ln_g=v_sgu_ln_g, w_spatial=v_w_spatial,
                b_spatial=v_b_spatial, w_o=v_w_o, ln1_g=v_ln1_g, ln1_b=v_ln1_b, w_gate_up=v_w_gate_up, w_down=v_w_down,
                ln2_g=v_ln2_g, ln2_b=v_ln2_b)
    depth = w_in.shape[0]
    conv_cols = w_conv.shape[2]
    chip = _chip_index(lax.axis_index("x"), lax.axis_index("y"))

    conv_flat = jnp.pad(w_conv.reshape(-1), (0, 16 * LANES - w_conv.size)).reshape(1, 16, LANES)
    conv_full = _gather_shards([conv_flat])[0].reshape(N_CHIPS, 16 * LANES)[:, :w_conv.size].reshape(N_CHIPS, depth, 3, conv_cols)
    conv_full = conv_full.transpose(1, 2, 0, 3).reshape(depth, 3, N_CHIPS * conv_cols)

    big_w = dict(w_in=jnp.swapaxes(w_in, 1, 2), w_o=w_o, w_gate_up=w_gate_up, w_down=w_down)
    big_m = dict(w_in=jnp.swapaxes(m_w_in, 1, 2), w_o=m_w_o, w_gate_up=m_w_gate_up, w_down=m_w_down)
    big_v = dict(w_in=jnp.swapaxes(v_w_in, 1, 2), w_o=v_w_o, w_gate_up=v_w_gate_up, w_down=v_w_down)

    def shards_of(l, g):
        return [big_w[n][l].astype(bf16) for n in GROUPS[g]]

    def send(l, g, after):
        return _gather_start(shards_of(l, g), placed[l, g], GROUP_AXES[g], after)

    def receive(g, flight, after):
        _, lands, token = _gather_wait(flight, GROUP_AXES[g], after)
        return _gather_finish(lands, GROUP_AXES[g]), token

    placed = {(0, 0): _place_layer(shards_of(0, 0), GROUP_AXES[0], conv_full)}
    flight, token = send(0, 0, conv_full)
    behind = conv_full
    for l, g in [(l, g) for l in range(depth) for g in (0, 1)][1:]:
        placed[l, g] = _place_layer(shards_of(l, g), GROUP_AXES[g], token)
        behind = placed[l, g][0]
    act = x[0]
    layers, saved = [], []
    for l in range(depth):
        w = dict(w_conv=conv_full[l], w_pool=w_pool[l], pool_scale=pool_scale[l][None], sgu_ln_g=sgu_ln_g[l][None],
                 w_spatial=w_spatial[l], b_spatial=b_spatial[l][:, :, None], ln1_g=ln1_g[l][None], ln1_b=ln1_b[l][None],
                 ln2_g=ln2_g[l][None], ln2_b=ln2_b[l][None])
        mats, token = receive(0, flight, behind if l == 0 else act)
        flight, token = send(l, 1, token)
        w.update(zip(GROUPS[0], mats))
        sv = _fwd_mix(act, w, token)
        mats, token = receive(1, flight, sv["xhat1"])
        if l + 1 < depth:
            flight, token = send(l + 1, 0, token)
        w.update(zip(GROUPS[1], mats))
        act = _fwd_mlp(sv, w, token)
        layers.append(w)
        saved.append(sv)

    big_outs = {n: [lax.empty(big_w[n].shape, f32) for _ in range(4)] for n in BIG}
    small_sums = [None] * depth
    pending, updates = [], []
    latest = dict(token=None)

    def begin(l, g, parts):
        axes = GROUP_AXES[g] + (0,) * (len(parts) - len(GROUPS[g]))
        lands = [_half_blocks(p, ax) for p, ax in zip(parts, axes)]
        flight, latest["token"] = _split_copy_start("pair", _pair_plan(axes), N_CHIPS * len(parts), parts, lands, latest["token"])
        pending.append(dict(l=l, g=g, axes=axes, step="pair", age=0, flight=flight))

    def advance(st, recent):
        if st["step"] == "pair":
            parts, got, _ = _split_copy_wait("pair", _pair_plan(st["axes"]), st["flight"], recent)
            sums = _add_pair_layer(parts, got, st["axes"])
            st["flight"], latest["token"] = _scatter_start(sums, latest["token"])
            st["step"] = "scatter"
        elif st["step"] == "scatter":
            sums, slots, _ = _scatter_wait(st["flight"], recent)
            filled = _add_slots(sums, slots)
            st["flight"], latest["token"] = _split_copy_start("join", _join_plan, len(filled), [], filled, latest["token"])
            st["step"] = "join"
        else:
            _, summed, _ = _split_copy_wait("join", _join_plan, st["flight"], recent)
            updates.append((st["l"], st["g"], summed[:len(GROUPS[st["g"]])]))
            if st["g"] == 0:
                small_sums[st["l"]] = summed[-1]
            st["step"] = "done"
        st["age"] = 0

    def hook(recent):
        for st in reversed(list(pending)):
            st["age"] += 1
            if st["age"] >= SCATTER_HOOKS or st["step"] != "scatter":
                advance(st, recent)
                if st["step"] == "done":
                    pending.remove(st)
        return latest["token"]

    def update(count, recent):
        for l, g, totals in updates[:count]:
            names = GROUPS[g]
            new = _adamw_layer(l, [big_w[n] for n in names], [big_m[n] for n in names], [big_v[n] for n in names], totals,
                               [big_outs[n] for n in names], ADAMW_STEPS[g], latest["token"])
            big_outs.update(zip(names, new))
            recent = new[-1][1]
        del updates[:count]
        return recent

    grad_x, sq = _loss_head(act, loss_target[0])
    latest["token"] = sq
    grads = [None] * depth
    for l in reversed(range(depth)):
        dz, g_mlp = _bwd_mlp(grad_x, layers[l], saved[l], latest["token"], hook)
        hook(g_mlp["w_down"])
        begin(l, 1, [g_mlp[n] for n in GROUPS[1]])
        grad_x, g_mix = _bwd_mix(dz, layers[l], saved[l], latest["token"], hook)
        grads[l] = dict(g_mlp, **g_mix)
        hook(g_mix["w_o"])
        begin(l, 0, [g_mix[n] for n in GROUPS[0]] + [_pack_layer([grads[l][n] for n in SMALL])])
    recent = g_mix["w_o"]
    while pending:
        recent = update(-(-len(updates) // 2), recent)
        hook(recent)
    update(len(updates), recent)
    loss = lax.psum(0.5 / D_MODEL * jnp.sum(sq), ("x", "y", "c"))

    small_sum = _gather_shards([jnp.stack(small_sums)])[0].reshape(depth, SMALL_LAYER_ROWS * LANES)
    grad = {n: [jnp.swapaxes(o, 1, 2) for o in big_outs[n]] if n == "w_in" else big_outs[n] for n in BIG}
    delta = {n: o[1] for n, o in grad.items()}
    new_m = {n: o[2] for n, o in grad.items()}
    new_v = {n: o[3] for n, o in grad.items()}
    grad = {n: o[0] for n, o in grad.items()}
    grad.update(_unpack_layers(small_sum, {n: (3, N_CHIPS * conv_cols) if n == "w_conv" else weights[n].shape[1:] for n in SMALL}))
    grad["w_conv"] = lax.dynamic_slice_in_dim(grad["w_conv"], chip * conv_cols, conv_cols, axis=2)

    delta["w_conv"], new_m["w_conv"], new_v["w_conv"] = _adamw(w_conv, grad["w_conv"], m_w_conv, v_w_conv)
    rest = [n for n in SMALL if n != "w_conv"]
    rest_shapes = {n: weights[n].shape[1:] for n in rest}
    packed = [jnp.concatenate([_pack_layer([src[n][l] for n in rest]) for l in range(depth)]) for src in (weights, grad, m_in, v_in)]
    for dst, res in zip((delta, new_m, new_v), _adamw(*packed)):
        dst.update(_unpack_layers(res.reshape(depth, SMALL_LAYER_ROWS * LANES), rest_shapes))

    return (loss, grad_x[None], *[grad[n] for n in WEIGHTS], *[delta[n] for n in WEIGHTS],
            *[new_m[n] for n in WEIGHTS], *[new_v[n] for n in WEIGHTS])
```

```python
import functools

import jax
import jax.numpy as jnp
from jax import lax
from jax.experimental import pallas as pl
from jax.experimental.pallas import tpu as pltpu

f32 = jnp.float32
bf16 = jnp.bfloat16

D_MODEL = 1024
DEPTH = 4
CONV_W = 384
POOL_W = 256
SGU_W = 384
IN_W = 3 * CONV_W + POOL_W + 2 * SGU_W
D_FF = 2816
CHUNK = 128
HEAD = 64
POOL_WINDOWS = (2, 4, 8, 16)
ALPHA = float((2 * DEPTH) ** 0.25)
LN_EPS = 1e-5
ADAM_LR = 0.001
ADAM_B1 = 0.9
ADAM_B2 = 0.999
ADAM_EPS = 1e-08
ADAM_WD = 0.01
ADAM_STEP = 10

LANES = 128
TOKEN_TILE = 256
N_CHIPS = 4
VMEM_LIMIT = 56 * 1024 * 1024

BLK_XA, BLK_GB, BLK_GC, BLK_P, BLK_U, BLK_V = 0, 3, 6, 9, 11, 14

MESH = pl.DeviceIdType.MESH


def _params(sem=None):
    return pltpu.CompilerParams(dimension_semantics=sem, vmem_limit_bytes=VMEM_LIMIT)


def _rows(width, tile=TOKEN_TILE):
    return pl.BlockSpec((tile, width), lambda i: (i, 0))


def _resident(shape):
    zeros = (0,) * len(shape)
    return pl.BlockSpec(shape, lambda *_: zeros, pipeline_mode=pl.Buffered(1))


def _nt(a, b):
    return lax.dot_general(a, b, (((1,), (1,)), ((), ())), preferred_element_type=f32)


def _tn(a, b):
    return lax.dot_general(a, b, (((0,), (0,)), ((), ())), preferred_element_type=f32)


def _mm(a, b):
    return jnp.dot(a, b, preferred_element_type=f32)


def _norm_fwd(z):
    mu = jnp.mean(z, axis=-1, keepdims=True)
    zc = z - mu
    var = jnp.mean(zc * zc, axis=-1, keepdims=True)
    rstd = lax.rsqrt(var + LN_EPS)
    return zc * rstd, rstd


def _norm_bwd(dxhat, xhat, rstd):
    m1 = jnp.mean(dxhat, axis=-1, keepdims=True)
    m2 = jnp.mean(dxhat * xhat, axis=-1, keepdims=True)
    return rstd * (dxhat - m1 - xhat * m2)


def _proj(x, w_in_b, after):
    s = x.shape[0]

    def body(x_ref, w_ref, after_ref, p_ref, xb_ref):
        xb = x_ref[...].astype(bf16)
        xb_ref[...] = xb
        p_ref[...] = _nt(xb, w_ref[...])

    return pl.pallas_call(
        body, grid=(s // TOKEN_TILE,),
        in_specs=[_rows(D_MODEL), _resident((IN_W, D_MODEL)), pl.BlockSpec(memory_space=pl.ANY)],
        out_specs=[_rows(IN_W), _rows(D_MODEL)],
        out_shape=[jax.ShapeDtypeStruct((s, IN_W), f32), jax.ShapeDtypeStruct((s, D_MODEL), bf16)],
        name="proj", compiler_params=_params(("arbitrary",)))(x, w_in_b, after)


def _row_ranges(parts):
    out, at = [], 0
    for p in parts:
        out.append((at, at + p.shape[1]))
        at += p.shape[1]
    return out


def _wo_ln1(mix, x, w_o_b, g, b):
    s = x.shape[0]
    n = len(mix)
    ranges = _row_ranges(mix)

    def body(*refs):
        m_refs = refs[:n]
        x_ref, w_ref, g_ref, b_ref, xhat_ref, rstd_ref, hb_ref = refs[n:]
        z = ALPHA * x_ref[...]
        for m_ref, (lo, hi) in zip(m_refs, ranges):
            z = z + _mm(m_ref[...], w_ref[lo:hi, :])
        xhat, rstd = _norm_fwd(z)
        xhat_ref[...] = xhat
        rstd_ref[...] = rstd
        hb_ref[...] = (xhat * g_ref[...] + b_ref[...]).astype(bf16)

    return pl.pallas_call(
        body, grid=(s // TOKEN_TILE,),
        in_specs=[_rows(m.shape[1]) for m in mix] + [_rows(D_MODEL), _resident((D_MODEL, D_MODEL)), _resident((1, D_MODEL)),
                                                     _resident((1, D_MODEL))],
        out_specs=[_rows(D_MODEL), _rows(1), _rows(D_MODEL)],
        out_shape=[jax.ShapeDtypeStruct((s, D_MODEL), f32), jax.ShapeDtypeStruct((s, 1), f32),
                   jax.ShapeDtypeStruct((s, D_MODEL), bf16)],
        name="wo_ln1", compiler_params=_params(("arbitrary",)))(*mix, x, w_o_b, g, b)


def _mlp_fwd(xhat1, g1, b1, w_gu_b, w_down_b, g2, b2, after):
    s = xhat1.shape[0]

    def body(xh_ref, g1_ref, b1_ref, wgu_ref, wd_ref, g2_ref, b2_ref, after_ref, gu_ref, xhat2_ref, rstd2_ref, y_ref):
        h = xh_ref[...] * g1_ref[...] + b1_ref[...]
        gu = _mm(h.astype(bf16), wgu_ref[...])
        gu_ref[...] = gu
        gate = gu[:, :D_FF]
        act = gate * jax.nn.sigmoid(gate) * gu[:, D_FF:]
        z = ALPHA * h + _mm(act.astype(bf16), wd_ref[...])
        xhat2, rstd2 = _norm_fwd(z)
        xhat2_ref[...] = xhat2
        rstd2_ref[...] = rstd2
        y_ref[...] = xhat2 * g2_ref[...] + b2_ref[...]

    vec = _resident((1, D_MODEL))
    return pl.pallas_call(
        body, grid=(s // TOKEN_TILE,),
        in_specs=[_rows(D_MODEL), vec, vec, _resident((D_MODEL, 2 * D_FF)), _resident((D_FF, D_MODEL)), vec, vec,
                  pl.BlockSpec(memory_space=pl.ANY)],
        out_specs=[_rows(2 * D_FF), _rows(D_MODEL), _rows(1), _rows(D_MODEL)],
        out_shape=[jax.ShapeDtypeStruct((s, 2 * D_FF), f32), jax.ShapeDtypeStruct((s, D_MODEL), f32),
                   jax.ShapeDtypeStruct((s, 1), f32), jax.ShapeDtypeStruct((s, D_MODEL), f32)],
        name="mlp_fwd", compiler_params=_params(("arbitrary",)))(xhat1, g1, b1, w_gu_b, w_down_b, g2, b2, after)


def _loss_head(y, target):
    s = y.shape[0]

    def body(y_ref, t_ref, dy_ref, sq_ref):
        @pl.when(pl.program_id(0) == 0)
        def _():
            sq_ref[...] = jnp.zeros_like(sq_ref)

        e = y_ref[...] - t_ref[...]
        dy_ref[...] = e * (1.0 / D_MODEL)
        sq_ref[...] += jnp.sum(e * e, axis=0, keepdims=True)

    return pl.pallas_call(
        body, grid=(s // TOKEN_TILE,),
        in_specs=[_rows(D_MODEL), _rows(D_MODEL)],
        out_specs=[_rows(D_MODEL), pl.BlockSpec((1, D_MODEL), lambda i: (0, 0))],
        out_shape=[jax.ShapeDtypeStruct((s, D_MODEL), f32), jax.ShapeDtypeStruct((1, D_MODEL), f32)],
        name="loss_head", compiler_params=_params(("arbitrary",)))(y, target)


def _mlp_bwd(dy, xhat2, rstd2, g2, gu, w_gu_b, w_down_b, xhat1, rstd1, g1, w_o_b, after):
    s = dy.shape[0]

    def body(dy_ref, xh_ref, rs_ref, g2_ref, gu_ref, wgu_ref, wd_ref, xh1_ref, rs1_ref, g1_ref, wo_ref, after_ref,
             dz_ref, act_ref, dgu_ref, dz1_ref, dz1b_ref, dm_ref, gg_ref, gb_ref, gg1_ref, gb1_ref):
        @pl.when(pl.program_id(0) == 0)
        def _():
            for ref in (gg_ref, gb_ref, gg1_ref, gb1_ref):
                ref[...] = jnp.zeros_like(ref)

        dy_t = dy_ref[...]
        xhat = xh_ref[...]
        gg_ref[...] += jnp.sum(dy_t * xhat, axis=0, keepdims=True)
        gb_ref[...] += jnp.sum(dy_t, axis=0, keepdims=True)
        dz = _norm_bwd(dy_t * g2_ref[...], xhat, rs_ref[...])
        dzb = dz.astype(bf16)
        dz_ref[...] = dzb
        dact = _nt(dzb, wd_ref[...])
        gate = gu_ref[:, :D_FF]
        up = gu_ref[:, D_FF:]
        sg = jax.nn.sigmoid(gate)
        silu = gate * sg
        act_ref[...] = (silu * up).astype(bf16)
        dgu_ref[:, :D_FF] = (dact * up * (sg * (1.0 + gate * (1.0 - sg)))).astype(bf16)
        dgu_ref[:, D_FF:] = (dact * silu).astype(bf16)
        dh = ALPHA * dz + _nt(dgu_ref[...], wgu_ref[...])
        xhat1 = xh1_ref[...]
        gg1_ref[...] += jnp.sum(dh * xhat1, axis=0, keepdims=True)
        gb1_ref[...] += jnp.sum(dh, axis=0, keepdims=True)
        dz1 = _norm_bwd(dh * g1_ref[...], xhat1, rs1_ref[...])
        dz1_ref[...] = dz1
        dz1b = dz1.astype(bf16)
        dz1b_ref[...] = dz1b
        dm_ref[...] = _nt(dz1b, wo_ref[...])

    vec, vec_out = _resident((1, D_MODEL)), pl.BlockSpec((1, D_MODEL), lambda i: (0, 0))
    tokens_f32, tokens_bf16 = jax.ShapeDtypeStruct((s, D_MODEL), f32), jax.ShapeDtypeStruct((s, D_MODEL), bf16)
    sums = jax.ShapeDtypeStruct((1, D_MODEL), f32)
    return pl.pallas_call(
        body, grid=(s // TOKEN_TILE,),
        in_specs=[_rows(D_MODEL), _rows(D_MODEL), _rows(1), vec, _rows(2 * D_FF),
                  _resident((D_MODEL, 2 * D_FF)), _resident((D_FF, D_MODEL)), _rows(D_MODEL), _rows(1), vec,
                  _resident((D_MODEL, D_MODEL)), pl.BlockSpec(memory_space=pl.ANY)],
        out_specs=[_rows(D_MODEL), _rows(D_FF), _rows(2 * D_FF), _rows(D_MODEL), _rows(D_MODEL), _rows(D_MODEL),
                   vec_out, vec_out, vec_out, vec_out],
        out_shape=[tokens_bf16, jax.ShapeDtypeStruct((s, D_FF), bf16), jax.ShapeDtypeStruct((s, 2 * D_FF), bf16),
                   tokens_f32, tokens_bf16, tokens_f32, sums, sums, sums, sums],
        name="mlp_bwd", compiler_params=_params(("arbitrary",)))(
            dy, xhat2, rstd2, g2, gu, w_gu_b, w_down_b, xhat1, rstd1, g1, w_o_b, after)


def _dx(dz1, dparts, w_in_t, after):
    s = dz1.shape[0]
    n = len(dparts)
    ranges = _row_ranges(dparts)

    def body(*refs):
        d_refs = refs[:n]
        dz_ref, w_ref, _, dx_ref = refs[n:]
        acc = ALPHA * dz_ref[...]
        for d_ref, (lo, hi) in zip(d_refs, ranges):
            acc = acc + _mm(d_ref[...], w_ref[lo:hi, :])
        dx_ref[...] = acc

    return pl.pallas_call(
        body, grid=(s // TOKEN_TILE,),
        in_specs=[_rows(d.shape[1]) for d in dparts] + [_rows(D_MODEL), _resident((IN_W, D_MODEL)),
                                                        pl.BlockSpec(memory_space=pl.ANY)],
        out_specs=_rows(D_MODEL),
        out_shape=jax.ShapeDtypeStruct((s, D_MODEL), f32),
        name="dx", compiler_params=_params(("arbitrary",)))(*dparts, dz1, w_in_t, after)


def _weight_grad_rows(parts, b, bn):
    s, n_cols = b.shape
    n = len(parts)
    ranges = _row_ranges(parts)
    m = ranges[-1][1]

    def body(*refs):
        p_refs = refs[:n]
        b_ref, o_ref = refs[n:]
        for p_ref, (lo, hi) in zip(p_refs, ranges):
            o_ref[lo:hi, :] = _tn(p_ref[...], b_ref[...]).astype(bf16)

    return pl.pallas_call(
        body, grid=(n_cols // bn,),
        in_specs=[_resident(p.shape) for p in parts] + [pl.BlockSpec((s, bn), lambda j: (0, j))],
        out_specs=pl.BlockSpec((m, bn), lambda j: (0, j)),
        out_shape=jax.ShapeDtypeStruct((m, n_cols), bf16),
        name="weight_grad_rows", compiler_params=_params(("arbitrary",)))(*parts, b)


def _weight_grad(a, b, bm, bn, after):
    s, m = a.shape
    n = b.shape[1]

    def body(a_ref, b_ref, after_ref, o_ref):
        o_ref[...] = _tn(a_ref[...], b_ref[...]).astype(bf16)

    return pl.pallas_call(
        body, grid=(m // bm, n // bn),
        in_specs=[pl.BlockSpec((s, bm), lambda i, j: (0, i)), pl.BlockSpec((s, bn), lambda i, j: (0, j)),
                  pl.BlockSpec(memory_space=pl.ANY)],
        out_specs=pl.BlockSpec((bm, bn), lambda i, j: (i, j)),
        out_shape=jax.ShapeDtypeStruct((m, n), bf16),
        name="weight_grad", compiler_params=_params(("arbitrary", "arbitrary")))(a, b, after)


def _shift_down(a, k):
    row = lax.broadcasted_iota(jnp.int32, a.shape, 0)
    return jnp.where(row >= k, pltpu.roll(a, k, 0), 0.0)


def _shift_up(a, k):
    n = a.shape[0]
    row = lax.broadcasted_iota(jnp.int32, a.shape, 0)
    return jnp.where(row < n - k, pltpu.roll(a, n - k, 0), 0.0)


def _slab(s, block):
    return pl.BlockSpec((s, LANES), lambda k: (0, block + k))


def _conv_y(z, w):
    return w[0:1, :] * _shift_down(z, 2) + w[1:2, :] * _shift_down(z, 1) + w[2:3, :] * z


def _conv_fwd(proj, w_conv):
    s = proj.shape[0]

    def body(xa_ref, gb_ref, gc_ref, w_ref, o_ref):
        z = gc_ref[...] * xa_ref[...]
        o_ref[...] = (gb_ref[...] * _conv_y(z, w_ref[...])).astype(bf16)

    return pl.pallas_call(
        body, grid=(CONV_W // LANES,),
        in_specs=[_slab(s, BLK_XA), _slab(s, BLK_GB), _slab(s, BLK_GC), pl.BlockSpec((3, LANES), lambda k: (0, k))],
        out_specs=_slab(s, 0),
        out_shape=jax.ShapeDtypeStruct((s, CONV_W), bf16),
        name="conv_fwd", compiler_params=_params(("arbitrary",)))(proj, proj, proj, w_conv)


def _conv_bwd(proj, dmix, w_conv, after):
    s = proj.shape[0]

    def body(xa_ref, gb_ref, gc_ref, dy_ref, w_ref, after_ref, dxa_ref, dgb_ref, dgc_ref, dw_ref):
        xa = xa_ref[...]
        gc = gc_ref[...]
        w = w_ref[...]
        z = gc * xa
        dya = dy_ref[...]
        dgb_ref[...] = (dya * _conv_y(z, w)).astype(bf16)
        dy = dya * gb_ref[...]
        dz = w[2:3, :] * dy + w[1:2, :] * _shift_up(dy, 1) + w[0:1, :] * _shift_up(dy, 2)
        dxa_ref[...] = (dz * gc).astype(bf16)
        dgc_ref[...] = (dz * xa).astype(bf16)
        dw_ref[0:1, :] = jnp.sum(dy * _shift_down(z, 2), axis=0, keepdims=True)
        dw_ref[1:2, :] = jnp.sum(dy * _shift_down(z, 1), axis=0, keepdims=True)
        dw_ref[2:3, :] = jnp.sum(dy * z, axis=0, keepdims=True)

    out = jax.ShapeDtypeStruct((s, CONV_W), bf16)
    return pl.pallas_call(
        body, grid=(CONV_W // LANES,),
        in_specs=[_slab(s, BLK_XA), _slab(s, BLK_GB), _slab(s, BLK_GC), _slab(s, 0), pl.BlockSpec((3, LANES), lambda k: (0, k)),
                  pl.BlockSpec(memory_space=pl.ANY)],
        out_specs=[_slab(s, 0), _slab(s, 0), _slab(s, 0), pl.BlockSpec((3, LANES), lambda k: (0, k))],
        out_shape=[out, out, out, jax.ShapeDtypeStruct((3, CONV_W), f32)],
        name="conv_bwd", compiler_params=_params(("arbitrary",)))(proj, proj, proj, dmix, w_conv, after)


def _pool_window(k):
    lane = lax.broadcasted_iota(jnp.int32, (1, LANES), 1)
    low = lane < HEAD
    first = k == 0
    wlen = jnp.where(low, jnp.where(first, POOL_WINDOWS[0], POOL_WINDOWS[2]), jnp.where(first, POOL_WINDOWS[1], POOL_WINDOWS[3]))
    return wlen, low, first


def _pool_diff(p, k):
    wlen, low, first = _pool_window(k)
    s2 = p + _shift_down(p, 1)
    s4 = s2 + _shift_down(s2, 2)
    s8 = s4 + _shift_down(s4, 4)
    s16 = s8 + _shift_down(s8, 8)
    win = jnp.where(low, jnp.where(first, s2, s8), jnp.where(first, s4, s16))
    row = lax.broadcasted_iota(jnp.int32, p.shape, 0)
    count = jnp.minimum(row + 1, wlen).astype(f32)
    return win / count - p, count


def _pool_weight(w_ref):
    zero = jnp.zeros((HEAD, HEAD), f32)
    top = jnp.concatenate([w_ref[0], zero], axis=1)
    bottom = jnp.concatenate([zero, w_ref[1]], axis=1)
    return jnp.concatenate([top, bottom], axis=0).astype(bf16)


def _pool_fwd(proj, w_pool, pool_scale):
    s = proj.shape[0]

    def body(p_ref, w_ref, sc_ref, o_ref):
        d, _ = _pool_diff(p_ref[...], pl.program_id(0))
        o_ref[...] = (_mm(d.astype(bf16), _pool_weight(w_ref)) * sc_ref[...]).astype(bf16)

    return pl.pallas_call(
        body, grid=(POOL_W // LANES,),
        in_specs=[_slab(s, BLK_P), pl.BlockSpec((2, HEAD, HEAD), lambda k: (k, 0, 0)), pl.BlockSpec((1, LANES), lambda k: (0, k))],
        out_specs=_slab(s, 0),
        out_shape=jax.ShapeDtypeStruct((s, POOL_W), bf16),
        name="pool_fwd", compiler_params=_params(("arbitrary",)))(proj, w_pool, pool_scale)


def _pool_bwd(proj, dmix, w_pool, pool_scale):
    s = proj.shape[0]

    def body(p_ref, dy_ref, w_ref, sc_ref, dp_ref, dw_ref, dsc_ref):
        k = pl.program_id(0)
        d, count = _pool_diff(p_ref[...], k)
        wbd = _pool_weight(w_ref)
        db = d.astype(bf16)
        dyb = dy_ref[...]
        dsc_ref[...] = jnp.sum(dyb * _mm(db, wbd), axis=0, keepdims=True)
        dpre = (dyb * sc_ref[...]).astype(bf16)
        dwbd = _tn(db, dpre)
        dw_ref[0] = dwbd[:HEAD, :HEAD]
        dw_ref[1] = dwbd[HEAD:, HEAD:]
        dd = _nt(dpre, wbd)
        e = dd / count
        wlen, low, first = _pool_window(k)
        a2 = e + _shift_up(e, 1)
        a4 = a2 + _shift_up(a2, 2)
        a8 = a4 + _shift_up(a4, 4)
        a16 = a8 + _shift_up(a8, 8)
        back = jnp.where(low, jnp.where(first, a2, a8), jnp.where(first, a4, a16))
        dp_ref[...] = (back - dd).astype(bf16)

    return pl.pallas_call(
        body, grid=(POOL_W // LANES,),
        in_specs=[_slab(s, BLK_P), _slab(s, CONV_W // LANES), pl.BlockSpec((2, HEAD, HEAD), lambda k: (k, 0, 0)),
                  pl.BlockSpec((1, LANES), lambda k: (0, k))],
        out_specs=[_slab(s, 0), pl.BlockSpec((2, HEAD, HEAD), lambda k: (k, 0, 0)), pl.BlockSpec((1, LANES), lambda k: (0, k))],
        out_shape=[jax.ShapeDtypeStruct((s, POOL_W), bf16), jax.ShapeDtypeStruct((4, HEAD, HEAD), f32),
                   jax.ShapeDtypeStruct((1, POOL_W), f32)],
        name="pool_bwd", compiler_params=_params(("arbitrary",)))(proj, dmix, w_pool, pool_scale)


SGU_UNROLL = 4
INV_SQRT2 = 0.7071067811865476
INV_SQRT_2PI = 0.3989422804014327


def _gelu(x):
    return 0.5 * x * (1.0 + lax.erf(x * INV_SQRT2))


def _gelu_grad(x):
    return 0.5 * (1.0 + lax.erf(x * INV_SQRT2)) + x * (INV_SQRT_2PI * jnp.exp(-0.5 * x * x))


def _head_mean(a, low):
    s_low = jnp.sum(jnp.where(low, a, 0.0), axis=-1, keepdims=True)
    s_high = jnp.sum(jnp.where(low, 0.0, a), axis=-1, keepdims=True)
    return jnp.where(low, s_low, s_high) * (1.0 / HEAD)


def _tril():
    r = lax.broadcasted_iota(jnp.int32, (CHUNK, CHUNK), 0)
    c = lax.broadcasted_iota(jnp.int32, (CHUNK, CHUNK), 1)
    return r >= c


def _sgu_chunk(up, vp, g, wm0, wm1, b0, b1, low):
    ug = _gelu(up)
    vg = _gelu(vp)
    vc = vg - _head_mean(vg, low)
    rstd = lax.rsqrt(_head_mean(vc * vc, low) + LN_EPS)
    vn = vc * rstd
    vb = (vn * g).astype(bf16)
    mixed = jnp.where(low, _mm(wm0, vb) + b0, _mm(wm1, vb) + b1)
    return ug, vn, rstd, vb, mixed


def _sgu_specs(s):
    return [_slab(s, BLK_U), _slab(s, BLK_V), pl.BlockSpec((1, LANES), lambda k: (0, k)),
            pl.BlockSpec((2, CHUNK, CHUNK), lambda k: (k, 0, 0)), pl.BlockSpec((2, CHUNK, 1), lambda k: (k, 0, 0))]


def _sgu_fwd(proj, sgu_g, w_spatial, b_spatial3):
    s = proj.shape[0]

    def body(u_ref, v_ref, g_ref, w_ref, b_ref, o_ref):
        low = lax.broadcasted_iota(jnp.int32, (1, LANES), 1) < HEAD
        mask = _tril()
        wm0 = jnp.where(mask, w_ref[0], 0.0).astype(bf16)
        wm1 = jnp.where(mask, w_ref[1], 0.0).astype(bf16)
        g = g_ref[...]
        b0 = b_ref[0]
        b1 = b_ref[1]

        def chunk(n, carry):
            rows = pl.ds(pl.multiple_of(n * CHUNK, CHUNK), CHUNK)
            ug, _, _, _, mixed = _sgu_chunk(u_ref[rows, :], v_ref[rows, :], g, wm0, wm1, b0, b1, low)
            o_ref[rows, :] = (ug * mixed).astype(bf16)
            return carry

        lax.fori_loop(0, s // CHUNK, chunk, 0, unroll=SGU_UNROLL)

    return pl.pallas_call(
        body, grid=(SGU_W // LANES,),
        in_specs=_sgu_specs(s),
        out_specs=_slab(s, 0),
        out_shape=jax.ShapeDtypeStruct((s, SGU_W), bf16),
        name="sgu_fwd", compiler_params=_params(("arbitrary",)))(proj, proj, sgu_g, w_spatial, b_spatial3)


def _sgu_bwd(proj, dmix, sgu_g, w_spatial, b_spatial3):
    s = proj.shape[0]

    def body(u_ref, v_ref, g_ref, w_ref, b_ref, dy_ref, du_ref, dv_ref, dg_ref, dw_ref, db_ref):
        low = lax.broadcasted_iota(jnp.int32, (1, LANES), 1) < HEAD
        mask = _tril()
        w0 = jnp.where(mask, w_ref[0], 0.0)
        w1 = jnp.where(mask, w_ref[1], 0.0)
        wm0 = w0.astype(bf16)
        wm1 = w1.astype(bf16)
        wt0 = w0.T.astype(bf16)
        wt1 = w1.T.astype(bf16)
        g = g_ref[...]
        b0 = b_ref[0]
        b1 = b_ref[1]
        dg_ref[...] = jnp.zeros_like(dg_ref)
        dw_ref[...] = jnp.zeros_like(dw_ref)
        db_ref[...] = jnp.zeros_like(db_ref)

        def chunk(n, carry):
            rows = pl.ds(pl.multiple_of(n * CHUNK, CHUNK), CHUNK)
            up = u_ref[rows, :]
            vp = v_ref[rows, :]
            ug, vn, rstd, vb, mixed = _sgu_chunk(up, vp, g, wm0, wm1, b0, b1, low)
            dy = dy_ref[rows, :]
            du_ref[rows, :] = (dy * mixed * _gelu_grad(up)).astype(bf16)
            dmix_c = dy * ug
            db_ref[0] += jnp.sum(jnp.where(low, dmix_c, 0.0), axis=-1, keepdims=True)
            db_ref[1] += jnp.sum(jnp.where(low, 0.0, dmix_c), axis=-1, keepdims=True)
            dmb = dmix_c.astype(bf16)
            zero = jnp.zeros_like(dmb)
            dw_ref[0] += _nt(jnp.where(low, dmb, zero), vb)
            dw_ref[1] += _nt(jnp.where(low, zero, dmb), vb)
            dvnorm = jnp.where(low, _mm(wt0, dmb), _mm(wt1, dmb))
            dg_ref[...] += jnp.sum(dvnorm * vn, axis=0, keepdims=True)
            dvn = dvnorm * g
            dvg = rstd * (dvn - _head_mean(dvn, low) - vn * _head_mean(dvn * vn, low))
            dv_ref[rows, :] = (dvg * _gelu_grad(vp)).astype(bf16)
            return carry

        lax.fori_loop(0, s // CHUNK, chunk, 0, unroll=SGU_UNROLL)
        dw_ref[0] = jnp.where(mask, dw_ref[0], 0.0)
        dw_ref[1] = jnp.where(mask, dw_ref[1], 0.0)

    out = jax.ShapeDtypeStruct((s, SGU_W), bf16)
    return pl.pallas_call(
        body, grid=(SGU_W // LANES,),
        in_specs=_sgu_specs(s) + [_slab(s, (CONV_W + POOL_W) // LANES)],
        out_specs=[_slab(s, 0), _slab(s, 0), pl.BlockSpec((1, LANES), lambda k: (0, k)),
                   pl.BlockSpec((2, CHUNK, CHUNK), lambda k: (k, 0, 0)), pl.BlockSpec((2, CHUNK, 1), lambda k: (k, 0, 0))],
        out_shape=[out, out, jax.ShapeDtypeStruct((1, SGU_W), f32), jax.ShapeDtypeStruct((6, CHUNK, CHUNK), f32),
                   jax.ShapeDtypeStruct((6, CHUNK, 1), f32)],
        name="sgu_bwd", compiler_params=_params(("arbitrary",)))(proj, proj, sgu_g, w_spatial, b_spatial3, dmix)


def _fwd_mix(x, w, after):
    proj, xb = _proj(x, w["w_in"], after)
    mix = [_conv_fwd(proj, w["w_conv"]), _pool_fwd(proj, w["w_pool"], w["pool_scale"]),
           _sgu_fwd(proj, w["sgu_ln_g"], w["w_spatial"], w["b_spatial"])]
    xhat1, rstd1, hb = _wo_ln1(mix, x, w["w_o"], w["ln1_g"], w["ln1_b"])
    return dict(proj=proj, xb=xb, mix=mix, xhat1=xhat1, rstd1=rstd1, hb=hb)


def _fwd_mlp(sv, w, after):
    gu, xhat2, rstd2, y = _mlp_fwd(sv["xhat1"], w["ln1_g"], w["ln1_b"], w["w_gate_up"], w["w_down"], w["ln2_g"], w["ln2_b"], after)
    sv.update(gu=gu, xhat2=xhat2, rstd2=rstd2)
    return y


def _bwd_mlp(dy, w, sv, after, hook):
    dz2b, actb, dgub, dz1, dz1b, dmix, g_ln2_g, g_ln2_b, g_ln1_g, g_ln1_b = _mlp_bwd(
        dy, sv["xhat2"], sv["rstd2"], w["ln2_g"], sv["gu"], w["w_gate_up"], w["w_down"], sv["xhat1"], sv["rstd1"], w["ln1_g"],
        w["w_o"], after)
    after = hook(dz1)
    grads = dict(w_gate_up=_weight_grad(sv["hb"], dgub, 512, D_FF // 2, after),
                 w_down=_weight_grad(actb, dz2b, D_FF // 2, D_MODEL, after),
                 ln2_g=g_ln2_g, ln2_b=g_ln2_b, ln1_g=g_ln1_g, ln1_b=g_ln1_b)
    return (dz1, dz1b, dmix), grads


def _bwd_mix(dz, w, sv, after, hook):
    dz1, dz1b, dmix = dz
    dxa, dgb, dgc, g_conv = _conv_bwd(sv["proj"], dmix, w["w_conv"], after)
    dp, g_pool, g_pscale = _pool_bwd(sv["proj"], dmix, w["w_pool"], w["pool_scale"])
    du, dv, g_sgu_g, g_spatial, g_bsp = _sgu_bwd(sv["proj"], dmix, w["sgu_ln_g"], w["w_spatial"], w["b_spatial"])
    dparts = [dxa, dgb, dgc, dp, du, dv]
    dx = _dx(dz1, dparts, w["w_in"], hook(du))
    grads = dict(
        w_in=_weight_grad_rows(dparts, sv["xb"], 512), w_o=_weight_grad_rows(sv["mix"], dz1b, D_MODEL),
        w_conv=g_conv, w_pool=g_pool, pool_scale=g_pscale, sgu_ln_g=g_sgu_g, w_spatial=g_spatial,
        b_spatial=g_bsp.reshape(6, CHUNK))
    return dx, grads


def _local_step(x, target, layers):
    saved = []
    for w in layers:
        sv = _fwd_mix(x, w, x)
        x = _fwd_mlp(sv, w, x)
        saved.append(sv)
    dy, sq = _loss_head(x, target)
    grads = [None] * len(layers)
    for l in reversed(range(len(layers))):
        dz, g_mlp = _bwd_mlp(dy, layers[l], saved[l], sq, lambda a: a)
        dy, g_mix = _bwd_mix(dz, layers[l], saved[l], dz[0], lambda a: a)
        grads[l] = dict(g_mlp, **g_mix)
    return sq, dy, grads


ANY = pl.BlockSpec(memory_space=pl.ANY)


def _place():
    x, y, c = lax.axis_index("x"), lax.axis_index("y"), lax.axis_index("c")
    others = [(1 - x, y), (x, 1 - y), (1 - x, 1 - y)]
    return x, y, c, others


def _chip_index(cx, cy):
    return 2 * cx + cy


def _half(ref_rows, c):
    half = ref_rows // 2
    return pl.ds(pl.multiple_of(c * half, 8), half)


def _remote(src, dst, send_sem, recv_sem, device):
    return pltpu.make_async_remote_copy(src_ref=src, dst_ref=dst, send_sem=send_sem, recv_sem=recv_sem,
                                        device_id=device, device_id_type=MESH)


def _gather_shards(shards):
    n = len(shards)
    base, total = [], 0
    for s in shards:
        base.append(total)
        total += 6 * s.shape[0]

    def body(*refs):
        ins, outs = refs[:n], refs[n:2 * n]
        send, recv = refs[2 * n:]
        x, y, c, others = _place()
        me = _chip_index(x, y)
        sib = (x, y, 1 - c)
        sends = []
        for f in range(n):
            depth, rows = ins[f].shape[0], ins[f].shape[1]
            for l in range(depth):
                for k, (cx, cy) in enumerate(others):
                    sem = base[f] + 6 * l + k
                    cp = _remote(ins[f].at[l, _half(rows, c)], outs[f].at[l, me, _half(rows, c)],
                                 send.at[sem], recv.at[sem], (cx, cy, c))
                    cp.start()
                    sends.append(cp)
        for f in range(n):
            depth, rows = ins[f].shape[0], ins[f].shape[1]
            for l in range(depth):
                for k, (cx, cy) in enumerate(others):
                    sem = base[f] + 6 * l + k
                    landed = outs[f].at[l, _chip_index(cx, cy), _half(rows, c)]
                    _remote(landed, landed, send.at[sem], recv.at[sem], (cx, cy, c)).wait_recv()
                    cp = _remote(landed, landed, send.at[sem + 3], recv.at[sem + 3], sib)
                    cp.start()
                    sends.append(cp)
        for f in range(n):
            depth, rows = ins[f].shape[0], ins[f].shape[1]
            for l in range(depth):
                for k, (cx, cy) in enumerate(others):
                    sem = base[f] + 6 * l + k + 3
                    passed = outs[f].at[l, _chip_index(cx, cy), _half(rows, 1 - c)]
                    _remote(passed, passed, send.at[sem], recv.at[sem], sib).wait_recv()
        for cp in sends:
            cp.wait_send()

    gathered = pl.pallas_call(
        body, in_specs=[ANY] * n, out_specs=[ANY] * n,
        out_shape=[jax.ShapeDtypeStruct((s.shape[0], N_CHIPS) + s.shape[1:], s.dtype) for s in shards],
        scratch_shapes=[pltpu.SemaphoreType.DMA((total,)), pltpu.SemaphoreType.DMA((total,))],
        name="gather_shards")(*shards)
    return [_place_own(g, s) for g, s in zip(gathered, shards)]


def _scalar(value):
    return jnp.reshape(value, (1,)).astype(jnp.int32)


def _place_own(blocks, shard):
    depth, rows, cols = shard.shape

    def body(me_ref, b_ref, s_ref, o_ref):
        o_ref[...] = s_ref[...]

    return pl.pallas_call(
        body,
        grid_spec=pltpu.PrefetchScalarGridSpec(
            num_scalar_prefetch=1, grid=(depth,),
            in_specs=[ANY, pl.BlockSpec((None, rows, cols), lambda l, me: (l, 0, 0))],
            out_specs=pl.BlockSpec((None, None, rows, cols), lambda l, me: (l, me[0], 0, 0))),
        out_shape=jax.ShapeDtypeStruct(blocks.shape, blocks.dtype),
        input_output_aliases={1: 0},
        name="place_own", compiler_params=_params(("arbitrary",)))(
            _scalar(_chip_index(lax.axis_index("x"), lax.axis_index("y"))), blocks, shard)


HBM = pl.BlockSpec(memory_space=pltpu.HBM)
SEM = pl.BlockSpec(memory_space=pltpu.SEMAPHORE)
TOKEN = jax.ShapeDtypeStruct((8, LANES), f32)
SPLIT_COPY = pltpu.CompilerParams(has_side_effects=pltpu.SideEffectType.DATAFLOW_SIDE_EFFECTING)


def _in_hbm(a):
    return pltpu.with_memory_space_constraint(a, pltpu.HBM)


def _full_shape(shard, axis):
    rows, cols = shard.shape
    return (N_CHIPS * rows, cols) if axis == 0 else (rows, N_CHIPS * cols)


def _block_half(ref, axis, j, h):
    if axis == 0:
        rows = ref.shape[0] // N_CHIPS
        return ref.at[pl.ds(pl.multiple_of(j * rows + h * (rows // 2), 16), rows // 2), :]
    half, cols = ref.shape[0] // 2, ref.shape[1] // N_CHIPS
    return ref.at[pl.ds(pl.multiple_of(h * half, 16), half), pl.ds(pl.multiple_of(j * cols, LANES), cols)]


def _place_layer(shards, axes, after):
    n = len(shards)

    def body(me_ref, *refs):
        ins, outs = refs[n:2 * n], refs[2 * n + 1:]
        for f in range(n):
            outs[f][...] = ins[f][...]

    lands = [lax.empty(_full_shape(s, ax), s.dtype) for s, ax in zip(shards, axes)]
    return pl.pallas_call(
        body,
        grid_spec=pltpu.PrefetchScalarGridSpec(
            num_scalar_prefetch=1, grid=(1,),
            in_specs=[ANY] * n + [pl.BlockSpec(s.shape, lambda i, me: (0, 0)) for s in shards] + [ANY],
            out_specs=[pl.BlockSpec(s.shape, (lambda i, me: (me[0], 0)) if ax == 0 else (lambda i, me: (0, me[0])))
                       for s, ax in zip(shards, axes)]),
        out_shape=[jax.ShapeDtypeStruct(a.shape, a.dtype) for a in lands],
        input_output_aliases={1 + f: f for f in range(n)},
        name="place_layer", compiler_params=_params(("arbitrary",)))(
            _scalar(_chip_index(lax.axis_index("x"), lax.axis_index("y"))), *lands, *shards, after)


def _gather_start(shards, lands, axes, after):
    return _split_copy_start("gather", _gather_plan(axes), 3 * len(shards), shards, lands, after)


def _gather_wait(state, axes, after):
    return _split_copy_wait("gather", _gather_plan(axes), state, after)


SIBLING_PAIR_ID = 0


def _split_copy_start(name, plan, count, ins, lands, after, sibling_only=False):
    arrays = list(ins) + list(lands)
    n_in, n = len(ins), len(arrays)

    def body(*refs):
        send, recv, token = refs[n + 1], refs[n + 2], refs[-1]
        if sibling_only:
            x, y, c, _ = _place()
            barrier = pltpu.get_barrier_semaphore()
            pl.semaphore_signal(barrier, inc=1, device_id=(x, y, 1 - c), device_id_type=MESH)
            pl.semaphore_wait(barrier, 1)
        for i, (src, dst, _, peer) in enumerate(plan(refs[:n_in], refs[n_in:n])):
            _remote(src, dst, send.at[i], recv.at[i], peer).start()
        token[...] = jnp.zeros_like(token)

    effect = pltpu.SideEffectType.DATAFLOW_SIDE_EFFECTING
    outs = pl.pallas_call(
        body, name=name + "_start",
        in_specs=[HBM] * n + [ANY],
        out_specs=(SEM, SEM, *[HBM] * n, pl.BlockSpec(memory_space=pltpu.VMEM)),
        out_shape=(pltpu.SemaphoreType.DMA((count,)), pltpu.SemaphoreType.DMA((count,)),
                   *[pltpu.HBM(a.shape, a.dtype) for a in arrays], TOKEN),
        input_output_aliases={i: 2 + i for i in range(n)},
        compiler_params=pltpu.CompilerParams(has_side_effects=effect, collective_id=SIBLING_PAIR_ID) if sibling_only
        else SPLIT_COPY)(*[_in_hbm(a) for a in arrays], after)
    return (outs[0], outs[1], outs[2:2 + n_in], outs[2 + n_in:2 + n]), outs[-1]


def _split_copy_wait(name, plan, state, after):
    send_sems, recv_sems, ins, lands = state
    arrays = list(ins) + list(lands)
    n_in, n = len(ins), len(arrays)

    def body(*refs):
        send, recv, token = refs[n], refs[n + 1], refs[-1]
        for i, (src, _, landing, peer) in enumerate(plan(refs[:n_in], refs[n_in:n])):
            cp = _remote(src, landing, send.at[i], recv.at[i], peer)
            cp.wait_send()
            cp.wait_recv()
        token[...] = jnp.zeros_like(token)

    outs = pl.pallas_call(
        body, name=name + "_wait",
        in_specs=[HBM] * n + [SEM, SEM, ANY],
        out_specs=(*[HBM] * n, pl.BlockSpec(memory_space=pltpu.VMEM)),
        out_shape=(*[pltpu.HBM(a.shape, a.dtype) for a in arrays], TOKEN),
        input_output_aliases={i: i for i in range(n)},
        compiler_params=SPLIT_COPY)(*arrays, send_sems, recv_sems, after)
    return outs[:n_in], outs[n_in:n], outs[-1]


def _gather_plan(axes):
    def plan(ins, lnd):
        x, y, c, others = _place()
        me = _chip_index(x, y)
        return [(ins[f].at[_half(ins[f].shape[0], c)], _block_half(lnd[f], ax, me, c),
                 _block_half(lnd[f], ax, _chip_index(cx, cy), c), (cx, cy, c))
                for f, ax in enumerate(axes) for cx, cy in others]
    return plan


def _pair_plan(axes):
    def plan(ins, lnd):
        x, y, c, _ = _place()
        return [(_block_half(ins[f], ax, j, 1 - c), lnd[f].at[j], lnd[f].at[j], (x, y, 1 - c))
                for f, ax in enumerate(axes) for j in range(N_CHIPS)]
    return plan


def _scatter_plan(ins, lnd):
    x, y, c, others = _place()
    return [(ins[f].at[_chip_index(cx, cy)], lnd[f].at[k], lnd[f].at[k], (cx, cy, c))
            for f in range(len(ins)) for k, (cx, cy) in enumerate(others)]


def _join_plan(ins, lnd):
    x, y, c, _ = _place()
    return [(lnd[f].at[_half(lnd[f].shape[0], c)], lnd[f].at[_half(lnd[f].shape[0], c)],
             lnd[f].at[_half(lnd[f].shape[0], 1 - c)], (x, y, 1 - c)) for f in range(len(lnd))]


def _gather_finish(lands, axes):
    n = len(lands)

    def body(*refs):
        outs = refs[n:2 * n]
        send, recv = refs[2 * n:]
        x, y, c, others = _place()
        sib = (x, y, 1 - c)
        sends = []
        for f in range(n):
            for k, (cx, cy) in enumerate(others):
                landed = _block_half(outs[f], axes[f], _chip_index(cx, cy), c)
                cp = _remote(landed, landed, send.at[3 * f + k], recv.at[3 * f + k], sib)
                cp.start()
                sends.append(cp)
        for f in range(n):
            for k, (cx, cy) in enumerate(others):
                passed = _block_half(outs[f], axes[f], _chip_index(cx, cy), 1 - c)
                _remote(passed, passed, send.at[3 * f + k], recv.at[3 * f + k], sib).wait_recv()
        for cp in sends:
            cp.wait_send()

    return pl.pallas_call(
        body, in_specs=[ANY] * n, out_specs=[ANY] * n,
        out_shape=[jax.ShapeDtypeStruct(a.shape, a.dtype) for a in lands],
        input_output_aliases={f: f for f in range(n)},
        scratch_shapes=[pltpu.SemaphoreType.DMA((3 * n,)), pltpu.SemaphoreType.DMA((3 * n,))],
        name="gather_finish")(*lands)


def _half_blocks(part, axis):
    rows, cols = (part.shape[0] // N_CHIPS, part.shape[1]) if axis == 0 else (part.shape[0], part.shape[1] // N_CHIPS)
    return lax.empty((N_CHIPS, rows // 2, cols), part.dtype)


def _add_pair_layer(parts, gots, axes):
    k = len(parts)

    def body(c_ref, *refs):
        for f in range(k):
            a_ref, b_ref, o_ref = refs[2 * f], refs[2 * f + 1], refs[2 * k + f]
            o_ref[...] = (a_ref[...].astype(f32) + b_ref[...].astype(f32)).astype(o_ref.dtype)

    in_specs, out_specs, operands = [], [], []
    for part, got, axis in zip(parts, gots, axes):
        _, half, cols = got.shape
        if axis == 0:
            part = part.reshape(N_CHIPS, 2, half, cols)
            mine = pl.BlockSpec((None, None, half, cols), lambda j, c: (j, c[0], 0, 0))
        else:
            mine = pl.BlockSpec((half, cols), lambda j, c: (c[0], j))
        block = pl.BlockSpec((None, half, cols), lambda j, c: (j, 0, 0))
        in_specs += [mine, block]
        out_specs.append(block)
        operands += [part, got]
    return pl.pallas_call(
        body,
        grid_spec=pltpu.PrefetchScalarGridSpec(num_scalar_prefetch=1, grid=(N_CHIPS,), in_specs=in_specs, out_specs=out_specs),
        out_shape=[jax.ShapeDtypeStruct(g.shape, p.dtype) for p, g in zip(parts, gots)],
        name="add_pair_layer", compiler_params=_params(("arbitrary",)))(_scalar(lax.axis_index("c")), *operands)


def _scatter_start(sums, after):
    lands = [lax.empty((3,) + s.shape[1:], s.dtype) for s in sums]
    return _split_copy_start("scatter", _scatter_plan, 3 * len(sums), sums, lands, after)


def _scatter_wait(state, after):
    return _split_copy_wait("scatter", _scatter_plan, state, after)


ELEMENTWISE_BLOCK_BYTES = 1 << 20


def _row_tile(rows, cols):
    best = None
    for tile in range(8, rows + 1, 8):
        if rows % tile == 0 and tile * cols * 4 <= ELEMENTWISE_BLOCK_BYTES:
            best = tile
    return best or rows


def _add_slots(chip_sums, slots):
    k = len(chip_sums)

    def body(at_ref, *refs):
        for f in range(k):
            own_ref, s_ref, o_ref = refs[2 * f], refs[2 * f + 1], refs[2 * k + f]
            acc = own_ref[...].astype(f32)
            for j in range(3):
                acc = acc + s_ref[j].astype(f32)
            o_ref[...] = acc

    in_specs, out_specs, operands = [], [], []
    for cs, s in zip(chip_sums, slots):
        _, half, cols = cs.shape
        in_specs += [pl.BlockSpec((None, half, cols), lambda i, at: (at[0], 0, 0)), pl.BlockSpec((3, half, cols), lambda i, at: (0, 0, 0))]
        out_specs.append(pl.BlockSpec((None, half, cols), lambda i, at: (at[1], 0, 0)))
        operands += [cs, s]
    at = jnp.concatenate([_scalar(_chip_index(lax.axis_index("x"), lax.axis_index("y"))), _scalar(lax.axis_index("c"))])
    outs = pl.pallas_call(
        body,
        grid_spec=pltpu.PrefetchScalarGridSpec(num_scalar_prefetch=1, grid=(1,), in_specs=in_specs, out_specs=out_specs),
        out_shape=[jax.ShapeDtypeStruct((2,) + cs.shape[1:], f32) for cs in chip_sums],
        name="add_slots", compiler_params=_params(("arbitrary",)))(at, *operands)
    return [o.reshape(2 * o.shape[1], o.shape[2]) for o in outs]


def _adamw_math(w, grad, m, v):
    nm = ADAM_B1 * m + (1.0 - ADAM_B1) * grad
    nv = ADAM_B2 * v + (1.0 - ADAM_B2) * (grad * grad)
    m_hat = nm / (1.0 - ADAM_B1 ** ADAM_STEP)
    v_hat = nv / (1.0 - ADAM_B2 ** ADAM_STEP)
    return nm, nv, -ADAM_LR * (m_hat / (jnp.sqrt(v_hat) + ADAM_EPS) + ADAM_WD * w)


def _adamw(w, g, m, v):
    shape = w.shape
    flat = [a.reshape(-1, shape[-1]) for a in (w, g, m, v)]
    tile = _row_tile(flat[0].shape[0], shape[-1])

    def body(w_ref, g_ref, m_ref, v_ref, d_ref, nm_ref, nv_ref):
        nm, nv, step = _adamw_math(w_ref[...], g_ref[...], m_ref[...], v_ref[...])
        d_ref[...] = step
        nm_ref[...] = nm
        nv_ref[...] = nv

    spec = _rows(shape[-1], tile)
    out = jax.ShapeDtypeStruct(flat[0].shape, f32)
    res = pl.pallas_call(
        body, grid=(flat[0].shape[0] // tile,),
        in_specs=[spec] * 4, out_specs=[spec] * 3, out_shape=[out] * 3,
        name="adamw", compiler_params=_params(("arbitrary",)))(*flat)
    return [r.reshape(shape) for r in res]


def _adamw_layer(l, ws, ms, vs, gs, outs, steps, after):
    k = len(ws)

    def body(*refs):
        ins, new = refs[:4 * k], refs[8 * k + 1:]
        for f in range(k):
            w_ref, m_ref, v_ref, g_ref = ins[4 * f:4 * f + 4]
            go_ref, d_ref, nm_ref, nv_ref = new[4 * f:4 * f + 4]
            grad = g_ref[...]
            nm, nv, step = _adamw_math(w_ref[...], grad, m_ref[...], v_ref[...])
            go_ref[...] = grad
            d_ref[...] = step
            nm_ref[...] = nm
            nv_ref[...] = nv

    in_specs, out_specs, operands = [], [], []
    for w, m, v, g in zip(ws, ms, vs, gs):
        _, rows, cols = w.shape
        tile = rows // steps
        layer = pl.BlockSpec((None, tile, cols), lambda i: (l, i, 0))
        in_specs += [layer] * 3 + [_rows(cols, tile)]
        out_specs += [layer] * 4
        operands += [w, m, v, g]
    flat_outs = [o for four in outs for o in four]
    res = pl.pallas_call(
        body, grid=(steps,),
        in_specs=in_specs + [ANY] * (4 * k + 1), out_specs=out_specs,
        out_shape=[jax.ShapeDtypeStruct(o.shape, f32) for o in flat_outs],
        input_output_aliases={4 * k + j: j for j in range(4 * k)},
        name="adamw_layer", compiler_params=_params(("arbitrary",)))(*operands, *flat_outs, after)
    return [res[4 * f:4 * f + 4] for f in range(k)]


SMALL = ("w_conv", "w_pool", "pool_scale", "sgu_ln_g", "w_spatial", "b_spatial", "ln1_g", "ln1_b", "ln2_g", "ln2_b")
WEIGHTS = ("w_in", "w_conv", "w_pool", "pool_scale", "sgu_ln_g", "w_spatial", "b_spatial", "w_o", "ln1_g", "ln1_b",
           "w_gate_up", "w_down", "ln2_g", "ln2_b")
BIG = ("w_in", "w_o", "w_gate_up", "w_down")
GROUPS = (("w_in", "w_o"), ("w_gate_up", "w_down"))
GROUP_AXES = ((0, 0), (1, 0))
SCATTER_HOOKS = 2
ADAMW_STEPS = (2, 4)
SMALL_LAYER_ROWS = 1024


def _pack_layer(arrays):
    flat = jnp.concatenate([a.reshape(-1) for a in arrays])
    return jnp.pad(flat, (0, SMALL_LAYER_ROWS * LANES - flat.shape[0])).reshape(SMALL_LAYER_ROWS, LANES)


def _unpack_layers(flat, shapes):
    out, at = {}, 0
    for name, shape in shapes.items():
        size = 1
        for d in shape:
            size *= d
        out[name] = flat[:, at:at + size].reshape((flat.shape[0],) + tuple(shape))
        at += size
    return out


def kernel(x, w_in, w_conv, w_pool, pool_scale, sgu_ln_g, w_spatial, b_spatial, w_o, ln1_g, ln1_b, w_gate_up, w_down, ln2_g, ln2_b, loss_target, m_w_in, m_w_conv, m_w_pool, m_pool_scale, m_sgu_ln_g, m_w_spatial, m_b_spatial, m_w_o, m_ln1_g, m_ln1_b, m_w_gate_up, m_w_down, m_ln2_g, m_ln2_b, v_w_in, v_w_conv, v_w_pool, v_pool_scale, v_sgu_ln_g, v_w_spatial, v_b_spatial, v_w_o, v_ln1_g, v_ln1_b, v_w_gate_up, v_w_down, v_ln2_g, v_ln2_b):
    weights = dict(w_in=w_in, w_conv=w_conv, w_pool=w_pool, pool_scale=pool_scale, sgu_ln_g=sgu_ln_g, w_spatial=w_spatial,
                   b_spatial=b_spatial, w_o=w_o, ln1_g=ln1_g, ln1_b=ln1_b, w_gate_up=w_gate_up, w_down=w_down, ln2_g=ln2_g, ln2_b=ln2_b)
    m_in = dict(w_in=m_w_in, w_conv=m_w_conv, w_pool=m_w_pool, pool_scale=m_pool_scale, sgu_ln_g=m_sgu_ln_g, w_spatial=m_w_spatial,
                b_spatial=m_b_spatial, w_o=m_w_o, ln1_g=m_ln1_g, ln1_b=m_ln1_b, w_gate_up=m_w_gate_up, w_down=m_w_down,
                ln2_g=m_ln2_g, ln2_b=m_ln2_b)
    v_in = dict(w_in=v_w_in, w_conv=v_w_conv, w_pool=v_w_pool, pool_scale=v_pool_scale, sgu_ln_g=v_sgu_ln_g, w_spatial=v_w_spatial,
                b_spatial=v_b_spatial, w_o=v_w_o, ln1_g=v_ln1_g, ln1_b=v_ln1_b, w_gate_up=v_w_gate_up, w_down=v_w_down,
                ln2_g=v_ln2_g, ln2_b=v_ln2_b)
    depth = w_in.shape[0]
    conv_cols = w_conv.shape[2]
    chip = _chip_index(lax.axis_index("x"), lax.axis_index("y"))

    conv_flat = jnp.pad(w_conv.reshape(-1), (0, 16 * LANES - w_conv.size)).reshape(1, 16, LANES)
    conv_full = _gather_shards([conv_flat])[0].reshape(N_CHIPS, 16 * LANES)[:, :w_conv.size].reshape(N_CHIPS, depth, 3, conv_cols)
    conv_full = conv_full.transpose(1, 2, 0, 3).reshape(depth, 3, N_CHIPS * conv_cols)

    big_w = dict(w_in=jnp.swapaxes(w_in, 1, 2), w_o=w_o, w_gate_up=w_gate_up, w_down=w_down)
    big_m = dict(w_in=jnp.swapaxes(m_w_in, 1, 2), w_o=m_w_o, w_gate_up=m_w_gate_up, w_down=m_w_down)
    big_v = dict(w_in=jnp.swapaxes(v_w_in, 1, 2), w_o=v_w_o, w_gate_up=v_w_gate_up, w_down=v_w_down)

    def shards_of(l, g):
        return [big_w[n][l].astype(bf16) for n in GROUPS[g]]

    def send(l, g, after):
        return _gather_start(shards_of(l, g), placed[l, g], GROUP_AXES[g], after)

    def receive(g, flight, after):
        _, lands, token = _gather_wait(flight, GROUP_AXES[g], after)
        return _gather_finish(lands, GROUP_AXES[g]), token

    placed = {(0, 0): _place_layer(shards_of(0, 0), GROUP_AXES[0], conv_full)}
    flight, token = send(0, 0, conv_full)
    behind = conv_full
    for l, g in [(l, g) for l in range(depth) for g in (0, 1)][1:]:
        placed[l, g] = _place_layer(shards_of(l, g), GROUP_AXES[g], token)
        behind = placed[l, g][0]
    act = x[0]
    layers, saved = [], []
    for l in range(depth):
        w = dict(w_conv=conv_full[l], w_pool=w_pool[l], pool_scale=pool_scale[l][None], sgu_ln_g=sgu_ln_g[l][None],
                 w_spatial=w_spatial[l], b_spatial=b_spatial[l][:, :, None], ln1_g=ln1_g[l][None], ln1_b=ln1_b[l][None],
                 ln2_g=ln2_g[l][None], ln2_b=ln2_b[l][None])
        mats, token = receive(0, flight, behind if l == 0 else act)
        flight, token = send(l, 1, token)
        w.update(zip(GROUPS[0], mats))
        sv = _fwd_mix(act, w, token)
        mats, token = receive(1, flight, sv["xhat1"])
        if l + 1 < depth:
            flight, token = send(l + 1, 0, token)
        w.update(zip(GROUPS[1], mats))
        act = _fwd_mlp(sv, w, token)
        layers.append(w)
        saved.append(sv)

    big_outs = {n: [lax.empty(big_w[n].shape, f32) for _ in range(4)] for n in BIG}
    small_sums = [None] * depth
    pending, updates = [], []
    latest = dict(token=None)

    def begin(l, g, parts):
        axes = GROUP_AXES[g] + (0,) * (len(parts) - len(GROUPS[g]))
        lands = [_half_blocks(p, ax) for p, ax in zip(parts, axes)]
        flight, latest["token"] = _split_copy_start("pair", _pair_plan(axes), N_CHIPS * len(parts), parts, lands, latest["token"],
                                                    sibling_only=True)
        pending.append(dict(l=l, g=g, axes=axes, step="pair", age=0, flight=flight))

    def advance(st, recent):
        if st["step"] == "pair":
            parts, got, _ = _split_copy_wait("pair", _pair_plan(st["axes"]), st["flight"], recent)
            sums = _add_pair_layer(parts, got, st["axes"])
            st["flight"], latest["token"] = _scatter_start(sums, latest["token"])
            st["step"] = "scatter"
        elif st["step"] == "scatter":
            sums, slots, _ = _scatter_wait(st["flight"], recent)
            filled = _add_slots(sums, slots)
            st["flight"], latest["token"] = _split_copy_start("join", _join_plan, len(filled), [], filled, latest["token"],
                                                              sibling_only=True)
            st["step"] = "join"
        else:
            _, summed, _ = _split_copy_wait("join", _join_plan, st["flight"], recent)
            updates.append((st["l"], st["g"], summed[:len(GROUPS[st["g"]])]))
            if st["g"] == 0:
                small_sums[st["l"]] = summed[-1]
            st["step"] = "done"
        st["age"] = 0

    def hook(recent):
        for st in reversed(list(pending)):
            st["age"] += 1
            if st["age"] >= SCATTER_HOOKS or st["step"] != "scatter":
                advance(st, recent)
                if st["step"] == "done":
                    pending.remove(st)
        return latest["token"]

    def update(count, recent):
        for l, g, totals in updates[:count]:
            names = GROUPS[g]
            new = _adamw_layer(l, [big_w[n] for n in names], [big_m[n] for n in names], [big_v[n] for n in names], totals,
                               [big_outs[n] for n in names], ADAMW_STEPS[g], latest["token"])
            big_outs.update(zip(names, new))
            recent = new[-1][1]
        del updates[:count]
        return recent

    grad_x, sq = _loss_head(act, loss_target[0])
    latest["token"] = sq
    grads = [None] * depth
    for l in reversed(range(depth)):
        dz, g_mlp = _bwd_mlp(grad_x, layers[l], saved[l], latest["token"], hook)
        hook(g_mlp["w_down"])
        begin(l, 1, [g_mlp[n] for n in GROUPS[1]])
        grad_x, g_mix = _bwd_mix(dz, layers[l], saved[l], latest["token"], hook)
        grads[l] = dict(g_mlp, **g_mix)
        hook(g_mix["w_o"])
        begin(l, 0, [g_mix[n] for n in GROUPS[0]] + [_pack_layer([grads[l][n] for n in SMALL])])
    recent = g_mix["w_o"]
    while pending:
        recent = update(-(-2 * len(updates) // 3), recent)
        hook(recent)
    update(len(updates), recent)
    loss = lax.psum(0.5 / D_MODEL * jnp.sum(sq), ("x", "y", "c"))

    small_sum = _gather_shards([jnp.stack(small_sums)])[0].reshape(depth, SMALL_LAYER_ROWS * LANES)
    grad = {n: [jnp.swapaxes(o, 1, 2) for o in big_outs[n]] if n == "w_in" else big_outs[n] for n in BIG}
    delta = {n: o[1] for n, o in grad.items()}
    new_m = {n: o[2] for n, o in grad.items()}
    new_v = {n: o[3] for n, o in grad.items()}
    grad = {n: o[0] for n, o in grad.items()}
    grad.update(_unpack_layers(small_sum, {n: (3, N_CHIPS * conv_cols) if n == "w_conv" else weights[n].shape[1:] for n in SMALL}))
    grad["w_conv"] = lax.dynamic_slice_in_dim(grad["w_conv"], chip * conv_cols, conv_cols, axis=2)

    delta["w_conv"], new_m["w_conv"], new_v["w_conv"] = _adamw(w_conv, grad["w_conv"], m_w_conv, v_w_conv)
    rest = [n for n in SMALL if n != "w_conv"]
    rest_shapes = {n: weights[n].shape[1:] for n in rest}
    packed = [jnp.concatenate([_pack_layer([src[n][l] for n in rest]) for l in range(depth)]) for src in (weights, grad, m_in, v_in)]
    for dst, res in zip((delta, new_m, new_v), _adamw(*packed)):
        dst.update(_unpack_layers(res.reshape(depth, SMALL_LAYER_ROWS * LANES), rest_shapes))

    return (loss, grad_x[None], *[grad[n] for n in WEIGHTS], *[delta[n] for n in WEIGHTS],
            *[new_m[n] for n in WEIGHTS], *[new_v[n] for n in WEIGHTS])
```

```python
import functools

import jax
import jax.numpy as jnp
from jax import lax
from jax.experimental import pallas as pl
from jax.experimental.pallas import tpu as pltpu

f32 = jnp.float32
bf16 = jnp.bfloat16

D_MODEL = 1024
DEPTH = 4
CONV_W = 384
POOL_W = 256
SGU_W = 384
IN_W = 3 * CONV_W + POOL_W + 2 * SGU_W
D_FF = 2816
CHUNK = 128
HEAD = 64
POOL_WINDOWS = (2, 4, 8, 16)
ALPHA = float((2 * DEPTH) ** 0.25)
LN_EPS = 1e-5
ADAM_LR = 0.001
ADAM_B1 = 0.9
ADAM_B2 = 0.999
ADAM_EPS = 1e-08
ADAM_WD = 0.01
ADAM_STEP = 10

LANES = 128
TOKEN_TILE = 256
N_CHIPS = 4
VMEM_LIMIT = 56 * 1024 * 1024

BLK_XA, BLK_GB, BLK_GC, BLK_P, BLK_U, BLK_V = 0, 3, 6, 9, 11, 14

MESH = pl.DeviceIdType.MESH


def _params(sem=None):
    return pltpu.CompilerParams(dimension_semantics=sem, vmem_limit_bytes=VMEM_LIMIT)


def _rows(width, tile=TOKEN_TILE):
    return pl.BlockSpec((tile, width), lambda i: (i, 0))


def _resident(shape):
    zeros = (0,) * len(shape)
    return pl.BlockSpec(shape, lambda *_: zeros, pipeline_mode=pl.Buffered(1))


def _nt(a, b):
    return lax.dot_general(a, b, (((1,), (1,)), ((), ())), preferred_element_type=f32)


def _tn(a, b):
    return lax.dot_general(a, b, (((0,), (0,)), ((), ())), preferred_element_type=f32)


def _mm(a, b):
    return jnp.dot(a, b, preferred_element_type=f32)


def _norm_fwd(z):
    mu = jnp.mean(z, axis=-1, keepdims=True)
    zc = z - mu
    var = jnp.mean(zc * zc, axis=-1, keepdims=True)
    rstd = lax.rsqrt(var + LN_EPS)
    return zc * rstd, rstd


def _norm_bwd(dxhat, xhat, rstd):
    m1 = jnp.mean(dxhat, axis=-1, keepdims=True)
    m2 = jnp.mean(dxhat * xhat, axis=-1, keepdims=True)
    return rstd * (dxhat - m1 - xhat * m2)


def _proj(x, w_in_b, after):
    s = x.shape[0]

    def body(x_ref, w_ref, after_ref, p_ref, xb_ref):
        xb = x_ref[...].astype(bf16)
        xb_ref[...] = xb
        p_ref[...] = _nt(xb, w_ref[...])

    return pl.pallas_call(
        body, grid=(s // TOKEN_TILE,),
        in_specs=[_rows(D_MODEL), _resident((IN_W, D_MODEL)), pl.BlockSpec(memory_space=pl.ANY)],
        out_specs=[_rows(IN_W), _rows(D_MODEL)],
        out_shape=[jax.ShapeDtypeStruct((s, IN_W), f32), jax.ShapeDtypeStruct((s, D_MODEL), bf16)],
        name="proj", compiler_params=_params(("arbitrary",)))(x, w_in_b, after)


def _row_ranges(parts):
    out, at = [], 0
    for p in parts:
        out.append((at, at + p.shape[1]))
        at += p.shape[1]
    return out


def _wo_ln1(mix, x, w_o_b, g, b):
    s = x.shape[0]
    n = len(mix)
    ranges = _row_ranges(mix)

    def body(*refs):
        m_refs = refs[:n]
        x_ref, w_ref, g_ref, b_ref, xhat_ref, rstd_ref, hb_ref = refs[n:]
        z = ALPHA * x_ref[...]
        for m_ref, (lo, hi) in zip(m_refs, ranges):
            z = z + _mm(m_ref[...], w_ref[lo:hi, :])
        xhat, rstd = _norm_fwd(z)
        xhat_ref[...] = xhat
        rstd_ref[...] = rstd
        hb_ref[...] = (xhat * g_ref[...] + b_ref[...]).astype(bf16)

    return pl.pallas_call(
        body, grid=(s // TOKEN_TILE,),
        in_specs=[_rows(m.shape[1]) for m in mix] + [_rows(D_MODEL), _resident((D_MODEL, D_MODEL)), _resident((1, D_MODEL)),
                                                     _resident((1, D_MODEL))],
        out_specs=[_rows(D_MODEL), _rows(1), _rows(D_MODEL)],
        out_shape=[jax.ShapeDtypeStruct((s, D_MODEL), f32), jax.ShapeDtypeStruct((s, 1), f32),
                   jax.ShapeDtypeStruct((s, D_MODEL), bf16)],
        name="wo_ln1", compiler_params=_params(("arbitrary",)))(*mix, x, w_o_b, g, b)


def _mlp_fwd(xhat1, g1, b1, w_gu_b, w_down_b, g2, b2, after):
    s = xhat1.shape[0]

    def body(xh_ref, g1_ref, b1_ref, wgu_ref, wd_ref, g2_ref, b2_ref, after_ref, gu_ref, xhat2_ref, rstd2_ref, y_ref):
        h = xh_ref[...] * g1_ref[...] + b1_ref[...]
        gu = _mm(h.astype(bf16), wgu_ref[...])
        gu_ref[...] = gu
        gate = gu[:, :D_FF]
        act = gate * jax.nn.sigmoid(gate) * gu[:, D_FF:]
        z = ALPHA * h + _mm(act.astype(bf16), wd_ref[...])
        xhat2, rstd2 = _norm_fwd(z)
        xhat2_ref[...] = xhat2
        rstd2_ref[...] = rstd2
        y_ref[...] = xhat2 * g2_ref[...] + b2_ref[...]

    vec = _resident((1, D_MODEL))
    return pl.pallas_call(
        body, grid=(s // TOKEN_TILE,),
        in_specs=[_rows(D_MODEL), vec, vec, _resident((D_MODEL, 2 * D_FF)), _resident((D_FF, D_MODEL)), vec, vec,
                  pl.BlockSpec(memory_space=pl.ANY)],
        out_specs=[_rows(2 * D_FF), _rows(D_MODEL), _rows(1), _rows(D_MODEL)],
        out_shape=[jax.ShapeDtypeStruct((s, 2 * D_FF), f32), jax.ShapeDtypeStruct((s, D_MODEL), f32),
                   jax.ShapeDtypeStruct((s, 1), f32), jax.ShapeDtypeStruct((s, D_MODEL), f32)],
        name="mlp_fwd", compiler_params=_params(("arbitrary",)))(xhat1, g1, b1, w_gu_b, w_down_b, g2, b2, after)


def _loss_head(y, target):
    s = y.shape[0]

    def body(y_ref, t_ref, dy_ref, sq_ref):
        @pl.when(pl.program_id(0) == 0)
        def _():
            sq_ref[...] = jnp.zeros_like(sq_ref)

        e = y_ref[...] - t_ref[...]
        dy_ref[...] = e * (1.0 / D_MODEL)
        sq_ref[...] += jnp.sum(e * e, axis=0, keepdims=True)

    return pl.pallas_call(
        body, grid=(s // TOKEN_TILE,),
        in_specs=[_rows(D_MODEL), _rows(D_MODEL)],
        out_specs=[_rows(D_MODEL), pl.BlockSpec((1, D_MODEL), lambda i: (0, 0))],
        out_shape=[jax.ShapeDtypeStruct((s, D_MODEL), f32), jax.ShapeDtypeStruct((1, D_MODEL), f32)],
        name="loss_head", compiler_params=_params(("arbitrary",)))(y, target)


def _mlp_bwd(dy, xhat2, rstd2, g2, gu, w_gu_b, w_down_b, xhat1, rstd1, g1, w_o_b, after):
    s = dy.shape[0]

    def body(dy_ref, xh_ref, rs_ref, g2_ref, gu_ref, wgu_ref, wd_ref, xh1_ref, rs1_ref, g1_ref, wo_ref, after_ref,
             dz_ref, act_ref, dgu_ref, dz1_ref, dz1b_ref, dm_ref, gg_ref, gb_ref, gg1_ref, gb1_ref):
        @pl.when(pl.program_id(0) == 0)
        def _():
            for ref in (gg_ref, gb_ref, gg1_ref, gb1_ref):
                ref[...] = jnp.zeros_like(ref)

        dy_t = dy_ref[...]
        xhat = xh_ref[...]
        gg_ref[...] += jnp.sum(dy_t * xhat, axis=0, keepdims=True)
        gb_ref[...] += jnp.sum(dy_t, axis=0, keepdims=True)
        dz = _norm_bwd(dy_t * g2_ref[...], xhat, rs_ref[...])
        dzb = dz.astype(bf16)
        dz_ref[...] = dzb
        dact = _nt(dzb, wd_ref[...])
        gate = gu_ref[:, :D_FF]
        up = gu_ref[:, D_FF:]
        sg = jax.nn.sigmoid(gate)
        silu = gate * sg
        act_ref[...] = (silu * up).astype(bf16)
        dgu_ref[:, :D_FF] = (dact * up * (sg * (1.0 + gate * (1.0 - sg)))).astype(bf16)
        dgu_ref[:, D_FF:] = (dact * silu).astype(bf16)
        dh = ALPHA * dz + _nt(dgu_ref[...], wgu_ref[...])
        xhat1 = xh1_ref[...]
        gg1_ref[...] += jnp.sum(dh * xhat1, axis=0, keepdims=True)
        gb1_ref[...] += jnp.sum(dh, axis=0, keepdims=True)
        dz1 = _norm_bwd(dh * g1_ref[...], xhat1, rs1_ref[...])
        dz1_ref[...] = dz1
        dz1b = dz1.astype(bf16)
        dz1b_ref[...] = dz1b
        dm_ref[...] = _nt(dz1b, wo_ref[...])

    vec, vec_out = _resident((1, D_MODEL)), pl.BlockSpec((1, D_MODEL), lambda i: (0, 0))
    tokens_f32, tokens_bf16 = jax.ShapeDtypeStruct((s, D_MODEL), f32), jax.ShapeDtypeStruct((s, D_MODEL), bf16)
    sums = jax.ShapeDtypeStruct((1, D_MODEL), f32)
    return pl.pallas_call(
        body, grid=(s // TOKEN_TILE,),
        in_specs=[_rows(D_MODEL), _rows(D_MODEL), _rows(1), vec, _rows(2 * D_FF),
                  _resident((D_MODEL, 2 * D_FF)), _resident((D_FF, D_MODEL)), _rows(D_MODEL), _rows(1), vec,
                  _resident((D_MODEL, D_MODEL)), pl.BlockSpec(memory_space=pl.ANY)],
        out_specs=[_rows(D_MODEL), _rows(D_FF), _rows(2 * D_FF), _rows(D_MODEL), _rows(D_MODEL), _rows(D_MODEL),
                   vec_out, vec_out, vec_out, vec_out],
        out_shape=[tokens_bf16, jax.ShapeDtypeStruct((s, D_FF), bf16), jax.ShapeDtypeStruct((s, 2 * D_FF), bf16),
                   tokens_f32, tokens_bf16, tokens_f32, sums, sums, sums, sums],
        name="mlp_bwd", compiler_params=_params(("arbitrary",)))(
            dy, xhat2, rstd2, g2, gu, w_gu_b, w_down_b, xhat1, rstd1, g1, w_o_b, after)


def _dx(dz1, dparts, w_in_t, after):
    s = dz1.shape[0]
    n = len(dparts)
    ranges = _row_ranges(dparts)

    def body(*refs):
        d_refs = refs[:n]
        dz_ref, w_ref, _, dx_ref = refs[n:]
        acc = ALPHA * dz_ref[...]
        for d_ref, (lo, hi) in zip(d_refs, ranges):
            acc = acc + _mm(d_ref[...], w_ref[lo:hi, :])
        dx_ref[...] = acc

    return pl.pallas_call(
        body, grid=(s // TOKEN_TILE,),
        in_specs=[_rows(d.shape[1]) for d in dparts] + [_rows(D_MODEL), _resident((IN_W, D_MODEL)),
                                                        pl.BlockSpec(memory_space=pl.ANY)],
        out_specs=_rows(D_MODEL),
        out_shape=jax.ShapeDtypeStruct((s, D_MODEL), f32),
        name="dx", compiler_params=_params(("arbitrary",)))(*dparts, dz1, w_in_t, after)


def _weight_grad_rows(parts, b, bn):
    s, n_cols = b.shape
    n = len(parts)
    ranges = _row_ranges(parts)
    m = ranges[-1][1]

    def body(*refs):
        p_refs = refs[:n]
        b_ref, o_ref = refs[n:]
        for p_ref, (lo, hi) in zip(p_refs, ranges):
            o_ref[lo:hi, :] = _tn(p_ref[...], b_ref[...]).astype(bf16)

    return pl.pallas_call(
        body, grid=(n_cols // bn,),
        in_specs=[_resident(p.shape) for p in parts] + [pl.BlockSpec((s, bn), lambda j: (0, j))],
        out_specs=pl.BlockSpec((m, bn), lambda j: (0, j)),
        out_shape=jax.ShapeDtypeStruct((m, n_cols), bf16),
        name="weight_grad_rows", compiler_params=_params(("arbitrary",)))(*parts, b)


def _weight_grad(a, b, bm, bn, after):
    s, m = a.shape
    n = b.shape[1]

    def body(a_ref, b_ref, after_ref, o_ref):
        o_ref[...] = _tn(a_ref[...], b_ref[...]).astype(bf16)

    return pl.pallas_call(
        body, grid=(m // bm, n // bn),
        in_specs=[pl.BlockSpec((s, bm), lambda i, j: (0, i)), pl.BlockSpec((s, bn), lambda i, j: (0, j)),
                  pl.BlockSpec(memory_space=pl.ANY)],
        out_specs=pl.BlockSpec((bm, bn), lambda i, j: (i, j)),
        out_shape=jax.ShapeDtypeStruct((m, n), bf16),
        name="weight_grad", compiler_params=_params(("arbitrary", "arbitrary")))(a, b, after)


def _shift_down(a, k):
    row = lax.broadcasted_iota(jnp.int32, a.shape, 0)
    return jnp.where(row >= k, pltpu.roll(a, k, 0), 0.0)


def _shift_up(a, k):
    n = a.shape[0]
    row = lax.broadcasted_iota(jnp.int32, a.shape, 0)
    return jnp.where(row < n - k, pltpu.roll(a, n - k, 0), 0.0)


def _slab(s, block):
    return pl.BlockSpec((s, LANES), lambda k: (0, block + k))


def _conv_y(z, w):
    return w[0:1, :] * _shift_down(z, 2) + w[1:2, :] * _shift_down(z, 1) + w[2:3, :] * z


def _conv_fwd(proj, w_conv):
    s = proj.shape[0]

    def body(xa_ref, gb_ref, gc_ref, w_ref, o_ref):
        z = gc_ref[...] * xa_ref[...]
        o_ref[...] = (gb_ref[...] * _conv_y(z, w_ref[...])).astype(bf16)

    return pl.pallas_call(
        body, grid=(CONV_W // LANES,),
        in_specs=[_slab(s, BLK_XA), _slab(s, BLK_GB), _slab(s, BLK_GC), pl.BlockSpec((3, LANES), lambda k: (0, k))],
        out_specs=_slab(s, 0),
        out_shape=jax.ShapeDtypeStruct((s, CONV_W), bf16),
        name="conv_fwd", compiler_params=_params(("arbitrary",)))(proj, proj, proj, w_conv)


def _conv_bwd(proj, dmix, w_conv, after):
    s = proj.shape[0]

    def body(xa_ref, gb_ref, gc_ref, dy_ref, w_ref, after_ref, dxa_ref, dgb_ref, dgc_ref, dw_ref):
        xa = xa_ref[...]
        gc = gc_ref[...]
        w = w_ref[...]
        z = gc * xa
        dya = dy_ref[...]
        dgb_ref[...] = (dya * _conv_y(z, w)).astype(bf16)
        dy = dya * gb_ref[...]
        dz = w[2:3, :] * dy + w[1:2, :] * _shift_up(dy, 1) + w[0:1, :] * _shift_up(dy, 2)
        dxa_ref[...] = (dz * gc).astype(bf16)
        dgc_ref[...] = (dz * xa).astype(bf16)
        dw_ref[0:1, :] = jnp.sum(dy * _shift_down(z, 2), axis=0, keepdims=True)
        dw_ref[1:2, :] = jnp.sum(dy * _shift_down(z, 1), axis=0, keepdims=True)
        dw_ref[2:3, :] = jnp.sum(dy * z, axis=0, keepdims=True)

    out = jax.ShapeDtypeStruct((s, CONV_W), bf16)
    return pl.pallas_call(
        body, grid=(CONV_W // LANES,),
        in_specs=[_slab(s, BLK_XA), _slab(s, BLK_GB), _slab(s, BLK_GC), _slab(s, 0), pl.BlockSpec((3, LANES), lambda k: (0, k)),
                  pl.BlockSpec(memory_space=pl.ANY)],
        out_specs=[_slab(s, 0), _slab(s, 0), _slab(s, 0), pl.BlockSpec((3, LANES), lambda k: (0, k))],
        out_shape=[out, out, out, jax.ShapeDtypeStruct((3, CONV_W), f32)],
        name="conv_bwd", compiler_params=_params(("arbitrary",)))(proj, proj, proj, dmix, w_conv, after)


def _pool_window(k):
    lane = lax.broadcasted_iota(jnp.int32, (1, LANES), 1)
    low = lane < HEAD
    first = k == 0
    wlen = jnp.where(low, jnp.where(first, POOL_WINDOWS[0], POOL_WINDOWS[2]), jnp.where(first, POOL_WINDOWS[1], POOL_WINDOWS[3]))
    return wlen, low, first


def _pool_diff(p, k):
    wlen, low, first = _pool_window(k)
    s2 = p + _shift_down(p, 1)
    s4 = s2 + _shift_down(s2, 2)
    s8 = s4 + _shift_down(s4, 4)
    s16 = s8 + _shift_down(s8, 8)
    win = jnp.where(low, jnp.where(first, s2, s8), jnp.where(first, s4, s16))
    row = lax.broadcasted_iota(jnp.int32, p.shape, 0)
    count = jnp.minimum(row + 1, wlen).astype(f32)
    return win / count - p, count


def _pool_weight(w_ref):
    zero = jnp.zeros((HEAD, HEAD), f32)
    top = jnp.concatenate([w_ref[0], zero], axis=1)
    bottom = jnp.concatenate([zero, w_ref[1]], axis=1)
    return jnp.concatenate([top, bottom], axis=0).astype(bf16)


def _pool_fwd(proj, w_pool, pool_scale):
    s = proj.shape[0]

    def body(p_ref, w_ref, sc_ref, o_ref):
        d, _ = _pool_diff(p_ref[...], pl.program_id(0))
        o_ref[...] = (_mm(d.astype(bf16), _pool_weight(w_ref)) * sc_ref[...]).astype(bf16)

    return pl.pallas_call(
        body, grid=(POOL_W // LANES,),
        in_specs=[_slab(s, BLK_P), pl.BlockSpec((2, HEAD, HEAD), lambda k: (k, 0, 0)), pl.BlockSpec((1, LANES), lambda k: (0, k))],
        out_specs=_slab(s, 0),
        out_shape=jax.ShapeDtypeStruct((s, POOL_W), bf16),
        name="pool_fwd", compiler_params=_params(("arbitrary",)))(proj, w_pool, pool_scale)


def _pool_bwd(proj, dmix, w_pool, pool_scale):
    s = proj.shape[0]

    def body(p_ref, dy_ref, w_ref, sc_ref, dp_ref, dw_ref, dsc_ref):
        k = pl.program_id(0)
        d, count = _pool_diff(p_ref[...], k)
        wbd = _pool_weight(w_ref)
        db = d.astype(bf16)
        dyb = dy_ref[...]
        dsc_ref[...] = jnp.sum(dyb * _mm(db, wbd), axis=0, keepdims=True)
        dpre = (dyb * sc_ref[...]).astype(bf16)
        dwbd = _tn(db, dpre)
        dw_ref[0] = dwbd[:HEAD, :HEAD]
        dw_ref[1] = dwbd[HEAD:, HEAD:]
        dd = _nt(dpre, wbd)
        e = dd / count
        wlen, low, first = _pool_window(k)
        a2 = e + _shift_up(e, 1)
        a4 = a2 + _shift_up(a2, 2)
        a8 = a4 + _shift_up(a4, 4)
        a16 = a8 + _shift_up(a8, 8)
        back = jnp.where(low, jnp.where(first, a2, a8), jnp.where(first, a4, a16))
        dp_ref[...] = (back - dd).astype(bf16)

    return pl.pallas_call(
        body, grid=(POOL_W // LANES,),
        in_specs=[_slab(s, BLK_P), _slab(s, CONV_W // LANES), pl.BlockSpec((2, HEAD, HEAD), lambda k: (k, 0, 0)),
                  pl.BlockSpec((1, LANES), lambda k: (0, k))],
        out_specs=[_slab(s, 0), pl.BlockSpec((2, HEAD, HEAD), lambda k: (k, 0, 0)), pl.BlockSpec((1, LANES), lambda k: (0, k))],
        out_shape=[jax.ShapeDtypeStruct((s, POOL_W), bf16), jax.ShapeDtypeStruct((4, HEAD, HEAD), f32),
                   jax.ShapeDtypeStruct((1, POOL_W), f32)],
        name="pool_bwd", compiler_params=_params(("arbitrary",)))(proj, dmix, w_pool, pool_scale)


SGU_UNROLL = 4
INV_SQRT2 = 0.7071067811865476
INV_SQRT_2PI = 0.3989422804014327


def _gelu(x):
    return 0.5 * x * (1.0 + lax.erf(x * INV_SQRT2))


def _gelu_grad(x):
    return 0.5 * (1.0 + lax.erf(x * INV_SQRT2)) + x * (INV_SQRT_2PI * jnp.exp(-0.5 * x * x))


def _head_mean(a, low):
    s_low = jnp.sum(jnp.where(low, a, 0.0), axis=-1, keepdims=True)
    s_high = jnp.sum(jnp.where(low, 0.0, a), axis=-1, keepdims=True)
    return jnp.where(low, s_low, s_high) * (1.0 / HEAD)


def _tril():
    r = lax.broadcasted_iota(jnp.int32, (CHUNK, CHUNK), 0)
    c = lax.broadcasted_iota(jnp.int32, (CHUNK, CHUNK), 1)
    return r >= c


def _sgu_chunk(up, vp, g, wm0, wm1, b0, b1, low):
    ug = _gelu(up)
    vg = _gelu(vp)
    vc = vg - _head_mean(vg, low)
    rstd = lax.rsqrt(_head_mean(vc * vc, low) + LN_EPS)
    vn = vc * rstd
    vb = (vn * g).astype(bf16)
    mixed = jnp.where(low, _mm(wm0, vb) + b0, _mm(wm1, vb) + b1)
    return ug, vn, rstd, vb, mixed


def _sgu_specs(s):
    return [_slab(s, BLK_U), _slab(s, BLK_V), pl.BlockSpec((1, LANES), lambda k: (0, k)),
            pl.BlockSpec((2, CHUNK, CHUNK), lambda k: (k, 0, 0)), pl.BlockSpec((2, CHUNK, 1), lambda k: (k, 0, 0))]


def _sgu_fwd(proj, sgu_g, w_spatial, b_spatial3):
    s = proj.shape[0]

    def body(u_ref, v_ref, g_ref, w_ref, b_ref, o_ref):
        low = lax.broadcasted_iota(jnp.int32, (1, LANES), 1) < HEAD
        mask = _tril()
        wm0 = jnp.where(mask, w_ref[0], 0.0).astype(bf16)
        wm1 = jnp.where(mask, w_ref[1], 0.0).astype(bf16)
        g = g_ref[...]
        b0 = b_ref[0]
        b1 = b_ref[1]

        def chunk(n, carry):
            rows = pl.ds(pl.multiple_of(n * CHUNK, CHUNK), CHUNK)
            ug, _, _, _, mixed = _sgu_chunk(u_ref[rows, :], v_ref[rows, :], g, wm0, wm1, b0, b1, low)
            o_ref[rows, :] = (ug * mixed).astype(bf16)
            return carry

        lax.fori_loop(0, s // CHUNK, chunk, 0, unroll=SGU_UNROLL)

    return pl.pallas_call(
        body, grid=(SGU_W // LANES,),
        in_specs=_sgu_specs(s),
        out_specs=_slab(s, 0),
        out_shape=jax.ShapeDtypeStruct((s, SGU_W), bf16),
        name="sgu_fwd", compiler_params=_params(("arbitrary",)))(proj, proj, sgu_g, w_spatial, b_spatial3)


def _sgu_bwd(proj, dmix, sgu_g, w_spatial, b_spatial3):
    s = proj.shape[0]

    def body(u_ref, v_ref, g_ref, w_ref, b_ref, dy_ref, du_ref, dv_ref, dg_ref, dw_ref, db_ref):
        low = lax.broadcasted_iota(jnp.int32, (1, LANES), 1) < HEAD
        mask = _tril()
        w0 = jnp.where(mask, w_ref[0], 0.0)
        w1 = jnp.where(mask, w_ref[1], 0.0)
        wm0 = w0.astype(bf16)
        wm1 = w1.astype(bf16)
        wt0 = w0.T.astype(bf16)
        wt1 = w1.T.astype(bf16)
        g = g_ref[...]
        b0 = b_ref[0]
        b1 = b_ref[1]
        dg_ref[...] = jnp.zeros_like(dg_ref)
        dw_ref[...] = jnp.zeros_like(dw_ref)
        db_ref[...] = jnp.zeros_like(db_ref)

        def chunk(n, carry):
            rows = pl.ds(pl.multiple_of(n * CHUNK, CHUNK), CHUNK)
            up = u_ref[rows, :]
            vp = v_ref[rows, :]
            ug, vn, rstd, vb, mixed = _sgu_chunk(up, vp, g, wm0, wm1, b0, b1, low)
            dy = dy_ref[rows, :]
            du_ref[rows, :] = (dy * mixed * _gelu_grad(up)).astype(bf16)
            dmix_c = dy * ug
            db_ref[0] += jnp.sum(jnp.where(low, dmix_c, 0.0), axis=-1, keepdims=True)
            db_ref[1] += jnp.sum(jnp.where(low, 0.0, dmix_c), axis=-1, keepdims=True)
            dmb = dmix_c.astype(bf16)
            zero = jnp.zeros_like(dmb)
            dw_ref[0] += _nt(jnp.where(low, dmb, zero), vb)
            dw_ref[1] += _nt(jnp.where(low, zero, dmb), vb)
            dvnorm = jnp.where(low, _mm(wt0, dmb), _mm(wt1, dmb))
            dg_ref[...] += jnp.sum(dvnorm * vn, axis=0, keepdims=True)
            dvn = dvnorm * g
            dvg = rstd * (dvn - _head_mean(dvn, low) - vn * _head_mean(dvn * vn, low))
            dv_ref[rows, :] = (dvg * _gelu_grad(vp)).astype(bf16)
            return carry

        lax.fori_loop(0, s // CHUNK, chunk, 0, unroll=SGU_UNROLL)
        dw_ref[0] = jnp.where(mask, dw_ref[0], 0.0)
        dw_ref[1] = jnp.where(mask, dw_ref[1], 0.0)

    out = jax.ShapeDtypeStruct((s, SGU_W), bf16)
    return pl.pallas_call(
        body, grid=(SGU_W // LANES,),
        in_specs=_sgu_specs(s) + [_slab(s, (CONV_W + POOL_W) // LANES)],
        out_specs=[_slab(s, 0), _slab(s, 0), pl.BlockSpec((1, LANES), lambda k: (0, k)),
                   pl.BlockSpec((2, CHUNK, CHUNK), lambda k: (k, 0, 0)), pl.BlockSpec((2, CHUNK, 1), lambda k: (k, 0, 0))],
        out_shape=[out, out, jax.ShapeDtypeStruct((1, SGU_W), f32), jax.ShapeDtypeStruct((6, CHUNK, CHUNK), f32),
                   jax.ShapeDtypeStruct((6, CHUNK, 1), f32)],
        name="sgu_bwd", compiler_params=_params(("arbitrary",)))(proj, proj, sgu_g, w_spatial, b_spatial3, dmix)


def _fwd_mix(x, w, after):
    proj, xb = _proj(x, w["w_in"], after)
    mix = [_conv_fwd(proj, w["w_conv"]), _pool_fwd(proj, w["w_pool"], w["pool_scale"]),
           _sgu_fwd(proj, w["sgu_ln_g"], w["w_spatial"], w["b_spatial"])]
    xhat1, rstd1, hb = _wo_ln1(mix, x, w["w_o"], w["ln1_g"], w["ln1_b"])
    return dict(proj=proj, xb=xb, mix=mix, xhat1=xhat1, rstd1=rstd1, hb=hb)


def _fwd_mlp(sv, w, after):
    gu, xhat2, rstd2, y = _mlp_fwd(sv["xhat1"], w["ln1_g"], w["ln1_b"], w["w_gate_up"], w["w_down"], w["ln2_g"], w["ln2_b"], after)
    sv.update(gu=gu, xhat2=xhat2, rstd2=rstd2)
    return y


def _bwd_mlp(dy, w, sv, after, hook):
    dz2b, actb, dgub, dz1, dz1b, dmix, g_ln2_g, g_ln2_b, g_ln1_g, g_ln1_b = _mlp_bwd(
        dy, sv["xhat2"], sv["rstd2"], w["ln2_g"], sv["gu"], w["w_gate_up"], w["w_down"], sv["xhat1"], sv["rstd1"], w["ln1_g"],
        w["w_o"], after)
    after = hook(dz1)
    grads = dict(w_gate_up=_weight_grad(sv["hb"], dgub, 512, D_FF // 2, after),
                 w_down=_weight_grad(actb, dz2b, D_FF // 2, D_MODEL, after),
                 ln2_g=g_ln2_g, ln2_b=g_ln2_b, ln1_g=g_ln1_g, ln1_b=g_ln1_b)
    return (dz1, dz1b, dmix), grads


def _bwd_mix(dz, w, sv, after, hook):
    dz1, dz1b, dmix = dz
    dxa, dgb, dgc, g_conv = _conv_bwd(sv["proj"], dmix, w["w_conv"], after)
    dp, g_pool, g_pscale = _pool_bwd(sv["proj"], dmix, w["w_pool"], w["pool_scale"])
    du, dv, g_sgu_g, g_spatial, g_bsp = _sgu_bwd(sv["proj"], dmix, w["sgu_ln_g"], w["w_spatial"], w["b_spatial"])
    dparts = [dxa, dgb, dgc, dp, du, dv]
    dx = _dx(dz1, dparts, w["w_in"], hook(du))
    grads = dict(
        w_in=_weight_grad_rows(dparts, sv["xb"], 512), w_o=_weight_grad_rows(sv["mix"], dz1b, D_MODEL),
        w_conv=g_conv, w_pool=g_pool, pool_scale=g_pscale, sgu_ln_g=g_sgu_g, w_spatial=g_spatial,
        b_spatial=g_bsp.reshape(6, CHUNK))
    return dx, grads


def _local_step(x, target, layers):
    saved = []
    for w in layers:
        sv = _fwd_mix(x, w, x)
        x = _fwd_mlp(sv, w, x)
        saved.append(sv)
    dy, sq = _loss_head(x, target)
    grads = [None] * len(layers)
    for l in reversed(range(len(layers))):
        dz, g_mlp = _bwd_mlp(dy, layers[l], saved[l], sq, lambda a: a)
        dy, g_mix = _bwd_mix(dz, layers[l], saved[l], dz[0], lambda a: a)
        grads[l] = dict(g_mlp, **g_mix)
    return sq, dy, grads


ANY = pl.BlockSpec(memory_space=pl.ANY)


def _place():
    x, y, c = lax.axis_index("x"), lax.axis_index("y"), lax.axis_index("c")
    others = [(1 - x, y), (x, 1 - y), (1 - x, 1 - y)]
    return x, y, c, others


def _chip_index(cx, cy):
    return 2 * cx + cy


def _half(ref_rows, c):
    half = ref_rows // 2
    return pl.ds(pl.multiple_of(c * half, 8), half)


def _remote(src, dst, send_sem, recv_sem, device):
    return pltpu.make_async_remote_copy(src_ref=src, dst_ref=dst, send_sem=send_sem, recv_sem=recv_sem,
                                        device_id=device, device_id_type=MESH)


def _gather_shards(shards):
    n = len(shards)
    base, total = [], 0
    for s in shards:
        base.append(total)
        total += 6 * s.shape[0]

    def body(*refs):
        ins, outs = refs[:n], refs[n:2 * n]
        send, recv = refs[2 * n:]
        x, y, c, others = _place()
        me = _chip_index(x, y)
        sib = (x, y, 1 - c)
        sends = []
        for f in range(n):
            depth, rows = ins[f].shape[0], ins[f].shape[1]
            for l in range(depth):
                for k, (cx, cy) in enumerate(others):
                    sem = base[f] + 6 * l + k
                    cp = _remote(ins[f].at[l, _half(rows, c)], outs[f].at[l, me, _half(rows, c)],
                                 send.at[sem], recv.at[sem], (cx, cy, c))
                    cp.start()
                    sends.append(cp)
        for f in range(n):
            depth, rows = ins[f].shape[0], ins[f].shape[1]
            for l in range(depth):
                for k, (cx, cy) in enumerate(others):
                    sem = base[f] + 6 * l + k
                    landed = outs[f].at[l, _chip_index(cx, cy), _half(rows, c)]
                    _remote(landed, landed, send.at[sem], recv.at[sem], (cx, cy, c)).wait_recv()
                    cp = _remote(landed, landed, send.at[sem + 3], recv.at[sem + 3], sib)
                    cp.start()
                    sends.append(cp)
        for f in range(n):
            depth, rows = ins[f].shape[0], ins[f].shape[1]
            for l in range(depth):
                for k, (cx, cy) in enumerate(others):
                    sem = base[f] + 6 * l + k + 3
                    passed = outs[f].at[l, _chip_index(cx, cy), _half(rows, 1 - c)]
                    _remote(passed, passed, send.at[sem], recv.at[sem], sib).wait_recv()
        for cp in sends:
            cp.wait_send()

    gathered = pl.pallas_call(
        body, in_specs=[ANY] * n, out_specs=[ANY] * n,
        out_shape=[jax.ShapeDtypeStruct((s.shape[0], N_CHIPS) + s.shape[1:], s.dtype) for s in shards],
        scratch_shapes=[pltpu.SemaphoreType.DMA((total,)), pltpu.SemaphoreType.DMA((total,))],
        name="gather_shards")(*shards)
    return [_place_own(g, s) for g, s in zip(gathered, shards)]


def _scalar(value):
    return jnp.reshape(value, (1,)).astype(jnp.int32)


def _place_own(blocks, shard):
    depth, rows, cols = shard.shape

    def body(me_ref, b_ref, s_ref, o_ref):
        o_ref[...] = s_ref[...]

    return pl.pallas_call(
        body,
        grid_spec=pltpu.PrefetchScalarGridSpec(
            num_scalar_prefetch=1, grid=(depth,),
            in_specs=[ANY, pl.BlockSpec((None, rows, cols), lambda l, me: (l, 0, 0))],
            out_specs=pl.BlockSpec((None, None, rows, cols), lambda l, me: (l, me[0], 0, 0))),
        out_shape=jax.ShapeDtypeStruct(blocks.shape, blocks.dtype),
        input_output_aliases={1: 0},
        name="place_own", compiler_params=_params(("arbitrary",)))(
            _scalar(_chip_index(lax.axis_index("x"), lax.axis_index("y"))), blocks, shard)


HBM = pl.BlockSpec(memory_space=pltpu.HBM)
SEM = pl.BlockSpec(memory_space=pltpu.SEMAPHORE)
TOKEN = jax.ShapeDtypeStruct((8, LANES), f32)
SPLIT_COPY = pltpu.CompilerParams(has_side_effects=pltpu.SideEffectType.DATAFLOW_SIDE_EFFECTING)


def _in_hbm(a):
    return pltpu.with_memory_space_constraint(a, pltpu.HBM)


def _full_shape(shard, axis):
    rows, cols = shard.shape
    return (N_CHIPS * rows, cols) if axis == 0 else (rows, N_CHIPS * cols)


def _block_half(ref, axis, j, h):
    if axis == 0:
        rows = ref.shape[0] // N_CHIPS
        return ref.at[pl.ds(pl.multiple_of(j * rows + h * (rows // 2), 16), rows // 2), :]
    half, cols = ref.shape[0] // 2, ref.shape[1] // N_CHIPS
    return ref.at[pl.ds(pl.multiple_of(h * half, 16), half), pl.ds(pl.multiple_of(j * cols, LANES), cols)]


def _place_layer(shards, axes, after):
    n = len(shards)

    def body(me_ref, *refs):
        ins, outs = refs[n:2 * n], refs[2 * n + 1:]
        for f in range(n):
            outs[f][...] = ins[f][...]

    lands = [lax.empty(_full_shape(s, ax), s.dtype) for s, ax in zip(shards, axes)]
    return pl.pallas_call(
        body,
        grid_spec=pltpu.PrefetchScalarGridSpec(
            num_scalar_prefetch=1, grid=(1,),
            in_specs=[ANY] * n + [pl.BlockSpec(s.shape, lambda i, me: (0, 0)) for s in shards] + [ANY],
            out_specs=[pl.BlockSpec(s.shape, (lambda i, me: (me[0], 0)) if ax == 0 else (lambda i, me: (0, me[0])))
                       for s, ax in zip(shards, axes)]),
        out_shape=[jax.ShapeDtypeStruct(a.shape, a.dtype) for a in lands],
        input_output_aliases={1 + f: f for f in range(n)},
        name="place_layer", compiler_params=_params(("arbitrary",)))(
            _scalar(_chip_index(lax.axis_index("x"), lax.axis_index("y"))), *lands, *shards, after)


def _gather_start(shards, lands, axes, after):
    return _split_copy_start("gather", _gather_plan(axes), 3 * len(shards), shards, lands, after)


def _gather_wait(state, axes, after):
    return _split_copy_wait("gather", _gather_plan(axes), state, after)


SIBLING_PAIR_ID = 0


def _split_copy_start(name, plan, count, ins, lands, after, sibling_only=False):
    arrays = list(ins) + list(lands)
    n_in, n = len(ins), len(arrays)

    def body(*refs):
        send, recv, token = refs[n + 1], refs[n + 2], refs[-1]
        if sibling_only:
            x, y, c, _ = _place()
            barrier = pltpu.get_barrier_semaphore()
            pl.semaphore_signal(barrier, inc=1, device_id=(x, y, 1 - c), device_id_type=MESH)
            pl.semaphore_wait(barrier, 1)
        for i, (src, dst, _, peer) in enumerate(plan(refs[:n_in], refs[n_in:n])):
            _remote(src, dst, send.at[i], recv.at[i], peer).start()
        token[...] = jnp.zeros_like(token)

    effect = pltpu.SideEffectType.DATAFLOW_SIDE_EFFECTING
    outs = pl.pallas_call(
        body, name=name + "_start",
        in_specs=[HBM] * n + [ANY],
        out_specs=(SEM, SEM, *[HBM] * n, pl.BlockSpec(memory_space=pltpu.VMEM)),
        out_shape=(pltpu.SemaphoreType.DMA((count,)), pltpu.SemaphoreType.DMA((count,)),
                   *[pltpu.HBM(a.shape, a.dtype) for a in arrays], TOKEN),
        input_output_aliases={i: 2 + i for i in range(n)},
        compiler_params=pltpu.CompilerParams(has_side_effects=effect, collective_id=SIBLING_PAIR_ID) if sibling_only
        else SPLIT_COPY)(*[_in_hbm(a) for a in arrays], after)
    return (outs[0], outs[1], outs[2:2 + n_in], outs[2 + n_in:2 + n]), outs[-1]


def _split_copy_wait(name, plan, state, after):
    send_sems, recv_sems, ins, lands = state
    arrays = list(ins) + list(lands)
    n_in, n = len(ins), len(arrays)

    def body(*refs):
        send, recv, token = refs[n], refs[n + 1], refs[-1]
        for i, (src, _, landing, peer) in enumerate(plan(refs[:n_in], refs[n_in:n])):
            cp = _remote(src, landing, send.at[i], recv.at[i], peer)
            cp.wait_send()
            cp.wait_recv()
        token[...] = jnp.zeros_like(token)

    outs = pl.pallas_call(
        body, name=name + "_wait",
        in_specs=[HBM] * n + [SEM, SEM, ANY],
        out_specs=(*[HBM] * n, pl.BlockSpec(memory_space=pltpu.VMEM)),
        out_shape=(*[pltpu.HBM(a.shape, a.dtype) for a in arrays], TOKEN),
        input_output_aliases={i: i for i in range(n)},
        compiler_params=SPLIT_COPY)(*arrays, send_sems, recv_sems, after)
    return outs[:n_in], outs[n_in:n], outs[-1]


def _gather_plan(axes):
    def plan(ins, lnd):
        x, y, c, others = _place()
        me = _chip_index(x, y)
        return [(ins[f].at[_half(ins[f].shape[0], c)], _block_half(lnd[f], ax, me, c),
                 _block_half(lnd[f], ax, _chip_index(cx, cy), c), (cx, cy, c))
                for f, ax in enumerate(axes) for cx, cy in others]
    return plan


def _pair_plan(axes):
    def plan(ins, lnd):
        x, y, c, _ = _place()
        return [(_block_half(ins[f], ax, j, 1 - c), lnd[f].at[j], lnd[f].at[j], (x, y, 1 - c))
                for f, ax in enumerate(axes) for j in range(N_CHIPS)]
    return plan


def _scatter_plan(ins, lnd):
    x, y, c, others = _place()
    return [(ins[f].at[_chip_index(cx, cy)], lnd[f].at[k], lnd[f].at[k], (cx, cy, c))
            for f in range(len(ins)) for k, (cx, cy) in enumerate(others)]


def _join_plan(ins, lnd):
    x, y, c, _ = _place()
    return [(lnd[f].at[_half(lnd[f].shape[0], c)], lnd[f].at[_half(lnd[f].shape[0], c)],
             lnd[f].at[_half(lnd[f].shape[0], 1 - c)], (x, y, 1 - c)) for f in range(len(lnd))]


def _gather_finish(lands, axes, after):
    n = len(lands)

    def body(*refs):
        outs = refs[n + 1:2 * n + 1]
        send, recv = refs[2 * n + 1:]
        x, y, c, others = _place()
        sib = (x, y, 1 - c)
        sends = []
        for f in range(n):
            for k, (cx, cy) in enumerate(others):
                landed = _block_half(outs[f], axes[f], _chip_index(cx, cy), c)
                cp = _remote(landed, landed, send.at[3 * f + k], recv.at[3 * f + k], sib)
                cp.start()
                sends.append(cp)
        for f in range(n):
            for k, (cx, cy) in enumerate(others):
                passed = _block_half(outs[f], axes[f], _chip_index(cx, cy), 1 - c)
                _remote(passed, passed, send.at[3 * f + k], recv.at[3 * f + k], sib).wait_recv()
        for cp in sends:
            cp.wait_send()

    return pl.pallas_call(
        body, in_specs=[ANY] * (n + 1), out_specs=[ANY] * n,
        out_shape=[jax.ShapeDtypeStruct(a.shape, a.dtype) for a in lands],
        input_output_aliases={f: f for f in range(n)},
        scratch_shapes=[pltpu.SemaphoreType.DMA((3 * n,)), pltpu.SemaphoreType.DMA((3 * n,))],
        name="gather_finish")(*lands, after)


def _half_blocks(part, axis):
    rows, cols = (part.shape[0] // N_CHIPS, part.shape[1]) if axis == 0 else (part.shape[0], part.shape[1] // N_CHIPS)
    return lax.empty((N_CHIPS, rows // 2, cols), part.dtype)


def _add_pair_layer(parts, gots, axes):
    k = len(parts)

    def body(c_ref, *refs):
        for f in range(k):
            a_ref, b_ref, o_ref = refs[2 * f], refs[2 * f + 1], refs[2 * k + f]
            o_ref[...] = (a_ref[...].astype(f32) + b_ref[...].astype(f32)).astype(o_ref.dtype)

    in_specs, out_specs, operands = [], [], []
    for part, got, axis in zip(parts, gots, axes):
        _, half, cols = got.shape
        if axis == 0:
            part = part.reshape(N_CHIPS, 2, half, cols)
            mine = pl.BlockSpec((None, None, half, cols), lambda j, c: (j, c[0], 0, 0))
        else:
            mine = pl.BlockSpec((half, cols), lambda j, c: (c[0], j))
        block = pl.BlockSpec((None, half, cols), lambda j, c: (j, 0, 0))
        in_specs += [mine, block]
        out_specs.append(block)
        operands += [part, got]
    return pl.pallas_call(
        body,
        grid_spec=pltpu.PrefetchScalarGridSpec(num_scalar_prefetch=1, grid=(N_CHIPS,), in_specs=in_specs, out_specs=out_specs),
        out_shape=[jax.ShapeDtypeStruct(g.shape, p.dtype) for p, g in zip(parts, gots)],
        name="add_pair_layer", compiler_params=_params(("arbitrary",)))(_scalar(lax.axis_index("c")), *operands)


def _scatter_start(sums, after):
    lands = [lax.empty((3,) + s.shape[1:], s.dtype) for s in sums]
    return _split_copy_start("scatter", _scatter_plan, 3 * len(sums), sums, lands, after)


def _scatter_wait(state, after):
    return _split_copy_wait("scatter", _scatter_plan, state, after)


ELEMENTWISE_BLOCK_BYTES = 1 << 20


def _row_tile(rows, cols):
    best = None
    for tile in range(8, rows + 1, 8):
        if rows % tile == 0 and tile * cols * 4 <= ELEMENTWISE_BLOCK_BYTES:
            best = tile
    return best or rows


def _add_slots(chip_sums, slots):
    k = len(chip_sums)

    def body(at_ref, *refs):
        for f in range(k):
            own_ref, s_ref, o_ref = refs[2 * f], refs[2 * f + 1], refs[2 * k + f]
            acc = own_ref[...].astype(f32)
            for j in range(3):
                acc = acc + s_ref[j].astype(f32)
            o_ref[...] = acc

    in_specs, out_specs, operands = [], [], []
    for cs, s in zip(chip_sums, slots):
        _, half, cols = cs.shape
        in_specs += [pl.BlockSpec((None, half, cols), lambda i, at: (at[0], 0, 0)), pl.BlockSpec((3, half, cols), lambda i, at: (0, 0, 0))]
        out_specs.append(pl.BlockSpec((None, half, cols), lambda i, at: (at[1], 0, 0)))
        operands += [cs, s]
    at = jnp.concatenate([_scalar(_chip_index(lax.axis_index("x"), lax.axis_index("y"))), _scalar(lax.axis_index("c"))])
    outs = pl.pallas_call(
        body,
        grid_spec=pltpu.PrefetchScalarGridSpec(num_scalar_prefetch=1, grid=(1,), in_specs=in_specs, out_specs=out_specs),
        out_shape=[jax.ShapeDtypeStruct((2,) + cs.shape[1:], f32) for cs in chip_sums],
        name="add_slots", compiler_params=_params(("arbitrary",)))(at, *operands)
    return [o.reshape(2 * o.shape[1], o.shape[2]) for o in outs]


def _adamw_math(w, grad, m, v):
    nm = ADAM_B1 * m + (1.0 - ADAM_B1) * grad
    nv = ADAM_B2 * v + (1.0 - ADAM_B2) * (grad * grad)
    m_hat = nm / (1.0 - ADAM_B1 ** ADAM_STEP)
    v_hat = nv / (1.0 - ADAM_B2 ** ADAM_STEP)
    return nm, nv, -ADAM_LR * (m_hat / (jnp.sqrt(v_hat) + ADAM_EPS) + ADAM_WD * w)


def _adamw(w, g, m, v):
    shape = w.shape
    flat = [a.reshape(-1, shape[-1]) for a in (w, g, m, v)]
    tile = _row_tile(flat[0].shape[0], shape[-1])

    def body(w_ref, g_ref, m_ref, v_ref, d_ref, nm_ref, nv_ref):
        nm, nv, step = _adamw_math(w_ref[...], g_ref[...], m_ref[...], v_ref[...])
        d_ref[...] = step
        nm_ref[...] = nm
        nv_ref[...] = nv

    spec = _rows(shape[-1], tile)
    out = jax.ShapeDtypeStruct(flat[0].shape, f32)
    res = pl.pallas_call(
        body, grid=(flat[0].shape[0] // tile,),
        in_specs=[spec] * 4, out_specs=[spec] * 3, out_shape=[out] * 3,
        name="adamw", compiler_params=_params(("arbitrary",)))(*flat)
    return [r.reshape(shape) for r in res]


def _adamw_layer(l, ws, ms, vs, gs, outs, steps, after):
    k = len(ws)

    def body(*refs):
        ins, new = refs[:4 * k], refs[8 * k + 1:]
        for f in range(k):
            w_ref, m_ref, v_ref, g_ref = ins[4 * f:4 * f + 4]
            go_ref, d_ref, nm_ref, nv_ref = new[4 * f:4 * f + 4]
            grad = g_ref[...]
            nm, nv, step = _adamw_math(w_ref[...], grad, m_ref[...], v_ref[...])
            go_ref[...] = grad
            d_ref[...] = step
            nm_ref[...] = nm
            nv_ref[...] = nv

    in_specs, out_specs, operands = [], [], []
    for w, m, v, g in zip(ws, ms, vs, gs):
        _, rows, cols = w.shape
        tile = rows // steps
        layer = pl.BlockSpec((None, tile, cols), lambda i: (l, i, 0))
        in_specs += [layer] * 3 + [_rows(cols, tile)]
        out_specs += [layer] * 4
        operands += [w, m, v, g]
    flat_outs = [o for four in outs for o in four]
    res = pl.pallas_call(
        body, grid=(steps,),
        in_specs=in_specs + [ANY] * (4 * k + 1), out_specs=out_specs,
        out_shape=[jax.ShapeDtypeStruct(o.shape, f32) for o in flat_outs],
        input_output_aliases={4 * k + j: j for j in range(4 * k)},
        name="adamw_layer", compiler_params=_params(("arbitrary",)))(*operands, *flat_outs, after)
    return [res[4 * f:4 * f + 4] for f in range(k)]


SMALL = ("w_conv", "w_pool", "pool_scale", "sgu_ln_g", "w_spatial", "b_spatial", "ln1_g", "ln1_b", "ln2_g", "ln2_b")
WEIGHTS = ("w_in", "w_conv", "w_pool", "pool_scale", "sgu_ln_g", "w_spatial", "b_spatial", "w_o", "ln1_g", "ln1_b",
           "w_gate_up", "w_down", "ln2_g", "ln2_b")
BIG = ("w_in", "w_o", "w_gate_up", "w_down")
GROUPS = (("w_in", "w_o"), ("w_gate_up", "w_down"))
GROUP_AXES = ((0, 0), (1, 0))
SCATTER_HOOKS = 2
ADAMW_STEPS = (2, 4)
SMALL_LAYER_ROWS = 1024


def _pack_layer(arrays):
    flat = jnp.concatenate([a.reshape(-1) for a in arrays])
    return jnp.pad(flat, (0, SMALL_LAYER_ROWS * LANES - flat.shape[0])).reshape(SMALL_LAYER_ROWS, LANES)


def _unpack_layers(flat, shapes):
    out, at = {}, 0
    for name, shape in shapes.items():
        size = 1
        for d in shape:
            size *= d
        out[name] = flat[:, at:at + size].reshape((flat.shape[0],) + tuple(shape))
        at += size
    return out


def kernel(x, w_in, w_conv, w_pool, pool_scale, sgu_ln_g, w_spatial, b_spatial, w_o, ln1_g, ln1_b, w_gate_up, w_down, ln2_g, ln2_b, loss_target, m_w_in, m_w_conv, m_w_pool, m_pool_scale, m_sgu_ln_g, m_w_spatial, m_b_spatial, m_w_o, m_ln1_g, m_ln1_b, m_w_gate_up, m_w_down, m_ln2_g, m_ln2_b, v_w_in, v_w_conv, v_w_pool, v_pool_scale, v_sgu_ln_g, v_w_spatial, v_b_spatial, v_w_o, v_ln1_g, v_ln1_b, v_w_gate_up, v_w_down, v_ln2_g, v_ln2_b):
    weights = dict(w_in=w_in, w_conv=w_conv, w_pool=w_pool, pool_scale=pool_scale, sgu_ln_g=sgu_ln_g, w_spatial=w_spatial,
                   b_spatial=b_spatial, w_o=w_o, ln1_g=ln1_g, ln1_b=ln1_b, w_gate_up=w_gate_up, w_down=w_down, ln2_g=ln2_g, ln2_b=ln2_b)
    m_in = dict(w_in=m_w_in, w_conv=m_w_conv, w_pool=m_w_pool, pool_scale=m_pool_scale, sgu_ln_g=m_sgu_ln_g, w_spatial=m_w_spatial,
                b_spatial=m_b_spatial, w_o=m_w_o, ln1_g=m_ln1_g, ln1_b=m_ln1_b, w_gate_up=m_w_gate_up, w_down=m_w_down,
                ln2_g=m_ln2_g, ln2_b=m_ln2_b)
    v_in = dict(w_in=v_w_in, w_conv=v_w_conv, w_pool=v_w_pool, pool_scale=v_pool_scale, sgu_ln_g=v_sgu_ln_g, w_spatial=v_w_spatial,
                b_spatial=v_b_spatial, w_o=v_w_o, ln1_g=v_ln1_g, ln1_b=v_ln1_b, w_gate_up=v_w_gate_up, w_down=v_w_down,
                ln2_g=v_ln2_g, ln2_b=v_ln2_b)
    depth = w_in.shape[0]
    conv_cols = w_conv.shape[2]
    chip = _chip_index(lax.axis_index("x"), lax.axis_index("y"))

    conv_flat = jnp.pad(w_conv.reshape(-1), (0, 16 * LANES - w_conv.size)).reshape(1, 16, LANES)
    conv_full = _gather_shards([conv_flat])[0].reshape(N_CHIPS, 16 * LANES)[:, :w_conv.size].reshape(N_CHIPS, depth, 3, conv_cols)
    conv_full = conv_full.transpose(1, 2, 0, 3).reshape(depth, 3, N_CHIPS * conv_cols)

    big_w = dict(w_in=jnp.swapaxes(w_in, 1, 2), w_o=w_o, w_gate_up=w_gate_up, w_down=w_down)
    big_m = dict(w_in=jnp.swapaxes(m_w_in, 1, 2), w_o=m_w_o, w_gate_up=m_w_gate_up, w_down=m_w_down)
    big_v = dict(w_in=jnp.swapaxes(v_w_in, 1, 2), w_o=v_w_o, w_gate_up=v_w_gate_up, w_down=v_w_down)

    def shards_of(l, g):
        return [big_w[n][l].astype(bf16) for n in GROUPS[g]]

    def send(l, g, after):
        return _gather_start(shards_of(l, g), placed[l, g], GROUP_AXES[g], after)

    stages = [(l, g) for l in range(depth) for g in (0, 1)]
    placed, flights = {}, {}
    token = conv_full
    for st in stages[:2]:
        placed[st] = _place_layer(shards_of(*st), GROUP_AXES[st[1]], token)
        flights[st], token = send(*st, placed[st][0])
    recent = token
    for st in stages[2:]:
        placed[st] = _place_layer(shards_of(*st), GROUP_AXES[st[1]], token)
        recent = placed[st][0]
    act = x[0]
    layers, saved = [], []
    for i, (l, g) in enumerate(stages):
        if g == 0:
            w = dict(w_conv=conv_full[l], w_pool=w_pool[l], pool_scale=pool_scale[l][None], sgu_ln_g=sgu_ln_g[l][None],
                     w_spatial=w_spatial[l], b_spatial=b_spatial[l][:, :, None], ln1_g=ln1_g[l][None], ln1_b=ln1_b[l][None],
                     ln2_g=ln2_g[l][None], ln2_b=ln2_b[l][None])
        _, lands, token = _gather_wait(flights[l, g], GROUP_AXES[g], recent)
        if i + 2 < len(stages):
            flights[stages[i + 2]], token = send(*stages[i + 2], token)
        w.update(zip(GROUPS[g], _gather_finish(lands, GROUP_AXES[g], token)))
        if g == 0:
            sv = _fwd_mix(act, w, token)
            recent = sv["xhat1"]
        else:
            act = recent = _fwd_mlp(sv, w, token)
            layers.append(w)
            saved.append(sv)

    big_outs = {n: [lax.empty(big_w[n].shape, f32) for _ in range(4)] for n in BIG}
    small_sums = [None] * depth
    pending, updates = [], []
    latest = dict(token=None)

    def begin(l, g, parts):
        axes = GROUP_AXES[g] + (0,) * (len(parts) - len(GROUPS[g]))
        lands = [_half_blocks(p, ax) for p, ax in zip(parts, axes)]
        flight, latest["token"] = _split_copy_start("pair", _pair_plan(axes), N_CHIPS * len(parts), parts, lands, latest["token"],
                                                    sibling_only=True)
        pending.append(dict(l=l, g=g, axes=axes, step="pair", age=0, flight=flight))

    def advance(st, recent):
        if st["step"] == "pair":
            parts, got, _ = _split_copy_wait("pair", _pair_plan(st["axes"]), st["flight"], recent)
            sums = _add_pair_layer(parts, got, st["axes"])
            st["flight"], latest["token"] = _scatter_start(sums, latest["token"])
            st["step"] = "scatter"
        elif st["step"] == "scatter":
            sums, slots, _ = _scatter_wait(st["flight"], recent)
            filled = _add_slots(sums, slots)
            st["flight"], latest["token"] = _split_copy_start("join", _join_plan, len(filled), [], filled, latest["token"],
                                                              sibling_only=True)
            st["step"] = "join"
        else:
            _, summed, _ = _split_copy_wait("join", _join_plan, st["flight"], recent)
            updates.append((st["l"], st["g"], summed[:len(GROUPS[st["g"]])]))
            if st["g"] == 0:
                small_sums[st["l"]] = summed[-1]
            st["step"] = "done"
        st["age"] = 0

    def hook(recent):
        for st in reversed(list(pending)):
            st["age"] += 1
            if st["age"] >= SCATTER_HOOKS or st["step"] != "scatter":
                advance(st, recent)
                if st["step"] == "done":
                    pending.remove(st)
        return latest["token"]

    def update(count, recent):
        for l, g, totals in updates[:count]:
            names = GROUPS[g]
            new = _adamw_layer(l, [big_w[n] for n in names], [big_m[n] for n in names], [big_v[n] for n in names], totals,
                               [big_outs[n] for n in names], ADAMW_STEPS[g], latest["token"])
            big_outs.update(zip(names, new))
            recent = new[-1][1]
        del updates[:count]
        return recent

    grad_x, sq = _loss_head(act, loss_target[0])
    latest["token"] = sq
    grads = [None] * depth
    for l in reversed(range(depth)):
        dz, g_mlp = _bwd_mlp(grad_x, layers[l], saved[l], latest["token"], hook)
        hook(g_mlp["w_down"])
        begin(l, 1, [g_mlp[n] for n in GROUPS[1]])
        grad_x, g_mix = _bwd_mix(dz, layers[l], saved[l], latest["token"], hook)
        grads[l] = dict(g_mlp, **g_mix)
        hook(g_mix["w_o"])
        begin(l, 0, [g_mix[n] for n in GROUPS[0]] + [_pack_layer([grads[l][n] for n in SMALL])])
    recent = g_mix["w_o"]
    while pending:
        recent = update(-(-2 * len(updates) // 3), recent)
        hook(recent)
    update(len(updates), recent)
    loss = lax.psum(0.5 / D_MODEL * jnp.sum(sq), ("x", "y", "c"))

    small_sum = _gather_shards([jnp.stack(small_sums)])[0].reshape(depth, SMALL_LAYER_ROWS * LANES)
    grad = {n: [jnp.swapaxes(o, 1, 2) for o in big_outs[n]] if n == "w_in" else big_outs[n] for n in BIG}
    delta = {n: o[1] for n, o in grad.items()}
    new_m = {n: o[2] for n, o in grad.items()}
    new_v = {n: o[3] for n, o in grad.items()}
    grad = {n: o[0] for n, o in grad.items()}
    grad.update(_unpack_layers(small_sum, {n: (3, N_CHIPS * conv_cols) if n == "w_conv" else weights[n].shape[1:] for n in SMALL}))
    grad["w_conv"] = lax.dynamic_slice_in_dim(grad["w_conv"], chip * conv_cols, conv_cols, axis=2)

    delta["w_conv"], new_m["w_conv"], new_v["w_conv"] = _adamw(w_conv, grad["w_conv"], m_w_conv, v_w_conv)
    rest = [n for n in SMALL if n != "w_conv"]
    rest_shapes = {n: weights[n].shape[1:] for n in rest}
    packed = [jnp.concatenate([_pack_layer([src[n][l] for n in rest]) for l in range(depth)]) for src in (weights, grad, m_in, v_in)]
    for dst, res in zip((delta, new_m, new_v), _adamw(*packed)):
        dst.update(_unpack_layers(res.reshape(depth, SMALL_LAYER_ROWS * LANES), rest_shapes))

    return (loss, grad_x[None], *[grad[n] for n in WEIGHTS], *[delta[n] for n in WEIGHTS],
            *[new_m[n] for n in WEIGHTS], *[new_v[n] for n in WEIGHTS])
```

```python
import functools

import jax
import jax.numpy as jnp
from jax import lax
from jax.experimental import pallas as pl
from jax.experimental.pallas import tpu as pltpu

f32 = jnp.float32
bf16 = jnp.bfloat16

D_MODEL = 1024
DEPTH = 4
CONV_W = 384
POOL_W = 256
SGU_W = 384
IN_W = 3 * CONV_W + POOL_W + 2 * SGU_W
D_FF = 2816
CHUNK = 128
HEAD = 64
POOL_WINDOWS = (2, 4, 8, 16)
ALPHA = float((2 * DEPTH) ** 0.25)
LN_EPS = 1e-5
ADAM_LR = 0.001
ADAM_B1 = 0.9
ADAM_B2 = 0.999
ADAM_EPS = 1e-08
ADAM_WD = 0.01
ADAM_STEP = 10

LANES = 128
TOKEN_TILE = 256
N_CHIPS = 4
VMEM_LIMIT = 56 * 1024 * 1024

BLK_XA, BLK_GB, BLK_GC, BLK_P, BLK_U, BLK_V = 0, 3, 6, 9, 11, 14

MESH = pl.DeviceIdType.MESH


def _params(sem=None):
    return pltpu.CompilerParams(dimension_semantics=sem, vmem_limit_bytes=VMEM_LIMIT)


def _rows(width, tile=TOKEN_TILE):
    return pl.BlockSpec((tile, width), lambda i: (i, 0))


def _resident(shape):
    zeros = (0,) * len(shape)
    return pl.BlockSpec(shape, lambda *_: zeros, pipeline_mode=pl.Buffered(1))


def _nt(a, b):
    return lax.dot_general(a, b, (((1,), (1,)), ((), ())), preferred_element_type=f32)


def _tn(a, b):
    return lax.dot_general(a, b, (((0,), (0,)), ((), ())), preferred_element_type=f32)


def _mm(a, b):
    return jnp.dot(a, b, preferred_element_type=f32)


def _norm_fwd(z):
    mu = jnp.mean(z, axis=-1, keepdims=True)
    zc = z - mu
    var = jnp.mean(zc * zc, axis=-1, keepdims=True)
    rstd = lax.rsqrt(var + LN_EPS)
    return zc * rstd, rstd


def _norm_bwd(dxhat, xhat, rstd):
    m1 = jnp.mean(dxhat, axis=-1, keepdims=True)
    m2 = jnp.mean(dxhat * xhat, axis=-1, keepdims=True)
    return rstd * (dxhat - m1 - xhat * m2)


def _proj(x, w_in_b, after):
    s = x.shape[0]

    def body(x_ref, w_ref, after_ref, p_ref, xb_ref):
        xb = x_ref[...].astype(bf16)
        xb_ref[...] = xb
        p_ref[...] = _nt(xb, w_ref[...])

    return pl.pallas_call(
        body, grid=(s // TOKEN_TILE,),
        in_specs=[_rows(D_MODEL), _resident((IN_W, D_MODEL)), pl.BlockSpec(memory_space=pl.ANY)],
        out_specs=[_rows(IN_W), _rows(D_MODEL)],
        out_shape=[jax.ShapeDtypeStruct((s, IN_W), f32), jax.ShapeDtypeStruct((s, D_MODEL), bf16)],
        name="proj", compiler_params=_params(("arbitrary",)))(x, w_in_b, after)


def _row_ranges(parts):
    out, at = [], 0
    for p in parts:
        out.append((at, at + p.shape[1]))
        at += p.shape[1]
    return out


def _wo_ln1(mix, x, w_o_b, g, b):
    s = x.shape[0]
    n = len(mix)
    ranges = _row_ranges(mix)

    def body(*refs):
        m_refs = refs[:n]
        x_ref, w_ref, g_ref, b_ref, xhat_ref, rstd_ref, hb_ref = refs[n:]
        z = ALPHA * x_ref[...]
        for m_ref, (lo, hi) in zip(m_refs, ranges):
            z = z + _mm(m_ref[...], w_ref[lo:hi, :])
        xhat, rstd = _norm_fwd(z)
        xhat_ref[...] = xhat
        rstd_ref[...] = rstd
        hb_ref[...] = (xhat * g_ref[...] + b_ref[...]).astype(bf16)

    return pl.pallas_call(
        body, grid=(s // TOKEN_TILE,),
        in_specs=[_rows(m.shape[1]) for m in mix] + [_rows(D_MODEL), _resident((D_MODEL, D_MODEL)), _resident((1, D_MODEL)),
                                                     _resident((1, D_MODEL))],
        out_specs=[_rows(D_MODEL), _rows(1), _rows(D_MODEL)],
        out_shape=[jax.ShapeDtypeStruct((s, D_MODEL), f32), jax.ShapeDtypeStruct((s, 1), f32),
                   jax.ShapeDtypeStruct((s, D_MODEL), bf16)],
        name="wo_ln1", compiler_params=_params(("arbitrary",)))(*mix, x, w_o_b, g, b)


def _mlp_fwd(xhat1, g1, b1, w_gu_b, w_down_b, g2, b2, after):
    s = xhat1.shape[0]

    def body(xh_ref, g1_ref, b1_ref, wgu_ref, wd_ref, g2_ref, b2_ref, after_ref, gu_ref, xhat2_ref, rstd2_ref, y_ref):
        h = xh_ref[...] * g1_ref[...] + b1_ref[...]
        gu = _mm(h.astype(bf16), wgu_ref[...])
        gu_ref[...] = gu
        gate = gu[:, :D_FF]
        act = gate * jax.nn.sigmoid(gate) * gu[:, D_FF:]
        z = ALPHA * h + _mm(act.astype(bf16), wd_ref[...])
        xhat2, rstd2 = _norm_fwd(z)
        xhat2_ref[...] = xhat2
        rstd2_ref[...] = rstd2
        y_ref[...] = xhat2 * g2_ref[...] + b2_ref[...]

    vec = _resident((1, D_MODEL))
    return pl.pallas_call(
        body, grid=(s // TOKEN_TILE,),
        in_specs=[_rows(D_MODEL), vec, vec, _resident((D_MODEL, 2 * D_FF)), _resident((D_FF, D_MODEL)), vec, vec,
                  pl.BlockSpec(memory_space=pl.ANY)],
        out_specs=[_rows(2 * D_FF), _rows(D_MODEL), _rows(1), _rows(D_MODEL)],
        out_shape=[jax.ShapeDtypeStruct((s, 2 * D_FF), f32), jax.ShapeDtypeStruct((s, D_MODEL), f32),
                   jax.ShapeDtypeStruct((s, 1), f32), jax.ShapeDtypeStruct((s, D_MODEL), f32)],
        name="mlp_fwd", compiler_params=_params(("arbitrary",)))(xhat1, g1, b1, w_gu_b, w_down_b, g2, b2, after)


def _loss_head(y, target):
    s = y.shape[0]

    def body(y_ref, t_ref, dy_ref, sq_ref):
        @pl.when(pl.program_id(0) == 0)
        def _():
            sq_ref[...] = jnp.zeros_like(sq_ref)

        e = y_ref[...] - t_ref[...]
        dy_ref[...] = e * (1.0 / D_MODEL)
        sq_ref[...] += jnp.sum(e * e, axis=0, keepdims=True)

    return pl.pallas_call(
        body, grid=(s // TOKEN_TILE,),
        in_specs=[_rows(D_MODEL), _rows(D_MODEL)],
        out_specs=[_rows(D_MODEL), pl.BlockSpec((1, D_MODEL), lambda i: (0, 0))],
        out_shape=[jax.ShapeDtypeStruct((s, D_MODEL), f32), jax.ShapeDtypeStruct((1, D_MODEL), f32)],
        name="loss_head", compiler_params=_params(("arbitrary",)))(y, target)


def _mlp_bwd(dy, xhat2, rstd2, g2, gu, w_gu_b, w_down_b, xhat1, rstd1, g1, w_o_b, after):
    s = dy.shape[0]

    def body(dy_ref, xh_ref, rs_ref, g2_ref, gu_ref, wgu_ref, wd_ref, xh1_ref, rs1_ref, g1_ref, wo_ref, after_ref,
             dz_ref, act_ref, dgu_ref, dz1_ref, dz1b_ref, dm_ref, gg_ref, gb_ref, gg1_ref, gb1_ref):
        @pl.when(pl.program_id(0) == 0)
        def _():
            for ref in (gg_ref, gb_ref, gg1_ref, gb1_ref):
                ref[...] = jnp.zeros_like(ref)

        dy_t = dy_ref[...]
        xhat = xh_ref[...]
        gg_ref[...] += jnp.sum(dy_t * xhat, axis=0, keepdims=True)
        gb_ref[...] += jnp.sum(dy_t, axis=0, keepdims=True)
        dz = _norm_bwd(dy_t * g2_ref[...], xhat, rs_ref[...])
        dzb = dz.astype(bf16)
        dz_ref[...] = dzb
        dact = _nt(dzb, wd_ref[...])
        gate = gu_ref[:, :D_FF]
        up = gu_ref[:, D_FF:]
        sg = jax.nn.sigmoid(gate)
        silu = gate * sg
        act_ref[...] = (silu * up).astype(bf16)
        dgu_ref[:, :D_FF] = (dact * up * (sg * (1.0 + gate * (1.0 - sg)))).astype(bf16)
        dgu_ref[:, D_FF:] = (dact * silu).astype(bf16)
        dh = ALPHA * dz + _nt(dgu_ref[...], wgu_ref[...])
        xhat1 = xh1_ref[...]
        gg1_ref[...] += jnp.sum(dh * xhat1, axis=0, keepdims=True)
        gb1_ref[...] += jnp.sum(dh, axis=0, keepdims=True)
        dz1 = _norm_bwd(dh * g1_ref[...], xhat1, rs1_ref[...])
        dz1_ref[...] = dz1
        dz1b = dz1.astype(bf16)
        dz1b_ref[...] = dz1b
        dm_ref[...] = _nt(dz1b, wo_ref[...])

    vec, vec_out = _resident((1, D_MODEL)), pl.BlockSpec((1, D_MODEL), lambda i: (0, 0))
    tokens_f32, tokens_bf16 = jax.ShapeDtypeStruct((s, D_MODEL), f32), jax.ShapeDtypeStruct((s, D_MODEL), bf16)
    sums = jax.ShapeDtypeStruct((1, D_MODEL), f32)
    return pl.pallas_call(
        body, grid=(s // TOKEN_TILE,),
        in_specs=[_rows(D_MODEL), _rows(D_MODEL), _rows(1), vec, _rows(2 * D_FF),
                  _resident((D_MODEL, 2 * D_FF)), _resident((D_FF, D_MODEL)), _rows(D_MODEL), _rows(1), vec,
                  _resident((D_MODEL, D_MODEL)), pl.BlockSpec(memory_space=pl.ANY)],
        out_specs=[_rows(D_MODEL), _rows(D_FF), _rows(2 * D_FF), _rows(D_MODEL), _rows(D_MODEL), _rows(D_MODEL),
                   vec_out, vec_out, vec_out, vec_out],
        out_shape=[tokens_bf16, jax.ShapeDtypeStruct((s, D_FF), bf16), jax.ShapeDtypeStruct((s, 2 * D_FF), bf16),
                   tokens_f32, tokens_bf16, tokens_f32, sums, sums, sums, sums],
        name="mlp_bwd", compiler_params=_params(("arbitrary",)))(
            dy, xhat2, rstd2, g2, gu, w_gu_b, w_down_b, xhat1, rstd1, g1, w_o_b, after)


def _dx(dz1, dparts, w_in_t, after):
    s = dz1.shape[0]
    n = len(dparts)
    ranges = _row_ranges(dparts)

    def body(*refs):
        d_refs = refs[:n]
        dz_ref, w_ref, _, dx_ref = refs[n:]
        acc = ALPHA * dz_ref[...]
        for d_ref, (lo, hi) in zip(d_refs, ranges):
            acc = acc + _mm(d_ref[...], w_ref[lo:hi, :])
        dx_ref[...] = acc

    return pl.pallas_call(
        body, grid=(s // TOKEN_TILE,),
        in_specs=[_rows(d.shape[1]) for d in dparts] + [_rows(D_MODEL), _resident((IN_W, D_MODEL)),
                                                        pl.BlockSpec(memory_space=pl.ANY)],
        out_specs=_rows(D_MODEL),
        out_shape=jax.ShapeDtypeStruct((s, D_MODEL), f32),
        name="dx", compiler_params=_params(("arbitrary",)))(*dparts, dz1, w_in_t, after)


def _weight_grad_rows(parts, b, bn):
    s, n_cols = b.shape
    n = len(parts)
    ranges = _row_ranges(parts)
    m = ranges[-1][1]

    def body(*refs):
        p_refs = refs[:n]
        b_ref, o_ref = refs[n:]
        for p_ref, (lo, hi) in zip(p_refs, ranges):
            o_ref[lo:hi, :] = _tn(p_ref[...], b_ref[...]).astype(bf16)

    return pl.pallas_call(
        body, grid=(n_cols // bn,),
        in_specs=[_resident(p.shape) for p in parts] + [pl.BlockSpec((s, bn), lambda j: (0, j))],
        out_specs=pl.BlockSpec((m, bn), lambda j: (0, j)),
        out_shape=jax.ShapeDtypeStruct((m, n_cols), bf16),
        name="weight_grad_rows", compiler_params=_params(("arbitrary",)))(*parts, b)


def _weight_grad(a, b, bm, bn, after):
    s, m = a.shape
    n = b.shape[1]

    def body(a_ref, b_ref, after_ref, o_ref):
        o_ref[...] = _tn(a_ref[...], b_ref[...]).astype(bf16)

    return pl.pallas_call(
        body, grid=(m // bm, n // bn),
        in_specs=[pl.BlockSpec((s, bm), lambda i, j: (0, i)), pl.BlockSpec((s, bn), lambda i, j: (0, j)),
                  pl.BlockSpec(memory_space=pl.ANY)],
        out_specs=pl.BlockSpec((bm, bn), lambda i, j: (i, j)),
        out_shape=jax.ShapeDtypeStruct((m, n), bf16),
        name="weight_grad", compiler_params=_params(("arbitrary", "arbitrary")))(a, b, after)


def _shift_down(a, k):
    row = lax.broadcasted_iota(jnp.int32, a.shape, 0)
    return jnp.where(row >= k, pltpu.roll(a, k, 0), 0.0)


def _shift_up(a, k):
    n = a.shape[0]
    row = lax.broadcasted_iota(jnp.int32, a.shape, 0)
    return jnp.where(row < n - k, pltpu.roll(a, n - k, 0), 0.0)


def _slab(s, block):
    return pl.BlockSpec((s, LANES), lambda k: (0, block + k))


def _conv_y(z, w):
    return w[0:1, :] * _shift_down(z, 2) + w[1:2, :] * _shift_down(z, 1) + w[2:3, :] * z


def _conv_fwd(proj, w_conv):
    s = proj.shape[0]

    def body(xa_ref, gb_ref, gc_ref, w_ref, o_ref):
        z = gc_ref[...] * xa_ref[...]
        o_ref[...] = (gb_ref[...] * _conv_y(z, w_ref[...])).astype(bf16)

    return pl.pallas_call(
        body, grid=(CONV_W // LANES,),
        in_specs=[_slab(s, BLK_XA), _slab(s, BLK_GB), _slab(s, BLK_GC), pl.BlockSpec((3, LANES), lambda k: (0, k))],
        out_specs=_slab(s, 0),
        out_shape=jax.ShapeDtypeStruct((s, CONV_W), bf16),
        name="conv_fwd", compiler_params=_params(("arbitrary",)))(proj, proj, proj, w_conv)


def _conv_bwd(proj, dmix, w_conv, after):
    s = proj.shape[0]

    def body(xa_ref, gb_ref, gc_ref, dy_ref, w_ref, after_ref, dxa_ref, dgb_ref, dgc_ref, dw_ref):
        xa = xa_ref[...]
        gc = gc_ref[...]
        w = w_ref[...]
        z = gc * xa
        dya = dy_ref[...]
        dgb_ref[...] = (dya * _conv_y(z, w)).astype(bf16)
        dy = dya * gb_ref[...]
        dz = w[2:3, :] * dy + w[1:2, :] * _shift_up(dy, 1) + w[0:1, :] * _shift_up(dy, 2)
        dxa_ref[...] = (dz * gc).astype(bf16)
        dgc_ref[...] = (dz * xa).astype(bf16)
        dw_ref[0:1, :] = jnp.sum(dy * _shift_down(z, 2), axis=0, keepdims=True)
        dw_ref[1:2, :] = jnp.sum(dy * _shift_down(z, 1), axis=0, keepdims=True)
        dw_ref[2:3, :] = jnp.sum(dy * z, axis=0, keepdims=True)

    out = jax.ShapeDtypeStruct((s, CONV_W), bf16)
    return pl.pallas_call(
        body, grid=(CONV_W // LANES,),
        in_specs=[_slab(s, BLK_XA), _slab(s, BLK_GB), _slab(s, BLK_GC), _slab(s, 0), pl.BlockSpec((3, LANES), lambda k: (0, k)),
                  pl.BlockSpec(memory_space=pl.ANY)],
        out_specs=[_slab(s, 0), _slab(s, 0), _slab(s, 0), pl.BlockSpec((3, LANES), lambda k: (0, k))],
        out_shape=[out, out, out, jax.ShapeDtypeStruct((3, CONV_W), f32)],
        name="conv_bwd", compiler_params=_params(("arbitrary",)))(proj, proj, proj, dmix, w_conv, after)


def _pool_window(k):
    lane = lax.broadcasted_iota(jnp.int32, (1, LANES), 1)
    low = lane < HEAD
    first = k == 0
    wlen = jnp.where(low, jnp.where(first, POOL_WINDOWS[0], POOL_WINDOWS[2]), jnp.where(first, POOL_WINDOWS[1], POOL_WINDOWS[3]))
    return wlen, low, first


def _pool_diff(p, k):
    wlen, low, first = _pool_window(k)
    s2 = p + _shift_down(p, 1)
    s4 = s2 + _shift_down(s2, 2)
    s8 = s4 + _shift_down(s4, 4)
    s16 = s8 + _shift_down(s8, 8)
    win = jnp.where(low, jnp.where(first, s2, s8), jnp.where(first, s4, s16))
    row = lax.broadcasted_iota(jnp.int32, p.shape, 0)
    count = jnp.minimum(row + 1, wlen).astype(f32)
    return win / count - p, count


def _pool_weight(w_ref):
    zero = jnp.zeros((HEAD, HEAD), f32)
    top = jnp.concatenate([w_ref[0], zero], axis=1)
    bottom = jnp.concatenate([zero, w_ref[1]], axis=1)
    return jnp.concatenate([top, bottom], axis=0).astype(bf16)


def _pool_fwd(proj, w_pool, pool_scale):
    s = proj.shape[0]

    def body(p_ref, w_ref, sc_ref, o_ref):
        d, _ = _pool_diff(p_ref[...], pl.program_id(0))
        o_ref[...] = (_mm(d.astype(bf16), _pool_weight(w_ref)) * sc_ref[...]).astype(bf16)

    return pl.pallas_call(
        body, grid=(POOL_W // LANES,),
        in_specs=[_slab(s, BLK_P), pl.BlockSpec((2, HEAD, HEAD), lambda k: (k, 0, 0)), pl.BlockSpec((1, LANES), lambda k: (0, k))],
        out_specs=_slab(s, 0),
        out_shape=jax.ShapeDtypeStruct((s, POOL_W), bf16),
        name="pool_fwd", compiler_params=_params(("arbitrary",)))(proj, w_pool, pool_scale)


def _pool_bwd(proj, dmix, w_pool, pool_scale):
    s = proj.shape[0]

    def body(p_ref, dy_ref, w_ref, sc_ref, dp_ref, dw_ref, dsc_ref):
        k = pl.program_id(0)
        d, count = _pool_diff(p_ref[...], k)
        wbd = _pool_weight(w_ref)
        db = d.astype(bf16)
        dyb = dy_ref[...]
        dsc_ref[...] = jnp.sum(dyb * _mm(db, wbd), axis=0, keepdims=True)
        dpre = (dyb * sc_ref[...]).astype(bf16)
        dwbd = _tn(db, dpre)
        dw_ref[0] = dwbd[:HEAD, :HEAD]
        dw_ref[1] = dwbd[HEAD:, HEAD:]
        dd = _nt(dpre, wbd)
        e = dd / count
        wlen, low, first = _pool_window(k)
        a2 = e + _shift_up(e, 1)
        a4 = a2 + _shift_up(a2, 2)
        a8 = a4 + _shift_up(a4, 4)
        a16 = a8 + _shift_up(a8, 8)
        back = jnp.where(low, jnp.where(first, a2, a8), jnp.where(first, a4, a16))
        dp_ref[...] = (back - dd).astype(bf16)

    return pl.pallas_call(
        body, grid=(POOL_W // LANES,),
        in_specs=[_slab(s, BLK_P), _slab(s, CONV_W // LANES), pl.BlockSpec((2, HEAD, HEAD), lambda k: (k, 0, 0)),
                  pl.BlockSpec((1, LANES), lambda k: (0, k))],
        out_specs=[_slab(s, 0), pl.BlockSpec((2, HEAD, HEAD), lambda k: (k, 0, 0)), pl.BlockSpec((1, LANES), lambda k: (0, k))],
        out_shape=[jax.ShapeDtypeStruct((s, POOL_W), bf16), jax.ShapeDtypeStruct((4, HEAD, HEAD), f32),
                   jax.ShapeDtypeStruct((1, POOL_W), f32)],
        name="pool_bwd", compiler_params=_params(("arbitrary",)))(proj, dmix, w_pool, pool_scale)


SGU_UNROLL = 4
INV_SQRT2 = 0.7071067811865476
INV_SQRT_2PI = 0.3989422804014327


def _gelu(x):
    return 0.5 * x * (1.0 + lax.erf(x * INV_SQRT2))


def _gelu_grad(x):
    return 0.5 * (1.0 + lax.erf(x * INV_SQRT2)) + x * (INV_SQRT_2PI * jnp.exp(-0.5 * x * x))


def _head_mean(a, low):
    s_low = jnp.sum(jnp.where(low, a, 0.0), axis=-1, keepdims=True)
    s_high = jnp.sum(jnp.where(low, 0.0, a), axis=-1, keepdims=True)
    return jnp.where(low, s_low, s_high) * (1.0 / HEAD)


def _tril():
    r = lax.broadcasted_iota(jnp.int32, (CHUNK, CHUNK), 0)
    c = lax.broadcasted_iota(jnp.int32, (CHUNK, CHUNK), 1)
    return r >= c


def _sgu_chunk(up, vp, g, wm0, wm1, b0, b1, low):
    ug = _gelu(up)
    vg = _gelu(vp)
    vc = vg - _head_mean(vg, low)
    rstd = lax.rsqrt(_head_mean(vc * vc, low) + LN_EPS)
    vn = vc * rstd
    vb = (vn * g).astype(bf16)
    mixed = jnp.where(low, _mm(wm0, vb) + b0, _mm(wm1, vb) + b1)
    return ug, vn, rstd, vb, mixed


def _sgu_specs(s):
    return [_slab(s, BLK_U), _slab(s, BLK_V), pl.BlockSpec((1, LANES), lambda k: (0, k)),
            pl.BlockSpec((2, CHUNK, CHUNK), lambda k: (k, 0, 0)), pl.BlockSpec((2, CHUNK, 1), lambda k: (k, 0, 0))]


def _sgu_fwd(proj, sgu_g, w_spatial, b_spatial3):
    s = proj.shape[0]

    def body(u_ref, v_ref, g_ref, w_ref, b_ref, o_ref):
        low = lax.broadcasted_iota(jnp.int32, (1, LANES), 1) < HEAD
        mask = _tril()
        wm0 = jnp.where(mask, w_ref[0], 0.0).astype(bf16)
        wm1 = jnp.where(mask, w_ref[1], 0.0).astype(bf16)
        g = g_ref[...]
        b0 = b_ref[0]
        b1 = b_ref[1]

        def chunk(n, carry):
            rows = pl.ds(pl.multiple_of(n * CHUNK, CHUNK), CHUNK)
            ug, _, _, _, mixed = _sgu_chunk(u_ref[rows, :], v_ref[rows, :], g, wm0, wm1, b0, b1, low)
            o_ref[rows, :] = (ug * mixed).astype(bf16)
            return carry

        lax.fori_loop(0, s // CHUNK, chunk, 0, unroll=SGU_UNROLL)

    return pl.pallas_call(
        body, grid=(SGU_W // LANES,),
        in_specs=_sgu_specs(s),
        out_specs=_slab(s, 0),
        out_shape=jax.ShapeDtypeStruct((s, SGU_W), bf16),
        name="sgu_fwd", compiler_params=_params(("arbitrary",)))(proj, proj, sgu_g, w_spatial, b_spatial3)


def _sgu_bwd(proj, dmix, sgu_g, w_spatial, b_spatial3):
    s = proj.shape[0]

    def body(u_ref, v_ref, g_ref, w_ref, b_ref, dy_ref, du_ref, dv_ref, dg_ref, dw_ref, db_ref):
        low = lax.broadcasted_iota(jnp.int32, (1, LANES), 1) < HEAD
        mask = _tril()
        w0 = jnp.where(mask, w_ref[0], 0.0)
        w1 = jnp.where(mask, w_ref[1], 0.0)
        wm0 = w0.astype(bf16)
        wm1 = w1.astype(bf16)
        wt0 = w0.T.astype(bf16)
        wt1 = w1.T.astype(bf16)
        g = g_ref[...]
        b0 = b_ref[0]
        b1 = b_ref[1]
        dg_ref[...] = jnp.zeros_like(dg_ref)
        dw_ref[...] = jnp.zeros_like(dw_ref)
        db_ref[...] = jnp.zeros_like(db_ref)

        def chunk(n, carry):
            rows = pl.ds(pl.multiple_of(n * CHUNK, CHUNK), CHUNK)
            up = u_ref[rows, :]
            vp = v_ref[rows, :]
            ug, vn, rstd, vb, mixed = _sgu_chunk(up, vp, g, wm0, wm1, b0, b1, low)
            dy = dy_ref[rows, :]
            du_ref[rows, :] = (dy * mixed * _gelu_grad(up)).astype(bf16)
            dmix_c = dy * ug
            db_ref[0] += jnp.sum(jnp.where(low, dmix_c, 0.0), axis=-1, keepdims=True)
            db_ref[1] += jnp.sum(jnp.where(low, 0.0, dmix_c), axis=-1, keepdims=True)
            dmb = dmix_c.astype(bf16)
            zero = jnp.zeros_like(dmb)
            dw_ref[0] += _nt(jnp.where(low, dmb, zero), vb)
            dw_ref[1] += _nt(jnp.where(low, zero, dmb), vb)
            dvnorm = jnp.where(low, _mm(wt0, dmb), _mm(wt1, dmb))
            dg_ref[...] += jnp.sum(dvnorm * vn, axis=0, keepdims=True)
            dvn = dvnorm * g
            dvg = rstd * (dvn - _head_mean(dvn, low) - vn * _head_mean(dvn * vn, low))
            dv_ref[rows, :] = (dvg * _gelu_grad(vp)).astype(bf16)
            return carry

        lax.fori_loop(0, s // CHUNK, chunk, 0, unroll=SGU_UNROLL)
        dw_ref[0] = jnp.where(mask, dw_ref[0], 0.0)
        dw_ref[1] = jnp.where(mask, dw_ref[1], 0.0)

    out = jax.ShapeDtypeStruct((s, SGU_W), bf16)
    return pl.pallas_call(
        body, grid=(SGU_W // LANES,),
        in_specs=_sgu_specs(s) + [_slab(s, (CONV_W + POOL_W) // LANES)],
        out_specs=[_slab(s, 0), _slab(s, 0), pl.BlockSpec((1, LANES), lambda k: (0, k)),
                   pl.BlockSpec((2, CHUNK, CHUNK), lambda k: (k, 0, 0)), pl.BlockSpec((2, CHUNK, 1), lambda k: (k, 0, 0))],
        out_shape=[out, out, jax.ShapeDtypeStruct((1, SGU_W), f32), jax.ShapeDtypeStruct((6, CHUNK, CHUNK), f32),
                   jax.ShapeDtypeStruct((6, CHUNK, 1), f32)],
        name="sgu_bwd", compiler_params=_params(("arbitrary",)))(proj, proj, sgu_g, w_spatial, b_spatial3, dmix)


def _fwd_mix(x, w, after):
    proj, xb = _proj(x, w["w_in"], after)
    mix = [_conv_fwd(proj, w["w_conv"]), _pool_fwd(proj, w["w_pool"], w["pool_scale"]),
           _sgu_fwd(proj, w["sgu_ln_g"], w["w_spatial"], w["b_spatial"])]
    xhat1, rstd1, hb = _wo_ln1(mix, x, w["w_o"], w["ln1_g"], w["ln1_b"])
    return dict(proj=proj, xb=xb, mix=mix, xhat1=xhat1, rstd1=rstd1, hb=hb)


def _fwd_mlp(sv, w, after):
    gu, xhat2, rstd2, y = _mlp_fwd(sv["xhat1"], w["ln1_g"], w["ln1_b"], w["w_gate_up"], w["w_down"], w["ln2_g"], w["ln2_b"], after)
    sv.update(gu=gu, xhat2=xhat2, rstd2=rstd2)
    return y


def _bwd_mlp(dy, w, sv, after, hook):
    dz2b, actb, dgub, dz1, dz1b, dmix, g_ln2_g, g_ln2_b, g_ln1_g, g_ln1_b = _mlp_bwd(
        dy, sv["xhat2"], sv["rstd2"], w["ln2_g"], sv["gu"], w["w_gate_up"], w["w_down"], sv["xhat1"], sv["rstd1"], w["ln1_g"],
        w["w_o"], after)
    after = hook(dz1)
    grads = dict(w_gate_up=_weight_grad(sv["hb"], dgub, 512, D_FF // 2, after),
                 w_down=_weight_grad(actb, dz2b, D_FF // 2, D_MODEL, after),
                 ln2_g=g_ln2_g, ln2_b=g_ln2_b, ln1_g=g_ln1_g, ln1_b=g_ln1_b)
    return (dz1, dz1b, dmix), grads


def _bwd_mix(dz, w, sv, after, hook):
    dz1, dz1b, dmix = dz
    dxa, dgb, dgc, g_conv = _conv_bwd(sv["proj"], dmix, w["w_conv"], after)
    dp, g_pool, g_pscale = _pool_bwd(sv["proj"], dmix, w["w_pool"], w["pool_scale"])
    du, dv, g_sgu_g, g_spatial, g_bsp = _sgu_bwd(sv["proj"], dmix, w["sgu_ln_g"], w["w_spatial"], w["b_spatial"])
    dparts = [dxa, dgb, dgc, dp, du, dv]
    dx = _dx(dz1, dparts, w["w_in"], hook(du))
    grads = dict(
        w_in=_weight_grad_rows(dparts, sv["xb"], 512), w_o=_weight_grad_rows(sv["mix"], dz1b, D_MODEL),
        w_conv=g_conv, w_pool=g_pool, pool_scale=g_pscale, sgu_ln_g=g_sgu_g, w_spatial=g_spatial,
        b_spatial=g_bsp.reshape(6, CHUNK))
    return dx, grads


def _local_step(x, target, layers):
    saved = []
    for w in layers:
        sv = _fwd_mix(x, w, x)
        x = _fwd_mlp(sv, w, x)
        saved.append(sv)
    dy, sq = _loss_head(x, target)
    grads = [None] * len(layers)
    for l in reversed(range(len(layers))):
        dz, g_mlp = _bwd_mlp(dy, layers[l], saved[l], sq, lambda a: a)
        dy, g_mix = _bwd_mix(dz, layers[l], saved[l], dz[0], lambda a: a)
        grads[l] = dict(g_mlp, **g_mix)
    return sq, dy, grads


ANY = pl.BlockSpec(memory_space=pl.ANY)


def _place():
    x, y, c = lax.axis_index("x"), lax.axis_index("y"), lax.axis_index("c")
    others = [(1 - x, y), (x, 1 - y), (1 - x, 1 - y)]
    return x, y, c, others


def _chip_index(cx, cy):
    return 2 * cx + cy


def _half(ref_rows, c):
    half = ref_rows // 2
    return pl.ds(pl.multiple_of(c * half, 8), half)


def _remote(src, dst, send_sem, recv_sem, device):
    return pltpu.make_async_remote_copy(src_ref=src, dst_ref=dst, send_sem=send_sem, recv_sem=recv_sem,
                                        device_id=device, device_id_type=MESH)


def _gather_shards(shards):
    n = len(shards)
    base, total = [], 0
    for s in shards:
        base.append(total)
        total += 6 * s.shape[0]

    def body(*refs):
        ins, outs = refs[:n], refs[n:2 * n]
        send, recv = refs[2 * n:]
        x, y, c, others = _place()
        me = _chip_index(x, y)
        sib = (x, y, 1 - c)
        sends = []
        for f in range(n):
            depth, rows = ins[f].shape[0], ins[f].shape[1]
            for l in range(depth):
                for k, (cx, cy) in enumerate(others):
                    sem = base[f] + 6 * l + k
                    cp = _remote(ins[f].at[l, _half(rows, c)], outs[f].at[l, me, _half(rows, c)],
                                 send.at[sem], recv.at[sem], (cx, cy, c))
                    cp.start()
                    sends.append(cp)
        for f in range(n):
            depth, rows = ins[f].shape[0], ins[f].shape[1]
            for l in range(depth):
                for k, (cx, cy) in enumerate(others):
                    sem = base[f] + 6 * l + k
                    landed = outs[f].at[l, _chip_index(cx, cy), _half(rows, c)]
                    _remote(landed, landed, send.at[sem], recv.at[sem], (cx, cy, c)).wait_recv()
                    cp = _remote(landed, landed, send.at[sem + 3], recv.at[sem + 3], sib)
                    cp.start()
                    sends.append(cp)
        for f in range(n):
            depth, rows = ins[f].shape[0], ins[f].shape[1]
            for l in range(depth):
                for k, (cx, cy) in enumerate(others):
                    sem = base[f] + 6 * l + k + 3
                    passed = outs[f].at[l, _chip_index(cx, cy), _half(rows, 1 - c)]
                    _remote(passed, passed, send.at[sem], recv.at[sem], sib).wait_recv()
        for cp in sends:
            cp.wait_send()

    gathered = pl.pallas_call(
        body, in_specs=[ANY] * n, out_specs=[ANY] * n,
        out_shape=[jax.ShapeDtypeStruct((s.shape[0], N_CHIPS) + s.shape[1:], s.dtype) for s in shards],
        scratch_shapes=[pltpu.SemaphoreType.DMA((total,)), pltpu.SemaphoreType.DMA((total,))],
        name="gather_shards")(*shards)
    return [_place_own(g, s) for g, s in zip(gathered, shards)]


def _scalar(value):
    return jnp.reshape(value, (1,)).astype(jnp.int32)


def _place_own(blocks, shard):
    depth, rows, cols = shard.shape

    def body(me_ref, b_ref, s_ref, o_ref):
        o_ref[...] = s_ref[...]

    return pl.pallas_call(
        body,
        grid_spec=pltpu.PrefetchScalarGridSpec(
            num_scalar_prefetch=1, grid=(depth,),
            in_specs=[ANY, pl.BlockSpec((None, rows, cols), lambda l, me: (l, 0, 0))],
            out_specs=pl.BlockSpec((None, None, rows, cols), lambda l, me: (l, me[0], 0, 0))),
        out_shape=jax.ShapeDtypeStruct(blocks.shape, blocks.dtype),
        input_output_aliases={1: 0},
        name="place_own", compiler_params=_params(("arbitrary",)))(
            _scalar(_chip_index(lax.axis_index("x"), lax.axis_index("y"))), blocks, shard)


HBM = pl.BlockSpec(memory_space=pltpu.HBM)
SEM = pl.BlockSpec(memory_space=pltpu.SEMAPHORE)
TOKEN = jax.ShapeDtypeStruct((8, LANES), f32)
SPLIT_COPY = pltpu.CompilerParams(has_side_effects=pltpu.SideEffectType.DATAFLOW_SIDE_EFFECTING)


def _in_hbm(a):
    return pltpu.with_memory_space_constraint(a, pltpu.HBM)


def _full_shape(shard, axis):
    rows, cols = shard.shape
    return (N_CHIPS * rows, cols) if axis == 0 else (rows, N_CHIPS * cols)


def _block_half(ref, axis, j, h):
    if axis == 0:
        rows = ref.shape[0] // N_CHIPS
        return ref.at[pl.ds(pl.multiple_of(j * rows + h * (rows // 2), 16), rows // 2), :]
    half, cols = ref.shape[0] // 2, ref.shape[1] // N_CHIPS
    return ref.at[pl.ds(pl.multiple_of(h * half, 16), half), pl.ds(pl.multiple_of(j * cols, LANES), cols)]


def _place_layer(shards, axes, after):
    n = len(shards)

    def body(me_ref, *refs):
        ins, outs = refs[n:2 * n], refs[2 * n + 1:]
        for f in range(n):
            block = ins[f][...].astype(bf16)
            outs[f][...] = block
            outs[n + f][...] = block

    lands = [lax.empty(_full_shape(s, ax), bf16) for s, ax in zip(shards, axes)]
    own = [pl.BlockSpec(s.shape, lambda i, me: (0, 0)) for s in shards]
    outs = pl.pallas_call(
        body,
        grid_spec=pltpu.PrefetchScalarGridSpec(
            num_scalar_prefetch=1, grid=(1,),
            in_specs=[ANY] * n + own + [ANY],
            out_specs=own + [pl.BlockSpec(s.shape, (lambda i, me: (me[0], 0)) if ax == 0 else (lambda i, me: (0, me[0])))
                             for s, ax in zip(shards, axes)]),
        out_shape=[jax.ShapeDtypeStruct(s.shape, bf16) for s in shards] + [jax.ShapeDtypeStruct(a.shape, bf16) for a in lands],
        input_output_aliases={1 + f: n + f for f in range(n)},
        name="place_layer", compiler_params=_params(("arbitrary",)))(
            _scalar(_chip_index(lax.axis_index("x"), lax.axis_index("y"))), *lands, *shards, after)
    return outs[:n], outs[n:]


def _gather_start(shards, lands, axes, after):
    return _split_copy_start("gather", _gather_plan(axes), 3 * len(shards), shards, lands, after)


def _gather_wait(state, axes, after):
    return _split_copy_wait("gather", _gather_plan(axes), state, after)


SIBLING_PAIR_ID = 0


def _split_copy_start(name, plan, count, ins, lands, after, sibling_only=False):
    arrays = list(ins) + list(lands)
    n_in, n = len(ins), len(arrays)

    def body(*refs):
        send, recv, token = refs[n + 1], refs[n + 2], refs[-1]
        if sibling_only:
            x, y, c, _ = _place()
            barrier = pltpu.get_barrier_semaphore()
            pl.semaphore_signal(barrier, inc=1, device_id=(x, y, 1 - c), device_id_type=MESH)
            pl.semaphore_wait(barrier, 1)
        for i, (src, dst, _, peer) in enumerate(plan(refs[:n_in], refs[n_in:n])):
            _remote(src, dst, send.at[i], recv.at[i], peer).start()
        token[...] = jnp.zeros_like(token)

    effect = pltpu.SideEffectType.DATAFLOW_SIDE_EFFECTING
    outs = pl.pallas_call(
        body, name=name + "_start",
        in_specs=[HBM] * n + [ANY],
        out_specs=(SEM, SEM, *[HBM] * n, pl.BlockSpec(memory_space=pltpu.VMEM)),
        out_shape=(pltpu.SemaphoreType.DMA((count,)), pltpu.SemaphoreType.DMA((count,)),
                   *[pltpu.HBM(a.shape, a.dtype) for a in arrays], TOKEN),
        input_output_aliases={i: 2 + i for i in range(n)},
        compiler_params=pltpu.CompilerParams(has_side_effects=effect, collective_id=SIBLING_PAIR_ID) if sibling_only
        else SPLIT_COPY)(*[_in_hbm(a) for a in arrays], after)
    return (outs[0], outs[1], outs[2:2 + n_in], outs[2 + n_in:2 + n]), outs[-1]


def _split_copy_wait(name, plan, state, after):
    send_sems, recv_sems, ins, lands = state
    arrays = list(ins) + list(lands)
    n_in, n = len(ins), len(arrays)

    def body(*refs):
        send, recv, token = refs[n], refs[n + 1], refs[-1]
        for i, (src, _, landing, peer) in enumerate(plan(refs[:n_in], refs[n_in:n])):
            cp = _remote(src, landing, send.at[i], recv.at[i], peer)
            cp.wait_send()
            cp.wait_recv()
        token[...] = jnp.zeros_like(token)

    outs = pl.pallas_call(
        body, name=name + "_wait",
        in_specs=[HBM] * n + [SEM, SEM, ANY],
        out_specs=(*[HBM] * n, pl.BlockSpec(memory_space=pltpu.VMEM)),
        out_shape=(*[pltpu.HBM(a.shape, a.dtype) for a in arrays], TOKEN),
        input_output_aliases={i: i for i in range(n)},
        compiler_params=SPLIT_COPY)(*arrays, send_sems, recv_sems, after)
    return outs[:n_in], outs[n_in:n], outs[-1]


def _gather_plan(axes):
    def plan(ins, lnd):
        x, y, c, others = _place()
        me = _chip_index(x, y)
        return [(ins[f].at[_half(ins[f].shape[0], c)], _block_half(lnd[f], ax, me, c),
                 _block_half(lnd[f], ax, _chip_index(cx, cy), c), (cx, cy, c))
                for f, ax in enumerate(axes) for cx, cy in others]
    return plan


def _pair_plan(axes):
    def plan(ins, lnd):
        x, y, c, _ = _place()
        return [(_block_half(ins[f], ax, j, 1 - c), lnd[f].at[j], lnd[f].at[j], (x, y, 1 - c))
                for f, ax in enumerate(axes) for j in range(N_CHIPS)]
    return plan


def _scatter_plan(ins, lnd):
    x, y, c, others = _place()
    return [(ins[f].at[_chip_index(cx, cy)], lnd[f].at[k], lnd[f].at[k], (cx, cy, c))
            for f in range(len(ins)) for k, (cx, cy) in enumerate(others)]


def _join_plan(ins, lnd):
    x, y, c, _ = _place()
    return [(lnd[f].at[_half(lnd[f].shape[0], c)], lnd[f].at[_half(lnd[f].shape[0], c)],
             lnd[f].at[_half(lnd[f].shape[0], 1 - c)], (x, y, 1 - c)) for f in range(len(lnd))]


def _gather_finish(lands, axes, after):
    n = len(lands)

    def body(*refs):
        outs = refs[n + 1:2 * n + 1]
        send, recv = refs[2 * n + 1:]
        x, y, c, others = _place()
        sib = (x, y, 1 - c)
        barrier = pltpu.get_barrier_semaphore()
        pl.semaphore_signal(barrier, inc=1, device_id=sib, device_id_type=MESH)
        pl.semaphore_wait(barrier, 1)
        sends = []
        for f in range(n):
            for k, (cx, cy) in enumerate(others):
                landed = _block_half(outs[f], axes[f], _chip_index(cx, cy), c)
                cp = _remote(landed, landed, send.at[3 * f + k], recv.at[3 * f + k], sib)
                cp.start()
                sends.append(cp)
        for f in range(n):
            for k, (cx, cy) in enumerate(others):
                passed = _block_half(outs[f], axes[f], _chip_index(cx, cy), 1 - c)
                _remote(passed, passed, send.at[3 * f + k], recv.at[3 * f + k], sib).wait_recv()
        for cp in sends:
            cp.wait_send()

    return pl.pallas_call(
        body, in_specs=[ANY] * (n + 1), out_specs=[ANY] * n,
        out_shape=[jax.ShapeDtypeStruct(a.shape, a.dtype) for a in lands],
        input_output_aliases={f: f for f in range(n)},
        scratch_shapes=[pltpu.SemaphoreType.DMA((3 * n,)), pltpu.SemaphoreType.DMA((3 * n,))],
        compiler_params=pltpu.CompilerParams(collective_id=SIBLING_PAIR_ID),
        name="gather_finish")(*lands, after)


def _half_blocks(part, axis):
    rows, cols = (part.shape[0] // N_CHIPS, part.shape[1]) if axis == 0 else (part.shape[0], part.shape[1] // N_CHIPS)
    return lax.empty((N_CHIPS, rows // 2, cols), part.dtype)


def _add_pair_layer(parts, gots, axes):
    k = len(parts)

    def body(c_ref, *refs):
        for f in range(k):
            a_ref, b_ref, o_ref = refs[2 * f], refs[2 * f + 1], refs[2 * k + f]
            o_ref[...] = (a_ref[...].astype(f32) + b_ref[...].astype(f32)).astype(o_ref.dtype)

    in_specs, out_specs, operands = [], [], []
    for part, got, axis in zip(parts, gots, axes):
        _, half, cols = got.shape
        if axis == 0:
            part = part.reshape(N_CHIPS, 2, half, cols)
            mine = pl.BlockSpec((None, None, half, cols), lambda j, c: (j, c[0], 0, 0))
        else:
            mine = pl.BlockSpec((half, cols), lambda j, c: (c[0], j))
        block = pl.BlockSpec((None, half, cols), lambda j, c: (j, 0, 0))
        in_specs += [mine, block]
        out_specs.append(block)
        operands += [part, got]
    return pl.pallas_call(
        body,
        grid_spec=pltpu.PrefetchScalarGridSpec(num_scalar_prefetch=1, grid=(N_CHIPS,), in_specs=in_specs, out_specs=out_specs),
        out_shape=[jax.ShapeDtypeStruct(g.shape, p.dtype) for p, g in zip(parts, gots)],
        name="add_pair_layer", compiler_params=_params(("arbitrary",)))(_scalar(lax.axis_index("c")), *operands)


def _scatter_start(sums, after):
    lands = [lax.empty((3,) + s.shape[1:], s.dtype) for s in sums]
    return _split_copy_start("scatter", _scatter_plan, 3 * len(sums), sums, lands, after)


def _scatter_wait(state, after):
    return _split_copy_wait("scatter", _scatter_plan, state, after)


def _add_slots(chip_sums, slots):
    k = len(chip_sums)

    def body(at_ref, *refs):
        for f in range(k):
            own_ref, s_ref, o_ref = refs[2 * f], refs[2 * f + 1], refs[2 * k + f]
            acc = own_ref[...].astype(f32)
            for j in range(3):
                acc = acc + s_ref[j].astype(f32)
            o_ref[...] = acc

    in_specs, out_specs, operands = [], [], []
    for cs, s in zip(chip_sums, slots):
        _, half, cols = cs.shape
        in_specs += [pl.BlockSpec((None, half, cols), lambda i, at: (at[0], 0, 0)), pl.BlockSpec((3, half, cols), lambda i, at: (0, 0, 0))]
        out_specs.append(pl.BlockSpec((None, half, cols), lambda i, at: (at[1], 0, 0)))
        operands += [cs, s]
    at = jnp.concatenate([_scalar(_chip_index(lax.axis_index("x"), lax.axis_index("y"))), _scalar(lax.axis_index("c"))])
    outs = pl.pallas_call(
        body,
        grid_spec=pltpu.PrefetchScalarGridSpec(num_scalar_prefetch=1, grid=(1,), in_specs=in_specs, out_specs=out_specs),
        out_shape=[jax.ShapeDtypeStruct((2,) + cs.shape[1:], f32) for cs in chip_sums],
        name="add_slots", compiler_params=_params(("arbitrary",)))(at, *operands)
    return [o.reshape(2 * o.shape[1], o.shape[2]) for o in outs]


def _adamw_math(w, grad, m, v):
    nm = ADAM_B1 * m + (1.0 - ADAM_B1) * grad
    nv = ADAM_B2 * v + (1.0 - ADAM_B2) * (grad * grad)
    m_hat = nm / (1.0 - ADAM_B1 ** ADAM_STEP)
    v_hat = nv / (1.0 - ADAM_B2 ** ADAM_STEP)
    return nm, nv, -ADAM_LR * (m_hat / (jnp.sqrt(v_hat) + ADAM_EPS) + ADAM_WD * w)


def _adamw_small(ws, gs, ms, vs):
    k = len(ws)

    def body(*refs):
        for f in range(k):
            w_ref, g_ref, m_ref, v_ref = refs[4 * f:4 * f + 4]
            d_ref, nm_ref, nv_ref = refs[4 * k + 3 * f:4 * k + 3 * f + 3]
            nm, nv, step = _adamw_math(w_ref[...], g_ref[...], m_ref[...], v_ref[...])
            d_ref[...] = step
            nm_ref[...] = nm
            nv_ref[...] = nv

    whole = pl.BlockSpec(memory_space=pltpu.VMEM)
    res = pl.pallas_call(
        body, in_specs=[whole] * (4 * k), out_specs=[whole] * (3 * k),
        out_shape=[jax.ShapeDtypeStruct(w.shape, f32) for w in ws for _ in range(3)],
        name="adamw_small", compiler_params=_params())(*[a for four in zip(ws, gs, ms, vs) for a in four])
    return [res[3 * f:3 * f + 3] for f in range(k)]


def _adamw_layer(l, ws, ms, vs, gs, outs, steps, after):
    k = len(ws)

    def body(*refs):
        ins, new = refs[:4 * k], refs[8 * k + 1:]
        for f in range(k):
            w_ref, m_ref, v_ref, g_ref = ins[4 * f:4 * f + 4]
            go_ref, d_ref, nm_ref, nv_ref = new[4 * f:4 * f + 4]
            grad = g_ref[...]
            nm, nv, step = _adamw_math(w_ref[...], grad, m_ref[...], v_ref[...])
            go_ref[...] = grad
            d_ref[...] = step
            nm_ref[...] = nm
            nv_ref[...] = nv

    in_specs, out_specs, operands = [], [], []
    for w, m, v, g in zip(ws, ms, vs, gs):
        _, rows, cols = w.shape
        tile = rows // steps
        layer = pl.BlockSpec((None, tile, cols), lambda i: (l, i, 0))
        in_specs += [layer] * 3 + [_rows(cols, tile)]
        out_specs += [layer] * 4
        operands += [w, m, v, g]
    flat_outs = [o for four in outs for o in four]
    res = pl.pallas_call(
        body, grid=(steps,),
        in_specs=in_specs + [ANY] * (4 * k + 1), out_specs=out_specs,
        out_shape=[jax.ShapeDtypeStruct(o.shape, f32) for o in flat_outs],
        input_output_aliases={4 * k + j: j for j in range(4 * k)},
        name="adamw_layer", compiler_params=_params(("arbitrary",)))(*operands, *flat_outs, after)
    return [res[4 * f:4 * f + 4] for f in range(k)]


SMALL = ("w_conv", "w_pool", "pool_scale", "sgu_ln_g", "w_spatial", "b_spatial", "ln1_g", "ln1_b", "ln2_g", "ln2_b")
WEIGHTS = ("w_in", "w_conv", "w_pool", "pool_scale", "sgu_ln_g", "w_spatial", "b_spatial", "w_o", "ln1_g", "ln1_b",
           "w_gate_up", "w_down", "ln2_g", "ln2_b")
BIG = ("w_in", "w_o", "w_gate_up", "w_down")
GROUPS = (("w_in", "w_o"), ("w_gate_up", "w_down"))
GROUP_AXES = ((0, 0), (1, 0))
SCATTER_HOOKS = 2
ADAMW_STEPS = (2, 4)
SMALL_LAYER_ROWS = 1024


def _pack_layer(arrays):
    flat = jnp.concatenate([a.reshape(-1) for a in arrays])
    return jnp.pad(flat, (0, SMALL_LAYER_ROWS * LANES - flat.shape[0])).reshape(SMALL_LAYER_ROWS, LANES)


def _unpack_layers(flat, shapes):
    out, at = {}, 0
    for name, shape in shapes.items():
        size = 1
        for d in shape:
            size *= d
        out[name] = flat[:, at:at + size].reshape((flat.shape[0],) + tuple(shape))
        at += size
    return out


def kernel(x, w_in, w_conv, w_pool, pool_scale, sgu_ln_g, w_spatial, b_spatial, w_o, ln1_g, ln1_b, w_gate_up, w_down, ln2_g, ln2_b, loss_target, m_w_in, m_w_conv, m_w_pool, m_pool_scale, m_sgu_ln_g, m_w_spatial, m_b_spatial, m_w_o, m_ln1_g, m_ln1_b, m_w_gate_up, m_w_down, m_ln2_g, m_ln2_b, v_w_in, v_w_conv, v_w_pool, v_pool_scale, v_sgu_ln_g, v_w_spatial, v_b_spatial, v_w_o, v_ln1_g, v_ln1_b, v_w_gate_up, v_w_down, v_ln2_g, v_ln2_b):
    weights = dict(w_in=w_in, w_conv=w_conv, w_pool=w_pool, pool_scale=pool_scale, sgu_ln_g=sgu_ln_g, w_spatial=w_spatial,
                   b_spatial=b_spatial, w_o=w_o, ln1_g=ln1_g, ln1_b=ln1_b, w_gate_up=w_gate_up, w_down=w_down, ln2_g=ln2_g, ln2_b=ln2_b)
    m_in = dict(w_in=m_w_in, w_conv=m_w_conv, w_pool=m_w_pool, pool_scale=m_pool_scale, sgu_ln_g=m_sgu_ln_g, w_spatial=m_w_spatial,
                b_spatial=m_b_spatial, w_o=m_w_o, ln1_g=m_ln1_g, ln1_b=m_ln1_b, w_gate_up=m_w_gate_up, w_down=m_w_down,
                ln2_g=m_ln2_g, ln2_b=m_ln2_b)
    v_in = dict(w_in=v_w_in, w_conv=v_w_conv, w_pool=v_w_pool, pool_scale=v_pool_scale, sgu_ln_g=v_sgu_ln_g, w_spatial=v_w_spatial,
                b_spatial=v_b_spatial, w_o=v_w_o, ln1_g=v_ln1_g, ln1_b=v_ln1_b, w_gate_up=v_w_gate_up, w_down=v_w_down,
                ln2_g=v_ln2_g, ln2_b=v_ln2_b)
    depth = w_in.shape[0]
    conv_cols = w_conv.shape[2]
    chip = _chip_index(lax.axis_index("x"), lax.axis_index("y"))

    conv_flat = jnp.pad(w_conv.reshape(-1), (0, 16 * LANES - w_conv.size)).reshape(1, 16, LANES)
    conv_full = _gather_shards([conv_flat])[0].reshape(N_CHIPS, 16 * LANES)[:, :w_conv.size].reshape(N_CHIPS, depth, 3, conv_cols)
    conv_full = conv_full.transpose(1, 2, 0, 3).reshape(depth, 3, N_CHIPS * conv_cols)

    big_w = dict(w_in=jnp.swapaxes(w_in, 1, 2), w_o=w_o, w_gate_up=w_gate_up, w_down=w_down)
    big_m = dict(w_in=jnp.swapaxes(m_w_in, 1, 2), w_o=m_w_o, w_gate_up=m_w_gate_up, w_down=m_w_down)
    big_v = dict(w_in=jnp.swapaxes(v_w_in, 1, 2), w_o=v_w_o, w_gate_up=v_w_gate_up, w_down=v_w_down)

    def place(l, g, after):
        return _place_layer([big_w[n][l] for n in GROUPS[g]], GROUP_AXES[g], after)

    def send(l, g, after):
        return _gather_start(*placed[l, g], GROUP_AXES[g], after)

    stages = [(l, g) for l in range(depth) for g in (0, 1)]
    placed, flights = {}, {}
    token = conv_full
    for st in stages[:2]:
        placed[st] = place(*st, token)
        flights[st], token = send(*st, placed[st][0][0])
    recent = token
    for st in stages[2:]:
        placed[st] = place(*st, token)
        recent = placed[st][0][0]
    act = x[0]
    layers, saved = [], []
    for i, (l, g) in enumerate(stages):
        if g == 0:
            w = dict(w_conv=conv_full[l], w_pool=w_pool[l], pool_scale=pool_scale[l][None], sgu_ln_g=sgu_ln_g[l][None],
                     w_spatial=w_spatial[l], b_spatial=b_spatial[l][:, :, None], ln1_g=ln1_g[l][None], ln1_b=ln1_b[l][None],
                     ln2_g=ln2_g[l][None], ln2_b=ln2_b[l][None])
        _, lands, token = _gather_wait(flights[l, g], GROUP_AXES[g], recent)
        if i + 2 < len(stages):
            flights[stages[i + 2]], token = send(*stages[i + 2], token)
        w.update(zip(GROUPS[g], _gather_finish(lands, GROUP_AXES[g], token)))
        if g == 0:
            sv = _fwd_mix(act, w, token)
            recent = sv["xhat1"]
        else:
            act = recent = _fwd_mlp(sv, w, token)
            layers.append(w)
            saved.append(sv)

    big_outs = {n: [lax.empty(big_w[n].shape, f32) for _ in range(4)] for n in BIG}
    small_sums = [None] * depth
    pending, updates = [], []
    latest = dict(token=None)

    def begin(l, g, parts):
        axes = GROUP_AXES[g] + (0,) * (len(parts) - len(GROUPS[g]))
        lands = [_half_blocks(p, ax) for p, ax in zip(parts, axes)]
        flight, latest["token"] = _split_copy_start("pair", _pair_plan(axes), N_CHIPS * len(parts), parts, lands, latest["token"],
                                                    sibling_only=True)
        pending.append(dict(l=l, g=g, axes=axes, step="pair", age=0, flight=flight))

    def advance(st, recent):
        if st["step"] == "pair":
            parts, got, _ = _split_copy_wait("pair", _pair_plan(st["axes"]), st["flight"], recent)
            sums = _add_pair_layer(parts, got, st["axes"])
            st["flight"], latest["token"] = _scatter_start(sums, latest["token"])
            st["step"] = "scatter"
        elif st["step"] == "scatter":
            sums, slots, _ = _scatter_wait(st["flight"], recent)
            filled = _add_slots(sums, slots)
            st["flight"], latest["token"] = _split_copy_start("join", _join_plan, len(filled), [], filled, latest["token"],
                                                              sibling_only=True)
            st["step"] = "join"
        else:
            _, summed, _ = _split_copy_wait("join", _join_plan, st["flight"], recent)
            updates.append((st["l"], st["g"], summed[:len(GROUPS[st["g"]])]))
            if st["g"] == 0:
                small_sums[st["l"]] = summed[-1]
            st["step"] = "done"
        st["age"] = 0

    def hook(recent):
        for st in reversed(list(pending)):
            st["age"] += 1
            if st["age"] >= SCATTER_HOOKS or st["step"] != "scatter":
                advance(st, recent)
                if st["step"] == "done":
                    pending.remove(st)
        return latest["token"]

    def update(count, recent):
        for l, g, totals in updates[:count]:
            names = GROUPS[g]
            new = _adamw_layer(l, [big_w[n] for n in names], [big_m[n] for n in names], [big_v[n] for n in names], totals,
                               [big_outs[n] for n in names], ADAMW_STEPS[g], latest["token"])
            big_outs.update(zip(names, new))
            recent = new[-1][1]
        del updates[:count]
        return recent

    grad_x, sq = _loss_head(act, loss_target[0])
    latest["token"] = sq
    grads = [None] * depth
    for l in reversed(range(depth)):
        dz, g_mlp = _bwd_mlp(grad_x, layers[l], saved[l], latest["token"], hook)
        hook(g_mlp["w_down"])
        begin(l, 1, [g_mlp[n] for n in GROUPS[1]])
        grad_x, g_mix = _bwd_mix(dz, layers[l], saved[l], latest["token"], hook)
        grads[l] = dict(g_mlp, **g_mix)
        hook(g_mix["w_o"])
        begin(l, 0, [g_mix[n] for n in GROUPS[0]] + [_pack_layer([grads[l][n] for n in SMALL])])
    recent = g_mix["w_o"]
    while pending:
        recent = update(-(-2 * len(updates) // 3), recent)
        hook(recent)
    update(len(updates), recent)
    loss = lax.psum(0.5 / D_MODEL * jnp.sum(sq), ("x", "y", "c"))

    small_sum = _gather_shards([jnp.stack(small_sums)])[0].reshape(depth, SMALL_LAYER_ROWS * LANES)
    grad = {n: [jnp.swapaxes(o, 1, 2) for o in big_outs[n]] if n == "w_in" else big_outs[n] for n in BIG}
    delta = {n: o[1] for n, o in grad.items()}
    new_m = {n: o[2] for n, o in grad.items()}
    new_v = {n: o[3] for n, o in grad.items()}
    grad = {n: o[0] for n, o in grad.items()}
    grad.update(_unpack_layers(small_sum, {n: (3, N_CHIPS * conv_cols) if n == "w_conv" else weights[n].shape[1:] for n in SMALL}))
    grad["w_conv"] = lax.dynamic_slice_in_dim(grad["w_conv"], chip * conv_cols, conv_cols, axis=2)

    results = _adamw_small(*[[src[n] for n in SMALL] for src in (weights, grad, m_in, v_in)])
    for n, (step, moment1, moment2) in zip(SMALL, results):
        delta[n], new_m[n], new_v[n] = step, moment1, moment2

    return (loss, grad_x[None], *[grad[n] for n in WEIGHTS], *[delta[n] for n in WEIGHTS],
            *[new_m[n] for n in WEIGHTS], *[new_v[n] for n in WEIGHTS])
```

```python
import jax
import jax.numpy as jnp
from jax import lax
from jax.experimental import pallas as pl
from jax.experimental.pallas import tpu as pltpu

f32 = jnp.float32
bf16 = jnp.bfloat16

D_MODEL = 1024
DEPTH = 4
CONV_W = 384
POOL_W = 256
SGU_W = 384
IN_W = 3 * CONV_W + POOL_W + 2 * SGU_W
D_FF = 2816
CHUNK = 128
HEAD = 64
POOL_WINDOWS = (2, 4, 8, 16)
ALPHA = float((2 * DEPTH) ** 0.25)
LN_EPS = 1e-5
ADAM_LR = 0.001
ADAM_B1 = 0.9
ADAM_B2 = 0.999
ADAM_EPS = 1e-08
ADAM_WD = 0.01
ADAM_STEP = 10

LANES = 128
TOKEN_TILE = 256
N_CHIPS = 4
VMEM_LIMIT = 56 * 1024 * 1024

BLK_XA, BLK_GB, BLK_GC, BLK_P, BLK_U, BLK_V = 0, 3, 6, 9, 11, 14

MESH = pl.DeviceIdType.MESH


def _params(sem=None):
    return pltpu.CompilerParams(dimension_semantics=sem, vmem_limit_bytes=VMEM_LIMIT)


def _rows(width, tile=TOKEN_TILE):
    return pl.BlockSpec((tile, width), lambda i: (i, 0))


def _resident(shape):
    zeros = (0,) * len(shape)
    return pl.BlockSpec(shape, lambda *_: zeros, pipeline_mode=pl.Buffered(1))


def _nt(a, b):
    return lax.dot_general(a, b, (((1,), (1,)), ((), ())), preferred_element_type=f32)


def _tn(a, b):
    return lax.dot_general(a, b, (((0,), (0,)), ((), ())), preferred_element_type=f32)


def _mm(a, b):
    return jnp.dot(a, b, preferred_element_type=f32)


def _norm_fwd(z):
    mu = jnp.mean(z, axis=-1, keepdims=True)
    zc = z - mu
    var = jnp.mean(zc * zc, axis=-1, keepdims=True)
    rstd = lax.rsqrt(var + LN_EPS)
    return zc * rstd, rstd


def _norm_bwd(dxhat, xhat, rstd):
    m1 = jnp.mean(dxhat, axis=-1, keepdims=True)
    m2 = jnp.mean(dxhat * xhat, axis=-1, keepdims=True)
    return rstd * (dxhat - m1 - xhat * m2)


def _proj(x, w_in_b, after):
    s = x.shape[0]

    def body(x_ref, w_ref, after_ref, p_ref, xb_ref):
        xb = x_ref[...].astype(bf16)
        xb_ref[...] = xb
        p_ref[...] = _nt(xb, w_ref[...])

    return pl.pallas_call(
        body, grid=(s // TOKEN_TILE,),
        in_specs=[_rows(D_MODEL), _resident((IN_W, D_MODEL)), pl.BlockSpec(memory_space=pl.ANY)],
        out_specs=[_rows(IN_W), _rows(D_MODEL)],
        out_shape=[jax.ShapeDtypeStruct((s, IN_W), f32), jax.ShapeDtypeStruct((s, D_MODEL), bf16)],
        name="proj", compiler_params=_params(("arbitrary",)))(x, w_in_b, after)


def _row_ranges(parts):
    out, at = [], 0
    for p in parts:
        out.append((at, at + p.shape[1]))
        at += p.shape[1]
    return out


def _wo_ln1(mix, x, w_o_b, g, b):
    s = x.shape[0]
    n = len(mix)
    ranges = _row_ranges(mix)

    def body(*refs):
        m_refs = refs[:n]
        x_ref, w_ref, g_ref, b_ref, xhat_ref, rstd_ref, hb_ref = refs[n:]
        z = ALPHA * x_ref[...]
        for m_ref, (lo, hi) in zip(m_refs, ranges):
            z = z + _mm(m_ref[...], w_ref[lo:hi, :])
        xhat, rstd = _norm_fwd(z)
        xhat_ref[...] = xhat
        rstd_ref[...] = rstd
        hb_ref[...] = (xhat * g_ref[...] + b_ref[...]).astype(bf16)

    return pl.pallas_call(
        body, grid=(s // TOKEN_TILE,),
        in_specs=[_rows(m.shape[1]) for m in mix] + [_rows(D_MODEL), _resident((D_MODEL, D_MODEL)), _resident((1, D_MODEL)),
                                                     _resident((1, D_MODEL))],
        out_specs=[_rows(D_MODEL), _rows(1), _rows(D_MODEL)],
        out_shape=[jax.ShapeDtypeStruct((s, D_MODEL), f32), jax.ShapeDtypeStruct((s, 1), f32),
                   jax.ShapeDtypeStruct((s, D_MODEL), bf16)],
        name="wo_ln1", compiler_params=_params(("arbitrary",)))(*mix, x, w_o_b, g, b)


def _mlp_fwd(xhat1, g1, b1, w_gu_b, w_down_b, g2, b2, after):
    s = xhat1.shape[0]

    def body(xh_ref, g1_ref, b1_ref, wgu_ref, wd_ref, g2_ref, b2_ref, after_ref, gu_ref, xhat2_ref, rstd2_ref, y_ref):
        h = xh_ref[...] * g1_ref[...] + b1_ref[...]
        gu = _mm(h.astype(bf16), wgu_ref[...])
        gu_ref[...] = gu
        gate = gu[:, :D_FF]
        act = gate * jax.nn.sigmoid(gate) * gu[:, D_FF:]
        z = ALPHA * h + _mm(act.astype(bf16), wd_ref[...])
        xhat2, rstd2 = _norm_fwd(z)
        xhat2_ref[...] = xhat2
        rstd2_ref[...] = rstd2
        y_ref[...] = xhat2 * g2_ref[...] + b2_ref[...]

    vec = _resident((1, D_MODEL))
    return pl.pallas_call(
        body, grid=(s // TOKEN_TILE,),
        in_specs=[_rows(D_MODEL), vec, vec, _resident((D_MODEL, 2 * D_FF)), _resident((D_FF, D_MODEL)), vec, vec,
                  pl.BlockSpec(memory_space=pl.ANY)],
        out_specs=[_rows(2 * D_FF), _rows(D_MODEL), _rows(1), _rows(D_MODEL)],
        out_shape=[jax.ShapeDtypeStruct((s, 2 * D_FF), f32), jax.ShapeDtypeStruct((s, D_MODEL), f32),
                   jax.ShapeDtypeStruct((s, 1), f32), jax.ShapeDtypeStruct((s, D_MODEL), f32)],
        name="mlp_fwd", compiler_params=_params(("arbitrary",)))(xhat1, g1, b1, w_gu_b, w_down_b, g2, b2, after)


def _loss_head(y, target):
    s = y.shape[0]

    def body(y_ref, t_ref, dy_ref, sq_ref):
        @pl.when(pl.program_id(0) == 0)
        def _():
            sq_ref[...] = jnp.zeros_like(sq_ref)

        e = y_ref[...] - t_ref[...]
        dy_ref[...] = e * (1.0 / D_MODEL)
        sq_ref[...] += jnp.sum(e * e, axis=0, keepdims=True)

    return pl.pallas_call(
        body, grid=(s // TOKEN_TILE,),
        in_specs=[_rows(D_MODEL), _rows(D_MODEL)],
        out_specs=[_rows(D_MODEL), pl.BlockSpec((1, D_MODEL), lambda i: (0, 0))],
        out_shape=[jax.ShapeDtypeStruct((s, D_MODEL), f32), jax.ShapeDtypeStruct((1, D_MODEL), f32)],
        name="loss_head", compiler_params=_params(("arbitrary",)))(y, target)


def _mlp_bwd(dy, xhat2, rstd2, g2, gu, w_gu_b, w_down_b, xhat1, rstd1, g1, w_o_b, after):
    s = dy.shape[0]

    def body(dy_ref, xh_ref, rs_ref, g2_ref, gu_ref, wgu_ref, wd_ref, xh1_ref, rs1_ref, g1_ref, wo_ref, after_ref,
             dz_ref, act_ref, dgu_ref, dz1_ref, dz1b_ref, dm_ref, gg_ref, gb_ref, gg1_ref, gb1_ref):
        @pl.when(pl.program_id(0) == 0)
        def _():
            for ref in (gg_ref, gb_ref, gg1_ref, gb1_ref):
                ref[...] = jnp.zeros_like(ref)

        dy_t = dy_ref[...]
        xhat = xh_ref[...]
        gg_ref[...] += jnp.sum(dy_t * xhat, axis=0, keepdims=True)
        gb_ref[...] += jnp.sum(dy_t, axis=0, keepdims=True)
        dz = _norm_bwd(dy_t * g2_ref[...], xhat, rs_ref[...])
        dzb = dz.astype(bf16)
        dz_ref[...] = dzb
        dact = _nt(dzb, wd_ref[...])
        gate = gu_ref[:, :D_FF]
        up = gu_ref[:, D_FF:]
        sg = jax.nn.sigmoid(gate)
        silu = gate * sg
        act_ref[...] = (silu * up).astype(bf16)
        dgu_ref[:, :D_FF] = (dact * up * (sg * (1.0 + gate * (1.0 - sg)))).astype(bf16)
        dgu_ref[:, D_FF:] = (dact * silu).astype(bf16)
        dh = ALPHA * dz + _nt(dgu_ref[...], wgu_ref[...])
        xhat1 = xh1_ref[...]
        gg1_ref[...] += jnp.sum(dh * xhat1, axis=0, keepdims=True)
        gb1_ref[...] += jnp.sum(dh, axis=0, keepdims=True)
        dz1 = _norm_bwd(dh * g1_ref[...], xhat1, rs1_ref[...])
        dz1_ref[...] = dz1
        dz1b = dz1.astype(bf16)
        dz1b_ref[...] = dz1b
        dm_ref[...] = _nt(dz1b, wo_ref[...])

    vec, vec_out = _resident((1, D_MODEL)), pl.BlockSpec((1, D_MODEL), lambda i: (0, 0))
    tokens_f32, tokens_bf16 = jax.ShapeDtypeStruct((s, D_MODEL), f32), jax.ShapeDtypeStruct((s, D_MODEL), bf16)
    sums = jax.ShapeDtypeStruct((1, D_MODEL), f32)
    return pl.pallas_call(
        body, grid=(s // TOKEN_TILE,),
        in_specs=[_rows(D_MODEL), _rows(D_MODEL), _rows(1), vec, _rows(2 * D_FF),
                  _resident((D_MODEL, 2 * D_FF)), _resident((D_FF, D_MODEL)), _rows(D_MODEL), _rows(1), vec,
                  _resident((D_MODEL, D_MODEL)), pl.BlockSpec(memory_space=pl.ANY)],
        out_specs=[_rows(D_MODEL), _rows(D_FF), _rows(2 * D_FF), _rows(D_MODEL), _rows(D_MODEL), _rows(D_MODEL),
                   vec_out, vec_out, vec_out, vec_out],
        out_shape=[tokens_bf16, jax.ShapeDtypeStruct((s, D_FF), bf16), jax.ShapeDtypeStruct((s, 2 * D_FF), bf16),
                   tokens_f32, tokens_bf16, tokens_f32, sums, sums, sums, sums],
        name="mlp_bwd", compiler_params=_params(("arbitrary",)))(
            dy, xhat2, rstd2, g2, gu, w_gu_b, w_down_b, xhat1, rstd1, g1, w_o_b, after)


def _dx(dz1, dparts, w_in_t, after):
    s = dz1.shape[0]
    n = len(dparts)
    ranges = _row_ranges(dparts)

    def body(*refs):
        d_refs = refs[:n]
        dz_ref, w_ref, _, dx_ref = refs[n:]
        acc = ALPHA * dz_ref[...]
        for d_ref, (lo, hi) in zip(d_refs, ranges):
            acc = acc + _mm(d_ref[...], w_ref[lo:hi, :])
        dx_ref[...] = acc

    return pl.pallas_call(
        body, grid=(s // TOKEN_TILE,),
        in_specs=[_rows(d.shape[1]) for d in dparts] + [_rows(D_MODEL), _resident((IN_W, D_MODEL)),
                                                        pl.BlockSpec(memory_space=pl.ANY)],
        out_specs=_rows(D_MODEL),
        out_shape=jax.ShapeDtypeStruct((s, D_MODEL), f32),
        name="dx", compiler_params=_params(("arbitrary",)))(*dparts, dz1, w_in_t, after)


def _weight_grad_rows(parts, b, bn):
    s, n_cols = b.shape
    n = len(parts)
    ranges = _row_ranges(parts)
    m = ranges[-1][1]

    def body(*refs):
        p_refs = refs[:n]
        b_ref, o_ref = refs[n:]
        for p_ref, (lo, hi) in zip(p_refs, ranges):
            o_ref[lo:hi, :] = _tn(p_ref[...], b_ref[...]).astype(bf16)

    return pl.pallas_call(
        body, grid=(n_cols // bn,),
        in_specs=[_resident(p.shape) for p in parts] + [pl.BlockSpec((s, bn), lambda j: (0, j))],
        out_specs=pl.BlockSpec((m, bn), lambda j: (0, j)),
        out_shape=jax.ShapeDtypeStruct((m, n_cols), bf16),
        name="weight_grad_rows", compiler_params=_params(("arbitrary",)))(*parts, b)


def _weight_grad(a, b, bm, bn, after):
    s, m = a.shape
    n = b.shape[1]

    def body(a_ref, b_ref, after_ref, o_ref):
        o_ref[...] = _tn(a_ref[...], b_ref[...]).astype(bf16)

    return pl.pallas_call(
        body, grid=(m // bm, n // bn),
        in_specs=[pl.BlockSpec((s, bm), lambda i, j: (0, i)), pl.BlockSpec((s, bn), lambda i, j: (0, j)),
                  pl.BlockSpec(memory_space=pl.ANY)],
        out_specs=pl.BlockSpec((bm, bn), lambda i, j: (i, j)),
        out_shape=jax.ShapeDtypeStruct((m, n), bf16),
        name="weight_grad", compiler_params=_params(("arbitrary", "arbitrary")))(a, b, after)


def _shift_down(a, k):
    row = lax.broadcasted_iota(jnp.int32, a.shape, 0)
    return jnp.where(row >= k, pltpu.roll(a, k, 0), 0.0)


def _shift_up(a, k):
    n = a.shape[0]
    row = lax.broadcasted_iota(jnp.int32, a.shape, 0)
    return jnp.where(row < n - k, pltpu.roll(a, n - k, 0), 0.0)


def _slab(s, block):
    return pl.BlockSpec((s, LANES), lambda k: (0, block + k))


def _conv_y(z, w):
    return w[0:1, :] * _shift_down(z, 2) + w[1:2, :] * _shift_down(z, 1) + w[2:3, :] * z


def _conv_fwd(proj, w_conv):
    s = proj.shape[0]

    def body(xa_ref, gb_ref, gc_ref, w_ref, o_ref):
        z = gc_ref[...] * xa_ref[...]
        o_ref[...] = (gb_ref[...] * _conv_y(z, w_ref[...])).astype(bf16)

    return pl.pallas_call(
        body, grid=(CONV_W // LANES,),
        in_specs=[_slab(s, BLK_XA), _slab(s, BLK_GB), _slab(s, BLK_GC), pl.BlockSpec((3, LANES), lambda k: (0, k))],
        out_specs=_slab(s, 0),
        out_shape=jax.ShapeDtypeStruct((s, CONV_W), bf16),
        name="conv_fwd", compiler_params=_params(("arbitrary",)))(proj, proj, proj, w_conv)


def _conv_bwd(proj, dmix, w_conv, after):
    s = proj.shape[0]

    def body(xa_ref, gb_ref, gc_ref, dy_ref, w_ref, after_ref, dxa_ref, dgb_ref, dgc_ref, dw_ref):
        xa = xa_ref[...]
        gc = gc_ref[...]
        w = w_ref[...]
        z = gc * xa
        dya = dy_ref[...]
        dgb_ref[...] = (dya * _conv_y(z, w)).astype(bf16)
        dy = dya * gb_ref[...]
        dz = w[2:3, :] * dy + w[1:2, :] * _shift_up(dy, 1) + w[0:1, :] * _shift_up(dy, 2)
        dxa_ref[...] = (dz * gc).astype(bf16)
        dgc_ref[...] = (dz * xa).astype(bf16)
        dw_ref[0:1, :] = jnp.sum(dy * _shift_down(z, 2), axis=0, keepdims=True)
        dw_ref[1:2, :] = jnp.sum(dy * _shift_down(z, 1), axis=0, keepdims=True)
        dw_ref[2:3, :] = jnp.sum(dy * z, axis=0, keepdims=True)

    out = jax.ShapeDtypeStruct((s, CONV_W), bf16)
    return pl.pallas_call(
        body, grid=(CONV_W // LANES,),
        in_specs=[_slab(s, BLK_XA), _slab(s, BLK_GB), _slab(s, BLK_GC), _slab(s, 0), pl.BlockSpec((3, LANES), lambda k: (0, k)),
                  pl.BlockSpec(memory_space=pl.ANY)],
        out_specs=[_slab(s, 0), _slab(s, 0), _slab(s, 0), pl.BlockSpec((3, LANES), lambda k: (0, k))],
        out_shape=[out, out, out, jax.ShapeDtypeStruct((3, CONV_W), f32)],
        name="conv_bwd", compiler_params=_params(("arbitrary",)))(proj, proj, proj, dmix, w_conv, after)


def _pool_window(k):
    lane = lax.broadcasted_iota(jnp.int32, (1, LANES), 1)
    low = lane < HEAD
    first = k == 0
    wlen = jnp.where(low, jnp.where(first, POOL_WINDOWS[0], POOL_WINDOWS[2]), jnp.where(first, POOL_WINDOWS[1], POOL_WINDOWS[3]))
    return wlen, low, first


def _pool_diff(p, k):
    wlen, low, first = _pool_window(k)
    s2 = p + _shift_down(p, 1)
    s4 = s2 + _shift_down(s2, 2)
    s8 = s4 + _shift_down(s4, 4)
    s16 = s8 + _shift_down(s8, 8)
    win = jnp.where(low, jnp.where(first, s2, s8), jnp.where(first, s4, s16))
    row = lax.broadcasted_iota(jnp.int32, p.shape, 0)
    count = jnp.minimum(row + 1, wlen).astype(f32)
    return win / count - p, count


def _pool_weight(w_ref):
    zero = jnp.zeros((HEAD, HEAD), f32)
    top = jnp.concatenate([w_ref[0], zero], axis=1)
    bottom = jnp.concatenate([zero, w_ref[1]], axis=1)
    return jnp.concatenate([top, bottom], axis=0).astype(bf16)


def _pool_fwd(proj, w_pool, pool_scale):
    s = proj.shape[0]

    def body(p_ref, w_ref, sc_ref, o_ref):
        d, _ = _pool_diff(p_ref[...], pl.program_id(0))
        o_ref[...] = (_mm(d.astype(bf16), _pool_weight(w_ref)) * sc_ref[...]).astype(bf16)

    return pl.pallas_call(
        body, grid=(POOL_W // LANES,),
        in_specs=[_slab(s, BLK_P), pl.BlockSpec((2, HEAD, HEAD), lambda k: (k, 0, 0)), pl.BlockSpec((1, LANES), lambda k: (0, k))],
        out_specs=_slab(s, 0),
        out_shape=jax.ShapeDtypeStruct((s, POOL_W), bf16),
        name="pool_fwd", compiler_params=_params(("arbitrary",)))(proj, w_pool, pool_scale)


def _pool_bwd(proj, dmix, w_pool, pool_scale):
    s = proj.shape[0]

    def body(p_ref, dy_ref, w_ref, sc_ref, dp_ref, dw_ref, dsc_ref):
        k = pl.program_id(0)
        d, count = _pool_diff(p_ref[...], k)
        wbd = _pool_weight(w_ref)
        db = d.astype(bf16)
        dyb = dy_ref[...]
        dsc_ref[...] = jnp.sum(dyb * _mm(db, wbd), axis=0, keepdims=True)
        dpre = (dyb * sc_ref[...]).astype(bf16)
        dwbd = _tn(db, dpre)
        dw_ref[0] = dwbd[:HEAD, :HEAD]
        dw_ref[1] = dwbd[HEAD:, HEAD:]
        dd = _nt(dpre, wbd)
        e = dd / count
        wlen, low, first = _pool_window(k)
        a2 = e + _shift_up(e, 1)
        a4 = a2 + _shift_up(a2, 2)
        a8 = a4 + _shift_up(a4, 4)
        a16 = a8 + _shift_up(a8, 8)
        back = jnp.where(low, jnp.where(first, a2, a8), jnp.where(first, a4, a16))
        dp_ref[...] = (back - dd).astype(bf16)

    return pl.pallas_call(
        body, grid=(POOL_W // LANES,),
        in_specs=[_slab(s, BLK_P), _slab(s, CONV_W // LANES), pl.BlockSpec((2, HEAD, HEAD), lambda k: (k, 0, 0)),
                  pl.BlockSpec((1, LANES), lambda k: (0, k))],
        out_specs=[_slab(s, 0), pl.BlockSpec((2, HEAD, HEAD), lambda k: (k, 0, 0)), pl.BlockSpec((1, LANES), lambda k: (0, k))],
        out_shape=[jax.ShapeDtypeStruct((s, POOL_W), bf16), jax.ShapeDtypeStruct((4, HEAD, HEAD), f32),
                   jax.ShapeDtypeStruct((1, POOL_W), f32)],
        name="pool_bwd", compiler_params=_params(("arbitrary",)))(proj, dmix, w_pool, pool_scale)


SGU_UNROLL = 4
INV_SQRT2 = 0.7071067811865476
INV_SQRT_2PI = 0.3989422804014327


def _gelu(x):
    return 0.5 * x * (1.0 + lax.erf(x * INV_SQRT2))


def _gelu_grad(x):
    return 0.5 * (1.0 + lax.erf(x * INV_SQRT2)) + x * (INV_SQRT_2PI * jnp.exp(-0.5 * x * x))


def _head_mean(a, low):
    s_low = jnp.sum(jnp.where(low, a, 0.0), axis=-1, keepdims=True)
    s_high = jnp.sum(jnp.where(low, 0.0, a), axis=-1, keepdims=True)
    return jnp.where(low, s_low, s_high) * (1.0 / HEAD)


def _tril():
    r = lax.broadcasted_iota(jnp.int32, (CHUNK, CHUNK), 0)
    c = lax.broadcasted_iota(jnp.int32, (CHUNK, CHUNK), 1)
    return r >= c


def _sgu_chunk(up, vp, g, wm0, wm1, b0, b1, low):
    ug = _gelu(up)
    vg = _gelu(vp)
    vc = vg - _head_mean(vg, low)
    rstd = lax.rsqrt(_head_mean(vc * vc, low) + LN_EPS)
    vn = vc * rstd
    vb = (vn * g).astype(bf16)
    mixed = jnp.where(low, _mm(wm0, vb) + b0, _mm(wm1, vb) + b1)
    return ug, vn, rstd, vb, mixed


def _sgu_specs(s):
    return [_slab(s, BLK_U), _slab(s, BLK_V), pl.BlockSpec((1, LANES), lambda k: (0, k)),
            pl.BlockSpec((2, CHUNK, CHUNK), lambda k: (k, 0, 0)), pl.BlockSpec((2, CHUNK, 1), lambda k: (k, 0, 0))]


def _sgu_fwd(proj, sgu_g, w_spatial, b_spatial3):
    s = proj.shape[0]

    def body(u_ref, v_ref, g_ref, w_ref, b_ref, o_ref):
        low = lax.broadcasted_iota(jnp.int32, (1, LANES), 1) < HEAD
        mask = _tril()
        wm0 = jnp.where(mask, w_ref[0], 0.0).astype(bf16)
        wm1 = jnp.where(mask, w_ref[1], 0.0).astype(bf16)
        g = g_ref[...]
        b0 = b_ref[0]
        b1 = b_ref[1]

        def chunk(n, carry):
            rows = pl.ds(pl.multiple_of(n * CHUNK, CHUNK), CHUNK)
            ug, _, _, _, mixed = _sgu_chunk(u_ref[rows, :], v_ref[rows, :], g, wm0, wm1, b0, b1, low)
            o_ref[rows, :] = (ug * mixed).astype(bf16)
            return carry

        lax.fori_loop(0, s // CHUNK, chunk, 0, unroll=SGU_UNROLL)

    return pl.pallas_call(
        body, grid=(SGU_W // LANES,),
        in_specs=_sgu_specs(s),
        out_specs=_slab(s, 0),
        out_shape=jax.ShapeDtypeStruct((s, SGU_W), bf16),
        name="sgu_fwd", compiler_params=_params(("arbitrary",)))(proj, proj, sgu_g, w_spatial, b_spatial3)


def _sgu_bwd(proj, dmix, sgu_g, w_spatial, b_spatial3):
    s = proj.shape[0]

    def body(u_ref, v_ref, g_ref, w_ref, b_ref, dy_ref, du_ref, dv_ref, dg_ref, dw_ref, db_ref):
        low = lax.broadcasted_iota(jnp.int32, (1, LANES), 1) < HEAD
        mask = _tril()
        w0 = jnp.where(mask, w_ref[0], 0.0)
        w1 = jnp.where(mask, w_ref[1], 0.0)
        wm0 = w0.astype(bf16)
        wm1 = w1.astype(bf16)
        wt0 = w0.T.astype(bf16)
        wt1 = w1.T.astype(bf16)
        g = g_ref[...]
        b0 = b_ref[0]
        b1 = b_ref[1]
        dg_ref[...] = jnp.zeros_like(dg_ref)
        dw_ref[...] = jnp.zeros_like(dw_ref)
        db_ref[...] = jnp.zeros_like(db_ref)

        def chunk(n, carry):
            rows = pl.ds(pl.multiple_of(n * CHUNK, CHUNK), CHUNK)
            up = u_ref[rows, :]
            vp = v_ref[rows, :]
            ug, vn, rstd, vb, mixed = _sgu_chunk(up, vp, g, wm0, wm1, b0, b1, low)
            dy = dy_ref[rows, :]
            du_ref[rows, :] = (dy * mixed * _gelu_grad(up)).astype(bf16)
            dmix_c = dy * ug
            db_ref[0] += jnp.sum(jnp.where(low, dmix_c, 0.0), axis=-1, keepdims=True)
            db_ref[1] += jnp.sum(jnp.where(low, 0.0, dmix_c), axis=-1, keepdims=True)
            dmb = dmix_c.astype(bf16)
            zero = jnp.zeros_like(dmb)
            dw_ref[0] += _nt(jnp.where(low, dmb, zero), vb)
            dw_ref[1] += _nt(jnp.where(low, zero, dmb), vb)
            dvnorm = jnp.where(low, _mm(wt0, dmb), _mm(wt1, dmb))
            dg_ref[...] += jnp.sum(dvnorm * vn, axis=0, keepdims=True)
            dvn = dvnorm * g
            dvg = rstd * (dvn - _head_mean(dvn, low) - vn * _head_mean(dvn * vn, low))
            dv_ref[rows, :] = (dvg * _gelu_grad(vp)).astype(bf16)
            return carry

        lax.fori_loop(0, s // CHUNK, chunk, 0, unroll=SGU_UNROLL)
        dw_ref[0] = jnp.where(mask, dw_ref[0], 0.0)
        dw_ref[1] = jnp.where(mask, dw_ref[1], 0.0)

    out = jax.ShapeDtypeStruct((s, SGU_W), bf16)
    return pl.pallas_call(
        body, grid=(SGU_W // LANES,),
        in_specs=_sgu_specs(s) + [_slab(s, (CONV_W + POOL_W) // LANES)],
        out_specs=[_slab(s, 0), _slab(s, 0), pl.BlockSpec((1, LANES), lambda k: (0, k)),
                   pl.BlockSpec((2, CHUNK, CHUNK), lambda k: (k, 0, 0)), pl.BlockSpec((2, CHUNK, 1), lambda k: (k, 0, 0))],
        out_shape=[out, out, jax.ShapeDtypeStruct((1, SGU_W), f32), jax.ShapeDtypeStruct((6, CHUNK, CHUNK), f32),
                   jax.ShapeDtypeStruct((6, CHUNK, 1), f32)],
        name="sgu_bwd", compiler_params=_params(("arbitrary",)))(proj, proj, sgu_g, w_spatial, b_spatial3, dmix)


def _fwd_mix(x, w, after):
    proj, xb = _proj(x, w["w_in"], after)
    mix = [_conv_fwd(proj, w["w_conv"]), _pool_fwd(proj, w["w_pool"], w["pool_scale"]),
           _sgu_fwd(proj, w["sgu_ln_g"], w["w_spatial"], w["b_spatial"])]
    xhat1, rstd1, hb = _wo_ln1(mix, x, w["w_o"], w["ln1_g"], w["ln1_b"])
    return dict(proj=proj, xb=xb, mix=mix, xhat1=xhat1, rstd1=rstd1, hb=hb)


def _fwd_mlp(sv, w, after):
    gu, xhat2, rstd2, y = _mlp_fwd(sv["xhat1"], w["ln1_g"], w["ln1_b"], w["w_gate_up"], w["w_down"], w["ln2_g"], w["ln2_b"], after)
    sv.update(gu=gu, xhat2=xhat2, rstd2=rstd2)
    return y


def _bwd_mlp(dy, w, sv, after, hook):
    dz2b, actb, dgub, dz1, dz1b, dmix, g_ln2_g, g_ln2_b, g_ln1_g, g_ln1_b = _mlp_bwd(
        dy, sv["xhat2"], sv["rstd2"], w["ln2_g"], sv["gu"], w["w_gate_up"], w["w_down"], sv["xhat1"], sv["rstd1"], w["ln1_g"],
        w["w_o"], after)
    after = hook(dz1)
    grads = dict(w_gate_up=_weight_grad(sv["hb"], dgub, D_MODEL, D_FF // 2, after),
                 w_down=_weight_grad(actb, dz2b, D_FF // 2, D_MODEL, after),
                 ln2_g=g_ln2_g, ln2_b=g_ln2_b, ln1_g=g_ln1_g, ln1_b=g_ln1_b)
    return (dz1, dz1b, dmix), grads


def _bwd_mix(dz, w, sv, after, hook):
    dz1, dz1b, dmix = dz
    dxa, dgb, dgc, g_conv = _conv_bwd(sv["proj"], dmix, w["w_conv"], after)
    dp, g_pool, g_pscale = _pool_bwd(sv["proj"], dmix, w["w_pool"], w["pool_scale"])
    du, dv, g_sgu_g, g_spatial, g_bsp = _sgu_bwd(sv["proj"], dmix, w["sgu_ln_g"], w["w_spatial"], w["b_spatial"])
    dparts = [dxa, dgb, dgc, dp, du, dv]
    dx = _dx(dz1, dparts, w["w_in"], hook(du))
    grads = dict(
        w_in=_weight_grad_rows(dparts, sv["xb"], 512), w_o=_weight_grad_rows(sv["mix"], dz1b, D_MODEL),
        w_conv=g_conv, w_pool=g_pool, pool_scale=g_pscale, sgu_ln_g=g_sgu_g, w_spatial=g_spatial,
        b_spatial=g_bsp.reshape(6, CHUNK))
    return dx, grads


def _local_step(x, target, layers):
    saved = []
    for w in layers:
        sv = _fwd_mix(x, w, x)
        x = _fwd_mlp(sv, w, x)
        saved.append(sv)
    dy, sq = _loss_head(x, target)
    grads = [None] * len(layers)
    for l in reversed(range(len(layers))):
        dz, g_mlp = _bwd_mlp(dy, layers[l], saved[l], sq, lambda a: a)
        dy, g_mix = _bwd_mix(dz, layers[l], saved[l], dz[0], lambda a: a)
        grads[l] = dict(g_mlp, **g_mix)
    return sq, dy, grads


ANY = pl.BlockSpec(memory_space=pl.ANY)


def _place():
    x, y, c = lax.axis_index("x"), lax.axis_index("y"), lax.axis_index("c")
    others = [(1 - x, y), (x, 1 - y), (1 - x, 1 - y)]
    return x, y, c, others


def _chip_index(cx, cy):
    return 2 * cx + cy


def _half(ref_rows, c):
    half = ref_rows // 2
    return pl.ds(pl.multiple_of(c * half, 8), half)


def _remote(src, dst, send_sem, recv_sem, device):
    return pltpu.make_async_remote_copy(src_ref=src, dst_ref=dst, send_sem=send_sem, recv_sem=recv_sem,
                                        device_id=device, device_id_type=MESH)


def _gather_shards(shards):
    n = len(shards)
    base, total = [], 0
    for s in shards:
        base.append(total)
        total += 6 * s.shape[0]

    def body(*refs):
        ins, outs = refs[:n], refs[n:2 * n]
        send, recv = refs[2 * n:]
        x, y, c, others = _place()
        me = _chip_index(x, y)
        sib = (x, y, 1 - c)
        sends = []
        for f in range(n):
            depth, rows = ins[f].shape[0], ins[f].shape[1]
            for l in range(depth):
                for k, (cx, cy) in enumerate(others):
                    sem = base[f] + 6 * l + k
                    cp = _remote(ins[f].at[l, _half(rows, c)], outs[f].at[l, me, _half(rows, c)],
                                 send.at[sem], recv.at[sem], (cx, cy, c))
                    cp.start()
                    sends.append(cp)
        for f in range(n):
            depth, rows = ins[f].shape[0], ins[f].shape[1]
            for l in range(depth):
                for k, (cx, cy) in enumerate(others):
                    sem = base[f] + 6 * l + k
                    landed = outs[f].at[l, _chip_index(cx, cy), _half(rows, c)]
                    _remote(landed, landed, send.at[sem], recv.at[sem], (cx, cy, c)).wait_recv()
                    cp = _remote(landed, landed, send.at[sem + 3], recv.at[sem + 3], sib)
                    cp.start()
                    sends.append(cp)
        for f in range(n):
            depth, rows = ins[f].shape[0], ins[f].shape[1]
            for l in range(depth):
                for k, (cx, cy) in enumerate(others):
                    sem = base[f] + 6 * l + k + 3
                    passed = outs[f].at[l, _chip_index(cx, cy), _half(rows, 1 - c)]
                    _remote(passed, passed, send.at[sem], recv.at[sem], sib).wait_recv()
        for cp in sends:
            cp.wait_send()

    gathered = pl.pallas_call(
        body, in_specs=[ANY] * n, out_specs=[ANY] * n,
        out_shape=[jax.ShapeDtypeStruct((s.shape[0], N_CHIPS) + s.shape[1:], s.dtype) for s in shards],
        scratch_shapes=[pltpu.SemaphoreType.DMA((total,)), pltpu.SemaphoreType.DMA((total,))],
        name="gather_shards")(*shards)
    return [_place_own(g, s) for g, s in zip(gathered, shards)]


def _scalar(value):
    return jnp.reshape(value, (1,)).astype(jnp.int32)


def _place_own(blocks, shard):
    depth, rows, cols = shard.shape

    def body(me_ref, b_ref, s_ref, o_ref):
        o_ref[...] = s_ref[...]

    return pl.pallas_call(
        body,
        grid_spec=pltpu.PrefetchScalarGridSpec(
            num_scalar_prefetch=1, grid=(depth,),
            in_specs=[ANY, pl.BlockSpec((None, rows, cols), lambda l, me: (l, 0, 0))],
            out_specs=pl.BlockSpec((None, None, rows, cols), lambda l, me: (l, me[0], 0, 0))),
        out_shape=jax.ShapeDtypeStruct(blocks.shape, blocks.dtype),
        input_output_aliases={1: 0},
        name="place_own", compiler_params=_params(("arbitrary",)))(
            _scalar(_chip_index(lax.axis_index("x"), lax.axis_index("y"))), blocks, shard)


HBM = pl.BlockSpec(memory_space=pltpu.HBM)
SEM = pl.BlockSpec(memory_space=pltpu.SEMAPHORE)
TOKEN = jax.ShapeDtypeStruct((8, LANES), f32)
SPLIT_COPY = pltpu.CompilerParams(has_side_effects=pltpu.SideEffectType.DATAFLOW_SIDE_EFFECTING)


def _in_hbm(a):
    return pltpu.with_memory_space_constraint(a, pltpu.HBM)


def _full_shape(shard, axis):
    rows, cols = shard.shape
    return (N_CHIPS * rows, cols) if axis == 0 else (rows, N_CHIPS * cols)


def _block_half(ref, axis, j, h):
    if axis == 0:
        rows = ref.shape[0] // N_CHIPS
        return ref.at[pl.ds(pl.multiple_of(j * rows + h * (rows // 2), 16), rows // 2), :]
    half, cols = ref.shape[0] // 2, ref.shape[1] // N_CHIPS
    return ref.at[pl.ds(pl.multiple_of(h * half, 16), half), pl.ds(pl.multiple_of(j * cols, LANES), cols)]


def _place_layer(shards, axes, after):
    n = len(shards)

    def body(me_ref, *refs):
        ins, outs = refs[n:2 * n], refs[2 * n + 1:]
        for f in range(n):
            block = ins[f][...].astype(bf16)
            outs[f][...] = block
            outs[n + f][...] = block

    lands = [lax.empty(_full_shape(s, ax), bf16) for s, ax in zip(shards, axes)]
    own = [pl.BlockSpec(s.shape, lambda i, me: (0, 0)) for s in shards]
    outs = pl.pallas_call(
        body,
        grid_spec=pltpu.PrefetchScalarGridSpec(
            num_scalar_prefetch=1, grid=(1,),
            in_specs=[ANY] * n + own + [ANY],
            out_specs=own + [pl.BlockSpec(s.shape, (lambda i, me: (me[0], 0)) if ax == 0 else (lambda i, me: (0, me[0])))
                             for s, ax in zip(shards, axes)]),
        out_shape=[jax.ShapeDtypeStruct(s.shape, bf16) for s in shards] + [jax.ShapeDtypeStruct(a.shape, bf16) for a in lands],
        input_output_aliases={1 + f: n + f for f in range(n)},
        name="place_layer", compiler_params=_params(("arbitrary",)))(
            _scalar(_chip_index(lax.axis_index("x"), lax.axis_index("y"))), *lands, *shards, after)
    return outs[:n], outs[n:]


def _gather_start(shards, lands, axes, after):
    return _split_copy_start("gather", _gather_plan(axes), 3 * len(shards), shards, lands, after)


def _gather_wait(state, axes, after):
    return _split_copy_wait("gather", _gather_plan(axes), state, after)


SIBLING_PAIR_ID = 0


def _split_copy_start(name, plan, count, ins, lands, after, sibling_only=False):
    arrays = list(ins) + list(lands)
    n_in, n = len(ins), len(arrays)

    def body(*refs):
        send, recv, token = refs[n + 1], refs[n + 2], refs[-1]
        if sibling_only:
            x, y, c, _ = _place()
            barrier = pltpu.get_barrier_semaphore()
            pl.semaphore_signal(barrier, inc=1, device_id=(x, y, 1 - c), device_id_type=MESH)
            pl.semaphore_wait(barrier, 1)
        for i, (src, dst, _, peer) in enumerate(plan(refs[:n_in], refs[n_in:n])):
            _remote(src, dst, send.at[i], recv.at[i], peer).start()
        token[...] = jnp.zeros_like(token)

    effect = pltpu.SideEffectType.DATAFLOW_SIDE_EFFECTING
    outs = pl.pallas_call(
        body, name=name + "_start",
        in_specs=[HBM] * n + [ANY],
        out_specs=(SEM, SEM, *[HBM] * n, pl.BlockSpec(memory_space=pltpu.VMEM)),
        out_shape=(pltpu.SemaphoreType.DMA((count,)), pltpu.SemaphoreType.DMA((count,)),
                   *[pltpu.HBM(a.shape, a.dtype) for a in arrays], TOKEN),
        input_output_aliases={i: 2 + i for i in range(n)},
        compiler_params=pltpu.CompilerParams(has_side_effects=effect, collective_id=SIBLING_PAIR_ID) if sibling_only
        else SPLIT_COPY)(*[_in_hbm(a) for a in arrays], after)
    return (outs[0], outs[1], outs[2:2 + n_in], outs[2 + n_in:2 + n]), outs[-1]


def _split_copy_wait(name, plan, state, after):
    send_sems, recv_sems, ins, lands = state
    arrays = list(ins) + list(lands)
    n_in, n = len(ins), len(arrays)

    def body(*refs):
        send, recv, token = refs[n], refs[n + 1], refs[-1]
        for i, (src, _, landing, peer) in enumerate(plan(refs[:n_in], refs[n_in:n])):
            cp = _remote(src, landing, send.at[i], recv.at[i], peer)
            cp.wait_send()
            cp.wait_recv()
        token[...] = jnp.zeros_like(token)

    outs = pl.pallas_call(
        body, name=name + "_wait",
        in_specs=[HBM] * n + [SEM, SEM, ANY],
        out_specs=(*[HBM] * n, pl.BlockSpec(memory_space=pltpu.VMEM)),
        out_shape=(*[pltpu.HBM(a.shape, a.dtype) for a in arrays], TOKEN),
        input_output_aliases={i: i for i in range(n)},
        compiler_params=SPLIT_COPY)(*arrays, send_sems, recv_sems, after)
    return outs[:n_in], outs[n_in:n], outs[-1]


def _gather_plan(axes):
    def plan(ins, lnd):
        x, y, c, others = _place()
        me = _chip_index(x, y)
        return [(ins[f].at[_half(ins[f].shape[0], c)], _block_half(lnd[f], ax, me, c),
                 _block_half(lnd[f], ax, _chip_index(cx, cy), c), (cx, cy, c))
                for f, ax in enumerate(axes) for cx, cy in others]
    return plan


def _pair_plan(axes):
    def plan(ins, lnd):
        x, y, c, _ = _place()
        return [(_block_half(ins[f], ax, j, 1 - c), lnd[f].at[j], lnd[f].at[j], (x, y, 1 - c))
                for f, ax in enumerate(axes) for j in range(N_CHIPS)]
    return plan


def _scatter_plan(ins, lnd):
    x, y, c, others = _place()
    return [(ins[f].at[_chip_index(cx, cy)], lnd[f].at[k], lnd[f].at[k], (cx, cy, c))
            for f in range(len(ins)) for k, (cx, cy) in enumerate(others)]


def _join_plan(ins, lnd):
    x, y, c, _ = _place()
    return [(lnd[f].at[_half(lnd[f].shape[0], c)], lnd[f].at[_half(lnd[f].shape[0], c)],
             lnd[f].at[_half(lnd[f].shape[0], 1 - c)], (x, y, 1 - c)) for f in range(len(lnd))]


def _gather_finish(lands, axes, after):
    n = len(lands)

    def body(*refs):
        outs = refs[n + 1:2 * n + 1]
        send, recv = refs[2 * n + 1:]
        x, y, c, others = _place()
        sib = (x, y, 1 - c)
        barrier = pltpu.get_barrier_semaphore()
        pl.semaphore_signal(barrier, inc=1, device_id=sib, device_id_type=MESH)
        pl.semaphore_wait(barrier, 1)
        sends = []
        for f in range(n):
            for k, (cx, cy) in enumerate(others):
                landed = _block_half(outs[f], axes[f], _chip_index(cx, cy), c)
                cp = _remote(landed, landed, send.at[3 * f + k], recv.at[3 * f + k], sib)
                cp.start()
                sends.append(cp)
        for f in range(n):
            for k, (cx, cy) in enumerate(others):
                passed = _block_half(outs[f], axes[f], _chip_index(cx, cy), 1 - c)
                _remote(passed, passed, send.at[3 * f + k], recv.at[3 * f + k], sib).wait_recv()
        for cp in sends:
            cp.wait_send()

    return pl.pallas_call(
        body, in_specs=[ANY] * (n + 1), out_specs=[ANY] * n,
        out_shape=[jax.ShapeDtypeStruct(a.shape, a.dtype) for a in lands],
        input_output_aliases={f: f for f in range(n)},
        scratch_shapes=[pltpu.SemaphoreType.DMA((3 * n,)), pltpu.SemaphoreType.DMA((3 * n,))],
        compiler_params=pltpu.CompilerParams(collective_id=SIBLING_PAIR_ID),
        name="gather_finish")(*lands, after)


def _half_blocks(part, axis):
    rows, cols = (part.shape[0] // N_CHIPS, part.shape[1]) if axis == 0 else (part.shape[0], part.shape[1] // N_CHIPS)
    return lax.empty((N_CHIPS, rows // 2, cols), part.dtype)


def _add_pair_layer(parts, gots, axes):
    k = len(parts)

    def body(c_ref, *refs):
        for f in range(k):
            a_ref, b_ref, o_ref = refs[2 * f], refs[2 * f + 1], refs[2 * k + f]
            o_ref[...] = (a_ref[...].astype(f32) + b_ref[...].astype(f32)).astype(o_ref.dtype)

    in_specs, out_specs, operands = [], [], []
    for part, got, axis in zip(parts, gots, axes):
        _, half, cols = got.shape
        if axis == 0:
            part = part.reshape(N_CHIPS, 2, half, cols)
            mine = pl.BlockSpec((None, None, half, cols), lambda j, c: (j, c[0], 0, 0))
        else:
            mine = pl.BlockSpec((half, cols), lambda j, c: (c[0], j))
        block = pl.BlockSpec((None, half, cols), lambda j, c: (j, 0, 0))
        in_specs += [mine, block]
        out_specs.append(block)
        operands += [part, got]
    return pl.pallas_call(
        body,
        grid_spec=pltpu.PrefetchScalarGridSpec(num_scalar_prefetch=1, grid=(N_CHIPS,), in_specs=in_specs, out_specs=out_specs),
        out_shape=[jax.ShapeDtypeStruct(g.shape, p.dtype) for p, g in zip(parts, gots)],
        name="add_pair_layer", compiler_params=_params(("arbitrary",)))(_scalar(lax.axis_index("c")), *operands)


def _scatter_start(sums, after):
    lands = [lax.empty((3,) + s.shape[1:], s.dtype) for s in sums]
    return _split_copy_start("scatter", _scatter_plan, 3 * len(sums), sums, lands, after)


def _scatter_wait(state, after):
    return _split_copy_wait("scatter", _scatter_plan, state, after)


def _add_slots(chip_sums, slots):
    k = len(chip_sums)

    def body(at_ref, *refs):
        for f in range(k):
            own_ref, s_ref, o_ref = refs[2 * f], refs[2 * f + 1], refs[2 * k + f]
            acc = own_ref[...].astype(f32)
            for j in range(3):
                acc = acc + s_ref[j].astype(f32)
            o_ref[...] = acc

    in_specs, out_specs, operands = [], [], []
    for cs, s in zip(chip_sums, slots):
        _, half, cols = cs.shape
        in_specs += [pl.BlockSpec((None, half, cols), lambda i, at: (at[0], 0, 0)), pl.BlockSpec((3, half, cols), lambda i, at: (0, 0, 0))]
        out_specs.append(pl.BlockSpec((None, half, cols), lambda i, at: (at[1], 0, 0)))
        operands += [cs, s]
    at = jnp.concatenate([_scalar(_chip_index(lax.axis_index("x"), lax.axis_index("y"))), _scalar(lax.axis_index("c"))])
    outs = pl.pallas_call(
        body,
        grid_spec=pltpu.PrefetchScalarGridSpec(num_scalar_prefetch=1, grid=(1,), in_specs=in_specs, out_specs=out_specs),
        out_shape=[jax.ShapeDtypeStruct((2,) + cs.shape[1:], f32) for cs in chip_sums],
        name="add_slots", compiler_params=_params(("arbitrary",)))(at, *operands)
    return [o.reshape(2 * o.shape[1], o.shape[2]) for o in outs]


def _adamw_math(w, grad, m, v):
    nm = ADAM_B1 * m + (1.0 - ADAM_B1) * grad
    nv = ADAM_B2 * v + (1.0 - ADAM_B2) * (grad * grad)
    m_hat = nm / (1.0 - ADAM_B1 ** ADAM_STEP)
    v_hat = nv / (1.0 - ADAM_B2 ** ADAM_STEP)
    return nm, nv, -ADAM_LR * (m_hat / (jnp.sqrt(v_hat) + ADAM_EPS) + ADAM_WD * w)


def _adamw_small(ws, gs, ms, vs):
    k = len(ws)

    def body(*refs):
        for f in range(k):
            w_ref, g_ref, m_ref, v_ref = refs[4 * f:4 * f + 4]
            d_ref, nm_ref, nv_ref = refs[4 * k + 3 * f:4 * k + 3 * f + 3]
            nm, nv, step = _adamw_math(w_ref[...], g_ref[...], m_ref[...], v_ref[...])
            d_ref[...] = step
            nm_ref[...] = nm
            nv_ref[...] = nv

    whole = pl.BlockSpec(memory_space=pltpu.VMEM)
    res = pl.pallas_call(
        body, in_specs=[whole] * (4 * k), out_specs=[whole] * (3 * k),
        out_shape=[jax.ShapeDtypeStruct(w.shape, f32) for w in ws for _ in range(3)],
        name="adamw_small", compiler_params=_params())(*[a for four in zip(ws, gs, ms, vs) for a in four])
    return [res[3 * f:3 * f + 3] for f in range(k)]


def _adamw_layer(l, ws, ms, vs, gs, outs, steps, after):
    k = len(ws)

    def body(*refs):
        ins, new = refs[:4 * k], refs[8 * k + 1:]
        for f in range(k):
            w_ref, m_ref, v_ref, g_ref = ins[4 * f:4 * f + 4]
            go_ref, d_ref, nm_ref, nv_ref = new[4 * f:4 * f + 4]
            grad = g_ref[...]
            nm, nv, step = _adamw_math(w_ref[...], grad, m_ref[...], v_ref[...])
            go_ref[...] = grad
            d_ref[...] = step
            nm_ref[...] = nm
            nv_ref[...] = nv

    in_specs, out_specs, operands = [], [], []
    for w, m, v, g in zip(ws, ms, vs, gs):
        _, rows, cols = w.shape
        tile = rows // steps
        layer = pl.BlockSpec((None, tile, cols), lambda i: (l, i, 0))
        in_specs += [layer] * 3 + [_rows(cols, tile)]
        out_specs += [layer] * 4
        operands += [w, m, v, g]
    flat_outs = [o for four in outs for o in four]
    res = pl.pallas_call(
        body, grid=(steps,),
        in_specs=in_specs + [ANY] * (4 * k + 1), out_specs=out_specs,
        out_shape=[jax.ShapeDtypeStruct(o.shape, f32) for o in flat_outs],
        input_output_aliases={4 * k + j: j for j in range(4 * k)},
        name="adamw_layer", compiler_params=_params(("arbitrary",)))(*operands, *flat_outs, after)
    return [res[4 * f:4 * f + 4] for f in range(k)]


SMALL = ("w_conv", "w_pool", "pool_scale", "sgu_ln_g", "w_spatial", "b_spatial", "ln1_g", "ln1_b", "ln2_g", "ln2_b")
WEIGHTS = ("w_in", "w_conv", "w_pool", "pool_scale", "sgu_ln_g", "w_spatial", "b_spatial", "w_o", "ln1_g", "ln1_b",
           "w_gate_up", "w_down", "ln2_g", "ln2_b")
BIG = ("w_in", "w_o", "w_gate_up", "w_down")
GROUPS = (("w_in", "w_o"), ("w_gate_up", "w_down"))
GROUP_AXES = ((0, 0), (1, 0))
SCATTER_HOOKS = 2
ADAMW_STEPS = (2, 4)
SMALL_LAYER_ROWS = 1024


def _pack_layer(arrays):
    flat = jnp.concatenate([a.reshape(-1) for a in arrays])
    return jnp.pad(flat, (0, SMALL_LAYER_ROWS * LANES - flat.shape[0])).reshape(SMALL_LAYER_ROWS, LANES)


def _unpack_layers(flat, shapes):
    out, at = {}, 0
    for name, shape in shapes.items():
        size = 1
        for d in shape:
            size *= d
        out[name] = flat[:, at:at + size].reshape((flat.shape[0],) + tuple(shape))
        at += size
    return out


def kernel(x, w_in, w_conv, w_pool, pool_scale, sgu_ln_g, w_spatial, b_spatial, w_o, ln1_g, ln1_b, w_gate_up, w_down, ln2_g, ln2_b, loss_target, m_w_in, m_w_conv, m_w_pool, m_pool_scale, m_sgu_ln_g, m_w_spatial, m_b_spatial, m_w_o, m_ln1_g, m_ln1_b, m_w_gate_up, m_w_down, m_ln2_g, m_ln2_b, v_w_in, v_w_conv, v_w_pool, v_pool_scale, v_sgu_ln_g, v_w_spatial, v_b_spatial, v_w_o, v_ln1_g, v_ln1_b, v_w_gate_up, v_w_down, v_ln2_g, v_ln2_b):
    weights = dict(w_in=w_in, w_conv=w_conv, w_pool=w_pool, pool_scale=pool_scale, sgu_ln_g=sgu_ln_g, w_spatial=w_spatial,
                   b_spatial=b_spatial, w_o=w_o, ln1_g=ln1_g, ln1_b=ln1_b, w_gate_up=w_gate_up, w_down=w_down, ln2_g=ln2_g, ln2_b=ln2_b)
    m_in = dict(w_in=m_w_in, w_conv=m_w_conv, w_pool=m_w_pool, pool_scale=m_pool_scale, sgu_ln_g=m_sgu_ln_g, w_spatial=m_w_spatial,
                b_spatial=m_b_spatial, w_o=m_w_o, ln1_g=m_ln1_g, ln1_b=m_ln1_b, w_gate_up=m_w_gate_up, w_down=m_w_down,
                ln2_g=m_ln2_g, ln2_b=m_ln2_b)
    v_in = dict(w_in=v_w_in, w_conv=v_w_conv, w_pool=v_w_pool, pool_scale=v_pool_scale, sgu_ln_g=v_sgu_ln_g, w_spatial=v_w_spatial,
                b_spatial=v_b_spatial, w_o=v_w_o, ln1_g=v_ln1_g, ln1_b=v_ln1_b, w_gate_up=v_w_gate_up, w_down=v_w_down,
                ln2_g=v_ln2_g, ln2_b=v_ln2_b)
    depth = w_in.shape[0]
    conv_cols = w_conv.shape[2]
    chip = _chip_index(lax.axis_index("x"), lax.axis_index("y"))

    conv_flat = jnp.pad(w_conv.reshape(-1), (0, 16 * LANES - w_conv.size)).reshape(1, 16, LANES)
    conv_full = _gather_shards([conv_flat])[0].reshape(N_CHIPS, 16 * LANES)[:, :w_conv.size].reshape(N_CHIPS, depth, 3, conv_cols)
    conv_full = conv_full.transpose(1, 2, 0, 3).reshape(depth, 3, N_CHIPS * conv_cols)

    big_w = dict(w_in=jnp.swapaxes(w_in, 1, 2), w_o=w_o, w_gate_up=w_gate_up, w_down=w_down)
    big_m = dict(w_in=jnp.swapaxes(m_w_in, 1, 2), w_o=m_w_o, w_gate_up=m_w_gate_up, w_down=m_w_down)
    big_v = dict(w_in=jnp.swapaxes(v_w_in, 1, 2), w_o=v_w_o, w_gate_up=v_w_gate_up, w_down=v_w_down)

    def place(l, g, after):
        return _place_layer([big_w[n][l] for n in GROUPS[g]], GROUP_AXES[g], after)

    def send(l, g, after):
        return _gather_start(*placed[l, g], GROUP_AXES[g], after)

    stages = [(l, g) for l in range(depth) for g in (0, 1)]
    placed, flights = {}, {}
    token = conv_full
    for st in stages[:2]:
        placed[st] = place(*st, token)
        flights[st], token = send(*st, placed[st][0][0])
    recent = token
    for st in stages[2:]:
        placed[st] = place(*st, token)
        recent = placed[st][0][0]
    act = x[0]
    layers, saved = [], []
    for i, (l, g) in enumerate(stages):
        if g == 0:
            w = dict(w_conv=conv_full[l], w_pool=w_pool[l], pool_scale=pool_scale[l][None], sgu_ln_g=sgu_ln_g[l][None],
                     w_spatial=w_spatial[l], b_spatial=b_spatial[l][:, :, None], ln1_g=ln1_g[l][None], ln1_b=ln1_b[l][None],
                     ln2_g=ln2_g[l][None], ln2_b=ln2_b[l][None])
        _, lands, token = _gather_wait(flights[l, g], GROUP_AXES[g], recent)
        if i + 2 < len(stages):
            flights[stages[i + 2]], token = send(*stages[i + 2], token)
        w.update(zip(GROUPS[g], _gather_finish(lands, GROUP_AXES[g], token)))
        if g == 0:
            sv = _fwd_mix(act, w, token)
            recent = sv["xhat1"]
        else:
            act = recent = _fwd_mlp(sv, w, token)
            layers.append(w)
            saved.append(sv)

    big_outs = {n: [lax.empty(big_w[n].shape, f32) for _ in range(4)] for n in BIG}
    small_sums = [None] * depth
    pending, updates = [], []
    latest = dict(token=None)

    def begin(l, g, parts):
        axes = GROUP_AXES[g] + (0,) * (len(parts) - len(GROUPS[g]))
        lands = [_half_blocks(p, ax) for p, ax in zip(parts, axes)]
        flight, latest["token"] = _split_copy_start("pair", _pair_plan(axes), N_CHIPS * len(parts), parts, lands, latest["token"],
                                                    sibling_only=True)
        pending.append(dict(l=l, g=g, axes=axes, step="pair", age=0, flight=flight))

    def advance(st, recent):
        if st["step"] == "pair":
            parts, got, _ = _split_copy_wait("pair", _pair_plan(st["axes"]), st["flight"], recent)
            sums = _add_pair_layer(parts, got, st["axes"])
            st["flight"], latest["token"] = _scatter_start(sums, latest["token"])
            st["step"] = "scatter"
        elif st["step"] == "scatter":
            sums, slots, _ = _scatter_wait(st["flight"], recent)
            filled = _add_slots(sums, slots)
            st["flight"], latest["token"] = _split_copy_start("join", _join_plan, len(filled), [], filled, latest["token"],
                                                              sibling_only=True)
            st["step"] = "join"
        else:
            _, summed, _ = _split_copy_wait("join", _join_plan, st["flight"], recent)
            updates.append((st["l"], st["g"], summed[:len(GROUPS[st["g"]])]))
            if st["g"] == 0:
                small_sums[st["l"]] = summed[-1]
            st["step"] = "done"
        st["age"] = 0

    def hook(recent):
        for st in reversed(list(pending)):
            st["age"] += 1
            if st["age"] >= SCATTER_HOOKS or st["step"] != "scatter":
                advance(st, recent)
                if st["step"] == "done":
                    pending.remove(st)
        return latest["token"]

    def update(count, recent):
        for l, g, totals in updates[:count]:
            names = GROUPS[g]
            new = _adamw_layer(l, [big_w[n] for n in names], [big_m[n] for n in names], [big_v[n] for n in names], totals,
                               [big_outs[n] for n in names], ADAMW_STEPS[g], latest["token"])
            big_outs.update(zip(names, new))
            recent = new[-1][1]
        del updates[:count]
        return recent

    grad_x, sq = _loss_head(act, loss_target[0])
    latest["token"] = sq
    grads = [None] * depth
    for l in reversed(range(depth)):
        dz, g_mlp = _bwd_mlp(grad_x, layers[l], saved[l], latest["token"], hook)
        hook(g_mlp["w_down"])
        begin(l, 1, [g_mlp[n] for n in GROUPS[1]])
        grad_x, g_mix = _bwd_mix(dz, layers[l], saved[l], latest["token"], hook)
        grads[l] = dict(g_mlp, **g_mix)
        hook(g_mix["w_o"])
        begin(l, 0, [g_mix[n] for n in GROUPS[0]] + [_pack_layer([grads[l][n] for n in SMALL])])
    recent = g_mix["w_o"]
    while pending:
        recent = update(-(-2 * len(updates) // 3), recent)
        hook(recent)
    update(len(updates), recent)
    loss = lax.psum(0.5 / D_MODEL * jnp.sum(sq), ("x", "y", "c"))

    small_sum = _gather_shards([jnp.stack(small_sums)])[0].reshape(depth, SMALL_LAYER_ROWS * LANES)
    grad = {n: [jnp.swapaxes(o, 1, 2) for o in big_outs[n]] if n == "w_in" else big_outs[n] for n in BIG}
    delta = {n: o[1] for n, o in grad.items()}
    new_m = {n: o[2] for n, o in grad.items()}
    new_v = {n: o[3] for n, o in grad.items()}
    grad = {n: o[0] for n, o in grad.items()}
    grad.update(_unpack_layers(small_sum, {n: (3, N_CHIPS * conv_cols) if n == "w_conv" else weights[n].shape[1:] for n in SMALL}))
    grad["w_conv"] = lax.dynamic_slice_in_dim(grad["w_conv"], chip * conv_cols, conv_cols, axis=2)

    results = _adamw_small(*[[src[n] for n in SMALL] for src in (weights, grad, m_in, v_in)])
    for n, (step, moment1, moment2) in zip(SMALL, results):
        delta[n], new_m[n], new_v[n] = step, moment1, moment2

    return (loss, grad_x[None], *[grad[n] for n in WEIGHTS], *[delta[n] for n in WEIGHTS],
            *[new_m[n] for n in WEIGHTS], *[new_v[n] for n in WEIGHTS])
```

```python
import jax
import jax.numpy as jnp
from jax import lax
from jax.experimental import pallas as pl
from jax.experimental.pallas import tpu as pltpu

f32 = jnp.float32
bf16 = jnp.bfloat16

D_MODEL = 1024
DEPTH = 4
CONV_W = 384
POOL_W = 256
SGU_W = 384
IN_W = 3 * CONV_W + POOL_W + 2 * SGU_W
D_FF = 2816
CHUNK = 128
HEAD = 64
POOL_WINDOWS = (2, 4, 8, 16)
ALPHA = float((2 * DEPTH) ** 0.25)
LN_EPS = 1e-5
ADAM_LR = 0.001
ADAM_B1 = 0.9
ADAM_B2 = 0.999
ADAM_EPS = 1e-08
ADAM_WD = 0.01
ADAM_STEP = 10

LANES = 128
TOKEN_TILE = 256
N_CHIPS = 4
VMEM_LIMIT = 56 * 1024 * 1024

BLK_XA, BLK_GB, BLK_GC, BLK_P, BLK_U, BLK_V = 0, 3, 6, 9, 11, 14

MESH = pl.DeviceIdType.MESH


def _params(sem=None):
    return pltpu.CompilerParams(dimension_semantics=sem, vmem_limit_bytes=VMEM_LIMIT)


def _rows(width, tile=TOKEN_TILE):
    return pl.BlockSpec((tile, width), lambda i: (i, 0))


def _resident(shape):
    zeros = (0,) * len(shape)
    return pl.BlockSpec(shape, lambda *_: zeros, pipeline_mode=pl.Buffered(1))


def _nt(a, b):
    return lax.dot_general(a, b, (((1,), (1,)), ((), ())), preferred_element_type=f32)


def _tn(a, b):
    return lax.dot_general(a, b, (((0,), (0,)), ((), ())), preferred_element_type=f32)


def _mm(a, b):
    return jnp.dot(a, b, preferred_element_type=f32)


def _norm_fwd(z):
    mu = jnp.mean(z, axis=-1, keepdims=True)
    zc = z - mu
    var = jnp.mean(zc * zc, axis=-1, keepdims=True)
    rstd = lax.rsqrt(var + LN_EPS)
    return zc * rstd, rstd


def _norm_bwd(dxhat, xhat, rstd):
    m1 = jnp.mean(dxhat, axis=-1, keepdims=True)
    m2 = jnp.mean(dxhat * xhat, axis=-1, keepdims=True)
    return rstd * (dxhat - m1 - xhat * m2)


def _proj(x, w_in_b, after):
    s = x.shape[0]

    def body(x_ref, w_ref, after_ref, p_ref, xb_ref):
        xb = x_ref[...].astype(bf16)
        xb_ref[...] = xb
        p_ref[...] = _nt(xb, w_ref[...])

    return pl.pallas_call(
        body, grid=(s // TOKEN_TILE,),
        in_specs=[_rows(D_MODEL), _resident((IN_W, D_MODEL)), pl.BlockSpec(memory_space=pl.ANY)],
        out_specs=[_rows(IN_W), _rows(D_MODEL)],
        out_shape=[jax.ShapeDtypeStruct((s, IN_W), f32), jax.ShapeDtypeStruct((s, D_MODEL), bf16)],
        name="proj", compiler_params=_params(("arbitrary",)))(x, w_in_b, after)


def _row_ranges(parts):
    out, at = [], 0
    for p in parts:
        out.append((at, at + p.shape[1]))
        at += p.shape[1]
    return out


def _wo_ln1(mix, x, w_o_b, g, b):
    s = x.shape[0]
    n = len(mix)
    ranges = _row_ranges(mix)

    def body(*refs):
        m_refs = refs[:n]
        x_ref, w_ref, g_ref, b_ref, xhat_ref, rstd_ref, hb_ref = refs[n:]
        z = ALPHA * x_ref[...]
        for m_ref, (lo, hi) in zip(m_refs, ranges):
            z = z + _mm(m_ref[...], w_ref[lo:hi, :])
        xhat, rstd = _norm_fwd(z)
        xhat_ref[...] = xhat
        rstd_ref[...] = rstd
        hb_ref[...] = (xhat * g_ref[...] + b_ref[...]).astype(bf16)

    return pl.pallas_call(
        body, grid=(s // TOKEN_TILE,),
        in_specs=[_rows(m.shape[1]) for m in mix] + [_rows(D_MODEL), _resident((D_MODEL, D_MODEL)), _resident((1, D_MODEL)),
                                                     _resident((1, D_MODEL))],
        out_specs=[_rows(D_MODEL), _rows(1), _rows(D_MODEL)],
        out_shape=[jax.ShapeDtypeStruct((s, D_MODEL), f32), jax.ShapeDtypeStruct((s, 1), f32),
                   jax.ShapeDtypeStruct((s, D_MODEL), bf16)],
        name="wo_ln1", compiler_params=_params(("arbitrary",)))(*mix, x, w_o_b, g, b)


def _mlp_fwd(xhat1, g1, b1, w_gu_b, w_down_b, g2, b2, after):
    s = xhat1.shape[0]

    def body(xh_ref, g1_ref, b1_ref, wgu_ref, wd_ref, g2_ref, b2_ref, after_ref, gu_ref, xhat2_ref, rstd2_ref, y_ref):
        h = xh_ref[...] * g1_ref[...] + b1_ref[...]
        gu = _mm(h.astype(bf16), wgu_ref[...])
        gu_ref[...] = gu
        gate = gu[:, :D_FF]
        act = gate * jax.nn.sigmoid(gate) * gu[:, D_FF:]
        z = ALPHA * h + _mm(act.astype(bf16), wd_ref[...])
        xhat2, rstd2 = _norm_fwd(z)
        xhat2_ref[...] = xhat2
        rstd2_ref[...] = rstd2
        y_ref[...] = xhat2 * g2_ref[...] + b2_ref[...]

    vec = _resident((1, D_MODEL))
    return pl.pallas_call(
        body, grid=(s // TOKEN_TILE,),
        in_specs=[_rows(D_MODEL), vec, vec, _resident((D_MODEL, 2 * D_FF)), _resident((D_FF, D_MODEL)), vec, vec,
                  pl.BlockSpec(memory_space=pl.ANY)],
        out_specs=[_rows(2 * D_FF), _rows(D_MODEL), _rows(1), _rows(D_MODEL)],
        out_shape=[jax.ShapeDtypeStruct((s, 2 * D_FF), f32), jax.ShapeDtypeStruct((s, D_MODEL), f32),
                   jax.ShapeDtypeStruct((s, 1), f32), jax.ShapeDtypeStruct((s, D_MODEL), f32)],
        name="mlp_fwd", compiler_params=_params(("arbitrary",)))(xhat1, g1, b1, w_gu_b, w_down_b, g2, b2, after)


def _loss_head(y, target):
    s = y.shape[0]

    def body(y_ref, t_ref, dy_ref, sq_ref):
        @pl.when(pl.program_id(0) == 0)
        def _():
            sq_ref[...] = jnp.zeros_like(sq_ref)

        e = y_ref[...] - t_ref[...]
        dy_ref[...] = e * (1.0 / D_MODEL)
        sq_ref[...] += jnp.sum(e * e, axis=0, keepdims=True)

    return pl.pallas_call(
        body, grid=(s // TOKEN_TILE,),
        in_specs=[_rows(D_MODEL), _rows(D_MODEL)],
        out_specs=[_rows(D_MODEL), pl.BlockSpec((1, D_MODEL), lambda i: (0, 0))],
        out_shape=[jax.ShapeDtypeStruct((s, D_MODEL), f32), jax.ShapeDtypeStruct((1, D_MODEL), f32)],
        name="loss_head", compiler_params=_params(("arbitrary",)))(y, target)


def _mlp_bwd(dy, xhat2, rstd2, g2, gu, w_gu_b, w_down_b, xhat1, rstd1, g1, w_o_b, after):
    s = dy.shape[0]

    def body(dy_ref, xh_ref, rs_ref, g2_ref, gu_ref, wgu_ref, wd_ref, xh1_ref, rs1_ref, g1_ref, wo_ref, after_ref,
             dz_ref, act_ref, dgu_ref, dz1_ref, dz1b_ref, dm_ref, gg_ref, gb_ref, gg1_ref, gb1_ref):
        @pl.when(pl.program_id(0) == 0)
        def _():
            for ref in (gg_ref, gb_ref, gg1_ref, gb1_ref):
                ref[...] = jnp.zeros_like(ref)

        dy_t = dy_ref[...]
        xhat = xh_ref[...]
        gg_ref[...] += jnp.sum(dy_t * xhat, axis=0, keepdims=True)
        gb_ref[...] += jnp.sum(dy_t, axis=0, keepdims=True)
        dz = _norm_bwd(dy_t * g2_ref[...], xhat, rs_ref[...])
        dzb = dz.astype(bf16)
        dz_ref[...] = dzb
        dact = _nt(dzb, wd_ref[...])
        gate = gu_ref[:, :D_FF]
        up = gu_ref[:, D_FF:]
        sg = jax.nn.sigmoid(gate)
        silu = gate * sg
        act_ref[...] = (silu * up).astype(bf16)
        dgu_ref[:, :D_FF] = (dact * up * (sg * (1.0 + gate * (1.0 - sg)))).astype(bf16)
        dgu_ref[:, D_FF:] = (dact * silu).astype(bf16)
        dh = ALPHA * dz + _nt(dgu_ref[...], wgu_ref[...])
        xhat1 = xh1_ref[...]
        gg1_ref[...] += jnp.sum(dh * xhat1, axis=0, keepdims=True)
        gb1_ref[...] += jnp.sum(dh, axis=0, keepdims=True)
        dz1 = _norm_bwd(dh * g1_ref[...], xhat1, rs1_ref[...])
        dz1_ref[...] = dz1
        dz1b = dz1.astype(bf16)
        dz1b_ref[...] = dz1b
        dm_ref[...] = _nt(dz1b, wo_ref[...])

    vec, vec_out = _resident((1, D_MODEL)), pl.BlockSpec((1, D_MODEL), lambda i: (0, 0))
    tokens_f32, tokens_bf16 = jax.ShapeDtypeStruct((s, D_MODEL), f32), jax.ShapeDtypeStruct((s, D_MODEL), bf16)
    sums = jax.ShapeDtypeStruct((1, D_MODEL), f32)
    return pl.pallas_call(
        body, grid=(s // TOKEN_TILE,),
        in_specs=[_rows(D_MODEL), _rows(D_MODEL), _rows(1), vec, _rows(2 * D_FF),
                  _resident((D_MODEL, 2 * D_FF)), _resident((D_FF, D_MODEL)), _rows(D_MODEL), _rows(1), vec,
                  _resident((D_MODEL, D_MODEL)), pl.BlockSpec(memory_space=pl.ANY)],
        out_specs=[_rows(D_MODEL), _rows(D_FF), _rows(2 * D_FF), _rows(D_MODEL), _rows(D_MODEL), _rows(D_MODEL),
                   vec_out, vec_out, vec_out, vec_out],
        out_shape=[tokens_bf16, jax.ShapeDtypeStruct((s, D_FF), bf16), jax.ShapeDtypeStruct((s, 2 * D_FF), bf16),
                   tokens_f32, tokens_bf16, tokens_f32, sums, sums, sums, sums],
        name="mlp_bwd", compiler_params=_params(("arbitrary",)))(
            dy, xhat2, rstd2, g2, gu, w_gu_b, w_down_b, xhat1, rstd1, g1, w_o_b, after)


def _dx(dz1, dparts, w_in_t, after):
    s = dz1.shape[0]
    n = len(dparts)
    ranges = _row_ranges(dparts)

    def body(*refs):
        d_refs = refs[:n]
        dz_ref, w_ref, _, dx_ref = refs[n:]
        acc = ALPHA * dz_ref[...]
        for d_ref, (lo, hi) in zip(d_refs, ranges):
            acc = acc + _mm(d_ref[...], w_ref[lo:hi, :])
        dx_ref[...] = acc

    return pl.pallas_call(
        body, grid=(s // TOKEN_TILE,),
        in_specs=[_rows(d.shape[1]) for d in dparts] + [_rows(D_MODEL), _resident((IN_W, D_MODEL)),
                                                        pl.BlockSpec(memory_space=pl.ANY)],
        out_specs=_rows(D_MODEL),
        out_shape=jax.ShapeDtypeStruct((s, D_MODEL), f32),
        name="dx", compiler_params=_params(("arbitrary",)))(*dparts, dz1, w_in_t, after)


def _weight_grad_rows(parts, b, bn):
    s, n_cols = b.shape
    n = len(parts)
    ranges = _row_ranges(parts)
    m = ranges[-1][1]

    def body(*refs):
        p_refs = refs[:n]
        b_ref, o_ref = refs[n:]
        for p_ref, (lo, hi) in zip(p_refs, ranges):
            o_ref[lo:hi, :] = _tn(p_ref[...], b_ref[...]).astype(bf16)

    return pl.pallas_call(
        body, grid=(n_cols // bn,),
        in_specs=[_resident(p.shape) for p in parts] + [pl.BlockSpec((s, bn), lambda j: (0, j))],
        out_specs=pl.BlockSpec((m, bn), lambda j: (0, j)),
        out_shape=jax.ShapeDtypeStruct((m, n_cols), bf16),
        name="weight_grad_rows", compiler_params=_params(("arbitrary",)))(*parts, b)


def _weight_grad(a, b, bm, bn, after):
    s, m = a.shape
    n = b.shape[1]

    def body(a_ref, b_ref, after_ref, o_ref):
        o_ref[...] = _tn(a_ref[...], b_ref[...]).astype(bf16)

    return pl.pallas_call(
        body, grid=(m // bm, n // bn),
        in_specs=[pl.BlockSpec((s, bm), lambda i, j: (0, i)), pl.BlockSpec((s, bn), lambda i, j: (0, j)),
                  pl.BlockSpec(memory_space=pl.ANY)],
        out_specs=pl.BlockSpec((bm, bn), lambda i, j: (i, j)),
        out_shape=jax.ShapeDtypeStruct((m, n), bf16),
        name="weight_grad", compiler_params=_params(("arbitrary", "arbitrary")))(a, b, after)


def _shift_down(a, k):
    row = lax.broadcasted_iota(jnp.int32, a.shape, 0)
    return jnp.where(row >= k, pltpu.roll(a, k, 0), 0.0)


def _shift_up(a, k):
    n = a.shape[0]
    row = lax.broadcasted_iota(jnp.int32, a.shape, 0)
    return jnp.where(row < n - k, pltpu.roll(a, n - k, 0), 0.0)


def _slab(s, block):
    return pl.BlockSpec((s, LANES), lambda k: (0, block + k))


def _conv_y(z, w):
    return w[0:1, :] * _shift_down(z, 2) + w[1:2, :] * _shift_down(z, 1) + w[2:3, :] * z


def _conv_fwd(proj, w_conv):
    s = proj.shape[0]

    def body(xa_ref, gb_ref, gc_ref, w_ref, o_ref):
        z = gc_ref[...] * xa_ref[...]
        o_ref[...] = (gb_ref[...] * _conv_y(z, w_ref[...])).astype(bf16)

    return pl.pallas_call(
        body, grid=(CONV_W // LANES,),
        in_specs=[_slab(s, BLK_XA), _slab(s, BLK_GB), _slab(s, BLK_GC), pl.BlockSpec((3, LANES), lambda k: (0, k))],
        out_specs=_slab(s, 0),
        out_shape=jax.ShapeDtypeStruct((s, CONV_W), bf16),
        name="conv_fwd", compiler_params=_params(("arbitrary",)))(proj, proj, proj, w_conv)


def _conv_bwd(proj, dmix, w_conv, after):
    s = proj.shape[0]

    def body(xa_ref, gb_ref, gc_ref, dy_ref, w_ref, after_ref, dxa_ref, dgb_ref, dgc_ref, dw_ref):
        xa = xa_ref[...]
        gc = gc_ref[...]
        w = w_ref[...]
        z = gc * xa
        dya = dy_ref[...]
        dgb_ref[...] = (dya * _conv_y(z, w)).astype(bf16)
        dy = dya * gb_ref[...]
        dz = w[2:3, :] * dy + w[1:2, :] * _shift_up(dy, 1) + w[0:1, :] * _shift_up(dy, 2)
        dxa_ref[...] = (dz * gc).astype(bf16)
        dgc_ref[...] = (dz * xa).astype(bf16)
        dw_ref[0:1, :] = jnp.sum(dy * _shift_down(z, 2), axis=0, keepdims=True)
        dw_ref[1:2, :] = jnp.sum(dy * _shift_down(z, 1), axis=0, keepdims=True)
        dw_ref[2:3, :] = jnp.sum(dy * z, axis=0, keepdims=True)

    out = jax.ShapeDtypeStruct((s, CONV_W), bf16)
    return pl.pallas_call(
        body, grid=(CONV_W // LANES,),
        in_specs=[_slab(s, BLK_XA), _slab(s, BLK_GB), _slab(s, BLK_GC), _slab(s, 0), pl.BlockSpec((3, LANES), lambda k: (0, k)),
                  pl.BlockSpec(memory_space=pl.ANY)],
        out_specs=[_slab(s, 0), _slab(s, 0), _slab(s, 0), pl.BlockSpec((3, LANES), lambda k: (0, k))],
        out_shape=[out, out, out, jax.ShapeDtypeStruct((3, CONV_W), f32)],
        name="conv_bwd", compiler_params=_params(("arbitrary",)))(proj, proj, proj, dmix, w_conv, after)


def _pool_window(k):
    lane = lax.broadcasted_iota(jnp.int32, (1, LANES), 1)
    low = lane < HEAD
    first = k == 0
    wlen = jnp.where(low, jnp.where(first, POOL_WINDOWS[0], POOL_WINDOWS[2]), jnp.where(first, POOL_WINDOWS[1], POOL_WINDOWS[3]))
    return wlen, low, first


def _pool_diff(p, k):
    wlen, low, first = _pool_window(k)
    s2 = p + _shift_down(p, 1)
    s4 = s2 + _shift_down(s2, 2)
    s8 = s4 + _shift_down(s4, 4)
    s16 = s8 + _shift_down(s8, 8)
    win = jnp.where(low, jnp.where(first, s2, s8), jnp.where(first, s4, s16))
    row = lax.broadcasted_iota(jnp.int32, p.shape, 0)
    count = jnp.minimum(row + 1, wlen).astype(f32)
    return win / count - p, count


def _pool_weight(w_ref):
    zero = jnp.zeros((HEAD, HEAD), f32)
    top = jnp.concatenate([w_ref[0], zero], axis=1)
    bottom = jnp.concatenate([zero, w_ref[1]], axis=1)
    return jnp.concatenate([top, bottom], axis=0).astype(bf16)


def _pool_fwd(proj, w_pool, pool_scale):
    s = proj.shape[0]

    def body(p_ref, w_ref, sc_ref, o_ref):
        d, _ = _pool_diff(p_ref[...], pl.program_id(0))
        o_ref[...] = (_mm(d.astype(bf16), _pool_weight(w_ref)) * sc_ref[...]).astype(bf16)

    return pl.pallas_call(
        body, grid=(POOL_W // LANES,),
        in_specs=[_slab(s, BLK_P), pl.BlockSpec((2, HEAD, HEAD), lambda k: (k, 0, 0)), pl.BlockSpec((1, LANES), lambda k: (0, k))],
        out_specs=_slab(s, 0),
        out_shape=jax.ShapeDtypeStruct((s, POOL_W), bf16),
        name="pool_fwd", compiler_params=_params(("arbitrary",)))(proj, w_pool, pool_scale)


def _pool_bwd(proj, dmix, w_pool, pool_scale):
    s = proj.shape[0]

    def body(p_ref, dy_ref, w_ref, sc_ref, dp_ref, dw_ref, dsc_ref):
        k = pl.program_id(0)
        d, count = _pool_diff(p_ref[...], k)
        wbd = _pool_weight(w_ref)
        db = d.astype(bf16)
        dyb = dy_ref[...]
        dsc_ref[...] = jnp.sum(dyb * _mm(db, wbd), axis=0, keepdims=True)
        dpre = (dyb * sc_ref[...]).astype(bf16)
        dwbd = _tn(db, dpre)
        dw_ref[0] = dwbd[:HEAD, :HEAD]
        dw_ref[1] = dwbd[HEAD:, HEAD:]
        dd = _nt(dpre, wbd)
        e = dd / count
        wlen, low, first = _pool_window(k)
        a2 = e + _shift_up(e, 1)
        a4 = a2 + _shift_up(a2, 2)
        a8 = a4 + _shift_up(a4, 4)
        a16 = a8 + _shift_up(a8, 8)
        back = jnp.where(low, jnp.where(first, a2, a8), jnp.where(first, a4, a16))
        dp_ref[...] = (back - dd).astype(bf16)

    return pl.pallas_call(
        body, grid=(POOL_W // LANES,),
        in_specs=[_slab(s, BLK_P), _slab(s, CONV_W // LANES), pl.BlockSpec((2, HEAD, HEAD), lambda k: (k, 0, 0)),
                  pl.BlockSpec((1, LANES), lambda k: (0, k))],
        out_specs=[_slab(s, 0), pl.BlockSpec((2, HEAD, HEAD), lambda k: (k, 0, 0)), pl.BlockSpec((1, LANES), lambda k: (0, k))],
        out_shape=[jax.ShapeDtypeStruct((s, POOL_W), bf16), jax.ShapeDtypeStruct((4, HEAD, HEAD), f32),
                   jax.ShapeDtypeStruct((1, POOL_W), f32)],
        name="pool_bwd", compiler_params=_params(("arbitrary",)))(proj, dmix, w_pool, pool_scale)


SGU_UNROLL = 4
INV_SQRT2 = 0.7071067811865476
INV_SQRT_2PI = 0.3989422804014327


def _gelu(x):
    return 0.5 * x * (1.0 + lax.erf(x * INV_SQRT2))


def _gelu_grad(x):
    return 0.5 * (1.0 + lax.erf(x * INV_SQRT2)) + x * (INV_SQRT_2PI * jnp.exp(-0.5 * x * x))


def _head_mean(a, low):
    s_low = jnp.sum(jnp.where(low, a, 0.0), axis=-1, keepdims=True)
    s_high = jnp.sum(jnp.where(low, 0.0, a), axis=-1, keepdims=True)
    return jnp.where(low, s_low, s_high) * (1.0 / HEAD)


def _tril():
    r = lax.broadcasted_iota(jnp.int32, (CHUNK, CHUNK), 0)
    c = lax.broadcasted_iota(jnp.int32, (CHUNK, CHUNK), 1)
    return r >= c


def _sgu_chunk(up, vp, g, wm0, wm1, b0, b1, low):
    ug = _gelu(up)
    vg = _gelu(vp)
    vc = vg - _head_mean(vg, low)
    rstd = lax.rsqrt(_head_mean(vc * vc, low) + LN_EPS)
    vn = vc * rstd
    vb = (vn * g).astype(bf16)
    mixed = jnp.where(low, _mm(wm0, vb) + b0, _mm(wm1, vb) + b1)
    return ug, vn, rstd, vb, mixed


def _sgu_specs(s):
    return [_slab(s, BLK_U), _slab(s, BLK_V), pl.BlockSpec((1, LANES), lambda k: (0, k)),
            pl.BlockSpec((2, CHUNK, CHUNK), lambda k: (k, 0, 0)), pl.BlockSpec((2, CHUNK, 1), lambda k: (k, 0, 0))]


def _sgu_fwd(proj, sgu_g, w_spatial, b_spatial3):
    s = proj.shape[0]

    def body(u_ref, v_ref, g_ref, w_ref, b_ref, o_ref):
        low = lax.broadcasted_iota(jnp.int32, (1, LANES), 1) < HEAD
        mask = _tril()
        wm0 = jnp.where(mask, w_ref[0], 0.0).astype(bf16)
        wm1 = jnp.where(mask, w_ref[1], 0.0).astype(bf16)
        g = g_ref[...]
        b0 = b_ref[0]
        b1 = b_ref[1]

        def chunk(n, carry):
            rows = pl.ds(pl.multiple_of(n * CHUNK, CHUNK), CHUNK)
            ug, _, _, _, mixed = _sgu_chunk(u_ref[rows, :], v_ref[rows, :], g, wm0, wm1, b0, b1, low)
            o_ref[rows, :] = (ug * mixed).astype(bf16)
            return carry

        lax.fori_loop(0, s // CHUNK, chunk, 0, unroll=SGU_UNROLL)

    return pl.pallas_call(
        body, grid=(SGU_W // LANES,),
        in_specs=_sgu_specs(s),
        out_specs=_slab(s, 0),
        out_shape=jax.ShapeDtypeStruct((s, SGU_W), bf16),
        name="sgu_fwd", compiler_params=_params(("arbitrary",)))(proj, proj, sgu_g, w_spatial, b_spatial3)


def _sgu_bwd(proj, dmix, sgu_g, w_spatial, b_spatial3):
    s = proj.shape[0]

    def body(u_ref, v_ref, g_ref, w_ref, b_ref, dy_ref, du_ref, dv_ref, dg_ref, dw_ref, db_ref):
        low = lax.broadcasted_iota(jnp.int32, (1, LANES), 1) < HEAD
        mask = _tril()
        w0 = jnp.where(mask, w_ref[0], 0.0)
        w1 = jnp.where(mask, w_ref[1], 0.0)
        wm0 = w0.astype(bf16)
        wm1 = w1.astype(bf16)
        wt0 = w0.T.astype(bf16)
        wt1 = w1.T.astype(bf16)
        g = g_ref[...]
        b0 = b_ref[0]
        b1 = b_ref[1]
        dg_ref[...] = jnp.zeros_like(dg_ref)
        dw_ref[...] = jnp.zeros_like(dw_ref)
        db_ref[...] = jnp.zeros_like(db_ref)

        def chunk(n, carry):
            rows = pl.ds(pl.multiple_of(n * CHUNK, CHUNK), CHUNK)
            up = u_ref[rows, :]
            vp = v_ref[rows, :]
            ug, vn, rstd, vb, mixed = _sgu_chunk(up, vp, g, wm0, wm1, b0, b1, low)
            dy = dy_ref[rows, :]
            du_ref[rows, :] = (dy * mixed * _gelu_grad(up)).astype(bf16)
            dmix_c = dy * ug
            db_ref[0] += jnp.sum(jnp.where(low, dmix_c, 0.0), axis=-1, keepdims=True)
            db_ref[1] += jnp.sum(jnp.where(low, 0.0, dmix_c), axis=-1, keepdims=True)
            dmb = dmix_c.astype(bf16)
            zero = jnp.zeros_like(dmb)
            dw_ref[0] += _nt(jnp.where(low, dmb, zero), vb)
            dw_ref[1] += _nt(jnp.where(low, zero, dmb), vb)
            dvnorm = jnp.where(low, _mm(wt0, dmb), _mm(wt1, dmb))
            dg_ref[...] += jnp.sum(dvnorm * vn, axis=0, keepdims=True)
            dvn = dvnorm * g
            dvg = rstd * (dvn - _head_mean(dvn, low) - vn * _head_mean(dvn * vn, low))
            dv_ref[rows, :] = (dvg * _gelu_grad(vp)).astype(bf16)
            return carry

        lax.fori_loop(0, s // CHUNK, chunk, 0, unroll=SGU_UNROLL)
        dw_ref[0] = jnp.where(mask, dw_ref[0], 0.0)
        dw_ref[1] = jnp.where(mask, dw_ref[1], 0.0)

    out = jax.ShapeDtypeStruct((s, SGU_W), bf16)
    return pl.pallas_call(
        body, grid=(SGU_W // LANES,),
        in_specs=_sgu_specs(s) + [_slab(s, (CONV_W + POOL_W) // LANES)],
        out_specs=[_slab(s, 0), _slab(s, 0), pl.BlockSpec((1, LANES), lambda k: (0, k)),
                   pl.BlockSpec((2, CHUNK, CHUNK), lambda k: (k, 0, 0)), pl.BlockSpec((2, CHUNK, 1), lambda k: (k, 0, 0))],
        out_shape=[out, out, jax.ShapeDtypeStruct((1, SGU_W), f32), jax.ShapeDtypeStruct((6, CHUNK, CHUNK), f32),
                   jax.ShapeDtypeStruct((6, CHUNK, 1), f32)],
        name="sgu_bwd", compiler_params=_params(("arbitrary",)))(proj, proj, sgu_g, w_spatial, b_spatial3, dmix)


def _fwd_mix(x, w, after):
    proj, xb = _proj(x, w["w_in"], after)
    mix = [_conv_fwd(proj, w["w_conv"]), _pool_fwd(proj, w["w_pool"], w["pool_scale"]),
           _sgu_fwd(proj, w["sgu_ln_g"], w["w_spatial"], w["b_spatial"])]
    xhat1, rstd1, hb = _wo_ln1(mix, x, w["w_o"], w["ln1_g"], w["ln1_b"])
    return dict(proj=proj, xb=xb, mix=mix, xhat1=xhat1, rstd1=rstd1, hb=hb)


def _fwd_mlp(sv, w, after):
    gu, xhat2, rstd2, y = _mlp_fwd(sv["xhat1"], w["ln1_g"], w["ln1_b"], w["w_gate_up"], w["w_down"], w["ln2_g"], w["ln2_b"], after)
    sv.update(gu=gu, xhat2=xhat2, rstd2=rstd2)
    return y


def _bwd_mlp(dy, w, sv, after, hook):
    dz2b, actb, dgub, dz1, dz1b, dmix, g_ln2_g, g_ln2_b, g_ln1_g, g_ln1_b = _mlp_bwd(
        dy, sv["xhat2"], sv["rstd2"], w["ln2_g"], sv["gu"], w["w_gate_up"], w["w_down"], sv["xhat1"], sv["rstd1"], w["ln1_g"],
        w["w_o"], after)
    after = hook(dz1)
    grads = dict(w_gate_up=_weight_grad(sv["hb"], dgub, D_MODEL, D_FF // 2, after),
                 w_down=_weight_grad(actb, dz2b, D_FF // 2, D_MODEL, after),
                 ln2_g=g_ln2_g, ln2_b=g_ln2_b, ln1_g=g_ln1_g, ln1_b=g_ln1_b)
    return (dz1, dz1b, dmix), grads


def _bwd_mix(dz, w, sv, after, hook):
    dz1, dz1b, dmix = dz
    dxa, dgb, dgc, g_conv = _conv_bwd(sv["proj"], dmix, w["w_conv"], after)
    dp, g_pool, g_pscale = _pool_bwd(sv["proj"], dmix, w["w_pool"], w["pool_scale"])
    du, dv, g_sgu_g, g_spatial, g_bsp = _sgu_bwd(sv["proj"], dmix, w["sgu_ln_g"], w["w_spatial"], w["b_spatial"])
    dparts = [dxa, dgb, dgc, dp, du, dv]
    dx = _dx(dz1, dparts, w["w_in"], hook(du))
    grads = dict(
        w_in=_weight_grad_rows(dparts, sv["xb"], 512), w_o=_weight_grad_rows(sv["mix"], dz1b, D_MODEL),
        w_conv=g_conv, w_pool=g_pool, pool_scale=g_pscale, sgu_ln_g=g_sgu_g, w_spatial=g_spatial,
        b_spatial=g_bsp.reshape(6, CHUNK))
    return dx, grads


def _local_step(x, target, layers):
    saved = []
    for w in layers:
        sv = _fwd_mix(x, w, x)
        x = _fwd_mlp(sv, w, x)
        saved.append(sv)
    dy, sq = _loss_head(x, target)
    grads = [None] * len(layers)
    for l in reversed(range(len(layers))):
        dz, g_mlp = _bwd_mlp(dy, layers[l], saved[l], sq, lambda a: a)
        dy, g_mix = _bwd_mix(dz, layers[l], saved[l], dz[0], lambda a: a)
        grads[l] = dict(g_mlp, **g_mix)
    return sq, dy, grads


ANY = pl.BlockSpec(memory_space=pl.ANY)


def _place():
    x, y, c = lax.axis_index("x"), lax.axis_index("y"), lax.axis_index("c")
    others = [(1 - x, y), (x, 1 - y), (1 - x, 1 - y)]
    return x, y, c, others


def _chip_index(cx, cy):
    return 2 * cx + cy


def _half(ref_rows, c):
    half = ref_rows // 2
    return pl.ds(pl.multiple_of(c * half, 8), half)


def _remote(src, dst, send_sem, recv_sem, device):
    return pltpu.make_async_remote_copy(src_ref=src, dst_ref=dst, send_sem=send_sem, recv_sem=recv_sem,
                                        device_id=device, device_id_type=MESH)


def _gather_shards(shards, after):
    n = len(shards)
    base, total = [], 0
    for s in shards:
        base.append(total)
        total += 6 * s.shape[0]

    def body(*refs):
        ins, outs = refs[:n], refs[n + 1:2 * n + 1]
        send, recv = refs[2 * n + 1:]
        x, y, c, others = _place()
        me = _chip_index(x, y)
        sib = (x, y, 1 - c)
        sends = []
        for f in range(n):
            depth, rows = ins[f].shape[0], ins[f].shape[1]
            for l in range(depth):
                for k, (cx, cy) in enumerate(others):
                    sem = base[f] + 6 * l + k
                    cp = _remote(ins[f].at[l, _half(rows, c)], outs[f].at[l, me, _half(rows, c)],
                                 send.at[sem], recv.at[sem], (cx, cy, c))
                    cp.start()
                    sends.append(cp)
        for f in range(n):
            depth, rows = ins[f].shape[0], ins[f].shape[1]
            for l in range(depth):
                for k, (cx, cy) in enumerate(others):
                    sem = base[f] + 6 * l + k
                    landed = outs[f].at[l, _chip_index(cx, cy), _half(rows, c)]
                    _remote(landed, landed, send.at[sem], recv.at[sem], (cx, cy, c)).wait_recv()
                    cp = _remote(landed, landed, send.at[sem + 3], recv.at[sem + 3], sib)
                    cp.start()
                    sends.append(cp)
        for f in range(n):
            depth, rows = ins[f].shape[0], ins[f].shape[1]
            for l in range(depth):
                for k, (cx, cy) in enumerate(others):
                    sem = base[f] + 6 * l + k + 3
                    passed = outs[f].at[l, _chip_index(cx, cy), _half(rows, 1 - c)]
                    _remote(passed, passed, send.at[sem], recv.at[sem], sib).wait_recv()
        for cp in sends:
            cp.wait_send()

    gathered = pl.pallas_call(
        body, in_specs=[ANY] * (n + 1), out_specs=[ANY] * n,
        out_shape=[jax.ShapeDtypeStruct((s.shape[0], N_CHIPS) + s.shape[1:], s.dtype) for s in shards],
        scratch_shapes=[pltpu.SemaphoreType.DMA((total,)), pltpu.SemaphoreType.DMA((total,))],
        name="gather_shards")(*shards, after)
    return [_place_own(g, s) for g, s in zip(gathered, shards)]


def _scalar(value):
    return jnp.reshape(value, (1,)).astype(jnp.int32)


def _place_own(blocks, shard):
    depth, rows, cols = shard.shape

    def body(me_ref, b_ref, s_ref, o_ref):
        o_ref[...] = s_ref[...]

    return pl.pallas_call(
        body,
        grid_spec=pltpu.PrefetchScalarGridSpec(
            num_scalar_prefetch=1, grid=(depth,),
            in_specs=[ANY, pl.BlockSpec((None, rows, cols), lambda l, me: (l, 0, 0))],
            out_specs=pl.BlockSpec((None, None, rows, cols), lambda l, me: (l, me[0], 0, 0))),
        out_shape=jax.ShapeDtypeStruct(blocks.shape, blocks.dtype),
        input_output_aliases={1: 0},
        name="place_own", compiler_params=_params(("arbitrary",)))(
            _scalar(_chip_index(lax.axis_index("x"), lax.axis_index("y"))), blocks, shard)


HBM = pl.BlockSpec(memory_space=pltpu.HBM)
SEM = pl.BlockSpec(memory_space=pltpu.SEMAPHORE)
TOKEN = jax.ShapeDtypeStruct((8, LANES), f32)
SPLIT_COPY = pltpu.CompilerParams(has_side_effects=pltpu.SideEffectType.DATAFLOW_SIDE_EFFECTING)


def _in_hbm(a):
    return pltpu.with_memory_space_constraint(a, pltpu.HBM)


def _full_shape(shard, axis):
    rows, cols = shard.shape
    return (N_CHIPS * rows, cols) if axis == 0 else (rows, N_CHIPS * cols)


def _block_half(ref, axis, j, h):
    if axis == 0:
        rows = ref.shape[0] // N_CHIPS
        return ref.at[pl.ds(pl.multiple_of(j * rows + h * (rows // 2), 16), rows // 2), :]
    half, cols = ref.shape[0] // 2, ref.shape[1] // N_CHIPS
    return ref.at[pl.ds(pl.multiple_of(h * half, 16), half), pl.ds(pl.multiple_of(j * cols, LANES), cols)]


def _place_layer(shards, axes, after):
    n = len(shards)

    def body(me_ref, *refs):
        ins, outs = refs[n:2 * n], refs[2 * n + 1:]
        for f in range(n):
            block = ins[f][...].astype(bf16)
            outs[f][...] = block
            outs[n + f][...] = block

    lands = [lax.empty(_full_shape(s, ax), bf16) for s, ax in zip(shards, axes)]
    own = [pl.BlockSpec(s.shape, lambda i, me: (0, 0)) for s in shards]
    outs = pl.pallas_call(
        body,
        grid_spec=pltpu.PrefetchScalarGridSpec(
            num_scalar_prefetch=1, grid=(1,),
            in_specs=[ANY] * n + own + [ANY],
            out_specs=own + [pl.BlockSpec(s.shape, (lambda i, me: (me[0], 0)) if ax == 0 else (lambda i, me: (0, me[0])))
                             for s, ax in zip(shards, axes)]),
        out_shape=[jax.ShapeDtypeStruct(s.shape, bf16) for s in shards] + [jax.ShapeDtypeStruct(a.shape, bf16) for a in lands],
        input_output_aliases={1 + f: n + f for f in range(n)},
        name="place_layer", compiler_params=_params(("arbitrary",)))(
            _scalar(_chip_index(lax.axis_index("x"), lax.axis_index("y"))), *lands, *shards, after)
    return outs[:n], outs[n:]


def _gather_start(shards, lands, axes, after):
    return _split_copy_start("gather", _gather_plan(axes), 3 * len(shards), shards, lands, after)


def _gather_wait(state, axes, after):
    return _split_copy_wait("gather", _gather_plan(axes), state, after)


SIBLING_PAIR_ID = 0


def _split_copy_start(name, plan, count, ins, lands, after, sibling_only=False):
    arrays = list(ins) + list(lands)
    n_in, n = len(ins), len(arrays)

    def body(*refs):
        send, recv, token = refs[n + 1], refs[n + 2], refs[-1]
        if sibling_only:
            x, y, c, _ = _place()
            barrier = pltpu.get_barrier_semaphore()
            pl.semaphore_signal(barrier, inc=1, device_id=(x, y, 1 - c), device_id_type=MESH)
            pl.semaphore_wait(barrier, 1)
        for i, (src, dst, _, peer) in enumerate(plan(refs[:n_in], refs[n_in:n])):
            _remote(src, dst, send.at[i], recv.at[i], peer).start()
        token[...] = jnp.zeros_like(token)

    effect = pltpu.SideEffectType.DATAFLOW_SIDE_EFFECTING
    outs = pl.pallas_call(
        body, name=name + "_start",
        in_specs=[HBM] * n + [ANY],
        out_specs=(SEM, SEM, *[HBM] * n, pl.BlockSpec(memory_space=pltpu.VMEM)),
        out_shape=(pltpu.SemaphoreType.DMA((count,)), pltpu.SemaphoreType.DMA((count,)),
                   *[pltpu.HBM(a.shape, a.dtype) for a in arrays], TOKEN),
        input_output_aliases={i: 2 + i for i in range(n)},
        compiler_params=pltpu.CompilerParams(has_side_effects=effect, collective_id=SIBLING_PAIR_ID) if sibling_only
        else SPLIT_COPY)(*[_in_hbm(a) for a in arrays], after)
    return (outs[0], outs[1], outs[2:2 + n_in], outs[2 + n_in:2 + n]), outs[-1]


def _split_copy_wait(name, plan, state, after):
    send_sems, recv_sems, ins, lands = state
    arrays = list(ins) + list(lands)
    n_in, n = len(ins), len(arrays)

    def body(*refs):
        send, recv, token = refs[n], refs[n + 1], refs[-1]
        for i, (src, _, landing, peer) in enumerate(plan(refs[:n_in], refs[n_in:n])):
            cp = _remote(src, landing, send.at[i], recv.at[i], peer)
            cp.wait_send()
            cp.wait_recv()
        token[...] = jnp.zeros_like(token)

    outs = pl.pallas_call(
        body, name=name + "_wait",
        in_specs=[HBM] * n + [SEM, SEM, ANY],
        out_specs=(*[HBM] * n, pl.BlockSpec(memory_space=pltpu.VMEM)),
        out_shape=(*[pltpu.HBM(a.shape, a.dtype) for a in arrays], TOKEN),
        input_output_aliases={i: i for i in range(n)},
        compiler_params=SPLIT_COPY)(*arrays, send_sems, recv_sems, after)
    return outs[:n_in], outs[n_in:n], outs[-1]


def _gather_plan(axes):
    def plan(ins, lnd):
        x, y, c, others = _place()
        me = _chip_index(x, y)
        return [(ins[f].at[_half(ins[f].shape[0], c)], _block_half(lnd[f], ax, me, c),
                 _block_half(lnd[f], ax, _chip_index(cx, cy), c), (cx, cy, c))
                for f, ax in enumerate(axes) for cx, cy in others]
    return plan


def _pair_plan(axes):
    def plan(ins, lnd):
        x, y, c, _ = _place()
        return [(_block_half(ins[f], ax, j, 1 - c), lnd[f].at[j], lnd[f].at[j], (x, y, 1 - c))
                for f, ax in enumerate(axes) for j in range(N_CHIPS)]
    return plan


def _scatter_plan(ins, lnd):
    x, y, c, others = _place()
    return [(ins[f].at[_chip_index(cx, cy)], lnd[f].at[k], lnd[f].at[k], (cx, cy, c))
            for f in range(len(ins)) for k, (cx, cy) in enumerate(others)]


def _join_plan(ins, lnd):
    x, y, c, _ = _place()
    return [(lnd[f].at[_half(lnd[f].shape[0], c)], lnd[f].at[_half(lnd[f].shape[0], c)],
             lnd[f].at[_half(lnd[f].shape[0], 1 - c)], (x, y, 1 - c)) for f in range(len(lnd))]


def _gather_finish(lands, axes, after):
    n = len(lands)

    def body(*refs):
        outs = refs[n + 1:2 * n + 1]
        send, recv = refs[2 * n + 1:]
        x, y, c, others = _place()
        sib = (x, y, 1 - c)
        barrier = pltpu.get_barrier_semaphore()
        pl.semaphore_signal(barrier, inc=1, device_id=sib, device_id_type=MESH)
        pl.semaphore_wait(barrier, 1)
        sends = []
        for f in range(n):
            for k, (cx, cy) in enumerate(others):
                landed = _block_half(outs[f], axes[f], _chip_index(cx, cy), c)
                cp = _remote(landed, landed, send.at[3 * f + k], recv.at[3 * f + k], sib)
                cp.start()
                sends.append(cp)
        for f in range(n):
            for k, (cx, cy) in enumerate(others):
                passed = _block_half(outs[f], axes[f], _chip_index(cx, cy), 1 - c)
                _remote(passed, passed, send.at[3 * f + k], recv.at[3 * f + k], sib).wait_recv()
        for cp in sends:
            cp.wait_send()

    return pl.pallas_call(
        body, in_specs=[ANY] * (n + 1), out_specs=[ANY] * n,
        out_shape=[jax.ShapeDtypeStruct(a.shape, a.dtype) for a in lands],
        input_output_aliases={f: f for f in range(n)},
        scratch_shapes=[pltpu.SemaphoreType.DMA((3 * n,)), pltpu.SemaphoreType.DMA((3 * n,))],
        compiler_params=pltpu.CompilerParams(collective_id=SIBLING_PAIR_ID),
        name="gather_finish")(*lands, after)


def _half_blocks(part, axis):
    rows, cols = (part.shape[0] // N_CHIPS, part.shape[1]) if axis == 0 else (part.shape[0], part.shape[1] // N_CHIPS)
    return lax.empty((N_CHIPS, rows // 2, cols), part.dtype)


def _add_pair_layer(parts, gots, axes):
    k = len(parts)

    def body(c_ref, *refs):
        for f in range(k):
            a_ref, b_ref, o_ref = refs[2 * f], refs[2 * f + 1], refs[2 * k + f]
            o_ref[...] = (a_ref[...].astype(f32) + b_ref[...].astype(f32)).astype(o_ref.dtype)

    in_specs, out_specs, operands = [], [], []
    for part, got, axis in zip(parts, gots, axes):
        _, half, cols = got.shape
        if axis == 0:
            part = part.reshape(N_CHIPS, 2, half, cols)
            mine = pl.BlockSpec((None, None, half, cols), lambda j, c: (j, c[0], 0, 0))
        else:
            mine = pl.BlockSpec((half, cols), lambda j, c: (c[0], j))
        block = pl.BlockSpec((None, half, cols), lambda j, c: (j, 0, 0))
        in_specs += [mine, block]
        out_specs.append(block)
        operands += [part, got]
    return pl.pallas_call(
        body,
        grid_spec=pltpu.PrefetchScalarGridSpec(num_scalar_prefetch=1, grid=(N_CHIPS,), in_specs=in_specs, out_specs=out_specs),
        out_shape=[jax.ShapeDtypeStruct(g.shape, p.dtype) for p, g in zip(parts, gots)],
        name="add_pair_layer", compiler_params=_params(("arbitrary",)))(_scalar(lax.axis_index("c")), *operands)


def _scatter_start(sums, after):
    lands = [lax.empty((3,) + s.shape[1:], s.dtype) for s in sums]
    return _split_copy_start("scatter", _scatter_plan, 3 * len(sums), sums, lands, after)


def _scatter_wait(state, after):
    return _split_copy_wait("scatter", _scatter_plan, state, after)


def _add_slots(chip_sums, slots):
    k = len(chip_sums)

    def body(at_ref, *refs):
        for f in range(k):
            own_ref, s_ref, o_ref = refs[2 * f], refs[2 * f + 1], refs[2 * k + f]
            acc = own_ref[...].astype(f32)
            for j in range(3):
                acc = acc + s_ref[j].astype(f32)
            o_ref[...] = acc

    in_specs, out_specs, operands = [], [], []
    for cs, s in zip(chip_sums, slots):
        _, half, cols = cs.shape
        in_specs += [pl.BlockSpec((None, half, cols), lambda i, at: (at[0], 0, 0)), pl.BlockSpec((3, half, cols), lambda i, at: (0, 0, 0))]
        out_specs.append(pl.BlockSpec((None, half, cols), lambda i, at: (at[1], 0, 0)))
        operands += [cs, s]
    at = jnp.concatenate([_scalar(_chip_index(lax.axis_index("x"), lax.axis_index("y"))), _scalar(lax.axis_index("c"))])
    outs = pl.pallas_call(
        body,
        grid_spec=pltpu.PrefetchScalarGridSpec(num_scalar_prefetch=1, grid=(1,), in_specs=in_specs, out_specs=out_specs),
        out_shape=[jax.ShapeDtypeStruct((2,) + cs.shape[1:], f32) for cs in chip_sums],
        name="add_slots", compiler_params=_params(("arbitrary",)))(at, *operands)
    return [o.reshape(2 * o.shape[1], o.shape[2]) for o in outs]


def _adamw_math(w, grad, m, v):
    nm = ADAM_B1 * m + (1.0 - ADAM_B1) * grad
    nv = ADAM_B2 * v + (1.0 - ADAM_B2) * (grad * grad)
    m_hat = nm / (1.0 - ADAM_B1 ** ADAM_STEP)
    v_hat = nv / (1.0 - ADAM_B2 ** ADAM_STEP)
    return nm, nv, -ADAM_LR * (m_hat / (jnp.sqrt(v_hat) + ADAM_EPS) + ADAM_WD * w)


def _adamw_small(ws, gs, ms, vs):
    k = len(ws)

    def body(*refs):
        for f in range(k):
            w_ref, g_ref, m_ref, v_ref = refs[4 * f:4 * f + 4]
            d_ref, nm_ref, nv_ref = refs[4 * k + 3 * f:4 * k + 3 * f + 3]
            nm, nv, step = _adamw_math(w_ref[...], g_ref[...], m_ref[...], v_ref[...])
            d_ref[...] = step
            nm_ref[...] = nm
            nv_ref[...] = nv

    whole = pl.BlockSpec(memory_space=pltpu.VMEM)
    res = pl.pallas_call(
        body, in_specs=[whole] * (4 * k), out_specs=[whole] * (3 * k),
        out_shape=[jax.ShapeDtypeStruct(w.shape, f32) for w in ws for _ in range(3)],
        name="adamw_small", compiler_params=_params())(*[a for four in zip(ws, gs, ms, vs) for a in four])
    return [res[3 * f:3 * f + 3] for f in range(k)]


def _adamw_layer(l, ws, ms, vs, gs, outs, steps, after):
    k = len(ws)

    def body(*refs):
        ins, new = refs[:4 * k], refs[8 * k + 1:]
        for f in range(k):
            w_ref, m_ref, v_ref, g_ref = ins[4 * f:4 * f + 4]
            go_ref, d_ref, nm_ref, nv_ref = new[4 * f:4 * f + 4]
            grad = g_ref[...]
            nm, nv, step = _adamw_math(w_ref[...], grad, m_ref[...], v_ref[...])
            go_ref[...] = grad
            d_ref[...] = step
            nm_ref[...] = nm
            nv_ref[...] = nv

    in_specs, out_specs, operands = [], [], []
    for w, m, v, g in zip(ws, ms, vs, gs):
        _, rows, cols = w.shape
        tile = rows // steps
        layer = pl.BlockSpec((None, tile, cols), lambda i: (l, i, 0))
        in_specs += [layer] * 3 + [_rows(cols, tile)]
        out_specs += [layer] * 4
        operands += [w, m, v, g]
    flat_outs = [o for four in outs for o in four]
    res = pl.pallas_call(
        body, grid=(steps,),
        in_specs=in_specs + [ANY] * (4 * k + 1), out_specs=out_specs,
        out_shape=[jax.ShapeDtypeStruct(o.shape, f32) for o in flat_outs],
        input_output_aliases={4 * k + j: j for j in range(4 * k)},
        name="adamw_layer", compiler_params=_params(("arbitrary",)))(*operands, *flat_outs, after)
    return [res[4 * f:4 * f + 4] for f in range(k)]


SMALL = ("w_conv", "w_pool", "pool_scale", "sgu_ln_g", "w_spatial", "b_spatial", "ln1_g", "ln1_b", "ln2_g", "ln2_b")
WEIGHTS = ("w_in", "w_conv", "w_pool", "pool_scale", "sgu_ln_g", "w_spatial", "b_spatial", "w_o", "ln1_g", "ln1_b",
           "w_gate_up", "w_down", "ln2_g", "ln2_b")
BIG = ("w_in", "w_o", "w_gate_up", "w_down")
GROUPS = (("w_in", "w_o"), ("w_gate_up", "w_down"))
GROUP_AXES = ((0, 0), (1, 0))
SCATTER_HOOKS = 2
ADAMW_STEPS = (2, 4)
SMALL_LAYER_ROWS = 1024


def _pack_layer(arrays):
    flat = jnp.concatenate([a.reshape(-1) for a in arrays])
    return jnp.pad(flat, (0, SMALL_LAYER_ROWS * LANES - flat.shape[0])).reshape(SMALL_LAYER_ROWS, LANES)


def _unpack_layers(flat, shapes):
    out, at = {}, 0
    for name, shape in shapes.items():
        size = 1
        for d in shape:
            size *= d
        out[name] = flat[:, at:at + size].reshape((flat.shape[0],) + tuple(shape))
        at += size
    return out


def kernel(x, w_in, w_conv, w_pool, pool_scale, sgu_ln_g, w_spatial, b_spatial, w_o, ln1_g, ln1_b, w_gate_up, w_down, ln2_g, ln2_b, loss_target, m_w_in, m_w_conv, m_w_pool, m_pool_scale, m_sgu_ln_g, m_w_spatial, m_b_spatial, m_w_o, m_ln1_g, m_ln1_b, m_w_gate_up, m_w_down, m_ln2_g, m_ln2_b, v_w_in, v_w_conv, v_w_pool, v_pool_scale, v_sgu_ln_g, v_w_spatial, v_b_spatial, v_w_o, v_ln1_g, v_ln1_b, v_w_gate_up, v_w_down, v_ln2_g, v_ln2_b):
    weights = dict(w_in=w_in, w_conv=w_conv, w_pool=w_pool, pool_scale=pool_scale, sgu_ln_g=sgu_ln_g, w_spatial=w_spatial,
                   b_spatial=b_spatial, w_o=w_o, ln1_g=ln1_g, ln1_b=ln1_b, w_gate_up=w_gate_up, w_down=w_down, ln2_g=ln2_g, ln2_b=ln2_b)
    m_in = dict(w_in=m_w_in, w_conv=m_w_conv, w_pool=m_w_pool, pool_scale=m_pool_scale, sgu_ln_g=m_sgu_ln_g, w_spatial=m_w_spatial,
                b_spatial=m_b_spatial, w_o=m_w_o, ln1_g=m_ln1_g, ln1_b=m_ln1_b, w_gate_up=m_w_gate_up, w_down=m_w_down,
                ln2_g=m_ln2_g, ln2_b=m_ln2_b)
    v_in = dict(w_in=v_w_in, w_conv=v_w_conv, w_pool=v_w_pool, pool_scale=v_pool_scale, sgu_ln_g=v_sgu_ln_g, w_spatial=v_w_spatial,
                b_spatial=v_b_spatial, w_o=v_w_o, ln1_g=v_ln1_g, ln1_b=v_ln1_b, w_gate_up=v_w_gate_up, w_down=v_w_down,
                ln2_g=v_ln2_g, ln2_b=v_ln2_b)
    depth = w_in.shape[0]
    conv_cols = w_conv.shape[2]
    chip = _chip_index(lax.axis_index("x"), lax.axis_index("y"))

    big_w = dict(w_in=jnp.swapaxes(w_in, 1, 2), w_o=w_o, w_gate_up=w_gate_up, w_down=w_down)
    big_m = dict(w_in=jnp.swapaxes(m_w_in, 1, 2), w_o=m_w_o, w_gate_up=m_w_gate_up, w_down=m_w_down)
    big_v = dict(w_in=jnp.swapaxes(v_w_in, 1, 2), w_o=v_w_o, w_gate_up=v_w_gate_up, w_down=v_w_down)

    def place(l, g, after):
        return _place_layer([big_w[n][l] for n in GROUPS[g]], GROUP_AXES[g], after)

    def send(l, g, after):
        return _gather_start(*placed[l, g], GROUP_AXES[g], after)

    stages = [(l, g) for l in range(depth) for g in (0, 1)]
    placed, flights = {}, {}
    conv_flat = jnp.pad(w_conv.reshape(-1), (0, 16 * LANES - w_conv.size)).reshape(1, 16, LANES)
    conv_full = _gather_shards([conv_flat], x)[0].reshape(N_CHIPS, 16 * LANES)[:, :w_conv.size]
    conv_full = conv_full.reshape(N_CHIPS, depth, 3, conv_cols).transpose(1, 2, 0, 3).reshape(depth, 3, N_CHIPS * conv_cols)
    token = conv_full
    for st in stages[:2]:
        placed[st] = place(*st, token)
        flights[st], token = send(*st, placed[st][0][0])
    recent = token
    for st in stages[2:]:
        placed[st] = place(*st, token)
        recent = placed[st][0][0]
    act = x[0]
    layers, saved = [], []
    for i, (l, g) in enumerate(stages):
        if g == 0:
            w = dict(w_conv=conv_full[l], w_pool=w_pool[l], pool_scale=pool_scale[l][None], sgu_ln_g=sgu_ln_g[l][None],
                     w_spatial=w_spatial[l], b_spatial=b_spatial[l][:, :, None], ln1_g=ln1_g[l][None], ln1_b=ln1_b[l][None],
                     ln2_g=ln2_g[l][None], ln2_b=ln2_b[l][None])
        _, lands, token = _gather_wait(flights[l, g], GROUP_AXES[g], recent)
        if i + 2 < len(stages):
            flights[stages[i + 2]], token = send(*stages[i + 2], token)
        w.update(zip(GROUPS[g], _gather_finish(lands, GROUP_AXES[g], token)))
        if g == 0:
            sv = _fwd_mix(act, w, token)
            recent = sv["xhat1"]
        else:
            act = recent = _fwd_mlp(sv, w, token)
            layers.append(w)
            saved.append(sv)

    big_outs = {n: [lax.empty(big_w[n].shape, f32) for _ in range(4)] for n in BIG}
    small_sums = [None] * depth
    pending, updates = [], []
    latest = dict(token=None)

    def begin(l, g, parts):
        axes = GROUP_AXES[g] + (0,) * (len(parts) - len(GROUPS[g]))
        lands = [_half_blocks(p, ax) for p, ax in zip(parts, axes)]
        flight, latest["token"] = _split_copy_start("pair", _pair_plan(axes), N_CHIPS * len(parts), parts, lands, latest["token"],
                                                    sibling_only=True)
        pending.append(dict(l=l, g=g, axes=axes, step="pair", age=0, flight=flight))

    def advance(st, recent):
        if st["step"] == "pair":
            parts, got, _ = _split_copy_wait("pair", _pair_plan(st["axes"]), st["flight"], recent)
            sums = _add_pair_layer(parts, got, st["axes"])
            st["flight"], latest["token"] = _scatter_start(sums, latest["token"])
            st["step"] = "scatter"
        elif st["step"] == "scatter":
            sums, slots, _ = _scatter_wait(st["flight"], recent)
            filled = _add_slots(sums, slots)
            st["flight"], latest["token"] = _split_copy_start("join", _join_plan, len(filled), [], filled, latest["token"],
                                                              sibling_only=True)
            st["step"] = "join"
        else:
            _, summed, _ = _split_copy_wait("join", _join_plan, st["flight"], recent)
            updates.append((st["l"], st["g"], summed[:len(GROUPS[st["g"]])]))
            if st["g"] == 0:
                small_sums[st["l"]] = summed[-1]
            st["step"] = "done"
        st["age"] = 0

    def hook(recent):
        for st in reversed(list(pending)):
            st["age"] += 1
            if st["age"] >= SCATTER_HOOKS or st["step"] != "scatter":
                advance(st, recent)
                if st["step"] == "done":
                    pending.remove(st)
        return latest["token"]

    def update(count, recent):
        for l, g, totals in updates[:count]:
            names = GROUPS[g]
            new = _adamw_layer(l, [big_w[n] for n in names], [big_m[n] for n in names], [big_v[n] for n in names], totals,
                               [big_outs[n] for n in names], ADAMW_STEPS[g], latest["token"])
            big_outs.update(zip(names, new))
            recent = new[-1][1]
        del updates[:count]
        return recent

    grad_x, sq = _loss_head(act, loss_target[0])
    latest["token"] = sq
    grads = [None] * depth
    for l in reversed(range(depth)):
        dz, g_mlp = _bwd_mlp(grad_x, layers[l], saved[l], latest["token"], hook)
        hook(g_mlp["w_down"])
        begin(l, 1, [g_mlp[n] for n in GROUPS[1]])
        grad_x, g_mix = _bwd_mix(dz, layers[l], saved[l], latest["token"], hook)
        grads[l] = dict(g_mlp, **g_mix)
        hook(g_mix["w_o"])
        begin(l, 0, [g_mix[n] for n in GROUPS[0]] + [_pack_layer([grads[l][n] for n in SMALL])])
    recent = g_mix["w_o"]
    while pending:
        recent = update(-(-3 * len(updates) // 4), recent)
        hook(recent)
    update(len(updates), recent)
    loss = lax.psum(0.5 / D_MODEL * jnp.sum(sq), ("x", "y", "c"))

    small_sum = _gather_shards([jnp.stack(small_sums)], recent)[0].reshape(depth, SMALL_LAYER_ROWS * LANES)
    grad = {n: [jnp.swapaxes(o, 1, 2) for o in big_outs[n]] if n == "w_in" else big_outs[n] for n in BIG}
    delta = {n: o[1] for n, o in grad.items()}
    new_m = {n: o[2] for n, o in grad.items()}
    new_v = {n: o[3] for n, o in grad.items()}
    grad = {n: o[0] for n, o in grad.items()}
    grad.update(_unpack_layers(small_sum, {n: (3, N_CHIPS * conv_cols) if n == "w_conv" else weights[n].shape[1:] for n in SMALL}))
    grad["w_conv"] = lax.dynamic_slice_in_dim(grad["w_conv"], chip * conv_cols, conv_cols, axis=2)

    results = _adamw_small(*[[src[n] for n in SMALL] for src in (weights, grad, m_in, v_in)])
    for n, (step, moment1, moment2) in zip(SMALL, results):
        delta[n], new_m[n], new_v[n] = step, moment1, moment2

    return (loss, grad_x[None], *[grad[n] for n in WEIGHTS], *[delta[n] for n in WEIGHTS],
            *[new_m[n] for n in WEIGHTS], *[new_v[n] for n in WEIGHTS])
```

```python
import jax
import jax.numpy as jnp
from jax import lax
from jax.experimental import pallas as pl
from jax.experimental.pallas import tpu as pltpu

f32 = jnp.float32
bf16 = jnp.bfloat16

D_MODEL = 1024
DEPTH = 4
CONV_W = 384
POOL_W = 256
SGU_W = 384
IN_W = 3 * CONV_W + POOL_W + 2 * SGU_W
D_FF = 2816
CHUNK = 128
HEAD = 64
POOL_WINDOWS = (2, 4, 8, 16)
ALPHA = float((2 * DEPTH) ** 0.25)
LN_EPS = 1e-5
ADAM_LR = 0.001
ADAM_B1 = 0.9
ADAM_B2 = 0.999
ADAM_EPS = 1e-08
ADAM_WD = 0.01
ADAM_STEP = 10

LANES = 128
TOKEN_TILE = 256
N_CHIPS = 4
VMEM_LIMIT = 56 * 1024 * 1024

BLK_XA, BLK_GB, BLK_GC, BLK_P, BLK_U, BLK_V = 0, 3, 6, 9, 11, 14

MESH = pl.DeviceIdType.MESH


def _params(sem=None):
    return pltpu.CompilerParams(dimension_semantics=sem, vmem_limit_bytes=VMEM_LIMIT)


def _rows(width, tile=TOKEN_TILE):
    return pl.BlockSpec((tile, width), lambda i: (i, 0))


def _resident(shape):
    zeros = (0,) * len(shape)
    return pl.BlockSpec(shape, lambda *_: zeros, pipeline_mode=pl.Buffered(1))


def _nt(a, b):
    return lax.dot_general(a, b, (((1,), (1,)), ((), ())), preferred_element_type=f32)


def _tn(a, b):
    return lax.dot_general(a, b, (((0,), (0,)), ((), ())), preferred_element_type=f32)


def _mm(a, b):
    return jnp.dot(a, b, preferred_element_type=f32)


def _norm_fwd(z):
    mu = jnp.mean(z, axis=-1, keepdims=True)
    zc = z - mu
    var = jnp.mean(zc * zc, axis=-1, keepdims=True)
    rstd = lax.rsqrt(var + LN_EPS)
    return zc * rstd, rstd


def _norm_bwd(dxhat, xhat, rstd):
    m1 = jnp.mean(dxhat, axis=-1, keepdims=True)
    m2 = jnp.mean(dxhat * xhat, axis=-1, keepdims=True)
    return rstd * (dxhat - m1 - xhat * m2)


def _proj(x, w_in_b, after):
    s = x.shape[0]

    def body(x_ref, w_ref, after_ref, p_ref, xb_ref):
        xb = x_ref[...].astype(bf16)
        xb_ref[...] = xb
        p_ref[...] = _nt(xb, w_ref[...])

    return pl.pallas_call(
        body, grid=(s // TOKEN_TILE,),
        in_specs=[_rows(D_MODEL), _resident((IN_W, D_MODEL)), pl.BlockSpec(memory_space=pl.ANY)],
        out_specs=[_rows(IN_W), _rows(D_MODEL)],
        out_shape=[jax.ShapeDtypeStruct((s, IN_W), f32), jax.ShapeDtypeStruct((s, D_MODEL), bf16)],
        name="proj", compiler_params=_params(("arbitrary",)))(x, w_in_b, after)


def _row_ranges(parts):
    out, at = [], 0
    for p in parts:
        out.append((at, at + p.shape[1]))
        at += p.shape[1]
    return out


def _wo_ln1(mix, x, w_o_b, g, b):
    s = x.shape[0]
    n = len(mix)
    ranges = _row_ranges(mix)

    def body(*refs):
        m_refs = refs[:n]
        x_ref, w_ref, g_ref, b_ref, xhat_ref, rstd_ref, hb_ref = refs[n:]
        z = ALPHA * x_ref[...]
        for m_ref, (lo, hi) in zip(m_refs, ranges):
            z = z + _mm(m_ref[...], w_ref[lo:hi, :])
        xhat, rstd = _norm_fwd(z)
        xhat_ref[...] = xhat
        rstd_ref[...] = rstd
        hb_ref[...] = (xhat * g_ref[...] + b_ref[...]).astype(bf16)

    return pl.pallas_call(
        body, grid=(s // TOKEN_TILE,),
        in_specs=[_rows(m.shape[1]) for m in mix] + [_rows(D_MODEL), _resident((D_MODEL, D_MODEL)), _resident((1, D_MODEL)),
                                                     _resident((1, D_MODEL))],
        out_specs=[_rows(D_MODEL), _rows(1), _rows(D_MODEL)],
        out_shape=[jax.ShapeDtypeStruct((s, D_MODEL), f32), jax.ShapeDtypeStruct((s, 1), f32),
                   jax.ShapeDtypeStruct((s, D_MODEL), bf16)],
        name="wo_ln1", compiler_params=_params(("arbitrary",)))(*mix, x, w_o_b, g, b)


def _mlp_fwd(xhat1, g1, b1, w_gu_b, w_down_b, g2, b2, after):
    s = xhat1.shape[0]

    def body(xh_ref, g1_ref, b1_ref, wgu_ref, wd_ref, g2_ref, b2_ref, after_ref, gu_ref, xhat2_ref, rstd2_ref, y_ref):
        h = xh_ref[...] * g1_ref[...] + b1_ref[...]
        gu = _mm(h.astype(bf16), wgu_ref[...])
        gu_ref[...] = gu
        gate = gu[:, :D_FF]
        act = gate * jax.nn.sigmoid(gate) * gu[:, D_FF:]
        z = ALPHA * h + _mm(act.astype(bf16), wd_ref[...])
        xhat2, rstd2 = _norm_fwd(z)
        xhat2_ref[...] = xhat2
        rstd2_ref[...] = rstd2
        y_ref[...] = xhat2 * g2_ref[...] + b2_ref[...]

    vec = _resident((1, D_MODEL))
    return pl.pallas_call(
        body, grid=(s // TOKEN_TILE,),
        in_specs=[_rows(D_MODEL), vec, vec, _resident((D_MODEL, 2 * D_FF)), _resident((D_FF, D_MODEL)), vec, vec,
                  pl.BlockSpec(memory_space=pl.ANY)],
        out_specs=[_rows(2 * D_FF), _rows(D_MODEL), _rows(1), _rows(D_MODEL)],
        out_shape=[jax.ShapeDtypeStruct((s, 2 * D_FF), f32), jax.ShapeDtypeStruct((s, D_MODEL), f32),
                   jax.ShapeDtypeStruct((s, 1), f32), jax.ShapeDtypeStruct((s, D_MODEL), f32)],
        name="mlp_fwd", compiler_params=_params(("arbitrary",)))(xhat1, g1, b1, w_gu_b, w_down_b, g2, b2, after)


def _loss_head(y, target):
    s = y.shape[0]

    def body(y_ref, t_ref, dy_ref, sq_ref):
        @pl.when(pl.program_id(0) == 0)
        def _():
            sq_ref[...] = jnp.zeros_like(sq_ref)

        e = y_ref[...] - t_ref[...]
        dy_ref[...] = e * (1.0 / D_MODEL)
        sq_ref[...] += jnp.sum(e * e, axis=0, keepdims=True)

    return pl.pallas_call(
        body, grid=(s // TOKEN_TILE,),
        in_specs=[_rows(D_MODEL), _rows(D_MODEL)],
        out_specs=[_rows(D_MODEL), pl.BlockSpec((1, D_MODEL), lambda i: (0, 0))],
        out_shape=[jax.ShapeDtypeStruct((s, D_MODEL), f32), jax.ShapeDtypeStruct((1, D_MODEL), f32)],
        name="loss_head", compiler_params=_params(("arbitrary",)))(y, target)


def _mlp_bwd(dy, xhat2, rstd2, g2, gu, w_gu_b, w_down_b, xhat1, rstd1, g1, w_o_b, after):
    s = dy.shape[0]

    def body(dy_ref, xh_ref, rs_ref, g2_ref, gu_ref, wgu_ref, wd_ref, xh1_ref, rs1_ref, g1_ref, wo_ref, after_ref,
             dz_ref, act_ref, dgu_ref, dz1_ref, dz1b_ref, dm_ref, gg_ref, gb_ref, gg1_ref, gb1_ref):
        @pl.when(pl.program_id(0) == 0)
        def _():
            for ref in (gg_ref, gb_ref, gg1_ref, gb1_ref):
                ref[...] = jnp.zeros_like(ref)

        dy_t = dy_ref[...]
        xhat = xh_ref[...]
        gg_ref[...] += jnp.sum(dy_t * xhat, axis=0, keepdims=True)
        gb_ref[...] += jnp.sum(dy_t, axis=0, keepdims=True)
        dz = _norm_bwd(dy_t * g2_ref[...], xhat, rs_ref[...])
        dzb = dz.astype(bf16)
        dz_ref[...] = dzb
        dact = _nt(dzb, wd_ref[...])
        gate = gu_ref[:, :D_FF]
        up = gu_ref[:, D_FF:]
        sg = jax.nn.sigmoid(gate)
        silu = gate * sg
        act_ref[...] = (silu * up).astype(bf16)
        dgu_ref[:, :D_FF] = (dact * up * (sg * (1.0 + gate * (1.0 - sg)))).astype(bf16)
        dgu_ref[:, D_FF:] = (dact * silu).astype(bf16)
        dh = ALPHA * dz + _nt(dgu_ref[...], wgu_ref[...])
        xhat1 = xh1_ref[...]
        gg1_ref[...] += jnp.sum(dh * xhat1, axis=0, keepdims=True)
        gb1_ref[...] += jnp.sum(dh, axis=0, keepdims=True)
        dz1 = _norm_bwd(dh * g1_ref[...], xhat1, rs1_ref[...])
        dz1_ref[...] = dz1
        dz1b = dz1.astype(bf16)
        dz1b_ref[...] = dz1b
        dm_ref[...] = _nt(dz1b, wo_ref[...])

    vec, vec_out = _resident((1, D_MODEL)), pl.BlockSpec((1, D_MODEL), lambda i: (0, 0))
    tokens_f32, tokens_bf16 = jax.ShapeDtypeStruct((s, D_MODEL), f32), jax.ShapeDtypeStruct((s, D_MODEL), bf16)
    sums = jax.ShapeDtypeStruct((1, D_MODEL), f32)
    return pl.pallas_call(
        body, grid=(s // TOKEN_TILE,),
        in_specs=[_rows(D_MODEL), _rows(D_MODEL), _rows(1), vec, _rows(2 * D_FF),
                  _resident((D_MODEL, 2 * D_FF)), _resident((D_FF, D_MODEL)), _rows(D_MODEL), _rows(1), vec,
                  _resident((D_MODEL, D_MODEL)), pl.BlockSpec(memory_space=pl.ANY)],
        out_specs=[_rows(D_MODEL), _rows(D_FF), _rows(2 * D_FF), _rows(D_MODEL), _rows(D_MODEL), _rows(D_MODEL),
                   vec_out, vec_out, vec_out, vec_out],
        out_shape=[tokens_bf16, jax.ShapeDtypeStruct((s, D_FF), bf16), jax.ShapeDtypeStruct((s, 2 * D_FF), bf16),
                   tokens_f32, tokens_bf16, tokens_f32, sums, sums, sums, sums],
        name="mlp_bwd", compiler_params=_params(("arbitrary",)))(
            dy, xhat2, rstd2, g2, gu, w_gu_b, w_down_b, xhat1, rstd1, g1, w_o_b, after)


def _dx(dz1, dparts, w_in_t, after):
    s = dz1.shape[0]
    n = len(dparts)
    ranges = _row_ranges(dparts)

    def body(*refs):
        d_refs = refs[:n]
        dz_ref, w_ref, _, dx_ref = refs[n:]
        acc = ALPHA * dz_ref[...]
        for d_ref, (lo, hi) in zip(d_refs, ranges):
            acc = acc + _mm(d_ref[...], w_ref[lo:hi, :])
        dx_ref[...] = acc

    return pl.pallas_call(
        body, grid=(s // TOKEN_TILE,),
        in_specs=[_rows(d.shape[1]) for d in dparts] + [_rows(D_MODEL), _resident((IN_W, D_MODEL)),
                                                        pl.BlockSpec(memory_space=pl.ANY)],
        out_specs=_rows(D_MODEL),
        out_shape=jax.ShapeDtypeStruct((s, D_MODEL), f32),
        name="dx", compiler_params=_params(("arbitrary",)))(*dparts, dz1, w_in_t, after)


def _weight_grad_rows(parts, b, bn):
    s, n_cols = b.shape
    n = len(parts)
    ranges = _row_ranges(parts)
    m = ranges[-1][1]

    def body(*refs):
        p_refs = refs[:n]
        b_ref, o_ref = refs[n:]
        for p_ref, (lo, hi) in zip(p_refs, ranges):
            o_ref[lo:hi, :] = _tn(p_ref[...], b_ref[...]).astype(bf16)

    return pl.pallas_call(
        body, grid=(n_cols // bn,),
        in_specs=[_resident(p.shape) for p in parts] + [pl.BlockSpec((s, bn), lambda j: (0, j))],
        out_specs=pl.BlockSpec((m, bn), lambda j: (0, j)),
        out_shape=jax.ShapeDtypeStruct((m, n_cols), bf16),
        name="weight_grad_rows", compiler_params=_params(("arbitrary",)))(*parts, b)


def _weight_grad(a, b, bm, bn, after):
    s, m = a.shape
    n = b.shape[1]

    def body(a_ref, b_ref, after_ref, o_ref):
        o_ref[...] = _tn(a_ref[...], b_ref[...]).astype(bf16)

    return pl.pallas_call(
        body, grid=(m // bm, n // bn),
        in_specs=[pl.BlockSpec((s, bm), lambda i, j: (0, i)), pl.BlockSpec((s, bn), lambda i, j: (0, j)),
                  pl.BlockSpec(memory_space=pl.ANY)],
        out_specs=pl.BlockSpec((bm, bn), lambda i, j: (i, j)),
        out_shape=jax.ShapeDtypeStruct((m, n), bf16),
        name="weight_grad", compiler_params=_params(("arbitrary", "arbitrary")))(a, b, after)


def _shift_down(a, k):
    row = lax.broadcasted_iota(jnp.int32, a.shape, 0)
    return jnp.where(row >= k, pltpu.roll(a, k, 0), 0.0)


def _shift_up(a, k):
    n = a.shape[0]
    row = lax.broadcasted_iota(jnp.int32, a.shape, 0)
    return jnp.where(row < n - k, pltpu.roll(a, n - k, 0), 0.0)


def _slab(s, block):
    return pl.BlockSpec((s, LANES), lambda k: (0, block + k))


def _conv_y(z, w):
    return w[0:1, :] * _shift_down(z, 2) + w[1:2, :] * _shift_down(z, 1) + w[2:3, :] * z


def _conv_fwd(proj, w_conv):
    s = proj.shape[0]

    def body(xa_ref, gb_ref, gc_ref, w_ref, o_ref):
        z = gc_ref[...] * xa_ref[...]
        o_ref[...] = (gb_ref[...] * _conv_y(z, w_ref[...])).astype(bf16)

    return pl.pallas_call(
        body, grid=(CONV_W // LANES,),
        in_specs=[_slab(s, BLK_XA), _slab(s, BLK_GB), _slab(s, BLK_GC), pl.BlockSpec((3, LANES), lambda k: (0, k))],
        out_specs=_slab(s, 0),
        out_shape=jax.ShapeDtypeStruct((s, CONV_W), bf16),
        name="conv_fwd", compiler_params=_params(("arbitrary",)))(proj, proj, proj, w_conv)


def _conv_bwd(proj, dmix, w_conv, after):
    s = proj.shape[0]

    def body(xa_ref, gb_ref, gc_ref, dy_ref, w_ref, after_ref, dxa_ref, dgb_ref, dgc_ref, dw_ref):
        xa = xa_ref[...]
        gc = gc_ref[...]
        w = w_ref[...]
        z = gc * xa
        dya = dy_ref[...]
        dgb_ref[...] = (dya * _conv_y(z, w)).astype(bf16)
        dy = dya * gb_ref[...]
        dz = w[2:3, :] * dy + w[1:2, :] * _shift_up(dy, 1) + w[0:1, :] * _shift_up(dy, 2)
        dxa_ref[...] = (dz * gc).astype(bf16)
        dgc_ref[...] = (dz * xa).astype(bf16)
        dw_ref[0:1, :] = jnp.sum(dy * _shift_down(z, 2), axis=0, keepdims=True)
        dw_ref[1:2, :] = jnp.sum(dy * _shift_down(z, 1), axis=0, keepdims=True)
        dw_ref[2:3, :] = jnp.sum(dy * z, axis=0, keepdims=True)

    out = jax.ShapeDtypeStruct((s, CONV_W), bf16)
    return pl.pallas_call(
        body, grid=(CONV_W // LANES,),
        in_specs=[_slab(s, BLK_XA), _slab(s, BLK_GB), _slab(s, BLK_GC), _slab(s, 0), pl.BlockSpec((3, LANES), lambda k: (0, k)),
                  pl.BlockSpec(memory_space=pl.ANY)],
        out_specs=[_slab(s, 0), _slab(s, 0), _slab(s, 0), pl.BlockSpec((3, LANES), lambda k: (0, k))],
        out_shape=[out, out, out, jax.ShapeDtypeStruct((3, CONV_W), f32)],
        name="conv_bwd", compiler_params=_params(("arbitrary",)))(proj, proj, proj, dmix, w_conv, after)


def _pool_window(k):
    lane = lax.broadcasted_iota(jnp.int32, (1, LANES), 1)
    low = lane < HEAD
    first = k == 0
    wlen = jnp.where(low, jnp.where(first, POOL_WINDOWS[0], POOL_WINDOWS[2]), jnp.where(first, POOL_WINDOWS[1], POOL_WINDOWS[3]))
    return wlen, low, first


def _pool_diff(p, k):
    wlen, low, first = _pool_window(k)
    s2 = p + _shift_down(p, 1)
    s4 = s2 + _shift_down(s2, 2)
    s8 = s4 + _shift_down(s4, 4)
    s16 = s8 + _shift_down(s8, 8)
    win = jnp.where(low, jnp.where(first, s2, s8), jnp.where(first, s4, s16))
    row = lax.broadcasted_iota(jnp.int32, p.shape, 0)
    count = jnp.minimum(row + 1, wlen).astype(f32)
    return win / count - p, count


def _pool_weight(w_ref):
    zero = jnp.zeros((HEAD, HEAD), f32)
    top = jnp.concatenate([w_ref[0], zero], axis=1)
    bottom = jnp.concatenate([zero, w_ref[1]], axis=1)
    return jnp.concatenate([top, bottom], axis=0).astype(bf16)


def _pool_fwd(proj, w_pool, pool_scale):
    s = proj.shape[0]

    def body(p_ref, w_ref, sc_ref, o_ref):
        d, _ = _pool_diff(p_ref[...], pl.program_id(0))
        o_ref[...] = (_mm(d.astype(bf16), _pool_weight(w_ref)) * sc_ref[...]).astype(bf16)

    return pl.pallas_call(
        body, grid=(POOL_W // LANES,),
        in_specs=[_slab(s, BLK_P), pl.BlockSpec((2, HEAD, HEAD), lambda k: (k, 0, 0)), pl.BlockSpec((1, LANES), lambda k: (0, k))],
        out_specs=_slab(s, 0),
        out_shape=jax.ShapeDtypeStruct((s, POOL_W), bf16),
        name="pool_fwd", compiler_params=_params(("arbitrary",)))(proj, w_pool, pool_scale)


def _pool_bwd(proj, dmix, w_pool, pool_scale):
    s = proj.shape[0]

    def body(p_ref, dy_ref, w_ref, sc_ref, dp_ref, dw_ref, dsc_ref):
        k = pl.program_id(0)
        d, count = _pool_diff(p_ref[...], k)
        wbd = _pool_weight(w_ref)
        db = d.astype(bf16)
        dyb = dy_ref[...]
        dsc_ref[...] = jnp.sum(dyb * _mm(db, wbd), axis=0, keepdims=True)
        dpre = (dyb * sc_ref[...]).astype(bf16)
        dwbd = _tn(db, dpre)
        dw_ref[0] = dwbd[:HEAD, :HEAD]
        dw_ref[1] = dwbd[HEAD:, HEAD:]
        dd = _nt(dpre, wbd)
        e = dd / count
        wlen, low, first = _pool_window(k)
        a2 = e + _shift_up(e, 1)
        a4 = a2 + _shift_up(a2, 2)
        a8 = a4 + _shift_up(a4, 4)
        a16 = a8 + _shift_up(a8, 8)
        back = jnp.where(low, jnp.where(first, a2, a8), jnp.where(first, a4, a16))
        dp_ref[...] = (back - dd).astype(bf16)

    return pl.pallas_call(
        body, grid=(POOL_W // LANES,),
        in_specs=[_slab(s, BLK_P), _slab(s, CONV_W // LANES), pl.BlockSpec((2, HEAD, HEAD), lambda k: (k, 0, 0)),
                  pl.BlockSpec((1, LANES), lambda k: (0, k))],
        out_specs=[_slab(s, 0), pl.BlockSpec((2, HEAD, HEAD), lambda k: (k, 0, 0)), pl.BlockSpec((1, LANES), lambda k: (0, k))],
        out_shape=[jax.ShapeDtypeStruct((s, POOL_W), bf16), jax.ShapeDtypeStruct((4, HEAD, HEAD), f32),
                   jax.ShapeDtypeStruct((1, POOL_W), f32)],
        name="pool_bwd", compiler_params=_params(("arbitrary",)))(proj, dmix, w_pool, pool_scale)


SGU_UNROLL = 4
INV_SQRT2 = 0.7071067811865476
INV_SQRT_2PI = 0.3989422804014327


def _gelu(x):
    return 0.5 * x * (1.0 + lax.erf(x * INV_SQRT2))


def _gelu_grad(x):
    return 0.5 * (1.0 + lax.erf(x * INV_SQRT2)) + x * (INV_SQRT_2PI * jnp.exp(-0.5 * x * x))


def _head_mean(a, low):
    s_low = jnp.sum(jnp.where(low, a, 0.0), axis=-1, keepdims=True)
    s_high = jnp.sum(jnp.where(low, 0.0, a), axis=-1, keepdims=True)
    return jnp.where(low, s_low, s_high) * (1.0 / HEAD)


def _tril():
    r = lax.broadcasted_iota(jnp.int32, (CHUNK, CHUNK), 0)
    c = lax.broadcasted_iota(jnp.int32, (CHUNK, CHUNK), 1)
    return r >= c


def _sgu_chunk(up, vp, g, wm0, wm1, b0, b1, low):
    ug = _gelu(up)
    vg = _gelu(vp)
    vc = vg - _head_mean(vg, low)
    rstd = lax.rsqrt(_head_mean(vc * vc, low) + LN_EPS)
    vn = vc * rstd
    vb = (vn * g).astype(bf16)
    mixed = jnp.where(low, _mm(wm0, vb) + b0, _mm(wm1, vb) + b1)
    return ug, vn, rstd, vb, mixed


def _sgu_specs(s):
    return [_slab(s, BLK_U), _slab(s, BLK_V), pl.BlockSpec((1, LANES), lambda k: (0, k)),
            pl.BlockSpec((2, CHUNK, CHUNK), lambda k: (k, 0, 0)), pl.BlockSpec((2, CHUNK, 1), lambda k: (k, 0, 0))]


def _sgu_fwd(proj, sgu_g, w_spatial, b_spatial3):
    s = proj.shape[0]

    def body(u_ref, v_ref, g_ref, w_ref, b_ref, o_ref):
        low = lax.broadcasted_iota(jnp.int32, (1, LANES), 1) < HEAD
        mask = _tril()
        wm0 = jnp.where(mask, w_ref[0], 0.0).astype(bf16)
        wm1 = jnp.where(mask, w_ref[1], 0.0).astype(bf16)
        g = g_ref[...]
        b0 = b_ref[0]
        b1 = b_ref[1]

        def chunk(n, carry):
            rows = pl.ds(pl.multiple_of(n * CHUNK, CHUNK), CHUNK)
            ug, _, _, _, mixed = _sgu_chunk(u_ref[rows, :], v_ref[rows, :], g, wm0, wm1, b0, b1, low)
            o_ref[rows, :] = (ug * mixed).astype(bf16)
            return carry

        lax.fori_loop(0, s // CHUNK, chunk, 0, unroll=SGU_UNROLL)

    return pl.pallas_call(
        body, grid=(SGU_W // LANES,),
        in_specs=_sgu_specs(s),
        out_specs=_slab(s, 0),
        out_shape=jax.ShapeDtypeStruct((s, SGU_W), bf16),
        name="sgu_fwd", compiler_params=_params(("arbitrary",)))(proj, proj, sgu_g, w_spatial, b_spatial3)


def _sgu_bwd(proj, dmix, sgu_g, w_spatial, b_spatial3):
    s = proj.shape[0]

    def body(u_ref, v_ref, g_ref, w_ref, b_ref, dy_ref, du_ref, dv_ref, dg_ref, dw_ref, db_ref):
        low = lax.broadcasted_iota(jnp.int32, (1, LANES), 1) < HEAD
        mask = _tril()
        w0 = jnp.where(mask, w_ref[0], 0.0)
        w1 = jnp.where(mask, w_ref[1], 0.0)
        wm0 = w0.astype(bf16)
        wm1 = w1.astype(bf16)
        wt0 = w0.T.astype(bf16)
        wt1 = w1.T.astype(bf16)
        g = g_ref[...]
        b0 = b_ref[0]
        b1 = b_ref[1]
        dg_ref[...] = jnp.zeros_like(dg_ref)
        dw_ref[...] = jnp.zeros_like(dw_ref)
        db_ref[...] = jnp.zeros_like(db_ref)

        def chunk(n, carry):
            rows = pl.ds(pl.multiple_of(n * CHUNK, CHUNK), CHUNK)
            up = u_ref[rows, :]
            vp = v_ref[rows, :]
            ug, vn, rstd, vb, mixed = _sgu_chunk(up, vp, g, wm0, wm1, b0, b1, low)
            dy = dy_ref[rows, :]
            du_ref[rows, :] = (dy * mixed * _gelu_grad(up)).astype(bf16)
            dmix_c = dy * ug
            db_ref[0] += jnp.sum(jnp.where(low, dmix_c, 0.0), axis=-1, keepdims=True)
            db_ref[1] += jnp.sum(jnp.where(low, 0.0, dmix_c), axis=-1, keepdims=True)
            dmb = dmix_c.astype(bf16)
            zero = jnp.zeros_like(dmb)
            dw_ref[0] += _nt(jnp.where(low, dmb, zero), vb)
            dw_ref[1] += _nt(jnp.where(low, zero, dmb), vb)
            dvnorm = jnp.where(low, _mm(wt0, dmb), _mm(wt1, dmb))
            dg_ref[...] += jnp.sum(dvnorm * vn, axis=0, keepdims=True)
            dvn = dvnorm * g
            dvg = rstd * (dvn - _head_mean(dvn, low) - vn * _head_mean(dvn * vn, low))
            dv_ref[rows, :] = (dvg * _gelu_grad(vp)).astype(bf16)
            return carry

        lax.fori_loop(0, s // CHUNK, chunk, 0, unroll=SGU_UNROLL)
        dw_ref[0] = jnp.where(mask, dw_ref[0], 0.0)
        dw_ref[1] = jnp.where(mask, dw_ref[1], 0.0)

    out = jax.ShapeDtypeStruct((s, SGU_W), bf16)
    return pl.pallas_call(
        body, grid=(SGU_W // LANES,),
        in_specs=_sgu_specs(s) + [_slab(s, (CONV_W + POOL_W) // LANES)],
        out_specs=[_slab(s, 0), _slab(s, 0), pl.BlockSpec((1, LANES), lambda k: (0, k)),
                   pl.BlockSpec((2, CHUNK, CHUNK), lambda k: (k, 0, 0)), pl.BlockSpec((2, CHUNK, 1), lambda k: (k, 0, 0))],
        out_shape=[out, out, jax.ShapeDtypeStruct((1, SGU_W), f32), jax.ShapeDtypeStruct((6, CHUNK, CHUNK), f32),
                   jax.ShapeDtypeStruct((6, CHUNK, 1), f32)],
        name="sgu_bwd", compiler_params=_params(("arbitrary",)))(proj, proj, sgu_g, w_spatial, b_spatial3, dmix)


def _fwd_mix(x, w, after):
    proj, xb = _proj(x, w["w_in"], after)
    mix = [_conv_fwd(proj, w["w_conv"]), _pool_fwd(proj, w["w_pool"], w["pool_scale"]),
           _sgu_fwd(proj, w["sgu_ln_g"], w["w_spatial"], w["b_spatial"])]
    xhat1, rstd1, hb = _wo_ln1(mix, x, w["w_o"], w["ln1_g"], w["ln1_b"])
    return dict(proj=proj, xb=xb, mix=mix, xhat1=xhat1, rstd1=rstd1, hb=hb)


def _fwd_mlp(sv, w, after):
    gu, xhat2, rstd2, y = _mlp_fwd(sv["xhat1"], w["ln1_g"], w["ln1_b"], w["w_gate_up"], w["w_down"], w["ln2_g"], w["ln2_b"], after)
    sv.update(gu=gu, xhat2=xhat2, rstd2=rstd2)
    return y


def _bwd_mlp(dy, w, sv, after, hook):
    dz2b, actb, dgub, dz1, dz1b, dmix, g_ln2_g, g_ln2_b, g_ln1_g, g_ln1_b = _mlp_bwd(
        dy, sv["xhat2"], sv["rstd2"], w["ln2_g"], sv["gu"], w["w_gate_up"], w["w_down"], sv["xhat1"], sv["rstd1"], w["ln1_g"],
        w["w_o"], after)
    after = hook(dz1)
    grads = dict(w_gate_up=_weight_grad(sv["hb"], dgub, D_MODEL, D_FF // 2, after),
                 w_down=_weight_grad(actb, dz2b, D_FF // 2, D_MODEL, after),
                 ln2_g=g_ln2_g, ln2_b=g_ln2_b, ln1_g=g_ln1_g, ln1_b=g_ln1_b)
    return (dz1, dz1b, dmix), grads


def _bwd_mix(dz, w, sv, after, hook):
    dz1, dz1b, dmix = dz
    dxa, dgb, dgc, g_conv = _conv_bwd(sv["proj"], dmix, w["w_conv"], after)
    dp, g_pool, g_pscale = _pool_bwd(sv["proj"], dmix, w["w_pool"], w["pool_scale"])
    du, dv, g_sgu_g, g_spatial, g_bsp = _sgu_bwd(sv["proj"], dmix, w["sgu_ln_g"], w["w_spatial"], w["b_spatial"])
    dparts = [dxa, dgb, dgc, dp, du, dv]
    dx = _dx(dz1, dparts, w["w_in"], hook(du))
    grads = dict(
        w_in=_weight_grad_rows(dparts, sv["xb"], 512), w_o=_weight_grad_rows(sv["mix"], dz1b, D_MODEL),
        w_conv=g_conv, w_pool=g_pool, pool_scale=g_pscale, sgu_ln_g=g_sgu_g, w_spatial=g_spatial,
        b_spatial=g_bsp.reshape(6, CHUNK))
    return dx, grads


def _local_step(x, target, layers):
    saved = []
    for w in layers:
        sv = _fwd_mix(x, w, x)
        x = _fwd_mlp(sv, w, x)
        saved.append(sv)
    dy, sq = _loss_head(x, target)
    grads = [None] * len(layers)
    for l in reversed(range(len(layers))):
        dz, g_mlp = _bwd_mlp(dy, layers[l], saved[l], sq, lambda a: a)
        dy, g_mix = _bwd_mix(dz, layers[l], saved[l], dz[0], lambda a: a)
        grads[l] = dict(g_mlp, **g_mix)
    return sq, dy, grads


ANY = pl.BlockSpec(memory_space=pl.ANY)


def _place():
    x, y, c = lax.axis_index("x"), lax.axis_index("y"), lax.axis_index("c")
    others = [(1 - x, y), (x, 1 - y), (1 - x, 1 - y)]
    return x, y, c, others


def _chip_index(cx, cy):
    return 2 * cx + cy


def _half(ref_rows, c):
    half = ref_rows // 2
    return pl.ds(pl.multiple_of(c * half, 8), half)


def _remote(src, dst, send_sem, recv_sem, device):
    return pltpu.make_async_remote_copy(src_ref=src, dst_ref=dst, send_sem=send_sem, recv_sem=recv_sem,
                                        device_id=device, device_id_type=MESH)


def _gather_shards(shards, after):
    n = len(shards)
    base, total = [], 0
    for s in shards:
        base.append(total)
        total += 6 * s.shape[0]

    def body(*refs):
        ins, outs = refs[:n], refs[n + 1:2 * n + 1]
        send, recv = refs[2 * n + 1:]
        x, y, c, others = _place()
        me = _chip_index(x, y)
        sib = (x, y, 1 - c)
        sends = []
        for f in range(n):
            depth, rows = ins[f].shape[0], ins[f].shape[1]
            for l in range(depth):
                for k, (cx, cy) in enumerate(others):
                    sem = base[f] + 6 * l + k
                    cp = _remote(ins[f].at[l, _half(rows, c)], outs[f].at[l, me, _half(rows, c)],
                                 send.at[sem], recv.at[sem], (cx, cy, c))
                    cp.start()
                    sends.append(cp)
        for f in range(n):
            depth, rows = ins[f].shape[0], ins[f].shape[1]
            for l in range(depth):
                for k, (cx, cy) in enumerate(others):
                    sem = base[f] + 6 * l + k
                    landed = outs[f].at[l, _chip_index(cx, cy), _half(rows, c)]
                    _remote(landed, landed, send.at[sem], recv.at[sem], (cx, cy, c)).wait_recv()
                    cp = _remote(landed, landed, send.at[sem + 3], recv.at[sem + 3], sib)
                    cp.start()
                    sends.append(cp)
        for f in range(n):
            depth, rows = ins[f].shape[0], ins[f].shape[1]
            for l in range(depth):
                for k, (cx, cy) in enumerate(others):
                    sem = base[f] + 6 * l + k + 3
                    passed = outs[f].at[l, _chip_index(cx, cy), _half(rows, 1 - c)]
                    _remote(passed, passed, send.at[sem], recv.at[sem], sib).wait_recv()
        for cp in sends:
            cp.wait_send()

    gathered = pl.pallas_call(
        body, in_specs=[ANY] * (n + 1), out_specs=[ANY] * n,
        out_shape=[jax.ShapeDtypeStruct((s.shape[0], N_CHIPS) + s.shape[1:], s.dtype) for s in shards],
        scratch_shapes=[pltpu.SemaphoreType.DMA((total,)), pltpu.SemaphoreType.DMA((total,))],
        name="gather_shards")(*shards, after)
    return [_place_own(g, s) for g, s in zip(gathered, shards)]


def _scalar(value):
    return jnp.reshape(value, (1,)).astype(jnp.int32)


def _place_own(blocks, shard):
    depth, rows, cols = shard.shape

    def body(me_ref, b_ref, s_ref, o_ref):
        o_ref[...] = s_ref[...]

    return pl.pallas_call(
        body,
        grid_spec=pltpu.PrefetchScalarGridSpec(
            num_scalar_prefetch=1, grid=(depth,),
            in_specs=[ANY, pl.BlockSpec((None, rows, cols), lambda l, me: (l, 0, 0))],
            out_specs=pl.BlockSpec((None, None, rows, cols), lambda l, me: (l, me[0], 0, 0))),
        out_shape=jax.ShapeDtypeStruct(blocks.shape, blocks.dtype),
        input_output_aliases={1: 0},
        name="place_own", compiler_params=_params(("arbitrary",)))(
            _scalar(_chip_index(lax.axis_index("x"), lax.axis_index("y"))), blocks, shard)


HBM = pl.BlockSpec(memory_space=pltpu.HBM)
SEM = pl.BlockSpec(memory_space=pltpu.SEMAPHORE)
TOKEN = jax.ShapeDtypeStruct((8, LANES), f32)
SPLIT_COPY = pltpu.CompilerParams(has_side_effects=pltpu.SideEffectType.DATAFLOW_SIDE_EFFECTING)


def _in_hbm(a):
    return pltpu.with_memory_space_constraint(a, pltpu.HBM)


def _full_shape(shard, axis):
    rows, cols = shard.shape
    return (N_CHIPS * rows, cols) if axis == 0 else (rows, N_CHIPS * cols)


def _block_half(ref, axis, j, h):
    if axis == 0:
        rows = ref.shape[0] // N_CHIPS
        return ref.at[pl.ds(pl.multiple_of(j * rows + h * (rows // 2), 16), rows // 2), :]
    half, cols = ref.shape[0] // 2, ref.shape[1] // N_CHIPS
    return ref.at[pl.ds(pl.multiple_of(h * half, 16), half), pl.ds(pl.multiple_of(j * cols, LANES), cols)]


def _place_layer(shards, axes, after):
    n = len(shards)

    def body(me_ref, *refs):
        ins, outs = refs[n:2 * n], refs[2 * n + 1:]
        for f in range(n):
            outs[f][...] = ins[f][...].astype(bf16)

    lands = [lax.empty(_full_shape(s, ax), bf16) for s, ax in zip(shards, axes)]
    return pl.pallas_call(
        body,
        grid_spec=pltpu.PrefetchScalarGridSpec(
            num_scalar_prefetch=1, grid=(1,),
            in_specs=[ANY] * n + [pl.BlockSpec(s.shape, lambda i, me: (0, 0)) for s in shards] + [ANY],
            out_specs=[pl.BlockSpec(s.shape, (lambda i, me: (me[0], 0)) if ax == 0 else (lambda i, me: (0, me[0])))
                       for s, ax in zip(shards, axes)]),
        out_shape=[jax.ShapeDtypeStruct(a.shape, bf16) for a in lands],
        input_output_aliases={1 + f: f for f in range(n)},
        name="place_layer", compiler_params=_params(("arbitrary",)))(
            _scalar(_chip_index(lax.axis_index("x"), lax.axis_index("y"))), *lands, *shards, after)


def _gather_start(lands, axes, after):
    return _split_copy_start("gather", _gather_plan(axes), 3 * len(lands), [], lands, after)


def _gather_wait(state, axes, after):
    return _split_copy_wait("gather", _gather_plan(axes), state, after)


SIBLING_PAIR_ID = 0


def _split_copy_start(name, plan, count, ins, lands, after, sibling_only=False):
    arrays = list(ins) + list(lands)
    n_in, n = len(ins), len(arrays)

    def body(*refs):
        send, recv, token = refs[n + 1], refs[n + 2], refs[-1]
        if sibling_only:
            x, y, c, _ = _place()
            barrier = pltpu.get_barrier_semaphore()
            pl.semaphore_signal(barrier, inc=1, device_id=(x, y, 1 - c), device_id_type=MESH)
            pl.semaphore_wait(barrier, 1)
        for i, (src, dst, _, peer) in enumerate(plan(refs[:n_in], refs[n_in:n])):
            _remote(src, dst, send.at[i], recv.at[i], peer).start()
        token[...] = jnp.zeros_like(token)

    effect = pltpu.SideEffectType.DATAFLOW_SIDE_EFFECTING
    outs = pl.pallas_call(
        body, name=name + "_start",
        in_specs=[HBM] * n + [ANY],
        out_specs=(SEM, SEM, *[HBM] * n, pl.BlockSpec(memory_space=pltpu.VMEM)),
        out_shape=(pltpu.SemaphoreType.DMA((count,)), pltpu.SemaphoreType.DMA((count,)),
                   *[pltpu.HBM(a.shape, a.dtype) for a in arrays], TOKEN),
        input_output_aliases={i: 2 + i for i in range(n)},
        compiler_params=pltpu.CompilerParams(has_side_effects=effect, collective_id=SIBLING_PAIR_ID) if sibling_only
        else SPLIT_COPY)(*[_in_hbm(a) for a in arrays], after)
    return (outs[0], outs[1], outs[2:2 + n_in], outs[2 + n_in:2 + n]), outs[-1]


def _split_copy_wait(name, plan, state, after):
    send_sems, recv_sems, ins, lands = state
    arrays = list(ins) + list(lands)
    n_in, n = len(ins), len(arrays)

    def body(*refs):
        send, recv, token = refs[n], refs[n + 1], refs[-1]
        for i, (src, _, landing, peer) in enumerate(plan(refs[:n_in], refs[n_in:n])):
            cp = _remote(src, landing, send.at[i], recv.at[i], peer)
            cp.wait_send()
            cp.wait_recv()
        token[...] = jnp.zeros_like(token)

    outs = pl.pallas_call(
        body, name=name + "_wait",
        in_specs=[HBM] * n + [SEM, SEM, ANY],
        out_specs=(*[HBM] * n, pl.BlockSpec(memory_space=pltpu.VMEM)),
        out_shape=(*[pltpu.HBM(a.shape, a.dtype) for a in arrays], TOKEN),
        input_output_aliases={i: i for i in range(n)},
        compiler_params=SPLIT_COPY)(*arrays, send_sems, recv_sems, after)
    return outs[:n_in], outs[n_in:n], outs[-1]


def _gather_plan(axes):
    def plan(ins, lnd):
        x, y, c, others = _place()
        me = _chip_index(x, y)
        return [(_block_half(lnd[f], ax, me, c), _block_half(lnd[f], ax, me, c),
                 _block_half(lnd[f], ax, _chip_index(cx, cy), c), (cx, cy, c))
                for f, ax in enumerate(axes) for cx, cy in others]
    return plan


def _pair_plan(axes):
    def plan(ins, lnd):
        x, y, c, _ = _place()
        return [(_block_half(ins[f], ax, j, 1 - c), lnd[f].at[j], lnd[f].at[j], (x, y, 1 - c))
                for f, ax in enumerate(axes) for j in range(N_CHIPS)]
    return plan


def _scatter_plan(ins, lnd):
    x, y, c, others = _place()
    return [(ins[f].at[_chip_index(cx, cy)], lnd[f].at[k], lnd[f].at[k], (cx, cy, c))
            for f in range(len(ins)) for k, (cx, cy) in enumerate(others)]


def _join_plan(ins, lnd):
    x, y, c, _ = _place()
    return [(lnd[f].at[_half(lnd[f].shape[0], c)], lnd[f].at[_half(lnd[f].shape[0], c)],
             lnd[f].at[_half(lnd[f].shape[0], 1 - c)], (x, y, 1 - c)) for f in range(len(lnd))]


def _gather_finish(lands, axes, after):
    n = len(lands)

    def body(*refs):
        outs = refs[n + 1:2 * n + 1]
        send, recv = refs[2 * n + 1:]
        x, y, c, others = _place()
        sib = (x, y, 1 - c)
        barrier = pltpu.get_barrier_semaphore()
        pl.semaphore_signal(barrier, inc=1, device_id=sib, device_id_type=MESH)
        pl.semaphore_wait(barrier, 1)
        sends = []
        for f in range(n):
            for k, (cx, cy) in enumerate(others):
                landed = _block_half(outs[f], axes[f], _chip_index(cx, cy), c)
                cp = _remote(landed, landed, send.at[3 * f + k], recv.at[3 * f + k], sib)
                cp.start()
                sends.append(cp)
        for f in range(n):
            for k, (cx, cy) in enumerate(others):
                passed = _block_half(outs[f], axes[f], _chip_index(cx, cy), 1 - c)
                _remote(passed, passed, send.at[3 * f + k], recv.at[3 * f + k], sib).wait_recv()
        for cp in sends:
            cp.wait_send()

    return pl.pallas_call(
        body, in_specs=[ANY] * (n + 1), out_specs=[ANY] * n,
        out_shape=[jax.ShapeDtypeStruct(a.shape, a.dtype) for a in lands],
        input_output_aliases={f: f for f in range(n)},
        scratch_shapes=[pltpu.SemaphoreType.DMA((3 * n,)), pltpu.SemaphoreType.DMA((3 * n,))],
        compiler_params=pltpu.CompilerParams(collective_id=SIBLING_PAIR_ID),
        name="gather_finish")(*lands, after)


def _half_blocks(part, axis):
    rows, cols = (part.shape[0] // N_CHIPS, part.shape[1]) if axis == 0 else (part.shape[0], part.shape[1] // N_CHIPS)
    return lax.empty((N_CHIPS, rows // 2, cols), part.dtype)


def _add_pair_layer(parts, gots, axes):
    k = len(parts)

    def body(c_ref, *refs):
        for f in range(k):
            a_ref, b_ref, o_ref = refs[2 * f], refs[2 * f + 1], refs[2 * k + f]
            o_ref[...] = (a_ref[...].astype(f32) + b_ref[...].astype(f32)).astype(o_ref.dtype)

    in_specs, out_specs, operands = [], [], []
    for part, got, axis in zip(parts, gots, axes):
        _, half, cols = got.shape
        if axis == 0:
            part = part.reshape(N_CHIPS, 2, half, cols)
            mine = pl.BlockSpec((None, None, half, cols), lambda j, c: (j, c[0], 0, 0))
        else:
            mine = pl.BlockSpec((half, cols), lambda j, c: (c[0], j))
        block = pl.BlockSpec((None, half, cols), lambda j, c: (j, 0, 0))
        in_specs += [mine, block]
        out_specs.append(block)
        operands += [part, got]
    return pl.pallas_call(
        body,
        grid_spec=pltpu.PrefetchScalarGridSpec(num_scalar_prefetch=1, grid=(N_CHIPS,), in_specs=in_specs, out_specs=out_specs),
        out_shape=[jax.ShapeDtypeStruct(g.shape, p.dtype) for p, g in zip(parts, gots)],
        name="add_pair_layer", compiler_params=_params(("arbitrary",)))(_scalar(lax.axis_index("c")), *operands)


def _scatter_start(sums, after):
    lands = [lax.empty((3,) + s.shape[1:], s.dtype) for s in sums]
    return _split_copy_start("scatter", _scatter_plan, 3 * len(sums), sums, lands, after)


def _scatter_wait(state, after):
    return _split_copy_wait("scatter", _scatter_plan, state, after)


def _add_slots(chip_sums, slots):
    k = len(chip_sums)

    def body(at_ref, *refs):
        for f in range(k):
            own_ref, s_ref, o_ref = refs[2 * f], refs[2 * f + 1], refs[2 * k + f]
            acc = own_ref[...].astype(f32)
            for j in range(3):
                acc = acc + s_ref[j].astype(f32)
            o_ref[...] = acc

    in_specs, out_specs, operands = [], [], []
    for cs, s in zip(chip_sums, slots):
        _, half, cols = cs.shape
        in_specs += [pl.BlockSpec((None, half, cols), lambda i, at: (at[0], 0, 0)), pl.BlockSpec((3, half, cols), lambda i, at: (0, 0, 0))]
        out_specs.append(pl.BlockSpec((None, half, cols), lambda i, at: (at[1], 0, 0)))
        operands += [cs, s]
    at = jnp.concatenate([_scalar(_chip_index(lax.axis_index("x"), lax.axis_index("y"))), _scalar(lax.axis_index("c"))])
    outs = pl.pallas_call(
        body,
        grid_spec=pltpu.PrefetchScalarGridSpec(num_scalar_prefetch=1, grid=(1,), in_specs=in_specs, out_specs=out_specs),
        out_shape=[jax.ShapeDtypeStruct((2,) + cs.shape[1:], f32) for cs in chip_sums],
        name="add_slots", compiler_params=_params(("arbitrary",)))(at, *operands)
    return [o.reshape(2 * o.shape[1], o.shape[2]) for o in outs]


def _adamw_math(w, grad, m, v):
    nm = ADAM_B1 * m + (1.0 - ADAM_B1) * grad
    nv = ADAM_B2 * v + (1.0 - ADAM_B2) * (grad * grad)
    m_hat = nm / (1.0 - ADAM_B1 ** ADAM_STEP)
    v_hat = nv / (1.0 - ADAM_B2 ** ADAM_STEP)
    return nm, nv, -ADAM_LR * (m_hat / (jnp.sqrt(v_hat) + ADAM_EPS) + ADAM_WD * w)


def _adamw_small(ws, gs, ms, vs):
    k = len(ws)

    def body(*refs):
        for f in range(k):
            w_ref, g_ref, m_ref, v_ref = refs[4 * f:4 * f + 4]
            d_ref, nm_ref, nv_ref = refs[4 * k + 3 * f:4 * k + 3 * f + 3]
            nm, nv, step = _adamw_math(w_ref[...], g_ref[...], m_ref[...], v_ref[...])
            d_ref[...] = step
            nm_ref[...] = nm
            nv_ref[...] = nv

    whole = pl.BlockSpec(memory_space=pltpu.VMEM)
    res = pl.pallas_call(
        body, in_specs=[whole] * (4 * k), out_specs=[whole] * (3 * k),
        out_shape=[jax.ShapeDtypeStruct(w.shape, f32) for w in ws for _ in range(3)],
        name="adamw_small", compiler_params=_params())(*[a for four in zip(ws, gs, ms, vs) for a in four])
    return [res[3 * f:3 * f + 3] for f in range(k)]


def _adamw_layer(l, ws, ms, vs, gs, outs, steps, after):
    k = len(ws)

    def body(*refs):
        ins, new = refs[:4 * k], refs[8 * k + 1:]
        for f in range(k):
            w_ref, m_ref, v_ref, g_ref = ins[4 * f:4 * f + 4]
            go_ref, d_ref, nm_ref, nv_ref = new[4 * f:4 * f + 4]
            grad = g_ref[...]
            nm, nv, step = _adamw_math(w_ref[...], grad, m_ref[...], v_ref[...])
            go_ref[...] = grad
            d_ref[...] = step
            nm_ref[...] = nm
            nv_ref[...] = nv

    in_specs, out_specs, operands = [], [], []
    for w, m, v, g in zip(ws, ms, vs, gs):
        _, rows, cols = w.shape
        tile = rows // steps
        layer = pl.BlockSpec((None, tile, cols), lambda i: (l, i, 0))
        in_specs += [layer] * 3 + [_rows(cols, tile)]
        out_specs += [layer] * 4
        operands += [w, m, v, g]
    flat_outs = [o for four in outs for o in four]
    res = pl.pallas_call(
        body, grid=(steps,),
        in_specs=in_specs + [ANY] * (4 * k + 1), out_specs=out_specs,
        out_shape=[jax.ShapeDtypeStruct(o.shape, f32) for o in flat_outs],
        input_output_aliases={4 * k + j: j for j in range(4 * k)},
        name="adamw_layer", compiler_params=_params(("arbitrary",)))(*operands, *flat_outs, after)
    return [res[4 * f:4 * f + 4] for f in range(k)]


SMALL = ("w_conv", "w_pool", "pool_scale", "sgu_ln_g", "w_spatial", "b_spatial", "ln1_g", "ln1_b", "ln2_g", "ln2_b")
WEIGHTS = ("w_in", "w_conv", "w_pool", "pool_scale", "sgu_ln_g", "w_spatial", "b_spatial", "w_o", "ln1_g", "ln1_b",
           "w_gate_up", "w_down", "ln2_g", "ln2_b")
BIG = ("w_in", "w_o", "w_gate_up", "w_down")
GROUPS = (("w_in", "w_o"), ("w_gate_up", "w_down"))
GROUP_AXES = ((0, 0), (1, 0))
SCATTER_HOOKS = 2
ADAMW_STEPS = (2, 4)
SMALL_LAYER_ROWS = 1024


def _pack_layer(arrays):
    flat = jnp.concatenate([a.reshape(-1) for a in arrays])
    return jnp.pad(flat, (0, SMALL_LAYER_ROWS * LANES - flat.shape[0])).reshape(SMALL_LAYER_ROWS, LANES)


def _unpack_layers(flat, shapes):
    out, at = {}, 0
    for name, shape in shapes.items():
        size = 1
        for d in shape:
            size *= d
        out[name] = flat[:, at:at + size].reshape((flat.shape[0],) + tuple(shape))
        at += size
    return out


def kernel(x, w_in, w_conv, w_pool, pool_scale, sgu_ln_g, w_spatial, b_spatial, w_o, ln1_g, ln1_b, w_gate_up, w_down, ln2_g, ln2_b, loss_target, m_w_in, m_w_conv, m_w_pool, m_pool_scale, m_sgu_ln_g, m_w_spatial, m_b_spatial, m_w_o, m_ln1_g, m_ln1_b, m_w_gate_up, m_w_down, m_ln2_g, m_ln2_b, v_w_in, v_w_conv, v_w_pool, v_pool_scale, v_sgu_ln_g, v_w_spatial, v_b_spatial, v_w_o, v_ln1_g, v_ln1_b, v_w_gate_up, v_w_down, v_ln2_g, v_ln2_b):
    weights = dict(w_in=w_in, w_conv=w_conv, w_pool=w_pool, pool_scale=pool_scale, sgu_ln_g=sgu_ln_g, w_spatial=w_spatial,
                   b_spatial=b_spatial, w_o=w_o, ln1_g=ln1_g, ln1_b=ln1_b, w_gate_up=w_gate_up, w_down=w_down, ln2_g=ln2_g, ln2_b=ln2_b)
    m_in = dict(w_in=m_w_in, w_conv=m_w_conv, w_pool=m_w_pool, pool_scale=m_pool_scale, sgu_ln_g=m_sgu_ln_g, w_spatial=m_w_spatial,
                b_spatial=m_b_spatial, w_o=m_w_o, ln1_g=m_ln1_g, ln1_b=m_ln1_b, w_gate_up=m_w_gate_up, w_down=m_w_down,
                ln2_g=m_ln2_g, ln2_b=m_ln2_b)
    v_in = dict(w_in=v_w_in, w_conv=v_w_conv, w_pool=v_w_pool, pool_scale=v_pool_scale, sgu_ln_g=v_sgu_ln_g, w_spatial=v_w_spatial,
                b_spatial=v_b_spatial, w_o=v_w_o, ln1_g=v_ln1_g, ln1_b=v_ln1_b, w_gate_up=v_w_gate_up, w_down=v_w_down,
                ln2_g=v_ln2_g, ln2_b=v_ln2_b)
    depth = w_in.shape[0]
    conv_cols = w_conv.shape[2]
    chip = _chip_index(lax.axis_index("x"), lax.axis_index("y"))

    big_w = dict(w_in=jnp.swapaxes(w_in, 1, 2), w_o=w_o, w_gate_up=w_gate_up, w_down=w_down)
    big_m = dict(w_in=jnp.swapaxes(m_w_in, 1, 2), w_o=m_w_o, w_gate_up=m_w_gate_up, w_down=m_w_down)
    big_v = dict(w_in=jnp.swapaxes(v_w_in, 1, 2), w_o=v_w_o, w_gate_up=v_w_gate_up, w_down=v_w_down)

    def place(l, g, after):
        return _place_layer([big_w[n][l] for n in GROUPS[g]], GROUP_AXES[g], after)

    def send(l, g, after):
        return _gather_start(placed[l, g], GROUP_AXES[g], after)

    stages = [(l, g) for l in range(depth) for g in (0, 1)]
    placed, flights = {}, {}
    conv_flat = jnp.pad(w_conv.reshape(-1), (0, 16 * LANES - w_conv.size)).reshape(1, 16, LANES)
    conv_full = _gather_shards([conv_flat], x)[0].reshape(N_CHIPS, 16 * LANES)[:, :w_conv.size]
    conv_full = conv_full.reshape(N_CHIPS, depth, 3, conv_cols).transpose(1, 2, 0, 3).reshape(depth, 3, N_CHIPS * conv_cols)
    token = conv_full
    for st in stages[:2]:
        placed[st] = place(*st, token)
        flights[st], token = send(*st, token)
    recent = token
    for st in stages[2:]:
        placed[st] = place(*st, token)
        recent = placed[st][0]
    act = x[0]
    layers, saved = [], []
    for i, (l, g) in enumerate(stages):
        if g == 0:
            w = dict(w_conv=conv_full[l], w_pool=w_pool[l], pool_scale=pool_scale[l][None], sgu_ln_g=sgu_ln_g[l][None],
                     w_spatial=w_spatial[l], b_spatial=b_spatial[l][:, :, None], ln1_g=ln1_g[l][None], ln1_b=ln1_b[l][None],
                     ln2_g=ln2_g[l][None], ln2_b=ln2_b[l][None])
        _, lands, token = _gather_wait(flights[l, g], GROUP_AXES[g], recent)
        if i + 2 < len(stages):
            flights[stages[i + 2]], token = send(*stages[i + 2], token)
        w.update(zip(GROUPS[g], _gather_finish(lands, GROUP_AXES[g], token)))
        if g == 0:
            sv = _fwd_mix(act, w, token)
            recent = sv["xhat1"]
        else:
            act = recent = _fwd_mlp(sv, w, token)
            layers.append(w)
            saved.append(sv)

    big_outs = {n: [lax.empty(big_w[n].shape, f32) for _ in range(4)] for n in BIG}
    small_sums = [None] * depth
    pending, updates = [], []
    latest = dict(token=None)

    def begin(l, g, parts):
        axes = GROUP_AXES[g] + (0,) * (len(parts) - len(GROUPS[g]))
        lands = [_half_blocks(p, ax) for p, ax in zip(parts, axes)]
        flight, latest["token"] = _split_copy_start("pair", _pair_plan(axes), N_CHIPS * len(parts), parts, lands, latest["token"],
                                                    sibling_only=True)
        pending.append(dict(l=l, g=g, axes=axes, step="pair", age=0, flight=flight))

    def advance(st, recent):
        if st["step"] == "pair":
            parts, got, _ = _split_copy_wait("pair", _pair_plan(st["axes"]), st["flight"], recent)
            sums = _add_pair_layer(parts, got, st["axes"])
            st["flight"], latest["token"] = _scatter_start(sums, latest["token"])
            st["step"] = "scatter"
        elif st["step"] == "scatter":
            sums, slots, _ = _scatter_wait(st["flight"], recent)
            filled = _add_slots(sums, slots)
            st["flight"], latest["token"] = _split_copy_start("join", _join_plan, len(filled), [], filled, latest["token"],
                                                              sibling_only=True)
            st["step"] = "join"
        else:
            _, summed, _ = _split_copy_wait("join", _join_plan, st["flight"], recent)
            updates.append((st["l"], st["g"], summed[:len(GROUPS[st["g"]])]))
            if st["g"] == 0:
                small_sums[st["l"]] = summed[-1]
            st["step"] = "done"
        st["age"] = 0

    def hook(recent):
        for st in reversed(list(pending)):
            st["age"] += 1
            if st["age"] >= SCATTER_HOOKS or st["step"] != "scatter":
                advance(st, recent)
                if st["step"] == "done":
                    pending.remove(st)
        return latest["token"]

    def update(count, recent):
        for l, g, totals in updates[:count]:
            names = GROUPS[g]
            new = _adamw_layer(l, [big_w[n] for n in names], [big_m[n] for n in names], [big_v[n] for n in names], totals,
                               [big_outs[n] for n in names], ADAMW_STEPS[g], latest["token"])
            big_outs.update(zip(names, new))
            recent = new[-1][1]
        del updates[:count]
        return recent

    grad_x, sq = _loss_head(act, loss_target[0])
    latest["token"] = sq
    grads = [None] * depth
    for l in reversed(range(depth)):
        dz, g_mlp = _bwd_mlp(grad_x, layers[l], saved[l], latest["token"], hook)
        hook(g_mlp["w_down"])
        begin(l, 1, [g_mlp[n] for n in GROUPS[1]])
        grad_x, g_mix = _bwd_mix(dz, layers[l], saved[l], latest["token"], hook)
        grads[l] = dict(g_mlp, **g_mix)
        hook(g_mix["w_o"])
        begin(l, 0, [g_mix[n] for n in GROUPS[0]] + [_pack_layer([grads[l][n] for n in SMALL])])
    recent = g_mix["w_o"]
    while pending:
        recent = update(-(-3 * len(updates) // 4), recent)
        hook(recent)
    update(len(updates), recent)
    loss = lax.psum(0.5 / D_MODEL * jnp.sum(sq), ("x", "y", "c"))

    small_sum = _gather_shards([jnp.stack(small_sums)], recent)[0].reshape(depth, SMALL_LAYER_ROWS * LANES)
    grad = {n: [jnp.swapaxes(o, 1, 2) for o in big_outs[n]] if n == "w_in" else big_outs[n] for n in BIG}
    delta = {n: o[1] for n, o in grad.items()}
    new_m = {n: o[2] for n, o in grad.items()}
    new_v = {n: o[3] for n, o in grad.items()}
    grad = {n: o[0] for n, o in grad.items()}
    grad.update(_unpack_layers(small_sum, {n: (3, N_CHIPS * conv_cols) if n == "w_conv" else weights[n].shape[1:] for n in SMALL}))
    grad["w_conv"] = lax.dynamic_slice_in_dim(grad["w_conv"], chip * conv_cols, conv_cols, axis=2)

    results = _adamw_small(*[[src[n] for n in SMALL] for src in (weights, grad, m_in, v_in)])
    for n, (step, moment1, moment2) in zip(SMALL, results):
        delta[n], new_m[n], new_v[n] = step, moment1, moment2

    return (loss, grad_x[None], *[grad[n] for n in WEIGHTS], *[delta[n] for n in WEIGHTS],
            *[new_m[n] for n in WEIGHTS], *[new_v[n] for n in WEIGHTS])
```

```python
import jax
import jax.numpy as jnp
from jax import lax
from jax.experimental import pallas as pl
from jax.experimental.pallas import tpu as pltpu

f32 = jnp.float32
bf16 = jnp.bfloat16

D_MODEL = 1024
DEPTH = 4
CONV_W = 384
POOL_W = 256
SGU_W = 384
IN_W = 3 * CONV_W + POOL_W + 2 * SGU_W
D_FF = 2816
CHUNK = 128
HEAD = 64
POOL_WINDOWS = (2, 4, 8, 16)
ALPHA = float((2 * DEPTH) ** 0.25)
LN_EPS = 1e-5
ADAM_LR = 0.001
ADAM_B1 = 0.9
ADAM_B2 = 0.999
ADAM_EPS = 1e-08
ADAM_WD = 0.01
ADAM_STEP = 10

LANES = 128
TOKEN_TILE = 256
N_CHIPS = 4
VMEM_LIMIT = 56 * 1024 * 1024

BLK_XA, BLK_GB, BLK_GC, BLK_P, BLK_U, BLK_V = 0, 3, 6, 9, 11, 14

MESH = pl.DeviceIdType.MESH


def _params(sem=None):
    return pltpu.CompilerParams(dimension_semantics=sem, vmem_limit_bytes=VMEM_LIMIT)


def _rows(width, tile=TOKEN_TILE):
    return pl.BlockSpec((tile, width), lambda i: (i, 0))


def _resident(shape):
    zeros = (0,) * len(shape)
    return pl.BlockSpec(shape, lambda *_: zeros, pipeline_mode=pl.Buffered(1))


def _nt(a, b):
    return lax.dot_general(a, b, (((1,), (1,)), ((), ())), preferred_element_type=f32)


def _tn(a, b):
    return lax.dot_general(a, b, (((0,), (0,)), ((), ())), preferred_element_type=f32)


def _mm(a, b):
    return jnp.dot(a, b, preferred_element_type=f32)


def _norm_fwd(z):
    mu = jnp.mean(z, axis=-1, keepdims=True)
    zc = z - mu
    var = jnp.mean(zc * zc, axis=-1, keepdims=True)
    rstd = lax.rsqrt(var + LN_EPS)
    return zc * rstd, rstd


def _norm_bwd(dxhat, xhat, rstd):
    m1 = jnp.mean(dxhat, axis=-1, keepdims=True)
    m2 = jnp.mean(dxhat * xhat, axis=-1, keepdims=True)
    return rstd * (dxhat - m1 - xhat * m2)


def _proj(x, w_in_b, after):
    s = x.shape[0]

    def body(x_ref, w_ref, after_ref, p_ref, xb_ref):
        xb = x_ref[...].astype(bf16)
        xb_ref[...] = xb
        p_ref[...] = _nt(xb, w_ref[...])

    return pl.pallas_call(
        body, grid=(s // TOKEN_TILE,),
        in_specs=[_rows(D_MODEL), _resident((IN_W, D_MODEL)), pl.BlockSpec(memory_space=pl.ANY)],
        out_specs=[_rows(IN_W), _rows(D_MODEL)],
        out_shape=[jax.ShapeDtypeStruct((s, IN_W), f32), jax.ShapeDtypeStruct((s, D_MODEL), bf16)],
        name="proj", compiler_params=_params(("arbitrary",)))(x, w_in_b, after)


def _row_ranges(parts):
    out, at = [], 0
    for p in parts:
        out.append((at, at + p.shape[1]))
        at += p.shape[1]
    return out


def _wo_ln1(mix, x, w_o_b, g, b):
    s = x.shape[0]
    n = len(mix)
    ranges = _row_ranges(mix)

    def body(*refs):
        m_refs = refs[:n]
        x_ref, w_ref, g_ref, b_ref, xhat_ref, rstd_ref, hb_ref = refs[n:]
        z = ALPHA * x_ref[...]
        for m_ref, (lo, hi) in zip(m_refs, ranges):
            z = z + _mm(m_ref[...], w_ref[lo:hi, :])
        xhat, rstd = _norm_fwd(z)
        xhat_ref[...] = xhat
        rstd_ref[...] = rstd
        hb_ref[...] = (xhat * g_ref[...] + b_ref[...]).astype(bf16)

    return pl.pallas_call(
        body, grid=(s // TOKEN_TILE,),
        in_specs=[_rows(m.shape[1]) for m in mix] + [_rows(D_MODEL), _resident((D_MODEL, D_MODEL)), _resident((1, D_MODEL)),
                                                     _resident((1, D_MODEL))],
        out_specs=[_rows(D_MODEL), _rows(1), _rows(D_MODEL)],
        out_shape=[jax.ShapeDtypeStruct((s, D_MODEL), f32), jax.ShapeDtypeStruct((s, 1), f32),
                   jax.ShapeDtypeStruct((s, D_MODEL), bf16)],
        name="wo_ln1", compiler_params=_params(("arbitrary",)))(*mix, x, w_o_b, g, b)


def _mlp_fwd(xhat1, g1, b1, w_gu_b, w_down_b, g2, b2, after):
    s = xhat1.shape[0]

    def body(xh_ref, g1_ref, b1_ref, wgu_ref, wd_ref, g2_ref, b2_ref, after_ref, gu_ref, xhat2_ref, rstd2_ref, y_ref):
        h = xh_ref[...] * g1_ref[...] + b1_ref[...]
        gu = _mm(h.astype(bf16), wgu_ref[...])
        gu_ref[...] = gu
        gate = gu[:, :D_FF]
        act = gate * jax.nn.sigmoid(gate) * gu[:, D_FF:]
        z = ALPHA * h + _mm(act.astype(bf16), wd_ref[...])
        xhat2, rstd2 = _norm_fwd(z)
        xhat2_ref[...] = xhat2
        rstd2_ref[...] = rstd2
        y_ref[...] = xhat2 * g2_ref[...] + b2_ref[...]

    vec = _resident((1, D_MODEL))
    return pl.pallas_call(
        body, grid=(s // TOKEN_TILE,),
        in_specs=[_rows(D_MODEL), vec, vec, _resident((D_MODEL, 2 * D_FF)), _resident((D_FF, D_MODEL)), vec, vec,
                  pl.BlockSpec(memory_space=pl.ANY)],
        out_specs=[_rows(2 * D_FF), _rows(D_MODEL), _rows(1), _rows(D_MODEL)],
        out_shape=[jax.ShapeDtypeStruct((s, 2 * D_FF), f32), jax.ShapeDtypeStruct((s, D_MODEL), f32),
                   jax.ShapeDtypeStruct((s, 1), f32), jax.ShapeDtypeStruct((s, D_MODEL), f32)],
        name="mlp_fwd", compiler_params=_params(("arbitrary",)))(xhat1, g1, b1, w_gu_b, w_down_b, g2, b2, after)


def _loss_head(y, target):
    s = y.shape[0]

    def body(y_ref, t_ref, dy_ref, sq_ref):
        @pl.when(pl.program_id(0) == 0)
        def _():
            sq_ref[...] = jnp.zeros_like(sq_ref)

        e = y_ref[...] - t_ref[...]
        dy_ref[...] = e * (1.0 / D_MODEL)
        sq_ref[...] += jnp.sum(e * e, axis=0, keepdims=True)

    return pl.pallas_call(
        body, grid=(s // TOKEN_TILE,),
        in_specs=[_rows(D_MODEL), _rows(D_MODEL)],
        out_specs=[_rows(D_MODEL), pl.BlockSpec((1, D_MODEL), lambda i: (0, 0))],
        out_shape=[jax.ShapeDtypeStruct((s, D_MODEL), f32), jax.ShapeDtypeStruct((1, D_MODEL), f32)],
        name="loss_head", compiler_params=_params(("arbitrary",)))(y, target)


def _mlp_bwd(dy, xhat2, rstd2, g2, gu, w_gu_b, w_down_b, xhat1, rstd1, g1, w_o_b, after):
    s = dy.shape[0]

    def body(dy_ref, xh_ref, rs_ref, g2_ref, gu_ref, wgu_ref, wd_ref, xh1_ref, rs1_ref, g1_ref, wo_ref, after_ref,
             dz_ref, act_ref, dgu_ref, dz1_ref, dz1b_ref, dm_ref, gg_ref, gb_ref, gg1_ref, gb1_ref):
        @pl.when(pl.program_id(0) == 0)
        def _():
            for ref in (gg_ref, gb_ref, gg1_ref, gb1_ref):
                ref[...] = jnp.zeros_like(ref)

        dy_t = dy_ref[...]
        xhat = xh_ref[...]
        gg_ref[...] += jnp.sum(dy_t * xhat, axis=0, keepdims=True)
        gb_ref[...] += jnp.sum(dy_t, axis=0, keepdims=True)
        dz = _norm_bwd(dy_t * g2_ref[...], xhat, rs_ref[...])
        dzb = dz.astype(bf16)
        dz_ref[...] = dzb
        dact = _nt(dzb, wd_ref[...])
        gate = gu_ref[:, :D_FF]
        up = gu_ref[:, D_FF:]
        sg = jax.nn.sigmoid(gate)
        silu = gate * sg
        act_ref[...] = (silu * up).astype(bf16)
        dgu_ref[:, :D_FF] = (dact * up * (sg * (1.0 + gate * (1.0 - sg)))).astype(bf16)
        dgu_ref[:, D_FF:] = (dact * silu).astype(bf16)
        dh = ALPHA * dz + _nt(dgu_ref[...], wgu_ref[...])
        xhat1 = xh1_ref[...]
        gg1_ref[...] += jnp.sum(dh * xhat1, axis=0, keepdims=True)
        gb1_ref[...] += jnp.sum(dh, axis=0, keepdims=True)
        dz1 = _norm_bwd(dh * g1_ref[...], xhat1, rs1_ref[...])
        dz1_ref[...] = dz1
        dz1b = dz1.astype(bf16)
        dz1b_ref[...] = dz1b
        dm_ref[...] = _nt(dz1b, wo_ref[...])

    vec, vec_out = _resident((1, D_MODEL)), pl.BlockSpec((1, D_MODEL), lambda i: (0, 0))
    tokens_f32, tokens_bf16 = jax.ShapeDtypeStruct((s, D_MODEL), f32), jax.ShapeDtypeStruct((s, D_MODEL), bf16)
    sums = jax.ShapeDtypeStruct((1, D_MODEL), f32)
    return pl.pallas_call(
        body, grid=(s // TOKEN_TILE,),
        in_specs=[_rows(D_MODEL), _rows(D_MODEL), _rows(1), vec, _rows(2 * D_FF),
                  _resident((D_MODEL, 2 * D_FF)), _resident((D_FF, D_MODEL)), _rows(D_MODEL), _rows(1), vec,
                  _resident((D_MODEL, D_MODEL)), pl.BlockSpec(memory_space=pl.ANY)],
        out_specs=[_rows(D_MODEL), _rows(D_FF), _rows(2 * D_FF), _rows(D_MODEL), _rows(D_MODEL), _rows(D_MODEL),
                   vec_out, vec_out, vec_out, vec_out],
        out_shape=[tokens_bf16, jax.ShapeDtypeStruct((s, D_FF), bf16), jax.ShapeDtypeStruct((s, 2 * D_FF), bf16),
                   tokens_f32, tokens_bf16, tokens_f32, sums, sums, sums, sums],
        name="mlp_bwd", compiler_params=_params(("arbitrary",)))(
            dy, xhat2, rstd2, g2, gu, w_gu_b, w_down_b, xhat1, rstd1, g1, w_o_b, after)


def _dx(dz1, dparts, w_in_t, after):
    s = dz1.shape[0]
    n = len(dparts)
    ranges = _row_ranges(dparts)

    def body(*refs):
        d_refs = refs[:n]
        dz_ref, w_ref, _, dx_ref = refs[n:]
        acc = ALPHA * dz_ref[...]
        for d_ref, (lo, hi) in zip(d_refs, ranges):
            acc = acc + _mm(d_ref[...], w_ref[lo:hi, :])
        dx_ref[...] = acc

    return pl.pallas_call(
        body, grid=(s // TOKEN_TILE,),
        in_specs=[_rows(d.shape[1]) for d in dparts] + [_rows(D_MODEL), _resident((IN_W, D_MODEL)),
                                                        pl.BlockSpec(memory_space=pl.ANY)],
        out_specs=_rows(D_MODEL),
        out_shape=jax.ShapeDtypeStruct((s, D_MODEL), f32),
        name="dx", compiler_params=_params(("arbitrary",)))(*dparts, dz1, w_in_t, after)


def _weight_grad_rows(parts, b, bn):
    s, n_cols = b.shape
    n = len(parts)
    ranges = _row_ranges(parts)
    m = ranges[-1][1]

    def body(*refs):
        p_refs = refs[:n]
        b_ref, o_ref = refs[n:]
        for p_ref, (lo, hi) in zip(p_refs, ranges):
            o_ref[lo:hi, :] = _tn(p_ref[...], b_ref[...]).astype(bf16)

    return pl.pallas_call(
        body, grid=(n_cols // bn,),
        in_specs=[_resident(p.shape) for p in parts] + [pl.BlockSpec((s, bn), lambda j: (0, j))],
        out_specs=pl.BlockSpec((m, bn), lambda j: (0, j)),
        out_shape=jax.ShapeDtypeStruct((m, n_cols), bf16),
        name="weight_grad_rows", compiler_params=_params(("arbitrary",)))(*parts, b)


def _weight_grad(a, b, bm, bn, after):
    s, m = a.shape
    n = b.shape[1]

    def body(a_ref, b_ref, after_ref, o_ref):
        o_ref[...] = _tn(a_ref[...], b_ref[...]).astype(bf16)

    return pl.pallas_call(
        body, grid=(m // bm, n // bn),
        in_specs=[pl.BlockSpec((s, bm), lambda i, j: (0, i)), pl.BlockSpec((s, bn), lambda i, j: (0, j)),
                  pl.BlockSpec(memory_space=pl.ANY)],
        out_specs=pl.BlockSpec((bm, bn), lambda i, j: (i, j)),
        out_shape=jax.ShapeDtypeStruct((m, n), bf16),
        name="weight_grad", compiler_params=_params(("arbitrary", "arbitrary")))(a, b, after)


def _shift_down(a, k):
    row = lax.broadcasted_iota(jnp.int32, a.shape, 0)
    return jnp.where(row >= k, pltpu.roll(a, k, 0), 0.0)


def _shift_up(a, k):
    n = a.shape[0]
    row = lax.broadcasted_iota(jnp.int32, a.shape, 0)
    return jnp.where(row < n - k, pltpu.roll(a, n - k, 0), 0.0)


def _slab(s, block):
    return pl.BlockSpec((s, LANES), lambda k: (0, block + k))


def _conv_y(z, w):
    return w[0:1, :] * _shift_down(z, 2) + w[1:2, :] * _shift_down(z, 1) + w[2:3, :] * z


def _conv_fwd(proj, w_conv):
    s = proj.shape[0]

    def body(xa_ref, gb_ref, gc_ref, w_ref, o_ref):
        z = gc_ref[...] * xa_ref[...]
        o_ref[...] = (gb_ref[...] * _conv_y(z, w_ref[...])).astype(bf16)

    return pl.pallas_call(
        body, grid=(CONV_W // LANES,),
        in_specs=[_slab(s, BLK_XA), _slab(s, BLK_GB), _slab(s, BLK_GC), pl.BlockSpec((3, LANES), lambda k: (0, k))],
        out_specs=_slab(s, 0),
        out_shape=jax.ShapeDtypeStruct((s, CONV_W), bf16),
        name="conv_fwd", compiler_params=_params(("arbitrary",)))(proj, proj, proj, w_conv)


def _conv_bwd(proj, dmix, w_conv, after):
    s = proj.shape[0]

    def body(xa_ref, gb_ref, gc_ref, dy_ref, w_ref, after_ref, dxa_ref, dgb_ref, dgc_ref, dw_ref):
        xa = xa_ref[...]
        gc = gc_ref[...]
        w = w_ref[...]
        z = gc * xa
        dya = dy_ref[...]
        dgb_ref[...] = (dya * _conv_y(z, w)).astype(bf16)
        dy = dya * gb_ref[...]
        dz = w[2:3, :] * dy + w[1:2, :] * _shift_up(dy, 1) + w[0:1, :] * _shift_up(dy, 2)
        dxa_ref[...] = (dz * gc).astype(bf16)
        dgc_ref[...] = (dz * xa).astype(bf16)
        dw_ref[0:1, :] = jnp.sum(dy * _shift_down(z, 2), axis=0, keepdims=True)
        dw_ref[1:2, :] = jnp.sum(dy * _shift_down(z, 1), axis=0, keepdims=True)
        dw_ref[2:3, :] = jnp.sum(dy * z, axis=0, keepdims=True)

    out = jax.ShapeDtypeStruct((s, CONV_W), bf16)
    return pl.pallas_call(
        body, grid=(CONV_W // LANES,),
        in_specs=[_slab(s, BLK_XA), _slab(s, BLK_GB), _slab(s, BLK_GC), _slab(s, 0), pl.BlockSpec((3, LANES), lambda k: (0, k)),
                  pl.BlockSpec(memory_space=pl.ANY)],
        out_specs=[_slab(s, 0), _slab(s, 0), _slab(s, 0), pl.BlockSpec((3, LANES), lambda k: (0, k))],
        out_shape=[out, out, out, jax.ShapeDtypeStruct((3, CONV_W), f32)],
        name="conv_bwd", compiler_params=_params(("arbitrary",)))(proj, proj, proj, dmix, w_conv, after)


def _pool_window(k):
    lane = lax.broadcasted_iota(jnp.int32, (1, LANES), 1)
    low = lane < HEAD
    first = k == 0
    wlen = jnp.where(low, jnp.where(first, POOL_WINDOWS[0], POOL_WINDOWS[2]), jnp.where(first, POOL_WINDOWS[1], POOL_WINDOWS[3]))
    return wlen, low, first


def _pool_diff(p, k):
    wlen, low, first = _pool_window(k)
    s2 = p + _shift_down(p, 1)
    s4 = s2 + _shift_down(s2, 2)
    s8 = s4 + _shift_down(s4, 4)
    s16 = s8 + _shift_down(s8, 8)
    win = jnp.where(low, jnp.where(first, s2, s8), jnp.where(first, s4, s16))
    row = lax.broadcasted_iota(jnp.int32, p.shape, 0)
    count = jnp.minimum(row + 1, wlen).astype(f32)
    return win / count - p, count


def _pool_weight(w_ref):
    zero = jnp.zeros((HEAD, HEAD), f32)
    top = jnp.concatenate([w_ref[0], zero], axis=1)
    bottom = jnp.concatenate([zero, w_ref[1]], axis=1)
    return jnp.concatenate([top, bottom], axis=0).astype(bf16)


def _pool_fwd(proj, w_pool, pool_scale):
    s = proj.shape[0]

    def body(p_ref, w_ref, sc_ref, o_ref):
        d, _ = _pool_diff(p_ref[...], pl.program_id(0))
        o_ref[...] = (_mm(d.astype(bf16), _pool_weight(w_ref)) * sc_ref[...]).astype(bf16)

    return pl.pallas_call(
        body, grid=(POOL_W // LANES,),
        in_specs=[_slab(s, BLK_P), pl.BlockSpec((2, HEAD, HEAD), lambda k: (k, 0, 0)), pl.BlockSpec((1, LANES), lambda k: (0, k))],
        out_specs=_slab(s, 0),
        out_shape=jax.ShapeDtypeStruct((s, POOL_W), bf16),
        name="pool_fwd", compiler_params=_params(("arbitrary",)))(proj, w_pool, pool_scale)


def _pool_bwd(proj, dmix, w_pool, pool_scale):
    s = proj.shape[0]

    def body(p_ref, dy_ref, w_ref, sc_ref, dp_ref, dw_ref, dsc_ref):
        k = pl.program_id(0)
        d, count = _pool_diff(p_ref[...], k)
        wbd = _pool_weight(w_ref)
        db = d.astype(bf16)
        dyb = dy_ref[...]
        dsc_ref[...] = jnp.sum(dyb * _mm(db, wbd), axis=0, keepdims=True)
        dpre = (dyb * sc_ref[...]).astype(bf16)
        dwbd = _tn(db, dpre)
        dw_ref[0] = dwbd[:HEAD, :HEAD]
        dw_ref[1] = dwbd[HEAD:, HEAD:]
        dd = _nt(dpre, wbd)
        e = dd / count
        wlen, low, first = _pool_window(k)
        a2 = e + _shift_up(e, 1)
        a4 = a2 + _shift_up(a2, 2)
        a8 = a4 + _shift_up(a4, 4)
        a16 = a8 + _shift_up(a8, 8)
        back = jnp.where(low, jnp.where(first, a2, a8), jnp.where(first, a4, a16))
        dp_ref[...] = (back - dd).astype(bf16)

    return pl.pallas_call(
        body, grid=(POOL_W // LANES,),
        in_specs=[_slab(s, BLK_P), _slab(s, CONV_W // LANES), pl.BlockSpec((2, HEAD, HEAD), lambda k: (k, 0, 0)),
                  pl.BlockSpec((1, LANES), lambda k: (0, k))],
        out_specs=[_slab(s, 0), pl.BlockSpec((2, HEAD, HEAD), lambda k: (k, 0, 0)), pl.BlockSpec((1, LANES), lambda k: (0, k))],
        out_shape=[jax.ShapeDtypeStruct((s, POOL_W), bf16), jax.ShapeDtypeStruct((4, HEAD, HEAD), f32),
                   jax.ShapeDtypeStruct((1, POOL_W), f32)],
        name="pool_bwd", compiler_params=_params(("arbitrary",)))(proj, dmix, w_pool, pool_scale)


SGU_UNROLL = 4
INV_SQRT2 = 0.7071067811865476
INV_SQRT_2PI = 0.3989422804014327


def _gelu(x):
    return 0.5 * x * (1.0 + lax.erf(x * INV_SQRT2))


def _gelu_grad(x):
    return 0.5 * (1.0 + lax.erf(x * INV_SQRT2)) + x * (INV_SQRT_2PI * jnp.exp(-0.5 * x * x))


def _head_mean(a, low):
    s_low = jnp.sum(jnp.where(low, a, 0.0), axis=-1, keepdims=True)
    s_high = jnp.sum(jnp.where(low, 0.0, a), axis=-1, keepdims=True)
    return jnp.where(low, s_low, s_high) * (1.0 / HEAD)


def _tril():
    r = lax.broadcasted_iota(jnp.int32, (CHUNK, CHUNK), 0)
    c = lax.broadcasted_iota(jnp.int32, (CHUNK, CHUNK), 1)
    return r >= c


def _sgu_chunk(up, vp, g, wm0, wm1, b0, b1, low):
    ug = _gelu(up)
    vg = _gelu(vp)
    vc = vg - _head_mean(vg, low)
    rstd = lax.rsqrt(_head_mean(vc * vc, low) + LN_EPS)
    vn = vc * rstd
    vb = (vn * g).astype(bf16)
    mixed = jnp.where(low, _mm(wm0, vb) + b0, _mm(wm1, vb) + b1)
    return ug, vn, rstd, vb, mixed


def _sgu_specs(s):
    return [_slab(s, BLK_U), _slab(s, BLK_V), pl.BlockSpec((1, LANES), lambda k: (0, k)),
            pl.BlockSpec((2, CHUNK, CHUNK), lambda k: (k, 0, 0)), pl.BlockSpec((2, CHUNK, 1), lambda k: (k, 0, 0))]


def _sgu_fwd(proj, sgu_g, w_spatial, b_spatial3):
    s = proj.shape[0]

    def body(u_ref, v_ref, g_ref, w_ref, b_ref, o_ref):
        low = lax.broadcasted_iota(jnp.int32, (1, LANES), 1) < HEAD
        mask = _tril()
        wm0 = jnp.where(mask, w_ref[0], 0.0).astype(bf16)
        wm1 = jnp.where(mask, w_ref[1], 0.0).astype(bf16)
        g = g_ref[...]
        b0 = b_ref[0]
        b1 = b_ref[1]

        def chunk(n, carry):
            rows = pl.ds(pl.multiple_of(n * CHUNK, CHUNK), CHUNK)
            ug, _, _, _, mixed = _sgu_chunk(u_ref[rows, :], v_ref[rows, :], g, wm0, wm1, b0, b1, low)
            o_ref[rows, :] = (ug * mixed).astype(bf16)
            return carry

        lax.fori_loop(0, s // CHUNK, chunk, 0, unroll=SGU_UNROLL)

    return pl.pallas_call(
        body, grid=(SGU_W // LANES,),
        in_specs=_sgu_specs(s),
        out_specs=_slab(s, 0),
        out_shape=jax.ShapeDtypeStruct((s, SGU_W), bf16),
        name="sgu_fwd", compiler_params=_params(("arbitrary",)))(proj, proj, sgu_g, w_spatial, b_spatial3)


def _sgu_bwd(proj, dmix, sgu_g, w_spatial, b_spatial3):
    s = proj.shape[0]

    def body(u_ref, v_ref, g_ref, w_ref, b_ref, dy_ref, du_ref, dv_ref, dg_ref, dw_ref, db_ref):
        low = lax.broadcasted_iota(jnp.int32, (1, LANES), 1) < HEAD
        mask = _tril()
        w0 = jnp.where(mask, w_ref[0], 0.0)
        w1 = jnp.where(mask, w_ref[1], 0.0)
        wm0 = w0.astype(bf16)
        wm1 = w1.astype(bf16)
        wt0 = w0.T.astype(bf16)
        wt1 = w1.T.astype(bf16)
        g = g_ref[...]
        b0 = b_ref[0]
        b1 = b_ref[1]
        dg_ref[...] = jnp.zeros_like(dg_ref)
        dw_ref[...] = jnp.zeros_like(dw_ref)
        db_ref[...] = jnp.zeros_like(db_ref)

        def chunk(n, carry):
            rows = pl.ds(pl.multiple_of(n * CHUNK, CHUNK), CHUNK)
            up = u_ref[rows, :]
            vp = v_ref[rows, :]
            ug, vn, rstd, vb, mixed = _sgu_chunk(up, vp, g, wm0, wm1, b0, b1, low)
            dy = dy_ref[rows, :]
            du_ref[rows, :] = (dy * mixed * _gelu_grad(up)).astype(bf16)
            dmix_c = dy * ug
            db_ref[0] += jnp.sum(jnp.where(low, dmix_c, 0.0), axis=-1, keepdims=True)
            db_ref[1] += jnp.sum(jnp.where(low, 0.0, dmix_c), axis=-1, keepdims=True)
            dmb = dmix_c.astype(bf16)
            zero = jnp.zeros_like(dmb)
            dw_ref[0] += _nt(jnp.where(low, dmb, zero), vb)
            dw_ref[1] += _nt(jnp.where(low, zero, dmb), vb)
            dvnorm = jnp.where(low, _mm(wt0, dmb), _mm(wt1, dmb))
            dg_ref[...] += jnp.sum(dvnorm * vn, axis=0, keepdims=True)
            dvn = dvnorm * g
            dvg = rstd * (dvn - _head_mean(dvn, low) - vn * _head_mean(dvn * vn, low))
            dv_ref[rows, :] = (dvg * _gelu_grad(vp)).astype(bf16)
            return carry

        lax.fori_loop(0, s // CHUNK, chunk, 0, unroll=SGU_UNROLL)
        dw_ref[0] = jnp.where(mask, dw_ref[0], 0.0)
        dw_ref[1] = jnp.where(mask, dw_ref[1], 0.0)

    out = jax.ShapeDtypeStruct((s, SGU_W), bf16)
    return pl.pallas_call(
        body, grid=(SGU_W // LANES,),
        in_specs=_sgu_specs(s) + [_slab(s, (CONV_W + POOL_W) // LANES)],
        out_specs=[_slab(s, 0), _slab(s, 0), pl.BlockSpec((1, LANES), lambda k: (0, k)),
                   pl.BlockSpec((2, CHUNK, CHUNK), lambda k: (k, 0, 0)), pl.BlockSpec((2, CHUNK, 1), lambda k: (k, 0, 0))],
        out_shape=[out, out, jax.ShapeDtypeStruct((1, SGU_W), f32), jax.ShapeDtypeStruct((6, CHUNK, CHUNK), f32),
                   jax.ShapeDtypeStruct((6, CHUNK, 1), f32)],
        name="sgu_bwd", compiler_params=_params(("arbitrary",)))(proj, proj, sgu_g, w_spatial, b_spatial3, dmix)


def _fwd_mix(x, w, after):
    proj, xb = _proj(x, w["w_in"], after)
    mix = [_conv_fwd(proj, w["w_conv"]), _pool_fwd(proj, w["w_pool"], w["pool_scale"]),
           _sgu_fwd(proj, w["sgu_ln_g"], w["w_spatial"], w["b_spatial"])]
    xhat1, rstd1, hb = _wo_ln1(mix, x, w["w_o"], w["ln1_g"], w["ln1_b"])
    return dict(proj=proj, xb=xb, mix=mix, xhat1=xhat1, rstd1=rstd1, hb=hb)


def _fwd_mlp(sv, w, after):
    gu, xhat2, rstd2, y = _mlp_fwd(sv["xhat1"], w["ln1_g"], w["ln1_b"], w["w_gate_up"], w["w_down"], w["ln2_g"], w["ln2_b"], after)
    sv.update(gu=gu, xhat2=xhat2, rstd2=rstd2)
    return y


def _bwd_mlp(dy, w, sv, after, hook):
    dz2b, actb, dgub, dz1, dz1b, dmix, g_ln2_g, g_ln2_b, g_ln1_g, g_ln1_b = _mlp_bwd(
        dy, sv["xhat2"], sv["rstd2"], w["ln2_g"], sv["gu"], w["w_gate_up"], w["w_down"], sv["xhat1"], sv["rstd1"], w["ln1_g"],
        w["w_o"], after)
    after = hook(dz1)
    grads = dict(w_gate_up=_weight_grad(sv["hb"], dgub, D_MODEL, D_FF // 2, after),
                 w_down=_weight_grad(actb, dz2b, D_FF // 2, D_MODEL, after),
                 ln2_g=g_ln2_g, ln2_b=g_ln2_b, ln1_g=g_ln1_g, ln1_b=g_ln1_b)
    return (dz1, dz1b, dmix), grads


def _bwd_mix(dz, w, sv, after, hook):
    dz1, dz1b, dmix = dz
    dxa, dgb, dgc, g_conv = _conv_bwd(sv["proj"], dmix, w["w_conv"], after)
    dp, g_pool, g_pscale = _pool_bwd(sv["proj"], dmix, w["w_pool"], w["pool_scale"])
    du, dv, g_sgu_g, g_spatial, g_bsp = _sgu_bwd(sv["proj"], dmix, w["sgu_ln_g"], w["w_spatial"], w["b_spatial"])
    dparts = [dxa, dgb, dgc, dp, du, dv]
    dx = _dx(dz1, dparts, w["w_in"], hook(du))
    grads = dict(
        w_in=_weight_grad_rows(dparts, sv["xb"], 512), w_o=_weight_grad_rows(sv["mix"], dz1b, D_MODEL),
        w_conv=g_conv, w_pool=g_pool, pool_scale=g_pscale, sgu_ln_g=g_sgu_g, w_spatial=g_spatial,
        b_spatial=g_bsp.reshape(6, CHUNK))
    return dx, grads


def _local_step(x, target, layers):
    saved = []
    for w in layers:
        sv = _fwd_mix(x, w, x)
        x = _fwd_mlp(sv, w, x)
        saved.append(sv)
    dy, sq = _loss_head(x, target)
    grads = [None] * len(layers)
    for l in reversed(range(len(layers))):
        dz, g_mlp = _bwd_mlp(dy, layers[l], saved[l], sq, lambda a: a)
        dy, g_mix = _bwd_mix(dz, layers[l], saved[l], dz[0], lambda a: a)
        grads[l] = dict(g_mlp, **g_mix)
    return sq, dy, grads


ANY = pl.BlockSpec(memory_space=pl.ANY)


def _place():
    x, y, c = lax.axis_index("x"), lax.axis_index("y"), lax.axis_index("c")
    others = [(1 - x, y), (x, 1 - y), (1 - x, 1 - y)]
    return x, y, c, others


def _chip_index(cx, cy):
    return 2 * cx + cy


def _half(ref_rows, c):
    half = ref_rows // 2
    return pl.ds(pl.multiple_of(c * half, 8), half)


def _remote(src, dst, send_sem, recv_sem, device):
    return pltpu.make_async_remote_copy(src_ref=src, dst_ref=dst, send_sem=send_sem, recv_sem=recv_sem,
                                        device_id=device, device_id_type=MESH)


def _gather_shards(shards, after):
    n = len(shards)
    base, total = [], 0
    for s in shards:
        base.append(total)
        total += 6 * s.shape[0]

    def body(*refs):
        ins, outs = refs[:n], refs[n + 1:2 * n + 1]
        send, recv = refs[2 * n + 1:]
        x, y, c, others = _place()
        me = _chip_index(x, y)
        sib = (x, y, 1 - c)
        sends = []
        for f in range(n):
            depth, rows = ins[f].shape[0], ins[f].shape[1]
            for l in range(depth):
                for k, (cx, cy) in enumerate(others):
                    sem = base[f] + 6 * l + k
                    cp = _remote(ins[f].at[l, _half(rows, c)], outs[f].at[l, me, _half(rows, c)],
                                 send.at[sem], recv.at[sem], (cx, cy, c))
                    cp.start()
                    sends.append(cp)
        for f in range(n):
            depth, rows = ins[f].shape[0], ins[f].shape[1]
            for l in range(depth):
                for k, (cx, cy) in enumerate(others):
                    sem = base[f] + 6 * l + k
                    landed = outs[f].at[l, _chip_index(cx, cy), _half(rows, c)]
                    _remote(landed, landed, send.at[sem], recv.at[sem], (cx, cy, c)).wait_recv()
                    cp = _remote(landed, landed, send.at[sem + 3], recv.at[sem + 3], sib)
                    cp.start()
                    sends.append(cp)
        for f in range(n):
            depth, rows = ins[f].shape[0], ins[f].shape[1]
            for l in range(depth):
                for k, (cx, cy) in enumerate(others):
                    sem = base[f] + 6 * l + k + 3
                    passed = outs[f].at[l, _chip_index(cx, cy), _half(rows, 1 - c)]
                    _remote(passed, passed, send.at[sem], recv.at[sem], sib).wait_recv()
        for cp in sends:
            cp.wait_send()

    gathered = pl.pallas_call(
        body, in_specs=[ANY] * (n + 1), out_specs=[ANY] * n,
        out_shape=[jax.ShapeDtypeStruct((s.shape[0], N_CHIPS) + s.shape[1:], s.dtype) for s in shards],
        scratch_shapes=[pltpu.SemaphoreType.DMA((total,)), pltpu.SemaphoreType.DMA((total,))],
        name="gather_shards")(*shards, after)
    return [_place_own(g, s) for g, s in zip(gathered, shards)]


def _scalar(value):
    return jnp.reshape(value, (1,)).astype(jnp.int32)


def _place_own(blocks, shard):
    depth, rows, cols = shard.shape

    def body(me_ref, b_ref, s_ref, o_ref):
        o_ref[...] = s_ref[...]

    return pl.pallas_call(
        body,
        grid_spec=pltpu.PrefetchScalarGridSpec(
            num_scalar_prefetch=1, grid=(depth,),
            in_specs=[ANY, pl.BlockSpec((None, rows, cols), lambda l, me: (l, 0, 0))],
            out_specs=pl.BlockSpec((None, None, rows, cols), lambda l, me: (l, me[0], 0, 0))),
        out_shape=jax.ShapeDtypeStruct(blocks.shape, blocks.dtype),
        input_output_aliases={1: 0},
        name="place_own", compiler_params=_params(("arbitrary",)))(
            _scalar(_chip_index(lax.axis_index("x"), lax.axis_index("y"))), blocks, shard)


HBM = pl.BlockSpec(memory_space=pltpu.HBM)
SEM = pl.BlockSpec(memory_space=pltpu.SEMAPHORE)
TOKEN = jax.ShapeDtypeStruct((8, LANES), f32)
SPLIT_COPY = pltpu.CompilerParams(has_side_effects=pltpu.SideEffectType.DATAFLOW_SIDE_EFFECTING)


def _in_hbm(a):
    return pltpu.with_memory_space_constraint(a, pltpu.HBM)


def _full_shape(shard, axis):
    rows, cols = shard.shape
    return (N_CHIPS * rows, cols) if axis == 0 else (rows, N_CHIPS * cols)


def _block_half(ref, axis, j, h):
    if axis == 0:
        rows = ref.shape[0] // N_CHIPS
        return ref.at[pl.ds(pl.multiple_of(j * rows + h * (rows // 2), 16), rows // 2), :]
    half, cols = ref.shape[0] // 2, ref.shape[1] // N_CHIPS
    return ref.at[pl.ds(pl.multiple_of(h * half, 16), half), pl.ds(pl.multiple_of(j * cols, LANES), cols)]


def _place_layer(shards, dtypes, axes, after):
    n = len(shards)

    def body(me_ref, *refs):
        ins, outs = refs[n:2 * n], refs[2 * n + 1:]
        for f in range(n):
            outs[f][...] = ins[f][...].astype(outs[f].dtype)

    lands = [lax.empty(_full_shape(s, ax), dt) for s, ax, dt in zip(shards, axes, dtypes)]
    return pl.pallas_call(
        body,
        grid_spec=pltpu.PrefetchScalarGridSpec(
            num_scalar_prefetch=1, grid=(1,),
            in_specs=[ANY] * n + [pl.BlockSpec(s.shape, lambda i, me: (0, 0)) for s in shards] + [ANY],
            out_specs=[pl.BlockSpec(s.shape, (lambda i, me: (me[0], 0)) if ax == 0 else (lambda i, me: (0, me[0])))
                       for s, ax in zip(shards, axes)]),
        out_shape=[jax.ShapeDtypeStruct(a.shape, a.dtype) for a in lands],
        input_output_aliases={1 + f: f for f in range(n)},
        name="place_layer", compiler_params=_params(("arbitrary",)))(
            _scalar(_chip_index(lax.axis_index("x"), lax.axis_index("y"))), *lands, *shards, after)


def _gather_start(lands, axes, after):
    return _split_copy_start("gather", _gather_plan(axes), 3 * len(lands), [], lands, after)


def _gather_wait(state, axes, after):
    return _split_copy_wait("gather", _gather_plan(axes), state, after)


SIBLING_PAIR_ID = 0


def _split_copy_start(name, plan, count, ins, lands, after, sibling_only=False):
    arrays = list(ins) + list(lands)
    n_in, n = len(ins), len(arrays)

    def body(*refs):
        send, recv, token = refs[n + 1], refs[n + 2], refs[-1]
        if sibling_only:
            x, y, c, _ = _place()
            barrier = pltpu.get_barrier_semaphore()
            pl.semaphore_signal(barrier, inc=1, device_id=(x, y, 1 - c), device_id_type=MESH)
            pl.semaphore_wait(barrier, 1)
        for i, (src, dst, _, peer) in enumerate(plan(refs[:n_in], refs[n_in:n])):
            _remote(src, dst, send.at[i], recv.at[i], peer).start()
        token[...] = jnp.zeros_like(token)

    effect = pltpu.SideEffectType.DATAFLOW_SIDE_EFFECTING
    outs = pl.pallas_call(
        body, name=name + "_start",
        in_specs=[HBM] * n + [ANY],
        out_specs=(SEM, SEM, *[HBM] * n, pl.BlockSpec(memory_space=pltpu.VMEM)),
        out_shape=(pltpu.SemaphoreType.DMA((count,)), pltpu.SemaphoreType.DMA((count,)),
                   *[pltpu.HBM(a.shape, a.dtype) for a in arrays], TOKEN),
        input_output_aliases={i: 2 + i for i in range(n)},
        compiler_params=pltpu.CompilerParams(has_side_effects=effect, collective_id=SIBLING_PAIR_ID) if sibling_only
        else SPLIT_COPY)(*[_in_hbm(a) for a in arrays], after)
    return (outs[0], outs[1], outs[2:2 + n_in], outs[2 + n_in:2 + n]), outs[-1]


def _split_copy_wait(name, plan, state, after):
    send_sems, recv_sems, ins, lands = state
    arrays = list(ins) + list(lands)
    n_in, n = len(ins), len(arrays)

    def body(*refs):
        send, recv, token = refs[n], refs[n + 1], refs[-1]
        for i, (src, _, landing, peer) in enumerate(plan(refs[:n_in], refs[n_in:n])):
            cp = _remote(src, landing, send.at[i], recv.at[i], peer)
            cp.wait_send()
            cp.wait_recv()
        token[...] = jnp.zeros_like(token)

    outs = pl.pallas_call(
        body, name=name + "_wait",
        in_specs=[HBM] * n + [SEM, SEM, ANY],
        out_specs=(*[HBM] * n, pl.BlockSpec(memory_space=pltpu.VMEM)),
        out_shape=(*[pltpu.HBM(a.shape, a.dtype) for a in arrays], TOKEN),
        input_output_aliases={i: i for i in range(n)},
        compiler_params=SPLIT_COPY)(*arrays, send_sems, recv_sems, after)
    return outs[:n_in], outs[n_in:n], outs[-1]


def _gather_plan(axes):
    def plan(ins, lnd):
        x, y, c, others = _place()
        me = _chip_index(x, y)
        return [(_block_half(lnd[f], ax, me, c), _block_half(lnd[f], ax, me, c),
                 _block_half(lnd[f], ax, _chip_index(cx, cy), c), (cx, cy, c))
                for f, ax in enumerate(axes) for cx, cy in others]
    return plan


def _pair_plan(axes):
    def plan(ins, lnd):
        x, y, c, _ = _place()
        return [(_block_half(ins[f], ax, j, 1 - c), lnd[f].at[j], lnd[f].at[j], (x, y, 1 - c))
                for f, ax in enumerate(axes) for j in range(N_CHIPS)]
    return plan


def _scatter_plan(ins, lnd):
    x, y, c, others = _place()
    return [(ins[f].at[_chip_index(cx, cy)], lnd[f].at[k], lnd[f].at[k], (cx, cy, c))
            for f in range(len(ins)) for k, (cx, cy) in enumerate(others)]


def _join_plan(ins, lnd):
    x, y, c, _ = _place()
    return [(lnd[f].at[_half(lnd[f].shape[0], c)], lnd[f].at[_half(lnd[f].shape[0], c)],
             lnd[f].at[_half(lnd[f].shape[0], 1 - c)], (x, y, 1 - c)) for f in range(len(lnd))]


def _gather_finish(lands, axes, after):
    n = len(lands)

    def body(*refs):
        outs = refs[n + 1:2 * n + 1]
        send, recv = refs[2 * n + 1:]
        x, y, c, others = _place()
        sib = (x, y, 1 - c)
        barrier = pltpu.get_barrier_semaphore()
        pl.semaphore_signal(barrier, inc=1, device_id=sib, device_id_type=MESH)
        pl.semaphore_wait(barrier, 1)
        sends = []
        for f in range(n):
            for k, (cx, cy) in enumerate(others):
                landed = _block_half(outs[f], axes[f], _chip_index(cx, cy), c)
                cp = _remote(landed, landed, send.at[3 * f + k], recv.at[3 * f + k], sib)
                cp.start()
                sends.append(cp)
        for f in range(n):
            for k, (cx, cy) in enumerate(others):
                passed = _block_half(outs[f], axes[f], _chip_index(cx, cy), 1 - c)
                _remote(passed, passed, send.at[3 * f + k], recv.at[3 * f + k], sib).wait_recv()
        for cp in sends:
            cp.wait_send()

    return pl.pallas_call(
        body, in_specs=[ANY] * (n + 1), out_specs=[ANY] * n,
        out_shape=[jax.ShapeDtypeStruct(a.shape, a.dtype) for a in lands],
        input_output_aliases={f: f for f in range(n)},
        scratch_shapes=[pltpu.SemaphoreType.DMA((3 * n,)), pltpu.SemaphoreType.DMA((3 * n,))],
        compiler_params=pltpu.CompilerParams(collective_id=SIBLING_PAIR_ID),
        name="gather_finish")(*lands, after)


def _half_blocks(part, axis):
    rows, cols = (part.shape[0] // N_CHIPS, part.shape[1]) if axis == 0 else (part.shape[0], part.shape[1] // N_CHIPS)
    return lax.empty((N_CHIPS, rows // 2, cols), part.dtype)


def _add_pair_layer(parts, gots, axes):
    k = len(parts)

    def body(c_ref, *refs):
        for f in range(k):
            a_ref, b_ref, o_ref = refs[2 * f], refs[2 * f + 1], refs[2 * k + f]
            o_ref[...] = (a_ref[...].astype(f32) + b_ref[...].astype(f32)).astype(o_ref.dtype)

    in_specs, out_specs, operands = [], [], []
    for part, got, axis in zip(parts, gots, axes):
        _, half, cols = got.shape
        if axis == 0:
            part = part.reshape(N_CHIPS, 2, half, cols)
            mine = pl.BlockSpec((None, None, half, cols), lambda j, c: (j, c[0], 0, 0))
        else:
            mine = pl.BlockSpec((half, cols), lambda j, c: (c[0], j))
        block = pl.BlockSpec((None, half, cols), lambda j, c: (j, 0, 0))
        in_specs += [mine, block]
        out_specs.append(block)
        operands += [part, got]
    return pl.pallas_call(
        body,
        grid_spec=pltpu.PrefetchScalarGridSpec(num_scalar_prefetch=1, grid=(N_CHIPS,), in_specs=in_specs, out_specs=out_specs),
        out_shape=[jax.ShapeDtypeStruct(g.shape, p.dtype) for p, g in zip(parts, gots)],
        name="add_pair_layer", compiler_params=_params(("arbitrary",)))(_scalar(lax.axis_index("c")), *operands)


def _scatter_start(sums, after):
    lands = [lax.empty((3,) + s.shape[1:], s.dtype) for s in sums]
    return _split_copy_start("scatter", _scatter_plan, 3 * len(sums), sums, lands, after)


def _scatter_wait(state, after):
    return _split_copy_wait("scatter", _scatter_plan, state, after)


def _add_slots(chip_sums, slots):
    k = len(chip_sums)

    def body(at_ref, *refs):
        for f in range(k):
            own_ref, s_ref, o_ref = refs[2 * f], refs[2 * f + 1], refs[2 * k + f]
            acc = own_ref[...].astype(f32)
            for j in range(3):
                acc = acc + s_ref[j].astype(f32)
            o_ref[...] = acc

    in_specs, out_specs, operands = [], [], []
    for cs, s in zip(chip_sums, slots):
        _, half, cols = cs.shape
        in_specs += [pl.BlockSpec((None, half, cols), lambda i, at: (at[0], 0, 0)), pl.BlockSpec((3, half, cols), lambda i, at: (0, 0, 0))]
        out_specs.append(pl.BlockSpec((None, half, cols), lambda i, at: (at[1], 0, 0)))
        operands += [cs, s]
    at = jnp.concatenate([_scalar(_chip_index(lax.axis_index("x"), lax.axis_index("y"))), _scalar(lax.axis_index("c"))])
    outs = pl.pallas_call(
        body,
        grid_spec=pltpu.PrefetchScalarGridSpec(num_scalar_prefetch=1, grid=(1,), in_specs=in_specs, out_specs=out_specs),
        out_shape=[jax.ShapeDtypeStruct((2,) + cs.shape[1:], f32) for cs in chip_sums],
        name="add_slots", compiler_params=_params(("arbitrary",)))(at, *operands)
    return [o.reshape(2 * o.shape[1], o.shape[2]) for o in outs]


def _adamw_math(w, grad, m, v):
    nm = ADAM_B1 * m + (1.0 - ADAM_B1) * grad
    nv = ADAM_B2 * v + (1.0 - ADAM_B2) * (grad * grad)
    m_hat = nm / (1.0 - ADAM_B1 ** ADAM_STEP)
    v_hat = nv / (1.0 - ADAM_B2 ** ADAM_STEP)
    return nm, nv, -ADAM_LR * (m_hat / (jnp.sqrt(v_hat) + ADAM_EPS) + ADAM_WD * w)


def _adamw_small(ws, gs, ms, vs):
    k = len(ws)

    def body(*refs):
        for f in range(k):
            w_ref, g_ref, m_ref, v_ref = refs[4 * f:4 * f + 4]
            d_ref, nm_ref, nv_ref = refs[4 * k + 3 * f:4 * k + 3 * f + 3]
            nm, nv, step = _adamw_math(w_ref[...], g_ref[...], m_ref[...], v_ref[...])
            d_ref[...] = step
            nm_ref[...] = nm
            nv_ref[...] = nv

    whole = pl.BlockSpec(memory_space=pltpu.VMEM)
    res = pl.pallas_call(
        body, in_specs=[whole] * (4 * k), out_specs=[whole] * (3 * k),
        out_shape=[jax.ShapeDtypeStruct(w.shape, f32) for w in ws for _ in range(3)],
        name="adamw_small", compiler_params=_params())(*[a for four in zip(ws, gs, ms, vs) for a in four])
    return [res[3 * f:3 * f + 3] for f in range(k)]


def _adamw_layer(l, ws, ms, vs, gs, outs, steps, after):
    k = len(ws)

    def body(*refs):
        ins, new = refs[:4 * k], refs[8 * k + 1:]
        for f in range(k):
            w_ref, m_ref, v_ref, g_ref = ins[4 * f:4 * f + 4]
            go_ref, d_ref, nm_ref, nv_ref = new[4 * f:4 * f + 4]
            grad = g_ref[...]
            nm, nv, step = _adamw_math(w_ref[...], grad, m_ref[...], v_ref[...])
            go_ref[...] = grad
            d_ref[...] = step
            nm_ref[...] = nm
            nv_ref[...] = nv

    in_specs, out_specs, operands = [], [], []
    for w, m, v, g in zip(ws, ms, vs, gs):
        _, rows, cols = w.shape
        tile = rows // steps
        layer = pl.BlockSpec((None, tile, cols), lambda i: (l, i, 0))
        in_specs += [layer] * 3 + [_rows(cols, tile)]
        out_specs += [layer] * 4
        operands += [w, m, v, g]
    flat_outs = [o for four in outs for o in four]
    res = pl.pallas_call(
        body, grid=(steps,),
        in_specs=in_specs + [ANY] * (4 * k + 1), out_specs=out_specs,
        out_shape=[jax.ShapeDtypeStruct(o.shape, f32) for o in flat_outs],
        input_output_aliases={4 * k + j: j for j in range(4 * k)},
        name="adamw_layer", compiler_params=_params(("arbitrary",)))(*operands, *flat_outs, after)
    return [res[4 * f:4 * f + 4] for f in range(k)]


SMALL = ("w_conv", "w_pool", "pool_scale", "sgu_ln_g", "w_spatial", "b_spatial", "ln1_g", "ln1_b", "ln2_g", "ln2_b")
WEIGHTS = ("w_in", "w_conv", "w_pool", "pool_scale", "sgu_ln_g", "w_spatial", "b_spatial", "w_o", "ln1_g", "ln1_b",
           "w_gate_up", "w_down", "ln2_g", "ln2_b")
BIG = ("w_in", "w_o", "w_gate_up", "w_down")
GROUPS = (("w_in", "w_o"), ("w_gate_up", "w_down"))
GROUP_AXES = ((0, 0), (1, 0))
SCATTER_HOOKS = 2
ADAMW_STEPS = (2, 4)
CONV_PAD_ROWS = 32
SMALL_LAYER_ROWS = 1024


def _pack_layer(arrays):
    flat = jnp.concatenate([a.reshape(-1) for a in arrays])
    return jnp.pad(flat, (0, SMALL_LAYER_ROWS * LANES - flat.shape[0])).reshape(SMALL_LAYER_ROWS, LANES)


def _unpack_layers(flat, shapes):
    out, at = {}, 0
    for name, shape in shapes.items():
        size = 1
        for d in shape:
            size *= d
        out[name] = flat[:, at:at + size].reshape((flat.shape[0],) + tuple(shape))
        at += size
    return out


def kernel(x, w_in, w_conv, w_pool, pool_scale, sgu_ln_g, w_spatial, b_spatial, w_o, ln1_g, ln1_b, w_gate_up, w_down, ln2_g, ln2_b, loss_target, m_w_in, m_w_conv, m_w_pool, m_pool_scale, m_sgu_ln_g, m_w_spatial, m_b_spatial, m_w_o, m_ln1_g, m_ln1_b, m_w_gate_up, m_w_down, m_ln2_g, m_ln2_b, v_w_in, v_w_conv, v_w_pool, v_pool_scale, v_sgu_ln_g, v_w_spatial, v_b_spatial, v_w_o, v_ln1_g, v_ln1_b, v_w_gate_up, v_w_down, v_ln2_g, v_ln2_b):
    weights = dict(w_in=w_in, w_conv=w_conv, w_pool=w_pool, pool_scale=pool_scale, sgu_ln_g=sgu_ln_g, w_spatial=w_spatial,
                   b_spatial=b_spatial, w_o=w_o, ln1_g=ln1_g, ln1_b=ln1_b, w_gate_up=w_gate_up, w_down=w_down, ln2_g=ln2_g, ln2_b=ln2_b)
    m_in = dict(w_in=m_w_in, w_conv=m_w_conv, w_pool=m_w_pool, pool_scale=m_pool_scale, sgu_ln_g=m_sgu_ln_g, w_spatial=m_w_spatial,
                b_spatial=m_b_spatial, w_o=m_w_o, ln1_g=m_ln1_g, ln1_b=m_ln1_b, w_gate_up=m_w_gate_up, w_down=m_w_down,
                ln2_g=m_ln2_g, ln2_b=m_ln2_b)
    v_in = dict(w_in=v_w_in, w_conv=v_w_conv, w_pool=v_w_pool, pool_scale=v_pool_scale, sgu_ln_g=v_sgu_ln_g, w_spatial=v_w_spatial,
                b_spatial=v_b_spatial, w_o=v_w_o, ln1_g=v_ln1_g, ln1_b=v_ln1_b, w_gate_up=v_w_gate_up, w_down=v_w_down,
                ln2_g=v_ln2_g, ln2_b=v_ln2_b)
    depth = w_in.shape[0]
    conv_cols = w_conv.shape[2]
    chip = _chip_index(lax.axis_index("x"), lax.axis_index("y"))

    big_w = dict(w_in=jnp.swapaxes(w_in, 1, 2), w_o=w_o, w_gate_up=w_gate_up, w_down=w_down)
    big_m = dict(w_in=jnp.swapaxes(m_w_in, 1, 2), w_o=m_w_o, w_gate_up=m_w_gate_up, w_down=m_w_down)
    big_v = dict(w_in=jnp.swapaxes(v_w_in, 1, 2), w_o=v_w_o, w_gate_up=v_w_gate_up, w_down=v_w_down)

    def group_axes(g):
        return GROUP_AXES[g] + ((0,) if g == 0 else ())

    def place(l, g, after):
        shards, dtypes = [big_w[n][l] for n in GROUPS[g]], [bf16, bf16]
        if g == 0:
            shards.append(jnp.pad(w_conv[l], ((0, CONV_PAD_ROWS - 3), (0, LANES - conv_cols))))
            dtypes.append(f32)
        return _place_layer(shards, dtypes, group_axes(g), after)

    def send(l, g, after):
        return _gather_start(placed[l, g], group_axes(g), after)

    stages = [(l, g) for l in range(depth) for g in (0, 1)]
    placed, flights = {}, {}
    token = x
    for st in stages[:2]:
        placed[st] = place(*st, token)
        flights[st], token = send(*st, token)
    recent = token
    for st in stages[2:]:
        placed[st] = place(*st, token)
        recent = placed[st][0]
    act = x[0]
    layers, saved = [], []
    for i, (l, g) in enumerate(stages):
        if g == 0:
            w = dict(w_pool=w_pool[l], pool_scale=pool_scale[l][None], sgu_ln_g=sgu_ln_g[l][None],
                     w_spatial=w_spatial[l], b_spatial=b_spatial[l][:, :, None], ln1_g=ln1_g[l][None], ln1_b=ln1_b[l][None],
                     ln2_g=ln2_g[l][None], ln2_b=ln2_b[l][None])
        _, lands, token = _gather_wait(flights[l, g], group_axes(g), recent)
        if i + 2 < len(stages):
            flights[stages[i + 2]], token = send(*stages[i + 2], token)
        mats = _gather_finish(lands, group_axes(g), token)
        w.update(zip(GROUPS[g], mats))
        if g == 0:
            blocks = mats[2].reshape(N_CHIPS, CONV_PAD_ROWS, LANES)[:, :3, :conv_cols]
            w["w_conv"] = blocks.transpose(1, 0, 2).reshape(3, N_CHIPS * conv_cols)
            sv = _fwd_mix(act, w, token)
            recent = sv["xhat1"]
        else:
            act = recent = _fwd_mlp(sv, w, token)
            layers.append(w)
            saved.append(sv)

    big_outs = {n: [lax.empty(big_w[n].shape, f32) for _ in range(4)] for n in BIG}
    small_sums = [None] * depth
    pending, updates = [], []
    latest = dict(token=None)

    def begin(l, g, parts):
        axes = GROUP_AXES[g] + (0,) * (len(parts) - len(GROUPS[g]))
        lands = [_half_blocks(p, ax) for p, ax in zip(parts, axes)]
        flight, latest["token"] = _split_copy_start("pair", _pair_plan(axes), N_CHIPS * len(parts), parts, lands, latest["token"],
                                                    sibling_only=True)
        pending.append(dict(l=l, g=g, axes=axes, step="pair", age=0, flight=flight))

    def advance(st, recent):
        if st["step"] == "pair":
            parts, got, _ = _split_copy_wait("pair", _pair_plan(st["axes"]), st["flight"], recent)
            sums = _add_pair_layer(parts, got, st["axes"])
            st["flight"], latest["token"] = _scatter_start(sums, latest["token"])
            st["step"] = "scatter"
        elif st["step"] == "scatter":
            sums, slots, _ = _scatter_wait(st["flight"], recent)
            filled = _add_slots(sums, slots)
            st["flight"], latest["token"] = _split_copy_start("join", _join_plan, len(filled), [], filled, latest["token"],
                                                              sibling_only=True)
            st["step"] = "join"
        else:
            _, summed, _ = _split_copy_wait("join", _join_plan, st["flight"], recent)
            updates.append((st["l"], st["g"], summed[:len(GROUPS[st["g"]])]))
            if st["g"] == 0:
                small_sums[st["l"]] = summed[-1]
            st["step"] = "done"
        st["age"] = 0

    def hook(recent):
        for st in reversed(list(pending)):
            st["age"] += 1
            if st["age"] >= SCATTER_HOOKS or st["step"] != "scatter":
                advance(st, recent)
                if st["step"] == "done":
                    pending.remove(st)
        return latest["token"]

    def update(count, recent):
        for l, g, totals in updates[:count]:
            names = GROUPS[g]
            new = _adamw_layer(l, [big_w[n] for n in names], [big_m[n] for n in names], [big_v[n] for n in names], totals,
                               [big_outs[n] for n in names], ADAMW_STEPS[g], latest["token"])
            big_outs.update(zip(names, new))
            recent = new[-1][1]
        del updates[:count]
        return recent

    grad_x, sq = _loss_head(act, loss_target[0])
    latest["token"] = sq
    grads = [None] * depth
    for l in reversed(range(depth)):
        dz, g_mlp = _bwd_mlp(grad_x, layers[l], saved[l], latest["token"], hook)
        hook(g_mlp["w_down"])
        begin(l, 1, [g_mlp[n] for n in GROUPS[1]])
        grad_x, g_mix = _bwd_mix(dz, layers[l], saved[l], latest["token"], hook)
        grads[l] = dict(g_mlp, **g_mix)
        hook(g_mix["w_o"])
        begin(l, 0, [g_mix[n] for n in GROUPS[0]] + [_pack_layer([grads[l][n] for n in SMALL])])
    recent = g_mix["w_o"]
    while pending:
        recent = update(-(-3 * len(updates) // 4), recent)
        hook(recent)
    update(len(updates), recent)
    loss = lax.psum(0.5 / D_MODEL * jnp.sum(sq), ("x", "y", "c"))

    small_sum = _gather_shards([jnp.stack(small_sums)], recent)[0].reshape(depth, SMALL_LAYER_ROWS * LANES)
    grad = {n: [jnp.swapaxes(o, 1, 2) for o in big_outs[n]] if n == "w_in" else big_outs[n] for n in BIG}
    delta = {n: o[1] for n, o in grad.items()}
    new_m = {n: o[2] for n, o in grad.items()}
    new_v = {n: o[3] for n, o in grad.items()}
    grad = {n: o[0] for n, o in grad.items()}
    grad.update(_unpack_layers(small_sum, {n: (3, N_CHIPS * conv_cols) if n == "w_conv" else weights[n].shape[1:] for n in SMALL}))
    grad["w_conv"] = lax.dynamic_slice_in_dim(grad["w_conv"], chip * conv_cols, conv_cols, axis=2)

    results = _adamw_small(*[[src[n] for n in SMALL] for src in (weights, grad, m_in, v_in)])
    for n, (step, moment1, moment2) in zip(SMALL, results):
        delta[n], new_m[n], new_v[n] = step, moment1, moment2

    return (loss, grad_x[None], *[grad[n] for n in WEIGHTS], *[delta[n] for n in WEIGHTS],
            *[new_m[n] for n in WEIGHTS], *[new_v[n] for n in WEIGHTS])
```

```python
import jax
import jax.numpy as jnp
from jax import lax
from jax.experimental import pallas as pl
from jax.experimental.pallas import tpu as pltpu

f32 = jnp.float32
bf16 = jnp.bfloat16

D_MODEL = 1024
DEPTH = 4
CONV_W = 384
POOL_W = 256
SGU_W = 384
IN_W = 3 * CONV_W + POOL_W + 2 * SGU_W
D_FF = 2816
CHUNK = 128
HEAD = 64
POOL_WINDOWS = (2, 4, 8, 16)
ALPHA = float((2 * DEPTH) ** 0.25)
LN_EPS = 1e-5
ADAM_LR = 0.001
ADAM_B1 = 0.9
ADAM_B2 = 0.999
ADAM_EPS = 1e-08
ADAM_WD = 0.01
ADAM_STEP = 10

LANES = 128
TOKEN_TILE = 256
N_CHIPS = 4
VMEM_LIMIT = 56 * 1024 * 1024

BLK_XA, BLK_GB, BLK_GC, BLK_P, BLK_U, BLK_V = 0, 3, 6, 9, 11, 14

MESH = pl.DeviceIdType.MESH


def _params(sem=None):
    return pltpu.CompilerParams(dimension_semantics=sem, vmem_limit_bytes=VMEM_LIMIT)


def _rows(width, tile=TOKEN_TILE):
    return pl.BlockSpec((tile, width), lambda i: (i, 0))


def _resident(shape):
    zeros = (0,) * len(shape)
    return pl.BlockSpec(shape, lambda *_: zeros, pipeline_mode=pl.Buffered(1))


def _nt(a, b):
    return lax.dot_general(a, b, (((1,), (1,)), ((), ())), preferred_element_type=f32)


def _tn(a, b):
    return lax.dot_general(a, b, (((0,), (0,)), ((), ())), preferred_element_type=f32)


def _mm(a, b):
    return jnp.dot(a, b, preferred_element_type=f32)


def _norm_fwd(z):
    mu = jnp.mean(z, axis=-1, keepdims=True)
    zc = z - mu
    var = jnp.mean(zc * zc, axis=-1, keepdims=True)
    rstd = lax.rsqrt(var + LN_EPS)
    return zc * rstd, rstd


def _norm_bwd(dxhat, xhat, rstd):
    m1 = jnp.mean(dxhat, axis=-1, keepdims=True)
    m2 = jnp.mean(dxhat * xhat, axis=-1, keepdims=True)
    return rstd * (dxhat - m1 - xhat * m2)


def _proj(x, w_in_b, after):
    s = x.shape[0]

    def body(x_ref, w_ref, after_ref, p_ref, xb_ref):
        xb = x_ref[...].astype(bf16)
        xb_ref[...] = xb
        p_ref[...] = _nt(xb, w_ref[...])

    return pl.pallas_call(
        body, grid=(s // TOKEN_TILE,),
        in_specs=[_rows(D_MODEL), _resident((IN_W, D_MODEL)), pl.BlockSpec(memory_space=pl.ANY)],
        out_specs=[_rows(IN_W), _rows(D_MODEL)],
        out_shape=[jax.ShapeDtypeStruct((s, IN_W), f32), jax.ShapeDtypeStruct((s, D_MODEL), bf16)],
        name="proj", compiler_params=_params(("arbitrary",)))(x, w_in_b, after)


def _row_ranges(parts):
    out, at = [], 0
    for p in parts:
        out.append((at, at + p.shape[1]))
        at += p.shape[1]
    return out


def _wo_ln1(mix, x, w_o_b, g, b):
    s = x.shape[0]
    n = len(mix)
    ranges = _row_ranges(mix)

    def body(*refs):
        m_refs = refs[:n]
        x_ref, w_ref, g_ref, b_ref, xhat_ref, rstd_ref, hb_ref = refs[n:]
        z = ALPHA * x_ref[...]
        for m_ref, (lo, hi) in zip(m_refs, ranges):
            z = z + _mm(m_ref[...], w_ref[lo:hi, :])
        xhat, rstd = _norm_fwd(z)
        xhat_ref[...] = xhat
        rstd_ref[...] = rstd
        hb_ref[...] = (xhat * g_ref[...] + b_ref[...]).astype(bf16)

    return pl.pallas_call(
        body, grid=(s // TOKEN_TILE,),
        in_specs=[_rows(m.shape[1]) for m in mix] + [_rows(D_MODEL), _resident((D_MODEL, D_MODEL)), _resident((1, D_MODEL)),
                                                     _resident((1, D_MODEL))],
        out_specs=[_rows(D_MODEL), _rows(1), _rows(D_MODEL)],
        out_shape=[jax.ShapeDtypeStruct((s, D_MODEL), f32), jax.ShapeDtypeStruct((s, 1), f32),
                   jax.ShapeDtypeStruct((s, D_MODEL), bf16)],
        name="wo_ln1", compiler_params=_params(("arbitrary",)))(*mix, x, w_o_b, g, b)


def _mlp_fwd(xhat1, g1, b1, w_gu_b, w_down_b, g2, b2, after):
    s = xhat1.shape[0]

    def body(xh_ref, g1_ref, b1_ref, wgu_ref, wd_ref, g2_ref, b2_ref, after_ref, gu_ref, xhat2_ref, rstd2_ref, y_ref):
        h = xh_ref[...] * g1_ref[...] + b1_ref[...]
        gu = _mm(h.astype(bf16), wgu_ref[...])
        gu_ref[...] = gu.astype(bf16)
        gate = gu[:, :D_FF]
        act = gate * jax.nn.sigmoid(gate) * gu[:, D_FF:]
        z = ALPHA * h + _mm(act.astype(bf16), wd_ref[...])
        xhat2, rstd2 = _norm_fwd(z)
        xhat2_ref[...] = xhat2
        rstd2_ref[...] = rstd2
        y_ref[...] = xhat2 * g2_ref[...] + b2_ref[...]

    vec = _resident((1, D_MODEL))
    return pl.pallas_call(
        body, grid=(s // TOKEN_TILE,),
        in_specs=[_rows(D_MODEL), vec, vec, _resident((D_MODEL, 2 * D_FF)), _resident((D_FF, D_MODEL)), vec, vec,
                  pl.BlockSpec(memory_space=pl.ANY)],
        out_specs=[_rows(2 * D_FF), _rows(D_MODEL), _rows(1), _rows(D_MODEL)],
        out_shape=[jax.ShapeDtypeStruct((s, 2 * D_FF), bf16), jax.ShapeDtypeStruct((s, D_MODEL), f32),
                   jax.ShapeDtypeStruct((s, 1), f32), jax.ShapeDtypeStruct((s, D_MODEL), f32)],
        name="mlp_fwd", compiler_params=_params(("arbitrary",)))(xhat1, g1, b1, w_gu_b, w_down_b, g2, b2, after)


def _loss_head(y, target):
    s = y.shape[0]

    def body(y_ref, t_ref, dy_ref, sq_ref):
        @pl.when(pl.program_id(0) == 0)
        def _():
            sq_ref[...] = jnp.zeros_like(sq_ref)

        e = y_ref[...] - t_ref[...]
        dy_ref[...] = e * (1.0 / D_MODEL)
        sq_ref[...] += jnp.sum(e * e, axis=0, keepdims=True)

    return pl.pallas_call(
        body, grid=(s // TOKEN_TILE,),
        in_specs=[_rows(D_MODEL), _rows(D_MODEL)],
        out_specs=[_rows(D_MODEL), pl.BlockSpec((1, D_MODEL), lambda i: (0, 0))],
        out_shape=[jax.ShapeDtypeStruct((s, D_MODEL), f32), jax.ShapeDtypeStruct((1, D_MODEL), f32)],
        name="loss_head", compiler_params=_params(("arbitrary",)))(y, target)


def _mlp_bwd(dy, xhat2, rstd2, g2, gu, w_gu_b, w_down_b, xhat1, rstd1, g1, w_o_b, after):
    s = dy.shape[0]

    def body(dy_ref, xh_ref, rs_ref, g2_ref, gu_ref, wgu_ref, wd_ref, xh1_ref, rs1_ref, g1_ref, wo_ref, after_ref,
             dz_ref, act_ref, dgu_ref, dz1_ref, dz1b_ref, dm_ref, gg_ref, gb_ref, gg1_ref, gb1_ref):
        @pl.when(pl.program_id(0) == 0)
        def _():
            for ref in (gg_ref, gb_ref, gg1_ref, gb1_ref):
                ref[...] = jnp.zeros_like(ref)

        dy_t = dy_ref[...]
        xhat = xh_ref[...]
        gg_ref[...] += jnp.sum(dy_t * xhat, axis=0, keepdims=True)
        gb_ref[...] += jnp.sum(dy_t, axis=0, keepdims=True)
        dz = _norm_bwd(dy_t * g2_ref[...], xhat, rs_ref[...])
        dzb = dz.astype(bf16)
        dz_ref[...] = dzb
        dact = _nt(dzb, wd_ref[...])
        gate = gu_ref[:, :D_FF].astype(f32)
        up = gu_ref[:, D_FF:].astype(f32)
        sg = jax.nn.sigmoid(gate)
        silu = gate * sg
        act_ref[...] = (silu * up).astype(bf16)
        dgu_ref[:, :D_FF] = (dact * up * (sg * (1.0 + gate * (1.0 - sg)))).astype(bf16)
        dgu_ref[:, D_FF:] = (dact * silu).astype(bf16)
        dh = ALPHA * dz + _nt(dgu_ref[...], wgu_ref[...])
        xhat1 = xh1_ref[...]
        gg1_ref[...] += jnp.sum(dh * xhat1, axis=0, keepdims=True)
        gb1_ref[...] += jnp.sum(dh, axis=0, keepdims=True)
        dz1 = _norm_bwd(dh * g1_ref[...], xhat1, rs1_ref[...])
        dz1_ref[...] = dz1
        dz1b = dz1.astype(bf16)
        dz1b_ref[...] = dz1b
        dm_ref[...] = _nt(dz1b, wo_ref[...])

    vec, vec_out = _resident((1, D_MODEL)), pl.BlockSpec((1, D_MODEL), lambda i: (0, 0))
    tokens_f32, tokens_bf16 = jax.ShapeDtypeStruct((s, D_MODEL), f32), jax.ShapeDtypeStruct((s, D_MODEL), bf16)
    sums = jax.ShapeDtypeStruct((1, D_MODEL), f32)
    return pl.pallas_call(
        body, grid=(s // TOKEN_TILE,),
        in_specs=[_rows(D_MODEL), _rows(D_MODEL), _rows(1), vec, _rows(2 * D_FF),
                  _resident((D_MODEL, 2 * D_FF)), _resident((D_FF, D_MODEL)), _rows(D_MODEL), _rows(1), vec,
                  _resident((D_MODEL, D_MODEL)), pl.BlockSpec(memory_space=pl.ANY)],
        out_specs=[_rows(D_MODEL), _rows(D_FF), _rows(2 * D_FF), _rows(D_MODEL), _rows(D_MODEL), _rows(D_MODEL),
                   vec_out, vec_out, vec_out, vec_out],
        out_shape=[tokens_bf16, jax.ShapeDtypeStruct((s, D_FF), bf16), jax.ShapeDtypeStruct((s, 2 * D_FF), bf16),
                   tokens_f32, tokens_bf16, tokens_f32, sums, sums, sums, sums],
        name="mlp_bwd", compiler_params=_params(("arbitrary",)))(
            dy, xhat2, rstd2, g2, gu, w_gu_b, w_down_b, xhat1, rstd1, g1, w_o_b, after)


def _dx(dz1, dparts, w_in_t, after):
    s = dz1.shape[0]
    n = len(dparts)
    ranges = _row_ranges(dparts)

    def body(*refs):
        d_refs = refs[:n]
        dz_ref, w_ref, _, dx_ref = refs[n:]
        acc = ALPHA * dz_ref[...]
        for d_ref, (lo, hi) in zip(d_refs, ranges):
            acc = acc + _mm(d_ref[...], w_ref[lo:hi, :])
        dx_ref[...] = acc

    return pl.pallas_call(
        body, grid=(s // TOKEN_TILE,),
        in_specs=[_rows(d.shape[1]) for d in dparts] + [_rows(D_MODEL), _resident((IN_W, D_MODEL)),
                                                        pl.BlockSpec(memory_space=pl.ANY)],
        out_specs=_rows(D_MODEL),
        out_shape=jax.ShapeDtypeStruct((s, D_MODEL), f32),
        name="dx", compiler_params=_params(("arbitrary",)))(*dparts, dz1, w_in_t, after)


def _weight_grad_rows(parts, b, bn):
    s, n_cols = b.shape
    n = len(parts)
    ranges = _row_ranges(parts)
    m = ranges[-1][1]

    def body(*refs):
        p_refs = refs[:n]
        b_ref, o_ref = refs[n:]
        for p_ref, (lo, hi) in zip(p_refs, ranges):
            o_ref[lo:hi, :] = _tn(p_ref[...], b_ref[...]).astype(bf16)

    return pl.pallas_call(
        body, grid=(n_cols // bn,),
        in_specs=[_resident(p.shape) for p in parts] + [pl.BlockSpec((s, bn), lambda j: (0, j))],
        out_specs=pl.BlockSpec((m, bn), lambda j: (0, j)),
        out_shape=jax.ShapeDtypeStruct((m, n_cols), bf16),
        name="weight_grad_rows", compiler_params=_params(("arbitrary",)))(*parts, b)


def _weight_grad(a, b, bm, bn, after):
    s, m = a.shape
    n = b.shape[1]

    def body(a_ref, b_ref, after_ref, o_ref):
        o_ref[...] = _tn(a_ref[...], b_ref[...]).astype(bf16)

    return pl.pallas_call(
        body, grid=(m // bm, n // bn),
        in_specs=[pl.BlockSpec((s, bm), lambda i, j: (0, i)), pl.BlockSpec((s, bn), lambda i, j: (0, j)),
                  pl.BlockSpec(memory_space=pl.ANY)],
        out_specs=pl.BlockSpec((bm, bn), lambda i, j: (i, j)),
        out_shape=jax.ShapeDtypeStruct((m, n), bf16),
        name="weight_grad", compiler_params=_params(("arbitrary", "arbitrary")))(a, b, after)


def _shift_down(a, k):
    row = lax.broadcasted_iota(jnp.int32, a.shape, 0)
    return jnp.where(row >= k, pltpu.roll(a, k, 0), 0.0)


def _shift_up(a, k):
    n = a.shape[0]
    row = lax.broadcasted_iota(jnp.int32, a.shape, 0)
    return jnp.where(row < n - k, pltpu.roll(a, n - k, 0), 0.0)


def _slab(s, block):
    return pl.BlockSpec((s, LANES), lambda k: (0, block + k))


def _conv_y(z, w):
    return w[0:1, :] * _shift_down(z, 2) + w[1:2, :] * _shift_down(z, 1) + w[2:3, :] * z


def _conv_fwd(proj, w_conv):
    s = proj.shape[0]

    def body(xa_ref, gb_ref, gc_ref, w_ref, o_ref):
        z = gc_ref[...] * xa_ref[...]
        o_ref[...] = (gb_ref[...] * _conv_y(z, w_ref[...])).astype(bf16)

    return pl.pallas_call(
        body, grid=(CONV_W // LANES,),
        in_specs=[_slab(s, BLK_XA), _slab(s, BLK_GB), _slab(s, BLK_GC), pl.BlockSpec((3, LANES), lambda k: (0, k))],
        out_specs=_slab(s, 0),
        out_shape=jax.ShapeDtypeStruct((s, CONV_W), bf16),
        name="conv_fwd", compiler_params=_params(("arbitrary",)))(proj, proj, proj, w_conv)


def _conv_bwd(proj, dmix, w_conv, after):
    s = proj.shape[0]

    def body(xa_ref, gb_ref, gc_ref, dy_ref, w_ref, after_ref, dxa_ref, dgb_ref, dgc_ref, dw_ref):
        xa = xa_ref[...]
        gc = gc_ref[...]
        w = w_ref[...]
        z = gc * xa
        dya = dy_ref[...]
        dgb_ref[...] = (dya * _conv_y(z, w)).astype(bf16)
        dy = dya * gb_ref[...]
        dz = w[2:3, :] * dy + w[1:2, :] * _shift_up(dy, 1) + w[0:1, :] * _shift_up(dy, 2)
        dxa_ref[...] = (dz * gc).astype(bf16)
        dgc_ref[...] = (dz * xa).astype(bf16)
        dw_ref[0:1, :] = jnp.sum(dy * _shift_down(z, 2), axis=0, keepdims=True)
        dw_ref[1:2, :] = jnp.sum(dy * _shift_down(z, 1), axis=0, keepdims=True)
        dw_ref[2:3, :] = jnp.sum(dy * z, axis=0, keepdims=True)

    out = jax.ShapeDtypeStruct((s, CONV_W), bf16)
    return pl.pallas_call(
        body, grid=(CONV_W // LANES,),
        in_specs=[_slab(s, BLK_XA), _slab(s, BLK_GB), _slab(s, BLK_GC), _slab(s, 0), pl.BlockSpec((3, LANES), lambda k: (0, k)),
                  pl.BlockSpec(memory_space=pl.ANY)],
        out_specs=[_slab(s, 0), _slab(s, 0), _slab(s, 0), pl.BlockSpec((3, LANES), lambda k: (0, k))],
        out_shape=[out, out, out, jax.ShapeDtypeStruct((3, CONV_W), f32)],
        name="conv_bwd", compiler_params=_params(("arbitrary",)))(proj, proj, proj, dmix, w_conv, after)


def _pool_window(k):
    lane = lax.broadcasted_iota(jnp.int32, (1, LANES), 1)
    low = lane < HEAD
    first = k == 0
    wlen = jnp.where(low, jnp.where(first, POOL_WINDOWS[0], POOL_WINDOWS[2]), jnp.where(first, POOL_WINDOWS[1], POOL_WINDOWS[3]))
    return wlen, low, first


def _pool_diff(p, k):
    wlen, low, first = _pool_window(k)
    s2 = p + _shift_down(p, 1)
    s4 = s2 + _shift_down(s2, 2)
    s8 = s4 + _shift_down(s4, 4)
    s16 = s8 + _shift_down(s8, 8)
    win = jnp.where(low, jnp.where(first, s2, s8), jnp.where(first, s4, s16))
    row = lax.broadcasted_iota(jnp.int32, p.shape, 0)
    count = jnp.minimum(row + 1, wlen).astype(f32)
    return win / count - p, count


def _pool_weight(w_ref):
    zero = jnp.zeros((HEAD, HEAD), f32)
    top = jnp.concatenate([w_ref[0], zero], axis=1)
    bottom = jnp.concatenate([zero, w_ref[1]], axis=1)
    return jnp.concatenate([top, bottom], axis=0).astype(bf16)


def _pool_fwd(proj, w_pool, pool_scale):
    s = proj.shape[0]

    def body(p_ref, w_ref, sc_ref, o_ref):
        d, _ = _pool_diff(p_ref[...], pl.program_id(0))
        o_ref[...] = (_mm(d.astype(bf16), _pool_weight(w_ref)) * sc_ref[...]).astype(bf16)

    return pl.pallas_call(
        body, grid=(POOL_W // LANES,),
        in_specs=[_slab(s, BLK_P), pl.BlockSpec((2, HEAD, HEAD), lambda k: (k, 0, 0)), pl.BlockSpec((1, LANES), lambda k: (0, k))],
        out_specs=_slab(s, 0),
        out_shape=jax.ShapeDtypeStruct((s, POOL_W), bf16),
        name="pool_fwd", compiler_params=_params(("arbitrary",)))(proj, w_pool, pool_scale)


def _pool_bwd(proj, dmix, w_pool, pool_scale):
    s = proj.shape[0]

    def body(p_ref, dy_ref, w_ref, sc_ref, dp_ref, dw_ref, dsc_ref):
        k = pl.program_id(0)
        d, count = _pool_diff(p_ref[...], k)
        wbd = _pool_weight(w_ref)
        db = d.astype(bf16)
        dyb = dy_ref[...]
        dsc_ref[...] = jnp.sum(dyb * _mm(db, wbd), axis=0, keepdims=True)
        dpre = (dyb * sc_ref[...]).astype(bf16)
        dwbd = _tn(db, dpre)
        dw_ref[0] = dwbd[:HEAD, :HEAD]
        dw_ref[1] = dwbd[HEAD:, HEAD:]
        dd = _nt(dpre, wbd)
        e = dd / count
        wlen, low, first = _pool_window(k)
        a2 = e + _shift_up(e, 1)
        a4 = a2 + _shift_up(a2, 2)
        a8 = a4 + _shift_up(a4, 4)
        a16 = a8 + _shift_up(a8, 8)
        back = jnp.where(low, jnp.where(first, a2, a8), jnp.where(first, a4, a16))
        dp_ref[...] = (back - dd).astype(bf16)

    return pl.pallas_call(
        body, grid=(POOL_W // LANES,),
        in_specs=[_slab(s, BLK_P), _slab(s, CONV_W // LANES), pl.BlockSpec((2, HEAD, HEAD), lambda k: (k, 0, 0)),
                  pl.BlockSpec((1, LANES), lambda k: (0, k))],
        out_specs=[_slab(s, 0), pl.BlockSpec((2, HEAD, HEAD), lambda k: (k, 0, 0)), pl.BlockSpec((1, LANES), lambda k: (0, k))],
        out_shape=[jax.ShapeDtypeStruct((s, POOL_W), bf16), jax.ShapeDtypeStruct((4, HEAD, HEAD), f32),
                   jax.ShapeDtypeStruct((1, POOL_W), f32)],
        name="pool_bwd", compiler_params=_params(("arbitrary",)))(proj, dmix, w_pool, pool_scale)


SGU_UNROLL = 4
INV_SQRT2 = 0.7071067811865476
INV_SQRT_2PI = 0.3989422804014327


def _gelu(x):
    return 0.5 * x * (1.0 + lax.erf(x * INV_SQRT2))


def _gelu_grad(x):
    return 0.5 * (1.0 + lax.erf(x * INV_SQRT2)) + x * (INV_SQRT_2PI * jnp.exp(-0.5 * x * x))


def _head_mean(a, low):
    s_low = jnp.sum(jnp.where(low, a, 0.0), axis=-1, keepdims=True)
    s_high = jnp.sum(jnp.where(low, 0.0, a), axis=-1, keepdims=True)
    return jnp.where(low, s_low, s_high) * (1.0 / HEAD)


def _tril():
    r = lax.broadcasted_iota(jnp.int32, (CHUNK, CHUNK), 0)
    c = lax.broadcasted_iota(jnp.int32, (CHUNK, CHUNK), 1)
    return r >= c


def _sgu_chunk(up, vp, g, wm0, wm1, b0, b1, low):
    ug = _gelu(up)
    vg = _gelu(vp)
    vc = vg - _head_mean(vg, low)
    rstd = lax.rsqrt(_head_mean(vc * vc, low) + LN_EPS)
    vn = vc * rstd
    vb = (vn * g).astype(bf16)
    mixed = jnp.where(low, _mm(wm0, vb) + b0, _mm(wm1, vb) + b1)
    return ug, vn, rstd, vb, mixed


def _sgu_specs(s):
    return [_slab(s, BLK_U), _slab(s, BLK_V), pl.BlockSpec((1, LANES), lambda k: (0, k)),
            pl.BlockSpec((2, CHUNK, CHUNK), lambda k: (k, 0, 0)), pl.BlockSpec((2, CHUNK, 1), lambda k: (k, 0, 0))]


def _sgu_fwd(proj, sgu_g, w_spatial, b_spatial3):
    s = proj.shape[0]

    def body(u_ref, v_ref, g_ref, w_ref, b_ref, o_ref):
        low = lax.broadcasted_iota(jnp.int32, (1, LANES), 1) < HEAD
        mask = _tril()
        wm0 = jnp.where(mask, w_ref[0], 0.0).astype(bf16)
        wm1 = jnp.where(mask, w_ref[1], 0.0).astype(bf16)
        g = g_ref[...]
        b0 = b_ref[0]
        b1 = b_ref[1]

        def chunk(n, carry):
            rows = pl.ds(pl.multiple_of(n * CHUNK, CHUNK), CHUNK)
            ug, _, _, _, mixed = _sgu_chunk(u_ref[rows, :], v_ref[rows, :], g, wm0, wm1, b0, b1, low)
            o_ref[rows, :] = (ug * mixed).astype(bf16)
            return carry

        lax.fori_loop(0, s // CHUNK, chunk, 0, unroll=SGU_UNROLL)

    return pl.pallas_call(
        body, grid=(SGU_W // LANES,),
        in_specs=_sgu_specs(s),
        out_specs=_slab(s, 0),
        out_shape=jax.ShapeDtypeStruct((s, SGU_W), bf16),
        name="sgu_fwd", compiler_params=_params(("arbitrary",)))(proj, proj, sgu_g, w_spatial, b_spatial3)


def _sgu_bwd(proj, dmix, sgu_g, w_spatial, b_spatial3):
    s = proj.shape[0]

    def body(u_ref, v_ref, g_ref, w_ref, b_ref, dy_ref, du_ref, dv_ref, dg_ref, dw_ref, db_ref):
        low = lax.broadcasted_iota(jnp.int32, (1, LANES), 1) < HEAD
        mask = _tril()
        w0 = jnp.where(mask, w_ref[0], 0.0)
        w1 = jnp.where(mask, w_ref[1], 0.0)
        wm0 = w0.astype(bf16)
        wm1 = w1.astype(bf16)
        wt0 = w0.T.astype(bf16)
        wt1 = w1.T.astype(bf16)
        g = g_ref[...]
        b0 = b_ref[0]
        b1 = b_ref[1]
        dg_ref[...] = jnp.zeros_like(dg_ref)
        dw_ref[...] = jnp.zeros_like(dw_ref)
        db_ref[...] = jnp.zeros_like(db_ref)

        def chunk(n, carry):
            rows = pl.ds(pl.multiple_of(n * CHUNK, CHUNK), CHUNK)
            up = u_ref[rows, :]
            vp = v_ref[rows, :]
            ug, vn, rstd, vb, mixed = _sgu_chunk(up, vp, g, wm0, wm1, b0, b1, low)
            dy = dy_ref[rows, :]
            du_ref[rows, :] = (dy * mixed * _gelu_grad(up)).astype(bf16)
            dmix_c = dy * ug
            db_ref[0] += jnp.sum(jnp.where(low, dmix_c, 0.0), axis=-1, keepdims=True)
            db_ref[1] += jnp.sum(jnp.where(low, 0.0, dmix_c), axis=-1, keepdims=True)
            dmb = dmix_c.astype(bf16)
            zero = jnp.zeros_like(dmb)
            dw_ref[0] += _nt(jnp.where(low, dmb, zero), vb)
            dw_ref[1] += _nt(jnp.where(low, zero, dmb), vb)
            dvnorm = jnp.where(low, _mm(wt0, dmb), _mm(wt1, dmb))
            dg_ref[...] += jnp.sum(dvnorm * vn, axis=0, keepdims=True)
            dvn = dvnorm * g
            dvg = rstd * (dvn - _head_mean(dvn, low) - vn * _head_mean(dvn * vn, low))
            dv_ref[rows, :] = (dvg * _gelu_grad(vp)).astype(bf16)
            return carry

        lax.fori_loop(0, s // CHUNK, chunk, 0, unroll=SGU_UNROLL)
        dw_ref[0] = jnp.where(mask, dw_ref[0], 0.0)
        dw_ref[1] = jnp.where(mask, dw_ref[1], 0.0)

    out = jax.ShapeDtypeStruct((s, SGU_W), bf16)
    return pl.pallas_call(
        body, grid=(SGU_W // LANES,),
        in_specs=_sgu_specs(s) + [_slab(s, (CONV_W + POOL_W) // LANES)],
        out_specs=[_slab(s, 0), _slab(s, 0), pl.BlockSpec((1, LANES), lambda k: (0, k)),
                   pl.BlockSpec((2, CHUNK, CHUNK), lambda k: (k, 0, 0)), pl.BlockSpec((2, CHUNK, 1), lambda k: (k, 0, 0))],
        out_shape=[out, out, jax.ShapeDtypeStruct((1, SGU_W), f32), jax.ShapeDtypeStruct((6, CHUNK, CHUNK), f32),
                   jax.ShapeDtypeStruct((6, CHUNK, 1), f32)],
        name="sgu_bwd", compiler_params=_params(("arbitrary",)))(proj, proj, sgu_g, w_spatial, b_spatial3, dmix)


def _fwd_mix(x, w, after):
    proj, xb = _proj(x, w["w_in"], after)
    mix = [_conv_fwd(proj, w["w_conv"]), _pool_fwd(proj, w["w_pool"], w["pool_scale"]),
           _sgu_fwd(proj, w["sgu_ln_g"], w["w_spatial"], w["b_spatial"])]
    xhat1, rstd1, hb = _wo_ln1(mix, x, w["w_o"], w["ln1_g"], w["ln1_b"])
    return dict(proj=proj, xb=xb, mix=mix, xhat1=xhat1, rstd1=rstd1, hb=hb)


def _fwd_mlp(sv, w, after):
    gu, xhat2, rstd2, y = _mlp_fwd(sv["xhat1"], w["ln1_g"], w["ln1_b"], w["w_gate_up"], w["w_down"], w["ln2_g"], w["ln2_b"], after)
    sv.update(gu=gu, xhat2=xhat2, rstd2=rstd2)
    return y


def _bwd_mlp(dy, w, sv, after, hook):
    dz2b, actb, dgub, dz1, dz1b, dmix, g_ln2_g, g_ln2_b, g_ln1_g, g_ln1_b = _mlp_bwd(
        dy, sv["xhat2"], sv["rstd2"], w["ln2_g"], sv["gu"], w["w_gate_up"], w["w_down"], sv["xhat1"], sv["rstd1"], w["ln1_g"],
        w["w_o"], after)
    after = hook(dz1)
    grads = dict(w_gate_up=_weight_grad(sv["hb"], dgub, D_MODEL, D_FF // 2, after),
                 w_down=_weight_grad(actb, dz2b, D_FF // 2, D_MODEL, after),
                 ln2_g=g_ln2_g, ln2_b=g_ln2_b, ln1_g=g_ln1_g, ln1_b=g_ln1_b)
    return (dz1, dz1b, dmix), grads


def _bwd_mix(dz, w, sv, after, hook):
    dz1, dz1b, dmix = dz
    dxa, dgb, dgc, g_conv = _conv_bwd(sv["proj"], dmix, w["w_conv"], after)
    dp, g_pool, g_pscale = _pool_bwd(sv["proj"], dmix, w["w_pool"], w["pool_scale"])
    du, dv, g_sgu_g, g_spatial, g_bsp = _sgu_bwd(sv["proj"], dmix, w["sgu_ln_g"], w["w_spatial"], w["b_spatial"])
    dparts = [dxa, dgb, dgc, dp, du, dv]
    dx = _dx(dz1, dparts, w["w_in"], hook(du))
    grads = dict(
        w_in=_weight_grad_rows(dparts, sv["xb"], 512), w_o=_weight_grad_rows(sv["mix"], dz1b, D_MODEL),
        w_conv=g_conv, w_pool=g_pool, pool_scale=g_pscale, sgu_ln_g=g_sgu_g, w_spatial=g_spatial,
        b_spatial=g_bsp.reshape(6, CHUNK))
    return dx, grads


def _local_step(x, target, layers):
    saved = []
    for w in layers:
        sv = _fwd_mix(x, w, x)
        x = _fwd_mlp(sv, w, x)
        saved.append(sv)
    dy, sq = _loss_head(x, target)
    grads = [None] * len(layers)
    for l in reversed(range(len(layers))):
        dz, g_mlp = _bwd_mlp(dy, layers[l], saved[l], sq, lambda a: a)
        dy, g_mix = _bwd_mix(dz, layers[l], saved[l], dz[0], lambda a: a)
        grads[l] = dict(g_mlp, **g_mix)
    return sq, dy, grads


ANY = pl.BlockSpec(memory_space=pl.ANY)


def _place():
    x, y, c = lax.axis_index("x"), lax.axis_index("y"), lax.axis_index("c")
    others = [(1 - x, y), (x, 1 - y), (1 - x, 1 - y)]
    return x, y, c, others


def _chip_index(cx, cy):
    return 2 * cx + cy


def _half(ref_rows, c):
    half = ref_rows // 2
    return pl.ds(pl.multiple_of(c * half, 8), half)


def _remote(src, dst, send_sem, recv_sem, device):
    return pltpu.make_async_remote_copy(src_ref=src, dst_ref=dst, send_sem=send_sem, recv_sem=recv_sem,
                                        device_id=device, device_id_type=MESH)


def _gather_shards(shards, after):
    n = len(shards)
    base, total = [], 0
    for s in shards:
        base.append(total)
        total += 6 * s.shape[0]

    def body(*refs):
        ins, outs = refs[:n], refs[n + 1:2 * n + 1]
        send, recv = refs[2 * n + 1:]
        x, y, c, others = _place()
        me = _chip_index(x, y)
        sib = (x, y, 1 - c)
        sends = []
        for f in range(n):
            depth, rows = ins[f].shape[0], ins[f].shape[1]
            for l in range(depth):
                for k, (cx, cy) in enumerate(others):
                    sem = base[f] + 6 * l + k
                    cp = _remote(ins[f].at[l, _half(rows, c)], outs[f].at[l, me, _half(rows, c)],
                                 send.at[sem], recv.at[sem], (cx, cy, c))
                    cp.start()
                    sends.append(cp)
        for f in range(n):
            depth, rows = ins[f].shape[0], ins[f].shape[1]
            for l in range(depth):
                for k, (cx, cy) in enumerate(others):
                    sem = base[f] + 6 * l + k
                    landed = outs[f].at[l, _chip_index(cx, cy), _half(rows, c)]
                    _remote(landed, landed, send.at[sem], recv.at[sem], (cx, cy, c)).wait_recv()
                    cp = _remote(landed, landed, send.at[sem + 3], recv.at[sem + 3], sib)
                    cp.start()
                    sends.append(cp)
        for f in range(n):
            depth, rows = ins[f].shape[0], ins[f].shape[1]
            for l in range(depth):
                for k, (cx, cy) in enumerate(others):
                    sem = base[f] + 6 * l + k + 3
                    passed = outs[f].at[l, _chip_index(cx, cy), _half(rows, 1 - c)]
                    _remote(passed, passed, send.at[sem], recv.at[sem], sib).wait_recv()
        for cp in sends:
            cp.wait_send()

    gathered = pl.pallas_call(
        body, in_specs=[ANY] * (n + 1), out_specs=[ANY] * n,
        out_shape=[jax.ShapeDtypeStruct((s.shape[0], N_CHIPS) + s.shape[1:], s.dtype) for s in shards],
        scratch_shapes=[pltpu.SemaphoreType.DMA((total,)), pltpu.SemaphoreType.DMA((total,))],
        name="gather_shards")(*shards, after)
    return [_place_own(g, s) for g, s in zip(gathered, shards)]


def _scalar(value):
    return jnp.reshape(value, (1,)).astype(jnp.int32)


def _place_own(blocks, shard):
    depth, rows, cols = shard.shape

    def body(me_ref, b_ref, s_ref, o_ref):
        o_ref[...] = s_ref[...]

    return pl.pallas_call(
        body,
        grid_spec=pltpu.PrefetchScalarGridSpec(
            num_scalar_prefetch=1, grid=(depth,),
            in_specs=[ANY, pl.BlockSpec((None, rows, cols), lambda l, me: (l, 0, 0))],
            out_specs=pl.BlockSpec((None, None, rows, cols), lambda l, me: (l, me[0], 0, 0))),
        out_shape=jax.ShapeDtypeStruct(blocks.shape, blocks.dtype),
        input_output_aliases={1: 0},
        name="place_own", compiler_params=_params(("arbitrary",)))(
            _scalar(_chip_index(lax.axis_index("x"), lax.axis_index("y"))), blocks, shard)


HBM = pl.BlockSpec(memory_space=pltpu.HBM)
SEM = pl.BlockSpec(memory_space=pltpu.SEMAPHORE)
TOKEN = jax.ShapeDtypeStruct((8, LANES), f32)
SPLIT_COPY = pltpu.CompilerParams(has_side_effects=pltpu.SideEffectType.DATAFLOW_SIDE_EFFECTING)


def _in_hbm(a):
    return pltpu.with_memory_space_constraint(a, pltpu.HBM)


def _full_shape(shard, axis):
    rows, cols = shard.shape
    return (N_CHIPS * rows, cols) if axis == 0 else (rows, N_CHIPS * cols)


def _block_half(ref, axis, j, h):
    if axis == 0:
        rows = ref.shape[0] // N_CHIPS
        return ref.at[pl.ds(pl.multiple_of(j * rows + h * (rows // 2), 16), rows // 2), :]
    half, cols = ref.shape[0] // 2, ref.shape[1] // N_CHIPS
    return ref.at[pl.ds(pl.multiple_of(h * half, 16), half), pl.ds(pl.multiple_of(j * cols, LANES), cols)]


def _place_layer(shards, dtypes, axes, after):
    n = len(shards)

    def body(me_ref, *refs):
        ins, outs = refs[n:2 * n], refs[2 * n + 1:]
        for f in range(n):
            outs[f][...] = ins[f][...].astype(outs[f].dtype)

    lands = [lax.empty(_full_shape(s, ax), dt) for s, ax, dt in zip(shards, axes, dtypes)]
    return pl.pallas_call(
        body,
        grid_spec=pltpu.PrefetchScalarGridSpec(
            num_scalar_prefetch=1, grid=(1,),
            in_specs=[ANY] * n + [pl.BlockSpec(s.shape, lambda i, me: (0, 0)) for s in shards] + [ANY],
            out_specs=[pl.BlockSpec(s.shape, (lambda i, me: (me[0], 0)) if ax == 0 else (lambda i, me: (0, me[0])))
                       for s, ax in zip(shards, axes)]),
        out_shape=[jax.ShapeDtypeStruct(a.shape, a.dtype) for a in lands],
        input_output_aliases={1 + f: f for f in range(n)},
        name="place_layer", compiler_params=_params(("arbitrary",)))(
            _scalar(_chip_index(lax.axis_index("x"), lax.axis_index("y"))), *lands, *shards, after)


def _gather_start(lands, axes, after):
    return _split_copy_start("gather", _gather_plan(axes), 3 * len(lands), [], lands, after)


def _gather_wait(state, axes, after):
    return _split_copy_wait("gather", _gather_plan(axes), state, after)


SIBLING_PAIR_ID = 0


def _split_copy_start(name, plan, count, ins, lands, after, sibling_only=False):
    arrays = list(ins) + list(lands)
    n_in, n = len(ins), len(arrays)

    def body(*refs):
        send, recv, token = refs[n + 1], refs[n + 2], refs[-1]
        if sibling_only:
            x, y, c, _ = _place()
            barrier = pltpu.get_barrier_semaphore()
            pl.semaphore_signal(barrier, inc=1, device_id=(x, y, 1 - c), device_id_type=MESH)
            pl.semaphore_wait(barrier, 1)
        for i, (src, dst, _, peer) in enumerate(plan(refs[:n_in], refs[n_in:n])):
            _remote(src, dst, send.at[i], recv.at[i], peer).start()
        token[...] = jnp.zeros_like(token)

    effect = pltpu.SideEffectType.DATAFLOW_SIDE_EFFECTING
    outs = pl.pallas_call(
        body, name=name + "_start",
        in_specs=[HBM] * n + [ANY],
        out_specs=(SEM, SEM, *[HBM] * n, pl.BlockSpec(memory_space=pltpu.VMEM)),
        out_shape=(pltpu.SemaphoreType.DMA((count,)), pltpu.SemaphoreType.DMA((count,)),
                   *[pltpu.HBM(a.shape, a.dtype) for a in arrays], TOKEN),
        input_output_aliases={i: 2 + i for i in range(n)},
        compiler_params=pltpu.CompilerParams(has_side_effects=effect, collective_id=SIBLING_PAIR_ID) if sibling_only
        else SPLIT_COPY)(*[_in_hbm(a) for a in arrays], after)
    return (outs[0], outs[1], outs[2:2 + n_in], outs[2 + n_in:2 + n]), outs[-1]


def _split_copy_wait(name, plan, state, after):
    send_sems, recv_sems, ins, lands = state
    arrays = list(ins) + list(lands)
    n_in, n = len(ins), len(arrays)

    def body(*refs):
        send, recv, token = refs[n], refs[n + 1], refs[-1]
        for i, (src, _, landing, peer) in enumerate(plan(refs[:n_in], refs[n_in:n])):
            cp = _remote(src, landing, send.at[i], recv.at[i], peer)
            cp.wait_send()
            cp.wait_recv()
        token[...] = jnp.zeros_like(token)

    outs = pl.pallas_call(
        body, name=name + "_wait",
        in_specs=[HBM] * n + [SEM, SEM, ANY],
        out_specs=(*[HBM] * n, pl.BlockSpec(memory_space=pltpu.VMEM)),
        out_shape=(*[pltpu.HBM(a.shape, a.dtype) for a in arrays], TOKEN),
        input_output_aliases={i: i for i in range(n)},
        compiler_params=SPLIT_COPY)(*arrays, send_sems, recv_sems, after)
    return outs[:n_in], outs[n_in:n], outs[-1]


def _gather_plan(axes):
    def plan(ins, lnd):
        x, y, c, others = _place()
        me = _chip_index(x, y)
        return [(_block_half(lnd[f], ax, me, c), _block_half(lnd[f], ax, me, c),
                 _block_half(lnd[f], ax, _chip_index(cx, cy), c), (cx, cy, c))
                for f, ax in enumerate(axes) for cx, cy in others]
    return plan


def _pair_plan(axes):
    def plan(ins, lnd):
        x, y, c, _ = _place()
        return [(_block_half(ins[f], ax, j, 1 - c), lnd[f].at[j], lnd[f].at[j], (x, y, 1 - c))
                for f, ax in enumerate(axes) for j in range(N_CHIPS)]
    return plan


def _scatter_plan(ins, lnd):
    x, y, c, others = _place()
    return [(ins[f].at[_chip_index(cx, cy)], lnd[f].at[k], lnd[f].at[k], (cx, cy, c))
            for f in range(len(ins)) for k, (cx, cy) in enumerate(others)]


def _join_plan(ins, lnd):
    x, y, c, _ = _place()
    return [(lnd[f].at[_half(lnd[f].shape[0], c)], lnd[f].at[_half(lnd[f].shape[0], c)],
             lnd[f].at[_half(lnd[f].shape[0], 1 - c)], (x, y, 1 - c)) for f in range(len(lnd))]


def _gather_finish(lands, axes, after):
    n = len(lands)

    def body(*refs):
        outs = refs[n + 1:2 * n + 1]
        send, recv = refs[2 * n + 1:]
        x, y, c, others = _place()
        sib = (x, y, 1 - c)
        barrier = pltpu.get_barrier_semaphore()
        pl.semaphore_signal(barrier, inc=1, device_id=sib, device_id_type=MESH)
        pl.semaphore_wait(barrier, 1)
        sends = []
        for f in range(n):
            for k, (cx, cy) in enumerate(others):
                landed = _block_half(outs[f], axes[f], _chip_index(cx, cy), c)
                cp = _remote(landed, landed, send.at[3 * f + k], recv.at[3 * f + k], sib)
                cp.start()
                sends.append(cp)
        for f in range(n):
            for k, (cx, cy) in enumerate(others):
                passed = _block_half(outs[f], axes[f], _chip_index(cx, cy), 1 - c)
                _remote(passed, passed, send.at[3 * f + k], recv.at[3 * f + k], sib).wait_recv()
        for cp in sends:
            cp.wait_send()

    return pl.pallas_call(
        body, in_specs=[ANY] * (n + 1), out_specs=[ANY] * n,
        out_shape=[jax.ShapeDtypeStruct(a.shape, a.dtype) for a in lands],
        input_output_aliases={f: f for f in range(n)},
        scratch_shapes=[pltpu.SemaphoreType.DMA((3 * n,)), pltpu.SemaphoreType.DMA((3 * n,))],
        compiler_params=pltpu.CompilerParams(collective_id=SIBLING_PAIR_ID),
        name="gather_finish")(*lands, after)


def _half_blocks(part, axis):
    rows, cols = (part.shape[0] // N_CHIPS, part.shape[1]) if axis == 0 else (part.shape[0], part.shape[1] // N_CHIPS)
    return lax.empty((N_CHIPS, rows // 2, cols), part.dtype)


def _add_pair_layer(parts, gots, axes):
    k = len(parts)

    def body(c_ref, *refs):
        for f in range(k):
            a_ref, b_ref, o_ref = refs[2 * f], refs[2 * f + 1], refs[2 * k + f]
            o_ref[...] = (a_ref[...].astype(f32) + b_ref[...].astype(f32)).astype(o_ref.dtype)

    in_specs, out_specs, operands = [], [], []
    for part, got, axis in zip(parts, gots, axes):
        _, half, cols = got.shape
        if axis == 0:
            part = part.reshape(N_CHIPS, 2, half, cols)
            mine = pl.BlockSpec((None, None, half, cols), lambda j, c: (j, c[0], 0, 0))
        else:
            mine = pl.BlockSpec((half, cols), lambda j, c: (c[0], j))
        block = pl.BlockSpec((None, half, cols), lambda j, c: (j, 0, 0))
        in_specs += [mine, block]
        out_specs.append(block)
        operands += [part, got]
    return pl.pallas_call(
        body,
        grid_spec=pltpu.PrefetchScalarGridSpec(num_scalar_prefetch=1, grid=(N_CHIPS,), in_specs=in_specs, out_specs=out_specs),
        out_shape=[jax.ShapeDtypeStruct(g.shape, p.dtype) for p, g in zip(parts, gots)],
        name="add_pair_layer", compiler_params=_params(("arbitrary",)))(_scalar(lax.axis_index("c")), *operands)


def _scatter_start(sums, after):
    lands = [lax.empty((3,) + s.shape[1:], s.dtype) for s in sums]
    return _split_copy_start("scatter", _scatter_plan, 3 * len(sums), sums, lands, after)


def _scatter_wait(state, after):
    return _split_copy_wait("scatter", _scatter_plan, state, after)


def _add_slots(chip_sums, slots):
    k = len(chip_sums)

    def body(at_ref, *refs):
        for f in range(k):
            own_ref, s_ref, o_ref = refs[2 * f], refs[2 * f + 1], refs[2 * k + f]
            acc = own_ref[...].astype(f32)
            for j in range(3):
                acc = acc + s_ref[j].astype(f32)
            o_ref[...] = acc

    in_specs, out_specs, operands = [], [], []
    for cs, s in zip(chip_sums, slots):
        _, half, cols = cs.shape
        in_specs += [pl.BlockSpec((None, half, cols), lambda i, at: (at[0], 0, 0)), pl.BlockSpec((3, half, cols), lambda i, at: (0, 0, 0))]
        out_specs.append(pl.BlockSpec((None, half, cols), lambda i, at: (at[1], 0, 0)))
        operands += [cs, s]
    at = jnp.concatenate([_scalar(_chip_index(lax.axis_index("x"), lax.axis_index("y"))), _scalar(lax.axis_index("c"))])
    outs = pl.pallas_call(
        body,
        grid_spec=pltpu.PrefetchScalarGridSpec(num_scalar_prefetch=1, grid=(1,), in_specs=in_specs, out_specs=out_specs),
        out_shape=[jax.ShapeDtypeStruct((2,) + cs.shape[1:], f32) for cs in chip_sums],
        name="add_slots", compiler_params=_params(("arbitrary",)))(at, *operands)
    return [o.reshape(2 * o.shape[1], o.shape[2]) for o in outs]


def _adamw_math(w, grad, m, v):
    nm = ADAM_B1 * m + (1.0 - ADAM_B1) * grad
    nv = ADAM_B2 * v + (1.0 - ADAM_B2) * (grad * grad)
    m_hat = nm / (1.0 - ADAM_B1 ** ADAM_STEP)
    v_hat = nv / (1.0 - ADAM_B2 ** ADAM_STEP)
    return nm, nv, -ADAM_LR * (m_hat / (jnp.sqrt(v_hat) + ADAM_EPS) + ADAM_WD * w)


def _adamw_small(ws, gs, ms, vs):
    k = len(ws)

    def body(*refs):
        for f in range(k):
            w_ref, g_ref, m_ref, v_ref = refs[4 * f:4 * f + 4]
            d_ref, nm_ref, nv_ref = refs[4 * k + 3 * f:4 * k + 3 * f + 3]
            nm, nv, step = _adamw_math(w_ref[...], g_ref[...], m_ref[...], v_ref[...])
            d_ref[...] = step
            nm_ref[...] = nm
            nv_ref[...] = nv

    whole = pl.BlockSpec(memory_space=pltpu.VMEM)
    res = pl.pallas_call(
        body, in_specs=[whole] * (4 * k), out_specs=[whole] * (3 * k),
        out_shape=[jax.ShapeDtypeStruct(w.shape, f32) for w in ws for _ in range(3)],
        name="adamw_small", compiler_params=_params())(*[a for four in zip(ws, gs, ms, vs) for a in four])
    return [res[3 * f:3 * f + 3] for f in range(k)]


def _adamw_layer(l, ws, ms, vs, gs, outs, steps, after):
    k = len(ws)

    def body(*refs):
        ins, new = refs[:4 * k], refs[8 * k + 1:]
        for f in range(k):
            w_ref, m_ref, v_ref, g_ref = ins[4 * f:4 * f + 4]
            go_ref, d_ref, nm_ref, nv_ref = new[4 * f:4 * f + 4]
            grad = g_ref[...]
            nm, nv, step = _adamw_math(w_ref[...], grad, m_ref[...], v_ref[...])
            go_ref[...] = grad
            d_ref[...] = step
            nm_ref[...] = nm
            nv_ref[...] = nv

    in_specs, out_specs, operands = [], [], []
    for w, m, v, g in zip(ws, ms, vs, gs):
        _, rows, cols = w.shape
        tile = rows // steps
        layer = pl.BlockSpec((None, tile, cols), lambda i: (l, i, 0))
        in_specs += [layer] * 3 + [_rows(cols, tile)]
        out_specs += [layer] * 4
        operands += [w, m, v, g]
    flat_outs = [o for four in outs for o in four]
    res = pl.pallas_call(
        body, grid=(steps,),
        in_specs=in_specs + [ANY] * (4 * k + 1), out_specs=out_specs,
        out_shape=[jax.ShapeDtypeStruct(o.shape, f32) for o in flat_outs],
        input_output_aliases={4 * k + j: j for j in range(4 * k)},
        name="adamw_layer", compiler_params=_params(("arbitrary",)))(*operands, *flat_outs, after)
    return [res[4 * f:4 * f + 4] for f in range(k)]


SMALL = ("w_conv", "w_pool", "pool_scale", "sgu_ln_g", "w_spatial", "b_spatial", "ln1_g", "ln1_b", "ln2_g", "ln2_b")
WEIGHTS = ("w_in", "w_conv", "w_pool", "pool_scale", "sgu_ln_g", "w_spatial", "b_spatial", "w_o", "ln1_g", "ln1_b",
           "w_gate_up", "w_down", "ln2_g", "ln2_b")
BIG = ("w_in", "w_o", "w_gate_up", "w_down")
GROUPS = (("w_in", "w_o"), ("w_gate_up", "w_down"))
GROUP_AXES = ((0, 0), (1, 0))
SCATTER_HOOKS = 2
ADAMW_STEPS = (2, 4)
CONV_PAD_ROWS = 32
SMALL_LAYER_ROWS = 1024


def _pack_layer(arrays):
    flat = jnp.concatenate([a.reshape(-1) for a in arrays])
    return jnp.pad(flat, (0, SMALL_LAYER_ROWS * LANES - flat.shape[0])).reshape(SMALL_LAYER_ROWS, LANES)


def _unpack_layers(flat, shapes):
    out, at = {}, 0
    for name, shape in shapes.items():
        size = 1
        for d in shape:
            size *= d
        out[name] = flat[:, at:at + size].reshape((flat.shape[0],) + tuple(shape))
        at += size
    return out


def kernel(x, w_in, w_conv, w_pool, pool_scale, sgu_ln_g, w_spatial, b_spatial, w_o, ln1_g, ln1_b, w_gate_up, w_down, ln2_g, ln2_b, loss_target, m_w_in, m_w_conv, m_w_pool, m_pool_scale, m_sgu_ln_g, m_w_spatial, m_b_spatial, m_w_o, m_ln1_g, m_ln1_b, m_w_gate_up, m_w_down, m_ln2_g, m_ln2_b, v_w_in, v_w_conv, v_w_pool, v_pool_scale, v_sgu_ln_g, v_w_spatial, v_b_spatial, v_w_o, v_ln1_g, v_ln1_b, v_w_gate_up, v_w_down, v_ln2_g, v_ln2_b):
    weights = dict(w_in=w_in, w_conv=w_conv, w_pool=w_pool, pool_scale=pool_scale, sgu_ln_g=sgu_ln_g, w_spatial=w_spatial,
                   b_spatial=b_spatial, w_o=w_o, ln1_g=ln1_g, ln1_b=ln1_b, w_gate_up=w_gate_up, w_down=w_down, ln2_g=ln2_g, ln2_b=ln2_b)
    m_in = dict(w_in=m_w_in, w_conv=m_w_conv, w_pool=m_w_pool, pool_scale=m_pool_scale, sgu_ln_g=m_sgu_ln_g, w_spatial=m_w_spatial,
                b_spatial=m_b_spatial, w_o=m_w_o, ln1_g=m_ln1_g, ln1_b=m_ln1_b, w_gate_up=m_w_gate_up, w_down=m_w_down,
                ln2_g=m_ln2_g, ln2_b=m_ln2_b)
    v_in = dict(w_in=v_w_in, w_conv=v_w_conv, w_pool=v_w_pool, pool_scale=v_pool_scale, sgu_ln_g=v_sgu_ln_g, w_spatial=v_w_spatial,
                b_spatial=v_b_spatial, w_o=v_w_o, ln1_g=v_ln1_g, ln1_b=v_ln1_b, w_gate_up=v_w_gate_up, w_down=v_w_down,
                ln2_g=v_ln2_g, ln2_b=v_ln2_b)
    depth = w_in.shape[0]
    conv_cols = w_conv.shape[2]
    chip = _chip_index(lax.axis_index("x"), lax.axis_index("y"))

    big_w = dict(w_in=jnp.swapaxes(w_in, 1, 2), w_o=w_o, w_gate_up=w_gate_up, w_down=w_down)
    big_m = dict(w_in=jnp.swapaxes(m_w_in, 1, 2), w_o=m_w_o, w_gate_up=m_w_gate_up, w_down=m_w_down)
    big_v = dict(w_in=jnp.swapaxes(v_w_in, 1, 2), w_o=v_w_o, w_gate_up=v_w_gate_up, w_down=v_w_down)

    def group_axes(g):
        return GROUP_AXES[g] + ((0,) if g == 0 else ())

    def place(l, g, after):
        shards, dtypes = [big_w[n][l] for n in GROUPS[g]], [bf16, bf16]
        if g == 0:
            shards.append(jnp.pad(w_conv[l], ((0, CONV_PAD_ROWS - 3), (0, LANES - conv_cols))))
            dtypes.append(f32)
        return _place_layer(shards, dtypes, group_axes(g), after)

    def send(l, g, after):
        return _gather_start(placed[l, g], group_axes(g), after)

    stages = [(l, g) for l in range(depth) for g in (0, 1)]
    placed, flights = {}, {}
    token = x
    for st in stages[:2]:
        placed[st] = place(*st, token)
        flights[st], token = send(*st, token)
    recent = token
    for st in stages[2:]:
        placed[st] = place(*st, token)
        recent = placed[st][0]
    act = x[0]
    layers, saved = [], []
    for i, (l, g) in enumerate(stages):
        if g == 0:
            w = dict(w_pool=w_pool[l], pool_scale=pool_scale[l][None], sgu_ln_g=sgu_ln_g[l][None],
                     w_spatial=w_spatial[l], b_spatial=b_spatial[l][:, :, None], ln1_g=ln1_g[l][None], ln1_b=ln1_b[l][None],
                     ln2_g=ln2_g[l][None], ln2_b=ln2_b[l][None])
        _, lands, token = _gather_wait(flights[l, g], group_axes(g), recent)
        if i + 2 < len(stages):
            flights[stages[i + 2]], token = send(*stages[i + 2], token)
        mats = _gather_finish(lands, group_axes(g), token)
        w.update(zip(GROUPS[g], mats))
        if g == 0:
            blocks = mats[2].reshape(N_CHIPS, CONV_PAD_ROWS, LANES)[:, :3, :conv_cols]
            w["w_conv"] = blocks.transpose(1, 0, 2).reshape(3, N_CHIPS * conv_cols)
            sv = _fwd_mix(act, w, token)
            recent = sv["xhat1"]
        else:
            act = recent = _fwd_mlp(sv, w, token)
            layers.append(w)
            saved.append(sv)

    big_outs = {n: [lax.empty(big_w[n].shape, f32) for _ in range(4)] for n in BIG}
    small_sums = [None] * depth
    pending, updates = [], []
    latest = dict(token=None)

    def begin(l, g, parts):
        axes = GROUP_AXES[g] + (0,) * (len(parts) - len(GROUPS[g]))
        lands = [_half_blocks(p, ax) for p, ax in zip(parts, axes)]
        flight, latest["token"] = _split_copy_start("pair", _pair_plan(axes), N_CHIPS * len(parts), parts, lands, latest["token"],
                                                    sibling_only=True)
        pending.append(dict(l=l, g=g, axes=axes, step="pair", age=0, flight=flight))

    def advance(st, recent):
        if st["step"] == "pair":
            parts, got, _ = _split_copy_wait("pair", _pair_plan(st["axes"]), st["flight"], recent)
            sums = _add_pair_layer(parts, got, st["axes"])
            st["flight"], latest["token"] = _scatter_start(sums, latest["token"])
            st["step"] = "scatter"
        elif st["step"] == "scatter":
            sums, slots, _ = _scatter_wait(st["flight"], recent)
            filled = _add_slots(sums, slots)
            st["flight"], latest["token"] = _split_copy_start("join", _join_plan, len(filled), [], filled, latest["token"],
                                                              sibling_only=True)
            st["step"] = "join"
        else:
            _, summed, _ = _split_copy_wait("join", _join_plan, st["flight"], recent)
            updates.append((st["l"], st["g"], summed[:len(GROUPS[st["g"]])]))
            if st["g"] == 0:
                small_sums[st["l"]] = summed[-1]
            st["step"] = "done"
        st["age"] = 0

    def hook(recent):
        for st in reversed(list(pending)):
            st["age"] += 1
            if st["age"] >= SCATTER_HOOKS or st["step"] != "scatter":
                advance(st, recent)
                if st["step"] == "done":
                    pending.remove(st)
        return latest["token"]

    def update(count, recent):
        for l, g, totals in updates[:count]:
            names = GROUPS[g]
            new = _adamw_layer(l, [big_w[n] for n in names], [big_m[n] for n in names], [big_v[n] for n in names], totals,
                               [big_outs[n] for n in names], ADAMW_STEPS[g], latest["token"])
            big_outs.update(zip(names, new))
            recent = new[-1][1]
        del updates[:count]
        return recent

    grad_x, sq = _loss_head(act, loss_target[0])
    latest["token"] = sq
    grads = [None] * depth
    for l in reversed(range(depth)):
        dz, g_mlp = _bwd_mlp(grad_x, layers[l], saved[l], latest["token"], hook)
        hook(g_mlp["w_down"])
        begin(l, 1, [g_mlp[n] for n in GROUPS[1]])
        grad_x, g_mix = _bwd_mix(dz, layers[l], saved[l], latest["token"], hook)
        grads[l] = dict(g_mlp, **g_mix)
        hook(g_mix["w_o"])
        begin(l, 0, [g_mix[n] for n in GROUPS[0]] + [_pack_layer([grads[l][n] for n in SMALL])])
    recent = g_mix["w_o"]
    while pending:
        recent = update(-(-3 * len(updates) // 4), recent)
        hook(recent)
    update(len(updates), recent)
    loss = lax.psum(0.5 / D_MODEL * jnp.sum(sq), ("x", "y", "c"))

    small_sum = _gather_shards([jnp.stack(small_sums)], recent)[0].reshape(depth, SMALL_LAYER_ROWS * LANES)
    grad = {n: [jnp.swapaxes(o, 1, 2) for o in big_outs[n]] if n == "w_in" else big_outs[n] for n in BIG}
    delta = {n: o[1] for n, o in grad.items()}
    new_m = {n: o[2] for n, o in grad.items()}
    new_v = {n: o[3] for n, o in grad.items()}
    grad = {n: o[0] for n, o in grad.items()}
    grad.update(_unpack_layers(small_sum, {n: (3, N_CHIPS * conv_cols) if n == "w_conv" else weights[n].shape[1:] for n in SMALL}))
    grad["w_conv"] = lax.dynamic_slice_in_dim(grad["w_conv"], chip * conv_cols, conv_cols, axis=2)

    results = _adamw_small(*[[src[n] for n in SMALL] for src in (weights, grad, m_in, v_in)])
    for n, (step, moment1, moment2) in zip(SMALL, results):
        delta[n], new_m[n], new_v[n] = step, moment1, moment2

    return (loss, grad_x[None], *[grad[n] for n in WEIGHTS], *[delta[n] for n in WEIGHTS],
            *[new_m[n] for n in WEIGHTS], *[new_v[n] for n in WEIGHTS])
```

```python
import jax
import jax.numpy as jnp
from jax import lax
from jax.experimental import pallas as pl
from jax.experimental.pallas import tpu as pltpu

f32 = jnp.float32
bf16 = jnp.bfloat16

D_MODEL = 1024
DEPTH = 4
CONV_W = 384
POOL_W = 256
SGU_W = 384
IN_W = 3 * CONV_W + POOL_W + 2 * SGU_W
D_FF = 2816
CHUNK = 128
HEAD = 64
POOL_WINDOWS = (2, 4, 8, 16)
ALPHA = float((2 * DEPTH) ** 0.25)
LN_EPS = 1e-5
ADAM_LR = 0.001
ADAM_B1 = 0.9
ADAM_B2 = 0.999
ADAM_EPS = 1e-08
ADAM_WD = 0.01
ADAM_STEP = 10

LANES = 128
TOKEN_TILE = 256
N_CHIPS = 4
VMEM_LIMIT = 56 * 1024 * 1024

BLK_XA, BLK_GB, BLK_GC, BLK_P, BLK_U, BLK_V = 0, 3, 6, 9, 11, 14

MESH = pl.DeviceIdType.MESH


def _params(sem=None):
    return pltpu.CompilerParams(dimension_semantics=sem, vmem_limit_bytes=VMEM_LIMIT)


def _rows(width, tile=TOKEN_TILE):
    return pl.BlockSpec((tile, width), lambda i: (i, 0))


WIDE_TILE = 512


def _wide(width):
    return _rows(width, WIDE_TILE)


def _resident(shape):
    zeros = (0,) * len(shape)
    return pl.BlockSpec(shape, lambda *_: zeros, pipeline_mode=pl.Buffered(1))


def _nt(a, b):
    return lax.dot_general(a, b, (((1,), (1,)), ((), ())), preferred_element_type=f32)


def _tn(a, b):
    return lax.dot_general(a, b, (((0,), (0,)), ((), ())), preferred_element_type=f32)


def _mm(a, b):
    return jnp.dot(a, b, preferred_element_type=f32)


def _norm_fwd(z):
    mu = jnp.mean(z, axis=-1, keepdims=True)
    zc = z - mu
    var = jnp.mean(zc * zc, axis=-1, keepdims=True)
    rstd = lax.rsqrt(var + LN_EPS)
    return zc * rstd, rstd


def _norm_bwd(dxhat, xhat, rstd):
    m1 = jnp.mean(dxhat, axis=-1, keepdims=True)
    m2 = jnp.mean(dxhat * xhat, axis=-1, keepdims=True)
    return rstd * (dxhat - m1 - xhat * m2)


def _proj(x, w_in_b, after):
    s = x.shape[0]

    def body(x_ref, w_ref, after_ref, p_ref, xb_ref):
        xb = x_ref[...].astype(bf16)
        xb_ref[...] = xb
        p_ref[...] = _nt(xb, w_ref[...])

    return pl.pallas_call(
        body, grid=(s // WIDE_TILE,),
        in_specs=[_wide(D_MODEL), _resident((IN_W, D_MODEL)), pl.BlockSpec(memory_space=pl.ANY)],
        out_specs=[_wide(IN_W), _wide(D_MODEL)],
        out_shape=[jax.ShapeDtypeStruct((s, IN_W), f32), jax.ShapeDtypeStruct((s, D_MODEL), bf16)],
        name="proj", compiler_params=_params(("arbitrary",)))(x, w_in_b, after)


def _row_ranges(parts):
    out, at = [], 0
    for p in parts:
        out.append((at, at + p.shape[1]))
        at += p.shape[1]
    return out


def _wo_ln1(mix, x, w_o_b, g, b):
    s = x.shape[0]
    n = len(mix)
    ranges = _row_ranges(mix)

    def body(*refs):
        m_refs = refs[:n]
        x_ref, w_ref, g_ref, b_ref, xhat_ref, rstd_ref, hb_ref = refs[n:]
        z = ALPHA * x_ref[...]
        for m_ref, (lo, hi) in zip(m_refs, ranges):
            z = z + _mm(m_ref[...], w_ref[lo:hi, :])
        xhat, rstd = _norm_fwd(z)
        xhat_ref[...] = xhat
        rstd_ref[...] = rstd
        hb_ref[...] = (xhat * g_ref[...] + b_ref[...]).astype(bf16)

    return pl.pallas_call(
        body, grid=(s // WIDE_TILE,),
        in_specs=[_wide(m.shape[1]) for m in mix] + [_wide(D_MODEL), _resident((D_MODEL, D_MODEL)), _resident((1, D_MODEL)),
                                                     _resident((1, D_MODEL))],
        out_specs=[_wide(D_MODEL), _wide(1), _wide(D_MODEL)],
        out_shape=[jax.ShapeDtypeStruct((s, D_MODEL), f32), jax.ShapeDtypeStruct((s, 1), f32),
                   jax.ShapeDtypeStruct((s, D_MODEL), bf16)],
        name="wo_ln1", compiler_params=_params(("arbitrary",)))(*mix, x, w_o_b, g, b)


def _mlp_fwd(xhat1, g1, b1, w_gu_b, w_down_b, g2, b2, after):
    s = xhat1.shape[0]

    def body(xh_ref, g1_ref, b1_ref, wgu_ref, wd_ref, g2_ref, b2_ref, after_ref, gu_ref, xhat2_ref, rstd2_ref, y_ref):
        h = xh_ref[...] * g1_ref[...] + b1_ref[...]
        gu = _mm(h.astype(bf16), wgu_ref[...])
        gu_ref[...] = gu
        gate = gu[:, :D_FF]
        act = gate * jax.nn.sigmoid(gate) * gu[:, D_FF:]
        z = ALPHA * h + _mm(act.astype(bf16), wd_ref[...])
        xhat2, rstd2 = _norm_fwd(z)
        xhat2_ref[...] = xhat2
        rstd2_ref[...] = rstd2
        y_ref[...] = xhat2 * g2_ref[...] + b2_ref[...]

    vec = _resident((1, D_MODEL))
    return pl.pallas_call(
        body, grid=(s // TOKEN_TILE,),
        in_specs=[_rows(D_MODEL), vec, vec, _resident((D_MODEL, 2 * D_FF)), _resident((D_FF, D_MODEL)), vec, vec,
                  pl.BlockSpec(memory_space=pl.ANY)],
        out_specs=[_rows(2 * D_FF), _rows(D_MODEL), _rows(1), _rows(D_MODEL)],
        out_shape=[jax.ShapeDtypeStruct((s, 2 * D_FF), f32), jax.ShapeDtypeStruct((s, D_MODEL), f32),
                   jax.ShapeDtypeStruct((s, 1), f32), jax.ShapeDtypeStruct((s, D_MODEL), f32)],
        name="mlp_fwd", compiler_params=_params(("arbitrary",)))(xhat1, g1, b1, w_gu_b, w_down_b, g2, b2, after)


def _loss_head(y, target):
    s = y.shape[0]

    def body(y_ref, t_ref, dy_ref, sq_ref):
        @pl.when(pl.program_id(0) == 0)
        def _():
            sq_ref[...] = jnp.zeros_like(sq_ref)

        e = y_ref[...] - t_ref[...]
        dy_ref[...] = e * (1.0 / D_MODEL)
        sq_ref[...] += jnp.sum(e * e, axis=0, keepdims=True)

    return pl.pallas_call(
        body, grid=(s // WIDE_TILE,),
        in_specs=[_wide(D_MODEL), _wide(D_MODEL)],
        out_specs=[_wide(D_MODEL), pl.BlockSpec((1, D_MODEL), lambda i: (0, 0))],
        out_shape=[jax.ShapeDtypeStruct((s, D_MODEL), f32), jax.ShapeDtypeStruct((1, D_MODEL), f32)],
        name="loss_head", compiler_params=_params(("arbitrary",)))(y, target)


def _mlp_bwd(dy, xhat2, rstd2, g2, gu, w_gu_b, w_down_b, xhat1, rstd1, g1, w_o_b, after):
    s = dy.shape[0]

    def body(dy_ref, xh_ref, rs_ref, g2_ref, gu_ref, wgu_ref, wd_ref, xh1_ref, rs1_ref, g1_ref, wo_ref, after_ref,
             dz_ref, act_ref, dgu_ref, dz1_ref, dz1b_ref, dm_ref, gg_ref, gb_ref, gg1_ref, gb1_ref):
        @pl.when(pl.program_id(0) == 0)
        def _():
            for ref in (gg_ref, gb_ref, gg1_ref, gb1_ref):
                ref[...] = jnp.zeros_like(ref)

        dy_t = dy_ref[...]
        xhat = xh_ref[...]
        gg_ref[...] += jnp.sum(dy_t * xhat, axis=0, keepdims=True)
        gb_ref[...] += jnp.sum(dy_t, axis=0, keepdims=True)
        dz = _norm_bwd(dy_t * g2_ref[...], xhat, rs_ref[...])
        dzb = dz.astype(bf16)
        dz_ref[...] = dzb
        dact = _nt(dzb, wd_ref[...])
        gate = gu_ref[:, :D_FF]
        up = gu_ref[:, D_FF:]
        sg = jax.nn.sigmoid(gate)
        silu = gate * sg
        act_ref[...] = (silu * up).astype(bf16)
        dgu_ref[:, :D_FF] = (dact * up * (sg * (1.0 + gate * (1.0 - sg)))).astype(bf16)
        dgu_ref[:, D_FF:] = (dact * silu).astype(bf16)
        dh = ALPHA * dz + _nt(dgu_ref[...], wgu_ref[...])
        xhat1 = xh1_ref[...]
        gg1_ref[...] += jnp.sum(dh * xhat1, axis=0, keepdims=True)
        gb1_ref[...] += jnp.sum(dh, axis=0, keepdims=True)
        dz1 = _norm_bwd(dh * g1_ref[...], xhat1, rs1_ref[...])
        dz1_ref[...] = dz1
        dz1b = dz1.astype(bf16)
        dz1b_ref[...] = dz1b
        dm_ref[...] = _nt(dz1b, wo_ref[...])

    vec, vec_out = _resident((1, D_MODEL)), pl.BlockSpec((1, D_MODEL), lambda i: (0, 0))
    tokens_f32, tokens_bf16 = jax.ShapeDtypeStruct((s, D_MODEL), f32), jax.ShapeDtypeStruct((s, D_MODEL), bf16)
    sums = jax.ShapeDtypeStruct((1, D_MODEL), f32)
    return pl.pallas_call(
        body, grid=(s // TOKEN_TILE,),
        in_specs=[_rows(D_MODEL), _rows(D_MODEL), _rows(1), vec, _rows(2 * D_FF),
                  _resident((D_MODEL, 2 * D_FF)), _resident((D_FF, D_MODEL)), _rows(D_MODEL), _rows(1), vec,
                  _resident((D_MODEL, D_MODEL)), pl.BlockSpec(memory_space=pl.ANY)],
        out_specs=[_rows(D_MODEL), _rows(D_FF), _rows(2 * D_FF), _rows(D_MODEL), _rows(D_MODEL), _rows(D_MODEL),
                   vec_out, vec_out, vec_out, vec_out],
        out_shape=[tokens_bf16, jax.ShapeDtypeStruct((s, D_FF), bf16), jax.ShapeDtypeStruct((s, 2 * D_FF), bf16),
                   tokens_f32, tokens_bf16, tokens_f32, sums, sums, sums, sums],
        name="mlp_bwd", compiler_params=_params(("arbitrary",)))(
            dy, xhat2, rstd2, g2, gu, w_gu_b, w_down_b, xhat1, rstd1, g1, w_o_b, after)


def _dx(dz1, dparts, w_in_t, after):
    s = dz1.shape[0]
    n = len(dparts)
    ranges = _row_ranges(dparts)

    def body(*refs):
        d_refs = refs[:n]
        dz_ref, w_ref, _, dx_ref = refs[n:]
        acc = ALPHA * dz_ref[...]
        for d_ref, (lo, hi) in zip(d_refs, ranges):
            acc = acc + _mm(d_ref[...], w_ref[lo:hi, :])
        dx_ref[...] = acc

    return pl.pallas_call(
        body, grid=(s // WIDE_TILE,),
        in_specs=[_wide(d.shape[1]) for d in dparts] + [_wide(D_MODEL), _resident((IN_W, D_MODEL)),
                                                        pl.BlockSpec(memory_space=pl.ANY)],
        out_specs=_wide(D_MODEL),
        out_shape=jax.ShapeDtypeStruct((s, D_MODEL), f32),
        name="dx", compiler_params=_params(("arbitrary",)))(*dparts, dz1, w_in_t, after)


def _weight_grad_rows(parts, b, bn):
    s, n_cols = b.shape
    n = len(parts)
    ranges = _row_ranges(parts)
    m = ranges[-1][1]

    def body(*refs):
        p_refs = refs[:n]
        b_ref, o_ref = refs[n:]
        for p_ref, (lo, hi) in zip(p_refs, ranges):
            o_ref[lo:hi, :] = _tn(p_ref[...], b_ref[...]).astype(bf16)

    return pl.pallas_call(
        body, grid=(n_cols // bn,),
        in_specs=[_resident(p.shape) for p in parts] + [pl.BlockSpec((s, bn), lambda j: (0, j))],
        out_specs=pl.BlockSpec((m, bn), lambda j: (0, j)),
        out_shape=jax.ShapeDtypeStruct((m, n_cols), bf16),
        name="weight_grad_rows", compiler_params=_params(("arbitrary",)))(*parts, b)


def _weight_grad(a, b, bm, bn, after):
    s, m = a.shape
    n = b.shape[1]

    def body(a_ref, b_ref, after_ref, o_ref):
        o_ref[...] = _tn(a_ref[...], b_ref[...]).astype(bf16)

    return pl.pallas_call(
        body, grid=(m // bm, n // bn),
        in_specs=[pl.BlockSpec((s, bm), lambda i, j: (0, i)), pl.BlockSpec((s, bn), lambda i, j: (0, j)),
                  pl.BlockSpec(memory_space=pl.ANY)],
        out_specs=pl.BlockSpec((bm, bn), lambda i, j: (i, j)),
        out_shape=jax.ShapeDtypeStruct((m, n), bf16),
        name="weight_grad", compiler_params=_params(("arbitrary", "arbitrary")))(a, b, after)


def _shift_down(a, k):
    row = lax.broadcasted_iota(jnp.int32, a.shape, 0)
    return jnp.where(row >= k, pltpu.roll(a, k, 0), 0.0)


def _shift_up(a, k):
    n = a.shape[0]
    row = lax.broadcasted_iota(jnp.int32, a.shape, 0)
    return jnp.where(row < n - k, pltpu.roll(a, n - k, 0), 0.0)


def _slab(s, block):
    return pl.BlockSpec((s, LANES), lambda k: (0, block + k))


def _conv_y(z, w):
    return w[0:1, :] * _shift_down(z, 2) + w[1:2, :] * _shift_down(z, 1) + w[2:3, :] * z


def _conv_fwd(proj, w_conv):
    s = proj.shape[0]

    def body(xa_ref, gb_ref, gc_ref, w_ref, o_ref):
        z = gc_ref[...] * xa_ref[...]
        o_ref[...] = (gb_ref[...] * _conv_y(z, w_ref[...])).astype(bf16)

    return pl.pallas_call(
        body, grid=(CONV_W // LANES,),
        in_specs=[_slab(s, BLK_XA), _slab(s, BLK_GB), _slab(s, BLK_GC), pl.BlockSpec((3, LANES), lambda k: (0, k))],
        out_specs=_slab(s, 0),
        out_shape=jax.ShapeDtypeStruct((s, CONV_W), bf16),
        name="conv_fwd", compiler_params=_params(("arbitrary",)))(proj, proj, proj, w_conv)


def _conv_bwd(proj, dmix, w_conv, after):
    s = proj.shape[0]

    def body(xa_ref, gb_ref, gc_ref, dy_ref, w_ref, after_ref, dxa_ref, dgb_ref, dgc_ref, dw_ref):
        xa = xa_ref[...]
        gc = gc_ref[...]
        w = w_ref[...]
        z = gc * xa
        dya = dy_ref[...]
        dgb_ref[...] = (dya * _conv_y(z, w)).astype(bf16)
        dy = dya * gb_ref[...]
        dz = w[2:3, :] * dy + w[1:2, :] * _shift_up(dy, 1) + w[0:1, :] * _shift_up(dy, 2)
        dxa_ref[...] = (dz * gc).astype(bf16)
        dgc_ref[...] = (dz * xa).astype(bf16)
        dw_ref[0:1, :] = jnp.sum(dy * _shift_down(z, 2), axis=0, keepdims=True)
        dw_ref[1:2, :] = jnp.sum(dy * _shift_down(z, 1), axis=0, keepdims=True)
        dw_ref[2:3, :] = jnp.sum(dy * z, axis=0, keepdims=True)

    out = jax.ShapeDtypeStruct((s, CONV_W), bf16)
    return pl.pallas_call(
        body, grid=(CONV_W // LANES,),
        in_specs=[_slab(s, BLK_XA), _slab(s, BLK_GB), _slab(s, BLK_GC), _slab(s, 0), pl.BlockSpec((3, LANES), lambda k: (0, k)),
                  pl.BlockSpec(memory_space=pl.ANY)],
        out_specs=[_slab(s, 0), _slab(s, 0), _slab(s, 0), pl.BlockSpec((3, LANES), lambda k: (0, k))],
        out_shape=[out, out, out, jax.ShapeDtypeStruct((3, CONV_W), f32)],
        name="conv_bwd", compiler_params=_params(("arbitrary",)))(proj, proj, proj, dmix, w_conv, after)


def _pool_window(k):
    lane = lax.broadcasted_iota(jnp.int32, (1, LANES), 1)
    low = lane < HEAD
    first = k == 0
    wlen = jnp.where(low, jnp.where(first, POOL_WINDOWS[0], POOL_WINDOWS[2]), jnp.where(first, POOL_WINDOWS[1], POOL_WINDOWS[3]))
    return wlen, low, first


def _pool_diff(p, k):
    wlen, low, first = _pool_window(k)
    s2 = p + _shift_down(p, 1)
    s4 = s2 + _shift_down(s2, 2)
    s8 = s4 + _shift_down(s4, 4)
    s16 = s8 + _shift_down(s8, 8)
    win = jnp.where(low, jnp.where(first, s2, s8), jnp.where(first, s4, s16))
    row = lax.broadcasted_iota(jnp.int32, p.shape, 0)
    count = jnp.minimum(row + 1, wlen).astype(f32)
    return win / count - p, count


def _pool_weight(w_ref):
    zero = jnp.zeros((HEAD, HEAD), f32)
    top = jnp.concatenate([w_ref[0], zero], axis=1)
    bottom = jnp.concatenate([zero, w_ref[1]], axis=1)
    return jnp.concatenate([top, bottom], axis=0).astype(bf16)


def _pool_fwd(proj, w_pool, pool_scale):
    s = proj.shape[0]

    def body(p_ref, w_ref, sc_ref, o_ref):
        d, _ = _pool_diff(p_ref[...], pl.program_id(0))
        o_ref[...] = (_mm(d.astype(bf16), _pool_weight(w_ref)) * sc_ref[...]).astype(bf16)

    return pl.pallas_call(
        body, grid=(POOL_W // LANES,),
        in_specs=[_slab(s, BLK_P), pl.BlockSpec((2, HEAD, HEAD), lambda k: (k, 0, 0)), pl.BlockSpec((1, LANES), lambda k: (0, k))],
        out_specs=_slab(s, 0),
        out_shape=jax.ShapeDtypeStruct((s, POOL_W), bf16),
        name="pool_fwd", compiler_params=_params(("arbitrary",)))(proj, w_pool, pool_scale)


def _pool_bwd(proj, dmix, w_pool, pool_scale):
    s = proj.shape[0]

    def body(p_ref, dy_ref, w_ref, sc_ref, dp_ref, dw_ref, dsc_ref):
        k = pl.program_id(0)
        d, count = _pool_diff(p_ref[...], k)
        wbd = _pool_weight(w_ref)
        db = d.astype(bf16)
        dyb = dy_ref[...]
        dsc_ref[...] = jnp.sum(dyb * _mm(db, wbd), axis=0, keepdims=True)
        dpre = (dyb * sc_ref[...]).astype(bf16)
        dwbd = _tn(db, dpre)
        dw_ref[0] = dwbd[:HEAD, :HEAD]
        dw_ref[1] = dwbd[HEAD:, HEAD:]
        dd = _nt(dpre, wbd)
        e = dd / count
        wlen, low, first = _pool_window(k)
        a2 = e + _shift_up(e, 1)
        a4 = a2 + _shift_up(a2, 2)
        a8 = a4 + _shift_up(a4, 4)
        a16 = a8 + _shift_up(a8, 8)
        back = jnp.where(low, jnp.where(first, a2, a8), jnp.where(first, a4, a16))
        dp_ref[...] = (back - dd).astype(bf16)

    return pl.pallas_call(
        body, grid=(POOL_W // LANES,),
        in_specs=[_slab(s, BLK_P), _slab(s, CONV_W // LANES), pl.BlockSpec((2, HEAD, HEAD), lambda k: (k, 0, 0)),
                  pl.BlockSpec((1, LANES), lambda k: (0, k))],
        out_specs=[_slab(s, 0), pl.BlockSpec((2, HEAD, HEAD), lambda k: (k, 0, 0)), pl.BlockSpec((1, LANES), lambda k: (0, k))],
        out_shape=[jax.ShapeDtypeStruct((s, POOL_W), bf16), jax.ShapeDtypeStruct((4, HEAD, HEAD), f32),
                   jax.ShapeDtypeStruct((1, POOL_W), f32)],
        name="pool_bwd", compiler_params=_params(("arbitrary",)))(proj, dmix, w_pool, pool_scale)


SGU_UNROLL = 4
INV_SQRT2 = 0.7071067811865476
INV_SQRT_2PI = 0.3989422804014327


def _gelu(x):
    return 0.5 * x * (1.0 + lax.erf(x * INV_SQRT2))


def _gelu_grad(x):
    return 0.5 * (1.0 + lax.erf(x * INV_SQRT2)) + x * (INV_SQRT_2PI * jnp.exp(-0.5 * x * x))


def _head_mean(a, low):
    s_low = jnp.sum(jnp.where(low, a, 0.0), axis=-1, keepdims=True)
    s_high = jnp.sum(jnp.where(low, 0.0, a), axis=-1, keepdims=True)
    return jnp.where(low, s_low, s_high) * (1.0 / HEAD)


def _tril():
    r = lax.broadcasted_iota(jnp.int32, (CHUNK, CHUNK), 0)
    c = lax.broadcasted_iota(jnp.int32, (CHUNK, CHUNK), 1)
    return r >= c


def _sgu_chunk(up, vp, g, wm0, wm1, b0, b1, low):
    ug = _gelu(up)
    vg = _gelu(vp)
    vc = vg - _head_mean(vg, low)
    rstd = lax.rsqrt(_head_mean(vc * vc, low) + LN_EPS)
    vn = vc * rstd
    vb = (vn * g).astype(bf16)
    mixed = jnp.where(low, _mm(wm0, vb) + b0, _mm(wm1, vb) + b1)
    return ug, vn, rstd, vb, mixed


def _sgu_specs(s):
    return [_slab(s, BLK_U), _slab(s, BLK_V), pl.BlockSpec((1, LANES), lambda k: (0, k)),
            pl.BlockSpec((2, CHUNK, CHUNK), lambda k: (k, 0, 0)), pl.BlockSpec((2, CHUNK, 1), lambda k: (k, 0, 0))]


def _sgu_fwd(proj, sgu_g, w_spatial, b_spatial3):
    s = proj.shape[0]

    def body(u_ref, v_ref, g_ref, w_ref, b_ref, o_ref):
        low = lax.broadcasted_iota(jnp.int32, (1, LANES), 1) < HEAD
        mask = _tril()
        wm0 = jnp.where(mask, w_ref[0], 0.0).astype(bf16)
        wm1 = jnp.where(mask, w_ref[1], 0.0).astype(bf16)
        g = g_ref[...]
        b0 = b_ref[0]
        b1 = b_ref[1]

        def chunk(n, carry):
            rows = pl.ds(pl.multiple_of(n * CHUNK, CHUNK), CHUNK)
            ug, _, _, _, mixed = _sgu_chunk(u_ref[rows, :], v_ref[rows, :], g, wm0, wm1, b0, b1, low)
            o_ref[rows, :] = (ug * mixed).astype(bf16)
            return carry

        lax.fori_loop(0, s // CHUNK, chunk, 0, unroll=SGU_UNROLL)

    return pl.pallas_call(
        body, grid=(SGU_W // LANES,),
        in_specs=_sgu_specs(s),
        out_specs=_slab(s, 0),
        out_shape=jax.ShapeDtypeStruct((s, SGU_W), bf16),
        name="sgu_fwd", compiler_params=_params(("arbitrary",)))(proj, proj, sgu_g, w_spatial, b_spatial3)


def _sgu_bwd(proj, dmix, sgu_g, w_spatial, b_spatial3):
    s = proj.shape[0]

    def body(u_ref, v_ref, g_ref, w_ref, b_ref, dy_ref, du_ref, dv_ref, dg_ref, dw_ref, db_ref):
        low = lax.broadcasted_iota(jnp.int32, (1, LANES), 1) < HEAD
        mask = _tril()
        w0 = jnp.where(mask, w_ref[0], 0.0)
        w1 = jnp.where(mask, w_ref[1], 0.0)
        wm0 = w0.astype(bf16)
        wm1 = w1.astype(bf16)
        wt0 = w0.T.astype(bf16)
        wt1 = w1.T.astype(bf16)
        g = g_ref[...]
        b0 = b_ref[0]
        b1 = b_ref[1]
        dg_ref[...] = jnp.zeros_like(dg_ref)
        dw_ref[...] = jnp.zeros_like(dw_ref)
        db_ref[...] = jnp.zeros_like(db_ref)

        def chunk(n, carry):
            rows = pl.ds(pl.multiple_of(n * CHUNK, CHUNK), CHUNK)
            up = u_ref[rows, :]
            vp = v_ref[rows, :]
            ug, vn, rstd, vb, mixed = _sgu_chunk(up, vp, g, wm0, wm1, b0, b1, low)
            dy = dy_ref[rows, :]
            du_ref[rows, :] = (dy * mixed * _gelu_grad(up)).astype(bf16)
            dmix_c = dy * ug
            db_ref[0] += jnp.sum(jnp.where(low, dmix_c, 0.0), axis=-1, keepdims=True)
            db_ref[1] += jnp.sum(jnp.where(low, 0.0, dmix_c), axis=-1, keepdims=True)
            dmb = dmix_c.astype(bf16)
            zero = jnp.zeros_like(dmb)
            dw_ref[0] += _nt(jnp.where(low, dmb, zero), vb)
            dw_ref[1] += _nt(jnp.where(low, zero, dmb), vb)
            dvnorm = jnp.where(low, _mm(wt0, dmb), _mm(wt1, dmb))
            dg_ref[...] += jnp.sum(dvnorm * vn, axis=0, keepdims=True)
            dvn = dvnorm * g
            dvg = rstd * (dvn - _head_mean(dvn, low) - vn * _head_mean(dvn * vn, low))
            dv_ref[rows, :] = (dvg * _gelu_grad(vp)).astype(bf16)
            return carry

        lax.fori_loop(0, s // CHUNK, chunk, 0, unroll=SGU_UNROLL)
        dw_ref[0] = jnp.where(mask, dw_ref[0], 0.0)
        dw_ref[1] = jnp.where(mask, dw_ref[1], 0.0)

    out = jax.ShapeDtypeStruct((s, SGU_W), bf16)
    return pl.pallas_call(
        body, grid=(SGU_W // LANES,),
        in_specs=_sgu_specs(s) + [_slab(s, (CONV_W + POOL_W) // LANES)],
        out_specs=[_slab(s, 0), _slab(s, 0), pl.BlockSpec((1, LANES), lambda k: (0, k)),
                   pl.BlockSpec((2, CHUNK, CHUNK), lambda k: (k, 0, 0)), pl.BlockSpec((2, CHUNK, 1), lambda k: (k, 0, 0))],
        out_shape=[out, out, jax.ShapeDtypeStruct((1, SGU_W), f32), jax.ShapeDtypeStruct((6, CHUNK, CHUNK), f32),
                   jax.ShapeDtypeStruct((6, CHUNK, 1), f32)],
        name="sgu_bwd", compiler_params=_params(("arbitrary",)))(proj, proj, sgu_g, w_spatial, b_spatial3, dmix)


def _fwd_mix(x, w, after):
    proj, xb = _proj(x, w["w_in"], after)
    mix = [_conv_fwd(proj, w["w_conv"]), _pool_fwd(proj, w["w_pool"], w["pool_scale"]),
           _sgu_fwd(proj, w["sgu_ln_g"], w["w_spatial"], w["b_spatial"])]
    xhat1, rstd1, hb = _wo_ln1(mix, x, w["w_o"], w["ln1_g"], w["ln1_b"])
    return dict(proj=proj, xb=xb, mix=mix, xhat1=xhat1, rstd1=rstd1, hb=hb)


def _fwd_mlp(sv, w, after):
    gu, xhat2, rstd2, y = _mlp_fwd(sv["xhat1"], w["ln1_g"], w["ln1_b"], w["w_gate_up"], w["w_down"], w["ln2_g"], w["ln2_b"], after)
    sv.update(gu=gu, xhat2=xhat2, rstd2=rstd2)
    return y


def _bwd_mlp(dy, w, sv, after, hook):
    dz2b, actb, dgub, dz1, dz1b, dmix, g_ln2_g, g_ln2_b, g_ln1_g, g_ln1_b = _mlp_bwd(
        dy, sv["xhat2"], sv["rstd2"], w["ln2_g"], sv["gu"], w["w_gate_up"], w["w_down"], sv["xhat1"], sv["rstd1"], w["ln1_g"],
        w["w_o"], after)
    after = hook(dz1)
    grads = dict(w_gate_up=_weight_grad(sv["hb"], dgub, D_MODEL, D_FF // 2, after),
                 w_down=_weight_grad(actb, dz2b, D_FF // 2, D_MODEL, after),
                 ln2_g=g_ln2_g, ln2_b=g_ln2_b, ln1_g=g_ln1_g, ln1_b=g_ln1_b)
    return (dz1, dz1b, dmix), grads


def _bwd_mix(dz, w, sv, after, hook):
    dz1, dz1b, dmix = dz
    dxa, dgb, dgc, g_conv = _conv_bwd(sv["proj"], dmix, w["w_conv"], after)
    dp, g_pool, g_pscale = _pool_bwd(sv["proj"], dmix, w["w_pool"], w["pool_scale"])
    du, dv, g_sgu_g, g_spatial, g_bsp = _sgu_bwd(sv["proj"], dmix, w["sgu_ln_g"], w["w_spatial"], w["b_spatial"])
    dparts = [dxa, dgb, dgc, dp, du, dv]
    dx = _dx(dz1, dparts, w["w_in"], hook(du))
    grads = dict(
        w_in=_weight_grad_rows(dparts, sv["xb"], 512), w_o=_weight_grad_rows(sv["mix"], dz1b, D_MODEL),
        w_conv=g_conv, w_pool=g_pool, pool_scale=g_pscale, sgu_ln_g=g_sgu_g, w_spatial=g_spatial,
        b_spatial=g_bsp.reshape(6, CHUNK))
    return dx, grads


def _local_step(x, target, layers):
    saved = []
    for w in layers:
        sv = _fwd_mix(x, w, x)
        x = _fwd_mlp(sv, w, x)
        saved.append(sv)
    dy, sq = _loss_head(x, target)
    grads = [None] * len(layers)
    for l in reversed(range(len(layers))):
        dz, g_mlp = _bwd_mlp(dy, layers[l], saved[l], sq, lambda a: a)
        dy, g_mix = _bwd_mix(dz, layers[l], saved[l], dz[0], lambda a: a)
        grads[l] = dict(g_mlp, **g_mix)
    return sq, dy, grads


ANY = pl.BlockSpec(memory_space=pl.ANY)


def _place():
    x, y, c = lax.axis_index("x"), lax.axis_index("y"), lax.axis_index("c")
    others = [(1 - x, y), (x, 1 - y), (1 - x, 1 - y)]
    return x, y, c, others


def _chip_index(cx, cy):
    return 2 * cx + cy


def _half(ref_rows, c):
    half = ref_rows // 2
    return pl.ds(pl.multiple_of(c * half, 8), half)


def _remote(src, dst, send_sem, recv_sem, device):
    return pltpu.make_async_remote_copy(src_ref=src, dst_ref=dst, send_sem=send_sem, recv_sem=recv_sem,
                                        device_id=device, device_id_type=MESH)


def _gather_shards(shards, after):
    n = len(shards)
    base, total = [], 0
    for s in shards:
        base.append(total)
        total += 6 * s.shape[0]

    def body(*refs):
        ins, outs = refs[:n], refs[n + 1:2 * n + 1]
        send, recv = refs[2 * n + 1:]
        x, y, c, others = _place()
        me = _chip_index(x, y)
        sib = (x, y, 1 - c)
        sends = []
        for f in range(n):
            depth, rows = ins[f].shape[0], ins[f].shape[1]
            for l in range(depth):
                for k, (cx, cy) in enumerate(others):
                    sem = base[f] + 6 * l + k
                    cp = _remote(ins[f].at[l, _half(rows, c)], outs[f].at[l, me, _half(rows, c)],
                                 send.at[sem], recv.at[sem], (cx, cy, c))
                    cp.start()
                    sends.append(cp)
        for f in range(n):
            depth, rows = ins[f].shape[0], ins[f].shape[1]
            for l in range(depth):
                for k, (cx, cy) in enumerate(others):
                    sem = base[f] + 6 * l + k
                    landed = outs[f].at[l, _chip_index(cx, cy), _half(rows, c)]
                    _remote(landed, landed, send.at[sem], recv.at[sem], (cx, cy, c)).wait_recv()
                    cp = _remote(landed, landed, send.at[sem + 3], recv.at[sem + 3], sib)
                    cp.start()
                    sends.append(cp)
        for f in range(n):
            depth, rows = ins[f].shape[0], ins[f].shape[1]
            for l in range(depth):
                for k, (cx, cy) in enumerate(others):
                    sem = base[f] + 6 * l + k + 3
                    passed = outs[f].at[l, _chip_index(cx, cy), _half(rows, 1 - c)]
                    _remote(passed, passed, send.at[sem], recv.at[sem], sib).wait_recv()
        for cp in sends:
            cp.wait_send()

    gathered = pl.pallas_call(
        body, in_specs=[ANY] * (n + 1), out_specs=[ANY] * n,
        out_shape=[jax.ShapeDtypeStruct((s.shape[0], N_CHIPS) + s.shape[1:], s.dtype) for s in shards],
        scratch_shapes=[pltpu.SemaphoreType.DMA((total,)), pltpu.SemaphoreType.DMA((total,))],
        name="gather_shards")(*shards, after)
    return [_place_own(g, s) for g, s in zip(gathered, shards)]


def _scalar(value):
    return jnp.reshape(value, (1,)).astype(jnp.int32)


def _place_own(blocks, shard):
    depth, rows, cols = shard.shape

    def body(me_ref, b_ref, s_ref, o_ref):
        o_ref[...] = s_ref[...]

    return pl.pallas_call(
        body,
        grid_spec=pltpu.PrefetchScalarGridSpec(
            num_scalar_prefetch=1, grid=(depth,),
            in_specs=[ANY, pl.BlockSpec((None, rows, cols), lambda l, me: (l, 0, 0))],
            out_specs=pl.BlockSpec((None, None, rows, cols), lambda l, me: (l, me[0], 0, 0))),
        out_shape=jax.ShapeDtypeStruct(blocks.shape, blocks.dtype),
        input_output_aliases={1: 0},
        name="place_own", compiler_params=_params(("arbitrary",)))(
            _scalar(_chip_index(lax.axis_index("x"), lax.axis_index("y"))), blocks, shard)


HBM = pl.BlockSpec(memory_space=pltpu.HBM)
SEM = pl.BlockSpec(memory_space=pltpu.SEMAPHORE)
TOKEN = jax.ShapeDtypeStruct((8, LANES), f32)
SPLIT_COPY = pltpu.CompilerParams(has_side_effects=pltpu.SideEffectType.DATAFLOW_SIDE_EFFECTING)


def _in_hbm(a):
    return pltpu.with_memory_space_constraint(a, pltpu.HBM)


def _full_shape(shard, axis):
    rows, cols = shard.shape
    return (N_CHIPS * rows, cols) if axis == 0 else (rows, N_CHIPS * cols)


def _block_half(ref, axis, j, h):
    if axis == 0:
        rows = ref.shape[0] // N_CHIPS
        return ref.at[pl.ds(pl.multiple_of(j * rows + h * (rows // 2), 16), rows // 2), :]
    half, cols = ref.shape[0] // 2, ref.shape[1] // N_CHIPS
    return ref.at[pl.ds(pl.multiple_of(h * half, 16), half), pl.ds(pl.multiple_of(j * cols, LANES), cols)]


def _place_layer(shards, dtypes, axes, after):
    n = len(shards)

    def body(me_ref, *refs):
        ins, outs = refs[n:2 * n], refs[2 * n + 1:]
        for f in range(n):
            outs[f][...] = ins[f][...].astype(outs[f].dtype)

    lands = [lax.empty(_full_shape(s, ax), dt) for s, ax, dt in zip(shards, axes, dtypes)]
    return pl.pallas_call(
        body,
        grid_spec=pltpu.PrefetchScalarGridSpec(
            num_scalar_prefetch=1, grid=(1,),
            in_specs=[ANY] * n + [pl.BlockSpec(s.shape, lambda i, me: (0, 0)) for s in shards] + [ANY],
            out_specs=[pl.BlockSpec(s.shape, (lambda i, me: (me[0], 0)) if ax == 0 else (lambda i, me: (0, me[0])))
                       for s, ax in zip(shards, axes)]),
        out_shape=[jax.ShapeDtypeStruct(a.shape, a.dtype) for a in lands],
        input_output_aliases={1 + f: f for f in range(n)},
        name="place_layer", compiler_params=_params(("arbitrary",)))(
            _scalar(_chip_index(lax.axis_index("x"), lax.axis_index("y"))), *lands, *shards, after)


def _gather_start(lands, axes, after):
    return _split_copy_start("gather", _gather_plan(axes), 3 * len(lands), [], lands, after)


def _gather_wait(state, axes, after):
    return _split_copy_wait("gather", _gather_plan(axes), state, after)


SIBLING_PAIR_ID = 0


def _split_copy_start(name, plan, count, ins, lands, after, sibling_only=False):
    arrays = list(ins) + list(lands)
    n_in, n = len(ins), len(arrays)

    def body(*refs):
        send, recv, token = refs[n + 1], refs[n + 2], refs[-1]
        if sibling_only:
            x, y, c, _ = _place()
            barrier = pltpu.get_barrier_semaphore()
            pl.semaphore_signal(barrier, inc=1, device_id=(x, y, 1 - c), device_id_type=MESH)
            pl.semaphore_wait(barrier, 1)
        for i, (src, dst, _, peer) in enumerate(plan(refs[:n_in], refs[n_in:n])):
            _remote(src, dst, send.at[i], recv.at[i], peer).start()
        token[...] = jnp.zeros_like(token)

    effect = pltpu.SideEffectType.DATAFLOW_SIDE_EFFECTING
    outs = pl.pallas_call(
        body, name=name + "_start",
        in_specs=[HBM] * n + [ANY],
        out_specs=(SEM, SEM, *[HBM] * n, pl.BlockSpec(memory_space=pltpu.VMEM)),
        out_shape=(pltpu.SemaphoreType.DMA((count,)), pltpu.SemaphoreType.DMA((count,)),
                   *[pltpu.HBM(a.shape, a.dtype) for a in arrays], TOKEN),
        input_output_aliases={i: 2 + i for i in range(n)},
        compiler_params=pltpu.CompilerParams(has_side_effects=effect, collective_id=SIBLING_PAIR_ID) if sibling_only
        else SPLIT_COPY)(*[_in_hbm(a) for a in arrays], after)
    return (outs[0], outs[1], outs[2:2 + n_in], outs[2 + n_in:2 + n]), outs[-1]


def _split_copy_wait(name, plan, state, after):
    send_sems, recv_sems, ins, lands = state
    arrays = list(ins) + list(lands)
    n_in, n = len(ins), len(arrays)

    def body(*refs):
        send, recv, token = refs[n], refs[n + 1], refs[-1]
        for i, (src, _, landing, peer) in enumerate(plan(refs[:n_in], refs[n_in:n])):
            cp = _remote(src, landing, send.at[i], recv.at[i], peer)
            cp.wait_send()
            cp.wait_recv()
        token[...] = jnp.zeros_like(token)

    outs = pl.pallas_call(
        body, name=name + "_wait",
        in_specs=[HBM] * n + [SEM, SEM, ANY],
        out_specs=(*[HBM] * n, pl.BlockSpec(memory_space=pltpu.VMEM)),
        out_shape=(*[pltpu.HBM(a.shape, a.dtype) for a in arrays], TOKEN),
        input_output_aliases={i: i for i in range(n)},
        compiler_params=SPLIT_COPY)(*arrays, send_sems, recv_sems, after)
    return outs[:n_in], outs[n_in:n], outs[-1]


def _gather_plan(axes):
    def plan(ins, lnd):
        x, y, c, others = _place()
        me = _chip_index(x, y)
        return [(_block_half(lnd[f], ax, me, c), _block_half(lnd[f], ax, me, c),
                 _block_half(lnd[f], ax, _chip_index(cx, cy), c), (cx, cy, c))
                for f, ax in enumerate(axes) for cx, cy in others]
    return plan


def _pair_plan(axes):
    def plan(ins, lnd):
        x, y, c, _ = _place()
        return [(_block_half(ins[f], ax, j, 1 - c), lnd[f].at[j], lnd[f].at[j], (x, y, 1 - c))
                for f, ax in enumerate(axes) for j in range(N_CHIPS)]
    return plan


def _scatter_plan(ins, lnd):
    x, y, c, others = _place()
    return [(ins[f].at[_chip_index(cx, cy)], lnd[f].at[k], lnd[f].at[k], (cx, cy, c))
            for f in range(len(ins)) for k, (cx, cy) in enumerate(others)]


def _join_plan(ins, lnd):
    x, y, c, _ = _place()
    return [(lnd[f].at[_half(lnd[f].shape[0], c)], lnd[f].at[_half(lnd[f].shape[0], c)],
             lnd[f].at[_half(lnd[f].shape[0], 1 - c)], (x, y, 1 - c)) for f in range(len(lnd))]


def _gather_finish(lands, axes, after):
    n = len(lands)

    def body(*refs):
        outs = refs[n + 1:2 * n + 1]
        send, recv = refs[2 * n + 1:]
        x, y, c, others = _place()
        sib = (x, y, 1 - c)
        barrier = pltpu.get_barrier_semaphore()
        pl.semaphore_signal(barrier, inc=1, device_id=sib, device_id_type=MESH)
        pl.semaphore_wait(barrier, 1)
        sends = []
        for f in range(n):
            for k, (cx, cy) in enumerate(others):
                landed = _block_half(outs[f], axes[f], _chip_index(cx, cy), c)
                cp = _remote(landed, landed, send.at[3 * f + k], recv.at[3 * f + k], sib)
                cp.start()
                sends.append(cp)
        for f in range(n):
            for k, (cx, cy) in enumerate(others):
                passed = _block_half(outs[f], axes[f], _chip_index(cx, cy), 1 - c)
                _remote(passed, passed, send.at[3 * f + k], recv.at[3 * f + k], sib).wait_recv()
        for cp in sends:
            cp.wait_send()

    return pl.pallas_call(
        body, in_specs=[ANY] * (n + 1), out_specs=[ANY] * n,
        out_shape=[jax.ShapeDtypeStruct(a.shape, a.dtype) for a in lands],
        input_output_aliases={f: f for f in range(n)},
        scratch_shapes=[pltpu.SemaphoreType.DMA((3 * n,)), pltpu.SemaphoreType.DMA((3 * n,))],
        compiler_params=pltpu.CompilerParams(collective_id=SIBLING_PAIR_ID),
        name="gather_finish")(*lands, after)


def _half_blocks(part, axis):
    rows, cols = (part.shape[0] // N_CHIPS, part.shape[1]) if axis == 0 else (part.shape[0], part.shape[1] // N_CHIPS)
    return lax.empty((N_CHIPS, rows // 2, cols), part.dtype)


def _add_pair_layer(parts, gots, axes):
    k = len(parts)

    def body(c_ref, *refs):
        for f in range(k):
            a_ref, b_ref, o_ref = refs[2 * f], refs[2 * f + 1], refs[2 * k + f]
            o_ref[...] = (a_ref[...].astype(f32) + b_ref[...].astype(f32)).astype(o_ref.dtype)

    in_specs, out_specs, operands = [], [], []
    for part, got, axis in zip(parts, gots, axes):
        _, half, cols = got.shape
        if axis == 0:
            part = part.reshape(N_CHIPS, 2, half, cols)
            mine = pl.BlockSpec((None, None, half, cols), lambda j, c: (j, c[0], 0, 0))
        else:
            mine = pl.BlockSpec((half, cols), lambda j, c: (c[0], j))
        block = pl.BlockSpec((None, half, cols), lambda j, c: (j, 0, 0))
        in_specs += [mine, block]
        out_specs.append(block)
        operands += [part, got]
    return pl.pallas_call(
        body,
        grid_spec=pltpu.PrefetchScalarGridSpec(num_scalar_prefetch=1, grid=(N_CHIPS,), in_specs=in_specs, out_specs=out_specs),
        out_shape=[jax.ShapeDtypeStruct(g.shape, p.dtype) for p, g in zip(parts, gots)],
        name="add_pair_layer", compiler_params=_params(("arbitrary",)))(_scalar(lax.axis_index("c")), *operands)


def _scatter_start(sums, after):
    lands = [lax.empty((3,) + s.shape[1:], s.dtype) for s in sums]
    return _split_copy_start("scatter", _scatter_plan, 3 * len(sums), sums, lands, after)


def _scatter_wait(state, after):
    return _split_copy_wait("scatter", _scatter_plan, state, after)


def _add_slots(chip_sums, slots):
    k = len(chip_sums)

    def body(at_ref, *refs):
        for f in range(k):
            own_ref, s_ref, o_ref = refs[2 * f], refs[2 * f + 1], refs[2 * k + f]
            acc = own_ref[...].astype(f32)
            for j in range(3):
                acc = acc + s_ref[j].astype(f32)
            o_ref[...] = acc

    in_specs, out_specs, operands = [], [], []
    for cs, s in zip(chip_sums, slots):
        _, half, cols = cs.shape
        in_specs += [pl.BlockSpec((None, half, cols), lambda i, at: (at[0], 0, 0)), pl.BlockSpec((3, half, cols), lambda i, at: (0, 0, 0))]
        out_specs.append(pl.BlockSpec((None, half, cols), lambda i, at: (at[1], 0, 0)))
        operands += [cs, s]
    at = jnp.concatenate([_scalar(_chip_index(lax.axis_index("x"), lax.axis_index("y"))), _scalar(lax.axis_index("c"))])
    outs = pl.pallas_call(
        body,
        grid_spec=pltpu.PrefetchScalarGridSpec(num_scalar_prefetch=1, grid=(1,), in_specs=in_specs, out_specs=out_specs),
        out_shape=[jax.ShapeDtypeStruct((2,) + cs.shape[1:], f32) for cs in chip_sums],
        name="add_slots", compiler_params=_params(("arbitrary",)))(at, *operands)
    return [o.reshape(2 * o.shape[1], o.shape[2]) for o in outs]


def _adamw_math(w, grad, m, v):
    nm = ADAM_B1 * m + (1.0 - ADAM_B1) * grad
    nv = ADAM_B2 * v + (1.0 - ADAM_B2) * (grad * grad)
    m_hat = nm / (1.0 - ADAM_B1 ** ADAM_STEP)
    v_hat = nv / (1.0 - ADAM_B2 ** ADAM_STEP)
    return nm, nv, -ADAM_LR * (m_hat / (jnp.sqrt(v_hat) + ADAM_EPS) + ADAM_WD * w)


def _adamw_small(ws, gs, ms, vs):
    k = len(ws)

    def body(*refs):
        for f in range(k):
            w_ref, g_ref, m_ref, v_ref = refs[4 * f:4 * f + 4]
            d_ref, nm_ref, nv_ref = refs[4 * k + 3 * f:4 * k + 3 * f + 3]
            nm, nv, step = _adamw_math(w_ref[...], g_ref[...], m_ref[...], v_ref[...])
            d_ref[...] = step
            nm_ref[...] = nm
            nv_ref[...] = nv

    whole = pl.BlockSpec(memory_space=pltpu.VMEM)
    res = pl.pallas_call(
        body, in_specs=[whole] * (4 * k), out_specs=[whole] * (3 * k),
        out_shape=[jax.ShapeDtypeStruct(w.shape, f32) for w in ws for _ in range(3)],
        name="adamw_small", compiler_params=_params())(*[a for four in zip(ws, gs, ms, vs) for a in four])
    return [res[3 * f:3 * f + 3] for f in range(k)]


def _adamw_layer(l, ws, ms, vs, gs, outs, steps, after):
    k = len(ws)

    def body(*refs):
        ins, new = refs[:4 * k], refs[8 * k + 1:]
        for f in range(k):
            w_ref, m_ref, v_ref, g_ref = ins[4 * f:4 * f + 4]
            go_ref, d_ref, nm_ref, nv_ref = new[4 * f:4 * f + 4]
            grad = g_ref[...]
            nm, nv, step = _adamw_math(w_ref[...], grad, m_ref[...], v_ref[...])
            go_ref[...] = grad
            d_ref[...] = step
            nm_ref[...] = nm
            nv_ref[...] = nv

    in_specs, out_specs, operands = [], [], []
    for w, m, v, g in zip(ws, ms, vs, gs):
        _, rows, cols = w.shape
        tile = rows // steps
        layer = pl.BlockSpec((None, tile, cols), lambda i: (l, i, 0))
        in_specs += [layer] * 3 + [_rows(cols, tile)]
        out_specs += [layer] * 4
        operands += [w, m, v, g]
    flat_outs = [o for four in outs for o in four]
    res = pl.pallas_call(
        body, grid=(steps,),
        in_specs=in_specs + [ANY] * (4 * k + 1), out_specs=out_specs,
        out_shape=[jax.ShapeDtypeStruct(o.shape, f32) for o in flat_outs],
        input_output_aliases={4 * k + j: j for j in range(4 * k)},
        name="adamw_layer", compiler_params=_params(("arbitrary",)))(*operands, *flat_outs, after)
    return [res[4 * f:4 * f + 4] for f in range(k)]


SMALL = ("w_conv", "w_pool", "pool_scale", "sgu_ln_g", "w_spatial", "b_spatial", "ln1_g", "ln1_b", "ln2_g", "ln2_b")
WEIGHTS = ("w_in", "w_conv", "w_pool", "pool_scale", "sgu_ln_g", "w_spatial", "b_spatial", "w_o", "ln1_g", "ln1_b",
           "w_gate_up", "w_down", "ln2_g", "ln2_b")
BIG = ("w_in", "w_o", "w_gate_up", "w_down")
GROUPS = (("w_in", "w_o"), ("w_gate_up", "w_down"))
GROUP_AXES = ((0, 0), (1, 0))
SCATTER_HOOKS = 2
ADAMW_STEPS = (2, 4)
CONV_PAD_ROWS = 32
SMALL_LAYER_ROWS = 1024


def _pack_layer(arrays):
    flat = jnp.concatenate([a.reshape(-1) for a in arrays])
    return jnp.pad(flat, (0, SMALL_LAYER_ROWS * LANES - flat.shape[0])).reshape(SMALL_LAYER_ROWS, LANES)


def _unpack_layers(flat, shapes):
    out, at = {}, 0
    for name, shape in shapes.items():
        size = 1
        for d in shape:
            size *= d
        out[name] = flat[:, at:at + size].reshape((flat.shape[0],) + tuple(shape))
        at += size
    return out


def kernel(x, w_in, w_conv, w_pool, pool_scale, sgu_ln_g, w_spatial, b_spatial, w_o, ln1_g, ln1_b, w_gate_up, w_down, ln2_g, ln2_b, loss_target, m_w_in, m_w_conv, m_w_pool, m_pool_scale, m_sgu_ln_g, m_w_spatial, m_b_spatial, m_w_o, m_ln1_g, m_ln1_b, m_w_gate_up, m_w_down, m_ln2_g, m_ln2_b, v_w_in, v_w_conv, v_w_pool, v_pool_scale, v_sgu_ln_g, v_w_spatial, v_b_spatial, v_w_o, v_ln1_g, v_ln1_b, v_w_gate_up, v_w_down, v_ln2_g, v_ln2_b):
    weights = dict(w_in=w_in, w_conv=w_conv, w_pool=w_pool, pool_scale=pool_scale, sgu_ln_g=sgu_ln_g, w_spatial=w_spatial,
                   b_spatial=b_spatial, w_o=w_o, ln1_g=ln1_g, ln1_b=ln1_b, w_gate_up=w_gate_up, w_down=w_down, ln2_g=ln2_g, ln2_b=ln2_b)
    m_in = dict(w_in=m_w_in, w_conv=m_w_conv, w_pool=m_w_pool, pool_scale=m_pool_scale, sgu_ln_g=m_sgu_ln_g, w_spatial=m_w_spatial,
                b_spatial=m_b_spatial, w_o=m_w_o, ln1_g=m_ln1_g, ln1_b=m_ln1_b, w_gate_up=m_w_gate_up, w_down=m_w_down,
                ln2_g=m_ln2_g, ln2_b=m_ln2_b)
    v_in = dict(w_in=v_w_in, w_conv=v_w_conv, w_pool=v_w_pool, pool_scale=v_pool_scale, sgu_ln_g=v_sgu_ln_g, w_spatial=v_w_spatial,
                b_spatial=v_b_spatial, w_o=v_w_o, ln1_g=v_ln1_g, ln1_b=v_ln1_b, w_gate_up=v_w_gate_up, w_down=v_w_down,
                ln2_g=v_ln2_g, ln2_b=v_ln2_b)
    depth = w_in.shape[0]
    conv_cols = w_conv.shape[2]
    chip = _chip_index(lax.axis_index("x"), lax.axis_index("y"))

    big_w = dict(w_in=jnp.swapaxes(w_in, 1, 2), w_o=w_o, w_gate_up=w_gate_up, w_down=w_down)
    big_m = dict(w_in=jnp.swapaxes(m_w_in, 1, 2), w_o=m_w_o, w_gate_up=m_w_gate_up, w_down=m_w_down)
    big_v = dict(w_in=jnp.swapaxes(v_w_in, 1, 2), w_o=v_w_o, w_gate_up=v_w_gate_up, w_down=v_w_down)

    def group_axes(g):
        return GROUP_AXES[g] + ((0,) if g == 0 else ())

    def place(l, g, after):
        shards, dtypes = [big_w[n][l] for n in GROUPS[g]], [bf16, bf16]
        if g == 0:
            shards.append(jnp.pad(w_conv[l], ((0, CONV_PAD_ROWS - 3), (0, LANES - conv_cols))))
            dtypes.append(f32)
        return _place_layer(shards, dtypes, group_axes(g), after)

    def send(l, g, after):
        return _gather_start(placed[l, g], group_axes(g), after)

    stages = [(l, g) for l in range(depth) for g in (0, 1)]
    placed, flights = {}, {}
    token = x
    for st in stages[:2]:
        placed[st] = place(*st, token)
        flights[st], token = send(*st, token)
    recent = token
    for st in stages[2:]:
        placed[st] = place(*st, token)
        recent = placed[st][0]
    act = x[0]
    layers, saved = [], []
    for i, (l, g) in enumerate(stages):
        if g == 0:
            w = dict(w_pool=w_pool[l], pool_scale=pool_scale[l][None], sgu_ln_g=sgu_ln_g[l][None],
                     w_spatial=w_spatial[l], b_spatial=b_spatial[l][:, :, None], ln1_g=ln1_g[l][None], ln1_b=ln1_b[l][None],
                     ln2_g=ln2_g[l][None], ln2_b=ln2_b[l][None])
        _, lands, token = _gather_wait(flights[l, g], group_axes(g), recent)
        if i + 2 < len(stages):
            flights[stages[i + 2]], token = send(*stages[i + 2], token)
        mats = _gather_finish(lands, group_axes(g), token)
        w.update(zip(GROUPS[g], mats))
        if g == 0:
            blocks = mats[2].reshape(N_CHIPS, CONV_PAD_ROWS, LANES)[:, :3, :conv_cols]
            w["w_conv"] = blocks.transpose(1, 0, 2).reshape(3, N_CHIPS * conv_cols)
            sv = _fwd_mix(act, w, token)
            recent = sv["xhat1"]
        else:
            act = recent = _fwd_mlp(sv, w, token)
            layers.append(w)
            saved.append(sv)

    big_outs = {n: [lax.empty(big_w[n].shape, f32) for _ in range(4)] for n in BIG}
    small_sums = [None] * depth
    pending, updates = [], []
    latest = dict(token=None)

    def begin(l, g, parts):
        axes = GROUP_AXES[g] + (0,) * (len(parts) - len(GROUPS[g]))
        lands = [_half_blocks(p, ax) for p, ax in zip(parts, axes)]
        flight, latest["token"] = _split_copy_start("pair", _pair_plan(axes), N_CHIPS * len(parts), parts, lands, latest["token"],
                                                    sibling_only=True)
        pending.append(dict(l=l, g=g, axes=axes, step="pair", age=0, flight=flight))

    def advance(st, recent):
        if st["step"] == "pair":
            parts, got, _ = _split_copy_wait("pair", _pair_plan(st["axes"]), st["flight"], recent)
            sums = _add_pair_layer(parts, got, st["axes"])
            st["flight"], latest["token"] = _scatter_start(sums, latest["token"])
            st["step"] = "scatter"
        elif st["step"] == "scatter":
            sums, slots, _ = _scatter_wait(st["flight"], recent)
            filled = _add_slots(sums, slots)
            st["flight"], latest["token"] = _split_copy_start("join", _join_plan, len(filled), [], filled, latest["token"],
                                                              sibling_only=True)
            st["step"] = "join"
        else:
            _, summed, _ = _split_copy_wait("join", _join_plan, st["flight"], recent)
            updates.append((st["l"], st["g"], summed[:len(GROUPS[st["g"]])]))
            if st["g"] == 0:
                small_sums[st["l"]] = summed[-1]
            st["step"] = "done"
        st["age"] = 0

    def hook(recent):
        for st in reversed(list(pending)):
            st["age"] += 1
            if st["age"] >= SCATTER_HOOKS or st["step"] != "scatter":
                advance(st, recent)
                if st["step"] == "done":
                    pending.remove(st)
        return latest["token"]

    def update(count, recent):
        for l, g, totals in updates[:count]:
            names = GROUPS[g]
            new = _adamw_layer(l, [big_w[n] for n in names], [big_m[n] for n in names], [big_v[n] for n in names], totals,
                               [big_outs[n] for n in names], ADAMW_STEPS[g], latest["token"])
            big_outs.update(zip(names, new))
            recent = new[-1][1]
        del updates[:count]
        return recent

    grad_x, sq = _loss_head(act, loss_target[0])
    latest["token"] = sq
    grads = [None] * depth
    for l in reversed(range(depth)):
        dz, g_mlp = _bwd_mlp(grad_x, layers[l], saved[l], latest["token"], hook)
        hook(g_mlp["w_down"])
        begin(l, 1, [g_mlp[n] for n in GROUPS[1]])
        grad_x, g_mix = _bwd_mix(dz, layers[l], saved[l], latest["token"], hook)
        grads[l] = dict(g_mlp, **g_mix)
        hook(g_mix["w_o"])
        begin(l, 0, [g_mix[n] for n in GROUPS[0]] + [_pack_layer([grads[l][n] for n in SMALL])])
    recent = g_mix["w_o"]
    while pending:
        recent = update(-(-3 * len(updates) // 4), recent)
        hook(recent)
    update(len(updates), recent)
    loss = lax.psum(0.5 / D_MODEL * jnp.sum(sq), ("x", "y", "c"))

    small_sum = _gather_shards([jnp.stack(small_sums)], recent)[0].reshape(depth, SMALL_LAYER_ROWS * LANES)
    grad = {n: [jnp.swapaxes(o, 1, 2) for o in big_outs[n]] if n == "w_in" else big_outs[n] for n in BIG}
    delta = {n: o[1] for n, o in grad.items()}
    new_m = {n: o[2] for n, o in grad.items()}
    new_v = {n: o[3] for n, o in grad.items()}
    grad = {n: o[0] for n, o in grad.items()}
    grad.update(_unpack_layers(small_sum, {n: (3, N_CHIPS * conv_cols) if n == "w_conv" else weights[n].shape[1:] for n in SMALL}))
    grad["w_conv"] = lax.dynamic_slice_in_dim(grad["w_conv"], chip * conv_cols, conv_cols, axis=2)

    results = _adamw_small(*[[src[n] for n in SMALL] for src in (weights, grad, m_in, v_in)])
    for n, (step, moment1, moment2) in zip(SMALL, results):
        delta[n], new_m[n], new_v[n] = step, moment1, moment2

    return (loss, grad_x[None], *[grad[n] for n in WEIGHTS], *[delta[n] for n in WEIGHTS],
            *[new_m[n] for n in WEIGHTS], *[new_v[n] for n in WEIGHTS])
```
